```python
import math
import jax, jax.numpy as jnp
from jax import lax
import numpy as np

D_MODEL = 1024
BATCH = 8
SEQ = 4096
DEPTH = 1

RW_HEADS = 8
RW_HEAD_DIM = 64
RW_DIM = RW_HEADS * RW_HEAD_DIM
RW_DECAY_LORA = 64
RW_A_LORA = 64
RW_GATE_LORA = 128
RW_SIZES = (RW_DIM, RW_DIM, RW_DIM, RW_DECAY_LORA, RW_A_LORA, RW_GATE_LORA)
RW_COLS = 3 * RW_DIM + RW_DECAY_LORA + RW_A_LORA + RW_GATE_LORA
RW_GN_EPS = 64e-5

DSA_HEADS = 8
DSA_LATENT = 128
DSA_Q_DIM = DSA_HEADS * DSA_LATENT
IDX_HEADS = 4
IDX_DIM = 64
TOPK_MAX = 256
Q_BLOCK = 128

REL_BUCKETS = 32
REL_MAX_DIST = 128

IN_SIZES = (RW_COLS, DSA_Q_DIM, DSA_LATENT, IDX_HEADS * IDX_DIM, IDX_DIM, IDX_HEADS, D_MODEL, D_MODEL)
IN_COLS = RW_COLS + DSA_Q_DIM + DSA_LATENT + IDX_HEADS * IDX_DIM + IDX_DIM + IDX_HEADS + 2 * D_MODEL

PEER_HEADS = 8
PEER_N_KEYS = 128
PEER_N_EXPERTS = PEER_N_KEYS * PEER_N_KEYS
PEER_KEY_DIM = 128
PEER_HALF = PEER_KEY_DIM // 2
PEER_TOPK = 16
PEER_CHUNK = 128

LN_EPS = 1e-5
DEEPNORM_ALPHA = (2.0 * DEPTH) ** 0.25
DEEPNORM_BETA = (8.0 * DEPTH) ** -0.25

kernel_name = "rwkv7_dsa_peer_hybrid_block"


def _split_points(sizes):
    return np.cumsum(sizes)[:-1].tolist()


def _layer_norm(x, g, b):
    xf = x.astype(jnp.float32)
    mu = jnp.mean(xf, -1, keepdims=True)
    var = jnp.mean(jnp.square(xf - mu), -1, keepdims=True)
    return ((xf - mu) * lax.rsqrt(var + LN_EPS)).astype(x.dtype) * g + b


def _rms_norm(x, g):
    xf = x.astype(jnp.float32)
    ms = jnp.mean(jnp.square(xf), -1, keepdims=True)
    return (xf * lax.rsqrt(ms + LN_EPS)).astype(x.dtype) * g


def _heads(t, n_heads):
    return t.reshape(*t.shape[:-1], n_heads, t.shape[-1] // n_heads)


def _token_shift(z):
    return jnp.pad(z, ((0, 0), (1, 0), (0, 0)))[:, :-1]


def _t5_bucket(n):
    n = jnp.maximum(n, 0)
    max_exact = REL_BUCKETS // 2
    nf = jnp.maximum(n, 1).astype(jnp.float32)
    large = max_exact + (jnp.log(nf / max_exact) / math.log(REL_MAX_DIST / max_exact)
                         * (REL_BUCKETS - max_exact)).astype(jnp.int32)
    large = jnp.minimum(large, REL_BUCKETS - 1)
    return jnp.where(n < max_exact, n, large)


def _rwkv7_time_mix(z_rw, mu, w0, w2, a0, a2, g2, k_k, k_a, r_k, gn_g, gn_b):
    B, S, _ = z_rw.shape
    zs = z_rw + (_token_shift(z_rw) - z_rw) * mu
    r, k, v, wl, al, gl = jnp.split(zs, _split_points(RW_SIZES), axis=-1)
    log_w = -jax.nn.softplus(-(w0 + jnp.tanh(wl) @ w2)) - 0.5
    decay = jnp.exp(-jnp.exp(log_w))
    a = jax.nn.sigmoid(a0 + al @ a2)
    g = jax.nn.sigmoid(gl) @ g2
    kk = _heads(k * k_k, RW_HEADS)
    kk = kk * lax.rsqrt(jnp.maximum(jnp.sum(kk * kk, -1, keepdims=True), 1e-24))
    k = k * (1 + (a - 1) * k_a)
    rh, kh, vh, dh, ah = [_heads(t, RW_HEADS) for t in (r, k, v, decay, a)]
    a_vec = -kk
    b_vec = kk * ah

    def step(state, inp):
        r_t, w_t, k_t, v_t, a_t, b_t = inp
        sa = jnp.einsum('bhvk,bhk->bhv', state, a_t)
        state = (state * w_t[:, :, None, :] + sa[..., None] * b_t[:, :, None, :]
                 + v_t[..., None] * k_t[:, :, None, :])
        return state, jnp.einsum('bhvk,bhk->bhv', state, r_t)

    seq = tuple(jnp.moveaxis(t.astype(jnp.float32), 1, 0) for t in (rh, dh, kh, vh, a_vec, b_vec))
    state0 = jnp.zeros((B, RW_HEADS, RW_HEAD_DIM, RW_HEAD_DIM), jnp.float32)
    _, y = lax.scan(step, state0, seq)
    y = jnp.moveaxis(y, 0, 1)
    mean = jnp.mean(y, -1, keepdims=True)
    var = jnp.mean(jnp.square(y - mean), -1, keepdims=True)
    y = ((y - mean) * lax.rsqrt(var + RW_GN_EPS)).reshape(B, S, RW_DIM).astype(z_rw.dtype) * gn_g + gn_b
    bonus = (jnp.sum(rh * kh * r_k, -1, keepdims=True) * vh).reshape(B, S, RW_DIM)
    return (y + bonus) * g


def _dsa_attention(z_q, z_kv, z_qi, z_ki, z_wi, kv_g, idx_k_g, idx_k_b, rel_bias):
    B, S, _ = z_q.shape
    c_kv = _rms_norm(z_kv, kv_g)
    k_idx = _layer_norm(z_ki, idx_k_g, idx_k_b)
    q = _heads(z_q, DSA_HEADS)
    q_idx = _heads(z_qi, IDX_HEADS)
    w_idx = z_wi * (IDX_HEADS ** -0.5)
    topk = min(TOPK_MAX, S // 4)
    nb = S // Q_BLOCK

    def blockify(t):
        return jnp.swapaxes(t.reshape(B, nb, Q_BLOCK, *t.shape[2:]), 0, 1)

    q_pos = jnp.arange(S, dtype=jnp.int32).reshape(nb, Q_BLOCK)
    key_pos = jnp.arange(S, dtype=jnp.int32)

    def one_block(args):
        qb, qib, wb, tpos = args
        dots = jnp.einsum('bqhd,bsd->bqhs', qib, k_idx) * (IDX_DIM ** -0.5)
        score = jnp.einsum('bqh,bqhs->bqs', wb, jax.nn.relu(dots))
        causal = key_pos[None, :] <= tpos[:, None]
        score = jnp.where(causal[None], score, -jnp.inf)
        _, sel = lax.top_k(score, topk)
        valid = sel <= tpos[None, :, None]
        kv_sel = jax.vmap(lambda cb, ib: cb[ib])(c_kv, sel)
        logits = jnp.einsum('bqhd,bqkd->bqhk', qb, kv_sel) * (DSA_LATENT ** -0.5)
        bias = rel_bias[_t5_bucket(tpos[None, :, None] - sel)]
        logits = logits + jnp.transpose(bias, (0, 1, 3, 2))
        logits = jnp.where(valid[:, :, None, :], logits, -jnp.inf)
        p = jax.nn.softmax(logits.astype(jnp.float32), axis=-1).astype(kv_sel.dtype)
        return jnp.einsum('bqhk,bqkd->bqhd', p, kv_sel)

    out = lax.map(one_block, (blockify(q), blockify(q_idx), blockify(w_idx), q_pos))
    return jnp.swapaxes(out, 0, 1).reshape(B, S, DSA_Q_DIM)


def _peer(h, w_pq, sub_keys, u_tab, v_tab):
    B, S, D = h.shape
    T = B * S
    ht = h.reshape(T, D)
    q = (ht @ w_pq).reshape(T, PEER_HEADS, 2, PEER_HALF)
    s = jnp.einsum('thcd,hcnd->thcn', q, sub_keys)
    s1, i1 = lax.top_k(s[:, :, 0], PEER_TOPK)
    s2, i2 = lax.top_k(s[:, :, 1], PEER_TOPK)
    cand = (s1[..., :, None] + s2[..., None, :]).reshape(T, PEER_HEADS, PEER_TOPK * PEER_TOPK)
    cand_idx = (i1[..., :, None] * PEER_N_KEYS + i2[..., None, :]).reshape(T, PEER_HEADS, PEER_TOPK * PEER_TOPK)
    top_s, pos = lax.top_k(cand, PEER_TOPK)
    experts = jnp.take_along_axis(cand_idx, pos, axis=-1)
    gates = jax.nn.softmax(top_s.astype(jnp.float32), axis=-1).astype(h.dtype)
    n_chunks = T // PEER_CHUNK

    def chunk(args):
        hc, ec, gc = args
        act = jax.nn.gelu(jnp.einsum('cd,chkd->chk', hc, u_tab[ec]), approximate=False) * gc
        return jnp.einsum('chk,chkd->cd', act, v_tab[ec])

    out = lax.map(chunk, (ht.reshape(n_chunks, PEER_CHUNK, D),
                          experts.reshape(n_chunks, PEER_CHUNK, PEER_HEADS, PEER_TOPK),
                          gates.reshape(n_chunks, PEER_CHUNK, PEER_HEADS, PEER_TOPK)))
    return out.reshape(B, S, D)


def setup_inputs(seed: int = 0) -> dict:
    key = jax.random.key(seed)
    ks = iter(jax.random.split(key, 40))
    L, D = DEPTH, D_MODEL

    def nrm(shape, std):
        return jax.random.normal(next(ks), shape, jnp.float32) * std

    def unif(shape, lo, hi):
        return jax.random.uniform(next(ks), shape, jnp.float32, lo, hi)

    col_scale = jnp.concatenate([jnp.ones((2 * RW_DIM,), jnp.float32),
                                 jnp.full((RW_DIM,), DEEPNORM_BETA, jnp.float32),
                                 jnp.ones((IN_COLS - 3 * RW_DIM,), jnp.float32)])
    return {
        "x": nrm((BATCH, SEQ, D), 1.0),
        "c": nrm((BATCH, D), 1.0),
        "w_ada": nrm((L, D, 6 * D), D ** -0.5),
        "b_ada": nrm((L, 6 * D), 0.02),
        "w_in": nrm((L, D, IN_COLS), D ** -0.5) * col_scale,
        "rw_mu": unif((L, RW_COLS), 0.0, 1.0),
        "rw_w0": unif((L, RW_DIM), -4.0, 0.0),
        "rw_w2": nrm((L, RW_DECAY_LORA, RW_DIM), 0.5 * RW_DECAY_LORA ** -0.5),
        "rw_a0": nrm((L, RW_DIM), 0.1),
        "rw_a2": nrm((L, RW_A_LORA, RW_DIM), 0.5 * RW_A_LORA ** -0.5),
        "rw_g2": nrm((L, RW_GATE_LORA, RW_DIM), RW_GATE_LORA ** -0.5),
        "rw_k_k": 0.85 + nrm((L, RW_DIM), 0.05),
        "rw_k_a": 1.0 + nrm((L, RW_DIM), 0.05),
        "rw_r_k": nrm((L, RW_HEADS, RW_HEAD_DIM), 0.1),
        "rw_gn_g": 1.0 + nrm((L, RW_DIM), 0.02),
        "rw_gn_b": nrm((L, RW_DIM), 0.02),
        "dsa_kv_g": 1.0 + nrm((L, DSA_LATENT), 0.02),
        "idx_k_g": 1.0 + nrm((L, IDX_DIM), 0.02),
        "idx_k_b": nrm((L, IDX_DIM), 0.02),
        "rel_bias": nrm((REL_BUCKETS, DSA_HEADS), 0.5),
        "w_br_a": nrm((L, RW_DIM, D), DEEPNORM_BETA * RW_DIM ** -0.5),
        "w_br_b": nrm((L, DSA_Q_DIM, D), DEEPNORM_BETA * DSA_Q_DIM ** -0.5),
        "w_out": nrm((L, D, D), DEEPNORM_BETA * D ** -0.5),
        "ln1_g": 1.0 + nrm((L, D), 0.02),
        "ln1_b": nrm((L, D), 0.02),
        "peer_wq": nrm((L, D, PEER_HEADS * PEER_KEY_DIM), D ** -0.5),
        "peer_keys": nrm((L, PEER_HEADS, 2, PEER_N_KEYS, PEER_HALF), PEER_HALF ** -0.5),
        "peer_u": nrm((L, PEER_N_EXPERTS, D), D ** -0.5),
        "peer_v": nrm((L, PEER_N_EXPERTS, D), DEEPNORM_BETA * PEER_HEADS ** -0.5),
        "ln2_g": 1.0 + nrm((L, D), 0.02),
        "ln2_b": nrm((L, D), 0.02),
    }


def reference(x, c, w_ada, b_ada, w_in, rw_mu, rw_w0, rw_w2, rw_a0, rw_a2, rw_g2, rw_k_k, rw_k_a,
              rw_r_k, rw_gn_g, rw_gn_b, dsa_kv_g, idx_k_g, idx_k_b, rel_bias, w_br_a, w_br_b, w_out,
              ln1_g, ln1_b, peer_wq, peer_keys, peer_u, peer_v, ln2_g, ln2_b):
    for l in range(DEPTH):
        mod = jax.nn.silu(c) @ w_ada[l] + b_ada[l]
        sh1, sc1, gt1, sh2, sc2, gt2 = jnp.split(mod, 6, axis=-1)

        h = x * (1 + sc1[:, None]) + sh1[:, None]
        z = h @ w_in[l]
        z_rw, z_q, z_kv, z_qi, z_ki, z_wi, z_ga, z_gb = jnp.split(z, _split_points(IN_SIZES), axis=-1)
        y_a = _rwkv7_time_mix(z_rw, rw_mu[l], rw_w0[l], rw_w2[l], rw_a0[l], rw_a2[l], rw_g2[l],
                              rw_k_k[l], rw_k_a[l], rw_r_k[l], rw_gn_g[l], rw_gn_b[l]) @ w_br_a[l]
        y_b = _dsa_attention(z_q, z_kv, z_qi, z_ki, z_wi, dsa_kv_g[l], idx_k_g[l], idx_k_b[l],
                             rel_bias) @ w_br_b[l]
        merged = jax.nn.sigmoid(z_ga) * y_a + jax.nn.sigmoid(z_gb) * y_b
        mix = merged @ w_out[l]
        x = _layer_norm(DEEPNORM_ALPHA * x + gt1[:, None] * mix, ln1_g[l], ln1_b[l])

        h2 = x * (1 + sc2[:, None]) + sh2[:, None]
        y2 = _peer(h2, peer_wq[l], peer_keys[l], peer_u[l], peer_v[l])
        x = _layer_norm(DEEPNORM_ALPHA * x + gt2[:, None] * y2, ln2_g[l], ln2_b[l])
    return x
```

```python
import math
from functools import partial

import jax
import jax.numpy as jnp
import numpy as np
from jax import lax
from jax.experimental import pallas as pl
from jax.experimental.pallas import tpu as pltpu

D_MODEL = 1024
RW_HEADS = 8
RW_HEAD_DIM = 64
RW_DIM = 512
RW_DECAY_LORA = 64
RW_A_LORA = 64
RW_GATE_LORA = 128
RW_SIZES = (RW_DIM, RW_DIM, RW_DIM, RW_DECAY_LORA, RW_A_LORA, RW_GATE_LORA)
RW_COLS = 3 * RW_DIM + RW_DECAY_LORA + RW_A_LORA + RW_GATE_LORA
RW_GN_EPS = 64e-5
DSA_HEADS = 8
DSA_LATENT = 128
DSA_Q_DIM = DSA_HEADS * DSA_LATENT
IDX_HEADS = 4
IDX_DIM = 64
TOPK_MAX = 256
Q_BLOCK = 128
REL_BUCKETS = 32
REL_MAX_DIST = 128
IN_SIZES = (RW_COLS, DSA_Q_DIM, DSA_LATENT, IDX_HEADS * IDX_DIM, IDX_DIM, IDX_HEADS, D_MODEL, D_MODEL)
IN_COLS = sum(IN_SIZES)
PEER_HEADS = 8
PEER_N_KEYS = 128
PEER_KEY_DIM = 128
PEER_HALF = 64
PEER_TOPK = 16
PEER_CHUNK = 128
LN_EPS = 1e-5
DEPTH = 1
DEEPNORM_ALPHA = (2.0 * DEPTH) ** 0.25

LANES = 128


def _split_points(sizes):
    return np.cumsum(sizes)[:-1].tolist()


def _mod_matmul_kernel(x_ref, sc_ref, sh_ref, w_ref, o_ref):
    h = x_ref[0] * (1.0 + sc_ref[0]) + sh_ref[0]
    o_ref[0] = jnp.dot(h.astype(jnp.bfloat16), w_ref[...], preferred_element_type=jnp.float32)


def _mod_matmul(x, sc, sh, w_bf16, tm=512, tn=768):
    B, S, D = x.shape
    N = w_bf16.shape[1]
    return pl.pallas_call(
        _mod_matmul_kernel,
        grid=(B, S // tm, N // tn),
        in_specs=[
            pl.BlockSpec((1, tm, D), lambda b, i, j: (b, i, 0)),
            pl.BlockSpec((1, 1, D), lambda b, i, j: (b, 0, 0)),
            pl.BlockSpec((1, 1, D), lambda b, i, j: (b, 0, 0)),
            pl.BlockSpec((D, tn), lambda b, i, j: (0, j)),
        ],
        out_specs=pl.BlockSpec((1, tm, tn), lambda b, i, j: (b, i, j)),
        out_shape=jax.ShapeDtypeStruct((B, S, N), jnp.float32),
        name="mod_matmul",
    )(x, sc[:, None, :], sh[:, None, :], w_bf16)


def _layer_norm(x, g, b):
    mu = jnp.mean(x, -1, keepdims=True)
    var = jnp.mean(jnp.square(x - mu), -1, keepdims=True)
    return (x - mu) * lax.rsqrt(var + LN_EPS) * g + b


def _rms_norm(x, g):
    ms = jnp.mean(jnp.square(x), -1, keepdims=True)
    return x * lax.rsqrt(ms + LN_EPS) * g


def _heads(t, n_heads):
    return t.reshape(*t.shape[:-1], n_heads, t.shape[-1] // n_heads)


def _token_shift(z):
    return jnp.pad(z, ((0, 0), (1, 0), (0, 0)))[:, :-1]


def _t5_bucket(n):
    n = jnp.maximum(n, 0)
    max_exact = REL_BUCKETS // 2
    nf = jnp.maximum(n, 1).astype(jnp.float32)
    large = max_exact + (jnp.log(nf / max_exact) / math.log(REL_MAX_DIST / max_exact)
                         * (REL_BUCKETS - max_exact)).astype(jnp.int32)
    large = jnp.minimum(large, REL_BUCKETS - 1)
    return jnp.where(n < max_exact, n, large)


def _rwkv7_time_mix(z_rw, mu, w0, w2, a0, a2, g2, k_k, k_a, r_k, gn_g, gn_b):
    B, S, _ = z_rw.shape
    zs = z_rw + (_token_shift(z_rw) - z_rw) * mu
    r, k, v, wl, al, gl = jnp.split(zs, _split_points(RW_SIZES), axis=-1)
    log_w = -jax.nn.softplus(-(w0 + jnp.tanh(wl) @ w2)) - 0.5
    decay = jnp.exp(-jnp.exp(log_w))
    a = jax.nn.sigmoid(a0 + al @ a2)
    g = jax.nn.sigmoid(gl) @ g2
    kk = _heads(k * k_k, RW_HEADS)
    kk = kk * lax.rsqrt(jnp.maximum(jnp.sum(kk * kk, -1, keepdims=True), 1e-24))
    k = k * (1 + (a - 1) * k_a)
    rh, kh, vh, dh, ah = [_heads(t, RW_HEADS) for t in (r, k, v, decay, a)]
    a_vec = -kk
    b_vec = kk * ah

    def step(state, inp):
        r_t, w_t, k_t, v_t, a_t, b_t = inp
        sa = jnp.einsum('bhvk,bhk->bhv', state, a_t)
        state = (state * w_t[:, :, None, :] + sa[..., None] * b_t[:, :, None, :]
                 + v_t[..., None] * k_t[:, :, None, :])
        return state, jnp.einsum('bhvk,bhk->bhv', state, r_t)

    seq = tuple(jnp.moveaxis(t, 1, 0) for t in (rh, dh, kh, vh, a_vec, b_vec))
    state0 = jnp.zeros((B, RW_HEADS, RW_HEAD_DIM, RW_HEAD_DIM), jnp.float32)
    _, y = lax.scan(step, state0, seq)
    y = jnp.moveaxis(y, 0, 1)
    mean = jnp.mean(y, -1, keepdims=True)
    var = jnp.mean(jnp.square(y - mean), -1, keepdims=True)
    y = ((y - mean) * lax.rsqrt(var + RW_GN_EPS)).reshape(B, S, RW_DIM) * gn_g + gn_b
    bonus = (jnp.sum(rh * kh * r_k, -1, keepdims=True) * vh).reshape(B, S, RW_DIM)
    return (y + bonus) * g


def _dsa_attention(z_q, z_kv, z_qi, z_ki, z_wi, kv_g, idx_k_g, idx_k_b, rel_bias):
    B, S, _ = z_q.shape
    c_kv = _rms_norm(z_kv, kv_g)
    k_idx = _layer_norm(z_ki, idx_k_g, idx_k_b)
    q = _heads(z_q, DSA_HEADS)
    q_idx = _heads(z_qi, IDX_HEADS)
    w_idx = z_wi * (IDX_HEADS ** -0.5)
    topk = min(TOPK_MAX, S // 4)
    nb = S // Q_BLOCK

    def blockify(t):
        return jnp.swapaxes(t.reshape(B, nb, Q_BLOCK, *t.shape[2:]), 0, 1)

    q_pos = jnp.arange(S, dtype=jnp.int32).reshape(nb, Q_BLOCK)
    key_pos = jnp.arange(S, dtype=jnp.int32)

    def one_block(args):
        qb, qib, wb, tpos = args
        dots = jnp.einsum('bqhd,bsd->bqhs', qib, k_idx) * (IDX_DIM ** -0.5)
        score = jnp.einsum('bqh,bqhs->bqs', wb, jax.nn.relu(dots))
        causal = key_pos[None, :] <= tpos[:, None]
        score = jnp.where(causal[None], score, -jnp.inf)
        _, sel = lax.top_k(score, topk)
        valid = sel <= tpos[None, :, None]
        kv_sel = jax.vmap(lambda cb, ib: cb[ib])(c_kv, sel)
        logits = jnp.einsum('bqhd,bqkd->bqhk', qb, kv_sel) * (DSA_LATENT ** -0.5)
        bias = rel_bias[_t5_bucket(tpos[None, :, None] - sel)]
        logits = logits + jnp.transpose(bias, (0, 1, 3, 2))
        logits = jnp.where(valid[:, :, None, :], logits, -jnp.inf)
        p = jax.nn.softmax(logits, axis=-1)
        return jnp.einsum('bqhk,bqkd->bqhd', p, kv_sel)

    out = lax.map(one_block, (blockify(q), blockify(q_idx), blockify(w_idx), q_pos))
    return jnp.swapaxes(out, 0, 1).reshape(B, S, DSA_Q_DIM)


def _peer(h, w_pq, sub_keys, u_tab, v_tab):
    B, S, D = h.shape
    T = B * S
    ht = h.reshape(T, D)
    q = (ht @ w_pq).reshape(T, PEER_HEADS, 2, PEER_HALF)
    s = jnp.einsum('thcd,hcnd->thcn', q, sub_keys)
    s1, i1 = lax.top_k(s[:, :, 0], PEER_TOPK)
    s2, i2 = lax.top_k(s[:, :, 1], PEER_TOPK)
    cand = (s1[..., :, None] + s2[..., None, :]).reshape(T, PEER_HEADS, PEER_TOPK * PEER_TOPK)
    cand_idx = (i1[..., :, None] * PEER_N_KEYS + i2[..., None, :]).reshape(T, PEER_HEADS, PEER_TOPK * PEER_TOPK)
    top_s, pos = lax.top_k(cand, PEER_TOPK)
    experts = jnp.take_along_axis(cand_idx, pos, axis=-1)
    gates = jax.nn.softmax(top_s, axis=-1)
    n_chunks = T // PEER_CHUNK

    def chunk(args):
        hc, ec, gc = args
        act = jax.nn.gelu(jnp.einsum('cd,chkd->chk', hc, u_tab[ec]), approximate=False) * gc
        return jnp.einsum('chk,chkd->cd', act, v_tab[ec])

    out = lax.map(chunk, (ht.reshape(n_chunks, PEER_CHUNK, D),
                          experts.reshape(n_chunks, PEER_CHUNK, PEER_HEADS, PEER_TOPK),
                          gates.reshape(n_chunks, PEER_CHUNK, PEER_HEADS, PEER_TOPK)))
    return out.reshape(B, S, D)


def kernel(x, c, w_ada, b_ada, w_in, rw_mu, rw_w0, rw_w2, rw_a0, rw_a2, rw_g2, rw_k_k, rw_k_a, rw_r_k, rw_gn_g, rw_gn_b, dsa_kv_g, idx_k_g, idx_k_b, rel_bias, w_br_a, w_br_b, w_out, ln1_g, ln1_b, peer_wq, peer_keys, peer_u, peer_v, ln2_g, ln2_b):
    l = 0
    mod = jax.nn.silu(c) @ w_ada[l] + b_ada[l]
    sh1, sc1, gt1, sh2, sc2, gt2 = jnp.split(mod, 6, axis=-1)

    n_pad = (-IN_COLS) % 768
    w_in_p = jnp.pad(w_in[l], ((0, 0), (0, n_pad))).astype(jnp.bfloat16)
    z = _mod_matmul(x, sc1, sh1, w_in_p)[..., :IN_COLS]
    z_rw, z_q, z_kv, z_qi, z_ki, z_wi, z_ga, z_gb = jnp.split(z, _split_points(IN_SIZES), axis=-1)
    y_a = _rwkv7_time_mix(z_rw, rw_mu[l], rw_w0[l], rw_w2[l], rw_a0[l], rw_a2[l], rw_g2[l],
                          rw_k_k[l], rw_k_a[l], rw_r_k[l], rw_gn_g[l], rw_gn_b[l]) @ w_br_a[l]
    y_b = _dsa_attention(z_q, z_kv, z_qi, z_ki, z_wi, dsa_kv_g[l], idx_k_g[l], idx_k_b[l],
                         rel_bias) @ w_br_b[l]
    merged = jax.nn.sigmoid(z_ga) * y_a + jax.nn.sigmoid(z_gb) * y_b
    mix = merged @ w_out[l]
    x = _layer_norm(DEEPNORM_ALPHA * x + gt1[:, None] * mix, ln1_g[l], ln1_b[l])

    h2 = x * (1 + sc2[:, None]) + sh2[:, None]
    y2 = _peer(h2, peer_wq[l], peer_keys[l], peer_u[l], peer_v[l])
    x = _layer_norm(DEEPNORM_ALPHA * x + gt2[:, None] * y2, ln2_g[l], ln2_b[l])
    return x
```

```python
import math
from functools import partial

import jax
import jax.numpy as jnp
import numpy as np
from jax import lax
from jax.experimental import pallas as pl
from jax.experimental.pallas import tpu as pltpu

D_MODEL = 1024
RW_HEADS = 8
RW_HEAD_DIM = 64
RW_DIM = 512
RW_DECAY_LORA = 64
RW_A_LORA = 64
RW_GATE_LORA = 128
RW_SIZES = (RW_DIM, RW_DIM, RW_DIM, RW_DECAY_LORA, RW_A_LORA, RW_GATE_LORA)
RW_COLS = 3 * RW_DIM + RW_DECAY_LORA + RW_A_LORA + RW_GATE_LORA
RW_GN_EPS = 64e-5
DSA_HEADS = 8
DSA_LATENT = 128
DSA_Q_DIM = DSA_HEADS * DSA_LATENT
IDX_HEADS = 4
IDX_DIM = 64
TOPK_MAX = 256
Q_BLOCK = 128
REL_BUCKETS = 32
REL_MAX_DIST = 128
IN_SIZES = (RW_COLS, DSA_Q_DIM, DSA_LATENT, IDX_HEADS * IDX_DIM, IDX_DIM, IDX_HEADS, D_MODEL, D_MODEL)
IN_COLS = sum(IN_SIZES)
PEER_HEADS = 8
PEER_N_KEYS = 128
PEER_KEY_DIM = 128
PEER_HALF = 64
PEER_TOPK = 16
PEER_CHUNK = 128
LN_EPS = 1e-5
DEPTH = 1
DEEPNORM_ALPHA = (2.0 * DEPTH) ** 0.25

LANES = 128
MM_DTYPE = jnp.bfloat16


def _split_points(sizes):
    return np.cumsum(sizes)[:-1].tolist()


def _mod_matmul_kernel(x_ref, sc_ref, sh_ref, w_ref, o_ref):
    h = x_ref[0] * (1.0 + sc_ref[0]) + sh_ref[0]
    o_ref[0] = jnp.dot(h.astype(w_ref.dtype), w_ref[...],
                       preferred_element_type=jnp.float32).astype(o_ref.dtype)


def _mod_matmul(x, sc, sh, w, out_dtype, tn, tm=512):
    B, S, D = x.shape
    N = w.shape[1]
    return pl.pallas_call(
        _mod_matmul_kernel,
        grid=(B, S // tm, N // tn),
        in_specs=[
            pl.BlockSpec((1, tm, D), lambda b, i, j: (b, i, 0)),
            pl.BlockSpec((1, 1, D), lambda b, i, j: (b, 0, 0)),
            pl.BlockSpec((1, 1, D), lambda b, i, j: (b, 0, 0)),
            pl.BlockSpec((D, tn), lambda b, i, j: (0, j)),
        ],
        out_specs=pl.BlockSpec((1, tm, tn), lambda b, i, j: (b, i, j)),
        out_shape=jax.ShapeDtypeStruct((B, S, N), out_dtype),
        name="mod_matmul",
    )(x, sc[:, None, :], sh[:, None, :], w.astype(MM_DTYPE))


def _layer_norm(x, g, b):
    mu = jnp.mean(x, -1, keepdims=True)
    var = jnp.mean(jnp.square(x - mu), -1, keepdims=True)
    return (x - mu) * lax.rsqrt(var + LN_EPS) * g + b


def _rms_norm(x, g):
    ms = jnp.mean(jnp.square(x), -1, keepdims=True)
    return x * lax.rsqrt(ms + LN_EPS) * g


def _heads(t, n_heads):
    return t.reshape(*t.shape[:-1], n_heads, t.shape[-1] // n_heads)


def _token_shift(z):
    return jnp.pad(z, ((0, 0), (1, 0), (0, 0)))[:, :-1]


def _t5_bucket(n):
    n = jnp.maximum(n, 0)
    max_exact = REL_BUCKETS // 2
    nf = jnp.maximum(n, 1).astype(jnp.float32)
    large = max_exact + (jnp.log(nf / max_exact) / math.log(REL_MAX_DIST / max_exact)
                         * (REL_BUCKETS - max_exact)).astype(jnp.int32)
    large = jnp.minimum(large, REL_BUCKETS - 1)
    return jnp.where(n < max_exact, n, large)


def _rwkv7_time_mix(z_rw, mu, w0, w2, a0, a2, g2, k_k, k_a, r_k, gn_g, gn_b):
    B, S, _ = z_rw.shape
    zs = z_rw + (_token_shift(z_rw) - z_rw) * mu
    r, k, v, wl, al, gl = jnp.split(zs, _split_points(RW_SIZES), axis=-1)
    log_w = -jax.nn.softplus(-(w0 + jnp.tanh(wl) @ w2)) - 0.5
    decay = jnp.exp(-jnp.exp(log_w))
    a = jax.nn.sigmoid(a0 + al @ a2)
    g = jax.nn.sigmoid(gl) @ g2
    kk = _heads(k * k_k, RW_HEADS)
    kk = kk * lax.rsqrt(jnp.maximum(jnp.sum(kk * kk, -1, keepdims=True), 1e-24))
    k = k * (1 + (a - 1) * k_a)
    rh, kh, vh, dh, ah = [_heads(t, RW_HEADS) for t in (r, k, v, decay, a)]
    a_vec = -kk
    b_vec = kk * ah

    def step(state, inp):
        r_t, w_t, k_t, v_t, a_t, b_t = inp
        sa = jnp.einsum('bhvk,bhk->bhv', state, a_t)
        state = (state * w_t[:, :, None, :] + sa[..., None] * b_t[:, :, None, :]
                 + v_t[..., None] * k_t[:, :, None, :])
        return state, jnp.einsum('bhvk,bhk->bhv', state, r_t)

    seq = tuple(jnp.moveaxis(t, 1, 0) for t in (rh, dh, kh, vh, a_vec, b_vec))
    state0 = jnp.zeros((B, RW_HEADS, RW_HEAD_DIM, RW_HEAD_DIM), jnp.float32)
    _, y = lax.scan(step, state0, seq)
    y = jnp.moveaxis(y, 0, 1)
    mean = jnp.mean(y, -1, keepdims=True)
    var = jnp.mean(jnp.square(y - mean), -1, keepdims=True)
    y = ((y - mean) * lax.rsqrt(var + RW_GN_EPS)).reshape(B, S, RW_DIM) * gn_g + gn_b
    bonus = (jnp.sum(rh * kh * r_k, -1, keepdims=True) * vh).reshape(B, S, RW_DIM)
    return (y + bonus) * g


DSA_TQ = 256
MASK_NEG = -1e30
INT_MIN = -2 ** 31
KEY_NEG_INF = -2139095041
THRESH_BITS = 32


def _dsa_prep_kernel(z_ref, kvg_ref, kig_ref, kib_ref, qi_ref, kv_ref, ki_ref):
    z = z_ref[0]
    qi_ref[0] = z[:, :IDX_HEADS * IDX_DIM].astype(qi_ref.dtype)
    kv = z[:, 256:384]
    ms = jnp.mean(jnp.square(kv), -1, keepdims=True)
    kv_ref[0] = (kv * lax.rsqrt(ms + LN_EPS) * kvg_ref[...]).astype(kv_ref.dtype)
    ki = z[:, 384:448]
    mu = jnp.mean(ki, -1, keepdims=True)
    var = jnp.mean(jnp.square(ki - mu), -1, keepdims=True)
    ki_ref[0] = ((ki - mu) * lax.rsqrt(var + LN_EPS) * kig_ref[...] + kib_ref[...]).astype(ki_ref.dtype)


def _dsa_prep(z_small, kv_g, ki_g, ki_b, tm=512):
    B, S, W = z_small.shape
    return pl.pallas_call(
        _dsa_prep_kernel,
        grid=(B, S // tm),
        in_specs=[
            pl.BlockSpec((1, tm, W), lambda b, i: (b, i, 0)),
            pl.BlockSpec((1, DSA_LATENT), lambda b, i: (0, 0)),
            pl.BlockSpec((1, IDX_DIM), lambda b, i: (0, 0)),
            pl.BlockSpec((1, IDX_DIM), lambda b, i: (0, 0)),
        ],
        out_specs=[
            pl.BlockSpec((1, tm, IDX_HEADS * IDX_DIM), lambda b, i: (b, i, 0)),
            pl.BlockSpec((1, tm, DSA_LATENT), lambda b, i: (b, i, 0)),
            pl.BlockSpec((1, tm, IDX_DIM), lambda b, i: (b, i, 0)),
        ],
        out_shape=[
            jax.ShapeDtypeStruct((B, S, IDX_HEADS * IDX_DIM), MM_DTYPE),
            jax.ShapeDtypeStruct((B, S, DSA_LATENT), MM_DTYPE),
            jax.ShapeDtypeStruct((B, S, IDX_DIM), MM_DTYPE),
        ],
        name="dsa_prep",
    )(z_small, kv_g[None], ki_g[None], ki_b[None])


def _sortable_key(s):
    s = jnp.where(s == 0.0, 0.0, s)
    bits = pltpu.bitcast(s, jnp.int32)
    return bits ^ ((bits >> 31) & 0x7FFFFFFF)


def _col_count(mask_i32):
    tk, tq = mask_i32.shape
    return jnp.sum(mask_i32.reshape(tk // 8, 8, tq), axis=0)


def _dsa_kernel(bfar_ref, q_ref, qi_ref, wit_ref, kv_ref, kvt_ref, ki_ref, b0_ref, b1_ref, o_ref,
                key_ref, madd_ref, m_ref, l_ref, acc_ref, *, topk):
    tq = q_ref.shape[1]
    tk = tq
    i = pl.program_id(1)
    nj = i + 1
    f32 = jnp.float32
    krow = lax.broadcasted_iota(jnp.int32, (tk, tq), 0)
    qcol = lax.broadcasted_iota(jnp.int32, (tk, tq), 1)

    qi = qi_ref[0]
    wit = wit_ref[0] * (IDX_HEADS ** -0.5)

    def score_chunk(j, carry):
        off = pl.multiple_of(j * tk, tk)
        kc = ki_ref[0, pl.ds(off, tk), :]
        s = jnp.zeros((tk, tq), f32)
        for h in range(IDX_HEADS):
            d = lax.dot_general(kc, qi[:, h * IDX_DIM:(h + 1) * IDX_DIM],
                                (((1,), (1,)), ((), ())), preferred_element_type=f32)
            s = s + wit[h:h + 1, :] * jnp.maximum(d * (IDX_DIM ** -0.5), 0.0)
        causal = (krow + j * tk) <= (qcol + i * tq)
        s = jnp.where(causal, s, -jnp.inf)
        key_ref[j] = _sortable_key(s)
        return carry

    lax.fori_loop(0, nj, score_chunk, 0)

    def count_where(pred_fn):
        def body(j, acc):
            return acc + _col_count(pred_fn(key_ref[j], j).astype(jnp.int32))
        acc = lax.fori_loop(0, nj, body, jnp.zeros((8, tq), jnp.int32))
        return jnp.sum(acc, axis=0, keepdims=True)

    def bit_step(it, t_u):
        bit = THRESH_BITS - 1 - it
        cand_u = t_u | jnp.left_shift(jnp.int32(1), bit)
        cand = cand_u ^ INT_MIN
        cnt = count_where(lambda k, j: k >= cand)
        return jnp.where(cnt >= topk, cand_u, t_u)

    t_u = lax.fori_loop(0, THRESH_BITS, bit_step, jnp.zeros((1, tq), jnp.int32))
    thr = t_u ^ INT_MIN
    cnt_gt = count_where(lambda k, j: k > thr)
    cnt_ge = count_where(lambda k, j: k >= thr)
    is_neg = thr == KEY_NEG_INF
    need = jnp.logical_and(cnt_ge > topk, jnp.logical_not(is_neg))
    n_tie_take = topk - cnt_gt

    thr_open = jnp.where(is_neg, thr, thr - 1)
    any_need = jnp.max(need.astype(jnp.int32)) > 0

    @pl.when(jnp.logical_not(any_need))
    def _():
        def body(j, carry):
            madd_ref[j] = jnp.where(key_ref[j] > thr_open, 0.0, MASK_NEG)
            return carry
        lax.fori_loop(0, nj, body, 0)

    @pl.when(any_need)
    def _():
        s_len = tk * key_ref.shape[0]
        n_bits = max(1, int(math.ceil(math.log2(s_len))))

        def idx_step(it, p):
            bit = n_bits - 1 - it
            cand = p | jnp.left_shift(jnp.int32(1), bit)
            cnt = count_where(
                lambda k, j: jnp.where(k == thr, jnp.where((krow + j * tk) < cand, 1, 0), 0))
            return jnp.where(cnt < n_tie_take, cand, p)

        p_idx = lax.fori_loop(0, n_bits, idx_step, jnp.zeros((1, tq), jnp.int32))
        p_idx = jnp.where(need, p_idx, jnp.where(is_neg, -1, s_len))

        def body(j, carry):
            k = key_ref[j]
            tie_ok = jnp.where((krow + j * tk) <= p_idx, 0.0, MASK_NEG)
            madd_ref[j] = jnp.where(k > thr, 0.0, jnp.where(k == thr, tie_ok, MASK_NEG))
            return carry
        lax.fori_loop(0, nj, body, 0)

    m_ref[...] = jnp.full(m_ref.shape, MASK_NEG, f32)
    l_ref[...] = jnp.zeros(l_ref.shape, f32)
    acc_ref[...] = jnp.zeros(acc_ref.shape, f32)
    scale = DSA_LATENT ** -0.5

    def attend(j, bias_fn):
        off = pl.multiple_of(j * tk, tk)
        kc = kv_ref[0, pl.ds(off, tk), :]
        kct = kvt_ref[0, :, pl.ds(off, tk)]
        madd = madd_ref[j]
        for h in range(DSA_HEADS):
            qh = q_ref[0, :, h * DSA_LATENT:(h + 1) * DSA_LATENT]
            lg = lax.dot_general(kc, qh, (((1,), (1,)), ((), ())), preferred_element_type=f32)
            lg = lg * scale + bias_fn(h) + madd
            m_old = m_ref[h]
            m_new = jnp.maximum(m_old, jnp.max(lg, axis=0, keepdims=True))
            alpha = jnp.exp(m_old - m_new)
            p = jnp.exp(lg - m_new)
            l_ref[h] = alpha * l_ref[h] + jnp.sum(p, axis=0, keepdims=True)
            acc_ref[h] = alpha * acc_ref[h] + jnp.dot(kct, p.astype(kct.dtype),
                                                      preferred_element_type=f32)
            m_ref[h] = m_new

    def far_body(j, carry):
        attend(j, lambda h: bfar_ref[h])
        return carry

    lax.fori_loop(0, jnp.maximum(i - 1, 0), far_body, 0)

    @pl.when(i >= 1)
    def _():
        attend(i - 1, lambda h: b1_ref[h])

    attend(i, lambda h: b0_ref[h])

    for h in range(DSA_HEADS):
        out_t = acc_ref[h] / l_ref[h]
        o_ref[0, :, h * DSA_LATENT:(h + 1) * DSA_LATENT] = out_t.T.astype(o_ref.dtype)


def _dsa_bias_tiles(rel_bias, tq):
    dist = jnp.arange(2 * tq, dtype=jnp.int32)
    by_dist = rel_bias[_t5_bucket(dist)].T
    kk = jnp.arange(tq, dtype=jnp.int32)[:, None]
    qq = jnp.arange(tq, dtype=jnp.int32)[None, :]
    b0 = by_dist[:, jnp.maximum(qq - kk, 0)]
    b1 = by_dist[:, tq + qq - kk]
    return b0, b1, by_dist[:, -1]


def _dsa_attention(q, z_small, z_wi_t, kv_g, idx_k_g, idx_k_b, rel_bias, out_dtype):
    B, S, _ = q.shape
    tq = DSA_TQ
    topk = min(TOPK_MAX, S // 4)
    assert S % tq == 0 and tq >= REL_MAX_DIST and topk <= tq
    qi, ckv, kidx = _dsa_prep(z_small, kv_g, idx_k_g, idx_k_b)
    ckv_t = jnp.swapaxes(ckv, 1, 2)
    b0, b1, bfar = _dsa_bias_tiles(rel_bias, tq)
    nq = S // tq
    H, d = DSA_HEADS, DSA_LATENT
    grid_spec = pltpu.PrefetchScalarGridSpec(
        num_scalar_prefetch=0,
        grid=(B, nq),
        in_specs=[
            pl.BlockSpec(memory_space=pltpu.SMEM),
            pl.BlockSpec((1, tq, H * d), lambda b, i: (b, i, 0)),
            pl.BlockSpec((1, tq, IDX_HEADS * IDX_DIM), lambda b, i: (b, i, 0)),
            pl.BlockSpec((1, IDX_HEADS, tq), lambda b, i: (b, 0, i)),
            pl.BlockSpec((1, S, d), lambda b, i: (b, 0, 0)),
            pl.BlockSpec((1, d, S), lambda b, i: (b, 0, 0)),
            pl.BlockSpec((1, S, IDX_DIM), lambda b, i: (b, 0, 0)),
            pl.BlockSpec((H, tq, tq), lambda b, i: (0, 0, 0)),
            pl.BlockSpec((H, tq, tq), lambda b, i: (0, 0, 0)),
        ],
        out_specs=pl.BlockSpec((1, tq, H * d), lambda b, i: (b, i, 0)),
        scratch_shapes=[
            pltpu.VMEM((nq, tq, tq), jnp.int32),
            pltpu.VMEM((nq, tq, tq), jnp.float32),
            pltpu.VMEM((H, 1, tq), jnp.float32),
            pltpu.VMEM((H, 1, tq), jnp.float32),
            pltpu.VMEM((H, d, tq), jnp.float32),
        ],
    )
    return pl.pallas_call(
        partial(_dsa_kernel, topk=topk),
        grid_spec=grid_spec,
        out_shape=jax.ShapeDtypeStruct((B, S, H * d), out_dtype),
        compiler_params=pltpu.CompilerParams(vmem_limit_bytes=48 * 1024 * 1024),
        name="dsa_attention",
    )(bfar, q, qi, z_wi_t, ckv, ckv_t, kidx, b0, b1)


def _peer(h, w_pq, sub_keys, u_tab, v_tab):
    B, S, D = h.shape
    T = B * S
    ht = h.reshape(T, D)
    q = (ht @ w_pq).reshape(T, PEER_HEADS, 2, PEER_HALF)
    s = jnp.einsum('thcd,hcnd->thcn', q, sub_keys)
    s1, i1 = lax.top_k(s[:, :, 0], PEER_TOPK)
    s2, i2 = lax.top_k(s[:, :, 1], PEER_TOPK)
    cand = (s1[..., :, None] + s2[..., None, :]).reshape(T, PEER_HEADS, PEER_TOPK * PEER_TOPK)
    cand_idx = (i1[..., :, None] * PEER_N_KEYS + i2[..., None, :]).reshape(T, PEER_HEADS, PEER_TOPK * PEER_TOPK)
    top_s, pos = lax.top_k(cand, PEER_TOPK)
    experts = jnp.take_along_axis(cand_idx, pos, axis=-1)
    gates = jax.nn.softmax(top_s, axis=-1)
    n_chunks = T // PEER_CHUNK

    def chunk(args):
        hc, ec, gc = args
        act = jax.nn.gelu(jnp.einsum('cd,chkd->chk', hc, u_tab[ec]), approximate=False) * gc
        return jnp.einsum('chk,chkd->cd', act, v_tab[ec])

    out = lax.map(chunk, (ht.reshape(n_chunks, PEER_CHUNK, D),
                          experts.reshape(n_chunks, PEER_CHUNK, PEER_HEADS, PEER_TOPK),
                          gates.reshape(n_chunks, PEER_CHUNK, PEER_HEADS, PEER_TOPK)))
    return out.reshape(B, S, D)


def kernel(x, c, w_ada, b_ada, w_in, rw_mu, rw_w0, rw_w2, rw_a0, rw_a2, rw_g2, rw_k_k, rw_k_a, rw_r_k, rw_gn_g, rw_gn_b, dsa_kv_g, idx_k_g, idx_k_b, rel_bias, w_br_a, w_br_b, w_out, ln1_g, ln1_b, peer_wq, peer_keys, peer_u, peer_v, ln2_g, ln2_b):
    l = 0
    mod = jax.nn.silu(c) @ w_ada[l] + b_ada[l]
    sh1, sc1, gt1, sh2, sc2, gt2 = jnp.split(mod, 6, axis=-1)

    w_rw, w_q, w_kv, w_qi, w_ki, w_wi, w_ga, w_gb = jnp.split(w_in[l], _split_points(IN_SIZES), axis=-1)
    small_pad = jnp.zeros((D_MODEL, 512 - 256 - 128 - 64 - 4), w_in.dtype)
    w_small = jnp.concatenate([w_qi, w_kv, w_ki, w_wi, small_pad], axis=-1)
    z_rw = _mod_matmul(x, sc1, sh1, w_rw, jnp.float32, tn=896)
    z_q = _mod_matmul(x, sc1, sh1, w_q, MM_DTYPE, tn=1024)
    z_small = _mod_matmul(x, sc1, sh1, w_small, jnp.float32, tn=512)
    z_g = _mod_matmul(x, sc1, sh1, jnp.concatenate([w_ga, w_gb], axis=-1), jnp.float32, tn=1024)
    z_ga, z_gb = z_g[..., :D_MODEL], z_g[..., D_MODEL:]
    z_wi_t = jnp.swapaxes(z_small[..., 448:448 + IDX_HEADS], 1, 2)

    y_a = _rwkv7_time_mix(z_rw, rw_mu[l], rw_w0[l], rw_w2[l], rw_a0[l], rw_a2[l], rw_g2[l],
                          rw_k_k[l], rw_k_a[l], rw_r_k[l], rw_gn_g[l], rw_gn_b[l]) @ w_br_a[l]
    y_b = _dsa_attention(z_q, z_small, z_wi_t, dsa_kv_g[l], idx_k_g[l], idx_k_b[l],
                         rel_bias, jnp.float32) @ w_br_b[l]
    merged = jax.nn.sigmoid(z_ga) * y_a + jax.nn.sigmoid(z_gb) * y_b
    mix = merged @ w_out[l]
    x = _layer_norm(DEEPNORM_ALPHA * x + gt1[:, None] * mix, ln1_g[l], ln1_b[l])

    h2 = x * (1 + sc2[:, None]) + sh2[:, None]
    y2 = _peer(h2, peer_wq[l], peer_keys[l], peer_u[l], peer_v[l])
    x = _layer_norm(DEEPNORM_ALPHA * x + gt2[:, None] * y2, ln2_g[l], ln2_b[l])
    return x
```

```python
import math
from functools import partial

import jax
import jax.numpy as jnp
import numpy as np
from jax import lax
from jax.experimental import pallas as pl
from jax.experimental.pallas import tpu as pltpu
from jax.experimental.pallas import tpu_sc as plsc

D_MODEL = 1024
RW_HEADS = 8
RW_HEAD_DIM = 64
RW_DIM = 512
RW_DECAY_LORA = 64
RW_A_LORA = 64
RW_GATE_LORA = 128
RW_SIZES = (RW_DIM, RW_DIM, RW_DIM, RW_DECAY_LORA, RW_A_LORA, RW_GATE_LORA)
RW_COLS = 3 * RW_DIM + RW_DECAY_LORA + RW_A_LORA + RW_GATE_LORA
RW_GN_EPS = 64e-5
DSA_HEADS = 8
DSA_LATENT = 128
DSA_Q_DIM = DSA_HEADS * DSA_LATENT
IDX_HEADS = 4
IDX_DIM = 64
TOPK_MAX = 256
Q_BLOCK = 128
REL_BUCKETS = 32
REL_MAX_DIST = 128
IN_SIZES = (RW_COLS, DSA_Q_DIM, DSA_LATENT, IDX_HEADS * IDX_DIM, IDX_DIM, IDX_HEADS, D_MODEL, D_MODEL)
IN_COLS = sum(IN_SIZES)
PEER_HEADS = 8
PEER_N_KEYS = 128
PEER_KEY_DIM = 128
PEER_HALF = 64
PEER_TOPK = 16
PEER_CHUNK = 128
LN_EPS = 1e-5
DEPTH = 1
DEEPNORM_ALPHA = (2.0 * DEPTH) ** 0.25

LANES = 128
MM_DTYPE = jnp.bfloat16


def _split_points(sizes):
    return np.cumsum(sizes)[:-1].tolist()


def _mod_matmul_kernel(x_ref, sc_ref, sh_ref, w_ref, o_ref):
    h = x_ref[0] * (1.0 + sc_ref[0]) + sh_ref[0]
    o_ref[0] = jnp.dot(h.astype(w_ref.dtype), w_ref[...],
                       preferred_element_type=jnp.float32).astype(o_ref.dtype)


def _mod_matmul(x, sc, sh, w, out_dtype, tn, tm=512):
    B, S, D = x.shape
    N = w.shape[1]
    return pl.pallas_call(
        _mod_matmul_kernel,
        grid=(B, S // tm, N // tn),
        in_specs=[
            pl.BlockSpec((1, tm, D), lambda b, i, j: (b, i, 0)),
            pl.BlockSpec((1, 1, D), lambda b, i, j: (b, 0, 0)),
            pl.BlockSpec((1, 1, D), lambda b, i, j: (b, 0, 0)),
            pl.BlockSpec((D, tn), lambda b, i, j: (0, j)),
        ],
        out_specs=pl.BlockSpec((1, tm, tn), lambda b, i, j: (b, i, j)),
        out_shape=jax.ShapeDtypeStruct((B, S, N), out_dtype),
        name="mod_matmul",
    )(x, sc[:, None, :], sh[:, None, :], w.astype(MM_DTYPE))


def _layer_norm(x, g, b):
    mu = jnp.mean(x, -1, keepdims=True)
    var = jnp.mean(jnp.square(x - mu), -1, keepdims=True)
    return (x - mu) * lax.rsqrt(var + LN_EPS) * g + b


def _rms_norm(x, g):
    ms = jnp.mean(jnp.square(x), -1, keepdims=True)
    return x * lax.rsqrt(ms + LN_EPS) * g


def _heads(t, n_heads):
    return t.reshape(*t.shape[:-1], n_heads, t.shape[-1] // n_heads)


def _token_shift(z):
    return jnp.pad(z, ((0, 0), (1, 0), (0, 0)))[:, :-1]


def _t5_bucket(n):
    n = jnp.maximum(n, 0)
    max_exact = REL_BUCKETS // 2
    nf = jnp.maximum(n, 1).astype(jnp.float32)
    large = max_exact + (jnp.log(nf / max_exact) / math.log(REL_MAX_DIST / max_exact)
                         * (REL_BUCKETS - max_exact)).astype(jnp.int32)
    large = jnp.minimum(large, REL_BUCKETS - 1)
    return jnp.where(n < max_exact, n, large)


def _rwkv7_time_mix(z_rw, mu, w0, w2, a0, a2, g2, k_k, k_a, r_k, gn_g, gn_b):
    B, S, _ = z_rw.shape
    zs = z_rw + (_token_shift(z_rw) - z_rw) * mu
    r, k, v, wl, al, gl = jnp.split(zs, _split_points(RW_SIZES), axis=-1)
    log_w = -jax.nn.softplus(-(w0 + jnp.tanh(wl) @ w2)) - 0.5
    decay = jnp.exp(-jnp.exp(log_w))
    a = jax.nn.sigmoid(a0 + al @ a2)
    g = jax.nn.sigmoid(gl) @ g2
    kk = _heads(k * k_k, RW_HEADS)
    kk = kk * lax.rsqrt(jnp.maximum(jnp.sum(kk * kk, -1, keepdims=True), 1e-24))
    k = k * (1 + (a - 1) * k_a)
    rh, kh, vh, dh, ah = [_heads(t, RW_HEADS) for t in (r, k, v, decay, a)]
    a_vec = -kk
    b_vec = kk * ah

    def step(state, inp):
        r_t, w_t, k_t, v_t, a_t, b_t = inp
        sa = jnp.einsum('bhvk,bhk->bhv', state, a_t)
        state = (state * w_t[:, :, None, :] + sa[..., None] * b_t[:, :, None, :]
                 + v_t[..., None] * k_t[:, :, None, :])
        return state, jnp.einsum('bhvk,bhk->bhv', state, r_t)

    seq = tuple(jnp.moveaxis(t, 1, 0) for t in (rh, dh, kh, vh, a_vec, b_vec))
    state0 = jnp.zeros((B, RW_HEADS, RW_HEAD_DIM, RW_HEAD_DIM), jnp.float32)
    _, y = lax.scan(step, state0, seq)
    y = jnp.moveaxis(y, 0, 1)
    mean = jnp.mean(y, -1, keepdims=True)
    var = jnp.mean(jnp.square(y - mean), -1, keepdims=True)
    y = ((y - mean) * lax.rsqrt(var + RW_GN_EPS)).reshape(B, S, RW_DIM) * gn_g + gn_b
    bonus = (jnp.sum(rh * kh * r_k, -1, keepdims=True) * vh).reshape(B, S, RW_DIM)
    return (y + bonus) * g


DSA_TQ = 256
MASK_NEG = -1e30
INT_MIN = -2 ** 31
KEY_NEG_INF = -2139095041
THRESH_BITS = 32


def _dsa_prep_kernel(z_ref, kvg_ref, kig_ref, kib_ref, qi_ref, kv_ref, ki_ref):
    z = z_ref[0]
    qi_ref[0] = z[:, :IDX_HEADS * IDX_DIM].astype(qi_ref.dtype)
    kv = z[:, 256:384]
    ms = jnp.mean(jnp.square(kv), -1, keepdims=True)
    kv_ref[0] = (kv * lax.rsqrt(ms + LN_EPS) * kvg_ref[...]).astype(kv_ref.dtype)
    ki = z[:, 384:448]
    mu = jnp.mean(ki, -1, keepdims=True)
    var = jnp.mean(jnp.square(ki - mu), -1, keepdims=True)
    ki_ref[0] = ((ki - mu) * lax.rsqrt(var + LN_EPS) * kig_ref[...] + kib_ref[...]).astype(ki_ref.dtype)


def _dsa_prep(z_small, kv_g, ki_g, ki_b, tm=512):
    B, S, W = z_small.shape
    return pl.pallas_call(
        _dsa_prep_kernel,
        grid=(B, S // tm),
        in_specs=[
            pl.BlockSpec((1, tm, W), lambda b, i: (b, i, 0)),
            pl.BlockSpec((1, DSA_LATENT), lambda b, i: (0, 0)),
            pl.BlockSpec((1, IDX_DIM), lambda b, i: (0, 0)),
            pl.BlockSpec((1, IDX_DIM), lambda b, i: (0, 0)),
        ],
        out_specs=[
            pl.BlockSpec((1, tm, IDX_HEADS * IDX_DIM), lambda b, i: (b, i, 0)),
            pl.BlockSpec((1, tm, DSA_LATENT), lambda b, i: (b, i, 0)),
            pl.BlockSpec((1, tm, IDX_DIM), lambda b, i: (b, i, 0)),
        ],
        out_shape=[
            jax.ShapeDtypeStruct((B, S, IDX_HEADS * IDX_DIM), MM_DTYPE),
            jax.ShapeDtypeStruct((B, S, DSA_LATENT), MM_DTYPE),
            jax.ShapeDtypeStruct((B, S, IDX_DIM), MM_DTYPE),
        ],
        name="dsa_prep",
    )(z_small, kv_g[None], ki_g[None], ki_b[None])


def _sortable_key(s):
    s = jnp.where(s == 0.0, 0.0, s)
    bits = pltpu.bitcast(s, jnp.int32)
    return bits ^ ((bits >> 31) & 0x7FFFFFFF)


def _col_count(mask_i32):
    tk, tq = mask_i32.shape
    return jnp.sum(mask_i32.reshape(tk // 8, 8, tq), axis=0)


def _dsa_kernel(bfar_ref, q_ref, qi_ref, wit_ref, kv_ref, kvt_ref, ki_ref, b0_ref, b1_ref, o_ref,
                key_ref, madd_ref, m_ref, l_ref, acc_ref, *, topk):
    tq = q_ref.shape[1]
    tk = tq
    i = pl.program_id(1)
    nj = i + 1
    f32 = jnp.float32
    krow = lax.broadcasted_iota(jnp.int32, (tk, tq), 0)
    qcol = lax.broadcasted_iota(jnp.int32, (tk, tq), 1)

    qi = qi_ref[0]
    wit = wit_ref[0] * (IDX_HEADS ** -0.5)

    def score_chunk(j, carry):
        off = pl.multiple_of(j * tk, tk)
        kc = ki_ref[0, pl.ds(off, tk), :]
        s = jnp.zeros((tk, tq), f32)
        for h in range(IDX_HEADS):
            d = lax.dot_general(kc, qi[:, h * IDX_DIM:(h + 1) * IDX_DIM],
                                (((1,), (1,)), ((), ())), preferred_element_type=f32)
            s = s + wit[h:h + 1, :] * jnp.maximum(d * (IDX_DIM ** -0.5), 0.0)
        causal = (krow + j * tk) <= (qcol + i * tq)
        s = jnp.where(causal, s, -jnp.inf)
        key_ref[j] = _sortable_key(s)
        return carry

    lax.fori_loop(0, nj, score_chunk, 0)

    def count_where(pred_fn):
        def body(j, acc):
            return acc + _col_count(pred_fn(key_ref[j], j).astype(jnp.int32))
        acc = lax.fori_loop(0, nj, body, jnp.zeros((8, tq), jnp.int32))
        return jnp.sum(acc, axis=0, keepdims=True)

    def bit_step(it, t_u):
        bit = THRESH_BITS - 1 - it
        cand_u = t_u | jnp.left_shift(jnp.int32(1), bit)
        cand = cand_u ^ INT_MIN
        cnt = count_where(lambda k, j: k >= cand)
        return jnp.where(cnt >= topk, cand_u, t_u)

    t_u = lax.fori_loop(0, THRESH_BITS, bit_step, jnp.zeros((1, tq), jnp.int32))
    thr = t_u ^ INT_MIN
    cnt_gt = count_where(lambda k, j: k > thr)
    cnt_ge = count_where(lambda k, j: k >= thr)
    is_neg = thr == KEY_NEG_INF
    need = jnp.logical_and(cnt_ge > topk, jnp.logical_not(is_neg))
    n_tie_take = topk - cnt_gt

    thr_open = jnp.where(is_neg, thr, thr - 1)
    any_need = jnp.max(need.astype(jnp.int32)) > 0

    @pl.when(jnp.logical_not(any_need))
    def _():
        def body(j, carry):
            madd_ref[j] = jnp.where(key_ref[j] > thr_open, 0.0, MASK_NEG)
            return carry
        lax.fori_loop(0, nj, body, 0)

    @pl.when(any_need)
    def _():
        s_len = tk * key_ref.shape[0]
        n_bits = max(1, int(math.ceil(math.log2(s_len))))

        def idx_step(it, p):
            bit = n_bits - 1 - it
            cand = p | jnp.left_shift(jnp.int32(1), bit)
            cnt = count_where(
                lambda k, j: jnp.where(k == thr, jnp.where((krow + j * tk) < cand, 1, 0), 0))
            return jnp.where(cnt < n_tie_take, cand, p)

        p_idx = lax.fori_loop(0, n_bits, idx_step, jnp.zeros((1, tq), jnp.int32))
        p_idx = jnp.where(need, p_idx, jnp.where(is_neg, -1, s_len))

        def body(j, carry):
            k = key_ref[j]
            tie_ok = jnp.where((krow + j * tk) <= p_idx, 0.0, MASK_NEG)
            madd_ref[j] = jnp.where(k > thr, 0.0, jnp.where(k == thr, tie_ok, MASK_NEG))
            return carry
        lax.fori_loop(0, nj, body, 0)

    m_ref[...] = jnp.full(m_ref.shape, MASK_NEG, f32)
    l_ref[...] = jnp.zeros(l_ref.shape, f32)
    acc_ref[...] = jnp.zeros(acc_ref.shape, f32)
    scale = DSA_LATENT ** -0.5

    def attend(j, bias_fn):
        off = pl.multiple_of(j * tk, tk)
        kc = kv_ref[0, pl.ds(off, tk), :]
        kct = kvt_ref[0, :, pl.ds(off, tk)]
        madd = madd_ref[j]
        for h in range(DSA_HEADS):
            qh = q_ref[0, :, h * DSA_LATENT:(h + 1) * DSA_LATENT]
            lg = lax.dot_general(kc, qh, (((1,), (1,)), ((), ())), preferred_element_type=f32)
            lg = lg * scale + bias_fn(h) + madd
            m_old = m_ref[h]
            m_new = jnp.maximum(m_old, jnp.max(lg, axis=0, keepdims=True))
            alpha = jnp.exp(m_old - m_new)
            p = jnp.exp(lg - m_new)
            l_ref[h] = alpha * l_ref[h] + jnp.sum(p, axis=0, keepdims=True)
            acc_ref[h] = alpha * acc_ref[h] + jnp.dot(kct, p.astype(kct.dtype),
                                                      preferred_element_type=f32)
            m_ref[h] = m_new

    def far_body(j, carry):
        attend(j, lambda h: bfar_ref[h])
        return carry

    lax.fori_loop(0, jnp.maximum(i - 1, 0), far_body, 0)

    @pl.when(i >= 1)
    def _():
        attend(i - 1, lambda h: b1_ref[h])

    attend(i, lambda h: b0_ref[h])

    for h in range(DSA_HEADS):
        out_t = acc_ref[h] / l_ref[h]
        o_ref[0, :, h * DSA_LATENT:(h + 1) * DSA_LATENT] = out_t.T.astype(o_ref.dtype)


def _dsa_bias_tiles(rel_bias, tq):
    dist = jnp.arange(2 * tq, dtype=jnp.int32)
    by_dist = rel_bias[_t5_bucket(dist)].T
    kk = jnp.arange(tq, dtype=jnp.int32)[:, None]
    qq = jnp.arange(tq, dtype=jnp.int32)[None, :]
    b0 = by_dist[:, jnp.maximum(qq - kk, 0)]
    b1 = by_dist[:, tq + qq - kk]
    return b0, b1, by_dist[:, -1]


def _dsa_attention(q, z_small, z_wi_t, kv_g, idx_k_g, idx_k_b, rel_bias, out_dtype):
    B, S, _ = q.shape
    tq = DSA_TQ
    topk = min(TOPK_MAX, S // 4)
    assert S % tq == 0 and tq >= REL_MAX_DIST and topk <= tq
    qi, ckv, kidx = _dsa_prep(z_small, kv_g, idx_k_g, idx_k_b)
    ckv_t = jnp.swapaxes(ckv, 1, 2)
    b0, b1, bfar = _dsa_bias_tiles(rel_bias, tq)
    nq = S // tq
    H, d = DSA_HEADS, DSA_LATENT
    grid_spec = pltpu.PrefetchScalarGridSpec(
        num_scalar_prefetch=0,
        grid=(B, nq),
        in_specs=[
            pl.BlockSpec(memory_space=pltpu.SMEM),
            pl.BlockSpec((1, tq, H * d), lambda b, i: (b, i, 0)),
            pl.BlockSpec((1, tq, IDX_HEADS * IDX_DIM), lambda b, i: (b, i, 0)),
            pl.BlockSpec((1, IDX_HEADS, tq), lambda b, i: (b, 0, i)),
            pl.BlockSpec((1, S, d), lambda b, i: (b, 0, 0)),
            pl.BlockSpec((1, d, S), lambda b, i: (b, 0, 0)),
            pl.BlockSpec((1, S, IDX_DIM), lambda b, i: (b, 0, 0)),
            pl.BlockSpec((H, tq, tq), lambda b, i: (0, 0, 0)),
            pl.BlockSpec((H, tq, tq), lambda b, i: (0, 0, 0)),
        ],
        out_specs=pl.BlockSpec((1, tq, H * d), lambda b, i: (b, i, 0)),
        scratch_shapes=[
            pltpu.VMEM((nq, tq, tq), jnp.int32),
            pltpu.VMEM((nq, tq, tq), jnp.float32),
            pltpu.VMEM((H, 1, tq), jnp.float32),
            pltpu.VMEM((H, 1, tq), jnp.float32),
            pltpu.VMEM((H, d, tq), jnp.float32),
        ],
    )
    return pl.pallas_call(
        partial(_dsa_kernel, topk=topk),
        grid_spec=grid_spec,
        out_shape=jax.ShapeDtypeStruct((B, S, H * d), out_dtype),
        compiler_params=pltpu.CompilerParams(vmem_limit_bytes=48 * 1024 * 1024),
        name="dsa_attention",
    )(bfar, q, qi, z_wi_t, ckv, ckv_t, kidx, b0, b1)


SC_CORES = 2
SC_SUBCORES = 16
SC_LANES = 16
PEER_SC_TOKENS = 8
PEER_SC_RING = 4
PEER_SC_UNROLL = 4
PEER_SLOTS = PEER_HEADS * PEER_TOPK


def _peer_sc_call(body, T, out_width, stage_width):
    mesh = plsc.VectorSubcoreMesh(core_axis_name="c", subcore_axis_name="s")
    return pl.kernel(
        body, mesh=mesh,
        out_type=jax.ShapeDtypeStruct((T, out_width), jnp.float32),
        scratch_types=[
            pltpu.VMEM((PEER_SC_TOKENS, PEER_SLOTS), jnp.int32),
            pltpu.VMEM((PEER_SC_TOKENS, stage_width), jnp.float32),
            pltpu.VMEM((PEER_SC_RING, PEER_TOPK, D_MODEL), jnp.float32),
            pltpu.VMEM((PEER_SC_TOKENS, out_width), jnp.float32),
            pltpu.SemaphoreType.DMA((PEER_SC_RING,)),
        ],
        compiler_params=pltpu.CompilerParams(needs_layout_passes=False),
    )


def _peer_sc_body(compute, zero_out, x_hbm, idx_hbm, tab_hbm, out_hbm, idx_v, x_v, rows_v, out_v, sems):
    T = idx_hbm.shape[0]
    tokens_per_worker = T // (SC_CORES * SC_SUBCORES)
    n_steps = PEER_SC_TOKENS * PEER_HEADS
    worker = lax.axis_index("s") * SC_CORES + lax.axis_index("c")
    base = worker * tokens_per_worker

    def gather(s, b):
        ids = idx_v[s // PEER_HEADS, pl.ds((s % PEER_HEADS) * PEER_TOPK, PEER_TOPK)]
        return pltpu.make_async_copy(tab_hbm.at[ids], rows_v.at[b], sems.at[b])

    @pl.loop(0, tokens_per_worker // PEER_SC_TOKENS)
    def _(blk):
        tok0 = base + blk * PEER_SC_TOKENS
        pltpu.sync_copy(idx_hbm.at[pl.ds(tok0, PEER_SC_TOKENS)], idx_v)
        pltpu.sync_copy(x_hbm.at[pl.ds(tok0, PEER_SC_TOKENS)], x_v)
        for b in range(PEER_SC_RING - 1):
            gather(b, b).start()
        if zero_out:
            @pl.loop(0, PEER_SC_TOKENS)
            def _(t):
                @plsc.parallel_loop(0, out_v.shape[1] // SC_LANES, unroll=PEER_SC_UNROLL)
                def _(c):
                    out_v[t, pl.ds(pl.multiple_of(c * SC_LANES, SC_LANES), SC_LANES)] = (
                        jnp.zeros((SC_LANES,), jnp.float32))

        @pl.loop(0, n_steps, step=PEER_SC_RING)
        def _(s0):
            for b in range(PEER_SC_RING):
                s = s0 + b
                gather(s, b).wait()

                @pl.when(s + PEER_SC_RING - 1 < n_steps)
                def _():
                    gather(s + PEER_SC_RING - 1, (b + PEER_SC_RING - 1) % PEER_SC_RING).start()

                compute(s // PEER_HEADS, s % PEER_HEADS, b, x_v, rows_v, out_v)

        pltpu.sync_copy(out_v, out_hbm.at[pl.ds(tok0, PEER_SC_TOKENS)])


def _peer_dots_compute(t, hd, b, h_v, rows_v, dots_v):
    lane = lax.iota(jnp.int32, SC_LANES)

    def col_step(c, accs):
        off = pl.multiple_of(c * SC_LANES, SC_LANES)
        hv = h_v[t, pl.ds(off, SC_LANES)]
        return tuple(accs[r] + rows_v[b, r, pl.ds(off, SC_LANES)] * hv for r in range(PEER_TOPK))

    accs = plsc.parallel_loop(
        0, D_MODEL // SC_LANES, unroll=PEER_SC_UNROLL,
        carry=tuple(jnp.zeros((SC_LANES,), jnp.float32) for _ in range(PEER_TOPK)))(col_step)
    res = jnp.zeros((SC_LANES,), jnp.float32)
    for r in range(PEER_TOPK):
        res = jnp.where(lane == r, jnp.sum(accs[r]), res)
    dots_v[t, pl.ds(hd * PEER_TOPK, PEER_TOPK)] = res


def _peer_mix_compute(t, hd, b, act_v, rows_v, out_v):
    tvec = jnp.full((SC_LANES,), t, jnp.int32)
    weights = [plsc.load_gather(act_v, [tvec, jnp.full((SC_LANES,), hd * PEER_TOPK + r, jnp.int32)])
               for r in range(PEER_TOPK)]

    @plsc.parallel_loop(0, D_MODEL // SC_LANES, unroll=PEER_SC_UNROLL)
    def _(c):
        off = pl.multiple_of(c * SC_LANES, SC_LANES)
        a = rows_v[b, 0, pl.ds(off, SC_LANES)] * weights[0]
        for r in range(1, PEER_TOPK):
            a = a + rows_v[b, r, pl.ds(off, SC_LANES)] * weights[r]
        plsc.addupdate(out_v.at[t, pl.ds(off, SC_LANES)], a)


def _peer_expert_dots(h, experts, u_tab):
    T = h.shape[0]
    assert T % (SC_CORES * SC_SUBCORES * PEER_SC_TOKENS) == 0
    assert (PEER_SC_TOKENS * PEER_HEADS) % PEER_SC_RING == 0 and PEER_TOPK == SC_LANES
    body = partial(_peer_sc_body, _peer_dots_compute, False)
    return _peer_sc_call(body, T, PEER_SLOTS, D_MODEL)(h, experts, u_tab)


def _peer_expert_mix(act, experts, v_tab):
    T = act.shape[0]
    assert T % (SC_CORES * SC_SUBCORES * PEER_SC_TOKENS) == 0
    body = partial(_peer_sc_body, _peer_mix_compute, True)
    return _peer_sc_call(body, T, D_MODEL, PEER_SLOTS)(act, experts, v_tab)


def _peer(h, w_pq, sub_keys, u_tab, v_tab):
    B, S, D = h.shape
    T = B * S
    ht = h.reshape(T, D)
    q = (ht @ w_pq).reshape(T, PEER_HEADS, 2, PEER_HALF)
    s = jnp.einsum('thcd,hcnd->thcn', q, sub_keys)
    s1, i1 = lax.top_k(s[:, :, 0], PEER_TOPK)
    s2, i2 = lax.top_k(s[:, :, 1], PEER_TOPK)
    cand = (s1[..., :, None] + s2[..., None, :]).reshape(T, PEER_HEADS, PEER_TOPK * PEER_TOPK)
    cand_idx = (i1[..., :, None] * PEER_N_KEYS + i2[..., None, :]).reshape(T, PEER_HEADS, PEER_TOPK * PEER_TOPK)
    top_s, pos = lax.top_k(cand, PEER_TOPK)
    experts = jnp.take_along_axis(cand_idx, pos, axis=-1)
    gates = jax.nn.softmax(top_s, axis=-1)
    experts = experts.reshape(T, PEER_HEADS * PEER_TOPK)
    pre = _peer_expert_dots(ht, experts, u_tab)
    act = jax.nn.gelu(pre, approximate=False) * gates.reshape(T, PEER_HEADS * PEER_TOPK)
    return _peer_expert_mix(act, experts, v_tab).reshape(B, S, D)


def kernel(x, c, w_ada, b_ada, w_in, rw_mu, rw_w0, rw_w2, rw_a0, rw_a2, rw_g2, rw_k_k, rw_k_a, rw_r_k, rw_gn_g, rw_gn_b, dsa_kv_g, idx_k_g, idx_k_b, rel_bias, w_br_a, w_br_b, w_out, ln1_g, ln1_b, peer_wq, peer_keys, peer_u, peer_v, ln2_g, ln2_b):
    l = 0
    mod = jax.nn.silu(c) @ w_ada[l] + b_ada[l]
    sh1, sc1, gt1, sh2, sc2, gt2 = jnp.split(mod, 6, axis=-1)

    w_rw, w_q, w_kv, w_qi, w_ki, w_wi, w_ga, w_gb = jnp.split(w_in[l], _split_points(IN_SIZES), axis=-1)
    small_pad = jnp.zeros((D_MODEL, 512 - 256 - 128 - 64 - 4), w_in.dtype)
    w_small = jnp.concatenate([w_qi, w_kv, w_ki, w_wi, small_pad], axis=-1)
    z_rw = _mod_matmul(x, sc1, sh1, w_rw, jnp.float32, tn=896)
    z_q = _mod_matmul(x, sc1, sh1, w_q, MM_DTYPE, tn=1024)
    z_small = _mod_matmul(x, sc1, sh1, w_small, jnp.float32, tn=512)
    z_g = _mod_matmul(x, sc1, sh1, jnp.concatenate([w_ga, w_gb], axis=-1), jnp.float32, tn=1024)
    z_ga, z_gb = z_g[..., :D_MODEL], z_g[..., D_MODEL:]
    z_wi_t = jnp.swapaxes(z_small[..., 448:448 + IDX_HEADS], 1, 2)

    y_a = _rwkv7_time_mix(z_rw, rw_mu[l], rw_w0[l], rw_w2[l], rw_a0[l], rw_a2[l], rw_g2[l],
                          rw_k_k[l], rw_k_a[l], rw_r_k[l], rw_gn_g[l], rw_gn_b[l]) @ w_br_a[l]
    y_b = _dsa_attention(z_q, z_small, z_wi_t, dsa_kv_g[l], idx_k_g[l], idx_k_b[l],
                         rel_bias, jnp.float32) @ w_br_b[l]
    merged = jax.nn.sigmoid(z_ga) * y_a + jax.nn.sigmoid(z_gb) * y_b
    mix = merged @ w_out[l]
    x = _layer_norm(DEEPNORM_ALPHA * x + gt1[:, None] * mix, ln1_g[l], ln1_b[l])

    h2 = x * (1 + sc2[:, None]) + sh2[:, None]
    y2 = _peer(h2, peer_wq[l], peer_keys[l], peer_u[l], peer_v[l])
    x = _layer_norm(DEEPNORM_ALPHA * x + gt2[:, None] * y2, ln2_g[l], ln2_b[l])
    return x
```

```python
import math
from functools import partial

import jax
import jax.numpy as jnp
import numpy as np
from jax import lax
from jax.experimental import pallas as pl
from jax.experimental.pallas import tpu as pltpu
from jax.experimental.pallas import tpu_sc as plsc

D_MODEL = 1024
RW_HEADS = 8
RW_HEAD_DIM = 64
RW_DIM = 512
RW_DECAY_LORA = 64
RW_A_LORA = 64
RW_GATE_LORA = 128
RW_SIZES = (RW_DIM, RW_DIM, RW_DIM, RW_DECAY_LORA, RW_A_LORA, RW_GATE_LORA)
RW_COLS = 3 * RW_DIM + RW_DECAY_LORA + RW_A_LORA + RW_GATE_LORA
RW_GN_EPS = 64e-5
DSA_HEADS = 8
DSA_LATENT = 128
DSA_Q_DIM = DSA_HEADS * DSA_LATENT
IDX_HEADS = 4
IDX_DIM = 64
TOPK_MAX = 256
Q_BLOCK = 128
REL_BUCKETS = 32
REL_MAX_DIST = 128
IN_SIZES = (RW_COLS, DSA_Q_DIM, DSA_LATENT, IDX_HEADS * IDX_DIM, IDX_DIM, IDX_HEADS, D_MODEL, D_MODEL)
IN_COLS = sum(IN_SIZES)
PEER_HEADS = 8
PEER_N_KEYS = 128
PEER_KEY_DIM = 128
PEER_HALF = 64
PEER_TOPK = 16
PEER_CHUNK = 128
LN_EPS = 1e-5
DEPTH = 1
DEEPNORM_ALPHA = (2.0 * DEPTH) ** 0.25

LANES = 128
MM_DTYPE = jnp.bfloat16


def _split_points(sizes):
    return np.cumsum(sizes)[:-1].tolist()


def _mod_matmul_kernel(x_ref, sc_ref, sh_ref, w_ref, o_ref):
    h = x_ref[0] * (1.0 + sc_ref[0]) + sh_ref[0]
    o_ref[0] = jnp.dot(h.astype(w_ref.dtype), w_ref[...],
                       preferred_element_type=jnp.float32).astype(o_ref.dtype)


def _mod_matmul(x, sc, sh, w, out_dtype, tn, tm=512):
    B, S, D = x.shape
    N = w.shape[1]
    return pl.pallas_call(
        _mod_matmul_kernel,
        grid=(B, S // tm, N // tn),
        in_specs=[
            pl.BlockSpec((1, tm, D), lambda b, i, j: (b, i, 0)),
            pl.BlockSpec((1, 1, D), lambda b, i, j: (b, 0, 0)),
            pl.BlockSpec((1, 1, D), lambda b, i, j: (b, 0, 0)),
            pl.BlockSpec((D, tn), lambda b, i, j: (0, j)),
        ],
        out_specs=pl.BlockSpec((1, tm, tn), lambda b, i, j: (b, i, j)),
        out_shape=jax.ShapeDtypeStruct((B, S, N), out_dtype),
        name="mod_matmul",
    )(x, sc[:, None, :], sh[:, None, :], w.astype(MM_DTYPE))


def _layer_norm(x, g, b):
    mu = jnp.mean(x, -1, keepdims=True)
    var = jnp.mean(jnp.square(x - mu), -1, keepdims=True)
    return (x - mu) * lax.rsqrt(var + LN_EPS) * g + b


def _t5_bucket(n):
    n = jnp.maximum(n, 0)
    max_exact = REL_BUCKETS // 2
    nf = jnp.maximum(n, 1).astype(jnp.float32)
    large = max_exact + (jnp.log(nf / max_exact) / math.log(REL_MAX_DIST / max_exact)
                         * (REL_BUCKETS - max_exact)).astype(jnp.int32)
    large = jnp.minimum(large, REL_BUCKETS - 1)
    return jnp.where(n < max_exact, n, large)


RW_CHUNK = 64
RW_INV_BLOCK = 16
_NN = (((1,), (0,)), ((), ()))
_NT = (((1,), (1,)), ((), ()))
_TN = (((0,), (0,)), ((), ()))


def _dot_f32(a, b, dims=_NN):
    return lax.dot_general(a, b, dims, precision=lax.Precision.HIGHEST,
                           preferred_element_type=jnp.float32)


def _dot_bf16x3(a, b, dims=_NN):
    f32, bf = jnp.float32, jnp.bfloat16
    a_hi, b_hi = a.astype(bf), b.astype(bf)
    a_lo = (a - a_hi.astype(f32)).astype(bf)
    b_lo = (b - b_hi.astype(f32)).astype(bf)
    out = lax.dot_general(a_hi, b_hi, dims, preferred_element_type=f32)
    out = out + lax.dot_general(a_hi, b_lo, dims, preferred_element_type=f32)
    return out + lax.dot_general(a_lo, b_hi, dims, preferred_element_type=f32)


def _rwkv_kernel(z_ref, mu_ref, w0_ref, w2_ref, a0_ref, a2_ref, g2_ref, kk_ref, ka_ref, rk_ref,
                 gng_ref, gnb_ref, bd_ref, o_ref, m_ref, prev_ref, y_ref):
    C = z_ref.shape[1]
    N = RW_HEAD_DIM
    f32 = jnp.float32
    dot3 = _dot_bf16x3

    @pl.when(pl.program_id(1) == 0)
    def _():
        m_ref[...] = jnp.zeros(m_ref.shape, f32)
        prev_ref[...] = jnp.zeros(prev_ref.shape, f32)

    z = z_ref[0]
    row = lax.broadcasted_iota(jnp.int32, z.shape, 0)
    shifted = jnp.where(row == 0, prev_ref[...], pltpu.roll(z, 1, axis=0))
    prev_ref[...] = z[C - 1:C, :]
    zs = z + (shifted - z) * mu_ref[...]
    r = zs[:, 0:RW_DIM]
    k = zs[:, RW_DIM:2 * RW_DIM]
    v = zs[:, 2 * RW_DIM:3 * RW_DIM]
    o1 = 3 * RW_DIM
    wl = zs[:, o1:o1 + RW_DECAY_LORA]
    al = zs[:, o1 + RW_DECAY_LORA:o1 + RW_DECAY_LORA + RW_A_LORA]
    gl = zs[:, o1 + RW_DECAY_LORA + RW_A_LORA:]

    bd = bd_ref[...]
    log_w = -jax.nn.softplus(-(w0_ref[...] + _dot_f32(jnp.tanh(wl), w2_ref[...]))) - 0.5
    ldec = -jnp.exp(log_w)
    a_lr = jax.nn.sigmoid(a0_ref[...] + _dot_f32(al, a2_ref[...]))
    g = _dot_f32(jax.nn.sigmoid(gl), g2_ref[...])
    kk = k * kk_ref[...]
    kk = kk * lax.rsqrt(jnp.maximum(_dot_f32(kk * kk, bd), 1e-24))
    k2 = k * (1.0 + (a_lr - 1.0) * ka_ref[...])
    a_vec = -kk
    b_vec = kk * a_lr

    ti = lax.broadcasted_iota(jnp.int32, (C, C), 0)
    tj = lax.broadcasted_iota(jnp.int32, (C, C), 1)
    cum = _dot_f32((ti >= tj).astype(f32), ldec)
    cum_last = cum[C - 1:C, :]
    w_incl = jnp.exp(cum)
    w_excl = jnp.exp(cum - ldec)
    w_inv = jnp.exp(-cum)
    w_end = jnp.exp(cum_last - cum)
    w_all = jnp.exp(cum_last)
    a_t = a_vec * w_excl
    r_t = r * w_incl
    b_t = b_vec * w_inv
    k_t = k2 * w_inv
    b_e = b_vec * w_end
    k_e = k2 * w_end

    strict = ti > tj
    incl = ti >= tj
    bi, bj = ti // RW_INV_BLOCK, tj // RW_INV_BLOCK
    same_blk = bi == bj
    pair_blk = jnp.logical_and((bi // 2) == (bj // 2), jnp.logical_not(same_blk))
    half_blk = (bi // 2) != (bj // 2)
    eye = (ti == tj).astype(f32)

    for h in range(RW_HEADS):
        sl = slice(h * N, (h + 1) * N)
        ar = jnp.concatenate([a_t[:, sl], r_t[:, sl]], axis=0)
        bk = jnp.concatenate([b_t[:, sl], k_t[:, sl]], axis=0)
        gm = dot3(ar, bk, _NT)
        a_ab = jnp.where(strict, gm[:C, :C], 0.0)
        a_ak = jnp.where(strict, gm[:C, C:], 0.0)
        a_rb = jnp.where(incl, gm[C:, :C], 0.0)
        a_rk = jnp.where(incl, gm[C:, C:], 0.0)
        d1 = jnp.where(same_blk, a_ab, 0.0)
        xinv = eye + d1
        d2 = dot3(d1, d1)
        xinv = xinv + dot3(xinv, d2)
        d4 = dot3(d2, d2)
        xinv = xinv + dot3(xinv, d4)
        d8 = dot3(d4, d4)
        xinv = xinv + dot3(xinv, d8)
        xinv = xinv + dot3(dot3(xinv, jnp.where(pair_blk, a_ab, 0.0)), xinv)
        xinv = xinv + dot3(dot3(xinv, jnp.where(half_blk, a_ab, 0.0)), xinv)

        vh = v[:, sl]
        av = dot3(jnp.concatenate([a_ak, a_rk], axis=0), vh)
        p = dot3(xinv, jnp.concatenate([a_t[:, sl], av[:C]], axis=1))
        qm = dot3(a_rb, p)
        q1 = r_t[:, sl] + qm[:, :N]
        q2 = qm[:, N:] + av[C:]
        gmat = dot3(b_e[:, sl], p, _TN)
        g1 = eye * w_all[:, sl] + gmat[:, :N]
        g2 = gmat[:, N:] + dot3(k_e[:, sl], vh, _TN)
        m = m_ref[h]
        y_ref[:, sl] = _dot_f32(q1, m) + q2
        m_ref[h] = _dot_f32(g1, m) + g2

    y = y_ref[...]
    mean = _dot_f32(y, bd) * (1.0 / N)
    yc = y - mean
    var = _dot_f32(yc * yc, bd) * (1.0 / N)
    yn = yc * lax.rsqrt(var + RW_GN_EPS) * gng_ref[...] + gnb_ref[...]
    bonus = _dot_f32(r * k2 * rk_ref[...], bd) * v
    o_ref[0] = ((yn + bonus) * g).astype(o_ref.dtype)


def _rwkv7_time_mix(z_rw, mu, w0, w2, a0, a2, g2, k_k, k_a, r_k, gn_g, gn_b, out_dtype):
    B, S, _ = z_rw.shape
    C = RW_CHUNK
    assert S % C == 0 and C == RW_HEAD_DIM and C % (4 * RW_INV_BLOCK) == 0
    hid = jnp.arange(RW_DIM) // RW_HEAD_DIM
    bd = (hid[:, None] == hid[None, :]).astype(jnp.float32)
    row = lambda a: a.reshape(1, -1)
    full = lambda shape: pl.BlockSpec(shape, lambda b, c: (0,) * len(shape))
    return pl.pallas_call(
        _rwkv_kernel,
        grid=(B, S // C),
        in_specs=[
            pl.BlockSpec((1, C, RW_COLS), lambda b, c: (b, c, 0)),
            full((1, RW_COLS)), full((1, RW_DIM)), full((RW_DECAY_LORA, RW_DIM)), full((1, RW_DIM)),
            full((RW_A_LORA, RW_DIM)), full((RW_GATE_LORA, RW_DIM)), full((1, RW_DIM)), full((1, RW_DIM)),
            full((1, RW_DIM)), full((1, RW_DIM)), full((1, RW_DIM)), full((RW_DIM, RW_DIM)),
        ],
        out_specs=pl.BlockSpec((1, C, RW_DIM), lambda b, c: (b, c, 0)),
        out_shape=jax.ShapeDtypeStruct((B, S, RW_DIM), out_dtype),
        scratch_shapes=[
            pltpu.VMEM((RW_HEADS, RW_HEAD_DIM, RW_HEAD_DIM), jnp.float32),
            pltpu.VMEM((1, RW_COLS), jnp.float32),
            pltpu.VMEM((C, RW_DIM), jnp.float32),
        ],
        compiler_params=pltpu.CompilerParams(dimension_semantics=("parallel", "arbitrary")),
        name="rwkv7_time_mix",
    )(z_rw, row(mu), row(w0), w2, row(a0), a2, g2, row(k_k), row(k_a), row(r_k), row(gn_g), row(gn_b), bd)


DSA_TQ = 256
MASK_NEG = -1e30
INT_MIN = -2 ** 31
KEY_NEG_INF = -2139095041
THRESH_BITS = 32


def _dsa_prep_kernel(z_ref, kvg_ref, kig_ref, kib_ref, qi_ref, kv_ref, ki_ref):
    z = z_ref[0]
    qi_ref[0] = z[:, :IDX_HEADS * IDX_DIM].astype(qi_ref.dtype)
    kv = z[:, 256:384]
    ms = jnp.mean(jnp.square(kv), -1, keepdims=True)
    kv_ref[0] = (kv * lax.rsqrt(ms + LN_EPS) * kvg_ref[...]).astype(kv_ref.dtype)
    ki = z[:, 384:448]
    mu = jnp.mean(ki, -1, keepdims=True)
    var = jnp.mean(jnp.square(ki - mu), -1, keepdims=True)
    ki_ref[0] = ((ki - mu) * lax.rsqrt(var + LN_EPS) * kig_ref[...] + kib_ref[...]).astype(ki_ref.dtype)


def _dsa_prep(z_small, kv_g, ki_g, ki_b, tm=512):
    B, S, W = z_small.shape
    return pl.pallas_call(
        _dsa_prep_kernel,
        grid=(B, S // tm),
        in_specs=[
            pl.BlockSpec((1, tm, W), lambda b, i: (b, i, 0)),
            pl.BlockSpec((1, DSA_LATENT), lambda b, i: (0, 0)),
            pl.BlockSpec((1, IDX_DIM), lambda b, i: (0, 0)),
            pl.BlockSpec((1, IDX_DIM), lambda b, i: (0, 0)),
        ],
        out_specs=[
            pl.BlockSpec((1, tm, IDX_HEADS * IDX_DIM), lambda b, i: (b, i, 0)),
            pl.BlockSpec((1, tm, DSA_LATENT), lambda b, i: (b, i, 0)),
            pl.BlockSpec((1, tm, IDX_DIM), lambda b, i: (b, i, 0)),
        ],
        out_shape=[
            jax.ShapeDtypeStruct((B, S, IDX_HEADS * IDX_DIM), MM_DTYPE),
            jax.ShapeDtypeStruct((B, S, DSA_LATENT), MM_DTYPE),
            jax.ShapeDtypeStruct((B, S, IDX_DIM), MM_DTYPE),
        ],
        name="dsa_prep",
    )(z_small, kv_g[None], ki_g[None], ki_b[None])


def _sortable_key(s):
    s = jnp.where(s == 0.0, 0.0, s)
    bits = pltpu.bitcast(s, jnp.int32)
    return bits ^ ((bits >> 31) & 0x7FFFFFFF)


def _col_count(mask_i32):
    tk, tq = mask_i32.shape
    return jnp.sum(mask_i32.reshape(tk // 8, 8, tq), axis=0)


def _dsa_kernel(bfar_ref, q_ref, qi_ref, wit_ref, kv_ref, kvt_ref, ki_ref, b0_ref, b1_ref, o_ref,
                key_ref, madd_ref, m_ref, l_ref, acc_ref, *, topk):
    tq = q_ref.shape[1]
    tk = tq
    i = pl.program_id(1)
    nj = i + 1
    f32 = jnp.float32
    krow = lax.broadcasted_iota(jnp.int32, (tk, tq), 0)
    qcol = lax.broadcasted_iota(jnp.int32, (tk, tq), 1)

    qi = qi_ref[0]
    wit = wit_ref[0] * (IDX_HEADS ** -0.5)

    def score_chunk(j, carry):
        off = pl.multiple_of(j * tk, tk)
        kc = ki_ref[0, pl.ds(off, tk), :]
        s = jnp.zeros((tk, tq), f32)
        for h in range(IDX_HEADS):
            d = lax.dot_general(kc, qi[:, h * IDX_DIM:(h + 1) * IDX_DIM],
                                (((1,), (1,)), ((), ())), preferred_element_type=f32)
            s = s + wit[h:h + 1, :] * jnp.maximum(d * (IDX_DIM ** -0.5), 0.0)
        causal = (krow + j * tk) <= (qcol + i * tq)
        s = jnp.where(causal, s, -jnp.inf)
        key_ref[j] = _sortable_key(s)
        return carry

    lax.fori_loop(0, nj, score_chunk, 0)

    def count_where(pred_fn):
        def body(j, acc):
            return acc + _col_count(pred_fn(key_ref[j], j).astype(jnp.int32))
        acc = lax.fori_loop(0, nj, body, jnp.zeros((8, tq), jnp.int32))
        return jnp.sum(acc, axis=0, keepdims=True)

    def bit_step(it, t_u):
        bit = THRESH_BITS - 1 - it
        cand_u = t_u | jnp.left_shift(jnp.int32(1), bit)
        cand = cand_u ^ INT_MIN
        cnt = count_where(lambda k, j: k >= cand)
        return jnp.where(cnt >= topk, cand_u, t_u)

    t_u = lax.fori_loop(0, THRESH_BITS, bit_step, jnp.zeros((1, tq), jnp.int32))
    thr = t_u ^ INT_MIN
    cnt_gt = count_where(lambda k, j: k > thr)
    cnt_ge = count_where(lambda k, j: k >= thr)
    is_neg = thr == KEY_NEG_INF
    need = jnp.logical_and(cnt_ge > topk, jnp.logical_not(is_neg))
    n_tie_take = topk - cnt_gt

    thr_open = jnp.where(is_neg, thr, thr - 1)
    any_need = jnp.max(need.astype(jnp.int32)) > 0

    @pl.when(jnp.logical_not(any_need))
    def _():
        def body(j, carry):
            madd_ref[j] = jnp.where(key_ref[j] > thr_open, 0.0, MASK_NEG)
            return carry
        lax.fori_loop(0, nj, body, 0)

    @pl.when(any_need)
    def _():
        s_len = tk * key_ref.shape[0]
        n_bits = max(1, int(math.ceil(math.log2(s_len))))

        def idx_step(it, p):
            bit = n_bits - 1 - it
            cand = p | jnp.left_shift(jnp.int32(1), bit)
            cnt = count_where(
                lambda k, j: jnp.where(k == thr, jnp.where((krow + j * tk) < cand, 1, 0), 0))
            return jnp.where(cnt < n_tie_take, cand, p)

        p_idx = lax.fori_loop(0, n_bits, idx_step, jnp.zeros((1, tq), jnp.int32))
        p_idx = jnp.where(need, p_idx, jnp.where(is_neg, -1, s_len))

        def body(j, carry):
            k = key_ref[j]
            tie_ok = jnp.where((krow + j * tk) <= p_idx, 0.0, MASK_NEG)
            madd_ref[j] = jnp.where(k > thr, 0.0, jnp.where(k == thr, tie_ok, MASK_NEG))
            return carry
        lax.fori_loop(0, nj, body, 0)

    m_ref[...] = jnp.full(m_ref.shape, MASK_NEG, f32)
    l_ref[...] = jnp.zeros(l_ref.shape, f32)
    acc_ref[...] = jnp.zeros(acc_ref.shape, f32)
    scale = DSA_LATENT ** -0.5

    def attend(j, bias_fn):
        off = pl.multiple_of(j * tk, tk)
        kc = kv_ref[0, pl.ds(off, tk), :]
        kct = kvt_ref[0, :, pl.ds(off, tk)]
        madd = madd_ref[j]
        for h in range(DSA_HEADS):
            qh = q_ref[0, :, h * DSA_LATENT:(h + 1) * DSA_LATENT]
            lg = lax.dot_general(kc, qh, (((1,), (1,)), ((), ())), preferred_element_type=f32)
            lg = lg * scale + bias_fn(h) + madd
            m_old = m_ref[h]
            m_new = jnp.maximum(m_old, jnp.max(lg, axis=0, keepdims=True))
            alpha = jnp.exp(m_old - m_new)
            p = jnp.exp(lg - m_new)
            l_ref[h] = alpha * l_ref[h] + jnp.sum(p, axis=0, keepdims=True)
            acc_ref[h] = alpha * acc_ref[h] + jnp.dot(kct, p.astype(kct.dtype),
                                                      preferred_element_type=f32)
            m_ref[h] = m_new

    def far_body(j, carry):
        attend(j, lambda h: bfar_ref[h])
        return carry

    lax.fori_loop(0, jnp.maximum(i - 1, 0), far_body, 0)

    @pl.when(i >= 1)
    def _():
        attend(i - 1, lambda h: b1_ref[h])

    attend(i, lambda h: b0_ref[h])

    for h in range(DSA_HEADS):
        out_t = acc_ref[h] / l_ref[h]
        o_ref[0, :, h * DSA_LATENT:(h + 1) * DSA_LATENT] = out_t.T.astype(o_ref.dtype)


def _dsa_bias_tiles(rel_bias, tq):
    dist = jnp.arange(2 * tq, dtype=jnp.int32)
    by_dist = rel_bias[_t5_bucket(dist)].T
    kk = jnp.arange(tq, dtype=jnp.int32)[:, None]
    qq = jnp.arange(tq, dtype=jnp.int32)[None, :]
    b0 = by_dist[:, jnp.maximum(qq - kk, 0)]
    b1 = by_dist[:, tq + qq - kk]
    return b0, b1, by_dist[:, -1]


def _dsa_attention(q, z_small, z_wi_t, kv_g, idx_k_g, idx_k_b, rel_bias, out_dtype):
    B, S, _ = q.shape
    tq = DSA_TQ
    topk = min(TOPK_MAX, S // 4)
    assert S % tq == 0 and tq >= REL_MAX_DIST and topk <= tq
    qi, ckv, kidx = _dsa_prep(z_small, kv_g, idx_k_g, idx_k_b)
    ckv_t = jnp.swapaxes(ckv, 1, 2)
    b0, b1, bfar = _dsa_bias_tiles(rel_bias, tq)
    nq = S // tq
    H, d = DSA_HEADS, DSA_LATENT
    grid_spec = pltpu.PrefetchScalarGridSpec(
        num_scalar_prefetch=0,
        grid=(B, nq),
        in_specs=[
            pl.BlockSpec(memory_space=pltpu.SMEM),
            pl.BlockSpec((1, tq, H * d), lambda b, i: (b, i, 0)),
            pl.BlockSpec((1, tq, IDX_HEADS * IDX_DIM), lambda b, i: (b, i, 0)),
            pl.BlockSpec((1, IDX_HEADS, tq), lambda b, i: (b, 0, i)),
            pl.BlockSpec((1, S, d), lambda b, i: (b, 0, 0)),
            pl.BlockSpec((1, d, S), lambda b, i: (b, 0, 0)),
            pl.BlockSpec((1, S, IDX_DIM), lambda b, i: (b, 0, 0)),
            pl.BlockSpec((H, tq, tq), lambda b, i: (0, 0, 0)),
            pl.BlockSpec((H, tq, tq), lambda b, i: (0, 0, 0)),
        ],
        out_specs=pl.BlockSpec((1, tq, H * d), lambda b, i: (b, i, 0)),
        scratch_shapes=[
            pltpu.VMEM((nq, tq, tq), jnp.int32),
            pltpu.VMEM((nq, tq, tq), jnp.float32),
            pltpu.VMEM((H, 1, tq), jnp.float32),
            pltpu.VMEM((H, 1, tq), jnp.float32),
            pltpu.VMEM((H, d, tq), jnp.float32),
        ],
    )
    return pl.pallas_call(
        partial(_dsa_kernel, topk=topk),
        grid_spec=grid_spec,
        out_shape=jax.ShapeDtypeStruct((B, S, H * d), out_dtype),
        compiler_params=pltpu.CompilerParams(vmem_limit_bytes=48 * 1024 * 1024),
        name="dsa_attention",
    )(bfar, q, qi, z_wi_t, ckv, ckv_t, kidx, b0, b1)


SC_CORES = 2
SC_SUBCORES = 16
SC_LANES = 16
PEER_SC_TOKENS = 8
PEER_SC_RING = 4
PEER_SC_UNROLL = 4
PEER_SLOTS = PEER_HEADS * PEER_TOPK


def _peer_sc_call(body, T, out_width, stage_width):
    mesh = plsc.VectorSubcoreMesh(core_axis_name="c", subcore_axis_name="s")
    return pl.kernel(
        body, mesh=mesh,
        out_type=jax.ShapeDtypeStruct((T, out_width), jnp.float32),
        scratch_types=[
            pltpu.VMEM((PEER_SC_TOKENS, PEER_SLOTS), jnp.int32),
            pltpu.VMEM((PEER_SC_TOKENS, stage_width), jnp.float32),
            pltpu.VMEM((PEER_SC_RING, PEER_TOPK, D_MODEL), jnp.float32),
            pltpu.VMEM((PEER_SC_TOKENS, out_width), jnp.float32),
            pltpu.SemaphoreType.DMA((PEER_SC_RING,)),
        ],
        compiler_params=pltpu.CompilerParams(needs_layout_passes=False),
    )


def _peer_sc_body(compute, zero_out, x_hbm, idx_hbm, tab_hbm, out_hbm, idx_v, x_v, rows_v, out_v, sems):
    T = idx_hbm.shape[0]
    tokens_per_worker = T // (SC_CORES * SC_SUBCORES)
    n_steps = PEER_SC_TOKENS * PEER_HEADS
    worker = lax.axis_index("s") * SC_CORES + lax.axis_index("c")
    base = worker * tokens_per_worker

    def gather(s, b):
        ids = idx_v[s // PEER_HEADS, pl.ds((s % PEER_HEADS) * PEER_TOPK, PEER_TOPK)]
        return pltpu.make_async_copy(tab_hbm.at[ids], rows_v.at[b], sems.at[b])

    @pl.loop(0, tokens_per_worker // PEER_SC_TOKENS)
    def _(blk):
        tok0 = base + blk * PEER_SC_TOKENS
        pltpu.sync_copy(idx_hbm.at[pl.ds(tok0, PEER_SC_TOKENS)], idx_v)
        pltpu.sync_copy(x_hbm.at[pl.ds(tok0, PEER_SC_TOKENS)], x_v)
        for b in range(PEER_SC_RING - 1):
            gather(b, b).start()
        if zero_out:
            @pl.loop(0, PEER_SC_TOKENS)
            def _(t):
                @plsc.parallel_loop(0, out_v.shape[1] // SC_LANES, unroll=PEER_SC_UNROLL)
                def _(c):
                    out_v[t, pl.ds(pl.multiple_of(c * SC_LANES, SC_LANES), SC_LANES)] = (
                        jnp.zeros((SC_LANES,), jnp.float32))

        @pl.loop(0, n_steps, step=PEER_SC_RING)
        def _(s0):
            for b in range(PEER_SC_RING):
                s = s0 + b
                gather(s, b).wait()

                @pl.when(s + PEER_SC_RING - 1 < n_steps)
                def _():
                    gather(s + PEER_SC_RING - 1, (b + PEER_SC_RING - 1) % PEER_SC_RING).start()

                compute(s // PEER_HEADS, s % PEER_HEADS, b, x_v, rows_v, out_v)

        pltpu.sync_copy(out_v, out_hbm.at[pl.ds(tok0, PEER_SC_TOKENS)])


def _peer_dots_compute(t, hd, b, h_v, rows_v, dots_v):
    lane = lax.iota(jnp.int32, SC_LANES)

    def col_step(c, accs):
        off = pl.multiple_of(c * SC_LANES, SC_LANES)
        hv = h_v[t, pl.ds(off, SC_LANES)]
        return tuple(accs[r] + rows_v[b, r, pl.ds(off, SC_LANES)] * hv for r in range(PEER_TOPK))

    accs = plsc.parallel_loop(
        0, D_MODEL // SC_LANES, unroll=PEER_SC_UNROLL,
        carry=tuple(jnp.zeros((SC_LANES,), jnp.float32) for _ in range(PEER_TOPK)))(col_step)
    res = jnp.zeros((SC_LANES,), jnp.float32)
    for r in range(PEER_TOPK):
        res = jnp.where(lane == r, jnp.sum(accs[r]), res)
    dots_v[t, pl.ds(hd * PEER_TOPK, PEER_TOPK)] = res


def _peer_mix_compute(t, hd, b, act_v, rows_v, out_v):
    tvec = jnp.full((SC_LANES,), t, jnp.int32)
    weights = [plsc.load_gather(act_v, [tvec, jnp.full((SC_LANES,), hd * PEER_TOPK + r, jnp.int32)])
               for r in range(PEER_TOPK)]

    @plsc.parallel_loop(0, D_MODEL // SC_LANES, unroll=PEER_SC_UNROLL)
    def _(c):
        off = pl.multiple_of(c * SC_LANES, SC_LANES)
        a = rows_v[b, 0, pl.ds(off, SC_LANES)] * weights[0]
        for r in range(1, PEER_TOPK):
            a = a + rows_v[b, r, pl.ds(off, SC_LANES)] * weights[r]
        plsc.addupdate(out_v.at[t, pl.ds(off, SC_LANES)], a)


def _peer_expert_dots(h, experts, u_tab):
    T = h.shape[0]
    assert T % (SC_CORES * SC_SUBCORES * PEER_SC_TOKENS) == 0
    assert (PEER_SC_TOKENS * PEER_HEADS) % PEER_SC_RING == 0 and PEER_TOPK == SC_LANES
    body = partial(_peer_sc_body, _peer_dots_compute, False)
    return _peer_sc_call(body, T, PEER_SLOTS, D_MODEL)(h, experts, u_tab)


def _peer_expert_mix(act, experts, v_tab):
    T = act.shape[0]
    assert T % (SC_CORES * SC_SUBCORES * PEER_SC_TOKENS) == 0
    body = partial(_peer_sc_body, _peer_mix_compute, True)
    return _peer_sc_call(body, T, D_MODEL, PEER_SLOTS)(act, experts, v_tab)


def _peer(h, w_pq, sub_keys, u_tab, v_tab):
    B, S, D = h.shape
    T = B * S
    ht = h.reshape(T, D)
    q = (ht @ w_pq).reshape(T, PEER_HEADS, 2, PEER_HALF)
    s = jnp.einsum('thcd,hcnd->thcn', q, sub_keys)
    s1, i1 = lax.top_k(s[:, :, 0], PEER_TOPK)
    s2, i2 = lax.top_k(s[:, :, 1], PEER_TOPK)
    cand = (s1[..., :, None] + s2[..., None, :]).reshape(T, PEER_HEADS, PEER_TOPK * PEER_TOPK)
    cand_idx = (i1[..., :, None] * PEER_N_KEYS + i2[..., None, :]).reshape(T, PEER_HEADS, PEER_TOPK * PEER_TOPK)
    top_s, pos = lax.top_k(cand, PEER_TOPK)
    experts = jnp.take_along_axis(cand_idx, pos, axis=-1)
    gates = jax.nn.softmax(top_s, axis=-1)
    experts = experts.reshape(T, PEER_HEADS * PEER_TOPK)
    pre = _peer_expert_dots(ht, experts, u_tab)
    act = jax.nn.gelu(pre, approximate=False) * gates.reshape(T, PEER_HEADS * PEER_TOPK)
    return _peer_expert_mix(act, experts, v_tab).reshape(B, S, D)


def kernel(x, c, w_ada, b_ada, w_in, rw_mu, rw_w0, rw_w2, rw_a0, rw_a2, rw_g2, rw_k_k, rw_k_a, rw_r_k, rw_gn_g, rw_gn_b, dsa_kv_g, idx_k_g, idx_k_b, rel_bias, w_br_a, w_br_b, w_out, ln1_g, ln1_b, peer_wq, peer_keys, peer_u, peer_v, ln2_g, ln2_b):
    l = 0
    mod = jax.nn.silu(c) @ w_ada[l] + b_ada[l]
    sh1, sc1, gt1, sh2, sc2, gt2 = jnp.split(mod, 6, axis=-1)

    w_rw, w_q, w_kv, w_qi, w_ki, w_wi, w_ga, w_gb = jnp.split(w_in[l], _split_points(IN_SIZES), axis=-1)
    small_pad = jnp.zeros((D_MODEL, 512 - 256 - 128 - 64 - 4), w_in.dtype)
    w_small = jnp.concatenate([w_qi, w_kv, w_ki, w_wi, small_pad], axis=-1)
    z_rw = _mod_matmul(x, sc1, sh1, w_rw, jnp.float32, tn=896)
    z_q = _mod_matmul(x, sc1, sh1, w_q, MM_DTYPE, tn=1024)
    z_small = _mod_matmul(x, sc1, sh1, w_small, jnp.float32, tn=512)
    z_g = _mod_matmul(x, sc1, sh1, jnp.concatenate([w_ga, w_gb], axis=-1), jnp.float32, tn=1024)
    z_ga, z_gb = z_g[..., :D_MODEL], z_g[..., D_MODEL:]
    z_wi_t = jnp.swapaxes(z_small[..., 448:448 + IDX_HEADS], 1, 2)

    y_a = _rwkv7_time_mix(z_rw, rw_mu[l], rw_w0[l], rw_w2[l], rw_a0[l], rw_a2[l], rw_g2[l],
                          rw_k_k[l], rw_k_a[l], rw_r_k[l], rw_gn_g[l], rw_gn_b[l], jnp.float32) @ w_br_a[l]
    y_b = _dsa_attention(z_q, z_small, z_wi_t, dsa_kv_g[l], idx_k_g[l], idx_k_b[l],
                         rel_bias, jnp.float32) @ w_br_b[l]
    merged = jax.nn.sigmoid(z_ga) * y_a + jax.nn.sigmoid(z_gb) * y_b
    mix = merged @ w_out[l]
    x = _layer_norm(DEEPNORM_ALPHA * x + gt1[:, None] * mix, ln1_g[l], ln1_b[l])

    h2 = x * (1 + sc2[:, None]) + sh2[:, None]
    y2 = _peer(h2, peer_wq[l], peer_keys[l], peer_u[l], peer_v[l])
    x = _layer_norm(DEEPNORM_ALPHA * x + gt2[:, None] * y2, ln2_g[l], ln2_b[l])
    return x
```

```python
import math
from functools import partial

import jax
import jax.numpy as jnp
import numpy as np
from jax import lax
from jax.experimental import pallas as pl
from jax.experimental.pallas import tpu as pltpu
from jax.experimental.pallas import tpu_sc as plsc

D_MODEL = 1024
RW_HEADS = 8
RW_HEAD_DIM = 64
RW_DIM = 512
RW_DECAY_LORA = 64
RW_A_LORA = 64
RW_GATE_LORA = 128
RW_SIZES = (RW_DIM, RW_DIM, RW_DIM, RW_DECAY_LORA, RW_A_LORA, RW_GATE_LORA)
RW_COLS = 3 * RW_DIM + RW_DECAY_LORA + RW_A_LORA + RW_GATE_LORA
RW_GN_EPS = 64e-5
DSA_HEADS = 8
DSA_LATENT = 128
DSA_Q_DIM = DSA_HEADS * DSA_LATENT
IDX_HEADS = 4
IDX_DIM = 64
TOPK_MAX = 256
Q_BLOCK = 128
REL_BUCKETS = 32
REL_MAX_DIST = 128
IN_SIZES = (RW_COLS, DSA_Q_DIM, DSA_LATENT, IDX_HEADS * IDX_DIM, IDX_DIM, IDX_HEADS, D_MODEL, D_MODEL)
IN_COLS = sum(IN_SIZES)
PEER_HEADS = 8
PEER_N_KEYS = 128
PEER_KEY_DIM = 128
PEER_HALF = 64
PEER_TOPK = 16
PEER_CHUNK = 128
LN_EPS = 1e-5
DEPTH = 1
DEEPNORM_ALPHA = (2.0 * DEPTH) ** 0.25

LANES = 128
MM_DTYPE = jnp.bfloat16


def _split_points(sizes):
    return np.cumsum(sizes)[:-1].tolist()


def _mod_matmul_kernel(x_ref, sc_ref, sh_ref, w_ref, o_ref):
    h = x_ref[0] * (1.0 + sc_ref[0]) + sh_ref[0]
    o_ref[0] = jnp.dot(h.astype(w_ref.dtype), w_ref[...],
                       preferred_element_type=jnp.float32).astype(o_ref.dtype)


def _mod_matmul(x, sc, sh, w, out_dtype, tn, tm=512):
    B, S, D = x.shape
    N = w.shape[1]
    return pl.pallas_call(
        _mod_matmul_kernel,
        grid=(B, S // tm, N // tn),
        in_specs=[
            pl.BlockSpec((1, tm, D), lambda b, i, j: (b, i, 0)),
            pl.BlockSpec((1, 1, D), lambda b, i, j: (b, 0, 0)),
            pl.BlockSpec((1, 1, D), lambda b, i, j: (b, 0, 0)),
            pl.BlockSpec((D, tn), lambda b, i, j: (0, j)),
        ],
        out_specs=pl.BlockSpec((1, tm, tn), lambda b, i, j: (b, i, j)),
        out_shape=jax.ShapeDtypeStruct((B, S, N), out_dtype),
        name="mod_matmul",
    )(x, sc[:, None, :], sh[:, None, :], w.astype(MM_DTYPE))


def _layer_norm(x, g, b):
    mu = jnp.mean(x, -1, keepdims=True)
    var = jnp.mean(jnp.square(x - mu), -1, keepdims=True)
    return (x - mu) * lax.rsqrt(var + LN_EPS) * g + b


def _t5_bucket(n):
    n = jnp.maximum(n, 0)
    max_exact = REL_BUCKETS // 2
    nf = jnp.maximum(n, 1).astype(jnp.float32)
    large = max_exact + (jnp.log(nf / max_exact) / math.log(REL_MAX_DIST / max_exact)
                         * (REL_BUCKETS - max_exact)).astype(jnp.int32)
    large = jnp.minimum(large, REL_BUCKETS - 1)
    return jnp.where(n < max_exact, n, large)


RW_CHUNK = 64
RW_INV_BLOCK = 16
_NN = (((1,), (0,)), ((), ()))
_NT = (((1,), (1,)), ((), ()))
_TN = (((0,), (0,)), ((), ()))


def _dot_f32(a, b, dims=_NN):
    return lax.dot_general(a, b, dims, precision=lax.Precision.HIGHEST,
                           preferred_element_type=jnp.float32)


def _dot_bf16x3(a, b, dims=_NN):
    f32, bf = jnp.float32, jnp.bfloat16
    a_hi, b_hi = a.astype(bf), b.astype(bf)
    a_lo = (a - a_hi.astype(f32)).astype(bf)
    b_lo = (b - b_hi.astype(f32)).astype(bf)
    out = lax.dot_general(a_hi, b_hi, dims, preferred_element_type=f32)
    out = out + lax.dot_general(a_hi, b_lo, dims, preferred_element_type=f32)
    return out + lax.dot_general(a_lo, b_hi, dims, preferred_element_type=f32)


def _rwkv_kernel(z_ref, mu_ref, w0_ref, w2_ref, a0_ref, a2_ref, g2_ref, kk_ref, ka_ref, rk_ref,
                 gng_ref, gnb_ref, bd_ref, o_ref, m_ref, prev_ref, y_ref):
    C = z_ref.shape[1]
    N = RW_HEAD_DIM
    f32 = jnp.float32
    dot3 = _dot_bf16x3

    @pl.when(pl.program_id(1) == 0)
    def _():
        m_ref[...] = jnp.zeros(m_ref.shape, f32)
        prev_ref[...] = jnp.zeros(prev_ref.shape, f32)

    z = z_ref[0]
    row = lax.broadcasted_iota(jnp.int32, z.shape, 0)
    shifted = jnp.where(row == 0, prev_ref[...], pltpu.roll(z, 1, axis=0))
    prev_ref[...] = z[C - 1:C, :]
    zs = z + (shifted - z) * mu_ref[...]
    r = zs[:, 0:RW_DIM]
    k = zs[:, RW_DIM:2 * RW_DIM]
    v = zs[:, 2 * RW_DIM:3 * RW_DIM]
    o1 = 3 * RW_DIM
    wl = zs[:, o1:o1 + RW_DECAY_LORA]
    al = zs[:, o1 + RW_DECAY_LORA:o1 + RW_DECAY_LORA + RW_A_LORA]
    gl = zs[:, o1 + RW_DECAY_LORA + RW_A_LORA:]

    bd = bd_ref[...]
    log_w = -jax.nn.softplus(-(w0_ref[...] + _dot_f32(jnp.tanh(wl), w2_ref[...]))) - 0.5
    ldec = -jnp.exp(log_w)
    a_lr = jax.nn.sigmoid(a0_ref[...] + _dot_f32(al, a2_ref[...]))
    g = _dot_f32(jax.nn.sigmoid(gl), g2_ref[...])
    kk = k * kk_ref[...]
    kk = kk * lax.rsqrt(jnp.maximum(_dot_f32(kk * kk, bd), 1e-24))
    k2 = k * (1.0 + (a_lr - 1.0) * ka_ref[...])
    a_vec = -kk
    b_vec = kk * a_lr

    ti = lax.broadcasted_iota(jnp.int32, (C, C), 0)
    tj = lax.broadcasted_iota(jnp.int32, (C, C), 1)
    cum = _dot_f32((ti >= tj).astype(f32), ldec)
    cum_last = cum[C - 1:C, :]
    w_incl = jnp.exp(cum)
    w_excl = jnp.exp(cum - ldec)
    w_inv = jnp.exp(-cum)
    w_end = jnp.exp(cum_last - cum)
    w_all = jnp.exp(cum_last)
    a_t = a_vec * w_excl
    r_t = r * w_incl
    b_t = b_vec * w_inv
    k_t = k2 * w_inv
    b_e = b_vec * w_end
    k_e = k2 * w_end

    strict = ti > tj
    incl = ti >= tj
    bi, bj = ti // RW_INV_BLOCK, tj // RW_INV_BLOCK
    same_blk = bi == bj
    pair_blk = jnp.logical_and((bi // 2) == (bj // 2), jnp.logical_not(same_blk))
    half_blk = (bi // 2) != (bj // 2)
    eye = (ti == tj).astype(f32)

    for h in range(RW_HEADS):
        sl = slice(h * N, (h + 1) * N)
        ar = jnp.concatenate([a_t[:, sl], r_t[:, sl]], axis=0)
        bk = jnp.concatenate([b_t[:, sl], k_t[:, sl]], axis=0)
        gm = dot3(ar, bk, _NT)
        a_ab = jnp.where(strict, gm[:C, :C], 0.0)
        a_ak = jnp.where(strict, gm[:C, C:], 0.0)
        a_rb = jnp.where(incl, gm[C:, :C], 0.0)
        a_rk = jnp.where(incl, gm[C:, C:], 0.0)
        d1 = jnp.where(same_blk, a_ab, 0.0)
        xinv = eye + d1
        d2 = dot3(d1, d1)
        xinv = xinv + dot3(xinv, d2)
        d4 = dot3(d2, d2)
        xinv = xinv + dot3(xinv, d4)
        d8 = dot3(d4, d4)
        xinv = xinv + dot3(xinv, d8)
        xinv = xinv + dot3(dot3(xinv, jnp.where(pair_blk, a_ab, 0.0)), xinv)
        xinv = xinv + dot3(dot3(xinv, jnp.where(half_blk, a_ab, 0.0)), xinv)

        vh = v[:, sl]
        av = dot3(jnp.concatenate([a_ak, a_rk], axis=0), vh)
        p = dot3(xinv, jnp.concatenate([a_t[:, sl], av[:C]], axis=1))
        qm = dot3(a_rb, p)
        q1 = r_t[:, sl] + qm[:, :N]
        q2 = qm[:, N:] + av[C:]
        gmat = dot3(b_e[:, sl], p, _TN)
        g1 = eye * w_all[:, sl] + gmat[:, :N]
        g2 = gmat[:, N:] + dot3(k_e[:, sl], vh, _TN)
        m = m_ref[h]
        y_ref[:, sl] = _dot_f32(q1, m) + q2
        m_ref[h] = _dot_f32(g1, m) + g2

    y = y_ref[...]
    mean = _dot_f32(y, bd) * (1.0 / N)
    yc = y - mean
    var = _dot_f32(yc * yc, bd) * (1.0 / N)
    yn = yc * lax.rsqrt(var + RW_GN_EPS) * gng_ref[...] + gnb_ref[...]
    bonus = _dot_f32(r * k2 * rk_ref[...], bd) * v
    o_ref[0] = ((yn + bonus) * g).astype(o_ref.dtype)


def _rwkv7_time_mix(z_rw, mu, w0, w2, a0, a2, g2, k_k, k_a, r_k, gn_g, gn_b, out_dtype):
    B, S, _ = z_rw.shape
    C = RW_CHUNK
    assert S % C == 0 and C == RW_HEAD_DIM and C % (4 * RW_INV_BLOCK) == 0
    hid = jnp.arange(RW_DIM) // RW_HEAD_DIM
    bd = (hid[:, None] == hid[None, :]).astype(jnp.float32)
    row = lambda a: a.reshape(1, -1)
    full = lambda shape: pl.BlockSpec(shape, lambda b, c: (0,) * len(shape))
    return pl.pallas_call(
        _rwkv_kernel,
        grid=(B, S // C),
        in_specs=[
            pl.BlockSpec((1, C, RW_COLS), lambda b, c: (b, c, 0)),
            full((1, RW_COLS)), full((1, RW_DIM)), full((RW_DECAY_LORA, RW_DIM)), full((1, RW_DIM)),
            full((RW_A_LORA, RW_DIM)), full((RW_GATE_LORA, RW_DIM)), full((1, RW_DIM)), full((1, RW_DIM)),
            full((1, RW_DIM)), full((1, RW_DIM)), full((1, RW_DIM)), full((RW_DIM, RW_DIM)),
        ],
        out_specs=pl.BlockSpec((1, C, RW_DIM), lambda b, c: (b, c, 0)),
        out_shape=jax.ShapeDtypeStruct((B, S, RW_DIM), out_dtype),
        scratch_shapes=[
            pltpu.VMEM((RW_HEADS, RW_HEAD_DIM, RW_HEAD_DIM), jnp.float32),
            pltpu.VMEM((1, RW_COLS), jnp.float32),
            pltpu.VMEM((C, RW_DIM), jnp.float32),
        ],
        compiler_params=pltpu.CompilerParams(dimension_semantics=("parallel", "arbitrary")),
        name="rwkv7_time_mix",
    )(z_rw, row(mu), row(w0), w2, row(a0), a2, g2, row(k_k), row(k_a), row(r_k), row(gn_g), row(gn_b), bd)


DSA_TQ = 256
MASK_NEG = -1e30
INT_MIN = -2 ** 31
KEY_NEG_INF = -2139095041
THRESH_BITS = 32


def _dsa_prep_kernel(z_ref, kvg_ref, kig_ref, kib_ref, qi_ref, kv_ref, ki_ref):
    z = z_ref[0]
    qi_ref[0] = z[:, :IDX_HEADS * IDX_DIM].astype(qi_ref.dtype)
    kv = z[:, 256:384]
    ms = jnp.mean(jnp.square(kv), -1, keepdims=True)
    kv_ref[0] = (kv * lax.rsqrt(ms + LN_EPS) * kvg_ref[...]).astype(kv_ref.dtype)
    ki = z[:, 384:448]
    mu = jnp.mean(ki, -1, keepdims=True)
    var = jnp.mean(jnp.square(ki - mu), -1, keepdims=True)
    ki_ref[0] = ((ki - mu) * lax.rsqrt(var + LN_EPS) * kig_ref[...] + kib_ref[...]).astype(ki_ref.dtype)


def _dsa_prep(z_small, kv_g, ki_g, ki_b, tm=512):
    B, S, W = z_small.shape
    return pl.pallas_call(
        _dsa_prep_kernel,
        grid=(B, S // tm),
        in_specs=[
            pl.BlockSpec((1, tm, W), lambda b, i: (b, i, 0)),
            pl.BlockSpec((1, DSA_LATENT), lambda b, i: (0, 0)),
            pl.BlockSpec((1, IDX_DIM), lambda b, i: (0, 0)),
            pl.BlockSpec((1, IDX_DIM), lambda b, i: (0, 0)),
        ],
        out_specs=[
            pl.BlockSpec((1, tm, IDX_HEADS * IDX_DIM), lambda b, i: (b, i, 0)),
            pl.BlockSpec((1, tm, DSA_LATENT), lambda b, i: (b, i, 0)),
            pl.BlockSpec((1, tm, IDX_DIM), lambda b, i: (b, i, 0)),
        ],
        out_shape=[
            jax.ShapeDtypeStruct((B, S, IDX_HEADS * IDX_DIM), MM_DTYPE),
            jax.ShapeDtypeStruct((B, S, DSA_LATENT), MM_DTYPE),
            jax.ShapeDtypeStruct((B, S, IDX_DIM), MM_DTYPE),
        ],
        name="dsa_prep",
    )(z_small, kv_g[None], ki_g[None], ki_b[None])


def _sortable_key(s):
    s = jnp.where(s == 0.0, 0.0, s)
    bits = pltpu.bitcast(s, jnp.int32)
    return bits ^ ((bits >> 31) & 0x7FFFFFFF)


def _col_count(mask_i32):
    tk, tq = mask_i32.shape
    return jnp.sum(mask_i32.reshape(tk // 8, 8, tq), axis=0)


def _dsa_kernel(bfar_ref, q_ref, qi_ref, wit_ref, kv_ref, kvt_ref, ki_ref, b0_ref, b1_ref, o_ref,
                key_ref, madd_ref, m_ref, l_ref, acc_ref, *, topk):
    tq = q_ref.shape[1]
    tk = tq
    i = pl.program_id(1)
    nj = i + 1
    f32 = jnp.float32
    krow = lax.broadcasted_iota(jnp.int32, (tk, tq), 0)
    qcol = lax.broadcasted_iota(jnp.int32, (tk, tq), 1)

    qi = qi_ref[0]
    wit = wit_ref[0] * (IDX_HEADS ** -0.5)

    def score_chunk(j, carry):
        off = pl.multiple_of(j * tk, tk)
        kc = ki_ref[0, pl.ds(off, tk), :]
        s = jnp.zeros((tk, tq), f32)
        for h in range(IDX_HEADS):
            d = lax.dot_general(kc, qi[:, h * IDX_DIM:(h + 1) * IDX_DIM],
                                (((1,), (1,)), ((), ())), preferred_element_type=f32)
            s = s + wit[h:h + 1, :] * jnp.maximum(d * (IDX_DIM ** -0.5), 0.0)
        causal = (krow + j * tk) <= (qcol + i * tq)
        s = jnp.where(causal, s, -jnp.inf)
        key_ref[j] = _sortable_key(s)
        return carry

    lax.fori_loop(0, nj, score_chunk, 0)

    def count_where(pred_fn):
        def body(j, acc):
            return acc + _col_count(pred_fn(key_ref[j], j).astype(jnp.int32))
        acc = lax.fori_loop(0, nj, body, jnp.zeros((8, tq), jnp.int32))
        return jnp.sum(acc, axis=0, keepdims=True)

    def bit_step(it, t_u):
        bit = THRESH_BITS - 1 - it
        cand_u = t_u | jnp.left_shift(jnp.int32(1), bit)
        cand = cand_u ^ INT_MIN
        cnt = count_where(lambda k, j: k >= cand)
        return jnp.where(cnt >= topk, cand_u, t_u)

    t_u = lax.fori_loop(0, THRESH_BITS, bit_step, jnp.zeros((1, tq), jnp.int32))
    thr = t_u ^ INT_MIN
    cnt_gt = count_where(lambda k, j: k > thr)
    cnt_ge = count_where(lambda k, j: k >= thr)
    is_neg = thr == KEY_NEG_INF
    need = jnp.logical_and(cnt_ge > topk, jnp.logical_not(is_neg))
    n_tie_take = topk - cnt_gt

    thr_open = jnp.where(is_neg, thr, thr - 1)
    any_need = jnp.max(need.astype(jnp.int32)) > 0

    @pl.when(jnp.logical_not(any_need))
    def _():
        def body(j, carry):
            madd_ref[j] = jnp.where(key_ref[j] > thr_open, 0.0, MASK_NEG)
            return carry
        lax.fori_loop(0, nj, body, 0)

    @pl.when(any_need)
    def _():
        s_len = tk * key_ref.shape[0]
        n_bits = max(1, int(math.ceil(math.log2(s_len))))

        def idx_step(it, p):
            bit = n_bits - 1 - it
            cand = p | jnp.left_shift(jnp.int32(1), bit)
            cnt = count_where(
                lambda k, j: jnp.where(k == thr, jnp.where((krow + j * tk) < cand, 1, 0), 0))
            return jnp.where(cnt < n_tie_take, cand, p)

        p_idx = lax.fori_loop(0, n_bits, idx_step, jnp.zeros((1, tq), jnp.int32))
        p_idx = jnp.where(need, p_idx, jnp.where(is_neg, -1, s_len))

        def body(j, carry):
            k = key_ref[j]
            tie_ok = jnp.where((krow + j * tk) <= p_idx, 0.0, MASK_NEG)
            madd_ref[j] = jnp.where(k > thr, 0.0, jnp.where(k == thr, tie_ok, MASK_NEG))
            return carry
        lax.fori_loop(0, nj, body, 0)

    m_ref[...] = jnp.full(m_ref.shape, MASK_NEG, f32)
    l_ref[...] = jnp.zeros(l_ref.shape, f32)
    acc_ref[...] = jnp.zeros(acc_ref.shape, f32)
    scale = DSA_LATENT ** -0.5

    def attend(j, bias_fn):
        off = pl.multiple_of(j * tk, tk)
        kc = kv_ref[0, pl.ds(off, tk), :]
        kct = kvt_ref[0, :, pl.ds(off, tk)]
        madd = madd_ref[j]
        for h in range(DSA_HEADS):
            qh = q_ref[0, :, h * DSA_LATENT:(h + 1) * DSA_LATENT]
            lg = lax.dot_general(kc, qh, (((1,), (1,)), ((), ())), preferred_element_type=f32)
            lg = lg * scale + bias_fn(h) + madd
            m_old = m_ref[h]
            m_new = jnp.maximum(m_old, jnp.max(lg, axis=0, keepdims=True))
            alpha = jnp.exp(m_old - m_new)
            p = jnp.exp(lg - m_new)
            l_ref[h] = alpha * l_ref[h] + jnp.sum(p, axis=0, keepdims=True)
            acc_ref[h] = alpha * acc_ref[h] + jnp.dot(kct, p.astype(kct.dtype),
                                                      preferred_element_type=f32)
            m_ref[h] = m_new

    def far_body(j, carry):
        attend(j, lambda h: bfar_ref[h])
        return carry

    lax.fori_loop(0, jnp.maximum(i - 1, 0), far_body, 0)

    @pl.when(i >= 1)
    def _():
        attend(i - 1, lambda h: b1_ref[h])

    attend(i, lambda h: b0_ref[h])

    for h in range(DSA_HEADS):
        out_t = acc_ref[h] / l_ref[h]
        o_ref[0, :, h * DSA_LATENT:(h + 1) * DSA_LATENT] = out_t.T.astype(o_ref.dtype)


def _dsa_bias_tiles(rel_bias, tq):
    dist = jnp.arange(2 * tq, dtype=jnp.int32)
    by_dist = rel_bias[_t5_bucket(dist)].T
    kk = jnp.arange(tq, dtype=jnp.int32)[:, None]
    qq = jnp.arange(tq, dtype=jnp.int32)[None, :]
    b0 = by_dist[:, jnp.maximum(qq - kk, 0)]
    b1 = by_dist[:, tq + qq - kk]
    return b0, b1, by_dist[:, -1]


def _dsa_attention(q, z_small, z_wi_t, kv_g, idx_k_g, idx_k_b, rel_bias, out_dtype):
    B, S, _ = q.shape
    tq = DSA_TQ
    topk = min(TOPK_MAX, S // 4)
    assert S % tq == 0 and tq >= REL_MAX_DIST and topk <= tq
    qi, ckv, kidx = _dsa_prep(z_small, kv_g, idx_k_g, idx_k_b)
    ckv_t = jnp.swapaxes(ckv, 1, 2)
    b0, b1, bfar = _dsa_bias_tiles(rel_bias, tq)
    nq = S // tq
    H, d = DSA_HEADS, DSA_LATENT
    grid_spec = pltpu.PrefetchScalarGridSpec(
        num_scalar_prefetch=0,
        grid=(B, nq),
        in_specs=[
            pl.BlockSpec(memory_space=pltpu.SMEM),
            pl.BlockSpec((1, tq, H * d), lambda b, i: (b, i, 0)),
            pl.BlockSpec((1, tq, IDX_HEADS * IDX_DIM), lambda b, i: (b, i, 0)),
            pl.BlockSpec((1, IDX_HEADS, tq), lambda b, i: (b, 0, i)),
            pl.BlockSpec((1, S, d), lambda b, i: (b, 0, 0)),
            pl.BlockSpec((1, d, S), lambda b, i: (b, 0, 0)),
            pl.BlockSpec((1, S, IDX_DIM), lambda b, i: (b, 0, 0)),
            pl.BlockSpec((H, tq, tq), lambda b, i: (0, 0, 0)),
            pl.BlockSpec((H, tq, tq), lambda b, i: (0, 0, 0)),
        ],
        out_specs=pl.BlockSpec((1, tq, H * d), lambda b, i: (b, i, 0)),
        scratch_shapes=[
            pltpu.VMEM((nq, tq, tq), jnp.int32),
            pltpu.VMEM((nq, tq, tq), jnp.float32),
            pltpu.VMEM((H, 1, tq), jnp.float32),
            pltpu.VMEM((H, 1, tq), jnp.float32),
            pltpu.VMEM((H, d, tq), jnp.float32),
        ],
    )
    return pl.pallas_call(
        partial(_dsa_kernel, topk=topk),
        grid_spec=grid_spec,
        out_shape=jax.ShapeDtypeStruct((B, S, H * d), out_dtype),
        compiler_params=pltpu.CompilerParams(vmem_limit_bytes=48 * 1024 * 1024),
        name="dsa_attention",
    )(bfar, q, qi, z_wi_t, ckv, ckv_t, kidx, b0, b1)


SC_CORES = 2
SC_SUBCORES = 16
SC_LANES = 16
PEER_SC_TOKENS = 8
PEER_SC_RING = 4
PEER_SC_UNROLL = 4
PEER_SLOTS = PEER_HEADS * PEER_TOPK


def _peer_sc_call(body, T, out_width, stage_width):
    mesh = plsc.VectorSubcoreMesh(core_axis_name="c", subcore_axis_name="s")
    return pl.kernel(
        body, mesh=mesh,
        out_type=jax.ShapeDtypeStruct((T, out_width), jnp.float32),
        scratch_types=[
            pltpu.VMEM((PEER_SC_TOKENS, PEER_SLOTS), jnp.int32),
            pltpu.VMEM((PEER_SC_TOKENS, stage_width), jnp.float32),
            pltpu.VMEM((PEER_SC_RING, PEER_TOPK, D_MODEL), jnp.float32),
            pltpu.VMEM((PEER_SC_TOKENS, out_width), jnp.float32),
            pltpu.SemaphoreType.DMA((PEER_SC_RING,)),
        ],
        compiler_params=pltpu.CompilerParams(needs_layout_passes=False),
    )


def _peer_sc_body(compute, zero_out, x_hbm, idx_hbm, tab_hbm, out_hbm, idx_v, x_v, rows_v, out_v, sems):
    T = idx_hbm.shape[0]
    tokens_per_worker = T // (SC_CORES * SC_SUBCORES)
    n_steps = PEER_SC_TOKENS * PEER_HEADS
    worker = lax.axis_index("s") * SC_CORES + lax.axis_index("c")
    base = worker * tokens_per_worker

    def gather(s, b):
        ids = idx_v[s // PEER_HEADS, pl.ds((s % PEER_HEADS) * PEER_TOPK, PEER_TOPK)]
        return pltpu.make_async_copy(tab_hbm.at[ids], rows_v.at[b], sems.at[b])

    @pl.loop(0, tokens_per_worker // PEER_SC_TOKENS)
    def _(blk):
        tok0 = base + blk * PEER_SC_TOKENS
        pltpu.sync_copy(idx_hbm.at[pl.ds(tok0, PEER_SC_TOKENS)], idx_v)
        pltpu.sync_copy(x_hbm.at[pl.ds(tok0, PEER_SC_TOKENS)], x_v)
        for b in range(PEER_SC_RING - 1):
            gather(b, b).start()
        if zero_out:
            @pl.loop(0, PEER_SC_TOKENS)
            def _(t):
                @plsc.parallel_loop(0, out_v.shape[1] // SC_LANES, unroll=PEER_SC_UNROLL)
                def _(c):
                    out_v[t, pl.ds(pl.multiple_of(c * SC_LANES, SC_LANES), SC_LANES)] = (
                        jnp.zeros((SC_LANES,), jnp.float32))

        @pl.loop(0, n_steps, step=PEER_SC_RING)
        def _(s0):
            for b in range(PEER_SC_RING):
                s = s0 + b
                gather(s, b).wait()

                @pl.when(s + PEER_SC_RING - 1 < n_steps)
                def _():
                    gather(s + PEER_SC_RING - 1, (b + PEER_SC_RING - 1) % PEER_SC_RING).start()

                compute(s // PEER_HEADS, s % PEER_HEADS, b, x_v, rows_v, out_v)

        pltpu.sync_copy(out_v, out_hbm.at[pl.ds(tok0, PEER_SC_TOKENS)])


def _peer_dots_compute(t, hd, b, h_v, rows_v, dots_v):
    lane = lax.iota(jnp.int32, SC_LANES)

    def col_step(c, accs):
        off = pl.multiple_of(c * SC_LANES, SC_LANES)
        hv = h_v[t, pl.ds(off, SC_LANES)]
        return tuple(accs[r] + rows_v[b, r, pl.ds(off, SC_LANES)] * hv for r in range(PEER_TOPK))

    accs = plsc.parallel_loop(
        0, D_MODEL // SC_LANES, unroll=PEER_SC_UNROLL,
        carry=tuple(jnp.zeros((SC_LANES,), jnp.float32) for _ in range(PEER_TOPK)))(col_step)
    res = jnp.zeros((SC_LANES,), jnp.float32)
    for r in range(PEER_TOPK):
        res = jnp.where(lane == r, jnp.sum(accs[r]), res)
    dots_v[t, pl.ds(hd * PEER_TOPK, PEER_TOPK)] = res


def _peer_mix_compute(t, hd, b, act_v, rows_v, out_v):
    tvec = jnp.full((SC_LANES,), t, jnp.int32)
    weights = [plsc.load_gather(act_v, [tvec, jnp.full((SC_LANES,), hd * PEER_TOPK + r, jnp.int32)])
               for r in range(PEER_TOPK)]

    @plsc.parallel_loop(0, D_MODEL // SC_LANES, unroll=PEER_SC_UNROLL)
    def _(c):
        off = pl.multiple_of(c * SC_LANES, SC_LANES)
        a = rows_v[b, 0, pl.ds(off, SC_LANES)] * weights[0]
        for r in range(1, PEER_TOPK):
            a = a + rows_v[b, r, pl.ds(off, SC_LANES)] * weights[r]
        plsc.addupdate(out_v.at[t, pl.ds(off, SC_LANES)], a)


def _peer_expert_dots(h, experts, u_tab):
    T = h.shape[0]
    assert T % (SC_CORES * SC_SUBCORES * PEER_SC_TOKENS) == 0
    assert (PEER_SC_TOKENS * PEER_HEADS) % PEER_SC_RING == 0 and PEER_TOPK == SC_LANES
    body = partial(_peer_sc_body, _peer_dots_compute, False)
    return _peer_sc_call(body, T, PEER_SLOTS, D_MODEL)(h, experts, u_tab)


def _peer_expert_mix(act, experts, v_tab):
    T = act.shape[0]
    assert T % (SC_CORES * SC_SUBCORES * PEER_SC_TOKENS) == 0
    body = partial(_peer_sc_body, _peer_mix_compute, True)
    return _peer_sc_call(body, T, D_MODEL, PEER_SLOTS)(act, experts, v_tab)


PEER_TM = 256


def _extract_topk(s, k, payload=None):
    R = s.shape[0]
    riota = lax.broadcasted_iota(jnp.int32, s.shape, 0)
    vals, rows = [], []
    for _ in range(k):
        m = jnp.max(s, axis=0, keepdims=True)
        pos = jnp.min(jnp.where(s == m, riota, R), axis=0, keepdims=True)
        hit = riota == pos
        vals.append(m)
        rows.append(pos if payload is None else jnp.max(jnp.where(hit, payload, -1), axis=0, keepdims=True))
        s = jnp.where(hit, -jnp.inf, s)
    return jnp.concatenate(vals, axis=0), jnp.concatenate(rows, axis=0)


def _peer_route_kernel(x_ref, sc_ref, sh_ref, wq_ref, keys_ref, h_ref, ex_ref, gate_ref, q_ref, ext_ref, gt_ref):
    K = PEER_TOPK
    f32 = jnp.float32
    h = x_ref[0] * (1.0 + sc_ref[0]) + sh_ref[0]
    h_ref[0] = h
    q = jnp.dot(h.astype(wq_ref.dtype), wq_ref[...], preferred_element_type=f32)
    for hd in range(PEER_HEADS):
        q_ref[hd] = q[:, hd * PEER_KEY_DIM:(hd + 1) * PEER_KEY_DIM].astype(q_ref.dtype)

    def head_body(hd, carry):
        qh = q_ref[hd]
        s1 = lax.dot_general(keys_ref[hd, 0], qh[:, :PEER_HALF], _NT, preferred_element_type=f32)
        s2 = lax.dot_general(keys_ref[hd, 1], qh[:, PEER_HALF:], _NT, preferred_element_type=f32)
        v1, i1 = _extract_topk(s1, K)
        v2, i2 = _extract_topk(s2, K)
        tm = v1.shape[1]
        cand = (v1[:, None, :] + v2[None, :, :]).reshape(K * K, tm)
        cidx = (i1[:, None, :] * PEER_N_KEYS + i2[None, :, :]).reshape(K * K, tm)
        top_s, experts = _extract_topk(cand, K, payload=cidx)
        e = jnp.exp(top_s - top_s[0:1, :])
        gt_ref[pl.ds(pl.multiple_of(hd * K, K), K), :] = e / jnp.sum(e, axis=0, keepdims=True)
        ext_ref[pl.ds(pl.multiple_of(hd * K, K), K), :] = experts
        return carry

    lax.fori_loop(0, PEER_HEADS, head_body, 0)
    ex_ref[0] = ext_ref[...].T
    gate_ref[0] = gt_ref[...].T


def _peer_route(x, sc, sh, w_pq, sub_keys):
    B, S, D = x.shape
    tm = PEER_TM
    return pl.pallas_call(
        _peer_route_kernel,
        grid=(B, S // tm),
        in_specs=[
            pl.BlockSpec((1, tm, D), lambda b, i: (b, i, 0)),
            pl.BlockSpec((1, 1, D), lambda b, i: (b, 0, 0)),
            pl.BlockSpec((1, 1, D), lambda b, i: (b, 0, 0)),
            pl.BlockSpec((D, PEER_HEADS * PEER_KEY_DIM), lambda b, i: (0, 0)),
            pl.BlockSpec((PEER_HEADS, 2, PEER_N_KEYS, PEER_HALF), lambda b, i: (0, 0, 0, 0)),
        ],
        out_specs=[
            pl.BlockSpec((1, tm, D), lambda b, i: (b, i, 0)),
            pl.BlockSpec((1, tm, PEER_SLOTS), lambda b, i: (b, i, 0)),
            pl.BlockSpec((1, tm, PEER_SLOTS), lambda b, i: (b, i, 0)),
        ],
        out_shape=[
            jax.ShapeDtypeStruct((B, S, D), jnp.float32),
            jax.ShapeDtypeStruct((B, S, PEER_SLOTS), jnp.int32),
            jax.ShapeDtypeStruct((B, S, PEER_SLOTS), jnp.float32),
        ],
        scratch_shapes=[
            pltpu.VMEM((PEER_HEADS, tm, PEER_KEY_DIM), MM_DTYPE),
            pltpu.VMEM((PEER_SLOTS, tm), jnp.int32),
            pltpu.VMEM((PEER_SLOTS, tm), jnp.float32),
        ],
        compiler_params=pltpu.CompilerParams(dimension_semantics=("parallel", "parallel")),
        name="peer_route",
    )(x, sc[:, None, :], sh[:, None, :], w_pq.astype(MM_DTYPE), sub_keys.astype(MM_DTYPE))


def _peer(x, sc, sh, w_pq, sub_keys, u_tab, v_tab):
    B, S, D = x.shape
    T = B * S
    h2, experts, gates = _peer_route(x, sc, sh, w_pq, sub_keys)
    experts = experts.reshape(T, PEER_SLOTS)
    pre = _peer_expert_dots(h2.reshape(T, D), experts, u_tab)
    act = jax.nn.gelu(pre, approximate=False) * gates.reshape(T, PEER_SLOTS)
    return _peer_expert_mix(act, experts, v_tab).reshape(B, S, D)


def kernel(x, c, w_ada, b_ada, w_in, rw_mu, rw_w0, rw_w2, rw_a0, rw_a2, rw_g2, rw_k_k, rw_k_a, rw_r_k, rw_gn_g, rw_gn_b, dsa_kv_g, idx_k_g, idx_k_b, rel_bias, w_br_a, w_br_b, w_out, ln1_g, ln1_b, peer_wq, peer_keys, peer_u, peer_v, ln2_g, ln2_b):
    l = 0
    mod = jax.nn.silu(c) @ w_ada[l] + b_ada[l]
    sh1, sc1, gt1, sh2, sc2, gt2 = jnp.split(mod, 6, axis=-1)

    w_rw, w_q, w_kv, w_qi, w_ki, w_wi, w_ga, w_gb = jnp.split(w_in[l], _split_points(IN_SIZES), axis=-1)
    small_pad = jnp.zeros((D_MODEL, 512 - 256 - 128 - 64 - 4), w_in.dtype)
    w_small = jnp.concatenate([w_qi, w_kv, w_ki, w_wi, small_pad], axis=-1)
    z_rw = _mod_matmul(x, sc1, sh1, w_rw, jnp.float32, tn=896)
    z_q = _mod_matmul(x, sc1, sh1, w_q, MM_DTYPE, tn=1024)
    z_small = _mod_matmul(x, sc1, sh1, w_small, jnp.float32, tn=512)
    z_g = _mod_matmul(x, sc1, sh1, jnp.concatenate([w_ga, w_gb], axis=-1), jnp.float32, tn=1024)
    z_ga, z_gb = z_g[..., :D_MODEL], z_g[..., D_MODEL:]
    z_wi_t = jnp.swapaxes(z_small[..., 448:448 + IDX_HEADS], 1, 2)

    y_a = _rwkv7_time_mix(z_rw, rw_mu[l], rw_w0[l], rw_w2[l], rw_a0[l], rw_a2[l], rw_g2[l],
                          rw_k_k[l], rw_k_a[l], rw_r_k[l], rw_gn_g[l], rw_gn_b[l], jnp.float32) @ w_br_a[l]
    y_b = _dsa_attention(z_q, z_small, z_wi_t, dsa_kv_g[l], idx_k_g[l], idx_k_b[l],
                         rel_bias, jnp.float32) @ w_br_b[l]
    merged = jax.nn.sigmoid(z_ga) * y_a + jax.nn.sigmoid(z_gb) * y_b
    mix = merged @ w_out[l]
    x = _layer_norm(DEEPNORM_ALPHA * x + gt1[:, None] * mix, ln1_g[l], ln1_b[l])

    y2 = _peer(x, sc2, sh2, peer_wq[l], peer_keys[l], peer_u[l], peer_v[l])
    x = _layer_norm(DEEPNORM_ALPHA * x + gt2[:, None] * y2, ln2_g[l], ln2_b[l])
    return x
```

```python
import math
from functools import partial

import jax
import jax.numpy as jnp
import numpy as np
from jax import lax
from jax.experimental import pallas as pl
from jax.experimental.pallas import tpu as pltpu
from jax.experimental.pallas import tpu_sc as plsc

D_MODEL = 1024
RW_HEADS = 8
RW_HEAD_DIM = 64
RW_DIM = 512
RW_DECAY_LORA = 64
RW_A_LORA = 64
RW_GATE_LORA = 128
RW_SIZES = (RW_DIM, RW_DIM, RW_DIM, RW_DECAY_LORA, RW_A_LORA, RW_GATE_LORA)
RW_COLS = 3 * RW_DIM + RW_DECAY_LORA + RW_A_LORA + RW_GATE_LORA
RW_GN_EPS = 64e-5
DSA_HEADS = 8
DSA_LATENT = 128
DSA_Q_DIM = DSA_HEADS * DSA_LATENT
IDX_HEADS = 4
IDX_DIM = 64
TOPK_MAX = 256
Q_BLOCK = 128
REL_BUCKETS = 32
REL_MAX_DIST = 128
IN_SIZES = (RW_COLS, DSA_Q_DIM, DSA_LATENT, IDX_HEADS * IDX_DIM, IDX_DIM, IDX_HEADS, D_MODEL, D_MODEL)
IN_COLS = sum(IN_SIZES)
PEER_HEADS = 8
PEER_N_KEYS = 128
PEER_KEY_DIM = 128
PEER_HALF = 64
PEER_TOPK = 16
PEER_CHUNK = 128
LN_EPS = 1e-5
DEPTH = 1
DEEPNORM_ALPHA = (2.0 * DEPTH) ** 0.25

LANES = 128
MM_DTYPE = jnp.bfloat16
BATCH_GROUPS = 2


def _split_points(sizes):
    return np.cumsum(sizes)[:-1].tolist()


def _mod_matmul_kernel(x_ref, sc_ref, sh_ref, w_ref, o_ref):
    h = x_ref[0] * (1.0 + sc_ref[0]) + sh_ref[0]
    o_ref[0] = jnp.dot(h.astype(w_ref.dtype), w_ref[...],
                       preferred_element_type=jnp.float32).astype(o_ref.dtype)


def _mod_matmul(x, sc, sh, w, out_dtype, tn, tm=512):
    B, S, D = x.shape
    N = w.shape[1]
    return pl.pallas_call(
        _mod_matmul_kernel,
        grid=(B, S // tm, N // tn),
        in_specs=[
            pl.BlockSpec((1, tm, D), lambda b, i, j: (b, i, 0)),
            pl.BlockSpec((1, 1, D), lambda b, i, j: (b, 0, 0)),
            pl.BlockSpec((1, 1, D), lambda b, i, j: (b, 0, 0)),
            pl.BlockSpec((D, tn), lambda b, i, j: (0, j)),
        ],
        out_specs=pl.BlockSpec((1, tm, tn), lambda b, i, j: (b, i, j)),
        out_shape=jax.ShapeDtypeStruct((B, S, N), out_dtype),
        name="mod_matmul",
    )(x, sc[:, None, :], sh[:, None, :], w.astype(MM_DTYPE))


def _layer_norm(x, g, b):
    mu = jnp.mean(x, -1, keepdims=True)
    var = jnp.mean(jnp.square(x - mu), -1, keepdims=True)
    return (x - mu) * lax.rsqrt(var + LN_EPS) * g + b


def _t5_bucket(n):
    n = jnp.maximum(n, 0)
    max_exact = REL_BUCKETS // 2
    nf = jnp.maximum(n, 1).astype(jnp.float32)
    large = max_exact + (jnp.log(nf / max_exact) / math.log(REL_MAX_DIST / max_exact)
                         * (REL_BUCKETS - max_exact)).astype(jnp.int32)
    large = jnp.minimum(large, REL_BUCKETS - 1)
    return jnp.where(n < max_exact, n, large)


RW_CHUNK = 64
RW_INV_BLOCK = 16
_NN = (((1,), (0,)), ((), ()))
_NT = (((1,), (1,)), ((), ()))
_TN = (((0,), (0,)), ((), ()))


def _dot_f32(a, b, dims=_NN):
    return lax.dot_general(a, b, dims, precision=lax.Precision.HIGHEST,
                           preferred_element_type=jnp.float32)


def _dot_bf16x3(a, b, dims=_NN):
    f32, bf = jnp.float32, jnp.bfloat16
    a_hi, b_hi = a.astype(bf), b.astype(bf)
    a_lo = (a - a_hi.astype(f32)).astype(bf)
    b_lo = (b - b_hi.astype(f32)).astype(bf)
    out = lax.dot_general(a_hi, b_hi, dims, preferred_element_type=f32)
    out = out + lax.dot_general(a_hi, b_lo, dims, preferred_element_type=f32)
    return out + lax.dot_general(a_lo, b_hi, dims, preferred_element_type=f32)


def _rwkv_kernel(z_ref, mu_ref, w0_ref, w2_ref, a0_ref, a2_ref, g2_ref, kk_ref, ka_ref, rk_ref,
                 gng_ref, gnb_ref, bd_ref, o_ref, m_ref, prev_ref, y_ref):
    C = z_ref.shape[1]
    N = RW_HEAD_DIM
    f32 = jnp.float32
    dot3 = _dot_bf16x3

    @pl.when(pl.program_id(1) == 0)
    def _():
        m_ref[...] = jnp.zeros(m_ref.shape, f32)
        prev_ref[...] = jnp.zeros(prev_ref.shape, f32)

    z = z_ref[0]
    row = lax.broadcasted_iota(jnp.int32, z.shape, 0)
    shifted = jnp.where(row == 0, prev_ref[...], pltpu.roll(z, 1, axis=0))
    prev_ref[...] = z[C - 1:C, :]
    zs = z + (shifted - z) * mu_ref[...]
    r = zs[:, 0:RW_DIM]
    k = zs[:, RW_DIM:2 * RW_DIM]
    v = zs[:, 2 * RW_DIM:3 * RW_DIM]
    o1 = 3 * RW_DIM
    wl = zs[:, o1:o1 + RW_DECAY_LORA]
    al = zs[:, o1 + RW_DECAY_LORA:o1 + RW_DECAY_LORA + RW_A_LORA]
    gl = zs[:, o1 + RW_DECAY_LORA + RW_A_LORA:]

    bd = bd_ref[...]
    log_w = -jax.nn.softplus(-(w0_ref[...] + _dot_f32(jnp.tanh(wl), w2_ref[...]))) - 0.5
    ldec = -jnp.exp(log_w)
    a_lr = jax.nn.sigmoid(a0_ref[...] + _dot_f32(al, a2_ref[...]))
    g = _dot_f32(jax.nn.sigmoid(gl), g2_ref[...])
    kk = k * kk_ref[...]
    kk = kk * lax.rsqrt(jnp.maximum(_dot_f32(kk * kk, bd), 1e-24))
    k2 = k * (1.0 + (a_lr - 1.0) * ka_ref[...])
    a_vec = -kk
    b_vec = kk * a_lr

    ti = lax.broadcasted_iota(jnp.int32, (C, C), 0)
    tj = lax.broadcasted_iota(jnp.int32, (C, C), 1)
    cum = _dot_f32((ti >= tj).astype(f32), ldec)
    cum_last = cum[C - 1:C, :]
    w_incl = jnp.exp(cum)
    w_excl = jnp.exp(cum - ldec)
    w_inv = jnp.exp(-cum)
    w_end = jnp.exp(cum_last - cum)
    w_all = jnp.exp(cum_last)
    a_t = a_vec * w_excl
    r_t = r * w_incl
    b_t = b_vec * w_inv
    k_t = k2 * w_inv
    b_e = b_vec * w_end
    k_e = k2 * w_end

    strict = ti > tj
    incl = ti >= tj
    bi, bj = ti // RW_INV_BLOCK, tj // RW_INV_BLOCK
    same_blk = bi == bj
    pair_blk = jnp.logical_and((bi // 2) == (bj // 2), jnp.logical_not(same_blk))
    half_blk = (bi // 2) != (bj // 2)
    eye = (ti == tj).astype(f32)

    for h in range(RW_HEADS):
        sl = slice(h * N, (h + 1) * N)
        ar = jnp.concatenate([a_t[:, sl], r_t[:, sl]], axis=0)
        bk = jnp.concatenate([b_t[:, sl], k_t[:, sl]], axis=0)
        gm = dot3(ar, bk, _NT)
        a_ab = jnp.where(strict, gm[:C, :C], 0.0)
        a_ak = jnp.where(strict, gm[:C, C:], 0.0)
        a_rb = jnp.where(incl, gm[C:, :C], 0.0)
        a_rk = jnp.where(incl, gm[C:, C:], 0.0)
        d1 = jnp.where(same_blk, a_ab, 0.0)
        xinv = eye + d1
        d2 = dot3(d1, d1)
        xinv = xinv + dot3(xinv, d2)
        d4 = dot3(d2, d2)
        xinv = xinv + dot3(xinv, d4)
        d8 = dot3(d4, d4)
        xinv = xinv + dot3(xinv, d8)
        xinv = xinv + dot3(dot3(xinv, jnp.where(pair_blk, a_ab, 0.0)), xinv)
        xinv = xinv + dot3(dot3(xinv, jnp.where(half_blk, a_ab, 0.0)), xinv)

        vh = v[:, sl]
        av = dot3(jnp.concatenate([a_ak, a_rk], axis=0), vh)
        p = dot3(xinv, jnp.concatenate([a_t[:, sl], av[:C]], axis=1))
        qm = dot3(a_rb, p)
        q1 = r_t[:, sl] + qm[:, :N]
        q2 = qm[:, N:] + av[C:]
        gmat = dot3(b_e[:, sl], p, _TN)
        g1 = eye * w_all[:, sl] + gmat[:, :N]
        g2 = gmat[:, N:] + dot3(k_e[:, sl], vh, _TN)
        m = m_ref[h]
        y_ref[:, sl] = _dot_f32(q1, m) + q2
        m_ref[h] = _dot_f32(g1, m) + g2

    y = y_ref[...]
    mean = _dot_f32(y, bd) * (1.0 / N)
    yc = y - mean
    var = _dot_f32(yc * yc, bd) * (1.0 / N)
    yn = yc * lax.rsqrt(var + RW_GN_EPS) * gng_ref[...] + gnb_ref[...]
    bonus = _dot_f32(r * k2 * rk_ref[...], bd) * v
    o_ref[0] = ((yn + bonus) * g).astype(o_ref.dtype)


def _rwkv7_time_mix(z_rw, mu, w0, w2, a0, a2, g2, k_k, k_a, r_k, gn_g, gn_b, out_dtype):
    B, S, _ = z_rw.shape
    C = RW_CHUNK
    assert S % C == 0 and C == RW_HEAD_DIM and C % (4 * RW_INV_BLOCK) == 0
    hid = jnp.arange(RW_DIM) // RW_HEAD_DIM
    bd = (hid[:, None] == hid[None, :]).astype(jnp.float32)
    row = lambda a: a.reshape(1, -1)
    full = lambda shape: pl.BlockSpec(shape, lambda b, c: (0,) * len(shape))
    return pl.pallas_call(
        _rwkv_kernel,
        grid=(B, S // C),
        in_specs=[
            pl.BlockSpec((1, C, RW_COLS), lambda b, c: (b, c, 0)),
            full((1, RW_COLS)), full((1, RW_DIM)), full((RW_DECAY_LORA, RW_DIM)), full((1, RW_DIM)),
            full((RW_A_LORA, RW_DIM)), full((RW_GATE_LORA, RW_DIM)), full((1, RW_DIM)), full((1, RW_DIM)),
            full((1, RW_DIM)), full((1, RW_DIM)), full((1, RW_DIM)), full((RW_DIM, RW_DIM)),
        ],
        out_specs=pl.BlockSpec((1, C, RW_DIM), lambda b, c: (b, c, 0)),
        out_shape=jax.ShapeDtypeStruct((B, S, RW_DIM), out_dtype),
        scratch_shapes=[
            pltpu.VMEM((RW_HEADS, RW_HEAD_DIM, RW_HEAD_DIM), jnp.float32),
            pltpu.VMEM((1, RW_COLS), jnp.float32),
            pltpu.VMEM((C, RW_DIM), jnp.float32),
        ],
        compiler_params=pltpu.CompilerParams(dimension_semantics=("parallel", "arbitrary")),
        name="rwkv7_time_mix",
    )(z_rw, row(mu), row(w0), w2, row(a0), a2, g2, row(k_k), row(k_a), row(r_k), row(gn_g), row(gn_b), bd)


DSA_TQ = 256
MASK_NEG = -1e30
INT_MIN = -2 ** 31
KEY_NEG_INF = -2139095041
THRESH_BITS = 32


def _dsa_prep_kernel(z_ref, kvg_ref, kig_ref, kib_ref, qi_ref, kv_ref, ki_ref):
    z = z_ref[0]
    qi_ref[0] = z[:, :IDX_HEADS * IDX_DIM].astype(qi_ref.dtype)
    kv = z[:, 256:384]
    ms = jnp.mean(jnp.square(kv), -1, keepdims=True)
    kv_ref[0] = (kv * lax.rsqrt(ms + LN_EPS) * kvg_ref[...]).astype(kv_ref.dtype)
    ki = z[:, 384:448]
    mu = jnp.mean(ki, -1, keepdims=True)
    var = jnp.mean(jnp.square(ki - mu), -1, keepdims=True)
    ki_ref[0] = ((ki - mu) * lax.rsqrt(var + LN_EPS) * kig_ref[...] + kib_ref[...]).astype(ki_ref.dtype)


def _dsa_prep(z_small, kv_g, ki_g, ki_b, tm=512):
    B, S, W = z_small.shape
    return pl.pallas_call(
        _dsa_prep_kernel,
        grid=(B, S // tm),
        in_specs=[
            pl.BlockSpec((1, tm, W), lambda b, i: (b, i, 0)),
            pl.BlockSpec((1, DSA_LATENT), lambda b, i: (0, 0)),
            pl.BlockSpec((1, IDX_DIM), lambda b, i: (0, 0)),
            pl.BlockSpec((1, IDX_DIM), lambda b, i: (0, 0)),
        ],
        out_specs=[
            pl.BlockSpec((1, tm, IDX_HEADS * IDX_DIM), lambda b, i: (b, i, 0)),
            pl.BlockSpec((1, tm, DSA_LATENT), lambda b, i: (b, i, 0)),
            pl.BlockSpec((1, tm, IDX_DIM), lambda b, i: (b, i, 0)),
        ],
        out_shape=[
            jax.ShapeDtypeStruct((B, S, IDX_HEADS * IDX_DIM), MM_DTYPE),
            jax.ShapeDtypeStruct((B, S, DSA_LATENT), MM_DTYPE),
            jax.ShapeDtypeStruct((B, S, IDX_DIM), MM_DTYPE),
        ],
        name="dsa_prep",
    )(z_small, kv_g[None], ki_g[None], ki_b[None])


def _sortable_key(s):
    s = jnp.where(s == 0.0, 0.0, s)
    bits = pltpu.bitcast(s, jnp.int32)
    return bits ^ ((bits >> 31) & 0x7FFFFFFF)


def _col_count(mask_i32):
    tk, tq = mask_i32.shape
    return jnp.sum(mask_i32.reshape(tk // 8, 8, tq), axis=0)


def _dsa_kernel(bfar_ref, q_ref, qi_ref, wit_ref, kv_ref, kvt_ref, ki_ref, b0_ref, b1_ref, o_ref,
                key_ref, madd_ref, m_ref, l_ref, acc_ref, *, topk):
    tq = q_ref.shape[1]
    tk = tq
    i = pl.program_id(1)
    nj = i + 1
    f32 = jnp.float32
    krow = lax.broadcasted_iota(jnp.int32, (tk, tq), 0)
    qcol = lax.broadcasted_iota(jnp.int32, (tk, tq), 1)

    qi = qi_ref[0]
    wit = wit_ref[0] * (IDX_HEADS ** -0.5)

    def score_chunk(j, carry):
        off = pl.multiple_of(j * tk, tk)
        kc = ki_ref[0, pl.ds(off, tk), :]
        s = jnp.zeros((tk, tq), f32)
        for h in range(IDX_HEADS):
            d = lax.dot_general(kc, qi[:, h * IDX_DIM:(h + 1) * IDX_DIM],
                                (((1,), (1,)), ((), ())), preferred_element_type=f32)
            s = s + wit[h:h + 1, :] * jnp.maximum(d * (IDX_DIM ** -0.5), 0.0)
        causal = (krow + j * tk) <= (qcol + i * tq)
        s = jnp.where(causal, s, -jnp.inf)
        key_ref[j] = _sortable_key(s)
        return carry

    lax.fori_loop(0, nj, score_chunk, 0)

    def count_where(pred_fn):
        def body(j, acc):
            return acc + _col_count(pred_fn(key_ref[j], j).astype(jnp.int32))
        acc = lax.fori_loop(0, nj, body, jnp.zeros((8, tq), jnp.int32))
        return jnp.sum(acc, axis=0, keepdims=True)

    def bit_step(it, t_u):
        bit = THRESH_BITS - 1 - it
        cand_u = t_u | jnp.left_shift(jnp.int32(1), bit)
        cand = cand_u ^ INT_MIN
        cnt = count_where(lambda k, j: k >= cand)
        return jnp.where(cnt >= topk, cand_u, t_u)

    t_u = lax.fori_loop(0, THRESH_BITS, bit_step, jnp.zeros((1, tq), jnp.int32))
    thr = t_u ^ INT_MIN
    cnt_gt = count_where(lambda k, j: k > thr)
    cnt_ge = count_where(lambda k, j: k >= thr)
    is_neg = thr == KEY_NEG_INF
    need = jnp.logical_and(cnt_ge > topk, jnp.logical_not(is_neg))
    n_tie_take = topk - cnt_gt

    thr_open = jnp.where(is_neg, thr, thr - 1)
    any_need = jnp.max(need.astype(jnp.int32)) > 0

    @pl.when(jnp.logical_not(any_need))
    def _():
        def body(j, carry):
            madd_ref[j] = jnp.where(key_ref[j] > thr_open, 0.0, MASK_NEG)
            return carry
        lax.fori_loop(0, nj, body, 0)

    @pl.when(any_need)
    def _():
        s_len = tk * key_ref.shape[0]
        n_bits = max(1, int(math.ceil(math.log2(s_len))))

        def idx_step(it, p):
            bit = n_bits - 1 - it
            cand = p | jnp.left_shift(jnp.int32(1), bit)
            cnt = count_where(
                lambda k, j: jnp.where(k == thr, jnp.where((krow + j * tk) < cand, 1, 0), 0))
            return jnp.where(cnt < n_tie_take, cand, p)

        p_idx = lax.fori_loop(0, n_bits, idx_step, jnp.zeros((1, tq), jnp.int32))
        p_idx = jnp.where(need, p_idx, jnp.where(is_neg, -1, s_len))

        def body(j, carry):
            k = key_ref[j]
            tie_ok = jnp.where((krow + j * tk) <= p_idx, 0.0, MASK_NEG)
            madd_ref[j] = jnp.where(k > thr, 0.0, jnp.where(k == thr, tie_ok, MASK_NEG))
            return carry
        lax.fori_loop(0, nj, body, 0)

    m_ref[...] = jnp.full(m_ref.shape, MASK_NEG, f32)
    l_ref[...] = jnp.zeros(l_ref.shape, f32)
    acc_ref[...] = jnp.zeros(acc_ref.shape, f32)
    scale = DSA_LATENT ** -0.5

    def attend(j, bias_fn):
        off = pl.multiple_of(j * tk, tk)
        kc = kv_ref[0, pl.ds(off, tk), :]
        kct = kvt_ref[0, :, pl.ds(off, tk)]
        madd = madd_ref[j]
        for h in range(DSA_HEADS):
            qh = q_ref[0, :, h * DSA_LATENT:(h + 1) * DSA_LATENT]
            lg = lax.dot_general(kc, qh, (((1,), (1,)), ((), ())), preferred_element_type=f32)
            lg = lg * scale + bias_fn(h) + madd
            m_old = m_ref[h]
            m_new = jnp.maximum(m_old, jnp.max(lg, axis=0, keepdims=True))
            alpha = jnp.exp(m_old - m_new)
            p = jnp.exp(lg - m_new)
            l_ref[h] = alpha * l_ref[h] + jnp.sum(p, axis=0, keepdims=True)
            acc_ref[h] = alpha * acc_ref[h] + jnp.dot(kct, p.astype(kct.dtype),
                                                      preferred_element_type=f32)
            m_ref[h] = m_new

    def far_body(j, carry):
        attend(j, lambda h: bfar_ref[h])
        return carry

    lax.fori_loop(0, jnp.maximum(i - 1, 0), far_body, 0)

    @pl.when(i >= 1)
    def _():
        attend(i - 1, lambda h: b1_ref[h])

    attend(i, lambda h: b0_ref[h])

    for h in range(DSA_HEADS):
        out_t = acc_ref[h] / l_ref[h]
        o_ref[0, :, h * DSA_LATENT:(h + 1) * DSA_LATENT] = out_t.T.astype(o_ref.dtype)


def _dsa_bias_tiles(rel_bias, tq):
    dist = jnp.arange(2 * tq, dtype=jnp.int32)
    by_dist = rel_bias[_t5_bucket(dist)].T
    kk = jnp.arange(tq, dtype=jnp.int32)[:, None]
    qq = jnp.arange(tq, dtype=jnp.int32)[None, :]
    b0 = by_dist[:, jnp.maximum(qq - kk, 0)]
    b1 = by_dist[:, tq + qq - kk]
    return b0, b1, by_dist[:, -1]


def _dsa_attention(q, z_small, z_wi_t, kv_g, idx_k_g, idx_k_b, rel_bias, out_dtype):
    B, S, _ = q.shape
    tq = DSA_TQ
    topk = min(TOPK_MAX, S // 4)
    assert S % tq == 0 and tq >= REL_MAX_DIST and topk <= tq
    qi, ckv, kidx = _dsa_prep(z_small, kv_g, idx_k_g, idx_k_b)
    ckv_t = jnp.swapaxes(ckv, 1, 2)
    b0, b1, bfar = _dsa_bias_tiles(rel_bias, tq)
    nq = S // tq
    H, d = DSA_HEADS, DSA_LATENT
    grid_spec = pltpu.PrefetchScalarGridSpec(
        num_scalar_prefetch=0,
        grid=(B, nq),
        in_specs=[
            pl.BlockSpec(memory_space=pltpu.SMEM),
            pl.BlockSpec((1, tq, H * d), lambda b, i: (b, i, 0)),
            pl.BlockSpec((1, tq, IDX_HEADS * IDX_DIM), lambda b, i: (b, i, 0)),
            pl.BlockSpec((1, IDX_HEADS, tq), lambda b, i: (b, 0, i)),
            pl.BlockSpec((1, S, d), lambda b, i: (b, 0, 0)),
            pl.BlockSpec((1, d, S), lambda b, i: (b, 0, 0)),
            pl.BlockSpec((1, S, IDX_DIM), lambda b, i: (b, 0, 0)),
            pl.BlockSpec((H, tq, tq), lambda b, i: (0, 0, 0)),
            pl.BlockSpec((H, tq, tq), lambda b, i: (0, 0, 0)),
        ],
        out_specs=pl.BlockSpec((1, tq, H * d), lambda b, i: (b, i, 0)),
        scratch_shapes=[
            pltpu.VMEM((nq, tq, tq), jnp.int32),
            pltpu.VMEM((nq, tq, tq), jnp.float32),
            pltpu.VMEM((H, 1, tq), jnp.float32),
            pltpu.VMEM((H, 1, tq), jnp.float32),
            pltpu.VMEM((H, d, tq), jnp.float32),
        ],
    )
    return pl.pallas_call(
        partial(_dsa_kernel, topk=topk),
        grid_spec=grid_spec,
        out_shape=jax.ShapeDtypeStruct((B, S, H * d), out_dtype),
        compiler_params=pltpu.CompilerParams(vmem_limit_bytes=48 * 1024 * 1024),
        name="dsa_attention",
    )(bfar, q, qi, z_wi_t, ckv, ckv_t, kidx, b0, b1)


SC_CORES = 2
SC_SUBCORES = 16
SC_LANES = 16
PEER_SC_TOKENS = 8
PEER_SC_RING = 4
PEER_SC_UNROLL = 4
PEER_SLOTS = PEER_HEADS * PEER_TOPK


def _peer_sc_call(body, T, out_width, stage_width):
    mesh = plsc.VectorSubcoreMesh(core_axis_name="c", subcore_axis_name="s")
    return pl.kernel(
        body, mesh=mesh,
        out_type=jax.ShapeDtypeStruct((T, out_width), jnp.float32),
        scratch_types=[
            pltpu.VMEM((PEER_SC_TOKENS, PEER_SLOTS), jnp.int32),
            pltpu.VMEM((PEER_SC_TOKENS, stage_width), jnp.float32),
            pltpu.VMEM((PEER_SC_RING, PEER_TOPK, D_MODEL), jnp.float32),
            pltpu.VMEM((PEER_SC_TOKENS, out_width), jnp.float32),
            pltpu.SemaphoreType.DMA((PEER_SC_RING,)),
        ],
        compiler_params=pltpu.CompilerParams(needs_layout_passes=False),
    )


def _peer_sc_body(compute, zero_out, x_hbm, idx_hbm, tab_hbm, out_hbm, idx_v, x_v, rows_v, out_v, sems):
    T = idx_hbm.shape[0]
    tokens_per_worker = T // (SC_CORES * SC_SUBCORES)
    n_steps = PEER_SC_TOKENS * PEER_HEADS
    worker = lax.axis_index("s") * SC_CORES + lax.axis_index("c")
    base = worker * tokens_per_worker

    def gather(s, b):
        ids = idx_v[s // PEER_HEADS, pl.ds((s % PEER_HEADS) * PEER_TOPK, PEER_TOPK)]
        return pltpu.make_async_copy(tab_hbm.at[ids], rows_v.at[b], sems.at[b])

    @pl.loop(0, tokens_per_worker // PEER_SC_TOKENS)
    def _(blk):
        tok0 = base + blk * PEER_SC_TOKENS
        pltpu.sync_copy(idx_hbm.at[pl.ds(tok0, PEER_SC_TOKENS)], idx_v)
        pltpu.sync_copy(x_hbm.at[pl.ds(tok0, PEER_SC_TOKENS)], x_v)
        for b in range(PEER_SC_RING - 1):
            gather(b, b).start()
        if zero_out:
            @pl.loop(0, PEER_SC_TOKENS)
            def _(t):
                @plsc.parallel_loop(0, out_v.shape[1] // SC_LANES, unroll=PEER_SC_UNROLL)
                def _(c):
                    out_v[t, pl.ds(pl.multiple_of(c * SC_LANES, SC_LANES), SC_LANES)] = (
                        jnp.zeros((SC_LANES,), jnp.float32))

        @pl.loop(0, n_steps, step=PEER_SC_RING)
        def _(s0):
            for b in range(PEER_SC_RING):
                s = s0 + b
                gather(s, b).wait()

                @pl.when(s + PEER_SC_RING - 1 < n_steps)
                def _():
                    gather(s + PEER_SC_RING - 1, (b + PEER_SC_RING - 1) % PEER_SC_RING).start()

                compute(s // PEER_HEADS, s % PEER_HEADS, b, x_v, rows_v, out_v)

        pltpu.sync_copy(out_v, out_hbm.at[pl.ds(tok0, PEER_SC_TOKENS)])


def _peer_dots_compute(t, hd, b, h_v, rows_v, dots_v):
    lane = lax.iota(jnp.int32, SC_LANES)

    def col_step(c, accs):
        off = pl.multiple_of(c * SC_LANES, SC_LANES)
        hv = h_v[t, pl.ds(off, SC_LANES)]
        return tuple(accs[r] + rows_v[b, r, pl.ds(off, SC_LANES)] * hv for r in range(PEER_TOPK))

    accs = plsc.parallel_loop(
        0, D_MODEL // SC_LANES, unroll=PEER_SC_UNROLL,
        carry=tuple(jnp.zeros((SC_LANES,), jnp.float32) for _ in range(PEER_TOPK)))(col_step)
    res = jnp.zeros((SC_LANES,), jnp.float32)
    for r in range(PEER_TOPK):
        res = jnp.where(lane == r, jnp.sum(accs[r]), res)
    dots_v[t, pl.ds(hd * PEER_TOPK, PEER_TOPK)] = res


def _peer_mix_compute(t, hd, b, act_v, rows_v, out_v):
    tvec = jnp.full((SC_LANES,), t, jnp.int32)
    weights = [plsc.load_gather(act_v, [tvec, jnp.full((SC_LANES,), hd * PEER_TOPK + r, jnp.int32)])
               for r in range(PEER_TOPK)]

    @plsc.parallel_loop(0, D_MODEL // SC_LANES, unroll=PEER_SC_UNROLL)
    def _(c):
        off = pl.multiple_of(c * SC_LANES, SC_LANES)
        a = rows_v[b, 0, pl.ds(off, SC_LANES)] * weights[0]
        for r in range(1, PEER_TOPK):
            a = a + rows_v[b, r, pl.ds(off, SC_LANES)] * weights[r]
        plsc.addupdate(out_v.at[t, pl.ds(off, SC_LANES)], a)


def _peer_expert_dots(h, experts, u_tab):
    T = h.shape[0]
    assert T % (SC_CORES * SC_SUBCORES * PEER_SC_TOKENS) == 0
    assert (PEER_SC_TOKENS * PEER_HEADS) % PEER_SC_RING == 0 and PEER_TOPK == SC_LANES
    body = partial(_peer_sc_body, _peer_dots_compute, False)
    return _peer_sc_call(body, T, PEER_SLOTS, D_MODEL)(h, experts, u_tab)


def _peer_expert_mix(act, experts, v_tab):
    T = act.shape[0]
    assert T % (SC_CORES * SC_SUBCORES * PEER_SC_TOKENS) == 0
    body = partial(_peer_sc_body, _peer_mix_compute, True)
    return _peer_sc_call(body, T, D_MODEL, PEER_SLOTS)(act, experts, v_tab)


PEER_TM = 256


def _extract_topk(s, k, payload=None):
    R = s.shape[0]
    riota = lax.broadcasted_iota(jnp.int32, s.shape, 0)
    vals, rows = [], []
    for _ in range(k):
        m = jnp.max(s, axis=0, keepdims=True)
        pos = jnp.min(jnp.where(s == m, riota, R), axis=0, keepdims=True)
        hit = riota == pos
        vals.append(m)
        rows.append(pos if payload is None else jnp.max(jnp.where(hit, payload, -1), axis=0, keepdims=True))
        s = jnp.where(hit, -jnp.inf, s)
    return jnp.concatenate(vals, axis=0), jnp.concatenate(rows, axis=0)


def _peer_route_kernel(x_ref, sc_ref, sh_ref, wq_ref, keys_ref, h_ref, ex_ref, gate_ref, q_ref, ext_ref, gt_ref):
    K = PEER_TOPK
    f32 = jnp.float32
    h = x_ref[0] * (1.0 + sc_ref[0]) + sh_ref[0]
    h_ref[0] = h
    q = jnp.dot(h.astype(wq_ref.dtype), wq_ref[...], preferred_element_type=f32)
    for hd in range(PEER_HEADS):
        q_ref[hd] = q[:, hd * PEER_KEY_DIM:(hd + 1) * PEER_KEY_DIM].astype(q_ref.dtype)

    def head_body(hd, carry):
        qh = q_ref[hd]
        s1 = lax.dot_general(keys_ref[hd, 0], qh[:, :PEER_HALF], _NT, preferred_element_type=f32)
        s2 = lax.dot_general(keys_ref[hd, 1], qh[:, PEER_HALF:], _NT, preferred_element_type=f32)
        v1, i1 = _extract_topk(s1, K)
        v2, i2 = _extract_topk(s2, K)
        tm = v1.shape[1]
        cand = (v1[:, None, :] + v2[None, :, :]).reshape(K * K, tm)
        cidx = (i1[:, None, :] * PEER_N_KEYS + i2[None, :, :]).reshape(K * K, tm)
        top_s, experts = _extract_topk(cand, K, payload=cidx)
        e = jnp.exp(top_s - top_s[0:1, :])
        gt_ref[pl.ds(pl.multiple_of(hd * K, K), K), :] = e / jnp.sum(e, axis=0, keepdims=True)
        ext_ref[pl.ds(pl.multiple_of(hd * K, K), K), :] = experts
        return carry

    lax.fori_loop(0, PEER_HEADS, head_body, 0)
    ex_ref[0] = ext_ref[...].T
    gate_ref[0] = gt_ref[...].T


def _peer_route(x, sc, sh, w_pq, sub_keys):
    B, S, D = x.shape
    tm = PEER_TM
    return pl.pallas_call(
        _peer_route_kernel,
        grid=(B, S // tm),
        in_specs=[
            pl.BlockSpec((1, tm, D), lambda b, i: (b, i, 0)),
            pl.BlockSpec((1, 1, D), lambda b, i: (b, 0, 0)),
            pl.BlockSpec((1, 1, D), lambda b, i: (b, 0, 0)),
            pl.BlockSpec((D, PEER_HEADS * PEER_KEY_DIM), lambda b, i: (0, 0)),
            pl.BlockSpec((PEER_HEADS, 2, PEER_N_KEYS, PEER_HALF), lambda b, i: (0, 0, 0, 0)),
        ],
        out_specs=[
            pl.BlockSpec((1, tm, D), lambda b, i: (b, i, 0)),
            pl.BlockSpec((1, tm, PEER_SLOTS), lambda b, i: (b, i, 0)),
            pl.BlockSpec((1, tm, PEER_SLOTS), lambda b, i: (b, i, 0)),
        ],
        out_shape=[
            jax.ShapeDtypeStruct((B, S, D), jnp.float32),
            jax.ShapeDtypeStruct((B, S, PEER_SLOTS), jnp.int32),
            jax.ShapeDtypeStruct((B, S, PEER_SLOTS), jnp.float32),
        ],
        scratch_shapes=[
            pltpu.VMEM((PEER_HEADS, tm, PEER_KEY_DIM), MM_DTYPE),
            pltpu.VMEM((PEER_SLOTS, tm), jnp.int32),
            pltpu.VMEM((PEER_SLOTS, tm), jnp.float32),
        ],
        compiler_params=pltpu.CompilerParams(dimension_semantics=("parallel", "parallel")),
        name="peer_route",
    )(x, sc[:, None, :], sh[:, None, :], w_pq.astype(MM_DTYPE), sub_keys.astype(MM_DTYPE))


def _peer(x, sc, sh, w_pq, sub_keys, u_tab, v_tab):
    B, S, D = x.shape
    T = B * S
    h2, experts, gates = _peer_route(x, sc, sh, w_pq, sub_keys)
    experts = experts.reshape(T, PEER_SLOTS)
    pre = _peer_expert_dots(h2.reshape(T, D), experts, u_tab)
    act = jax.nn.gelu(pre, approximate=False) * gates.reshape(T, PEER_SLOTS)
    return _peer_expert_mix(act, experts, v_tab).reshape(B, S, D)


def kernel(x, c, w_ada, b_ada, w_in, rw_mu, rw_w0, rw_w2, rw_a0, rw_a2, rw_g2, rw_k_k, rw_k_a, rw_r_k, rw_gn_g, rw_gn_b, dsa_kv_g, idx_k_g, idx_k_b, rel_bias, w_br_a, w_br_b, w_out, ln1_g, ln1_b, peer_wq, peer_keys, peer_u, peer_v, ln2_g, ln2_b):
    l = 0
    mod = jax.nn.silu(c) @ w_ada[l] + b_ada[l]

    w_rw, w_q, w_kv, w_qi, w_ki, w_wi, w_ga, w_gb = jnp.split(w_in[l], _split_points(IN_SIZES), axis=-1)
    small_pad = jnp.zeros((D_MODEL, 512 - 256 - 128 - 64 - 4), w_in.dtype)
    w_small = jnp.concatenate([w_qi, w_kv, w_ki, w_wi, small_pad], axis=-1)
    w_gates = jnp.concatenate([w_ga, w_gb], axis=-1)

    def block(x, mod):
        sh1, sc1, gt1, sh2, sc2, gt2 = jnp.split(mod, 6, axis=-1)
        z_rw = _mod_matmul(x, sc1, sh1, w_rw, jnp.float32, tn=896)
        z_q = _mod_matmul(x, sc1, sh1, w_q, MM_DTYPE, tn=1024)
        z_small = _mod_matmul(x, sc1, sh1, w_small, jnp.float32, tn=512)
        z_g = _mod_matmul(x, sc1, sh1, w_gates, jnp.float32, tn=1024)
        z_ga, z_gb = z_g[..., :D_MODEL], z_g[..., D_MODEL:]
        z_wi_t = jnp.swapaxes(z_small[..., 448:448 + IDX_HEADS], 1, 2)

        y_a = _rwkv7_time_mix(z_rw, rw_mu[l], rw_w0[l], rw_w2[l], rw_a0[l], rw_a2[l], rw_g2[l],
                              rw_k_k[l], rw_k_a[l], rw_r_k[l], rw_gn_g[l], rw_gn_b[l], jnp.float32) @ w_br_a[l]
        y_b = _dsa_attention(z_q, z_small, z_wi_t, dsa_kv_g[l], idx_k_g[l], idx_k_b[l],
                             rel_bias, jnp.float32) @ w_br_b[l]
        merged = jax.nn.sigmoid(z_ga) * y_a + jax.nn.sigmoid(z_gb) * y_b
        mix = merged @ w_out[l]
        x = _layer_norm(DEEPNORM_ALPHA * x + gt1[:, None] * mix, ln1_g[l], ln1_b[l])

        y2 = _peer(x, sc2, sh2, peer_wq[l], peer_keys[l], peer_u[l], peer_v[l])
        return _layer_norm(DEEPNORM_ALPHA * x + gt2[:, None] * y2, ln2_g[l], ln2_b[l])

    gsz = x.shape[0] // BATCH_GROUPS
    outs = [block(x[g * gsz:(g + 1) * gsz], mod[g * gsz:(g + 1) * gsz]) for g in range(BATCH_GROUPS)]
    return jnp.concatenate(outs, axis=0)
```

```python
import math
from functools import partial

import jax
import jax.numpy as jnp
import numpy as np
from jax import lax
from jax.experimental import pallas as pl
from jax.experimental.pallas import tpu as pltpu
from jax.experimental.pallas import tpu_sc as plsc

D_MODEL = 1024
RW_HEADS = 8
RW_HEAD_DIM = 64
RW_DIM = 512
RW_DECAY_LORA = 64
RW_A_LORA = 64
RW_GATE_LORA = 128
RW_SIZES = (RW_DIM, RW_DIM, RW_DIM, RW_DECAY_LORA, RW_A_LORA, RW_GATE_LORA)
RW_COLS = 3 * RW_DIM + RW_DECAY_LORA + RW_A_LORA + RW_GATE_LORA
RW_GN_EPS = 64e-5
DSA_HEADS = 8
DSA_LATENT = 128
DSA_Q_DIM = DSA_HEADS * DSA_LATENT
IDX_HEADS = 4
IDX_DIM = 64
TOPK_MAX = 256
Q_BLOCK = 128
REL_BUCKETS = 32
REL_MAX_DIST = 128
IN_SIZES = (RW_COLS, DSA_Q_DIM, DSA_LATENT, IDX_HEADS * IDX_DIM, IDX_DIM, IDX_HEADS, D_MODEL, D_MODEL)
IN_COLS = sum(IN_SIZES)
PEER_HEADS = 8
PEER_N_KEYS = 128
PEER_KEY_DIM = 128
PEER_HALF = 64
PEER_TOPK = 16
PEER_CHUNK = 128
LN_EPS = 1e-5
DEPTH = 1
DEEPNORM_ALPHA = (2.0 * DEPTH) ** 0.25

LANES = 128
MM_DTYPE = jnp.bfloat16
BATCH_GROUPS = 4


def _split_points(sizes):
    return np.cumsum(sizes)[:-1].tolist()


def _mod_matmul_kernel(x_ref, sc_ref, sh_ref, w_ref, o_ref):
    h = x_ref[0] * (1.0 + sc_ref[0]) + sh_ref[0]
    o_ref[0] = jnp.dot(h.astype(w_ref.dtype), w_ref[...],
                       preferred_element_type=jnp.float32).astype(o_ref.dtype)


def _mod_matmul(x, sc, sh, w, out_dtype, tn, tm=512):
    B, S, D = x.shape
    N = w.shape[1]
    return pl.pallas_call(
        _mod_matmul_kernel,
        grid=(B, S // tm, N // tn),
        in_specs=[
            pl.BlockSpec((1, tm, D), lambda b, i, j: (b, i, 0)),
            pl.BlockSpec((1, 1, D), lambda b, i, j: (b, 0, 0)),
            pl.BlockSpec((1, 1, D), lambda b, i, j: (b, 0, 0)),
            pl.BlockSpec((D, tn), lambda b, i, j: (0, j)),
        ],
        out_specs=pl.BlockSpec((1, tm, tn), lambda b, i, j: (b, i, j)),
        out_shape=jax.ShapeDtypeStruct((B, S, N), out_dtype),
        name="mod_matmul",
    )(x, sc[:, None, :], sh[:, None, :], w.astype(MM_DTYPE))


def _layer_norm(x, g, b):
    mu = jnp.mean(x, -1, keepdims=True)
    var = jnp.mean(jnp.square(x - mu), -1, keepdims=True)
    return (x - mu) * lax.rsqrt(var + LN_EPS) * g + b


def _t5_bucket(n):
    n = jnp.maximum(n, 0)
    max_exact = REL_BUCKETS // 2
    nf = jnp.maximum(n, 1).astype(jnp.float32)
    large = max_exact + (jnp.log(nf / max_exact) / math.log(REL_MAX_DIST / max_exact)
                         * (REL_BUCKETS - max_exact)).astype(jnp.int32)
    large = jnp.minimum(large, REL_BUCKETS - 1)
    return jnp.where(n < max_exact, n, large)


RW_CHUNK = 64
RW_INV_BLOCK = 16
_NN = (((1,), (0,)), ((), ()))
_NT = (((1,), (1,)), ((), ()))
_BNN = (((2,), (1,)), ((0,), (0,)))
_BNT = (((2,), (2,)), ((0,), (0,)))


def _dot_f32(a, b, dims=_NN):
    return lax.dot_general(a, b, dims, precision=lax.Precision.HIGHEST,
                           preferred_element_type=jnp.float32)


def _dot_bf16x3(a, b, dims=_NN):
    f32, bf = jnp.float32, jnp.bfloat16
    a_hi, b_hi = a.astype(bf), b.astype(bf)
    a_lo = (a - a_hi.astype(f32)).astype(bf)
    b_lo = (b - b_hi.astype(f32)).astype(bf)
    out = lax.dot_general(a_hi, b_hi, dims, preferred_element_type=f32)
    out = out + lax.dot_general(a_hi, b_lo, dims, preferred_element_type=f32)
    return out + lax.dot_general(a_lo, b_hi, dims, preferred_element_type=f32)


def _bf16_terms(a):
    f32, bf = jnp.float32, jnp.bfloat16
    hi = a.astype(bf)
    r1 = a - hi.astype(f32)
    mid = r1.astype(bf)
    lo = (r1 - mid.astype(f32)).astype(bf)
    return hi, mid, lo


def _dot_lhs_split(a, b01):
    b = b01.astype(jnp.bfloat16)
    return sum(jnp.dot(t, b, preferred_element_type=jnp.float32) for t in _bf16_terms(a))


def _dot_rhs_split(a01, b):
    a = a01.astype(jnp.bfloat16)
    return sum(jnp.dot(a, t, preferred_element_type=jnp.float32) for t in _bf16_terms(b))


def _rwkv_kernel(z_ref, mu_ref, w0_ref, w2_ref, a0_ref, a2_ref, g2_ref, kk_ref, ka_ref, rk_ref,
                 gng_ref, gnb_ref, bd_ref, o_ref, m_ref, prev_ref, y_ref):
    C = z_ref.shape[1]
    N = RW_HEAD_DIM
    f32 = jnp.float32
    dot3 = _dot_bf16x3

    @pl.when(pl.program_id(1) == 0)
    def _():
        m_ref[...] = jnp.zeros(m_ref.shape, f32)
        prev_ref[...] = jnp.zeros(prev_ref.shape, f32)

    z = z_ref[0]
    row = lax.broadcasted_iota(jnp.int32, z.shape, 0)
    shifted = jnp.where(row == 0, prev_ref[...], pltpu.roll(z, 1, axis=0))
    prev_ref[...] = z[C - 1:C, :]
    zs = z + (shifted - z) * mu_ref[...]
    r = zs[:, 0:RW_DIM]
    k = zs[:, RW_DIM:2 * RW_DIM]
    v = zs[:, 2 * RW_DIM:3 * RW_DIM]
    o1 = 3 * RW_DIM
    wl = zs[:, o1:o1 + RW_DECAY_LORA]
    al = zs[:, o1 + RW_DECAY_LORA:o1 + RW_DECAY_LORA + RW_A_LORA]
    gl = zs[:, o1 + RW_DECAY_LORA + RW_A_LORA:]

    bd = bd_ref[...]
    log_w = -jax.nn.softplus(-(w0_ref[...] + dot3(jnp.tanh(wl), w2_ref[...]))) - 0.5
    ldec = -jnp.exp(log_w)
    a_lr = jax.nn.sigmoid(a0_ref[...] + dot3(al, a2_ref[...]))
    g = dot3(jax.nn.sigmoid(gl), g2_ref[...])
    kk = k * kk_ref[...]
    kk = kk * lax.rsqrt(jnp.maximum(_dot_lhs_split(kk * kk, bd), 1e-24))
    k2 = k * (1.0 + (a_lr - 1.0) * ka_ref[...])
    a_vec = -kk
    b_vec = kk * a_lr

    ti = lax.broadcasted_iota(jnp.int32, (C, C), 0)
    tj = lax.broadcasted_iota(jnp.int32, (C, C), 1)
    cum = _dot_rhs_split((ti >= tj).astype(f32), ldec)
    cum_last = cum[C - 1:C, :]
    w_incl = jnp.exp(cum)
    w_excl = jnp.exp(cum - ldec)
    w_inv = jnp.exp(-cum)
    w_end = jnp.exp(cum_last - cum)
    w_all = jnp.exp(cum_last)
    a_t = a_vec * w_excl
    r_t = r * w_incl
    b_t = b_vec * w_inv
    k_t = k2 * w_inv
    b_e = b_vec * w_end
    k_e = k2 * w_end

    strict = ti > tj
    incl = ti >= tj
    bi, bj = ti // RW_INV_BLOCK, tj // RW_INV_BLOCK
    same_blk = bi == bj
    pair_blk = jnp.logical_and((bi // 2) == (bj // 2), jnp.logical_not(same_blk))
    half_blk = (bi // 2) != (bj // 2)
    eye = (ti == tj).astype(f32)

    H = RW_HEADS
    heads = lambda x: jnp.stack([x[:, h * N:(h + 1) * N] for h in range(H)], axis=0)
    bmm = lambda a, b: dot3(a, b, _BNN)
    A, Rt, Bt, Kt, Be, Ke, V = (heads(t) for t in (a_t, r_t, b_t, k_t, b_e, k_e, v))
    gm = dot3(jnp.concatenate([A, Rt], axis=1), jnp.concatenate([Bt, Kt], axis=1), _BNT)
    a_ab = jnp.where(strict, gm[:, :C, :C], 0.0)
    a_ak = jnp.where(strict, gm[:, :C, C:], 0.0)
    a_rb = jnp.where(incl, gm[:, C:, :C], 0.0)
    a_rk = jnp.where(incl, gm[:, C:, C:], 0.0)
    d1 = jnp.where(same_blk, a_ab, 0.0)
    xinv = eye + d1
    d2 = bmm(d1, d1)
    xinv = xinv + bmm(xinv, d2)
    d4 = bmm(d2, d2)
    xinv = xinv + bmm(xinv, d4)
    d8 = bmm(d4, d4)
    xinv = xinv + bmm(xinv, d8)
    xinv = xinv + bmm(bmm(xinv, jnp.where(pair_blk, a_ab, 0.0)), xinv)
    xinv = xinv + bmm(bmm(xinv, jnp.where(half_blk, a_ab, 0.0)), xinv)

    av = bmm(jnp.concatenate([a_ak, a_rk], axis=1), V)
    p = bmm(xinv, jnp.concatenate([A, av[:, :C]], axis=2))
    qm = bmm(a_rb, p)
    q1 = Rt + qm[:, :, :N]
    q2 = qm[:, :, N:] + av[:, C:]
    gmat = bmm(jnp.swapaxes(Be, 1, 2), p)
    g1 = eye * heads(w_all) + gmat[:, :, :N]
    g2 = gmat[:, :, N:] + bmm(jnp.swapaxes(Ke, 1, 2), V)
    m = m_ref[...]
    yh = _dot_f32(q1, m, _BNN) + q2
    m_ref[...] = _dot_f32(g1, m, _BNN) + g2
    for h in range(H):
        y_ref[:, h * N:(h + 1) * N] = yh[h]

    y = y_ref[...]
    mean = _dot_lhs_split(y, bd) * (1.0 / N)
    yc = y - mean
    var = _dot_lhs_split(yc * yc, bd) * (1.0 / N)
    yn = yc * lax.rsqrt(var + RW_GN_EPS) * gng_ref[...] + gnb_ref[...]
    bonus = _dot_lhs_split(r * k2 * rk_ref[...], bd) * v
    o_ref[0] = ((yn + bonus) * g).astype(o_ref.dtype)


def _rwkv7_time_mix(z_rw, mu, w0, w2, a0, a2, g2, k_k, k_a, r_k, gn_g, gn_b, out_dtype):
    B, S, _ = z_rw.shape
    C = RW_CHUNK
    assert S % C == 0 and C == RW_HEAD_DIM and C % (4 * RW_INV_BLOCK) == 0
    hid = jnp.arange(RW_DIM) // RW_HEAD_DIM
    bd = (hid[:, None] == hid[None, :]).astype(jnp.float32)
    row = lambda a: a.reshape(1, -1)
    full = lambda shape: pl.BlockSpec(shape, lambda b, c: (0,) * len(shape))
    return pl.pallas_call(
        _rwkv_kernel,
        grid=(B, S // C),
        in_specs=[
            pl.BlockSpec((1, C, RW_COLS), lambda b, c: (b, c, 0)),
            full((1, RW_COLS)), full((1, RW_DIM)), full((RW_DECAY_LORA, RW_DIM)), full((1, RW_DIM)),
            full((RW_A_LORA, RW_DIM)), full((RW_GATE_LORA, RW_DIM)), full((1, RW_DIM)), full((1, RW_DIM)),
            full((1, RW_DIM)), full((1, RW_DIM)), full((1, RW_DIM)), full((RW_DIM, RW_DIM)),
        ],
        out_specs=pl.BlockSpec((1, C, RW_DIM), lambda b, c: (b, c, 0)),
        out_shape=jax.ShapeDtypeStruct((B, S, RW_DIM), out_dtype),
        scratch_shapes=[
            pltpu.VMEM((RW_HEADS, RW_HEAD_DIM, RW_HEAD_DIM), jnp.float32),
            pltpu.VMEM((1, RW_COLS), jnp.float32),
            pltpu.VMEM((C, RW_DIM), jnp.float32),
        ],
        compiler_params=pltpu.CompilerParams(dimension_semantics=("parallel", "arbitrary")),
        name="rwkv7_time_mix",
    )(z_rw, row(mu), row(w0), w2, row(a0), a2, g2, row(k_k), row(k_a), row(r_k), row(gn_g), row(gn_b), bd)


DSA_TQ = 256
MASK_NEG = -1e30
INT_MIN = -2 ** 31
KEY_NEG_INF = -2139095041
THRESH_BITS = 32


def _dsa_prep_kernel(z_ref, kvg_ref, kig_ref, kib_ref, qi_ref, kv_ref, ki_ref):
    z = z_ref[0]
    qi_ref[0] = z[:, :IDX_HEADS * IDX_DIM].astype(qi_ref.dtype)
    kv = z[:, 256:384]
    ms = jnp.mean(jnp.square(kv), -1, keepdims=True)
    kv_ref[0] = (kv * lax.rsqrt(ms + LN_EPS) * kvg_ref[...]).astype(kv_ref.dtype)
    ki = z[:, 384:448]
    mu = jnp.mean(ki, -1, keepdims=True)
    var = jnp.mean(jnp.square(ki - mu), -1, keepdims=True)
    ki_ref[0] = ((ki - mu) * lax.rsqrt(var + LN_EPS) * kig_ref[...] + kib_ref[...]).astype(ki_ref.dtype)


def _dsa_prep(z_small, kv_g, ki_g, ki_b, tm=512):
    B, S, W = z_small.shape
    return pl.pallas_call(
        _dsa_prep_kernel,
        grid=(B, S // tm),
        in_specs=[
            pl.BlockSpec((1, tm, W), lambda b, i: (b, i, 0)),
            pl.BlockSpec((1, DSA_LATENT), lambda b, i: (0, 0)),
            pl.BlockSpec((1, IDX_DIM), lambda b, i: (0, 0)),
            pl.BlockSpec((1, IDX_DIM), lambda b, i: (0, 0)),
        ],
        out_specs=[
            pl.BlockSpec((1, tm, IDX_HEADS * IDX_DIM), lambda b, i: (b, i, 0)),
            pl.BlockSpec((1, tm, DSA_LATENT), lambda b, i: (b, i, 0)),
            pl.BlockSpec((1, tm, IDX_DIM), lambda b, i: (b, i, 0)),
        ],
        out_shape=[
            jax.ShapeDtypeStruct((B, S, IDX_HEADS * IDX_DIM), MM_DTYPE),
            jax.ShapeDtypeStruct((B, S, DSA_LATENT), MM_DTYPE),
            jax.ShapeDtypeStruct((B, S, IDX_DIM), MM_DTYPE),
        ],
        name="dsa_prep",
    )(z_small, kv_g[None], ki_g[None], ki_b[None])


def _sortable_key(s):
    s = jnp.where(s == 0.0, 0.0, s)
    bits = pltpu.bitcast(s, jnp.int32)
    return bits ^ ((bits >> 31) & 0x7FFFFFFF)


def _col_count(mask_i32):
    tk, tq = mask_i32.shape
    return jnp.sum(mask_i32.reshape(tk // 8, 8, tq), axis=0)


def _dsa_kernel(bfar_ref, q_ref, qi_ref, wit_ref, kv_ref, kvt_ref, ki_ref, b0_ref, b1_ref, o_ref,
                key_ref, madd_ref, m_ref, l_ref, acc_ref, *, topk):
    tq = q_ref.shape[1]
    tk = tq
    i = pl.program_id(1)
    nj = i + 1
    f32 = jnp.float32
    krow = lax.broadcasted_iota(jnp.int32, (tk, tq), 0)
    qcol = lax.broadcasted_iota(jnp.int32, (tk, tq), 1)

    qi = qi_ref[0]
    wit = wit_ref[0] * (IDX_HEADS ** -0.5)

    def score_chunk(j, carry):
        off = pl.multiple_of(j * tk, tk)
        kc = ki_ref[0, pl.ds(off, tk), :]
        s = jnp.zeros((tk, tq), f32)
        for h in range(IDX_HEADS):
            d = lax.dot_general(kc, qi[:, h * IDX_DIM:(h + 1) * IDX_DIM],
                                (((1,), (1,)), ((), ())), preferred_element_type=f32)
            s = s + wit[h:h + 1, :] * jnp.maximum(d * (IDX_DIM ** -0.5), 0.0)
        causal = (krow + j * tk) <= (qcol + i * tq)
        s = jnp.where(causal, s, -jnp.inf)
        key_ref[j] = _sortable_key(s)
        return carry

    lax.fori_loop(0, nj, score_chunk, 0)

    def count_where(pred_fn):
        def body(j, acc):
            return acc + _col_count(pred_fn(key_ref[j], j).astype(jnp.int32))
        acc = lax.fori_loop(0, nj, body, jnp.zeros((8, tq), jnp.int32))
        return jnp.sum(acc, axis=0, keepdims=True)

    def bit_step(it, t_u):
        bit = THRESH_BITS - 1 - it
        cand_u = t_u | jnp.left_shift(jnp.int32(1), bit)
        cand = cand_u ^ INT_MIN
        cnt = count_where(lambda k, j: k >= cand)
        return jnp.where(cnt >= topk, cand_u, t_u)

    t_u = lax.fori_loop(0, THRESH_BITS, bit_step, jnp.zeros((1, tq), jnp.int32))
    thr = t_u ^ INT_MIN
    cnt_gt = count_where(lambda k, j: k > thr)
    cnt_ge = count_where(lambda k, j: k >= thr)
    is_neg = thr == KEY_NEG_INF
    need = jnp.logical_and(cnt_ge > topk, jnp.logical_not(is_neg))
    n_tie_take = topk - cnt_gt

    thr_open = jnp.where(is_neg, thr, thr - 1)
    any_need = jnp.max(need.astype(jnp.int32)) > 0

    @pl.when(jnp.logical_not(any_need))
    def _():
        def body(j, carry):
            madd_ref[j] = jnp.where(key_ref[j] > thr_open, 0.0, MASK_NEG)
            return carry
        lax.fori_loop(0, nj, body, 0)

    @pl.when(any_need)
    def _():
        s_len = tk * key_ref.shape[0]
        n_bits = max(1, int(math.ceil(math.log2(s_len))))

        def idx_step(it, p):
            bit = n_bits - 1 - it
            cand = p | jnp.left_shift(jnp.int32(1), bit)
            cnt = count_where(
                lambda k, j: jnp.where(k == thr, jnp.where((krow + j * tk) < cand, 1, 0), 0))
            return jnp.where(cnt < n_tie_take, cand, p)

        p_idx = lax.fori_loop(0, n_bits, idx_step, jnp.zeros((1, tq), jnp.int32))
        p_idx = jnp.where(need, p_idx, jnp.where(is_neg, -1, s_len))

        def body(j, carry):
            k = key_ref[j]
            tie_ok = jnp.where((krow + j * tk) <= p_idx, 0.0, MASK_NEG)
            madd_ref[j] = jnp.where(k > thr, 0.0, jnp.where(k == thr, tie_ok, MASK_NEG))
            return carry
        lax.fori_loop(0, nj, body, 0)

    m_ref[...] = jnp.full(m_ref.shape, MASK_NEG, f32)
    l_ref[...] = jnp.zeros(l_ref.shape, f32)
    acc_ref[...] = jnp.zeros(acc_ref.shape, f32)
    scale = DSA_LATENT ** -0.5

    def attend(j, bias_fn):
        off = pl.multiple_of(j * tk, tk)
        kc = kv_ref[0, pl.ds(off, tk), :]
        kct = kvt_ref[0, :, pl.ds(off, tk)]
        madd = madd_ref[j]
        for h in range(DSA_HEADS):
            qh = q_ref[0, :, h * DSA_LATENT:(h + 1) * DSA_LATENT]
            lg = lax.dot_general(kc, qh, (((1,), (1,)), ((), ())), preferred_element_type=f32)
            lg = lg * scale + bias_fn(h) + madd
            m_old = m_ref[h]
            m_new = jnp.maximum(m_old, jnp.max(lg, axis=0, keepdims=True))
            alpha = jnp.exp(m_old - m_new)
            p = jnp.exp(lg - m_new)
            l_ref[h] = alpha * l_ref[h] + jnp.sum(p, axis=0, keepdims=True)
            acc_ref[h] = alpha * acc_ref[h] + jnp.dot(kct, p.astype(kct.dtype),
                                                      preferred_element_type=f32)
            m_ref[h] = m_new

    def far_body(j, carry):
        attend(j, lambda h: bfar_ref[h])
        return carry

    lax.fori_loop(0, jnp.maximum(i - 1, 0), far_body, 0)

    @pl.when(i >= 1)
    def _():
        attend(i - 1, lambda h: b1_ref[h])

    attend(i, lambda h: b0_ref[h])

    for h in range(DSA_HEADS):
        out_t = acc_ref[h] / l_ref[h]
        o_ref[0, :, h * DSA_LATENT:(h + 1) * DSA_LATENT] = out_t.T.astype(o_ref.dtype)


def _dsa_bias_tiles(rel_bias, tq):
    dist = jnp.arange(2 * tq, dtype=jnp.int32)
    by_dist = rel_bias[_t5_bucket(dist)].T
    kk = jnp.arange(tq, dtype=jnp.int32)[:, None]
    qq = jnp.arange(tq, dtype=jnp.int32)[None, :]
    b0 = by_dist[:, jnp.maximum(qq - kk, 0)]
    b1 = by_dist[:, tq + qq - kk]
    return b0, b1, by_dist[:, -1]


def _dsa_attention(q, z_small, z_wi_t, kv_g, idx_k_g, idx_k_b, rel_bias, out_dtype):
    B, S, _ = q.shape
    tq = DSA_TQ
    topk = min(TOPK_MAX, S // 4)
    assert S % tq == 0 and tq >= REL_MAX_DIST and topk <= tq
    qi, ckv, kidx = _dsa_prep(z_small, kv_g, idx_k_g, idx_k_b)
    ckv_t = jnp.swapaxes(ckv, 1, 2)
    b0, b1, bfar = _dsa_bias_tiles(rel_bias, tq)
    nq = S // tq
    H, d = DSA_HEADS, DSA_LATENT
    grid_spec = pltpu.PrefetchScalarGridSpec(
        num_scalar_prefetch=0,
        grid=(B, nq),
        in_specs=[
            pl.BlockSpec(memory_space=pltpu.SMEM),
            pl.BlockSpec((1, tq, H * d), lambda b, i: (b, i, 0)),
            pl.BlockSpec((1, tq, IDX_HEADS * IDX_DIM), lambda b, i: (b, i, 0)),
            pl.BlockSpec((1, IDX_HEADS, tq), lambda b, i: (b, 0, i)),
            pl.BlockSpec((1, S, d), lambda b, i: (b, 0, 0)),
            pl.BlockSpec((1, d, S), lambda b, i: (b, 0, 0)),
            pl.BlockSpec((1, S, IDX_DIM), lambda b, i: (b, 0, 0)),
            pl.BlockSpec((H, tq, tq), lambda b, i: (0, 0, 0)),
            pl.BlockSpec((H, tq, tq), lambda b, i: (0, 0, 0)),
        ],
        out_specs=pl.BlockSpec((1, tq, H * d), lambda b, i: (b, i, 0)),
        scratch_shapes=[
            pltpu.VMEM((nq, tq, tq), jnp.int32),
            pltpu.VMEM((nq, tq, tq), jnp.float32),
            pltpu.VMEM((H, 1, tq), jnp.float32),
            pltpu.VMEM((H, 1, tq), jnp.float32),
            pltpu.VMEM((H, d, tq), jnp.float32),
        ],
    )
    return pl.pallas_call(
        partial(_dsa_kernel, topk=topk),
        grid_spec=grid_spec,
        out_shape=jax.ShapeDtypeStruct((B, S, H * d), out_dtype),
        compiler_params=pltpu.CompilerParams(vmem_limit_bytes=48 * 1024 * 1024),
        name="dsa_attention",
    )(bfar, q, qi, z_wi_t, ckv, ckv_t, kidx, b0, b1)


SC_CORES = 2
SC_SUBCORES = 16
SC_LANES = 16
PEER_SC_TOKENS = 8
PEER_SC_RING = 4
PEER_SC_UNROLL = 4
PEER_SLOTS = PEER_HEADS * PEER_TOPK


def _peer_sc_call(body, T, out_width, stage_width):
    mesh = plsc.VectorSubcoreMesh(core_axis_name="c", subcore_axis_name="s")
    return pl.kernel(
        body, mesh=mesh,
        out_type=jax.ShapeDtypeStruct((T, out_width), jnp.float32),
        scratch_types=[
            pltpu.VMEM((PEER_SC_TOKENS, PEER_SLOTS), jnp.int32),
            pltpu.VMEM((PEER_SC_TOKENS, stage_width), jnp.float32),
            pltpu.VMEM((PEER_SC_RING, PEER_TOPK, D_MODEL), jnp.float32),
            pltpu.VMEM((PEER_SC_TOKENS, out_width), jnp.float32),
            pltpu.SemaphoreType.DMA((PEER_SC_RING,)),
        ],
        compiler_params=pltpu.CompilerParams(needs_layout_passes=False),
    )


def _peer_sc_body(compute, zero_out, x_hbm, idx_hbm, tab_hbm, out_hbm, idx_v, x_v, rows_v, out_v, sems):
    T = idx_hbm.shape[0]
    tokens_per_worker = T // (SC_CORES * SC_SUBCORES)
    n_steps = PEER_SC_TOKENS * PEER_HEADS
    worker = lax.axis_index("s") * SC_CORES + lax.axis_index("c")
    base = worker * tokens_per_worker

    def gather(s, b):
        ids = idx_v[s // PEER_HEADS, pl.ds((s % PEER_HEADS) * PEER_TOPK, PEER_TOPK)]
        return pltpu.make_async_copy(tab_hbm.at[ids], rows_v.at[b], sems.at[b])

    @pl.loop(0, tokens_per_worker // PEER_SC_TOKENS)
    def _(blk):
        tok0 = base + blk * PEER_SC_TOKENS
        pltpu.sync_copy(idx_hbm.at[pl.ds(tok0, PEER_SC_TOKENS)], idx_v)
        pltpu.sync_copy(x_hbm.at[pl.ds(tok0, PEER_SC_TOKENS)], x_v)
        for b in range(PEER_SC_RING - 1):
            gather(b, b).start()
        if zero_out:
            @pl.loop(0, PEER_SC_TOKENS)
            def _(t):
                @plsc.parallel_loop(0, out_v.shape[1] // SC_LANES, unroll=PEER_SC_UNROLL)
                def _(c):
                    out_v[t, pl.ds(pl.multiple_of(c * SC_LANES, SC_LANES), SC_LANES)] = (
                        jnp.zeros((SC_LANES,), jnp.float32))

        @pl.loop(0, n_steps, step=PEER_SC_RING)
        def _(s0):
            for b in range(PEER_SC_RING):
                s = s0 + b
                gather(s, b).wait()

                @pl.when(s + PEER_SC_RING - 1 < n_steps)
                def _():
                    gather(s + PEER_SC_RING - 1, (b + PEER_SC_RING - 1) % PEER_SC_RING).start()

                compute(s // PEER_HEADS, s % PEER_HEADS, b, x_v, rows_v, out_v)

        pltpu.sync_copy(out_v, out_hbm.at[pl.ds(tok0, PEER_SC_TOKENS)])


def _peer_dots_compute(t, hd, b, h_v, rows_v, dots_v):
    lane = lax.iota(jnp.int32, SC_LANES)

    def col_step(c, accs):
        off = pl.multiple_of(c * SC_LANES, SC_LANES)
        hv = h_v[t, pl.ds(off, SC_LANES)]
        return tuple(accs[r] + rows_v[b, r, pl.ds(off, SC_LANES)] * hv for r in range(PEER_TOPK))

    accs = plsc.parallel_loop(
        0, D_MODEL // SC_LANES, unroll=PEER_SC_UNROLL,
        carry=tuple(jnp.zeros((SC_LANES,), jnp.float32) for _ in range(PEER_TOPK)))(col_step)
    res = jnp.zeros((SC_LANES,), jnp.float32)
    for r in range(PEER_TOPK):
        res = jnp.where(lane == r, jnp.sum(accs[r]), res)
    dots_v[t, pl.ds(hd * PEER_TOPK, PEER_TOPK)] = res


def _peer_mix_compute(t, hd, b, act_v, rows_v, out_v):
    tvec = jnp.full((SC_LANES,), t, jnp.int32)
    weights = [plsc.load_gather(act_v, [tvec, jnp.full((SC_LANES,), hd * PEER_TOPK + r, jnp.int32)])
               for r in range(PEER_TOPK)]

    @plsc.parallel_loop(0, D_MODEL // SC_LANES, unroll=PEER_SC_UNROLL)
    def _(c):
        off = pl.multiple_of(c * SC_LANES, SC_LANES)
        a = rows_v[b, 0, pl.ds(off, SC_LANES)] * weights[0]
        for r in range(1, PEER_TOPK):
            a = a + rows_v[b, r, pl.ds(off, SC_LANES)] * weights[r]
        plsc.addupdate(out_v.at[t, pl.ds(off, SC_LANES)], a)


def _peer_expert_dots(h, experts, u_tab):
    T = h.shape[0]
    assert T % (SC_CORES * SC_SUBCORES * PEER_SC_TOKENS) == 0
    assert (PEER_SC_TOKENS * PEER_HEADS) % PEER_SC_RING == 0 and PEER_TOPK == SC_LANES
    body = partial(_peer_sc_body, _peer_dots_compute, False)
    return _peer_sc_call(body, T, PEER_SLOTS, D_MODEL)(h, experts, u_tab)


def _peer_expert_mix(act, experts, v_tab):
    T = act.shape[0]
    assert T % (SC_CORES * SC_SUBCORES * PEER_SC_TOKENS) == 0
    body = partial(_peer_sc_body, _peer_mix_compute, True)
    return _peer_sc_call(body, T, D_MODEL, PEER_SLOTS)(act, experts, v_tab)


PEER_TM = 256


def _extract_topk(s, k, payload=None):
    R = s.shape[0]
    riota = lax.broadcasted_iota(jnp.int32, s.shape, 0)
    vals, rows = [], []
    for _ in range(k):
        m = jnp.max(s, axis=0, keepdims=True)
        pos = jnp.min(jnp.where(s == m, riota, R), axis=0, keepdims=True)
        hit = riota == pos
        vals.append(m)
        rows.append(pos if payload is None else jnp.max(jnp.where(hit, payload, -1), axis=0, keepdims=True))
        s = jnp.where(hit, -jnp.inf, s)
    return jnp.concatenate(vals, axis=0), jnp.concatenate(rows, axis=0)


def _peer_route_kernel(x_ref, sc_ref, sh_ref, wq_ref, keys_ref, h_ref, ex_ref, gate_ref, q_ref, ext_ref, gt_ref):
    K = PEER_TOPK
    f32 = jnp.float32
    h = x_ref[0] * (1.0 + sc_ref[0]) + sh_ref[0]
    h_ref[0] = h
    q = jnp.dot(h.astype(wq_ref.dtype), wq_ref[...], preferred_element_type=f32)
    for hd in range(PEER_HEADS):
        q_ref[hd] = q[:, hd * PEER_KEY_DIM:(hd + 1) * PEER_KEY_DIM].astype(q_ref.dtype)

    def head_body(hd, carry):
        qh = q_ref[hd]
        s1 = lax.dot_general(keys_ref[hd, 0], qh[:, :PEER_HALF], _NT, preferred_element_type=f32)
        s2 = lax.dot_general(keys_ref[hd, 1], qh[:, PEER_HALF:], _NT, preferred_element_type=f32)
        v1, i1 = _extract_topk(s1, K)
        v2, i2 = _extract_topk(s2, K)
        tm = v1.shape[1]
        cand = (v1[:, None, :] + v2[None, :, :]).reshape(K * K, tm)
        cidx = (i1[:, None, :] * PEER_N_KEYS + i2[None, :, :]).reshape(K * K, tm)
        top_s, experts = _extract_topk(cand, K, payload=cidx)
        e = jnp.exp(top_s - top_s[0:1, :])
        gt_ref[pl.ds(pl.multiple_of(hd * K, K), K), :] = e / jnp.sum(e, axis=0, keepdims=True)
        ext_ref[pl.ds(pl.multiple_of(hd * K, K), K), :] = experts
        return carry

    lax.fori_loop(0, PEER_HEADS, head_body, 0)
    ex_ref[0] = ext_ref[...].T
    gate_ref[0] = gt_ref[...].T


def _peer_route(x, sc, sh, w_pq, sub_keys):
    B, S, D = x.shape
    tm = PEER_TM
    return pl.pallas_call(
        _peer_route_kernel,
        grid=(B, S // tm),
        in_specs=[
            pl.BlockSpec((1, tm, D), lambda b, i: (b, i, 0)),
            pl.BlockSpec((1, 1, D), lambda b, i: (b, 0, 0)),
            pl.BlockSpec((1, 1, D), lambda b, i: (b, 0, 0)),
            pl.BlockSpec((D, PEER_HEADS * PEER_KEY_DIM), lambda b, i: (0, 0)),
            pl.BlockSpec((PEER_HEADS, 2, PEER_N_KEYS, PEER_HALF), lambda b, i: (0, 0, 0, 0)),
        ],
        out_specs=[
            pl.BlockSpec((1, tm, D), lambda b, i: (b, i, 0)),
            pl.BlockSpec((1, tm, PEER_SLOTS), lambda b, i: (b, i, 0)),
            pl.BlockSpec((1, tm, PEER_SLOTS), lambda b, i: (b, i, 0)),
        ],
        out_shape=[
            jax.ShapeDtypeStruct((B, S, D), jnp.float32),
            jax.ShapeDtypeStruct((B, S, PEER_SLOTS), jnp.int32),
            jax.ShapeDtypeStruct((B, S, PEER_SLOTS), jnp.float32),
        ],
        scratch_shapes=[
            pltpu.VMEM((PEER_HEADS, tm, PEER_KEY_DIM), MM_DTYPE),
            pltpu.VMEM((PEER_SLOTS, tm), jnp.int32),
            pltpu.VMEM((PEER_SLOTS, tm), jnp.float32),
        ],
        compiler_params=pltpu.CompilerParams(dimension_semantics=("parallel", "parallel")),
        name="peer_route",
    )(x, sc[:, None, :], sh[:, None, :], w_pq.astype(MM_DTYPE), sub_keys.astype(MM_DTYPE))


def _peer(x, sc, sh, w_pq, sub_keys, u_tab, v_tab):
    B, S, D = x.shape
    T = B * S
    h2, experts, gates = _peer_route(x, sc, sh, w_pq, sub_keys)
    experts = experts.reshape(T, PEER_SLOTS)
    pre = _peer_expert_dots(h2.reshape(T, D), experts, u_tab)
    act = jax.nn.gelu(pre, approximate=False) * gates.reshape(T, PEER_SLOTS)
    return _peer_expert_mix(act, experts, v_tab).reshape(B, S, D)


def kernel(x, c, w_ada, b_ada, w_in, rw_mu, rw_w0, rw_w2, rw_a0, rw_a2, rw_g2, rw_k_k, rw_k_a, rw_r_k, rw_gn_g, rw_gn_b, dsa_kv_g, idx_k_g, idx_k_b, rel_bias, w_br_a, w_br_b, w_out, ln1_g, ln1_b, peer_wq, peer_keys, peer_u, peer_v, ln2_g, ln2_b):
    l = 0
    mod = jax.nn.silu(c) @ w_ada[l] + b_ada[l]

    w_rw, w_q, w_kv, w_qi, w_ki, w_wi, w_ga, w_gb = jnp.split(w_in[l], _split_points(IN_SIZES), axis=-1)
    small_pad = jnp.zeros((D_MODEL, 512 - 256 - 128 - 64 - 4), w_in.dtype)
    w_small = jnp.concatenate([w_qi, w_kv, w_ki, w_wi, small_pad], axis=-1)
    w_gates = jnp.concatenate([w_ga, w_gb], axis=-1)

    def token_mix(x, mod):
        sh1, sc1, gt1 = jnp.split(mod, 6, axis=-1)[:3]
        z_rw = _mod_matmul(x, sc1, sh1, w_rw, jnp.float32, tn=896)
        z_q = _mod_matmul(x, sc1, sh1, w_q, MM_DTYPE, tn=1024)
        z_small = _mod_matmul(x, sc1, sh1, w_small, jnp.float32, tn=512)
        z_g = _mod_matmul(x, sc1, sh1, w_gates, jnp.float32, tn=1024)
        z_ga, z_gb = z_g[..., :D_MODEL], z_g[..., D_MODEL:]
        z_wi_t = jnp.swapaxes(z_small[..., 448:448 + IDX_HEADS], 1, 2)

        y_a = _rwkv7_time_mix(z_rw, rw_mu[l], rw_w0[l], rw_w2[l], rw_a0[l], rw_a2[l], rw_g2[l],
                              rw_k_k[l], rw_k_a[l], rw_r_k[l], rw_gn_g[l], rw_gn_b[l], jnp.float32) @ w_br_a[l]
        y_b = _dsa_attention(z_q, z_small, z_wi_t, dsa_kv_g[l], idx_k_g[l], idx_k_b[l],
                             rel_bias, jnp.float32) @ w_br_b[l]
        merged = jax.nn.sigmoid(z_ga) * y_a + jax.nn.sigmoid(z_gb) * y_b
        mix = merged @ w_out[l]
        return _layer_norm(DEEPNORM_ALPHA * x + gt1[:, None] * mix, ln1_g[l], ln1_b[l])

    def channel_mix(x, mod):
        sh2, sc2, gt2 = jnp.split(mod, 6, axis=-1)[3:]
        y2 = _peer(x, sc2, sh2, peer_wq[l], peer_keys[l], peer_u[l], peer_v[l])
        return y2, gt2

    gsz = x.shape[0] // BATCH_GROUPS
    outs = []
    pending = None
    for g in range(BATCH_GROUPS):
        mod_g = mod[g * gsz:(g + 1) * gsz]
        x1 = token_mix(x[g * gsz:(g + 1) * gsz], mod_g)
        if pending is not None:
            x1, pending = lax.optimization_barrier((x1, pending))
            outs.append(pending)
        y2, gt2 = channel_mix(x1, mod_g)
        pending = _layer_norm(DEEPNORM_ALPHA * x1 + gt2[:, None] * y2, ln2_g[l], ln2_b[l])
    outs.append(pending)
    return jnp.concatenate(outs, axis=0)
```

```python
import math
from functools import partial

import jax
import jax.numpy as jnp
import numpy as np
from jax import lax
from jax.experimental import pallas as pl
from jax.experimental.pallas import tpu as pltpu
from jax.experimental.pallas import tpu_sc as plsc

D_MODEL = 1024
RW_HEADS = 8
RW_HEAD_DIM = 64
RW_DIM = 512
RW_DECAY_LORA = 64
RW_A_LORA = 64
RW_GATE_LORA = 128
RW_SIZES = (RW_DIM, RW_DIM, RW_DIM, RW_DECAY_LORA, RW_A_LORA, RW_GATE_LORA)
RW_COLS = 3 * RW_DIM + RW_DECAY_LORA + RW_A_LORA + RW_GATE_LORA
RW_GN_EPS = 64e-5
DSA_HEADS = 8
DSA_LATENT = 128
DSA_Q_DIM = DSA_HEADS * DSA_LATENT
IDX_HEADS = 4
IDX_DIM = 64
TOPK_MAX = 256
Q_BLOCK = 128
REL_BUCKETS = 32
REL_MAX_DIST = 128
IN_SIZES = (RW_COLS, DSA_Q_DIM, DSA_LATENT, IDX_HEADS * IDX_DIM, IDX_DIM, IDX_HEADS, D_MODEL, D_MODEL)
IN_COLS = sum(IN_SIZES)
PEER_HEADS = 8
PEER_N_KEYS = 128
PEER_KEY_DIM = 128
PEER_HALF = 64
PEER_TOPK = 16
PEER_CHUNK = 128
LN_EPS = 1e-5
DEPTH = 1
DEEPNORM_ALPHA = (2.0 * DEPTH) ** 0.25

LANES = 128
MM_DTYPE = jnp.bfloat16
BATCH_GROUPS = 4


def _split_points(sizes):
    return np.cumsum(sizes)[:-1].tolist()


def _mod_matmul_kernel(x_ref, sc_ref, sh_ref, w_ref, o_ref):
    h = x_ref[0] * (1.0 + sc_ref[0]) + sh_ref[0]
    o_ref[0] = jnp.dot(h.astype(w_ref.dtype), w_ref[...],
                       preferred_element_type=jnp.float32).astype(o_ref.dtype)


def _mod_matmul(x, sc, sh, w, out_dtype, tn, tm=512):
    B, S, D = x.shape
    N = w.shape[1]
    return pl.pallas_call(
        _mod_matmul_kernel,
        grid=(B, S // tm, N // tn),
        in_specs=[
            pl.BlockSpec((1, tm, D), lambda b, i, j: (b, i, 0)),
            pl.BlockSpec((1, 1, D), lambda b, i, j: (b, 0, 0)),
            pl.BlockSpec((1, 1, D), lambda b, i, j: (b, 0, 0)),
            pl.BlockSpec((D, tn), lambda b, i, j: (0, j)),
        ],
        out_specs=pl.BlockSpec((1, tm, tn), lambda b, i, j: (b, i, j)),
        out_shape=jax.ShapeDtypeStruct((B, S, N), out_dtype),
        name="mod_matmul",
    )(x, sc[:, None, :], sh[:, None, :], w.astype(MM_DTYPE))


def _layer_norm(x, g, b):
    mu = jnp.mean(x, -1, keepdims=True)
    var = jnp.mean(jnp.square(x - mu), -1, keepdims=True)
    return (x - mu) * lax.rsqrt(var + LN_EPS) * g + b


def _t5_bucket(n):
    n = jnp.maximum(n, 0)
    max_exact = REL_BUCKETS // 2
    nf = jnp.maximum(n, 1).astype(jnp.float32)
    large = max_exact + (jnp.log(nf / max_exact) / math.log(REL_MAX_DIST / max_exact)
                         * (REL_BUCKETS - max_exact)).astype(jnp.int32)
    large = jnp.minimum(large, REL_BUCKETS - 1)
    return jnp.where(n < max_exact, n, large)


RW_CHUNK = 64
RW_INV_BLOCK = 16
_NN = (((1,), (0,)), ((), ()))
_NT = (((1,), (1,)), ((), ()))
_BNN = (((2,), (1,)), ((0,), (0,)))
_BNT = (((2,), (2,)), ((0,), (0,)))


def _dot_f32(a, b, dims=_NN):
    return lax.dot_general(a, b, dims, precision=lax.Precision.HIGHEST,
                           preferred_element_type=jnp.float32)


def _dot_bf16x3(a, b, dims=_NN):
    f32, bf = jnp.float32, jnp.bfloat16
    a_hi, b_hi = a.astype(bf), b.astype(bf)
    a_lo = (a - a_hi.astype(f32)).astype(bf)
    b_lo = (b - b_hi.astype(f32)).astype(bf)
    out = lax.dot_general(a_hi, b_hi, dims, preferred_element_type=f32)
    out = out + lax.dot_general(a_hi, b_lo, dims, preferred_element_type=f32)
    return out + lax.dot_general(a_lo, b_hi, dims, preferred_element_type=f32)


def _bf16_terms(a):
    f32, bf = jnp.float32, jnp.bfloat16
    hi = a.astype(bf)
    r1 = a - hi.astype(f32)
    mid = r1.astype(bf)
    lo = (r1 - mid.astype(f32)).astype(bf)
    return hi, mid, lo


def _dot_lhs_split(a, b01):
    b = b01.astype(jnp.bfloat16)
    return sum(jnp.dot(t, b, preferred_element_type=jnp.float32) for t in _bf16_terms(a))


def _dot_rhs_split(a01, b):
    a = a01.astype(jnp.bfloat16)
    return sum(jnp.dot(a, t, preferred_element_type=jnp.float32) for t in _bf16_terms(b))


def _rwkv_kernel(z_ref, mu_ref, w0_ref, w2_ref, a0_ref, a2_ref, g2_ref, kk_ref, ka_ref, rk_ref,
                 gng_ref, gnb_ref, bd_ref, o_ref, m_ref, prev_ref, y_ref):
    C = z_ref.shape[1]
    N = RW_HEAD_DIM
    f32 = jnp.float32
    dot3 = _dot_bf16x3

    @pl.when(pl.program_id(1) == 0)
    def _():
        m_ref[...] = jnp.zeros(m_ref.shape, f32)
        prev_ref[...] = jnp.zeros(prev_ref.shape, f32)

    z = z_ref[0]
    row = lax.broadcasted_iota(jnp.int32, z.shape, 0)
    shifted = jnp.where(row == 0, prev_ref[...], pltpu.roll(z, 1, axis=0))
    prev_ref[...] = z[C - 1:C, :]
    zs = z + (shifted - z) * mu_ref[...]
    r = zs[:, 0:RW_DIM]
    k = zs[:, RW_DIM:2 * RW_DIM]
    v = zs[:, 2 * RW_DIM:3 * RW_DIM]
    o1 = 3 * RW_DIM
    wl = zs[:, o1:o1 + RW_DECAY_LORA]
    al = zs[:, o1 + RW_DECAY_LORA:o1 + RW_DECAY_LORA + RW_A_LORA]
    gl = zs[:, o1 + RW_DECAY_LORA + RW_A_LORA:]

    bd = bd_ref[...]
    log_w = -jax.nn.softplus(-(w0_ref[...] + dot3(jnp.tanh(wl), w2_ref[...]))) - 0.5
    ldec = -jnp.exp(log_w)
    a_lr = jax.nn.sigmoid(a0_ref[...] + dot3(al, a2_ref[...]))
    g = dot3(jax.nn.sigmoid(gl), g2_ref[...])
    kk = k * kk_ref[...]
    kk = kk * lax.rsqrt(jnp.maximum(_dot_lhs_split(kk * kk, bd), 1e-24))
    k2 = k * (1.0 + (a_lr - 1.0) * ka_ref[...])
    a_vec = -kk
    b_vec = kk * a_lr

    ti = lax.broadcasted_iota(jnp.int32, (C, C), 0)
    tj = lax.broadcasted_iota(jnp.int32, (C, C), 1)
    cum = _dot_rhs_split((ti >= tj).astype(f32), ldec)
    cum_last = cum[C - 1:C, :]
    w_incl = jnp.exp(cum)
    w_excl = jnp.exp(cum - ldec)
    w_inv = jnp.exp(-cum)
    w_end = jnp.exp(cum_last - cum)
    w_all = jnp.exp(cum_last)
    a_t = a_vec * w_excl
    r_t = r * w_incl
    b_t = b_vec * w_inv
    k_t = k2 * w_inv
    b_e = b_vec * w_end
    k_e = k2 * w_end

    strict = ti > tj
    incl = ti >= tj
    bi, bj = ti // RW_INV_BLOCK, tj // RW_INV_BLOCK
    same_blk = bi == bj
    pair_blk = jnp.logical_and((bi // 2) == (bj // 2), jnp.logical_not(same_blk))
    half_blk = (bi // 2) != (bj // 2)
    eye = (ti == tj).astype(f32)

    H = RW_HEADS
    heads = lambda x: jnp.stack([x[:, h * N:(h + 1) * N] for h in range(H)], axis=0)
    bmm = lambda a, b: dot3(a, b, _BNN)
    A, Rt, Bt, Kt, Be, Ke, V = (heads(t) for t in (a_t, r_t, b_t, k_t, b_e, k_e, v))
    gm = dot3(jnp.concatenate([A, Rt], axis=1), jnp.concatenate([Bt, Kt], axis=1), _BNT)
    a_ab = jnp.where(strict, gm[:, :C, :C], 0.0)
    a_ak = jnp.where(strict, gm[:, :C, C:], 0.0)
    a_rb = jnp.where(incl, gm[:, C:, :C], 0.0)
    a_rk = jnp.where(incl, gm[:, C:, C:], 0.0)
    d1 = jnp.where(same_blk, a_ab, 0.0)
    xinv = eye + d1
    d2 = bmm(d1, d1)
    xinv = xinv + bmm(xinv, d2)
    d4 = bmm(d2, d2)
    xinv = xinv + bmm(xinv, d4)
    d8 = bmm(d4, d4)
    xinv = xinv + bmm(xinv, d8)
    xinv = xinv + bmm(bmm(xinv, jnp.where(pair_blk, a_ab, 0.0)), xinv)
    xinv = xinv + bmm(bmm(xinv, jnp.where(half_blk, a_ab, 0.0)), xinv)

    av = bmm(jnp.concatenate([a_ak, a_rk], axis=1), V)
    p = bmm(xinv, jnp.concatenate([A, av[:, :C]], axis=2))
    qm = bmm(a_rb, p)
    q1 = Rt + qm[:, :, :N]
    q2 = qm[:, :, N:] + av[:, C:]
    gmat = bmm(jnp.swapaxes(Be, 1, 2), p)
    g1 = eye * heads(w_all) + gmat[:, :, :N]
    g2 = gmat[:, :, N:] + bmm(jnp.swapaxes(Ke, 1, 2), V)
    m = m_ref[...]
    yh = _dot_f32(q1, m, _BNN) + q2
    m_ref[...] = _dot_f32(g1, m, _BNN) + g2
    for h in range(H):
        y_ref[:, h * N:(h + 1) * N] = yh[h]

    y = y_ref[...]
    mean = _dot_lhs_split(y, bd) * (1.0 / N)
    yc = y - mean
    var = _dot_lhs_split(yc * yc, bd) * (1.0 / N)
    yn = yc * lax.rsqrt(var + RW_GN_EPS) * gng_ref[...] + gnb_ref[...]
    bonus = _dot_lhs_split(r * k2 * rk_ref[...], bd) * v
    o_ref[0] = ((yn + bonus) * g).astype(o_ref.dtype)


def _rwkv7_time_mix(z_rw, mu, w0, w2, a0, a2, g2, k_k, k_a, r_k, gn_g, gn_b, out_dtype):
    B, S, _ = z_rw.shape
    C = RW_CHUNK
    assert S % C == 0 and C == RW_HEAD_DIM and C % (4 * RW_INV_BLOCK) == 0
    hid = jnp.arange(RW_DIM) // RW_HEAD_DIM
    bd = (hid[:, None] == hid[None, :]).astype(jnp.float32)
    row = lambda a: a.reshape(1, -1)
    full = lambda shape: pl.BlockSpec(shape, lambda b, c: (0,) * len(shape))
    return pl.pallas_call(
        _rwkv_kernel,
        grid=(B, S // C),
        in_specs=[
            pl.BlockSpec((1, C, RW_COLS), lambda b, c: (b, c, 0)),
            full((1, RW_COLS)), full((1, RW_DIM)), full((RW_DECAY_LORA, RW_DIM)), full((1, RW_DIM)),
            full((RW_A_LORA, RW_DIM)), full((RW_GATE_LORA, RW_DIM)), full((1, RW_DIM)), full((1, RW_DIM)),
            full((1, RW_DIM)), full((1, RW_DIM)), full((1, RW_DIM)), full((RW_DIM, RW_DIM)),
        ],
        out_specs=pl.BlockSpec((1, C, RW_DIM), lambda b, c: (b, c, 0)),
        out_shape=jax.ShapeDtypeStruct((B, S, RW_DIM), out_dtype),
        scratch_shapes=[
            pltpu.VMEM((RW_HEADS, RW_HEAD_DIM, RW_HEAD_DIM), jnp.float32),
            pltpu.VMEM((1, RW_COLS), jnp.float32),
            pltpu.VMEM((C, RW_DIM), jnp.float32),
        ],
        compiler_params=pltpu.CompilerParams(dimension_semantics=("parallel", "arbitrary")),
        name="rwkv7_time_mix",
    )(z_rw, row(mu), row(w0), w2, row(a0), a2, g2, row(k_k), row(k_a), row(r_k), row(gn_g), row(gn_b), bd)


DSA_TQ = 256
MASK_NEG = -1e30
INT_MIN = -2 ** 31
KEY_NEG_INF = -2139095041
THRESH_BITS = 32


def _dsa_prep_kernel(z_ref, kvg_ref, kig_ref, kib_ref, qi_ref, kv_ref, ki_ref):
    z = z_ref[0]
    qi_ref[0] = z[:, :IDX_HEADS * IDX_DIM].astype(qi_ref.dtype)
    kv = z[:, 256:384]
    ms = jnp.mean(jnp.square(kv), -1, keepdims=True)
    kv_ref[0] = (kv * lax.rsqrt(ms + LN_EPS) * kvg_ref[...]).astype(kv_ref.dtype)
    ki = z[:, 384:448]
    mu = jnp.mean(ki, -1, keepdims=True)
    var = jnp.mean(jnp.square(ki - mu), -1, keepdims=True)
    ki_ref[0] = ((ki - mu) * lax.rsqrt(var + LN_EPS) * kig_ref[...] + kib_ref[...]).astype(ki_ref.dtype)


def _dsa_prep(z_small, kv_g, ki_g, ki_b, tm=512):
    B, S, W = z_small.shape
    return pl.pallas_call(
        _dsa_prep_kernel,
        grid=(B, S // tm),
        in_specs=[
            pl.BlockSpec((1, tm, W), lambda b, i: (b, i, 0)),
            pl.BlockSpec((1, DSA_LATENT), lambda b, i: (0, 0)),
            pl.BlockSpec((1, IDX_DIM), lambda b, i: (0, 0)),
            pl.BlockSpec((1, IDX_DIM), lambda b, i: (0, 0)),
        ],
        out_specs=[
            pl.BlockSpec((1, tm, IDX_HEADS * IDX_DIM), lambda b, i: (b, i, 0)),
            pl.BlockSpec((1, tm, DSA_LATENT), lambda b, i: (b, i, 0)),
            pl.BlockSpec((1, tm, IDX_DIM), lambda b, i: (b, i, 0)),
        ],
        out_shape=[
            jax.ShapeDtypeStruct((B, S, IDX_HEADS * IDX_DIM), MM_DTYPE),
            jax.ShapeDtypeStruct((B, S, DSA_LATENT), MM_DTYPE),
            jax.ShapeDtypeStruct((B, S, IDX_DIM), MM_DTYPE),
        ],
        name="dsa_prep",
    )(z_small, kv_g[None], ki_g[None], ki_b[None])


def _sortable_key(s):
    s = jnp.where(s == 0.0, 0.0, s)
    bits = pltpu.bitcast(s, jnp.int32)
    return bits ^ ((bits >> 31) & 0x7FFFFFFF)


def _col_count(mask_i32):
    tk, tq = mask_i32.shape
    return jnp.sum(mask_i32.reshape(tk // 8, 8, tq), axis=0)


def _dsa_kernel(bfar_ref, q_ref, qi_ref, wit_ref, kv_ref, kvt_ref, ki_ref, b0_ref, b1_ref, o_ref,
                key_ref, madd_ref, m_ref, l_ref, acc_ref, *, topk):
    tq = q_ref.shape[1]
    tk = tq
    i = pl.program_id(1)
    nj = i + 1
    f32 = jnp.float32
    krow = lax.broadcasted_iota(jnp.int32, (tk, tq), 0)
    qcol = lax.broadcasted_iota(jnp.int32, (tk, tq), 1)

    qi = qi_ref[0]
    wit = wit_ref[0] * (IDX_HEADS ** -0.5)

    def score_chunk(j, carry):
        off = pl.multiple_of(j * tk, tk)
        kc = ki_ref[0, pl.ds(off, tk), :]
        s = jnp.zeros((tk, tq), f32)
        for h in range(IDX_HEADS):
            d = lax.dot_general(kc, qi[:, h * IDX_DIM:(h + 1) * IDX_DIM],
                                (((1,), (1,)), ((), ())), preferred_element_type=f32)
            s = s + wit[h:h + 1, :] * jnp.maximum(d * (IDX_DIM ** -0.5), 0.0)
        causal = (krow + j * tk) <= (qcol + i * tq)
        s = jnp.where(causal, s, -jnp.inf)
        key_ref[j] = _sortable_key(s)
        return carry

    lax.fori_loop(0, nj, score_chunk, 0)

    def count_where(pred_fn):
        def body(j, acc):
            return acc + _col_count(pred_fn(key_ref[j], j).astype(jnp.int32))
        acc = lax.fori_loop(0, nj, body, jnp.zeros((8, tq), jnp.int32))
        return jnp.sum(acc, axis=0, keepdims=True)

    def bit_step(it, t_u):
        bit = THRESH_BITS - 1 - it
        cand_u = t_u | jnp.left_shift(jnp.int32(1), bit)
        cand = cand_u ^ INT_MIN
        cnt = count_where(lambda k, j: k >= cand)
        return jnp.where(cnt >= topk, cand_u, t_u)

    t_u = lax.fori_loop(0, THRESH_BITS, bit_step, jnp.zeros((1, tq), jnp.int32))
    thr = t_u ^ INT_MIN
    cnt_gt = count_where(lambda k, j: k > thr)
    cnt_ge = count_where(lambda k, j: k >= thr)
    is_neg = thr == KEY_NEG_INF
    need = jnp.logical_and(cnt_ge > topk, jnp.logical_not(is_neg))
    n_tie_take = topk - cnt_gt

    thr_open = jnp.where(is_neg, thr, thr - 1)
    any_need = jnp.max(need.astype(jnp.int32)) > 0

    @pl.when(jnp.logical_not(any_need))
    def _():
        def body(j, carry):
            madd_ref[j] = jnp.where(key_ref[j] > thr_open, 0.0, MASK_NEG)
            return carry
        lax.fori_loop(0, nj, body, 0)

    @pl.when(any_need)
    def _():
        s_len = tk * key_ref.shape[0]
        n_bits = max(1, int(math.ceil(math.log2(s_len))))

        def idx_step(it, p):
            bit = n_bits - 1 - it
            cand = p | jnp.left_shift(jnp.int32(1), bit)
            cnt = count_where(
                lambda k, j: jnp.where(k == thr, jnp.where((krow + j * tk) < cand, 1, 0), 0))
            return jnp.where(cnt < n_tie_take, cand, p)

        p_idx = lax.fori_loop(0, n_bits, idx_step, jnp.zeros((1, tq), jnp.int32))
        p_idx = jnp.where(need, p_idx, jnp.where(is_neg, -1, s_len))

        def body(j, carry):
            k = key_ref[j]
            tie_ok = jnp.where((krow + j * tk) <= p_idx, 0.0, MASK_NEG)
            madd_ref[j] = jnp.where(k > thr, 0.0, jnp.where(k == thr, tie_ok, MASK_NEG))
            return carry
        lax.fori_loop(0, nj, body, 0)

    m_ref[...] = jnp.full(m_ref.shape, MASK_NEG, f32)
    l_ref[...] = jnp.zeros(l_ref.shape, f32)
    acc_ref[...] = jnp.zeros(acc_ref.shape, f32)
    scale = DSA_LATENT ** -0.5

    def attend(j, bias_fn):
        off = pl.multiple_of(j * tk, tk)
        kc = kv_ref[0, pl.ds(off, tk), :]
        kct = kvt_ref[0, :, pl.ds(off, tk)]
        madd = madd_ref[j]
        for h in range(DSA_HEADS):
            qh = q_ref[0, :, h * DSA_LATENT:(h + 1) * DSA_LATENT]
            lg = lax.dot_general(kc, qh, (((1,), (1,)), ((), ())), preferred_element_type=f32)
            lg = lg * scale + bias_fn(h) + madd
            m_old = m_ref[h]
            m_new = jnp.maximum(m_old, jnp.max(lg, axis=0, keepdims=True))
            alpha = jnp.exp(m_old - m_new)
            p = jnp.exp(lg - m_new)
            l_ref[h] = alpha * l_ref[h] + jnp.sum(p, axis=0, keepdims=True)
            acc_ref[h] = alpha * acc_ref[h] + jnp.dot(kct, p.astype(kct.dtype),
                                                      preferred_element_type=f32)
            m_ref[h] = m_new

    def far_body(j, carry):
        attend(j, lambda h: bfar_ref[h])
        return carry

    lax.fori_loop(0, jnp.maximum(i - 1, 0), far_body, 0)

    @pl.when(i >= 1)
    def _():
        attend(i - 1, lambda h: b1_ref[h])

    attend(i, lambda h: b0_ref[h])

    for h in range(DSA_HEADS):
        out_t = acc_ref[h] / l_ref[h]
        o_ref[0, :, h * DSA_LATENT:(h + 1) * DSA_LATENT] = out_t.T.astype(o_ref.dtype)


def _dsa_bias_tiles(rel_bias, tq):
    dist = jnp.arange(2 * tq, dtype=jnp.int32)
    by_dist = rel_bias[_t5_bucket(dist)].T
    kk = jnp.arange(tq, dtype=jnp.int32)[:, None]
    qq = jnp.arange(tq, dtype=jnp.int32)[None, :]
    b0 = by_dist[:, jnp.maximum(qq - kk, 0)]
    b1 = by_dist[:, tq + qq - kk]
    return b0, b1, by_dist[:, -1]


def _dsa_attention(q, z_small, z_wi_t, kv_g, idx_k_g, idx_k_b, rel_bias, out_dtype):
    B, S, _ = q.shape
    tq = DSA_TQ
    topk = min(TOPK_MAX, S // 4)
    assert S % tq == 0 and tq >= REL_MAX_DIST and topk <= tq
    qi, ckv, kidx = _dsa_prep(z_small, kv_g, idx_k_g, idx_k_b)
    ckv_t = jnp.swapaxes(ckv, 1, 2)
    b0, b1, bfar = _dsa_bias_tiles(rel_bias, tq)
    nq = S // tq
    H, d = DSA_HEADS, DSA_LATENT
    grid_spec = pltpu.PrefetchScalarGridSpec(
        num_scalar_prefetch=0,
        grid=(B, nq),
        in_specs=[
            pl.BlockSpec(memory_space=pltpu.SMEM),
            pl.BlockSpec((1, tq, H * d), lambda b, i: (b, i, 0)),
            pl.BlockSpec((1, tq, IDX_HEADS * IDX_DIM), lambda b, i: (b, i, 0)),
            pl.BlockSpec((1, IDX_HEADS, tq), lambda b, i: (b, 0, i)),
            pl.BlockSpec((1, S, d), lambda b, i: (b, 0, 0)),
            pl.BlockSpec((1, d, S), lambda b, i: (b, 0, 0)),
            pl.BlockSpec((1, S, IDX_DIM), lambda b, i: (b, 0, 0)),
            pl.BlockSpec((H, tq, tq), lambda b, i: (0, 0, 0)),
            pl.BlockSpec((H, tq, tq), lambda b, i: (0, 0, 0)),
        ],
        out_specs=pl.BlockSpec((1, tq, H * d), lambda b, i: (b, i, 0)),
        scratch_shapes=[
            pltpu.VMEM((nq, tq, tq), jnp.int32),
            pltpu.VMEM((nq, tq, tq), jnp.float32),
            pltpu.VMEM((H, 1, tq), jnp.float32),
            pltpu.VMEM((H, 1, tq), jnp.float32),
            pltpu.VMEM((H, d, tq), jnp.float32),
        ],
    )
    return pl.pallas_call(
        partial(_dsa_kernel, topk=topk),
        grid_spec=grid_spec,
        out_shape=jax.ShapeDtypeStruct((B, S, H * d), out_dtype),
        compiler_params=pltpu.CompilerParams(vmem_limit_bytes=48 * 1024 * 1024),
        name="dsa_attention",
    )(bfar, q, qi, z_wi_t, ckv, ckv_t, kidx, b0, b1)


SC_CORES = 2
SC_SUBCORES = 16
SC_LANES = 16
PEER_SC_TOKENS = 8
PEER_SC_RING = 4
PEER_SC_UNROLL = 4
PEER_SLOTS = PEER_HEADS * PEER_TOPK


def _peer_sc_call(body, T, out_width, stage_width):
    mesh = plsc.VectorSubcoreMesh(core_axis_name="c", subcore_axis_name="s")
    return pl.kernel(
        body, mesh=mesh,
        out_type=jax.ShapeDtypeStruct((T, out_width), jnp.float32),
        scratch_types=[
            pltpu.VMEM((PEER_SC_TOKENS, PEER_SLOTS), jnp.int32),
            pltpu.VMEM((PEER_SC_TOKENS, stage_width), jnp.float32),
            pltpu.VMEM((PEER_SC_RING, PEER_TOPK, D_MODEL), jnp.float32),
            pltpu.VMEM((PEER_SC_TOKENS, out_width), jnp.float32),
            pltpu.SemaphoreType.DMA((PEER_SC_RING,)),
        ],
        compiler_params=pltpu.CompilerParams(needs_layout_passes=False),
    )


def _peer_sc_body(compute, zero_out, x_hbm, idx_hbm, tab_hbm, out_hbm, idx_v, x_v, rows_v, out_v, sems):
    T = idx_hbm.shape[0]
    tokens_per_worker = T // (SC_CORES * SC_SUBCORES)
    n_steps = PEER_SC_TOKENS * PEER_HEADS
    worker = lax.axis_index("s") * SC_CORES + lax.axis_index("c")
    base = worker * tokens_per_worker

    def gather(s, b):
        ids = idx_v[s // PEER_HEADS, pl.ds((s % PEER_HEADS) * PEER_TOPK, PEER_TOPK)]
        return pltpu.make_async_copy(tab_hbm.at[ids], rows_v.at[b], sems.at[b])

    @pl.loop(0, tokens_per_worker // PEER_SC_TOKENS)
    def _(blk):
        tok0 = base + blk * PEER_SC_TOKENS
        pltpu.sync_copy(idx_hbm.at[pl.ds(tok0, PEER_SC_TOKENS)], idx_v)
        pltpu.sync_copy(x_hbm.at[pl.ds(tok0, PEER_SC_TOKENS)], x_v)
        for b in range(PEER_SC_RING - 1):
            gather(b, b).start()
        if zero_out:
            @pl.loop(0, PEER_SC_TOKENS)
            def _(t):
                @plsc.parallel_loop(0, out_v.shape[1] // SC_LANES, unroll=PEER_SC_UNROLL)
                def _(c):
                    out_v[t, pl.ds(pl.multiple_of(c * SC_LANES, SC_LANES), SC_LANES)] = (
                        jnp.zeros((SC_LANES,), jnp.float32))

        @pl.loop(0, n_steps, step=PEER_SC_RING)
        def _(s0):
            for b in range(PEER_SC_RING):
                s = s0 + b
                gather(s, b).wait()

                @pl.when(s + PEER_SC_RING - 1 < n_steps)
                def _():
                    gather(s + PEER_SC_RING - 1, (b + PEER_SC_RING - 1) % PEER_SC_RING).start()

                compute(s // PEER_HEADS, s % PEER_HEADS, b, x_v, rows_v, out_v)

        pltpu.sync_copy(out_v, out_hbm.at[pl.ds(tok0, PEER_SC_TOKENS)])


def _peer_dots_compute(t, hd, b, h_v, rows_v, dots_v):
    lane = lax.iota(jnp.int32, SC_LANES)

    def col_step(c, accs):
        off = pl.multiple_of(c * SC_LANES, SC_LANES)
        hv = h_v[t, pl.ds(off, SC_LANES)]
        return tuple(accs[r] + rows_v[b, r, pl.ds(off, SC_LANES)] * hv for r in range(PEER_TOPK))

    accs = plsc.parallel_loop(
        0, D_MODEL // SC_LANES, unroll=PEER_SC_UNROLL,
        carry=tuple(jnp.zeros((SC_LANES,), jnp.float32) for _ in range(PEER_TOPK)))(col_step)
    res = jnp.zeros((SC_LANES,), jnp.float32)
    for r in range(PEER_TOPK):
        res = jnp.where(lane == r, jnp.sum(accs[r]), res)
    dots_v[t, pl.ds(hd * PEER_TOPK, PEER_TOPK)] = res


def _peer_mix_compute(t, hd, b, act_v, rows_v, out_v):
    tvec = jnp.full((SC_LANES,), t, jnp.int32)
    weights = [plsc.load_gather(act_v, [tvec, jnp.full((SC_LANES,), hd * PEER_TOPK + r, jnp.int32)])
               for r in range(PEER_TOPK)]

    @plsc.parallel_loop(0, D_MODEL // SC_LANES, unroll=PEER_SC_UNROLL)
    def _(c):
        off = pl.multiple_of(c * SC_LANES, SC_LANES)
        a = rows_v[b, 0, pl.ds(off, SC_LANES)] * weights[0]
        for r in range(1, PEER_TOPK):
            a = a + rows_v[b, r, pl.ds(off, SC_LANES)] * weights[r]
        plsc.addupdate(out_v.at[t, pl.ds(off, SC_LANES)], a)


def _peer_expert_dots(h, experts, u_tab):
    T = h.shape[0]
    assert T % (SC_CORES * SC_SUBCORES * PEER_SC_TOKENS) == 0
    assert (PEER_SC_TOKENS * PEER_HEADS) % PEER_SC_RING == 0 and PEER_TOPK == SC_LANES
    body = partial(_peer_sc_body, _peer_dots_compute, False)
    return _peer_sc_call(body, T, PEER_SLOTS, D_MODEL)(h, experts, u_tab)


def _peer_expert_mix(act, experts, v_tab):
    T = act.shape[0]
    assert T % (SC_CORES * SC_SUBCORES * PEER_SC_TOKENS) == 0
    body = partial(_peer_sc_body, _peer_mix_compute, True)
    return _peer_sc_call(body, T, D_MODEL, PEER_SLOTS)(act, experts, v_tab)


PEER_TM = 256


def _extract_topk(s, k, payload=None):
    R = s.shape[0]
    riota = lax.broadcasted_iota(jnp.int32, s.shape, 0)
    vals, rows = [], []
    for _ in range(k):
        m = jnp.max(s, axis=0, keepdims=True)
        pos = jnp.min(jnp.where(s == m, riota, R), axis=0, keepdims=True)
        hit = riota == pos
        vals.append(m)
        rows.append(pos if payload is None else jnp.max(jnp.where(hit, payload, -1), axis=0, keepdims=True))
        s = jnp.where(hit, -jnp.inf, s)
    return jnp.concatenate(vals, axis=0), jnp.concatenate(rows, axis=0)


def _peer_route_kernel(x_ref, sc_ref, sh_ref, wq_ref, keys_ref, h_ref, ex_ref, gate_ref, q_ref, ext_ref, gt_ref):
    K = PEER_TOPK
    f32 = jnp.float32
    h = x_ref[0] * (1.0 + sc_ref[0]) + sh_ref[0]
    h_ref[0] = h
    q = jnp.dot(h.astype(wq_ref.dtype), wq_ref[...], preferred_element_type=f32)
    for hd in range(PEER_HEADS):
        q_ref[hd] = q[:, hd * PEER_KEY_DIM:(hd + 1) * PEER_KEY_DIM].astype(q_ref.dtype)

    def head_body(hd, carry):
        qh = q_ref[hd]
        s1 = lax.dot_general(keys_ref[hd, 0], qh[:, :PEER_HALF], _NT, preferred_element_type=f32)
        s2 = lax.dot_general(keys_ref[hd, 1], qh[:, PEER_HALF:], _NT, preferred_element_type=f32)
        v1, i1 = _extract_topk(s1, K)
        v2, i2 = _extract_topk(s2, K)
        tm = v1.shape[1]
        cand_rows, cidx_rows = [], []
        for a in range(K):
            nb = K // (a + 1)
            cand_rows.append(v1[a:a + 1, :] + v2[:nb, :])
            cidx_rows.append(i1[a:a + 1, :] * PEER_N_KEYS + i2[:nb, :])
        n_pad = -sum(r.shape[0] for r in cand_rows) % 8
        cand_rows.append(jnp.full((n_pad, tm), -jnp.inf, f32))
        cidx_rows.append(jnp.full((n_pad, tm), -1, jnp.int32))
        cand = jnp.concatenate(cand_rows, axis=0)
        cidx = jnp.concatenate(cidx_rows, axis=0)
        top_s, experts = _extract_topk(cand, K, payload=cidx)
        e = jnp.exp(top_s - top_s[0:1, :])
        gt_ref[pl.ds(pl.multiple_of(hd * K, K), K), :] = e / jnp.sum(e, axis=0, keepdims=True)
        ext_ref[pl.ds(pl.multiple_of(hd * K, K), K), :] = experts
        return carry

    lax.fori_loop(0, PEER_HEADS, head_body, 0)
    ex_ref[0] = ext_ref[...].T
    gate_ref[0] = gt_ref[...].T


def _peer_route(x, sc, sh, w_pq, sub_keys):
    B, S, D = x.shape
    tm = PEER_TM
    return pl.pallas_call(
        _peer_route_kernel,
        grid=(B, S // tm),
        in_specs=[
            pl.BlockSpec((1, tm, D), lambda b, i: (b, i, 0)),
            pl.BlockSpec((1, 1, D), lambda b, i: (b, 0, 0)),
            pl.BlockSpec((1, 1, D), lambda b, i: (b, 0, 0)),
            pl.BlockSpec((D, PEER_HEADS * PEER_KEY_DIM), lambda b, i: (0, 0)),
            pl.BlockSpec((PEER_HEADS, 2, PEER_N_KEYS, PEER_HALF), lambda b, i: (0, 0, 0, 0)),
        ],
        out_specs=[
            pl.BlockSpec((1, tm, D), lambda b, i: (b, i, 0)),
            pl.BlockSpec((1, tm, PEER_SLOTS), lambda b, i: (b, i, 0)),
            pl.BlockSpec((1, tm, PEER_SLOTS), lambda b, i: (b, i, 0)),
        ],
        out_shape=[
            jax.ShapeDtypeStruct((B, S, D), jnp.float32),
            jax.ShapeDtypeStruct((B, S, PEER_SLOTS), jnp.int32),
            jax.ShapeDtypeStruct((B, S, PEER_SLOTS), jnp.float32),
        ],
        scratch_shapes=[
            pltpu.VMEM((PEER_HEADS, tm, PEER_KEY_DIM), MM_DTYPE),
            pltpu.VMEM((PEER_SLOTS, tm), jnp.int32),
            pltpu.VMEM((PEER_SLOTS, tm), jnp.float32),
        ],
        compiler_params=pltpu.CompilerParams(dimension_semantics=("parallel", "parallel")),
        name="peer_route",
    )(x, sc[:, None, :], sh[:, None, :], w_pq.astype(MM_DTYPE), sub_keys.astype(MM_DTYPE))


def _peer_experts(h2, experts, gates, u_tab, v_tab):
    B, S, D = h2.shape
    T = B * S
    experts = experts.reshape(T, PEER_SLOTS)
    pre = _peer_expert_dots(h2.reshape(T, D), experts, u_tab)
    act = jax.nn.gelu(pre, approximate=False) * gates.reshape(T, PEER_SLOTS)
    return _peer_expert_mix(act, experts, v_tab).reshape(B, S, D)


def kernel(x, c, w_ada, b_ada, w_in, rw_mu, rw_w0, rw_w2, rw_a0, rw_a2, rw_g2, rw_k_k, rw_k_a, rw_r_k, rw_gn_g, rw_gn_b, dsa_kv_g, idx_k_g, idx_k_b, rel_bias, w_br_a, w_br_b, w_out, ln1_g, ln1_b, peer_wq, peer_keys, peer_u, peer_v, ln2_g, ln2_b):
    l = 0
    mod = jax.nn.silu(c) @ w_ada[l] + b_ada[l]

    w_rw, w_q, w_kv, w_qi, w_ki, w_wi, w_ga, w_gb = jnp.split(w_in[l], _split_points(IN_SIZES), axis=-1)
    small_pad = jnp.zeros((D_MODEL, 512 - 256 - 128 - 64 - 4), w_in.dtype)
    w_small = jnp.concatenate([w_qi, w_kv, w_ki, w_wi, small_pad], axis=-1)
    w_gates = jnp.concatenate([w_ga, w_gb], axis=-1)

    def token_mix(x, mod):
        sh1, sc1, gt1 = jnp.split(mod, 6, axis=-1)[:3]
        z_rw = _mod_matmul(x, sc1, sh1, w_rw, jnp.float32, tn=896)
        z_q = _mod_matmul(x, sc1, sh1, w_q, MM_DTYPE, tn=1024)
        z_small = _mod_matmul(x, sc1, sh1, w_small, jnp.float32, tn=512)
        z_g = _mod_matmul(x, sc1, sh1, w_gates, jnp.float32, tn=1024)
        z_ga, z_gb = z_g[..., :D_MODEL], z_g[..., D_MODEL:]
        z_wi_t = jnp.swapaxes(z_small[..., 448:448 + IDX_HEADS], 1, 2)

        y_a = _rwkv7_time_mix(z_rw, rw_mu[l], rw_w0[l], rw_w2[l], rw_a0[l], rw_a2[l], rw_g2[l],
                              rw_k_k[l], rw_k_a[l], rw_r_k[l], rw_gn_g[l], rw_gn_b[l], jnp.float32) @ w_br_a[l]
        y_b = _dsa_attention(z_q, z_small, z_wi_t, dsa_kv_g[l], idx_k_g[l], idx_k_b[l],
                             rel_bias, jnp.float32) @ w_br_b[l]
        merged = jax.nn.sigmoid(z_ga) * y_a + jax.nn.sigmoid(z_gb) * y_b
        mix = merged @ w_out[l]
        return _layer_norm(DEEPNORM_ALPHA * x + gt1[:, None] * mix, ln1_g[l], ln1_b[l])

    gsz = x.shape[0] // BATCH_GROUPS
    outs = []
    pending = routed = None
    for g in range(BATCH_GROUPS):
        mod_g = mod[g * gsz:(g + 1) * gsz]
        sh2, sc2, gt2 = jnp.split(mod_g, 6, axis=-1)[3:]
        x_g = x[g * gsz:(g + 1) * gsz]
        if routed is not None:
            x_g, routed = lax.optimization_barrier((x_g, routed))
        x1 = token_mix(x_g, mod_g)
        if pending is not None:
            x1, pending = lax.optimization_barrier((x1, pending))
            outs.append(pending)
        h2, experts, gates = _peer_route(x1, sc2, sh2, peer_wq[l], peer_keys[l])
        routed = experts
        y2 = _peer_experts(h2, experts, gates, peer_u[l], peer_v[l])
        pending = _layer_norm(DEEPNORM_ALPHA * x1 + gt2[:, None] * y2, ln2_g[l], ln2_b[l])
    outs.append(pending)
    return jnp.concatenate(outs, axis=0)
```

```python
import math
from functools import partial

import jax
import jax.numpy as jnp
import numpy as np
from jax import lax
from jax.experimental import pallas as pl
from jax.experimental.pallas import tpu as pltpu
from jax.experimental.pallas import tpu_sc as plsc

D_MODEL = 1024
RW_HEADS = 8
RW_HEAD_DIM = 64
RW_DIM = 512
RW_DECAY_LORA = 64
RW_A_LORA = 64
RW_GATE_LORA = 128
RW_SIZES = (RW_DIM, RW_DIM, RW_DIM, RW_DECAY_LORA, RW_A_LORA, RW_GATE_LORA)
RW_COLS = 3 * RW_DIM + RW_DECAY_LORA + RW_A_LORA + RW_GATE_LORA
RW_GN_EPS = 64e-5
DSA_HEADS = 8
DSA_LATENT = 128
DSA_Q_DIM = DSA_HEADS * DSA_LATENT
IDX_HEADS = 4
IDX_DIM = 64
TOPK_MAX = 256
Q_BLOCK = 128
REL_BUCKETS = 32
REL_MAX_DIST = 128
IN_SIZES = (RW_COLS, DSA_Q_DIM, DSA_LATENT, IDX_HEADS * IDX_DIM, IDX_DIM, IDX_HEADS, D_MODEL, D_MODEL)
IN_COLS = sum(IN_SIZES)
PEER_HEADS = 8
PEER_N_KEYS = 128
PEER_KEY_DIM = 128
PEER_HALF = 64
PEER_TOPK = 16
PEER_CHUNK = 128
LN_EPS = 1e-5
DEPTH = 1
DEEPNORM_ALPHA = (2.0 * DEPTH) ** 0.25

LANES = 128
MM_DTYPE = jnp.bfloat16
BATCH_GROUPS = 4


def _split_points(sizes):
    return np.cumsum(sizes)[:-1].tolist()


def _mod_matmul_kernel(x_ref, sc_ref, sh_ref, w_ref, o_ref):
    h = x_ref[0] * (1.0 + sc_ref[0]) + sh_ref[0]
    o_ref[0] = jnp.dot(h.astype(w_ref.dtype), w_ref[...],
                       preferred_element_type=jnp.float32).astype(o_ref.dtype)


def _mod_matmul(x, sc, sh, w, out_dtype, tn, tm=512):
    B, S, D = x.shape
    N = w.shape[1]
    return pl.pallas_call(
        _mod_matmul_kernel,
        grid=(B, S // tm, N // tn),
        in_specs=[
            pl.BlockSpec((1, tm, D), lambda b, i, j: (b, i, 0)),
            pl.BlockSpec((1, 1, D), lambda b, i, j: (b, 0, 0)),
            pl.BlockSpec((1, 1, D), lambda b, i, j: (b, 0, 0)),
            pl.BlockSpec((D, tn), lambda b, i, j: (0, j)),
        ],
        out_specs=pl.BlockSpec((1, tm, tn), lambda b, i, j: (b, i, j)),
        out_shape=jax.ShapeDtypeStruct((B, S, N), out_dtype),
        name="mod_matmul",
    )(x, sc[:, None, :], sh[:, None, :], w.astype(MM_DTYPE))


def _layer_norm(x, g, b):
    mu = jnp.mean(x, -1, keepdims=True)
    var = jnp.mean(jnp.square(x - mu), -1, keepdims=True)
    return (x - mu) * lax.rsqrt(var + LN_EPS) * g + b


def _t5_bucket(n):
    n = jnp.maximum(n, 0)
    max_exact = REL_BUCKETS // 2
    nf = jnp.maximum(n, 1).astype(jnp.float32)
    large = max_exact + (jnp.log(nf / max_exact) / math.log(REL_MAX_DIST / max_exact)
                         * (REL_BUCKETS - max_exact)).astype(jnp.int32)
    large = jnp.minimum(large, REL_BUCKETS - 1)
    return jnp.where(n < max_exact, n, large)


RW_CHUNK = 64
RW_INV_BLOCK = 16
_NN = (((1,), (0,)), ((), ()))
_NT = (((1,), (1,)), ((), ()))
_BNN = (((2,), (1,)), ((0,), (0,)))
_BNT = (((2,), (2,)), ((0,), (0,)))


def _dot_f32(a, b, dims=_NN):
    return lax.dot_general(a, b, dims, precision=lax.Precision.HIGHEST,
                           preferred_element_type=jnp.float32)


def _dot_bf16x3(a, b, dims=_NN):
    f32, bf = jnp.float32, jnp.bfloat16
    a_hi, b_hi = a.astype(bf), b.astype(bf)
    a_lo = (a - a_hi.astype(f32)).astype(bf)
    b_lo = (b - b_hi.astype(f32)).astype(bf)
    out = lax.dot_general(a_hi, b_hi, dims, preferred_element_type=f32)
    out = out + lax.dot_general(a_hi, b_lo, dims, preferred_element_type=f32)
    return out + lax.dot_general(a_lo, b_hi, dims, preferred_element_type=f32)


def _bf16_terms(a):
    f32, bf = jnp.float32, jnp.bfloat16
    hi = a.astype(bf)
    r1 = a - hi.astype(f32)
    mid = r1.astype(bf)
    lo = (r1 - mid.astype(f32)).astype(bf)
    return hi, mid, lo


def _dot_lhs_split(a, b01):
    b = b01.astype(jnp.bfloat16)
    return sum(jnp.dot(t, b, preferred_element_type=jnp.float32) for t in _bf16_terms(a))


def _dot_rhs_split(a01, b):
    a = a01.astype(jnp.bfloat16)
    return sum(jnp.dot(a, t, preferred_element_type=jnp.float32) for t in _bf16_terms(b))


def _rwkv_kernel(z_ref, mu_ref, w0_ref, w2_ref, a0_ref, a2_ref, g2_ref, kk_ref, ka_ref, rk_ref,
                 gng_ref, gnb_ref, bd_ref, o_ref, m_ref, prev_ref, y_ref):
    C = z_ref.shape[1]
    N = RW_HEAD_DIM
    f32 = jnp.float32
    dot3 = _dot_bf16x3

    @pl.when(pl.program_id(1) == 0)
    def _():
        m_ref[...] = jnp.zeros(m_ref.shape, f32)
        prev_ref[...] = jnp.zeros(prev_ref.shape, f32)

    z = z_ref[0]
    row = lax.broadcasted_iota(jnp.int32, z.shape, 0)
    shifted = jnp.where(row == 0, prev_ref[...], pltpu.roll(z, 1, axis=0))
    prev_ref[...] = z[C - 1:C, :]
    zs = z + (shifted - z) * mu_ref[...]
    r = zs[:, 0:RW_DIM]
    k = zs[:, RW_DIM:2 * RW_DIM]
    v = zs[:, 2 * RW_DIM:3 * RW_DIM]
    o1 = 3 * RW_DIM
    wl = zs[:, o1:o1 + RW_DECAY_LORA]
    al = zs[:, o1 + RW_DECAY_LORA:o1 + RW_DECAY_LORA + RW_A_LORA]
    gl = zs[:, o1 + RW_DECAY_LORA + RW_A_LORA:]

    bd = bd_ref[...]
    log_w = -jax.nn.softplus(-(w0_ref[...] + dot3(jnp.tanh(wl), w2_ref[...]))) - 0.5
    ldec = -jnp.exp(log_w)
    a_lr = jax.nn.sigmoid(a0_ref[...] + dot3(al, a2_ref[...]))
    g = dot3(jax.nn.sigmoid(gl), g2_ref[...])
    kk = k * kk_ref[...]
    kk = kk * lax.rsqrt(jnp.maximum(_dot_lhs_split(kk * kk, bd), 1e-24))
    k2 = k * (1.0 + (a_lr - 1.0) * ka_ref[...])
    a_vec = -kk
    b_vec = kk * a_lr

    ti = lax.broadcasted_iota(jnp.int32, (C, C), 0)
    tj = lax.broadcasted_iota(jnp.int32, (C, C), 1)
    cum = _dot_rhs_split((ti >= tj).astype(f32), ldec)
    cum_last = cum[C - 1:C, :]
    w_incl = jnp.exp(cum)
    w_excl = jnp.exp(cum - ldec)
    w_inv = jnp.exp(-cum)
    w_end = jnp.exp(cum_last - cum)
    w_all = jnp.exp(cum_last)
    a_t = a_vec * w_excl
    r_t = r * w_incl
    b_t = b_vec * w_inv
    k_t = k2 * w_inv
    b_e = b_vec * w_end
    k_e = k2 * w_end

    strict = ti > tj
    incl = ti >= tj
    bi, bj = ti // RW_INV_BLOCK, tj // RW_INV_BLOCK
    same_blk = bi == bj
    pair_blk = jnp.logical_and((bi // 2) == (bj // 2), jnp.logical_not(same_blk))
    half_blk = (bi // 2) != (bj // 2)
    eye = (ti == tj).astype(f32)

    H = RW_HEADS
    heads = lambda x: jnp.stack([x[:, h * N:(h + 1) * N] for h in range(H)], axis=0)
    bmm = lambda a, b: dot3(a, b, _BNN)
    A, Rt, Bt, Kt, Be, Ke, V = (heads(t) for t in (a_t, r_t, b_t, k_t, b_e, k_e, v))
    gm = dot3(jnp.concatenate([A, Rt], axis=1), jnp.concatenate([Bt, Kt], axis=1), _BNT)
    a_ab = jnp.where(strict, gm[:, :C, :C], 0.0)
    a_ak = jnp.where(strict, gm[:, :C, C:], 0.0)
    a_rb = jnp.where(incl, gm[:, C:, :C], 0.0)
    a_rk = jnp.where(incl, gm[:, C:, C:], 0.0)
    d1 = jnp.where(same_blk, a_ab, 0.0)
    xinv = eye + d1
    d2 = bmm(d1, d1)
    xinv = xinv + bmm(xinv, d2)
    d4 = bmm(d2, d2)
    xinv = xinv + bmm(xinv, d4)
    d8 = bmm(d4, d4)
    xinv = xinv + bmm(xinv, d8)
    xinv = xinv + bmm(bmm(xinv, jnp.where(pair_blk, a_ab, 0.0)), xinv)
    xinv = xinv + bmm(bmm(xinv, jnp.where(half_blk, a_ab, 0.0)), xinv)

    av = bmm(jnp.concatenate([a_ak, a_rk], axis=1), V)
    p = bmm(xinv, jnp.concatenate([A, av[:, :C]], axis=2))
    qm = bmm(a_rb, p)
    q1 = Rt + qm[:, :, :N]
    q2 = qm[:, :, N:] + av[:, C:]
    gmat = bmm(jnp.swapaxes(Be, 1, 2), p)
    g1 = eye * heads(w_all) + gmat[:, :, :N]
    g2 = gmat[:, :, N:] + bmm(jnp.swapaxes(Ke, 1, 2), V)
    m = m_ref[...]
    yh = _dot_f32(q1, m, _BNN) + q2
    m_ref[...] = _dot_f32(g1, m, _BNN) + g2
    for h in range(H):
        y_ref[:, h * N:(h + 1) * N] = yh[h]

    y = y_ref[...]
    mean = _dot_lhs_split(y, bd) * (1.0 / N)
    yc = y - mean
    var = _dot_lhs_split(yc * yc, bd) * (1.0 / N)
    yn = yc * lax.rsqrt(var + RW_GN_EPS) * gng_ref[...] + gnb_ref[...]
    bonus = _dot_lhs_split(r * k2 * rk_ref[...], bd) * v
    o_ref[0] = ((yn + bonus) * g).astype(o_ref.dtype)


def _rwkv7_time_mix(z_rw, mu, w0, w2, a0, a2, g2, k_k, k_a, r_k, gn_g, gn_b, out_dtype):
    B, S, _ = z_rw.shape
    C = RW_CHUNK
    assert S % C == 0 and C == RW_HEAD_DIM and C % (4 * RW_INV_BLOCK) == 0
    hid = jnp.arange(RW_DIM) // RW_HEAD_DIM
    bd = (hid[:, None] == hid[None, :]).astype(jnp.float32)
    row = lambda a: a.reshape(1, -1)
    full = lambda shape: pl.BlockSpec(shape, lambda b, c: (0,) * len(shape))
    return pl.pallas_call(
        _rwkv_kernel,
        grid=(B, S // C),
        in_specs=[
            pl.BlockSpec((1, C, RW_COLS), lambda b, c: (b, c, 0)),
            full((1, RW_COLS)), full((1, RW_DIM)), full((RW_DECAY_LORA, RW_DIM)), full((1, RW_DIM)),
            full((RW_A_LORA, RW_DIM)), full((RW_GATE_LORA, RW_DIM)), full((1, RW_DIM)), full((1, RW_DIM)),
            full((1, RW_DIM)), full((1, RW_DIM)), full((1, RW_DIM)), full((RW_DIM, RW_DIM)),
        ],
        out_specs=pl.BlockSpec((1, C, RW_DIM), lambda b, c: (b, c, 0)),
        out_shape=jax.ShapeDtypeStruct((B, S, RW_DIM), out_dtype),
        scratch_shapes=[
            pltpu.VMEM((RW_HEADS, RW_HEAD_DIM, RW_HEAD_DIM), jnp.float32),
            pltpu.VMEM((1, RW_COLS), jnp.float32),
            pltpu.VMEM((C, RW_DIM), jnp.float32),
        ],
        compiler_params=pltpu.CompilerParams(dimension_semantics=("parallel", "arbitrary")),
        name="rwkv7_time_mix",
    )(z_rw, row(mu), row(w0), w2, row(a0), a2, g2, row(k_k), row(k_a), row(r_k), row(gn_g), row(gn_b), bd)


DSA_TQ = 256
MASK_NEG = -1e30
INT_MIN = -2 ** 31
KEY_NEG_INF = -2139095041
THRESH_BITS = 32


def _dsa_prep_kernel(z_ref, kvg_ref, kig_ref, kib_ref, qi_ref, kv_ref, ki_ref):
    z = z_ref[0]
    qi_ref[0] = z[:, :IDX_HEADS * IDX_DIM].astype(qi_ref.dtype)
    kv = z[:, 256:384]
    ms = jnp.mean(jnp.square(kv), -1, keepdims=True)
    kv_ref[0] = (kv * lax.rsqrt(ms + LN_EPS) * kvg_ref[...]).astype(kv_ref.dtype)
    ki = z[:, 384:448]
    mu = jnp.mean(ki, -1, keepdims=True)
    var = jnp.mean(jnp.square(ki - mu), -1, keepdims=True)
    ki_ref[0] = ((ki - mu) * lax.rsqrt(var + LN_EPS) * kig_ref[...] + kib_ref[...]).astype(ki_ref.dtype)


def _dsa_prep(z_small, kv_g, ki_g, ki_b, tm=512):
    B, S, W = z_small.shape
    return pl.pallas_call(
        _dsa_prep_kernel,
        grid=(B, S // tm),
        in_specs=[
            pl.BlockSpec((1, tm, W), lambda b, i: (b, i, 0)),
            pl.BlockSpec((1, DSA_LATENT), lambda b, i: (0, 0)),
            pl.BlockSpec((1, IDX_DIM), lambda b, i: (0, 0)),
            pl.BlockSpec((1, IDX_DIM), lambda b, i: (0, 0)),
        ],
        out_specs=[
            pl.BlockSpec((1, tm, IDX_HEADS * IDX_DIM), lambda b, i: (b, i, 0)),
            pl.BlockSpec((1, tm, DSA_LATENT), lambda b, i: (b, i, 0)),
            pl.BlockSpec((1, tm, IDX_DIM), lambda b, i: (b, i, 0)),
        ],
        out_shape=[
            jax.ShapeDtypeStruct((B, S, IDX_HEADS * IDX_DIM), MM_DTYPE),
            jax.ShapeDtypeStruct((B, S, DSA_LATENT), MM_DTYPE),
            jax.ShapeDtypeStruct((B, S, IDX_DIM), MM_DTYPE),
        ],
        name="dsa_prep",
    )(z_small, kv_g[None], ki_g[None], ki_b[None])


def _sortable_key(s):
    s = jnp.where(s == 0.0, 0.0, s)
    bits = pltpu.bitcast(s, jnp.int32)
    return bits ^ ((bits >> 31) & 0x7FFFFFFF)


def _col_count(mask_i32):
    tk, tq = mask_i32.shape
    return jnp.sum(mask_i32.reshape(tk // 8, 8, tq), axis=0)


def _dsa_kernel(bfar_ref, q_ref, qi_ref, wit_ref, kv_ref, kvt_ref, ki_ref, b0_ref, b1_ref, o_ref,
                key_ref, madd_ref, m_ref, l_ref, acc_ref, *, topk):
    tq = q_ref.shape[1]
    tk = tq
    i = pl.program_id(1)
    nj = i + 1
    f32 = jnp.float32
    krow = lax.broadcasted_iota(jnp.int32, (tk, tq), 0)
    qcol = lax.broadcasted_iota(jnp.int32, (tk, tq), 1)

    qi = qi_ref[0]
    wit = wit_ref[0] * (IDX_HEADS ** -0.5)

    def score_chunk(j, carry):
        off = pl.multiple_of(j * tk, tk)
        kc = ki_ref[0, pl.ds(off, tk), :]
        s = jnp.zeros((tk, tq), f32)
        for h in range(IDX_HEADS):
            d = lax.dot_general(kc, qi[:, h * IDX_DIM:(h + 1) * IDX_DIM],
                                (((1,), (1,)), ((), ())), preferred_element_type=f32)
            s = s + wit[h:h + 1, :] * jnp.maximum(d * (IDX_DIM ** -0.5), 0.0)
        causal = (krow + j * tk) <= (qcol + i * tq)
        s = jnp.where(causal, s, -jnp.inf)
        key_ref[j] = _sortable_key(s)
        return carry

    lax.fori_loop(0, nj, score_chunk, 0)

    def count_where(pred_fn):
        def body(j, acc):
            return acc + _col_count(pred_fn(key_ref[j], j).astype(jnp.int32))
        acc = lax.fori_loop(0, nj, body, jnp.zeros((8, tq), jnp.int32))
        return jnp.sum(acc, axis=0, keepdims=True)

    def bit_step(it, t_u):
        bit = THRESH_BITS - 1 - it
        cand_u = t_u | jnp.left_shift(jnp.int32(1), bit)
        cand = cand_u ^ INT_MIN
        cnt = count_where(lambda k, j: k >= cand)
        return jnp.where(cnt >= topk, cand_u, t_u)

    t_u = lax.fori_loop(0, THRESH_BITS, bit_step, jnp.zeros((1, tq), jnp.int32))
    thr = t_u ^ INT_MIN
    cnt_gt = count_where(lambda k, j: k > thr)
    cnt_ge = count_where(lambda k, j: k >= thr)
    is_neg = thr == KEY_NEG_INF
    need = jnp.logical_and(cnt_ge > topk, jnp.logical_not(is_neg))
    n_tie_take = topk - cnt_gt

    thr_open = jnp.where(is_neg, thr, thr - 1)
    any_need = jnp.max(need.astype(jnp.int32)) > 0

    @pl.when(jnp.logical_not(any_need))
    def _():
        def body(j, carry):
            madd_ref[j] = jnp.where(key_ref[j] > thr_open, 0.0, MASK_NEG)
            return carry
        lax.fori_loop(0, nj, body, 0)

    @pl.when(any_need)
    def _():
        s_len = tk * key_ref.shape[0]
        n_bits = max(1, int(math.ceil(math.log2(s_len))))

        def idx_step(it, p):
            bit = n_bits - 1 - it
            cand = p | jnp.left_shift(jnp.int32(1), bit)
            cnt = count_where(
                lambda k, j: jnp.where(k == thr, jnp.where((krow + j * tk) < cand, 1, 0), 0))
            return jnp.where(cnt < n_tie_take, cand, p)

        p_idx = lax.fori_loop(0, n_bits, idx_step, jnp.zeros((1, tq), jnp.int32))
        p_idx = jnp.where(need, p_idx, jnp.where(is_neg, -1, s_len))

        def body(j, carry):
            k = key_ref[j]
            tie_ok = jnp.where((krow + j * tk) <= p_idx, 0.0, MASK_NEG)
            madd_ref[j] = jnp.where(k > thr, 0.0, jnp.where(k == thr, tie_ok, MASK_NEG))
            return carry
        lax.fori_loop(0, nj, body, 0)

    m_ref[...] = jnp.full(m_ref.shape, MASK_NEG, f32)
    l_ref[...] = jnp.zeros(l_ref.shape, f32)
    acc_ref[...] = jnp.zeros(acc_ref.shape, f32)
    scale = DSA_LATENT ** -0.5

    def attend(j, bias_fn):
        off = pl.multiple_of(j * tk, tk)
        kc = kv_ref[0, pl.ds(off, tk), :]
        kct = kvt_ref[0, :, pl.ds(off, tk)]
        madd = madd_ref[j]
        for h in range(DSA_HEADS):
            qh = q_ref[0, :, h * DSA_LATENT:(h + 1) * DSA_LATENT]
            lg = lax.dot_general(kc, qh, (((1,), (1,)), ((), ())), preferred_element_type=f32)
            lg = lg * scale + bias_fn(h) + madd
            m_old = m_ref[h]
            m_new = jnp.maximum(m_old, jnp.max(lg, axis=0, keepdims=True))
            alpha = jnp.exp(m_old - m_new)
            p = jnp.exp(lg - m_new)
            l_ref[h] = alpha * l_ref[h] + jnp.sum(p, axis=0, keepdims=True)
            acc_ref[h] = alpha * acc_ref[h] + jnp.dot(kct, p.astype(kct.dtype),
                                                      preferred_element_type=f32)
            m_ref[h] = m_new

    def far_body(j, carry):
        attend(j, lambda h: bfar_ref[h])
        return carry

    lax.fori_loop(0, jnp.maximum(i - 1, 0), far_body, 0)

    @pl.when(i >= 1)
    def _():
        attend(i - 1, lambda h: b1_ref[h])

    attend(i, lambda h: b0_ref[h])

    for h in range(DSA_HEADS):
        out_t = acc_ref[h] / l_ref[h]
        o_ref[0, :, h * DSA_LATENT:(h + 1) * DSA_LATENT] = out_t.T.astype(o_ref.dtype)


def _dsa_bias_tiles(rel_bias, tq):
    dist = jnp.arange(2 * tq, dtype=jnp.int32)
    by_dist = rel_bias[_t5_bucket(dist)].T
    kk = jnp.arange(tq, dtype=jnp.int32)[:, None]
    qq = jnp.arange(tq, dtype=jnp.int32)[None, :]
    b0 = by_dist[:, jnp.maximum(qq - kk, 0)]
    b1 = by_dist[:, tq + qq - kk]
    return b0, b1, by_dist[:, -1]


def _dsa_attention(q, z_small, z_wi_t, kv_g, idx_k_g, idx_k_b, rel_bias, out_dtype):
    B, S, _ = q.shape
    tq = DSA_TQ
    topk = min(TOPK_MAX, S // 4)
    assert S % tq == 0 and tq >= REL_MAX_DIST and topk <= tq
    qi, ckv, kidx = _dsa_prep(z_small, kv_g, idx_k_g, idx_k_b)
    ckv_t = jnp.swapaxes(ckv, 1, 2)
    b0, b1, bfar = _dsa_bias_tiles(rel_bias, tq)
    nq = S // tq
    H, d = DSA_HEADS, DSA_LATENT
    grid_spec = pltpu.PrefetchScalarGridSpec(
        num_scalar_prefetch=0,
        grid=(B, nq),
        in_specs=[
            pl.BlockSpec(memory_space=pltpu.SMEM),
            pl.BlockSpec((1, tq, H * d), lambda b, i: (b, i, 0)),
            pl.BlockSpec((1, tq, IDX_HEADS * IDX_DIM), lambda b, i: (b, i, 0)),
            pl.BlockSpec((1, IDX_HEADS, tq), lambda b, i: (b, 0, i)),
            pl.BlockSpec((1, S, d), lambda b, i: (b, 0, 0)),
            pl.BlockSpec((1, d, S), lambda b, i: (b, 0, 0)),
            pl.BlockSpec((1, S, IDX_DIM), lambda b, i: (b, 0, 0)),
            pl.BlockSpec((H, tq, tq), lambda b, i: (0, 0, 0)),
            pl.BlockSpec((H, tq, tq), lambda b, i: (0, 0, 0)),
        ],
        out_specs=pl.BlockSpec((1, tq, H * d), lambda b, i: (b, i, 0)),
        scratch_shapes=[
            pltpu.VMEM((nq, tq, tq), jnp.int32),
            pltpu.VMEM((nq, tq, tq), jnp.float32),
            pltpu.VMEM((H, 1, tq), jnp.float32),
            pltpu.VMEM((H, 1, tq), jnp.float32),
            pltpu.VMEM((H, d, tq), jnp.float32),
        ],
    )
    return pl.pallas_call(
        partial(_dsa_kernel, topk=topk),
        grid_spec=grid_spec,
        out_shape=jax.ShapeDtypeStruct((B, S, H * d), out_dtype),
        compiler_params=pltpu.CompilerParams(vmem_limit_bytes=48 * 1024 * 1024),
        name="dsa_attention",
    )(bfar, q, qi, z_wi_t, ckv, ckv_t, kidx, b0, b1)


SC_CORES = 2
SC_SUBCORES = 16
SC_LANES = 16
PEER_SC_TOKENS = 8
PEER_SC_RING = 4
PEER_SC_UNROLL = 4
PEER_SLOTS = PEER_HEADS * PEER_TOPK


def _peer_sc_call(body, T, out_width, stage_width):
    mesh = plsc.VectorSubcoreMesh(core_axis_name="c", subcore_axis_name="s")
    return pl.kernel(
        body, mesh=mesh,
        out_type=jax.ShapeDtypeStruct((T, out_width), jnp.float32),
        scratch_types=[
            pltpu.VMEM((PEER_SC_TOKENS, PEER_SLOTS), jnp.int32),
            pltpu.VMEM((PEER_SC_TOKENS, stage_width), jnp.float32),
            pltpu.VMEM((PEER_SC_RING, PEER_TOPK, D_MODEL), jnp.float32),
            pltpu.VMEM((PEER_SC_TOKENS, out_width), jnp.float32),
            pltpu.SemaphoreType.DMA((PEER_SC_RING,)),
        ],
        compiler_params=pltpu.CompilerParams(needs_layout_passes=False),
    )


def _peer_sc_body(compute, zero_out, x_hbm, idx_hbm, tab_hbm, out_hbm, idx_v, x_v, rows_v, out_v, sems):
    T = idx_hbm.shape[0]
    tokens_per_worker = T // (SC_CORES * SC_SUBCORES)
    n_steps = PEER_SC_TOKENS * PEER_HEADS
    worker = lax.axis_index("s") * SC_CORES + lax.axis_index("c")
    base = worker * tokens_per_worker

    def gather(s, b):
        ids = idx_v[s // PEER_HEADS, pl.ds((s % PEER_HEADS) * PEER_TOPK, PEER_TOPK)]
        return pltpu.make_async_copy(tab_hbm.at[ids], rows_v.at[b], sems.at[b])

    @pl.loop(0, tokens_per_worker // PEER_SC_TOKENS)
    def _(blk):
        tok0 = base + blk * PEER_SC_TOKENS
        pltpu.sync_copy(idx_hbm.at[pl.ds(tok0, PEER_SC_TOKENS)], idx_v)
        pltpu.sync_copy(x_hbm.at[pl.ds(tok0, PEER_SC_TOKENS)], x_v)
        for b in range(PEER_SC_RING - 1):
            gather(b, b).start()
        if zero_out:
            @pl.loop(0, PEER_SC_TOKENS)
            def _(t):
                @plsc.parallel_loop(0, out_v.shape[1] // SC_LANES, unroll=PEER_SC_UNROLL)
                def _(c):
                    out_v[t, pl.ds(pl.multiple_of(c * SC_LANES, SC_LANES), SC_LANES)] = (
                        jnp.zeros((SC_LANES,), jnp.float32))

        @pl.loop(0, n_steps, step=PEER_SC_RING)
        def _(s0):
            for b in range(PEER_SC_RING):
                s = s0 + b
                gather(s, b).wait()

                @pl.when(s + PEER_SC_RING - 1 < n_steps)
                def _():
                    gather(s + PEER_SC_RING - 1, (b + PEER_SC_RING - 1) % PEER_SC_RING).start()

                compute(s // PEER_HEADS, s % PEER_HEADS, b, x_v, rows_v, out_v)

        pltpu.sync_copy(out_v, out_hbm.at[pl.ds(tok0, PEER_SC_TOKENS)])


def _peer_dots_compute(t, hd, b, h_v, rows_v, dots_v):
    lane = lax.iota(jnp.int32, SC_LANES)

    def col_step(c, accs):
        off = pl.multiple_of(c * SC_LANES, SC_LANES)
        hv = h_v[t, pl.ds(off, SC_LANES)]
        return tuple(accs[r] + rows_v[b, r, pl.ds(off, SC_LANES)] * hv for r in range(PEER_TOPK))

    accs = plsc.parallel_loop(
        0, D_MODEL // SC_LANES, unroll=PEER_SC_UNROLL,
        carry=tuple(jnp.zeros((SC_LANES,), jnp.float32) for _ in range(PEER_TOPK)))(col_step)
    res = jnp.zeros((SC_LANES,), jnp.float32)
    for r in range(PEER_TOPK):
        res = jnp.where(lane == r, jnp.sum(accs[r]), res)
    dots_v[t, pl.ds(hd * PEER_TOPK, PEER_TOPK)] = res


def _peer_mix_compute(t, hd, b, act_v, rows_v, out_v):
    tvec = jnp.full((SC_LANES,), t, jnp.int32)
    weights = [plsc.load_gather(act_v, [tvec, jnp.full((SC_LANES,), hd * PEER_TOPK + r, jnp.int32)])
               for r in range(PEER_TOPK)]

    @plsc.parallel_loop(0, D_MODEL // SC_LANES, unroll=PEER_SC_UNROLL)
    def _(c):
        off = pl.multiple_of(c * SC_LANES, SC_LANES)
        a = rows_v[b, 0, pl.ds(off, SC_LANES)] * weights[0]
        for r in range(1, PEER_TOPK):
            a = a + rows_v[b, r, pl.ds(off, SC_LANES)] * weights[r]
        plsc.addupdate(out_v.at[t, pl.ds(off, SC_LANES)], a)


def _peer_expert_dots(h, experts, u_tab):
    T = h.shape[0]
    assert T % (SC_CORES * SC_SUBCORES * PEER_SC_TOKENS) == 0
    assert (PEER_SC_TOKENS * PEER_HEADS) % PEER_SC_RING == 0 and PEER_TOPK == SC_LANES
    body = partial(_peer_sc_body, _peer_dots_compute, False)
    return _peer_sc_call(body, T, PEER_SLOTS, D_MODEL)(h, experts, u_tab)


def _peer_expert_mix(act, experts, v_tab):
    T = act.shape[0]
    assert T % (SC_CORES * SC_SUBCORES * PEER_SC_TOKENS) == 0
    body = partial(_peer_sc_body, _peer_mix_compute, True)
    return _peer_sc_call(body, T, D_MODEL, PEER_SLOTS)(act, experts, v_tab)


PEER_TM = 256


def _extract_topk(s, k, payload=None):
    R = s.shape[0]
    riota = lax.broadcasted_iota(jnp.int32, s.shape, 0)
    vals, rows = [], []
    for _ in range(k):
        m = jnp.max(s, axis=0, keepdims=True)
        pos = jnp.min(jnp.where(s == m, riota, R), axis=0, keepdims=True)
        hit = riota == pos
        vals.append(m)
        rows.append(pos if payload is None else jnp.max(jnp.where(hit, payload, -1), axis=0, keepdims=True))
        s = jnp.where(hit, -jnp.inf, s)
    return jnp.concatenate(vals, axis=0), jnp.concatenate(rows, axis=0)


def _peer_route_kernel(x_ref, sc_ref, sh_ref, wq_ref, keys_ref, h_ref, ex_ref, gate_ref, q_ref, ext_ref, gt_ref):
    K = PEER_TOPK
    f32 = jnp.float32
    h = x_ref[0] * (1.0 + sc_ref[0]) + sh_ref[0]
    h_ref[0] = h
    q = jnp.dot(h.astype(wq_ref.dtype), wq_ref[...], preferred_element_type=f32)
    for hd in range(PEER_HEADS):
        q_ref[hd] = q[:, hd * PEER_KEY_DIM:(hd + 1) * PEER_KEY_DIM].astype(q_ref.dtype)

    def head_body(hd, carry):
        qh = q_ref[hd]
        s1 = lax.dot_general(keys_ref[hd, 0], qh[:, :PEER_HALF], _NT, preferred_element_type=f32)
        s2 = lax.dot_general(keys_ref[hd, 1], qh[:, PEER_HALF:], _NT, preferred_element_type=f32)
        v1, i1 = _extract_topk(s1, K)
        v2, i2 = _extract_topk(s2, K)
        tm = v1.shape[1]
        cand_rows, cidx_rows = [], []
        for a in range(K):
            nb = K // (a + 1)
            cand_rows.append(v1[a:a + 1, :] + v2[:nb, :])
            cidx_rows.append(i1[a:a + 1, :] * PEER_N_KEYS + i2[:nb, :])
        n_pad = -sum(r.shape[0] for r in cand_rows) % 8
        cand_rows.append(jnp.full((n_pad, tm), -jnp.inf, f32))
        cidx_rows.append(jnp.full((n_pad, tm), -1, jnp.int32))
        cand = jnp.concatenate(cand_rows, axis=0)
        cidx = jnp.concatenate(cidx_rows, axis=0)
        top_s, experts = _extract_topk(cand, K, payload=cidx)
        e = jnp.exp(top_s - top_s[0:1, :])
        gt_ref[pl.ds(pl.multiple_of(hd * K, K), K), :] = e / jnp.sum(e, axis=0, keepdims=True)
        ext_ref[pl.ds(pl.multiple_of(hd * K, K), K), :] = experts
        return carry

    lax.fori_loop(0, PEER_HEADS, head_body, 0)
    ex_ref[0] = ext_ref[...].T
    gate_ref[0] = gt_ref[...].T


def _peer_route(x, sc, sh, w_pq, sub_keys):
    B, S, D = x.shape
    tm = PEER_TM
    return pl.pallas_call(
        _peer_route_kernel,
        grid=(B, S // tm),
        in_specs=[
            pl.BlockSpec((1, tm, D), lambda b, i: (b, i, 0)),
            pl.BlockSpec((1, 1, D), lambda b, i: (b, 0, 0)),
            pl.BlockSpec((1, 1, D), lambda b, i: (b, 0, 0)),
            pl.BlockSpec((D, PEER_HEADS * PEER_KEY_DIM), lambda b, i: (0, 0)),
            pl.BlockSpec((PEER_HEADS, 2, PEER_N_KEYS, PEER_HALF), lambda b, i: (0, 0, 0, 0)),
        ],
        out_specs=[
            pl.BlockSpec((1, tm, D), lambda b, i: (b, i, 0)),
            pl.BlockSpec((1, tm, PEER_SLOTS), lambda b, i: (b, i, 0)),
            pl.BlockSpec((1, tm, PEER_SLOTS), lambda b, i: (b, i, 0)),
        ],
        out_shape=[
            jax.ShapeDtypeStruct((B, S, D), jnp.float32),
            jax.ShapeDtypeStruct((B, S, PEER_SLOTS), jnp.int32),
            jax.ShapeDtypeStruct((B, S, PEER_SLOTS), jnp.float32),
        ],
        scratch_shapes=[
            pltpu.VMEM((PEER_HEADS, tm, PEER_KEY_DIM), MM_DTYPE),
            pltpu.VMEM((PEER_SLOTS, tm), jnp.int32),
            pltpu.VMEM((PEER_SLOTS, tm), jnp.float32),
        ],
        compiler_params=pltpu.CompilerParams(dimension_semantics=("parallel", "parallel")),
        name="peer_route",
    )(x, sc[:, None, :], sh[:, None, :], w_pq.astype(MM_DTYPE), sub_keys.astype(MM_DTYPE))


def _peer_pre(h2, experts, u_tab):
    B, S, D = h2.shape
    return _peer_expert_dots(h2.reshape(B * S, D), experts.reshape(B * S, PEER_SLOTS), u_tab)


def _peer_out(pre, experts, gates, v_tab):
    B, S, _ = experts.shape
    act = jax.nn.gelu(pre, approximate=False) * gates.reshape(B * S, PEER_SLOTS)
    return _peer_expert_mix(act, experts.reshape(B * S, PEER_SLOTS), v_tab).reshape(B, S, D_MODEL)


def kernel(x, c, w_ada, b_ada, w_in, rw_mu, rw_w0, rw_w2, rw_a0, rw_a2, rw_g2, rw_k_k, rw_k_a, rw_r_k, rw_gn_g, rw_gn_b, dsa_kv_g, idx_k_g, idx_k_b, rel_bias, w_br_a, w_br_b, w_out, ln1_g, ln1_b, peer_wq, peer_keys, peer_u, peer_v, ln2_g, ln2_b):
    l = 0
    mod = jax.nn.silu(c) @ w_ada[l] + b_ada[l]

    w_rw, w_q, w_kv, w_qi, w_ki, w_wi, w_ga, w_gb = jnp.split(w_in[l], _split_points(IN_SIZES), axis=-1)
    small_pad = jnp.zeros((D_MODEL, 512 - 256 - 128 - 64 - 4), w_in.dtype)
    w_small = jnp.concatenate([w_qi, w_kv, w_ki, w_wi, small_pad], axis=-1)
    w_gates = jnp.concatenate([w_ga, w_gb], axis=-1)

    def rwkv_branch(x, sc1, sh1):
        z_rw = _mod_matmul(x, sc1, sh1, w_rw, jnp.float32, tn=896)
        return _rwkv7_time_mix(z_rw, rw_mu[l], rw_w0[l], rw_w2[l], rw_a0[l], rw_a2[l], rw_g2[l],
                               rw_k_k[l], rw_k_a[l], rw_r_k[l], rw_gn_g[l], rw_gn_b[l], jnp.float32) @ w_br_a[l]

    def dsa_branch_and_merge(x, sc1, sh1, gt1, y_a):
        z_q = _mod_matmul(x, sc1, sh1, w_q, MM_DTYPE, tn=1024)
        z_small = _mod_matmul(x, sc1, sh1, w_small, jnp.float32, tn=512)
        z_g = _mod_matmul(x, sc1, sh1, w_gates, jnp.float32, tn=1024)
        z_ga, z_gb = z_g[..., :D_MODEL], z_g[..., D_MODEL:]
        z_wi_t = jnp.swapaxes(z_small[..., 448:448 + IDX_HEADS], 1, 2)
        y_b = _dsa_attention(z_q, z_small, z_wi_t, dsa_kv_g[l], idx_k_g[l], idx_k_b[l],
                             rel_bias, jnp.float32) @ w_br_b[l]
        merged = jax.nn.sigmoid(z_ga) * y_a + jax.nn.sigmoid(z_gb) * y_b
        mix = merged @ w_out[l]
        return _layer_norm(DEEPNORM_ALPHA * x + gt1[:, None] * mix, ln1_g[l], ln1_b[l])

    def finish(st):
        y2 = _peer_out(st["pre"], st["experts"], st["gates"], peer_v[l])
        return _layer_norm(DEEPNORM_ALPHA * st["x1"] + st["gt2"][:, None] * y2, ln2_g[l], ln2_b[l])

    gsz = x.shape[0] // BATCH_GROUPS
    outs = []
    prev = None
    for g in range(BATCH_GROUPS):
        sh1, sc1, gt1, sh2, sc2, gt2 = jnp.split(mod[g * gsz:(g + 1) * gsz], 6, axis=-1)
        x_g = x[g * gsz:(g + 1) * gsz]
        if prev is not None:
            x_g, prev["experts"] = lax.optimization_barrier((x_g, prev["experts"]))
        y_a = rwkv_branch(x_g, sc1, sh1)
        if prev is not None:
            y_a, prev["pre"] = lax.optimization_barrier((y_a, prev["pre"]))
            done = finish(prev)
        x1 = dsa_branch_and_merge(x_g, sc1, sh1, gt1, y_a)
        if prev is not None:
            x1, done = lax.optimization_barrier((x1, done))
            outs.append(done)
        h2, experts, gates = _peer_route(x1, sc2, sh2, peer_wq[l], peer_keys[l])
        prev = dict(x1=x1, gt2=gt2, experts=experts, gates=gates, pre=_peer_pre(h2, experts, peer_u[l]))
    outs.append(finish(prev))
    return jnp.concatenate(outs, axis=0)
```

```python
import math
from functools import partial

import jax
import jax.numpy as jnp
import numpy as np
from jax import lax
from jax.experimental import pallas as pl
from jax.experimental.pallas import tpu as pltpu
from jax.experimental.pallas import tpu_sc as plsc

D_MODEL = 1024
RW_HEADS = 8
RW_HEAD_DIM = 64
RW_DIM = 512
RW_DECAY_LORA = 64
RW_A_LORA = 64
RW_GATE_LORA = 128
RW_SIZES = (RW_DIM, RW_DIM, RW_DIM, RW_DECAY_LORA, RW_A_LORA, RW_GATE_LORA)
RW_COLS = 3 * RW_DIM + RW_DECAY_LORA + RW_A_LORA + RW_GATE_LORA
RW_GN_EPS = 64e-5
DSA_HEADS = 8
DSA_LATENT = 128
DSA_Q_DIM = DSA_HEADS * DSA_LATENT
IDX_HEADS = 4
IDX_DIM = 64
TOPK_MAX = 256
Q_BLOCK = 128
REL_BUCKETS = 32
REL_MAX_DIST = 128
IN_SIZES = (RW_COLS, DSA_Q_DIM, DSA_LATENT, IDX_HEADS * IDX_DIM, IDX_DIM, IDX_HEADS, D_MODEL, D_MODEL)
IN_COLS = sum(IN_SIZES)
PEER_HEADS = 8
PEER_N_KEYS = 128
PEER_KEY_DIM = 128
PEER_HALF = 64
PEER_TOPK = 16
PEER_CHUNK = 128
LN_EPS = 1e-5
DEPTH = 1
DEEPNORM_ALPHA = (2.0 * DEPTH) ** 0.25

LANES = 128
MM_DTYPE = jnp.bfloat16
BATCH_GROUPS = 8


def _split_points(sizes):
    return np.cumsum(sizes)[:-1].tolist()


def _mod_matmul_kernel(x_ref, sc_ref, sh_ref, w_ref, o_ref):
    h = x_ref[0] * (1.0 + sc_ref[0]) + sh_ref[0]
    o_ref[0] = jnp.dot(h.astype(w_ref.dtype), w_ref[...],
                       preferred_element_type=jnp.float32).astype(o_ref.dtype)


def _mod_matmul(x, sc, sh, w, out_dtype, tn, tm=512):
    B, S, D = x.shape
    N = w.shape[1]
    return pl.pallas_call(
        _mod_matmul_kernel,
        grid=(B, S // tm, N // tn),
        in_specs=[
            pl.BlockSpec((1, tm, D), lambda b, i, j: (b, i, 0)),
            pl.BlockSpec((1, 1, D), lambda b, i, j: (b, 0, 0)),
            pl.BlockSpec((1, 1, D), lambda b, i, j: (b, 0, 0)),
            pl.BlockSpec((D, tn), lambda b, i, j: (0, j)),
        ],
        out_specs=pl.BlockSpec((1, tm, tn), lambda b, i, j: (b, i, j)),
        out_shape=jax.ShapeDtypeStruct((B, S, N), out_dtype),
        name="mod_matmul",
    )(x, sc[:, None, :], sh[:, None, :], w.astype(MM_DTYPE))


def _layer_norm(x, g, b):
    mu = jnp.mean(x, -1, keepdims=True)
    var = jnp.mean(jnp.square(x - mu), -1, keepdims=True)
    return (x - mu) * lax.rsqrt(var + LN_EPS) * g + b


def _t5_bucket(n):
    n = jnp.maximum(n, 0)
    max_exact = REL_BUCKETS // 2
    nf = jnp.maximum(n, 1).astype(jnp.float32)
    large = max_exact + (jnp.log(nf / max_exact) / math.log(REL_MAX_DIST / max_exact)
                         * (REL_BUCKETS - max_exact)).astype(jnp.int32)
    large = jnp.minimum(large, REL_BUCKETS - 1)
    return jnp.where(n < max_exact, n, large)


RW_CHUNK = 64
RW_INV_BLOCK = 16
_NN = (((1,), (0,)), ((), ()))
_NT = (((1,), (1,)), ((), ()))
_BNN = (((2,), (1,)), ((0,), (0,)))
_BNT = (((2,), (2,)), ((0,), (0,)))


def _dot_f32(a, b, dims=_NN):
    return lax.dot_general(a, b, dims, precision=lax.Precision.HIGHEST,
                           preferred_element_type=jnp.float32)


def _dot_bf16x3(a, b, dims=_NN):
    f32, bf = jnp.float32, jnp.bfloat16
    a_hi, b_hi = a.astype(bf), b.astype(bf)
    a_lo = (a - a_hi.astype(f32)).astype(bf)
    b_lo = (b - b_hi.astype(f32)).astype(bf)
    out = lax.dot_general(a_hi, b_hi, dims, preferred_element_type=f32)
    out = out + lax.dot_general(a_hi, b_lo, dims, preferred_element_type=f32)
    return out + lax.dot_general(a_lo, b_hi, dims, preferred_element_type=f32)


def _bf16_terms(a):
    f32, bf = jnp.float32, jnp.bfloat16
    hi = a.astype(bf)
    r1 = a - hi.astype(f32)
    mid = r1.astype(bf)
    lo = (r1 - mid.astype(f32)).astype(bf)
    return hi, mid, lo


def _dot_lhs_split(a, b01):
    b = b01.astype(jnp.bfloat16)
    return sum(jnp.dot(t, b, preferred_element_type=jnp.float32) for t in _bf16_terms(a))


def _dot_rhs_split(a01, b):
    a = a01.astype(jnp.bfloat16)
    return sum(jnp.dot(a, t, preferred_element_type=jnp.float32) for t in _bf16_terms(b))


def _rwkv_kernel(z_ref, mu_ref, w0_ref, w2_ref, a0_ref, a2_ref, g2_ref, kk_ref, ka_ref, rk_ref,
                 gng_ref, gnb_ref, bd_ref, o_ref, m_ref, prev_ref, y_ref):
    C = z_ref.shape[1]
    N = RW_HEAD_DIM
    f32 = jnp.float32
    dot3 = _dot_bf16x3

    @pl.when(pl.program_id(1) == 0)
    def _():
        m_ref[...] = jnp.zeros(m_ref.shape, f32)
        prev_ref[...] = jnp.zeros(prev_ref.shape, f32)

    z = z_ref[0]
    row = lax.broadcasted_iota(jnp.int32, z.shape, 0)
    shifted = jnp.where(row == 0, prev_ref[...], pltpu.roll(z, 1, axis=0))
    prev_ref[...] = z[C - 1:C, :]
    zs = z + (shifted - z) * mu_ref[...]
    r = zs[:, 0:RW_DIM]
    k = zs[:, RW_DIM:2 * RW_DIM]
    v = zs[:, 2 * RW_DIM:3 * RW_DIM]
    o1 = 3 * RW_DIM
    wl = zs[:, o1:o1 + RW_DECAY_LORA]
    al = zs[:, o1 + RW_DECAY_LORA:o1 + RW_DECAY_LORA + RW_A_LORA]
    gl = zs[:, o1 + RW_DECAY_LORA + RW_A_LORA:]

    bd = bd_ref[...]
    log_w = -jax.nn.softplus(-(w0_ref[...] + dot3(jnp.tanh(wl), w2_ref[...]))) - 0.5
    ldec = -jnp.exp(log_w)
    a_lr = jax.nn.sigmoid(a0_ref[...] + dot3(al, a2_ref[...]))
    g = dot3(jax.nn.sigmoid(gl), g2_ref[...])
    kk = k * kk_ref[...]
    kk = kk * lax.rsqrt(jnp.maximum(_dot_lhs_split(kk * kk, bd), 1e-24))
    k2 = k * (1.0 + (a_lr - 1.0) * ka_ref[...])
    a_vec = -kk
    b_vec = kk * a_lr

    ti = lax.broadcasted_iota(jnp.int32, (C, C), 0)
    tj = lax.broadcasted_iota(jnp.int32, (C, C), 1)
    cum = _dot_rhs_split((ti >= tj).astype(f32), ldec)
    cum_last = cum[C - 1:C, :]
    w_incl = jnp.exp(cum)
    w_excl = jnp.exp(cum - ldec)
    w_inv = jnp.exp(-cum)
    w_end = jnp.exp(cum_last - cum)
    w_all = jnp.exp(cum_last)
    a_t = a_vec * w_excl
    r_t = r * w_incl
    b_t = b_vec * w_inv
    k_t = k2 * w_inv
    b_e = b_vec * w_end
    k_e = k2 * w_end

    strict = ti > tj
    incl = ti >= tj
    bi, bj = ti // RW_INV_BLOCK, tj // RW_INV_BLOCK
    same_blk = bi == bj
    pair_blk = jnp.logical_and((bi // 2) == (bj // 2), jnp.logical_not(same_blk))
    half_blk = (bi // 2) != (bj // 2)
    eye = (ti == tj).astype(f32)

    H = RW_HEADS
    heads = lambda x: jnp.stack([x[:, h * N:(h + 1) * N] for h in range(H)], axis=0)
    bmm = lambda a, b: dot3(a, b, _BNN)
    A, Rt, Bt, Kt, Be, Ke, V = (heads(t) for t in (a_t, r_t, b_t, k_t, b_e, k_e, v))
    gm = dot3(jnp.concatenate([A, Rt], axis=1), jnp.concatenate([Bt, Kt], axis=1), _BNT)
    a_ab = jnp.where(strict, gm[:, :C, :C], 0.0)
    a_ak = jnp.where(strict, gm[:, :C, C:], 0.0)
    a_rb = jnp.where(incl, gm[:, C:, :C], 0.0)
    a_rk = jnp.where(incl, gm[:, C:, C:], 0.0)
    d1 = jnp.where(same_blk, a_ab, 0.0)
    xinv = eye + d1
    d2 = bmm(d1, d1)
    xinv = xinv + bmm(xinv, d2)
    d4 = bmm(d2, d2)
    xinv = xinv + bmm(xinv, d4)
    d8 = bmm(d4, d4)
    xinv = xinv + bmm(xinv, d8)
    xinv = xinv + bmm(bmm(xinv, jnp.where(pair_blk, a_ab, 0.0)), xinv)
    xinv = xinv + bmm(bmm(xinv, jnp.where(half_blk, a_ab, 0.0)), xinv)

    av = bmm(jnp.concatenate([a_ak, a_rk], axis=1), V)
    p = bmm(xinv, jnp.concatenate([A, av[:, :C]], axis=2))
    qm = bmm(a_rb, p)
    q1 = Rt + qm[:, :, :N]
    q2 = qm[:, :, N:] + av[:, C:]
    gmat = bmm(jnp.swapaxes(Be, 1, 2), p)
    g1 = eye * heads(w_all) + gmat[:, :, :N]
    g2 = gmat[:, :, N:] + bmm(jnp.swapaxes(Ke, 1, 2), V)
    m = m_ref[...]
    yh = _dot_f32(q1, m, _BNN) + q2
    m_ref[...] = _dot_f32(g1, m, _BNN) + g2
    for h in range(H):
        y_ref[:, h * N:(h + 1) * N] = yh[h]

    y = y_ref[...]
    mean = _dot_lhs_split(y, bd) * (1.0 / N)
    yc = y - mean
    var = _dot_lhs_split(yc * yc, bd) * (1.0 / N)
    yn = yc * lax.rsqrt(var + RW_GN_EPS) * gng_ref[...] + gnb_ref[...]
    bonus = _dot_lhs_split(r * k2 * rk_ref[...], bd) * v
    o_ref[0] = ((yn + bonus) * g).astype(o_ref.dtype)


def _rwkv7_time_mix(z_rw, mu, w0, w2, a0, a2, g2, k_k, k_a, r_k, gn_g, gn_b, out_dtype):
    B, S, _ = z_rw.shape
    C = RW_CHUNK
    assert S % C == 0 and C == RW_HEAD_DIM and C % (4 * RW_INV_BLOCK) == 0
    hid = jnp.arange(RW_DIM) // RW_HEAD_DIM
    bd = (hid[:, None] == hid[None, :]).astype(jnp.float32)
    row = lambda a: a.reshape(1, -1)
    full = lambda shape: pl.BlockSpec(shape, lambda b, c: (0,) * len(shape))
    return pl.pallas_call(
        _rwkv_kernel,
        grid=(B, S // C),
        in_specs=[
            pl.BlockSpec((1, C, RW_COLS), lambda b, c: (b, c, 0)),
            full((1, RW_COLS)), full((1, RW_DIM)), full((RW_DECAY_LORA, RW_DIM)), full((1, RW_DIM)),
            full((RW_A_LORA, RW_DIM)), full((RW_GATE_LORA, RW_DIM)), full((1, RW_DIM)), full((1, RW_DIM)),
            full((1, RW_DIM)), full((1, RW_DIM)), full((1, RW_DIM)), full((RW_DIM, RW_DIM)),
        ],
        out_specs=pl.BlockSpec((1, C, RW_DIM), lambda b, c: (b, c, 0)),
        out_shape=jax.ShapeDtypeStruct((B, S, RW_DIM), out_dtype),
        scratch_shapes=[
            pltpu.VMEM((RW_HEADS, RW_HEAD_DIM, RW_HEAD_DIM), jnp.float32),
            pltpu.VMEM((1, RW_COLS), jnp.float32),
            pltpu.VMEM((C, RW_DIM), jnp.float32),
        ],
        compiler_params=pltpu.CompilerParams(dimension_semantics=("parallel", "arbitrary")),
        name="rwkv7_time_mix",
    )(z_rw, row(mu), row(w0), w2, row(a0), a2, g2, row(k_k), row(k_a), row(r_k), row(gn_g), row(gn_b), bd)


DSA_TQ = 256
MASK_NEG = -1e30
INT_MIN = -2 ** 31
KEY_NEG_INF = -2139095041
THRESH_BITS = 32


def _dsa_prep_kernel(z_ref, kvg_ref, kig_ref, kib_ref, qi_ref, kv_ref, ki_ref):
    z = z_ref[0]
    qi_ref[0] = z[:, :IDX_HEADS * IDX_DIM].astype(qi_ref.dtype)
    kv = z[:, 256:384]
    ms = jnp.mean(jnp.square(kv), -1, keepdims=True)
    kv_ref[0] = (kv * lax.rsqrt(ms + LN_EPS) * kvg_ref[...]).astype(kv_ref.dtype)
    ki = z[:, 384:448]
    mu = jnp.mean(ki, -1, keepdims=True)
    var = jnp.mean(jnp.square(ki - mu), -1, keepdims=True)
    ki_ref[0] = ((ki - mu) * lax.rsqrt(var + LN_EPS) * kig_ref[...] + kib_ref[...]).astype(ki_ref.dtype)


def _dsa_prep(z_small, kv_g, ki_g, ki_b, tm=512):
    B, S, W = z_small.shape
    return pl.pallas_call(
        _dsa_prep_kernel,
        grid=(B, S // tm),
        in_specs=[
            pl.BlockSpec((1, tm, W), lambda b, i: (b, i, 0)),
            pl.BlockSpec((1, DSA_LATENT), lambda b, i: (0, 0)),
            pl.BlockSpec((1, IDX_DIM), lambda b, i: (0, 0)),
            pl.BlockSpec((1, IDX_DIM), lambda b, i: (0, 0)),
        ],
        out_specs=[
            pl.BlockSpec((1, tm, IDX_HEADS * IDX_DIM), lambda b, i: (b, i, 0)),
            pl.BlockSpec((1, tm, DSA_LATENT), lambda b, i: (b, i, 0)),
            pl.BlockSpec((1, tm, IDX_DIM), lambda b, i: (b, i, 0)),
        ],
        out_shape=[
            jax.ShapeDtypeStruct((B, S, IDX_HEADS * IDX_DIM), MM_DTYPE),
            jax.ShapeDtypeStruct((B, S, DSA_LATENT), MM_DTYPE),
            jax.ShapeDtypeStruct((B, S, IDX_DIM), MM_DTYPE),
        ],
        name="dsa_prep",
    )(z_small, kv_g[None], ki_g[None], ki_b[None])


def _sortable_key(s):
    s = jnp.where(s == 0.0, 0.0, s)
    bits = pltpu.bitcast(s, jnp.int32)
    return bits ^ ((bits >> 31) & 0x7FFFFFFF)


def _col_count(mask_i32):
    tk, tq = mask_i32.shape
    return jnp.sum(mask_i32.reshape(tk // 8, 8, tq), axis=0)


def _dsa_kernel(bfar_ref, q_ref, qi_ref, wit_ref, kv_ref, kvt_ref, ki_ref, b0_ref, b1_ref, o_ref,
                key_ref, madd_ref, m_ref, l_ref, acc_ref, *, topk):
    tq = q_ref.shape[1]
    tk = tq
    i = pl.program_id(1)
    nj = i + 1
    f32 = jnp.float32
    krow = lax.broadcasted_iota(jnp.int32, (tk, tq), 0)
    qcol = lax.broadcasted_iota(jnp.int32, (tk, tq), 1)

    qi = qi_ref[0]
    wit = wit_ref[0] * (IDX_HEADS ** -0.5)

    def score_chunk(j, carry):
        off = pl.multiple_of(j * tk, tk)
        kc = ki_ref[0, pl.ds(off, tk), :]
        s = jnp.zeros((tk, tq), f32)
        for h in range(IDX_HEADS):
            d = lax.dot_general(kc, qi[:, h * IDX_DIM:(h + 1) * IDX_DIM],
                                (((1,), (1,)), ((), ())), preferred_element_type=f32)
            s = s + wit[h:h + 1, :] * jnp.maximum(d * (IDX_DIM ** -0.5), 0.0)
        causal = (krow + j * tk) <= (qcol + i * tq)
        s = jnp.where(causal, s, -jnp.inf)
        key_ref[j] = _sortable_key(s)
        return carry

    lax.fori_loop(0, nj, score_chunk, 0)

    def count_where(pred_fn):
        def body(j, acc):
            return acc + _col_count(pred_fn(key_ref[j], j).astype(jnp.int32))
        acc = lax.fori_loop(0, nj, body, jnp.zeros((8, tq), jnp.int32))
        return jnp.sum(acc, axis=0, keepdims=True)

    def bit_step(it, t_u):
        bit = THRESH_BITS - 1 - it
        cand_u = t_u | jnp.left_shift(jnp.int32(1), bit)
        cand = cand_u ^ INT_MIN
        cnt = count_where(lambda k, j: k >= cand)
        return jnp.where(cnt >= topk, cand_u, t_u)

    t_u = lax.fori_loop(0, THRESH_BITS, bit_step, jnp.zeros((1, tq), jnp.int32))
    thr = t_u ^ INT_MIN
    cnt_gt = count_where(lambda k, j: k > thr)
    cnt_ge = count_where(lambda k, j: k >= thr)
    is_neg = thr == KEY_NEG_INF
    need = jnp.logical_and(cnt_ge > topk, jnp.logical_not(is_neg))
    n_tie_take = topk - cnt_gt

    thr_open = jnp.where(is_neg, thr, thr - 1)
    any_need = jnp.max(need.astype(jnp.int32)) > 0

    @pl.when(jnp.logical_not(any_need))
    def _():
        def body(j, carry):
            madd_ref[j] = jnp.where(key_ref[j] > thr_open, 0.0, MASK_NEG)
            return carry
        lax.fori_loop(0, nj, body, 0)

    @pl.when(any_need)
    def _():
        s_len = tk * key_ref.shape[0]
        n_bits = max(1, int(math.ceil(math.log2(s_len))))

        def idx_step(it, p):
            bit = n_bits - 1 - it
            cand = p | jnp.left_shift(jnp.int32(1), bit)
            cnt = count_where(
                lambda k, j: jnp.where(k == thr, jnp.where((krow + j * tk) < cand, 1, 0), 0))
            return jnp.where(cnt < n_tie_take, cand, p)

        p_idx = lax.fori_loop(0, n_bits, idx_step, jnp.zeros((1, tq), jnp.int32))
        p_idx = jnp.where(need, p_idx, jnp.where(is_neg, -1, s_len))

        def body(j, carry):
            k = key_ref[j]
            tie_ok = jnp.where((krow + j * tk) <= p_idx, 0.0, MASK_NEG)
            madd_ref[j] = jnp.where(k > thr, 0.0, jnp.where(k == thr, tie_ok, MASK_NEG))
            return carry
        lax.fori_loop(0, nj, body, 0)

    m_ref[...] = jnp.full(m_ref.shape, MASK_NEG, f32)
    l_ref[...] = jnp.zeros(l_ref.shape, f32)
    acc_ref[...] = jnp.zeros(acc_ref.shape, f32)
    scale = DSA_LATENT ** -0.5

    def attend(j, bias_fn):
        off = pl.multiple_of(j * tk, tk)
        kc = kv_ref[0, pl.ds(off, tk), :]
        kct = kvt_ref[0, :, pl.ds(off, tk)]
        madd = madd_ref[j]
        for h in range(DSA_HEADS):
            qh = q_ref[0, :, h * DSA_LATENT:(h + 1) * DSA_LATENT]
            lg = lax.dot_general(kc, qh, (((1,), (1,)), ((), ())), preferred_element_type=f32)
            lg = lg * scale + bias_fn(h) + madd
            m_old = m_ref[h]
            m_new = jnp.maximum(m_old, jnp.max(lg, axis=0, keepdims=True))
            alpha = jnp.exp(m_old - m_new)
            p = jnp.exp(lg - m_new)
            l_ref[h] = alpha * l_ref[h] + jnp.sum(p, axis=0, keepdims=True)
            acc_ref[h] = alpha * acc_ref[h] + jnp.dot(kct, p.astype(kct.dtype),
                                                      preferred_element_type=f32)
            m_ref[h] = m_new

    def far_body(j, carry):
        attend(j, lambda h: bfar_ref[h])
        return carry

    lax.fori_loop(0, jnp.maximum(i - 1, 0), far_body, 0)

    @pl.when(i >= 1)
    def _():
        attend(i - 1, lambda h: b1_ref[h])

    attend(i, lambda h: b0_ref[h])

    for h in range(DSA_HEADS):
        out_t = acc_ref[h] / l_ref[h]
        o_ref[0, :, h * DSA_LATENT:(h + 1) * DSA_LATENT] = out_t.T.astype(o_ref.dtype)


def _dsa_bias_tiles(rel_bias, tq):
    dist = jnp.arange(2 * tq, dtype=jnp.int32)
    by_dist = rel_bias[_t5_bucket(dist)].T
    kk = jnp.arange(tq, dtype=jnp.int32)[:, None]
    qq = jnp.arange(tq, dtype=jnp.int32)[None, :]
    b0 = by_dist[:, jnp.maximum(qq - kk, 0)]
    b1 = by_dist[:, tq + qq - kk]
    return b0, b1, by_dist[:, -1]


def _dsa_attention(q, z_small, z_wi_t, kv_g, idx_k_g, idx_k_b, rel_bias, out_dtype):
    B, S, _ = q.shape
    tq = DSA_TQ
    topk = min(TOPK_MAX, S // 4)
    assert S % tq == 0 and tq >= REL_MAX_DIST and topk <= tq
    qi, ckv, kidx = _dsa_prep(z_small, kv_g, idx_k_g, idx_k_b)
    ckv_t = jnp.swapaxes(ckv, 1, 2)
    b0, b1, bfar = _dsa_bias_tiles(rel_bias, tq)
    nq = S // tq
    H, d = DSA_HEADS, DSA_LATENT
    grid_spec = pltpu.PrefetchScalarGridSpec(
        num_scalar_prefetch=0,
        grid=(B, nq),
        in_specs=[
            pl.BlockSpec(memory_space=pltpu.SMEM),
            pl.BlockSpec((1, tq, H * d), lambda b, i: (b, i, 0)),
            pl.BlockSpec((1, tq, IDX_HEADS * IDX_DIM), lambda b, i: (b, i, 0)),
            pl.BlockSpec((1, IDX_HEADS, tq), lambda b, i: (b, 0, i)),
            pl.BlockSpec((1, S, d), lambda b, i: (b, 0, 0)),
            pl.BlockSpec((1, d, S), lambda b, i: (b, 0, 0)),
            pl.BlockSpec((1, S, IDX_DIM), lambda b, i: (b, 0, 0)),
            pl.BlockSpec((H, tq, tq), lambda b, i: (0, 0, 0)),
            pl.BlockSpec((H, tq, tq), lambda b, i: (0, 0, 0)),
        ],
        out_specs=pl.BlockSpec((1, tq, H * d), lambda b, i: (b, i, 0)),
        scratch_shapes=[
            pltpu.VMEM((nq, tq, tq), jnp.int32),
            pltpu.VMEM((nq, tq, tq), jnp.float32),
            pltpu.VMEM((H, 1, tq), jnp.float32),
            pltpu.VMEM((H, 1, tq), jnp.float32),
            pltpu.VMEM((H, d, tq), jnp.float32),
        ],
    )
    return pl.pallas_call(
        partial(_dsa_kernel, topk=topk),
        grid_spec=grid_spec,
        out_shape=jax.ShapeDtypeStruct((B, S, H * d), out_dtype),
        compiler_params=pltpu.CompilerParams(vmem_limit_bytes=48 * 1024 * 1024),
        name="dsa_attention",
    )(bfar, q, qi, z_wi_t, ckv, ckv_t, kidx, b0, b1)


SC_CORES = 2
SC_SUBCORES = 16
SC_LANES = 16
PEER_SC_TOKENS = 8
PEER_SC_RING = 4
PEER_SC_UNROLL = 4
PEER_SLOTS = PEER_HEADS * PEER_TOPK


def _peer_sc_call(body, T, out_width, stage_width):
    mesh = plsc.VectorSubcoreMesh(core_axis_name="c", subcore_axis_name="s")
    return pl.kernel(
        body, mesh=mesh,
        out_type=jax.ShapeDtypeStruct((T, out_width), jnp.float32),
        scratch_types=[
            pltpu.VMEM((PEER_SC_TOKENS, PEER_SLOTS), jnp.int32),
            pltpu.VMEM((PEER_SC_TOKENS, stage_width), jnp.float32),
            pltpu.VMEM((PEER_SC_RING, PEER_TOPK, D_MODEL), jnp.float32),
            pltpu.VMEM((PEER_SC_TOKENS, out_width), jnp.float32),
            pltpu.SemaphoreType.DMA((PEER_SC_RING,)),
        ],
        compiler_params=pltpu.CompilerParams(needs_layout_passes=False),
    )


def _peer_sc_body(compute, zero_out, x_hbm, idx_hbm, tab_hbm, out_hbm, idx_v, x_v, rows_v, out_v, sems):
    T = idx_hbm.shape[0]
    tokens_per_worker = T // (SC_CORES * SC_SUBCORES)
    n_steps = PEER_SC_TOKENS * PEER_HEADS
    worker = lax.axis_index("s") * SC_CORES + lax.axis_index("c")
    base = worker * tokens_per_worker

    def gather(s, b):
        ids = idx_v[s // PEER_HEADS, pl.ds((s % PEER_HEADS) * PEER_TOPK, PEER_TOPK)]
        return pltpu.make_async_copy(tab_hbm.at[ids], rows_v.at[b], sems.at[b])

    @pl.loop(0, tokens_per_worker // PEER_SC_TOKENS)
    def _(blk):
        tok0 = base + blk * PEER_SC_TOKENS
        pltpu.sync_copy(idx_hbm.at[pl.ds(tok0, PEER_SC_TOKENS)], idx_v)
        pltpu.sync_copy(x_hbm.at[pl.ds(tok0, PEER_SC_TOKENS)], x_v)
        for b in range(PEER_SC_RING - 1):
            gather(b, b).start()
        if zero_out:
            @pl.loop(0, PEER_SC_TOKENS)
            def _(t):
                @plsc.parallel_loop(0, out_v.shape[1] // SC_LANES, unroll=PEER_SC_UNROLL)
                def _(c):
                    out_v[t, pl.ds(pl.multiple_of(c * SC_LANES, SC_LANES), SC_LANES)] = (
                        jnp.zeros((SC_LANES,), jnp.float32))

        @pl.loop(0, n_steps, step=PEER_SC_RING)
        def _(s0):
            for b in range(PEER_SC_RING):
                s = s0 + b
                gather(s, b).wait()

                @pl.when(s + PEER_SC_RING - 1 < n_steps)
                def _():
                    gather(s + PEER_SC_RING - 1, (b + PEER_SC_RING - 1) % PEER_SC_RING).start()

                compute(s // PEER_HEADS, s % PEER_HEADS, b, x_v, rows_v, out_v)

        pltpu.sync_copy(out_v, out_hbm.at[pl.ds(tok0, PEER_SC_TOKENS)])


def _peer_dots_compute(t, hd, b, h_v, rows_v, dots_v):
    lane = lax.iota(jnp.int32, SC_LANES)

    def col_step(c, accs):
        off = pl.multiple_of(c * SC_LANES, SC_LANES)
        hv = h_v[t, pl.ds(off, SC_LANES)]
        return tuple(accs[r] + rows_v[b, r, pl.ds(off, SC_LANES)] * hv for r in range(PEER_TOPK))

    accs = plsc.parallel_loop(
        0, D_MODEL // SC_LANES, unroll=PEER_SC_UNROLL,
        carry=tuple(jnp.zeros((SC_LANES,), jnp.float32) for _ in range(PEER_TOPK)))(col_step)
    res = jnp.zeros((SC_LANES,), jnp.float32)
    for r in range(PEER_TOPK):
        res = jnp.where(lane == r, jnp.sum(accs[r]), res)
    dots_v[t, pl.ds(hd * PEER_TOPK, PEER_TOPK)] = res


def _peer_mix_compute(t, hd, b, act_v, rows_v, out_v):
    tvec = jnp.full((SC_LANES,), t, jnp.int32)
    weights = [plsc.load_gather(act_v, [tvec, jnp.full((SC_LANES,), hd * PEER_TOPK + r, jnp.int32)])
               for r in range(PEER_TOPK)]

    @plsc.parallel_loop(0, D_MODEL // SC_LANES, unroll=PEER_SC_UNROLL)
    def _(c):
        off = pl.multiple_of(c * SC_LANES, SC_LANES)
        a = rows_v[b, 0, pl.ds(off, SC_LANES)] * weights[0]
        for r in range(1, PEER_TOPK):
            a = a + rows_v[b, r, pl.ds(off, SC_LANES)] * weights[r]
        plsc.addupdate(out_v.at[t, pl.ds(off, SC_LANES)], a)


def _peer_expert_dots(h, experts, u_tab):
    T = h.shape[0]
    assert T % (SC_CORES * SC_SUBCORES * PEER_SC_TOKENS) == 0
    assert (PEER_SC_TOKENS * PEER_HEADS) % PEER_SC_RING == 0 and PEER_TOPK == SC_LANES
    body = partial(_peer_sc_body, _peer_dots_compute, False)
    return _peer_sc_call(body, T, PEER_SLOTS, D_MODEL)(h, experts, u_tab)


def _peer_expert_mix(act, experts, v_tab):
    T = act.shape[0]
    assert T % (SC_CORES * SC_SUBCORES * PEER_SC_TOKENS) == 0
    body = partial(_peer_sc_body, _peer_mix_compute, True)
    return _peer_sc_call(body, T, D_MODEL, PEER_SLOTS)(act, experts, v_tab)


PEER_TM = 256


def _extract_topk(s, k, payload=None):
    R = s.shape[0]
    riota = lax.broadcasted_iota(jnp.int32, s.shape, 0)
    vals, rows = [], []
    for _ in range(k):
        m = jnp.max(s, axis=0, keepdims=True)
        pos = jnp.min(jnp.where(s == m, riota, R), axis=0, keepdims=True)
        hit = riota == pos
        vals.append(m)
        rows.append(pos if payload is None else jnp.max(jnp.where(hit, payload, -1), axis=0, keepdims=True))
        s = jnp.where(hit, -jnp.inf, s)
    return jnp.concatenate(vals, axis=0), jnp.concatenate(rows, axis=0)


def _peer_route_kernel(x_ref, sc_ref, sh_ref, wq_ref, keys_ref, h_ref, ex_ref, gate_ref, q_ref, ext_ref, gt_ref):
    K = PEER_TOPK
    f32 = jnp.float32
    h = x_ref[0] * (1.0 + sc_ref[0]) + sh_ref[0]
    h_ref[0] = h
    q = jnp.dot(h.astype(wq_ref.dtype), wq_ref[...], preferred_element_type=f32)
    for hd in range(PEER_HEADS):
        q_ref[hd] = q[:, hd * PEER_KEY_DIM:(hd + 1) * PEER_KEY_DIM].astype(q_ref.dtype)

    def head_body(hd, carry):
        qh = q_ref[hd]
        s1 = lax.dot_general(keys_ref[hd, 0], qh[:, :PEER_HALF], _NT, preferred_element_type=f32)
        s2 = lax.dot_general(keys_ref[hd, 1], qh[:, PEER_HALF:], _NT, preferred_element_type=f32)
        v1, i1 = _extract_topk(s1, K)
        v2, i2 = _extract_topk(s2, K)
        tm = v1.shape[1]
        cand_rows, cidx_rows = [], []
        for a in range(K):
            nb = K // (a + 1)
            cand_rows.append(v1[a:a + 1, :] + v2[:nb, :])
            cidx_rows.append(i1[a:a + 1, :] * PEER_N_KEYS + i2[:nb, :])
        n_pad = -sum(r.shape[0] for r in cand_rows) % 8
        cand_rows.append(jnp.full((n_pad, tm), -jnp.inf, f32))
        cidx_rows.append(jnp.full((n_pad, tm), -1, jnp.int32))
        cand = jnp.concatenate(cand_rows, axis=0)
        cidx = jnp.concatenate(cidx_rows, axis=0)
        top_s, experts = _extract_topk(cand, K, payload=cidx)
        e = jnp.exp(top_s - top_s[0:1, :])
        gt_ref[pl.ds(pl.multiple_of(hd * K, K), K), :] = e / jnp.sum(e, axis=0, keepdims=True)
        ext_ref[pl.ds(pl.multiple_of(hd * K, K), K), :] = experts
        return carry

    lax.fori_loop(0, PEER_HEADS, head_body, 0)
    ex_ref[0] = ext_ref[...].T
    gate_ref[0] = gt_ref[...].T


def _peer_route(x, sc, sh, w_pq, sub_keys):
    B, S, D = x.shape
    tm = PEER_TM
    return pl.pallas_call(
        _peer_route_kernel,
        grid=(B, S // tm),
        in_specs=[
            pl.BlockSpec((1, tm, D), lambda b, i: (b, i, 0)),
            pl.BlockSpec((1, 1, D), lambda b, i: (b, 0, 0)),
            pl.BlockSpec((1, 1, D), lambda b, i: (b, 0, 0)),
            pl.BlockSpec((D, PEER_HEADS * PEER_KEY_DIM), lambda b, i: (0, 0)),
            pl.BlockSpec((PEER_HEADS, 2, PEER_N_KEYS, PEER_HALF), lambda b, i: (0, 0, 0, 0)),
        ],
        out_specs=[
            pl.BlockSpec((1, tm, D), lambda b, i: (b, i, 0)),
            pl.BlockSpec((1, tm, PEER_SLOTS), lambda b, i: (b, i, 0)),
            pl.BlockSpec((1, tm, PEER_SLOTS), lambda b, i: (b, i, 0)),
        ],
        out_shape=[
            jax.ShapeDtypeStruct((B, S, D), jnp.float32),
            jax.ShapeDtypeStruct((B, S, PEER_SLOTS), jnp.int32),
            jax.ShapeDtypeStruct((B, S, PEER_SLOTS), jnp.float32),
        ],
        scratch_shapes=[
            pltpu.VMEM((PEER_HEADS, tm, PEER_KEY_DIM), MM_DTYPE),
            pltpu.VMEM((PEER_SLOTS, tm), jnp.int32),
            pltpu.VMEM((PEER_SLOTS, tm), jnp.float32),
        ],
        compiler_params=pltpu.CompilerParams(dimension_semantics=("parallel", "parallel")),
        name="peer_route",
    )(x, sc[:, None, :], sh[:, None, :], w_pq.astype(MM_DTYPE), sub_keys.astype(MM_DTYPE))


def _peer_pre(h2, experts, u_tab):
    B, S, D = h2.shape
    return _peer_expert_dots(h2.reshape(B * S, D), experts.reshape(B * S, PEER_SLOTS), u_tab)


def _peer_out(pre, experts, gates, v_tab):
    B, S, _ = experts.shape
    act = jax.nn.gelu(pre, approximate=False) * gates.reshape(B * S, PEER_SLOTS)
    return _peer_expert_mix(act, experts.reshape(B * S, PEER_SLOTS), v_tab).reshape(B, S, D_MODEL)


def kernel(x, c, w_ada, b_ada, w_in, rw_mu, rw_w0, rw_w2, rw_a0, rw_a2, rw_g2, rw_k_k, rw_k_a, rw_r_k, rw_gn_g, rw_gn_b, dsa_kv_g, idx_k_g, idx_k_b, rel_bias, w_br_a, w_br_b, w_out, ln1_g, ln1_b, peer_wq, peer_keys, peer_u, peer_v, ln2_g, ln2_b):
    l = 0
    mod = jax.nn.silu(c) @ w_ada[l] + b_ada[l]

    w_rw, w_q, w_kv, w_qi, w_ki, w_wi, w_ga, w_gb = jnp.split(w_in[l], _split_points(IN_SIZES), axis=-1)
    small_pad = jnp.zeros((D_MODEL, 512 - 256 - 128 - 64 - 4), w_in.dtype)
    w_small = jnp.concatenate([w_qi, w_kv, w_ki, w_wi, small_pad], axis=-1)
    w_gates = jnp.concatenate([w_ga, w_gb], axis=-1)

    def project_and_rwkv(x, sc1, sh1):
        z_rw = _mod_matmul(x, sc1, sh1, w_rw, jnp.float32, tn=896)
        z_q = _mod_matmul(x, sc1, sh1, w_q, MM_DTYPE, tn=1024)
        z_small = _mod_matmul(x, sc1, sh1, w_small, jnp.float32, tn=512)
        z_g = _mod_matmul(x, sc1, sh1, w_gates, jnp.float32, tn=1024)
        y_a = _rwkv7_time_mix(z_rw, rw_mu[l], rw_w0[l], rw_w2[l], rw_a0[l], rw_a2[l], rw_g2[l],
                              rw_k_k[l], rw_k_a[l], rw_r_k[l], rw_gn_g[l], rw_gn_b[l], jnp.float32) @ w_br_a[l]
        return y_a, z_q, z_small, z_g

    def dsa_and_merge(x, gt1, y_a, z_q, z_small, z_g):
        z_ga, z_gb = z_g[..., :D_MODEL], z_g[..., D_MODEL:]
        z_wi_t = jnp.swapaxes(z_small[..., 448:448 + IDX_HEADS], 1, 2)
        y_b = _dsa_attention(z_q, z_small, z_wi_t, dsa_kv_g[l], idx_k_g[l], idx_k_b[l],
                             rel_bias, jnp.float32) @ w_br_b[l]
        merged = jax.nn.sigmoid(z_ga) * y_a + jax.nn.sigmoid(z_gb) * y_b
        mix = merged @ w_out[l]
        return _layer_norm(DEEPNORM_ALPHA * x + gt1[:, None] * mix, ln1_g[l], ln1_b[l])

    def finish(st):
        y2 = _peer_out(st["pre"], st["experts"], st["gates"], peer_v[l])
        return _layer_norm(DEEPNORM_ALPHA * st["x1"] + st["gt2"][:, None] * y2, ln2_g[l], ln2_b[l])

    gsz = x.shape[0] // BATCH_GROUPS
    outs = []
    prev = None
    for g in range(BATCH_GROUPS):
        sh1, sc1, gt1, sh2, sc2, gt2 = jnp.split(mod[g * gsz:(g + 1) * gsz], 6, axis=-1)
        x_g = x[g * gsz:(g + 1) * gsz]
        if prev is not None:
            x_g, prev["experts"] = lax.optimization_barrier((x_g, prev["experts"]))
        proj = project_and_rwkv(x_g, sc1, sh1)
        if prev is not None:
            proj, prev["pre"] = lax.optimization_barrier((proj, prev["pre"]))
            done = finish(prev)
        x1 = dsa_and_merge(x_g, gt1, *proj)
        if prev is not None:
            x1, done = lax.optimization_barrier((x1, done))
            outs.append(done)
        h2, experts, gates = _peer_route(x1, sc2, sh2, peer_wq[l], peer_keys[l])
        prev = dict(x1=x1, gt2=gt2, experts=experts, gates=gates, pre=_peer_pre(h2, experts, peer_u[l]))
    outs.append(finish(prev))
    return jnp.concatenate(outs, axis=0)
```

```python
import math
from functools import partial

import jax
import jax.numpy as jnp
import numpy as np
from jax import lax
from jax.experimental import pallas as pl
from jax.experimental.pallas import tpu as pltpu
from jax.experimental.pallas import tpu_sc as plsc

D_MODEL = 1024
RW_HEADS = 8
RW_HEAD_DIM = 64
RW_DIM = 512
RW_DECAY_LORA = 64
RW_A_LORA = 64
RW_GATE_LORA = 128
RW_SIZES = (RW_DIM, RW_DIM, RW_DIM, RW_DECAY_LORA, RW_A_LORA, RW_GATE_LORA)
RW_COLS = 3 * RW_DIM + RW_DECAY_LORA + RW_A_LORA + RW_GATE_LORA
RW_GN_EPS = 64e-5
DSA_HEADS = 8
DSA_LATENT = 128
DSA_Q_DIM = DSA_HEADS * DSA_LATENT
IDX_HEADS = 4
IDX_DIM = 64
TOPK_MAX = 256
Q_BLOCK = 128
REL_BUCKETS = 32
REL_MAX_DIST = 128
IN_SIZES = (RW_COLS, DSA_Q_DIM, DSA_LATENT, IDX_HEADS * IDX_DIM, IDX_DIM, IDX_HEADS, D_MODEL, D_MODEL)
IN_COLS = sum(IN_SIZES)
PEER_HEADS = 8
PEER_N_KEYS = 128
PEER_KEY_DIM = 128
PEER_HALF = 64
PEER_TOPK = 16
PEER_CHUNK = 128
LN_EPS = 1e-5
DEPTH = 1
DEEPNORM_ALPHA = (2.0 * DEPTH) ** 0.25

LANES = 128
MM_DTYPE = jnp.bfloat16
BATCH_GROUPS = 8


def _split_points(sizes):
    return np.cumsum(sizes)[:-1].tolist()


def _mod_matmul_kernel(x_ref, sc_ref, sh_ref, w_ref, o_ref):
    h = x_ref[0] * (1.0 + sc_ref[0]) + sh_ref[0]
    o_ref[0] = jnp.dot(h.astype(w_ref.dtype), w_ref[...],
                       preferred_element_type=jnp.float32).astype(o_ref.dtype)


def _mod_matmul(x, sc, sh, w, out_dtype, tn, tm=512):
    B, S, D = x.shape
    N = w.shape[1]
    return pl.pallas_call(
        _mod_matmul_kernel,
        grid=(B, S // tm, N // tn),
        in_specs=[
            pl.BlockSpec((1, tm, D), lambda b, i, j: (b, i, 0)),
            pl.BlockSpec((1, 1, D), lambda b, i, j: (b, 0, 0)),
            pl.BlockSpec((1, 1, D), lambda b, i, j: (b, 0, 0)),
            pl.BlockSpec((D, tn), lambda b, i, j: (0, j)),
        ],
        out_specs=pl.BlockSpec((1, tm, tn), lambda b, i, j: (b, i, j)),
        out_shape=jax.ShapeDtypeStruct((B, S, N), out_dtype),
        name="mod_matmul",
    )(x, sc[:, None, :], sh[:, None, :], w.astype(MM_DTYPE))


def _layer_norm(x, g, b):
    mu = jnp.mean(x, -1, keepdims=True)
    var = jnp.mean(jnp.square(x - mu), -1, keepdims=True)
    return (x - mu) * lax.rsqrt(var + LN_EPS) * g + b


def _t5_bucket(n):
    n = jnp.maximum(n, 0)
    max_exact = REL_BUCKETS // 2
    nf = jnp.maximum(n, 1).astype(jnp.float32)
    large = max_exact + (jnp.log(nf / max_exact) / math.log(REL_MAX_DIST / max_exact)
                         * (REL_BUCKETS - max_exact)).astype(jnp.int32)
    large = jnp.minimum(large, REL_BUCKETS - 1)
    return jnp.where(n < max_exact, n, large)


RW_CHUNK = 64
RW_INV_BLOCK = 16
_NN = (((1,), (0,)), ((), ()))
_NT = (((1,), (1,)), ((), ()))
_BNN = (((2,), (1,)), ((0,), (0,)))
_BNT = (((2,), (2,)), ((0,), (0,)))


def _dot_f32(a, b, dims=_NN):
    return lax.dot_general(a, b, dims, precision=lax.Precision.HIGHEST,
                           preferred_element_type=jnp.float32)


def _dot_bf16x3(a, b, dims=_NN):
    f32, bf = jnp.float32, jnp.bfloat16
    a_hi, b_hi = a.astype(bf), b.astype(bf)
    a_lo = (a - a_hi.astype(f32)).astype(bf)
    b_lo = (b - b_hi.astype(f32)).astype(bf)
    out = lax.dot_general(a_hi, b_hi, dims, preferred_element_type=f32)
    out = out + lax.dot_general(a_hi, b_lo, dims, preferred_element_type=f32)
    return out + lax.dot_general(a_lo, b_hi, dims, preferred_element_type=f32)


def _bf16_terms(a):
    f32, bf = jnp.float32, jnp.bfloat16
    hi = a.astype(bf)
    r1 = a - hi.astype(f32)
    mid = r1.astype(bf)
    lo = (r1 - mid.astype(f32)).astype(bf)
    return hi, mid, lo


def _dot_lhs_split(a, b01):
    b = b01.astype(jnp.bfloat16)
    return sum(jnp.dot(t, b, preferred_element_type=jnp.float32) for t in _bf16_terms(a))


def _dot_rhs_split(a01, b):
    a = a01.astype(jnp.bfloat16)
    return sum(jnp.dot(a, t, preferred_element_type=jnp.float32) for t in _bf16_terms(b))


def _rwkv_kernel(z_ref, mu_ref, w0_ref, w2_ref, a0_ref, a2_ref, g2_ref, kk_ref, ka_ref, rk_ref,
                 gng_ref, gnb_ref, bd_ref, o_ref, m_ref, prev_ref, y_ref):
    C = z_ref.shape[1]
    N = RW_HEAD_DIM
    f32 = jnp.float32
    dot3 = _dot_bf16x3

    @pl.when(pl.program_id(1) == 0)
    def _():
        m_ref[...] = jnp.zeros(m_ref.shape, f32)
        prev_ref[...] = jnp.zeros(prev_ref.shape, f32)

    z = z_ref[0]
    row = lax.broadcasted_iota(jnp.int32, z.shape, 0)
    shifted = jnp.where(row == 0, prev_ref[...], pltpu.roll(z, 1, axis=0))
    prev_ref[...] = z[C - 1:C, :]
    zs = z + (shifted - z) * mu_ref[...]
    r = zs[:, 0:RW_DIM]
    k = zs[:, RW_DIM:2 * RW_DIM]
    v = zs[:, 2 * RW_DIM:3 * RW_DIM]
    o1 = 3 * RW_DIM
    wl = zs[:, o1:o1 + RW_DECAY_LORA]
    al = zs[:, o1 + RW_DECAY_LORA:o1 + RW_DECAY_LORA + RW_A_LORA]
    gl = zs[:, o1 + RW_DECAY_LORA + RW_A_LORA:]

    bd = bd_ref[...]
    log_w = -jax.nn.softplus(-(w0_ref[...] + dot3(jnp.tanh(wl), w2_ref[...]))) - 0.5
    ldec = -jnp.exp(log_w)
    a_lr = jax.nn.sigmoid(a0_ref[...] + dot3(al, a2_ref[...]))
    g = dot3(jax.nn.sigmoid(gl), g2_ref[...])
    kk = k * kk_ref[...]
    kk = kk * lax.rsqrt(jnp.maximum(_dot_lhs_split(kk * kk, bd), 1e-24))
    k2 = k * (1.0 + (a_lr - 1.0) * ka_ref[...])
    a_vec = -kk
    b_vec = kk * a_lr

    ti = lax.broadcasted_iota(jnp.int32, (C, C), 0)
    tj = lax.broadcasted_iota(jnp.int32, (C, C), 1)
    cum = _dot_rhs_split((ti >= tj).astype(f32), ldec)
    cum_last = cum[C - 1:C, :]
    w_incl = jnp.exp(cum)
    w_excl = jnp.exp(cum - ldec)
    w_inv = jnp.exp(-cum)
    w_end = jnp.exp(cum_last - cum)
    w_all = jnp.exp(cum_last)
    a_t = a_vec * w_excl
    r_t = r * w_incl
    b_t = b_vec * w_inv
    k_t = k2 * w_inv
    b_e = b_vec * w_end
    k_e = k2 * w_end

    strict = ti > tj
    incl = ti >= tj
    bi, bj = ti // RW_INV_BLOCK, tj // RW_INV_BLOCK
    same_blk = bi == bj
    pair_blk = jnp.logical_and((bi // 2) == (bj // 2), jnp.logical_not(same_blk))
    half_blk = (bi // 2) != (bj // 2)
    eye = (ti == tj).astype(f32)

    H = RW_HEADS
    heads = lambda x: jnp.stack([x[:, h * N:(h + 1) * N] for h in range(H)], axis=0)
    bmm = lambda a, b: dot3(a, b, _BNN)
    A, Rt, Bt, Kt, Be, Ke, V = (heads(t) for t in (a_t, r_t, b_t, k_t, b_e, k_e, v))
    gm = dot3(jnp.concatenate([A, Rt], axis=1), jnp.concatenate([Bt, Kt], axis=1), _BNT)
    a_ab = jnp.where(strict, gm[:, :C, :C], 0.0)
    a_ak = jnp.where(strict, gm[:, :C, C:], 0.0)
    a_rb = jnp.where(incl, gm[:, C:, :C], 0.0)
    a_rk = jnp.where(incl, gm[:, C:, C:], 0.0)
    d1 = jnp.where(same_blk, a_ab, 0.0)
    xinv = eye + d1
    d2 = bmm(d1, d1)
    xinv = xinv + bmm(xinv, d2)
    d4 = bmm(d2, d2)
    xinv = xinv + bmm(xinv, d4)
    d8 = bmm(d4, d4)
    xinv = xinv + bmm(xinv, d8)
    xinv = xinv + bmm(bmm(xinv, jnp.where(pair_blk, a_ab, 0.0)), xinv)
    xinv = xinv + bmm(bmm(xinv, jnp.where(half_blk, a_ab, 0.0)), xinv)

    av = bmm(jnp.concatenate([a_ak, a_rk], axis=1), V)
    p = bmm(xinv, jnp.concatenate([A, av[:, :C]], axis=2))
    qm = bmm(a_rb, p)
    q1 = Rt + qm[:, :, :N]
    q2 = qm[:, :, N:] + av[:, C:]
    gmat = bmm(jnp.swapaxes(Be, 1, 2), p)
    g1 = eye * heads(w_all) + gmat[:, :, :N]
    g2 = gmat[:, :, N:] + bmm(jnp.swapaxes(Ke, 1, 2), V)
    m = m_ref[...]
    yh = _dot_f32(q1, m, _BNN) + q2
    m_ref[...] = _dot_f32(g1, m, _BNN) + g2
    for h in range(H):
        y_ref[:, h * N:(h + 1) * N] = yh[h]

    y = y_ref[...]
    mean = _dot_lhs_split(y, bd) * (1.0 / N)
    yc = y - mean
    var = _dot_lhs_split(yc * yc, bd) * (1.0 / N)
    yn = yc * lax.rsqrt(var + RW_GN_EPS) * gng_ref[...] + gnb_ref[...]
    bonus = _dot_lhs_split(r * k2 * rk_ref[...], bd) * v
    o_ref[0] = ((yn + bonus) * g).astype(o_ref.dtype)


def _rwkv7_time_mix(z_rw, mu, w0, w2, a0, a2, g2, k_k, k_a, r_k, gn_g, gn_b, out_dtype):
    B, S, _ = z_rw.shape
    C = RW_CHUNK
    assert S % C == 0 and C == RW_HEAD_DIM and C % (4 * RW_INV_BLOCK) == 0
    hid = jnp.arange(RW_DIM) // RW_HEAD_DIM
    bd = (hid[:, None] == hid[None, :]).astype(jnp.float32)
    row = lambda a: a.reshape(1, -1)
    full = lambda shape: pl.BlockSpec(shape, lambda b, c: (0,) * len(shape))
    return pl.pallas_call(
        _rwkv_kernel,
        grid=(B, S // C),
        in_specs=[
            pl.BlockSpec((1, C, RW_COLS), lambda b, c: (b, c, 0)),
            full((1, RW_COLS)), full((1, RW_DIM)), full((RW_DECAY_LORA, RW_DIM)), full((1, RW_DIM)),
            full((RW_A_LORA, RW_DIM)), full((RW_GATE_LORA, RW_DIM)), full((1, RW_DIM)), full((1, RW_DIM)),
            full((1, RW_DIM)), full((1, RW_DIM)), full((1, RW_DIM)), full((RW_DIM, RW_DIM)),
        ],
        out_specs=pl.BlockSpec((1, C, RW_DIM), lambda b, c: (b, c, 0)),
        out_shape=jax.ShapeDtypeStruct((B, S, RW_DIM), out_dtype),
        scratch_shapes=[
            pltpu.VMEM((RW_HEADS, RW_HEAD_DIM, RW_HEAD_DIM), jnp.float32),
            pltpu.VMEM((1, RW_COLS), jnp.float32),
            pltpu.VMEM((C, RW_DIM), jnp.float32),
        ],
        compiler_params=pltpu.CompilerParams(dimension_semantics=("parallel", "arbitrary")),
        name="rwkv7_time_mix",
    )(z_rw, row(mu), row(w0), w2, row(a0), a2, g2, row(k_k), row(k_a), row(r_k), row(gn_g), row(gn_b), bd)


DSA_TQ = 256
MASK_NEG = -1e30
INT_MIN = -2 ** 31
KEY_NEG_INF = -2139095041
THRESH_BITS = 32


def _dsa_prep_kernel(z_ref, kvg_ref, kig_ref, kib_ref, qi_ref, kv_ref, ki_ref):
    z = z_ref[0]
    qi_ref[0] = z[:, :IDX_HEADS * IDX_DIM].astype(qi_ref.dtype)
    kv = z[:, 256:384]
    ms = jnp.mean(jnp.square(kv), -1, keepdims=True)
    kv_ref[0] = (kv * lax.rsqrt(ms + LN_EPS) * kvg_ref[...]).astype(kv_ref.dtype)
    ki = z[:, 384:448]
    mu = jnp.mean(ki, -1, keepdims=True)
    var = jnp.mean(jnp.square(ki - mu), -1, keepdims=True)
    ki_ref[0] = ((ki - mu) * lax.rsqrt(var + LN_EPS) * kig_ref[...] + kib_ref[...]).astype(ki_ref.dtype)


def _dsa_prep(z_small, kv_g, ki_g, ki_b, tm=512):
    B, S, W = z_small.shape
    return pl.pallas_call(
        _dsa_prep_kernel,
        grid=(B, S // tm),
        in_specs=[
            pl.BlockSpec((1, tm, W), lambda b, i: (b, i, 0)),
            pl.BlockSpec((1, DSA_LATENT), lambda b, i: (0, 0)),
            pl.BlockSpec((1, IDX_DIM), lambda b, i: (0, 0)),
            pl.BlockSpec((1, IDX_DIM), lambda b, i: (0, 0)),
        ],
        out_specs=[
            pl.BlockSpec((1, tm, IDX_HEADS * IDX_DIM), lambda b, i: (b, i, 0)),
            pl.BlockSpec((1, tm, DSA_LATENT), lambda b, i: (b, i, 0)),
            pl.BlockSpec((1, tm, IDX_DIM), lambda b, i: (b, i, 0)),
        ],
        out_shape=[
            jax.ShapeDtypeStruct((B, S, IDX_HEADS * IDX_DIM), MM_DTYPE),
            jax.ShapeDtypeStruct((B, S, DSA_LATENT), MM_DTYPE),
            jax.ShapeDtypeStruct((B, S, IDX_DIM), MM_DTYPE),
        ],
        name="dsa_prep",
    )(z_small, kv_g[None], ki_g[None], ki_b[None])


def _sortable_key(s):
    s = jnp.where(s == 0.0, 0.0, s)
    bits = pltpu.bitcast(s, jnp.int32)
    return bits ^ ((bits >> 31) & 0x7FFFFFFF)


def _col_count(mask_i32):
    tk, tq = mask_i32.shape
    return jnp.sum(mask_i32.reshape(tk // 8, 8, tq), axis=0)


def _dsa_kernel(bfar_ref, q_ref, qi_ref, wit_ref, kv_ref, kvt_ref, ki_ref, b0_ref, b1_ref, o_ref,
                key_ref, madd_ref, m_ref, l_ref, acc_ref, *, topk):
    tq = q_ref.shape[1]
    tk = tq
    i = pl.program_id(1)
    nj = i + 1
    f32 = jnp.float32
    krow = lax.broadcasted_iota(jnp.int32, (tk, tq), 0)
    qcol = lax.broadcasted_iota(jnp.int32, (tk, tq), 1)

    qi = qi_ref[0]
    wit = wit_ref[0] * (IDX_HEADS ** -0.5)

    def score_chunk(j, carry):
        off = pl.multiple_of(j * tk, tk)
        kc = ki_ref[0, pl.ds(off, tk), :]
        s = jnp.zeros((tk, tq), f32)
        for h in range(IDX_HEADS):
            d = lax.dot_general(kc, qi[:, h * IDX_DIM:(h + 1) * IDX_DIM],
                                (((1,), (1,)), ((), ())), preferred_element_type=f32)
            s = s + wit[h:h + 1, :] * jnp.maximum(d * (IDX_DIM ** -0.5), 0.0)
        causal = (krow + j * tk) <= (qcol + i * tq)
        s = jnp.where(causal, s, -jnp.inf)
        key_ref[j] = _sortable_key(s)
        return carry

    lax.fori_loop(0, nj, score_chunk, 0)

    def count_where(pred_fn):
        def body(j, acc):
            return acc + _col_count(pred_fn(key_ref[j], j).astype(jnp.int32))
        acc = lax.fori_loop(0, nj, body, jnp.zeros((8, tq), jnp.int32))
        return jnp.sum(acc, axis=0, keepdims=True)

    def bit_step(it, t_u):
        bit = THRESH_BITS - 1 - it
        cand_u = t_u | jnp.left_shift(jnp.int32(1), bit)
        cand = cand_u ^ INT_MIN
        cnt = count_where(lambda k, j: k >= cand)
        return jnp.where(cnt >= topk, cand_u, t_u)

    t_u = lax.fori_loop(0, THRESH_BITS, bit_step, jnp.zeros((1, tq), jnp.int32))
    thr = t_u ^ INT_MIN
    cnt_gt = count_where(lambda k, j: k > thr)
    cnt_ge = count_where(lambda k, j: k >= thr)
    is_neg = thr == KEY_NEG_INF
    need = jnp.logical_and(cnt_ge > topk, jnp.logical_not(is_neg))
    n_tie_take = topk - cnt_gt

    thr_open = jnp.where(is_neg, thr, thr - 1)
    any_need = jnp.max(need.astype(jnp.int32)) > 0

    @pl.when(jnp.logical_not(any_need))
    def _():
        def body(j, carry):
            madd_ref[j] = jnp.where(key_ref[j] > thr_open, 0.0, MASK_NEG)
            return carry
        lax.fori_loop(0, nj, body, 0)

    @pl.when(any_need)
    def _():
        s_len = tk * key_ref.shape[0]
        n_bits = max(1, int(math.ceil(math.log2(s_len))))

        def idx_step(it, p):
            bit = n_bits - 1 - it
            cand = p | jnp.left_shift(jnp.int32(1), bit)
            cnt = count_where(
                lambda k, j: jnp.where(k == thr, jnp.where((krow + j * tk) < cand, 1, 0), 0))
            return jnp.where(cnt < n_tie_take, cand, p)

        p_idx = lax.fori_loop(0, n_bits, idx_step, jnp.zeros((1, tq), jnp.int32))
        p_idx = jnp.where(need, p_idx, jnp.where(is_neg, -1, s_len))

        def body(j, carry):
            k = key_ref[j]
            tie_ok = jnp.where((krow + j * tk) <= p_idx, 0.0, MASK_NEG)
            madd_ref[j] = jnp.where(k > thr, 0.0, jnp.where(k == thr, tie_ok, MASK_NEG))
            return carry
        lax.fori_loop(0, nj, body, 0)

    m_ref[...] = jnp.full(m_ref.shape, MASK_NEG, f32)
    l_ref[...] = jnp.zeros(l_ref.shape, f32)
    acc_ref[...] = jnp.zeros(acc_ref.shape, f32)
    scale = DSA_LATENT ** -0.5

    def attend(j, bias_fn):
        off = pl.multiple_of(j * tk, tk)
        kc = kv_ref[0, pl.ds(off, tk), :]
        kct = kvt_ref[0, :, pl.ds(off, tk)]
        madd = madd_ref[j]
        for h in range(DSA_HEADS):
            qh = q_ref[0, :, h * DSA_LATENT:(h + 1) * DSA_LATENT]
            lg = lax.dot_general(kc, qh, (((1,), (1,)), ((), ())), preferred_element_type=f32)
            lg = lg * scale + bias_fn(h) + madd
            m_old = m_ref[h]
            m_new = jnp.maximum(m_old, jnp.max(lg, axis=0, keepdims=True))
            alpha = jnp.exp(m_old - m_new)
            p = jnp.exp(lg - m_new)
            l_ref[h] = alpha * l_ref[h] + jnp.sum(p, axis=0, keepdims=True)
            acc_ref[h] = alpha * acc_ref[h] + jnp.dot(kct, p.astype(kct.dtype),
                                                      preferred_element_type=f32)
            m_ref[h] = m_new

    def far_body(j, carry):
        attend(j, lambda h: bfar_ref[h])
        return carry

    lax.fori_loop(0, jnp.maximum(i - 1, 0), far_body, 0)

    @pl.when(i >= 1)
    def _():
        attend(i - 1, lambda h: b1_ref[h])

    attend(i, lambda h: b0_ref[h])

    for h in range(DSA_HEADS):
        out_t = acc_ref[h] / l_ref[h]
        o_ref[0, :, h * DSA_LATENT:(h + 1) * DSA_LATENT] = out_t.T.astype(o_ref.dtype)


def _dsa_bias_tiles(rel_bias, tq):
    dist = jnp.arange(2 * tq, dtype=jnp.int32)
    by_dist = rel_bias[_t5_bucket(dist)].T
    kk = jnp.arange(tq, dtype=jnp.int32)[:, None]
    qq = jnp.arange(tq, dtype=jnp.int32)[None, :]
    b0 = by_dist[:, jnp.maximum(qq - kk, 0)]
    b1 = by_dist[:, tq + qq - kk]
    return b0, b1, by_dist[:, -1]


def _dsa_attention(q, z_small, z_wi_t, kv_g, idx_k_g, idx_k_b, rel_bias, out_dtype):
    B, S, _ = q.shape
    tq = DSA_TQ
    topk = min(TOPK_MAX, S // 4)
    assert S % tq == 0 and tq >= REL_MAX_DIST and topk <= tq
    qi, ckv, kidx = _dsa_prep(z_small, kv_g, idx_k_g, idx_k_b)
    ckv_t = jnp.swapaxes(ckv, 1, 2)
    b0, b1, bfar = _dsa_bias_tiles(rel_bias, tq)
    nq = S // tq
    H, d = DSA_HEADS, DSA_LATENT
    grid_spec = pltpu.PrefetchScalarGridSpec(
        num_scalar_prefetch=0,
        grid=(B, nq),
        in_specs=[
            pl.BlockSpec(memory_space=pltpu.SMEM),
            pl.BlockSpec((1, tq, H * d), lambda b, i: (b, i, 0)),
            pl.BlockSpec((1, tq, IDX_HEADS * IDX_DIM), lambda b, i: (b, i, 0)),
            pl.BlockSpec((1, IDX_HEADS, tq), lambda b, i: (b, 0, i)),
            pl.BlockSpec((1, S, d), lambda b, i: (b, 0, 0)),
            pl.BlockSpec((1, d, S), lambda b, i: (b, 0, 0)),
            pl.BlockSpec((1, S, IDX_DIM), lambda b, i: (b, 0, 0)),
            pl.BlockSpec((H, tq, tq), lambda b, i: (0, 0, 0)),
            pl.BlockSpec((H, tq, tq), lambda b, i: (0, 0, 0)),
        ],
        out_specs=pl.BlockSpec((1, tq, H * d), lambda b, i: (b, i, 0)),
        scratch_shapes=[
            pltpu.VMEM((nq, tq, tq), jnp.int32),
            pltpu.VMEM((nq, tq, tq), jnp.float32),
            pltpu.VMEM((H, 1, tq), jnp.float32),
            pltpu.VMEM((H, 1, tq), jnp.float32),
            pltpu.VMEM((H, d, tq), jnp.float32),
        ],
    )
    return pl.pallas_call(
        partial(_dsa_kernel, topk=topk),
        grid_spec=grid_spec,
        out_shape=jax.ShapeDtypeStruct((B, S, H * d), out_dtype),
        compiler_params=pltpu.CompilerParams(vmem_limit_bytes=48 * 1024 * 1024),
        name="dsa_attention",
    )(bfar, q, qi, z_wi_t, ckv, ckv_t, kidx, b0, b1)


SC_CORES = 2
SC_SUBCORES = 16
SC_LANES = 16
PEER_SC_TOKENS = 8
PEER_SC_RING = 4
PEER_SC_UNROLL = 4
PEER_SLOTS = PEER_HEADS * PEER_TOPK


def _peer_sc_call(body, T, out_width, stage_width):
    mesh = plsc.VectorSubcoreMesh(core_axis_name="c", subcore_axis_name="s")
    return pl.kernel(
        body, mesh=mesh,
        out_type=jax.ShapeDtypeStruct((T, out_width), jnp.float32),
        scratch_types=[
            pltpu.VMEM((PEER_SC_TOKENS, PEER_SLOTS), jnp.int32),
            pltpu.VMEM((PEER_SC_TOKENS, stage_width), jnp.float32),
            pltpu.VMEM((PEER_SC_RING, PEER_TOPK, D_MODEL), jnp.float32),
            pltpu.VMEM((PEER_SC_TOKENS, out_width), jnp.float32),
            pltpu.SemaphoreType.DMA((PEER_SC_RING,)),
        ],
        compiler_params=pltpu.CompilerParams(needs_layout_passes=False),
    )


def _peer_sc_body(compute, zero_out, x_hbm, idx_hbm, tab_hbm, out_hbm, idx_v, x_v, rows_v, out_v, sems):
    T = idx_hbm.shape[0]
    tokens_per_worker = T // (SC_CORES * SC_SUBCORES)
    n_steps = PEER_SC_TOKENS * PEER_HEADS
    worker = lax.axis_index("s") * SC_CORES + lax.axis_index("c")
    base = worker * tokens_per_worker

    def gather(s, b):
        ids = idx_v[s // PEER_HEADS, pl.ds((s % PEER_HEADS) * PEER_TOPK, PEER_TOPK)]
        return pltpu.make_async_copy(tab_hbm.at[ids], rows_v.at[b], sems.at[b])

    @pl.loop(0, tokens_per_worker // PEER_SC_TOKENS)
    def _(blk):
        tok0 = base + blk * PEER_SC_TOKENS
        pltpu.sync_copy(idx_hbm.at[pl.ds(tok0, PEER_SC_TOKENS)], idx_v)
        pltpu.sync_copy(x_hbm.at[pl.ds(tok0, PEER_SC_TOKENS)], x_v)
        for b in range(PEER_SC_RING - 1):
            gather(b, b).start()
        if zero_out:
            @pl.loop(0, PEER_SC_TOKENS)
            def _(t):
                @plsc.parallel_loop(0, out_v.shape[1] // SC_LANES, unroll=PEER_SC_UNROLL)
                def _(c):
                    out_v[t, pl.ds(pl.multiple_of(c * SC_LANES, SC_LANES), SC_LANES)] = (
                        jnp.zeros((SC_LANES,), jnp.float32))

        @pl.loop(0, n_steps, step=PEER_SC_RING)
        def _(s0):
            for b in range(PEER_SC_RING):
                s = s0 + b
                gather(s, b).wait()

                @pl.when(s + PEER_SC_RING - 1 < n_steps)
                def _():
                    gather(s + PEER_SC_RING - 1, (b + PEER_SC_RING - 1) % PEER_SC_RING).start()

                compute(s // PEER_HEADS, s % PEER_HEADS, b, x_v, rows_v, out_v)

        pltpu.sync_copy(out_v, out_hbm.at[pl.ds(tok0, PEER_SC_TOKENS)])


def _peer_dots_compute(t, hd, b, h_v, rows_v, dots_v):
    lane = lax.iota(jnp.int32, SC_LANES)

    def col_step(c, accs):
        off = pl.multiple_of(c * SC_LANES, SC_LANES)
        hv = h_v[t, pl.ds(off, SC_LANES)]
        return tuple(accs[r] + rows_v[b, r, pl.ds(off, SC_LANES)] * hv for r in range(PEER_TOPK))

    accs = plsc.parallel_loop(
        0, D_MODEL // SC_LANES, unroll=PEER_SC_UNROLL,
        carry=tuple(jnp.zeros((SC_LANES,), jnp.float32) for _ in range(PEER_TOPK)))(col_step)
    res = jnp.zeros((SC_LANES,), jnp.float32)
    for r in range(PEER_TOPK):
        res = jnp.where(lane == r, jnp.sum(accs[r]), res)
    dots_v[t, pl.ds(hd * PEER_TOPK, PEER_TOPK)] = res


def _peer_mix_compute(t, hd, b, act_v, rows_v, out_v):
    tvec = jnp.full((SC_LANES,), t, jnp.int32)
    weights = [plsc.load_gather(act_v, [tvec, jnp.full((SC_LANES,), hd * PEER_TOPK + r, jnp.int32)])
               for r in range(PEER_TOPK)]

    @plsc.parallel_loop(0, D_MODEL // SC_LANES, unroll=PEER_SC_UNROLL)
    def _(c):
        off = pl.multiple_of(c * SC_LANES, SC_LANES)
        a = rows_v[b, 0, pl.ds(off, SC_LANES)] * weights[0]
        for r in range(1, PEER_TOPK):
            a = a + rows_v[b, r, pl.ds(off, SC_LANES)] * weights[r]
        plsc.addupdate(out_v.at[t, pl.ds(off, SC_LANES)], a)


def _peer_expert_dots(h, experts, u_tab):
    T = h.shape[0]
    assert T % (SC_CORES * SC_SUBCORES * PEER_SC_TOKENS) == 0
    assert (PEER_SC_TOKENS * PEER_HEADS) % PEER_SC_RING == 0 and PEER_TOPK == SC_LANES
    body = partial(_peer_sc_body, _peer_dots_compute, False)
    return _peer_sc_call(body, T, PEER_SLOTS, D_MODEL)(h, experts, u_tab)


def _peer_expert_mix(act, experts, v_tab):
    T = act.shape[0]
    assert T % (SC_CORES * SC_SUBCORES * PEER_SC_TOKENS) == 0
    body = partial(_peer_sc_body, _peer_mix_compute, True)
    return _peer_sc_call(body, T, D_MODEL, PEER_SLOTS)(act, experts, v_tab)


PEER_TM = 256


def _extract_topk(s, k, payload=None):
    R = s.shape[0]
    riota = lax.broadcasted_iota(jnp.int32, s.shape, 0)
    vals, rows = [], []
    for _ in range(k):
        m = jnp.max(s, axis=0, keepdims=True)
        pos = jnp.min(jnp.where(s == m, riota, R), axis=0, keepdims=True)
        hit = riota == pos
        vals.append(m)
        rows.append(pos if payload is None else jnp.max(jnp.where(hit, payload, -1), axis=0, keepdims=True))
        s = jnp.where(hit, -jnp.inf, s)
    return jnp.concatenate(vals, axis=0), jnp.concatenate(rows, axis=0)


def _peer_route_kernel(x_ref, sc_ref, sh_ref, wq_ref, keys_ref, h_ref, ex_ref, gate_ref, q_ref, ext_ref, gt_ref):
    K = PEER_TOPK
    f32 = jnp.float32
    h = x_ref[0] * (1.0 + sc_ref[0]) + sh_ref[0]
    h_ref[0] = h
    q = jnp.dot(h.astype(wq_ref.dtype), wq_ref[...], preferred_element_type=f32)
    for hd in range(PEER_HEADS):
        q_ref[hd] = q[:, hd * PEER_KEY_DIM:(hd + 1) * PEER_KEY_DIM].astype(q_ref.dtype)

    def head_body(hd, carry):
        qh = q_ref[hd]
        s1 = lax.dot_general(keys_ref[hd, 0], qh[:, :PEER_HALF], _NT, preferred_element_type=f32)
        s2 = lax.dot_general(keys_ref[hd, 1], qh[:, PEER_HALF:], _NT, preferred_element_type=f32)
        v1, i1 = _extract_topk(s1, K)
        v2, i2 = _extract_topk(s2, K)
        tm = v1.shape[1]
        cand_rows, cidx_rows = [], []
        for a in range(K):
            nb = K // (a + 1)
            cand_rows.append(v1[a:a + 1, :] + v2[:nb, :])
            cidx_rows.append(i1[a:a + 1, :] * PEER_N_KEYS + i2[:nb, :])
        n_pad = -sum(r.shape[0] for r in cand_rows) % 8
        cand_rows.append(jnp.full((n_pad, tm), -jnp.inf, f32))
        cidx_rows.append(jnp.full((n_pad, tm), -1, jnp.int32))
        cand = jnp.concatenate(cand_rows, axis=0)
        cidx = jnp.concatenate(cidx_rows, axis=0)
        top_s, experts = _extract_topk(cand, K, payload=cidx)
        e = jnp.exp(top_s - top_s[0:1, :])
        gt_ref[pl.ds(pl.multiple_of(hd * K, K), K), :] = e / jnp.sum(e, axis=0, keepdims=True)
        ext_ref[pl.ds(pl.multiple_of(hd * K, K), K), :] = experts
        return carry

    lax.fori_loop(0, PEER_HEADS, head_body, 0)
    ex_ref[0] = ext_ref[...].T
    gate_ref[0] = gt_ref[...].T


def _peer_route(x, sc, sh, w_pq, sub_keys):
    B, S, D = x.shape
    tm = PEER_TM
    return pl.pallas_call(
        _peer_route_kernel,
        grid=(B, S // tm),
        in_specs=[
            pl.BlockSpec((1, tm, D), lambda b, i: (b, i, 0)),
            pl.BlockSpec((1, 1, D), lambda b, i: (b, 0, 0)),
            pl.BlockSpec((1, 1, D), lambda b, i: (b, 0, 0)),
            pl.BlockSpec((D, PEER_HEADS * PEER_KEY_DIM), lambda b, i: (0, 0)),
            pl.BlockSpec((PEER_HEADS, 2, PEER_N_KEYS, PEER_HALF), lambda b, i: (0, 0, 0, 0)),
        ],
        out_specs=[
            pl.BlockSpec((1, tm, D), lambda b, i: (b, i, 0)),
            pl.BlockSpec((1, tm, PEER_SLOTS), lambda b, i: (b, i, 0)),
            pl.BlockSpec((1, tm, PEER_SLOTS), lambda b, i: (b, i, 0)),
        ],
        out_shape=[
            jax.ShapeDtypeStruct((B, S, D), jnp.float32),
            jax.ShapeDtypeStruct((B, S, PEER_SLOTS), jnp.int32),
            jax.ShapeDtypeStruct((B, S, PEER_SLOTS), jnp.float32),
        ],
        scratch_shapes=[
            pltpu.VMEM((PEER_HEADS, tm, PEER_KEY_DIM), MM_DTYPE),
            pltpu.VMEM((PEER_SLOTS, tm), jnp.int32),
            pltpu.VMEM((PEER_SLOTS, tm), jnp.float32),
        ],
        compiler_params=pltpu.CompilerParams(dimension_semantics=("parallel", "parallel")),
        name="peer_route",
    )(x, sc[:, None, :], sh[:, None, :], w_pq.astype(MM_DTYPE), sub_keys.astype(MM_DTYPE))


def _peer_pre(h2, experts, u_tab):
    B, S, D = h2.shape
    return _peer_expert_dots(h2.reshape(B * S, D), experts.reshape(B * S, PEER_SLOTS), u_tab)


def _peer_out(pre, experts, gates, v_tab):
    B, S, _ = experts.shape
    act = jax.nn.gelu(pre, approximate=False) * gates.reshape(B * S, PEER_SLOTS)
    return _peer_expert_mix(act, experts.reshape(B * S, PEER_SLOTS), v_tab).reshape(B, S, D_MODEL)


def kernel(x, c, w_ada, b_ada, w_in, rw_mu, rw_w0, rw_w2, rw_a0, rw_a2, rw_g2, rw_k_k, rw_k_a, rw_r_k, rw_gn_g, rw_gn_b, dsa_kv_g, idx_k_g, idx_k_b, rel_bias, w_br_a, w_br_b, w_out, ln1_g, ln1_b, peer_wq, peer_keys, peer_u, peer_v, ln2_g, ln2_b):
    l = 0
    mod = jax.nn.silu(c) @ w_ada[l] + b_ada[l]

    w_rw, w_q, w_kv, w_qi, w_ki, w_wi, w_ga, w_gb = jnp.split(w_in[l], _split_points(IN_SIZES), axis=-1)
    small_pad = jnp.zeros((D_MODEL, 512 - 256 - 128 - 64 - 4), w_in.dtype)
    w_small = jnp.concatenate([w_qi, w_kv, w_ki, w_wi, small_pad], axis=-1)
    w_gates = jnp.concatenate([w_ga, w_gb], axis=-1)

    def project_and_dsa(x, sc1, sh1):
        z_rw = _mod_matmul(x, sc1, sh1, w_rw, jnp.float32, tn=896)
        z_q = _mod_matmul(x, sc1, sh1, w_q, MM_DTYPE, tn=1024)
        z_small = _mod_matmul(x, sc1, sh1, w_small, jnp.float32, tn=512)
        z_g = _mod_matmul(x, sc1, sh1, w_gates, jnp.float32, tn=1024)
        z_wi_t = jnp.swapaxes(z_small[..., 448:448 + IDX_HEADS], 1, 2)
        y_b = _dsa_attention(z_q, z_small, z_wi_t, dsa_kv_g[l], idx_k_g[l], idx_k_b[l],
                             rel_bias, jnp.float32) @ w_br_b[l]
        return y_b, z_rw, z_g

    def rwkv_and_merge(x, gt1, y_b, z_rw, z_g):
        z_ga, z_gb = z_g[..., :D_MODEL], z_g[..., D_MODEL:]
        y_a = _rwkv7_time_mix(z_rw, rw_mu[l], rw_w0[l], rw_w2[l], rw_a0[l], rw_a2[l], rw_g2[l],
                              rw_k_k[l], rw_k_a[l], rw_r_k[l], rw_gn_g[l], rw_gn_b[l], jnp.float32) @ w_br_a[l]
        merged = jax.nn.sigmoid(z_ga) * y_a + jax.nn.sigmoid(z_gb) * y_b
        mix = merged @ w_out[l]
        return _layer_norm(DEEPNORM_ALPHA * x + gt1[:, None] * mix, ln1_g[l], ln1_b[l])

    def finish(st):
        y2 = _peer_out(st["pre"], st["experts"], st["gates"], peer_v[l])
        return _layer_norm(DEEPNORM_ALPHA * st["x1"] + st["gt2"][:, None] * y2, ln2_g[l], ln2_b[l])

    gsz = x.shape[0] // BATCH_GROUPS
    outs = []
    prev = None
    for g in range(BATCH_GROUPS):
        sh1, sc1, gt1, sh2, sc2, gt2 = jnp.split(mod[g * gsz:(g + 1) * gsz], 6, axis=-1)
        x_g = x[g * gsz:(g + 1) * gsz]
        if prev is not None:
            x_g, prev["experts"] = lax.optimization_barrier((x_g, prev["experts"]))
        proj = project_and_dsa(x_g, sc1, sh1)
        if prev is not None:
            proj, prev["pre"] = lax.optimization_barrier((proj, prev["pre"]))
            done = finish(prev)
        x1 = rwkv_and_merge(x_g, gt1, *proj)
        if prev is not None:
            x1, done = lax.optimization_barrier((x1, done))
            outs.append(done)
        h2, experts, gates = _peer_route(x1, sc2, sh2, peer_wq[l], peer_keys[l])
        prev = dict(x1=x1, gt2=gt2, experts=experts, gates=gates, pre=_peer_pre(h2, experts, peer_u[l]))
    outs.append(finish(prev))
    return jnp.concatenate(outs, axis=0)
```

```python
import math
from functools import partial

import jax
import jax.numpy as jnp
import numpy as np
from jax import lax
from jax.experimental import pallas as pl
from jax.experimental.pallas import tpu as pltpu
from jax.experimental.pallas import tpu_sc as plsc

D_MODEL = 1024
RW_HEADS = 8
RW_HEAD_DIM = 64
RW_DIM = 512
RW_DECAY_LORA = 64
RW_A_LORA = 64
RW_GATE_LORA = 128
RW_SIZES = (RW_DIM, RW_DIM, RW_DIM, RW_DECAY_LORA, RW_A_LORA, RW_GATE_LORA)
RW_COLS = 3 * RW_DIM + RW_DECAY_LORA + RW_A_LORA + RW_GATE_LORA
RW_GN_EPS = 64e-5
DSA_HEADS = 8
DSA_LATENT = 128
DSA_Q_DIM = DSA_HEADS * DSA_LATENT
IDX_HEADS = 4
IDX_DIM = 64
TOPK_MAX = 256
Q_BLOCK = 128
REL_BUCKETS = 32
REL_MAX_DIST = 128
IN_SIZES = (RW_COLS, DSA_Q_DIM, DSA_LATENT, IDX_HEADS * IDX_DIM, IDX_DIM, IDX_HEADS, D_MODEL, D_MODEL)
IN_COLS = sum(IN_SIZES)
PEER_HEADS = 8
PEER_N_KEYS = 128
PEER_KEY_DIM = 128
PEER_HALF = 64
PEER_TOPK = 16
PEER_CHUNK = 128
LN_EPS = 1e-5
DEPTH = 1
DEEPNORM_ALPHA = (2.0 * DEPTH) ** 0.25

LANES = 128
MM_DTYPE = jnp.bfloat16
BATCH_GROUPS = 8


def _split_points(sizes):
    return np.cumsum(sizes)[:-1].tolist()


def _mod_matmul_kernel(x_ref, sc_ref, sh_ref, w_ref, o_ref):
    h = x_ref[0] * (1.0 + sc_ref[0]) + sh_ref[0]
    o_ref[0] = jnp.dot(h.astype(w_ref.dtype), w_ref[...],
                       preferred_element_type=jnp.float32).astype(o_ref.dtype)


def _mod_matmul(x, sc, sh, w, out_dtype, tn, tm=512):
    B, S, D = x.shape
    N = w.shape[1]
    return pl.pallas_call(
        _mod_matmul_kernel,
        grid=(B, S // tm, N // tn),
        in_specs=[
            pl.BlockSpec((1, tm, D), lambda b, i, j: (b, i, 0)),
            pl.BlockSpec((1, 1, D), lambda b, i, j: (b, 0, 0)),
            pl.BlockSpec((1, 1, D), lambda b, i, j: (b, 0, 0)),
            pl.BlockSpec((D, tn), lambda b, i, j: (0, j)),
        ],
        out_specs=pl.BlockSpec((1, tm, tn), lambda b, i, j: (b, i, j)),
        out_shape=jax.ShapeDtypeStruct((B, S, N), out_dtype),
        name="mod_matmul",
    )(x, sc[:, None, :], sh[:, None, :], w.astype(MM_DTYPE))


def _layer_norm(x, g, b):
    mu = jnp.mean(x, -1, keepdims=True)
    var = jnp.mean(jnp.square(x - mu), -1, keepdims=True)
    return (x - mu) * lax.rsqrt(var + LN_EPS) * g + b


def _t5_bucket(n):
    n = jnp.maximum(n, 0)
    max_exact = REL_BUCKETS // 2
    nf = jnp.maximum(n, 1).astype(jnp.float32)
    large = max_exact + (jnp.log(nf / max_exact) / math.log(REL_MAX_DIST / max_exact)
                         * (REL_BUCKETS - max_exact)).astype(jnp.int32)
    large = jnp.minimum(large, REL_BUCKETS - 1)
    return jnp.where(n < max_exact, n, large)


RW_CHUNK = 64
RW_INV_BLOCK = 16
_NN = (((1,), (0,)), ((), ()))
_NT = (((1,), (1,)), ((), ()))
_BNN = (((2,), (1,)), ((0,), (0,)))
_BNT = (((2,), (2,)), ((0,), (0,)))


def _dot_f32(a, b, dims=_NN):
    return lax.dot_general(a, b, dims, precision=lax.Precision.HIGHEST,
                           preferred_element_type=jnp.float32)


def _dot_bf16x3(a, b, dims=_NN):
    f32, bf = jnp.float32, jnp.bfloat16
    a_hi, b_hi = a.astype(bf), b.astype(bf)
    a_lo = (a - a_hi.astype(f32)).astype(bf)
    b_lo = (b - b_hi.astype(f32)).astype(bf)
    out = lax.dot_general(a_hi, b_hi, dims, preferred_element_type=f32)
    out = out + lax.dot_general(a_hi, b_lo, dims, preferred_element_type=f32)
    return out + lax.dot_general(a_lo, b_hi, dims, preferred_element_type=f32)


def _bf16_terms(a):
    f32, bf = jnp.float32, jnp.bfloat16
    hi = a.astype(bf)
    r1 = a - hi.astype(f32)
    mid = r1.astype(bf)
    lo = (r1 - mid.astype(f32)).astype(bf)
    return hi, mid, lo


def _dot_lhs_split(a, b01):
    b = b01.astype(jnp.bfloat16)
    return sum(jnp.dot(t, b, preferred_element_type=jnp.float32) for t in _bf16_terms(a))


def _dot_rhs_split(a01, b):
    a = a01.astype(jnp.bfloat16)
    return sum(jnp.dot(a, t, preferred_element_type=jnp.float32) for t in _bf16_terms(b))


def _rwkv_kernel(z_ref, mu_ref, w0_ref, w2_ref, a0_ref, a2_ref, g2_ref, kk_ref, ka_ref, rk_ref,
                 gng_ref, gnb_ref, bd_ref, o_ref, m_ref, prev_ref, y_ref):
    C = z_ref.shape[1]
    N = RW_HEAD_DIM
    f32 = jnp.float32
    dot3 = _dot_bf16x3

    @pl.when(pl.program_id(1) == 0)
    def _():
        m_ref[...] = jnp.zeros(m_ref.shape, f32)
        prev_ref[...] = jnp.zeros(prev_ref.shape, f32)

    z = z_ref[0]
    row = lax.broadcasted_iota(jnp.int32, z.shape, 0)
    shifted = jnp.where(row == 0, prev_ref[...], pltpu.roll(z, 1, axis=0))
    prev_ref[...] = z[C - 1:C, :]
    zs = z + (shifted - z) * mu_ref[...]
    r = zs[:, 0:RW_DIM]
    k = zs[:, RW_DIM:2 * RW_DIM]
    v = zs[:, 2 * RW_DIM:3 * RW_DIM]
    o1 = 3 * RW_DIM
    wl = zs[:, o1:o1 + RW_DECAY_LORA]
    al = zs[:, o1 + RW_DECAY_LORA:o1 + RW_DECAY_LORA + RW_A_LORA]
    gl = zs[:, o1 + RW_DECAY_LORA + RW_A_LORA:]

    bd = bd_ref[...]
    log_w = -jax.nn.softplus(-(w0_ref[...] + dot3(jnp.tanh(wl), w2_ref[...]))) - 0.5
    ldec = -jnp.exp(log_w)
    a_lr = jax.nn.sigmoid(a0_ref[...] + dot3(al, a2_ref[...]))
    g = dot3(jax.nn.sigmoid(gl), g2_ref[...])
    kk = k * kk_ref[...]
    kk = kk * lax.rsqrt(jnp.maximum(_dot_lhs_split(kk * kk, bd), 1e-24))
    k2 = k * (1.0 + (a_lr - 1.0) * ka_ref[...])
    a_vec = -kk
    b_vec = kk * a_lr

    ti = lax.broadcasted_iota(jnp.int32, (C, C), 0)
    tj = lax.broadcasted_iota(jnp.int32, (C, C), 1)
    cum = _dot_rhs_split((ti >= tj).astype(f32), ldec)
    cum_last = cum[C - 1:C, :]
    w_incl = jnp.exp(cum)
    w_excl = jnp.exp(cum - ldec)
    w_inv = jnp.exp(-cum)
    w_end = jnp.exp(cum_last - cum)
    w_all = jnp.exp(cum_last)
    a_t = a_vec * w_excl
    r_t = r * w_incl
    b_t = b_vec * w_inv
    k_t = k2 * w_inv
    b_e = b_vec * w_end
    k_e = k2 * w_end

    strict = ti > tj
    incl = ti >= tj
    bi, bj = ti // RW_INV_BLOCK, tj // RW_INV_BLOCK
    same_blk = bi == bj
    pair_blk = jnp.logical_and((bi // 2) == (bj // 2), jnp.logical_not(same_blk))
    half_blk = (bi // 2) != (bj // 2)
    eye = (ti == tj).astype(f32)

    H = RW_HEADS
    heads = lambda x: jnp.stack([x[:, h * N:(h + 1) * N] for h in range(H)], axis=0)
    bmm = lambda a, b: dot3(a, b, _BNN)
    A, Rt, Bt, Kt, Be, Ke, V = (heads(t) for t in (a_t, r_t, b_t, k_t, b_e, k_e, v))
    gm = dot3(jnp.concatenate([A, Rt], axis=1), jnp.concatenate([Bt, Kt], axis=1), _BNT)
    a_ab = jnp.where(strict, gm[:, :C, :C], 0.0)
    a_ak = jnp.where(strict, gm[:, :C, C:], 0.0)
    a_rb = jnp.where(incl, gm[:, C:, :C], 0.0)
    a_rk = jnp.where(incl, gm[:, C:, C:], 0.0)
    d1 = jnp.where(same_blk, a_ab, 0.0)
    xinv = eye + d1
    d2 = bmm(d1, d1)
    xinv = xinv + bmm(xinv, d2)
    d4 = bmm(d2, d2)
    xinv = xinv + bmm(xinv, d4)
    d8 = bmm(d4, d4)
    xinv = xinv + bmm(xinv, d8)
    xinv = xinv + bmm(bmm(xinv, jnp.where(pair_blk, a_ab, 0.0)), xinv)
    xinv = xinv + bmm(bmm(xinv, jnp.where(half_blk, a_ab, 0.0)), xinv)

    av = bmm(jnp.concatenate([a_ak, a_rk], axis=1), V)
    p = bmm(xinv, jnp.concatenate([A, av[:, :C]], axis=2))
    qm = bmm(a_rb, p)
    q1 = Rt + qm[:, :, :N]
    q2 = qm[:, :, N:] + av[:, C:]
    gmat = bmm(jnp.swapaxes(Be, 1, 2), p)
    g1 = eye * heads(w_all) + gmat[:, :, :N]
    g2 = gmat[:, :, N:] + bmm(jnp.swapaxes(Ke, 1, 2), V)
    m = m_ref[...]
    yh = _dot_f32(q1, m, _BNN) + q2
    m_ref[...] = _dot_f32(g1, m, _BNN) + g2
    for h in range(H):
        y_ref[:, h * N:(h + 1) * N] = yh[h]

    y = y_ref[...]
    mean = _dot_lhs_split(y, bd) * (1.0 / N)
    yc = y - mean
    var = _dot_lhs_split(yc * yc, bd) * (1.0 / N)
    yn = yc * lax.rsqrt(var + RW_GN_EPS) * gng_ref[...] + gnb_ref[...]
    bonus = _dot_lhs_split(r * k2 * rk_ref[...], bd) * v
    o_ref[0] = ((yn + bonus) * g).astype(o_ref.dtype)


def _rwkv7_time_mix(z_rw, mu, w0, w2, a0, a2, g2, k_k, k_a, r_k, gn_g, gn_b, out_dtype):
    B, S, _ = z_rw.shape
    C = RW_CHUNK
    assert S % C == 0 and C == RW_HEAD_DIM and C % (4 * RW_INV_BLOCK) == 0
    hid = jnp.arange(RW_DIM) // RW_HEAD_DIM
    bd = (hid[:, None] == hid[None, :]).astype(jnp.float32)
    row = lambda a: a.reshape(1, -1)
    full = lambda shape: pl.BlockSpec(shape, lambda b, c: (0,) * len(shape))
    return pl.pallas_call(
        _rwkv_kernel,
        grid=(B, S // C),
        in_specs=[
            pl.BlockSpec((1, C, RW_COLS), lambda b, c: (b, c, 0)),
            full((1, RW_COLS)), full((1, RW_DIM)), full((RW_DECAY_LORA, RW_DIM)), full((1, RW_DIM)),
            full((RW_A_LORA, RW_DIM)), full((RW_GATE_LORA, RW_DIM)), full((1, RW_DIM)), full((1, RW_DIM)),
            full((1, RW_DIM)), full((1, RW_DIM)), full((1, RW_DIM)), full((RW_DIM, RW_DIM)),
        ],
        out_specs=pl.BlockSpec((1, C, RW_DIM), lambda b, c: (b, c, 0)),
        out_shape=jax.ShapeDtypeStruct((B, S, RW_DIM), out_dtype),
        scratch_shapes=[
            pltpu.VMEM((RW_HEADS, RW_HEAD_DIM, RW_HEAD_DIM), jnp.float32),
            pltpu.VMEM((1, RW_COLS), jnp.float32),
            pltpu.VMEM((C, RW_DIM), jnp.float32),
        ],
        compiler_params=pltpu.CompilerParams(dimension_semantics=("parallel", "arbitrary")),
        name="rwkv7_time_mix",
    )(z_rw, row(mu), row(w0), w2, row(a0), a2, g2, row(k_k), row(k_a), row(r_k), row(gn_g), row(gn_b), bd)


DSA_TQ = 256
MASK_NEG = -1e30
INT_MIN = -2 ** 31
KEY_NEG_INF = -2139095041
THRESH_BITS = 32


def _dsa_prep_kernel(z_ref, kvg_ref, kig_ref, kib_ref, qi_ref, kv_ref, ki_ref):
    z = z_ref[0]
    qi_ref[0] = z[:, :IDX_HEADS * IDX_DIM].astype(qi_ref.dtype)
    kv = z[:, 256:384]
    ms = jnp.mean(jnp.square(kv), -1, keepdims=True)
    kv_ref[0] = (kv * lax.rsqrt(ms + LN_EPS) * kvg_ref[...]).astype(kv_ref.dtype)
    ki = z[:, 384:448]
    mu = jnp.mean(ki, -1, keepdims=True)
    var = jnp.mean(jnp.square(ki - mu), -1, keepdims=True)
    ki_ref[0] = ((ki - mu) * lax.rsqrt(var + LN_EPS) * kig_ref[...] + kib_ref[...]).astype(ki_ref.dtype)


def _dsa_prep(z_small, kv_g, ki_g, ki_b, tm=512):
    B, S, W = z_small.shape
    return pl.pallas_call(
        _dsa_prep_kernel,
        grid=(B, S // tm),
        in_specs=[
            pl.BlockSpec((1, tm, W), lambda b, i: (b, i, 0)),
            pl.BlockSpec((1, DSA_LATENT), lambda b, i: (0, 0)),
            pl.BlockSpec((1, IDX_DIM), lambda b, i: (0, 0)),
            pl.BlockSpec((1, IDX_DIM), lambda b, i: (0, 0)),
        ],
        out_specs=[
            pl.BlockSpec((1, tm, IDX_HEADS * IDX_DIM), lambda b, i: (b, i, 0)),
            pl.BlockSpec((1, tm, DSA_LATENT), lambda b, i: (b, i, 0)),
            pl.BlockSpec((1, tm, IDX_DIM), lambda b, i: (b, i, 0)),
        ],
        out_shape=[
            jax.ShapeDtypeStruct((B, S, IDX_HEADS * IDX_DIM), MM_DTYPE),
            jax.ShapeDtypeStruct((B, S, DSA_LATENT), MM_DTYPE),
            jax.ShapeDtypeStruct((B, S, IDX_DIM), MM_DTYPE),
        ],
        name="dsa_prep",
    )(z_small, kv_g[None], ki_g[None], ki_b[None])


def _sortable_key(s):
    s = jnp.where(s == 0.0, 0.0, s)
    bits = pltpu.bitcast(s, jnp.int32)
    return bits ^ ((bits >> 31) & 0x7FFFFFFF)


def _col_count(mask_i32):
    tk, tq = mask_i32.shape
    return jnp.sum(mask_i32.reshape(tk // 8, 8, tq), axis=0)


def _dsa_kernel(bfar_ref, q_ref, qi_ref, wit_ref, kv_ref, kvt_ref, ki_ref, b0_ref, b1_ref, o_ref,
                key_ref, madd_ref, m_ref, l_ref, acc_ref, *, topk):
    tq = q_ref.shape[1]
    tk = tq
    i = pl.program_id(1)
    nj = i + 1
    f32 = jnp.float32
    krow = lax.broadcasted_iota(jnp.int32, (tk, tq), 0)
    qcol = lax.broadcasted_iota(jnp.int32, (tk, tq), 1)

    qi = qi_ref[0]
    wit = wit_ref[0] * (IDX_HEADS ** -0.5)

    def score_chunk(j, carry):
        off = pl.multiple_of(j * tk, tk)
        kc = ki_ref[0, pl.ds(off, tk), :]
        s = jnp.zeros((tk, tq), f32)
        for h in range(IDX_HEADS):
            d = lax.dot_general(kc, qi[:, h * IDX_DIM:(h + 1) * IDX_DIM],
                                (((1,), (1,)), ((), ())), preferred_element_type=f32)
            s = s + wit[h:h + 1, :] * jnp.maximum(d * (IDX_DIM ** -0.5), 0.0)
        causal = (krow + j * tk) <= (qcol + i * tq)
        s = jnp.where(causal, s, -jnp.inf)
        key_ref[j] = _sortable_key(s)
        return carry

    lax.fori_loop(0, nj, score_chunk, 0)

    def count_where(pred_fn):
        def body(j, acc):
            return acc + _col_count(pred_fn(key_ref[j], j).astype(jnp.int32))
        acc = lax.fori_loop(0, nj, body, jnp.zeros((8, tq), jnp.int32))
        return jnp.sum(acc, axis=0, keepdims=True)

    def bit_step(it, t_u):
        bit = THRESH_BITS - 1 - it
        cand_u = t_u | jnp.left_shift(jnp.int32(1), bit)
        cand = cand_u ^ INT_MIN
        cnt = count_where(lambda k, j: k >= cand)
        return jnp.where(cnt >= topk, cand_u, t_u)

    t_u = lax.fori_loop(0, THRESH_BITS, bit_step, jnp.zeros((1, tq), jnp.int32))
    thr = t_u ^ INT_MIN
    cnt_gt = count_where(lambda k, j: k > thr)
    cnt_ge = count_where(lambda k, j: k >= thr)
    is_neg = thr == KEY_NEG_INF
    need = jnp.logical_and(cnt_ge > topk, jnp.logical_not(is_neg))
    n_tie_take = topk - cnt_gt

    thr_open = jnp.where(is_neg, thr, thr - 1)
    any_need = jnp.max(need.astype(jnp.int32)) > 0

    @pl.when(jnp.logical_not(any_need))
    def _():
        def body(j, carry):
            madd_ref[j] = jnp.where(key_ref[j] > thr_open, 0.0, MASK_NEG)
            return carry
        lax.fori_loop(0, nj, body, 0)

    @pl.when(any_need)
    def _():
        s_len = tk * key_ref.shape[0]
        n_bits = max(1, int(math.ceil(math.log2(s_len))))

        def idx_step(it, p):
            bit = n_bits - 1 - it
            cand = p | jnp.left_shift(jnp.int32(1), bit)
            cnt = count_where(
                lambda k, j: jnp.where(k == thr, jnp.where((krow + j * tk) < cand, 1, 0), 0))
            return jnp.where(cnt < n_tie_take, cand, p)

        p_idx = lax.fori_loop(0, n_bits, idx_step, jnp.zeros((1, tq), jnp.int32))
        p_idx = jnp.where(need, p_idx, jnp.where(is_neg, -1, s_len))

        def body(j, carry):
            k = key_ref[j]
            tie_ok = jnp.where((krow + j * tk) <= p_idx, 0.0, MASK_NEG)
            madd_ref[j] = jnp.where(k > thr, 0.0, jnp.where(k == thr, tie_ok, MASK_NEG))
            return carry
        lax.fori_loop(0, nj, body, 0)

    m_ref[...] = jnp.full(m_ref.shape, MASK_NEG, f32)
    l_ref[...] = jnp.zeros(l_ref.shape, f32)
    acc_ref[...] = jnp.zeros(acc_ref.shape, f32)
    scale = DSA_LATENT ** -0.5

    def attend(j, bias_fn):
        off = pl.multiple_of(j * tk, tk)
        kc = kv_ref[0, pl.ds(off, tk), :]
        kct = kvt_ref[0, :, pl.ds(off, tk)]
        madd = madd_ref[j]
        for h in range(DSA_HEADS):
            qh = q_ref[0, :, h * DSA_LATENT:(h + 1) * DSA_LATENT]
            lg = lax.dot_general(kc, qh, (((1,), (1,)), ((), ())), preferred_element_type=f32)
            lg = lg * scale + bias_fn(h) + madd
            m_old = m_ref[h]
            m_new = jnp.maximum(m_old, jnp.max(lg, axis=0, keepdims=True))
            alpha = jnp.exp(m_old - m_new)
            p = jnp.exp(lg - m_new)
            l_ref[h] = alpha * l_ref[h] + jnp.sum(p, axis=0, keepdims=True)
            acc_ref[h] = alpha * acc_ref[h] + jnp.dot(kct, p.astype(kct.dtype),
                                                      preferred_element_type=f32)
            m_ref[h] = m_new

    def far_body(j, carry):
        attend(j, lambda h: bfar_ref[h])
        return carry

    lax.fori_loop(0, jnp.maximum(i - 1, 0), far_body, 0)

    @pl.when(i >= 1)
    def _():
        attend(i - 1, lambda h: b1_ref[h])

    attend(i, lambda h: b0_ref[h])

    for h in range(DSA_HEADS):
        out_t = acc_ref[h] / l_ref[h]
        o_ref[0, :, h * DSA_LATENT:(h + 1) * DSA_LATENT] = out_t.T.astype(o_ref.dtype)


def _dsa_bias_tiles(rel_bias, tq):
    dist = jnp.arange(2 * tq, dtype=jnp.int32)
    by_dist = rel_bias[_t5_bucket(dist)].T
    kk = jnp.arange(tq, dtype=jnp.int32)[:, None]
    qq = jnp.arange(tq, dtype=jnp.int32)[None, :]
    b0 = by_dist[:, jnp.maximum(qq - kk, 0)]
    b1 = by_dist[:, tq + qq - kk]
    return b0, b1, by_dist[:, -1]


def _dsa_attention(q, z_small, z_wi_t, kv_g, idx_k_g, idx_k_b, rel_bias, out_dtype):
    B, S, _ = q.shape
    tq = DSA_TQ
    topk = min(TOPK_MAX, S // 4)
    assert S % tq == 0 and tq >= REL_MAX_DIST and topk <= tq
    qi, ckv, kidx = _dsa_prep(z_small, kv_g, idx_k_g, idx_k_b)
    ckv_t = jnp.swapaxes(ckv, 1, 2)
    b0, b1, bfar = _dsa_bias_tiles(rel_bias, tq)
    nq = S // tq
    H, d = DSA_HEADS, DSA_LATENT
    grid_spec = pltpu.PrefetchScalarGridSpec(
        num_scalar_prefetch=0,
        grid=(B, nq),
        in_specs=[
            pl.BlockSpec(memory_space=pltpu.SMEM),
            pl.BlockSpec((1, tq, H * d), lambda b, i: (b, i, 0)),
            pl.BlockSpec((1, tq, IDX_HEADS * IDX_DIM), lambda b, i: (b, i, 0)),
            pl.BlockSpec((1, IDX_HEADS, tq), lambda b, i: (b, 0, i)),
            pl.BlockSpec((1, S, d), lambda b, i: (b, 0, 0)),
            pl.BlockSpec((1, d, S), lambda b, i: (b, 0, 0)),
            pl.BlockSpec((1, S, IDX_DIM), lambda b, i: (b, 0, 0)),
            pl.BlockSpec((H, tq, tq), lambda b, i: (0, 0, 0)),
            pl.BlockSpec((H, tq, tq), lambda b, i: (0, 0, 0)),
        ],
        out_specs=pl.BlockSpec((1, tq, H * d), lambda b, i: (b, i, 0)),
        scratch_shapes=[
            pltpu.VMEM((nq, tq, tq), jnp.int32),
            pltpu.VMEM((nq, tq, tq), jnp.float32),
            pltpu.VMEM((H, 1, tq), jnp.float32),
            pltpu.VMEM((H, 1, tq), jnp.float32),
            pltpu.VMEM((H, d, tq), jnp.float32),
        ],
    )
    return pl.pallas_call(
        partial(_dsa_kernel, topk=topk),
        grid_spec=grid_spec,
        out_shape=jax.ShapeDtypeStruct((B, S, H * d), out_dtype),
        compiler_params=pltpu.CompilerParams(vmem_limit_bytes=48 * 1024 * 1024),
        name="dsa_attention",
    )(bfar, q, qi, z_wi_t, ckv, ckv_t, kidx, b0, b1)


SC_CORES = 2
SC_SUBCORES = 16
SC_LANES = 16
PEER_SC_TOKENS = 8
PEER_SC_RING = 4
PEER_SC_UNROLL = 4
PEER_SLOTS = PEER_HEADS * PEER_TOPK


def _peer_sc_call(body, T, out_width, stage_width):
    mesh = plsc.VectorSubcoreMesh(core_axis_name="c", subcore_axis_name="s")
    return pl.kernel(
        body, mesh=mesh,
        out_type=jax.ShapeDtypeStruct((T, out_width), jnp.float32),
        scratch_types=[
            pltpu.VMEM((PEER_SC_TOKENS, PEER_SLOTS), jnp.int32),
            pltpu.VMEM((PEER_SC_TOKENS, stage_width), jnp.float32),
            pltpu.VMEM((PEER_SC_RING, PEER_TOPK, D_MODEL), jnp.float32),
            pltpu.VMEM((PEER_SC_TOKENS, out_width), jnp.float32),
            pltpu.SemaphoreType.DMA((PEER_SC_RING,)),
        ],
        compiler_params=pltpu.CompilerParams(needs_layout_passes=False),
    )


def _peer_sc_body(compute, zero_out, x_hbm, idx_hbm, tab_hbm, out_hbm, idx_v, x_v, rows_v, out_v, sems):
    T = idx_hbm.shape[0]
    tokens_per_worker = T // (SC_CORES * SC_SUBCORES)
    n_steps = PEER_SC_TOKENS * PEER_HEADS
    worker = lax.axis_index("s") * SC_CORES + lax.axis_index("c")
    base = worker * tokens_per_worker

    def gather(s, b):
        ids = idx_v[s // PEER_HEADS, pl.ds((s % PEER_HEADS) * PEER_TOPK, PEER_TOPK)]
        return pltpu.make_async_copy(tab_hbm.at[ids], rows_v.at[b], sems.at[b])

    @pl.loop(0, tokens_per_worker // PEER_SC_TOKENS)
    def _(blk):
        tok0 = base + blk * PEER_SC_TOKENS
        pltpu.sync_copy(idx_hbm.at[pl.ds(tok0, PEER_SC_TOKENS)], idx_v)
        pltpu.sync_copy(x_hbm.at[pl.ds(tok0, PEER_SC_TOKENS)], x_v)
        for b in range(PEER_SC_RING - 1):
            gather(b, b).start()
        if zero_out:
            @pl.loop(0, PEER_SC_TOKENS)
            def _(t):
                @plsc.parallel_loop(0, out_v.shape[1] // SC_LANES, unroll=PEER_SC_UNROLL)
                def _(c):
                    out_v[t, pl.ds(pl.multiple_of(c * SC_LANES, SC_LANES), SC_LANES)] = (
                        jnp.zeros((SC_LANES,), jnp.float32))

        @pl.loop(0, n_steps, step=PEER_SC_RING)
        def _(s0):
            for b in range(PEER_SC_RING):
                s = s0 + b
                gather(s, b).wait()

                @pl.when(s + PEER_SC_RING - 1 < n_steps)
                def _():
                    gather(s + PEER_SC_RING - 1, (b + PEER_SC_RING - 1) % PEER_SC_RING).start()

                compute(s // PEER_HEADS, s % PEER_HEADS, b, x_v, rows_v, out_v)

        pltpu.sync_copy(out_v, out_hbm.at[pl.ds(tok0, PEER_SC_TOKENS)])


def _peer_dots_compute(t, hd, b, h_v, rows_v, dots_v):
    lane = lax.iota(jnp.int32, SC_LANES)

    def col_step(c, accs):
        off = pl.multiple_of(c * SC_LANES, SC_LANES)
        hv = h_v[t, pl.ds(off, SC_LANES)]
        return tuple(accs[r] + rows_v[b, r, pl.ds(off, SC_LANES)] * hv for r in range(PEER_TOPK))

    accs = plsc.parallel_loop(
        0, D_MODEL // SC_LANES, unroll=PEER_SC_UNROLL,
        carry=tuple(jnp.zeros((SC_LANES,), jnp.float32) for _ in range(PEER_TOPK)))(col_step)
    res = jnp.zeros((SC_LANES,), jnp.float32)
    for r in range(PEER_TOPK):
        res = jnp.where(lane == r, jnp.sum(accs[r]), res)
    dots_v[t, pl.ds(hd * PEER_TOPK, PEER_TOPK)] = res


def _peer_mix_compute(t, hd, b, act_v, rows_v, out_v):
    tvec = jnp.full((SC_LANES,), t, jnp.int32)
    weights = [plsc.load_gather(act_v, [tvec, jnp.full((SC_LANES,), hd * PEER_TOPK + r, jnp.int32)])
               for r in range(PEER_TOPK)]

    @plsc.parallel_loop(0, D_MODEL // SC_LANES, unroll=PEER_SC_UNROLL)
    def _(c):
        off = pl.multiple_of(c * SC_LANES, SC_LANES)
        a = rows_v[b, 0, pl.ds(off, SC_LANES)] * weights[0]
        for r in range(1, PEER_TOPK):
            a = a + rows_v[b, r, pl.ds(off, SC_LANES)] * weights[r]
        plsc.addupdate(out_v.at[t, pl.ds(off, SC_LANES)], a)


def _peer_expert_dots(h, experts, u_tab):
    T = h.shape[0]
    assert T % (SC_CORES * SC_SUBCORES * PEER_SC_TOKENS) == 0
    assert (PEER_SC_TOKENS * PEER_HEADS) % PEER_SC_RING == 0 and PEER_TOPK == SC_LANES
    body = partial(_peer_sc_body, _peer_dots_compute, False)
    return _peer_sc_call(body, T, PEER_SLOTS, D_MODEL)(h, experts, u_tab)


def _peer_expert_mix(act, experts, v_tab):
    T = act.shape[0]
    assert T % (SC_CORES * SC_SUBCORES * PEER_SC_TOKENS) == 0
    body = partial(_peer_sc_body, _peer_mix_compute, True)
    return _peer_sc_call(body, T, D_MODEL, PEER_SLOTS)(act, experts, v_tab)


PEER_TM = 256


def _extract_topk(s, k, payload=None):
    R = s.shape[0]
    riota = lax.broadcasted_iota(jnp.int32, s.shape, 0)
    vals, rows = [], []
    for _ in range(k):
        m = jnp.max(s, axis=0, keepdims=True)
        pos = jnp.min(jnp.where(s == m, riota, R), axis=0, keepdims=True)
        hit = riota == pos
        vals.append(m)
        rows.append(pos if payload is None else jnp.max(jnp.where(hit, payload, -1), axis=0, keepdims=True))
        s = jnp.where(hit, -jnp.inf, s)
    return jnp.concatenate(vals, axis=0), jnp.concatenate(rows, axis=0)


def _peer_route_kernel(x_ref, sc_ref, sh_ref, wq_ref, keys_ref, h_ref, ex_ref, gate_ref, q_ref, ext_ref, gt_ref):
    K = PEER_TOPK
    f32 = jnp.float32
    h = x_ref[0] * (1.0 + sc_ref[0]) + sh_ref[0]
    h_ref[0] = h
    q = jnp.dot(h.astype(wq_ref.dtype), wq_ref[...], preferred_element_type=f32)
    for hd in range(PEER_HEADS):
        q_ref[hd] = q[:, hd * PEER_KEY_DIM:(hd + 1) * PEER_KEY_DIM].astype(q_ref.dtype)

    def head_body(hd, carry):
        qh = q_ref[hd]
        s1 = lax.dot_general(keys_ref[hd, 0], qh[:, :PEER_HALF], _NT, preferred_element_type=f32)
        s2 = lax.dot_general(keys_ref[hd, 1], qh[:, PEER_HALF:], _NT, preferred_element_type=f32)
        v1, i1 = _extract_topk(s1, K)
        v2, i2 = _extract_topk(s2, K)
        tm = v1.shape[1]
        cand_rows, cidx_rows = [], []
        for a in range(K):
            nb = K // (a + 1)
            cand_rows.append(v1[a:a + 1, :] + v2[:nb, :])
            cidx_rows.append(i1[a:a + 1, :] * PEER_N_KEYS + i2[:nb, :])
        n_pad = -sum(r.shape[0] for r in cand_rows) % 8
        cand_rows.append(jnp.full((n_pad, tm), -jnp.inf, f32))
        cidx_rows.append(jnp.full((n_pad, tm), -1, jnp.int32))
        cand = jnp.concatenate(cand_rows, axis=0)
        cidx = jnp.concatenate(cidx_rows, axis=0)
        top_s, experts = _extract_topk(cand, K, payload=cidx)
        e = jnp.exp(top_s - top_s[0:1, :])
        gt_ref[pl.ds(pl.multiple_of(hd * K, K), K), :] = e / jnp.sum(e, axis=0, keepdims=True)
        ext_ref[pl.ds(pl.multiple_of(hd * K, K), K), :] = experts
        return carry

    lax.fori_loop(0, PEER_HEADS, head_body, 0)
    ex_ref[0] = ext_ref[...].T
    gate_ref[0] = gt_ref[...].T


def _peer_route(x, sc, sh, w_pq, sub_keys):
    B, S, D = x.shape
    tm = PEER_TM
    return pl.pallas_call(
        _peer_route_kernel,
        grid=(B, S // tm),
        in_specs=[
            pl.BlockSpec((1, tm, D), lambda b, i: (b, i, 0)),
            pl.BlockSpec((1, 1, D), lambda b, i: (b, 0, 0)),
            pl.BlockSpec((1, 1, D), lambda b, i: (b, 0, 0)),
            pl.BlockSpec((D, PEER_HEADS * PEER_KEY_DIM), lambda b, i: (0, 0)),
            pl.BlockSpec((PEER_HEADS, 2, PEER_N_KEYS, PEER_HALF), lambda b, i: (0, 0, 0, 0)),
        ],
        out_specs=[
            pl.BlockSpec((1, tm, D), lambda b, i: (b, i, 0)),
            pl.BlockSpec((1, tm, PEER_SLOTS), lambda b, i: (b, i, 0)),
            pl.BlockSpec((1, tm, PEER_SLOTS), lambda b, i: (b, i, 0)),
        ],
        out_shape=[
            jax.ShapeDtypeStruct((B, S, D), jnp.float32),
            jax.ShapeDtypeStruct((B, S, PEER_SLOTS), jnp.int32),
            jax.ShapeDtypeStruct((B, S, PEER_SLOTS), jnp.float32),
        ],
        scratch_shapes=[
            pltpu.VMEM((PEER_HEADS, tm, PEER_KEY_DIM), MM_DTYPE),
            pltpu.VMEM((PEER_SLOTS, tm), jnp.int32),
            pltpu.VMEM((PEER_SLOTS, tm), jnp.float32),
        ],
        compiler_params=pltpu.CompilerParams(dimension_semantics=("parallel", "parallel")),
        name="peer_route",
    )(x, sc[:, None, :], sh[:, None, :], w_pq.astype(MM_DTYPE), sub_keys.astype(MM_DTYPE))


def _peer_pre(h2, experts, u_tab):
    B, S, D = h2.shape
    return _peer_expert_dots(h2.reshape(B * S, D), experts.reshape(B * S, PEER_SLOTS), u_tab)


def _peer_out(pre, experts, gates, v_tab):
    B, S, _ = experts.shape
    act = jax.nn.gelu(pre, approximate=False) * gates.reshape(B * S, PEER_SLOTS)
    return _peer_expert_mix(act, experts.reshape(B * S, PEER_SLOTS), v_tab).reshape(B, S, D_MODEL)


def kernel(x, c, w_ada, b_ada, w_in, rw_mu, rw_w0, rw_w2, rw_a0, rw_a2, rw_g2, rw_k_k, rw_k_a, rw_r_k, rw_gn_g, rw_gn_b, dsa_kv_g, idx_k_g, idx_k_b, rel_bias, w_br_a, w_br_b, w_out, ln1_g, ln1_b, peer_wq, peer_keys, peer_u, peer_v, ln2_g, ln2_b):
    l = 0
    mod = jax.nn.silu(c) @ w_ada[l] + b_ada[l]

    w_rw, w_q, w_kv, w_qi, w_ki, w_wi, w_ga, w_gb = jnp.split(w_in[l], _split_points(IN_SIZES), axis=-1)
    small_pad = jnp.zeros((D_MODEL, 512 - 256 - 128 - 64 - 4), w_in.dtype)
    w_small = jnp.concatenate([w_qi, w_kv, w_ki, w_wi, small_pad], axis=-1)
    w_gates = jnp.concatenate([w_ga, w_gb], axis=-1)

    def project_and_dsa(x, sc1, sh1):
        z_rw = _mod_matmul(x, sc1, sh1, w_rw, jnp.float32, tn=896)
        z_q = _mod_matmul(x, sc1, sh1, w_q, MM_DTYPE, tn=1024)
        z_small = _mod_matmul(x, sc1, sh1, w_small, jnp.float32, tn=512)
        z_g = _mod_matmul(x, sc1, sh1, w_gates, jnp.float32, tn=1024)
        z_wi_t = jnp.swapaxes(z_small[..., 448:448 + IDX_HEADS], 1, 2)
        y_b = _dsa_attention(z_q, z_small, z_wi_t, dsa_kv_g[l], idx_k_g[l], idx_k_b[l],
                             rel_bias, jnp.float32) @ w_br_b[l]
        return y_b, z_rw, z_g

    def rwkv_and_merge(x, gt1, y_b, z_rw, z_g):
        z_ga, z_gb = z_g[..., :D_MODEL], z_g[..., D_MODEL:]
        y_a = _rwkv7_time_mix(z_rw, rw_mu[l], rw_w0[l], rw_w2[l], rw_a0[l], rw_a2[l], rw_g2[l],
                              rw_k_k[l], rw_k_a[l], rw_r_k[l], rw_gn_g[l], rw_gn_b[l], jnp.float32) @ w_br_a[l]
        merged = jax.nn.sigmoid(z_ga) * y_a + jax.nn.sigmoid(z_gb) * y_b
        mix = merged @ w_out[l]
        return _layer_norm(DEEPNORM_ALPHA * x + gt1[:, None] * mix, ln1_g[l], ln1_b[l])

    def finish(st):
        y2 = _peer_out(st["pre"], st["experts"], st["gates"], peer_v[l])
        return _layer_norm(DEEPNORM_ALPHA * st["x1"] + st["gt2"][:, None] * y2, ln2_g[l], ln2_b[l])

    gsz = x.shape[0] // BATCH_GROUPS
    outs = []
    prev = None
    for g in range(BATCH_GROUPS):
        sh1, sc1, gt1, sh2, sc2, gt2 = jnp.split(mod[g * gsz:(g + 1) * gsz], 6, axis=-1)
        x_g = x[g * gsz:(g + 1) * gsz]
        if prev is not None:
            x_g, prev["experts"] = lax.optimization_barrier((x_g, prev["experts"]))
        proj = project_and_dsa(x_g, sc1, sh1)
        if prev is not None:
            proj, prev["pre"] = lax.optimization_barrier((proj, prev["pre"]))
            outs.append(finish(prev))
        x1 = rwkv_and_merge(x_g, gt1, *proj)
        h2, experts, gates = _peer_route(x1, sc2, sh2, peer_wq[l], peer_keys[l])
        prev = dict(x1=x1, gt2=gt2, experts=experts, gates=gates, pre=_peer_pre(h2, experts, peer_u[l]))
    outs.append(finish(prev))
    return jnp.concatenate(outs, axis=0)
```

```python
import math
from functools import partial

import jax
import jax.numpy as jnp
import numpy as np
from jax import lax
from jax.experimental import pallas as pl
from jax.experimental.pallas import tpu as pltpu
from jax.experimental.pallas import tpu_sc as plsc

D_MODEL = 1024
RW_HEADS = 8
RW_HEAD_DIM = 64
RW_DIM = 512
RW_DECAY_LORA = 64
RW_A_LORA = 64
RW_GATE_LORA = 128
RW_SIZES = (RW_DIM, RW_DIM, RW_DIM, RW_DECAY_LORA, RW_A_LORA, RW_GATE_LORA)
RW_COLS = 3 * RW_DIM + RW_DECAY_LORA + RW_A_LORA + RW_GATE_LORA
RW_GN_EPS = 64e-5
DSA_HEADS = 8
DSA_LATENT = 128
DSA_Q_DIM = DSA_HEADS * DSA_LATENT
IDX_HEADS = 4
IDX_DIM = 64
TOPK_MAX = 256
Q_BLOCK = 128
REL_BUCKETS = 32
REL_MAX_DIST = 128
IN_SIZES = (RW_COLS, DSA_Q_DIM, DSA_LATENT, IDX_HEADS * IDX_DIM, IDX_DIM, IDX_HEADS, D_MODEL, D_MODEL)
IN_COLS = sum(IN_SIZES)
PEER_HEADS = 8
PEER_N_KEYS = 128
PEER_KEY_DIM = 128
PEER_HALF = 64
PEER_TOPK = 16
PEER_CHUNK = 128
LN_EPS = 1e-5
DEPTH = 1
DEEPNORM_ALPHA = (2.0 * DEPTH) ** 0.25

LANES = 128
MM_DTYPE = jnp.bfloat16
BATCH_GROUPS = 8


def _split_points(sizes):
    return np.cumsum(sizes)[:-1].tolist()


def _mod_matmul_kernel(x_ref, sc_ref, sh_ref, w_ref, o_ref):
    h = x_ref[0] * (1.0 + sc_ref[0]) + sh_ref[0]
    o_ref[0] = jnp.dot(h.astype(w_ref.dtype), w_ref[...],
                       preferred_element_type=jnp.float32).astype(o_ref.dtype)


def _mod_matmul(x, sc, sh, w, out_dtype, tn, tm=512):
    B, S, D = x.shape
    N = w.shape[1]
    return pl.pallas_call(
        _mod_matmul_kernel,
        grid=(B, S // tm, N // tn),
        in_specs=[
            pl.BlockSpec((1, tm, D), lambda b, i, j: (b, i, 0)),
            pl.BlockSpec((1, 1, D), lambda b, i, j: (b, 0, 0)),
            pl.BlockSpec((1, 1, D), lambda b, i, j: (b, 0, 0)),
            pl.BlockSpec((D, tn), lambda b, i, j: (0, j)),
        ],
        out_specs=pl.BlockSpec((1, tm, tn), lambda b, i, j: (b, i, j)),
        out_shape=jax.ShapeDtypeStruct((B, S, N), out_dtype),
        name="mod_matmul",
    )(x, sc[:, None, :], sh[:, None, :], w.astype(MM_DTYPE))


def _layer_norm(x, g, b):
    mu = jnp.mean(x, -1, keepdims=True)
    var = jnp.mean(jnp.square(x - mu), -1, keepdims=True)
    return (x - mu) * lax.rsqrt(var + LN_EPS) * g + b


def _t5_bucket(n):
    n = jnp.maximum(n, 0)
    max_exact = REL_BUCKETS // 2
    nf = jnp.maximum(n, 1).astype(jnp.float32)
    large = max_exact + (jnp.log(nf / max_exact) / math.log(REL_MAX_DIST / max_exact)
                         * (REL_BUCKETS - max_exact)).astype(jnp.int32)
    large = jnp.minimum(large, REL_BUCKETS - 1)
    return jnp.where(n < max_exact, n, large)


RW_CHUNK = 64
RW_INV_BLOCK = 16
_NN = (((1,), (0,)), ((), ()))
_NT = (((1,), (1,)), ((), ()))
_BNN = (((2,), (1,)), ((0,), (0,)))
_BNT = (((2,), (2,)), ((0,), (0,)))


def _dot_f32(a, b, dims=_NN):
    return lax.dot_general(a, b, dims, precision=lax.Precision.HIGHEST,
                           preferred_element_type=jnp.float32)


def _dot_bf16x3(a, b, dims=_NN):
    f32, bf = jnp.float32, jnp.bfloat16
    a_hi, b_hi = a.astype(bf), b.astype(bf)
    a_lo = (a - a_hi.astype(f32)).astype(bf)
    b_lo = (b - b_hi.astype(f32)).astype(bf)
    out = lax.dot_general(a_hi, b_hi, dims, preferred_element_type=f32)
    out = out + lax.dot_general(a_hi, b_lo, dims, preferred_element_type=f32)
    return out + lax.dot_general(a_lo, b_hi, dims, preferred_element_type=f32)


def _bf16_terms(a):
    f32, bf = jnp.float32, jnp.bfloat16
    hi = a.astype(bf)
    r1 = a - hi.astype(f32)
    mid = r1.astype(bf)
    lo = (r1 - mid.astype(f32)).astype(bf)
    return hi, mid, lo


def _dot_lhs_split(a, b01):
    b = b01.astype(jnp.bfloat16)
    return sum(jnp.dot(t, b, preferred_element_type=jnp.float32) for t in _bf16_terms(a))


def _dot_rhs_split(a01, b):
    a = a01.astype(jnp.bfloat16)
    return sum(jnp.dot(a, t, preferred_element_type=jnp.float32) for t in _bf16_terms(b))


def _rwkv_kernel(z_ref, mu_ref, w0_ref, w2_ref, a0_ref, a2_ref, g2_ref, kk_ref, ka_ref, rk_ref,
                 gng_ref, gnb_ref, bd_ref, o_ref, m_ref, prev_ref, y_ref):
    C = z_ref.shape[1]
    N = RW_HEAD_DIM
    f32 = jnp.float32
    dot3 = _dot_bf16x3

    @pl.when(pl.program_id(1) == 0)
    def _():
        m_ref[...] = jnp.zeros(m_ref.shape, f32)
        prev_ref[...] = jnp.zeros(prev_ref.shape, f32)

    z = z_ref[0]
    row = lax.broadcasted_iota(jnp.int32, z.shape, 0)
    shifted = jnp.where(row == 0, prev_ref[...], pltpu.roll(z, 1, axis=0))
    prev_ref[...] = z[C - 1:C, :]
    zs = z + (shifted - z) * mu_ref[...]
    r = zs[:, 0:RW_DIM]
    k = zs[:, RW_DIM:2 * RW_DIM]
    v = zs[:, 2 * RW_DIM:3 * RW_DIM]
    o1 = 3 * RW_DIM
    wl = zs[:, o1:o1 + RW_DECAY_LORA]
    al = zs[:, o1 + RW_DECAY_LORA:o1 + RW_DECAY_LORA + RW_A_LORA]
    gl = zs[:, o1 + RW_DECAY_LORA + RW_A_LORA:]

    bd = bd_ref[...]
    log_w = -jax.nn.softplus(-(w0_ref[...] + dot3(jnp.tanh(wl), w2_ref[...]))) - 0.5
    ldec = -jnp.exp(log_w)
    a_lr = jax.nn.sigmoid(a0_ref[...] + dot3(al, a2_ref[...]))
    g = dot3(jax.nn.sigmoid(gl), g2_ref[...])
    kk = k * kk_ref[...]
    kk = kk * lax.rsqrt(jnp.maximum(_dot_lhs_split(kk * kk, bd), 1e-24))
    k2 = k * (1.0 + (a_lr - 1.0) * ka_ref[...])
    a_vec = -kk
    b_vec = kk * a_lr

    ti = lax.broadcasted_iota(jnp.int32, (C, C), 0)
    tj = lax.broadcasted_iota(jnp.int32, (C, C), 1)
    cum = _dot_rhs_split((ti >= tj).astype(f32), ldec)
    cum_last = cum[C - 1:C, :]
    w_incl = jnp.exp(cum)
    w_excl = jnp.exp(cum - ldec)
    w_inv = jnp.exp(-cum)
    w_end = jnp.exp(cum_last - cum)
    w_all = jnp.exp(cum_last)
    a_t = a_vec * w_excl
    r_t = r * w_incl
    b_t = b_vec * w_inv
    k_t = k2 * w_inv
    b_e = b_vec * w_end
    k_e = k2 * w_end

    strict = ti > tj
    incl = ti >= tj
    bi, bj = ti // RW_INV_BLOCK, tj // RW_INV_BLOCK
    same_blk = bi == bj
    pair_blk = jnp.logical_and((bi // 2) == (bj // 2), jnp.logical_not(same_blk))
    half_blk = (bi // 2) != (bj // 2)
    eye = (ti == tj).astype(f32)

    H = RW_HEADS
    heads = lambda x: jnp.stack([x[:, h * N:(h + 1) * N] for h in range(H)], axis=0)
    bmm = lambda a, b: dot3(a, b, _BNN)
    A, Rt, Bt, Kt, Be, Ke, V = (heads(t) for t in (a_t, r_t, b_t, k_t, b_e, k_e, v))
    gm = dot3(jnp.concatenate([A, Rt], axis=1), jnp.concatenate([Bt, Kt], axis=1), _BNT)
    a_ab = jnp.where(strict, gm[:, :C, :C], 0.0)
    a_ak = jnp.where(strict, gm[:, :C, C:], 0.0)
    a_rb = jnp.where(incl, gm[:, C:, :C], 0.0)
    a_rk = jnp.where(incl, gm[:, C:, C:], 0.0)
    d1 = jnp.where(same_blk, a_ab, 0.0)
    xinv = eye + d1
    d2 = bmm(d1, d1)
    xinv = xinv + bmm(xinv, d2)
    d4 = bmm(d2, d2)
    xinv = xinv + bmm(xinv, d4)
    d8 = bmm(d4, d4)
    xinv = xinv + bmm(xinv, d8)
    xinv = xinv + bmm(bmm(xinv, jnp.where(pair_blk, a_ab, 0.0)), xinv)
    xinv = xinv + bmm(bmm(xinv, jnp.where(half_blk, a_ab, 0.0)), xinv)

    av = bmm(jnp.concatenate([a_ak, a_rk], axis=1), V)
    p = bmm(xinv, jnp.concatenate([A, av[:, :C]], axis=2))
    qm = bmm(a_rb, p)
    q1 = Rt + qm[:, :, :N]
    q2 = qm[:, :, N:] + av[:, C:]
    gmat = bmm(jnp.swapaxes(Be, 1, 2), p)
    g1 = eye * heads(w_all) + gmat[:, :, :N]
    g2 = gmat[:, :, N:] + bmm(jnp.swapaxes(Ke, 1, 2), V)
    m = m_ref[...]
    yh = _dot_f32(q1, m, _BNN) + q2
    m_ref[...] = _dot_f32(g1, m, _BNN) + g2
    for h in range(H):
        y_ref[:, h * N:(h + 1) * N] = yh[h]

    y = y_ref[...]
    mean = _dot_lhs_split(y, bd) * (1.0 / N)
    yc = y - mean
    var = _dot_lhs_split(yc * yc, bd) * (1.0 / N)
    yn = yc * lax.rsqrt(var + RW_GN_EPS) * gng_ref[...] + gnb_ref[...]
    bonus = _dot_lhs_split(r * k2 * rk_ref[...], bd) * v
    o_ref[0] = ((yn + bonus) * g).astype(o_ref.dtype)


def _rwkv7_time_mix(z_rw, mu, w0, w2, a0, a2, g2, k_k, k_a, r_k, gn_g, gn_b, out_dtype):
    B, S, _ = z_rw.shape
    C = RW_CHUNK
    assert S % C == 0 and C == RW_HEAD_DIM and C % (4 * RW_INV_BLOCK) == 0
    hid = jnp.arange(RW_DIM) // RW_HEAD_DIM
    bd = (hid[:, None] == hid[None, :]).astype(jnp.float32)
    row = lambda a: a.reshape(1, -1)
    full = lambda shape: pl.BlockSpec(shape, lambda b, c: (0,) * len(shape))
    return pl.pallas_call(
        _rwkv_kernel,
        grid=(B, S // C),
        in_specs=[
            pl.BlockSpec((1, C, RW_COLS), lambda b, c: (b, c, 0)),
            full((1, RW_COLS)), full((1, RW_DIM)), full((RW_DECAY_LORA, RW_DIM)), full((1, RW_DIM)),
            full((RW_A_LORA, RW_DIM)), full((RW_GATE_LORA, RW_DIM)), full((1, RW_DIM)), full((1, RW_DIM)),
            full((1, RW_DIM)), full((1, RW_DIM)), full((1, RW_DIM)), full((RW_DIM, RW_DIM)),
        ],
        out_specs=pl.BlockSpec((1, C, RW_DIM), lambda b, c: (b, c, 0)),
        out_shape=jax.ShapeDtypeStruct((B, S, RW_DIM), out_dtype),
        scratch_shapes=[
            pltpu.VMEM((RW_HEADS, RW_HEAD_DIM, RW_HEAD_DIM), jnp.float32),
            pltpu.VMEM((1, RW_COLS), jnp.float32),
            pltpu.VMEM((C, RW_DIM), jnp.float32),
        ],
        compiler_params=pltpu.CompilerParams(dimension_semantics=("parallel", "arbitrary")),
        name="rwkv7_time_mix",
    )(z_rw, row(mu), row(w0), w2, row(a0), a2, g2, row(k_k), row(k_a), row(r_k), row(gn_g), row(gn_b), bd)


DSA_TQ = 256
MASK_NEG = -1e30
INT_MIN = -2 ** 31
KEY_NEG_INF = -2139095041
THRESH_BITS = 32


def _dsa_prep_kernel(z_ref, kvg_ref, kig_ref, kib_ref, qi_ref, kv_ref, ki_ref):
    z = z_ref[0]
    qi_ref[0] = z[:, :IDX_HEADS * IDX_DIM].astype(qi_ref.dtype)
    kv = z[:, 256:384]
    ms = jnp.mean(jnp.square(kv), -1, keepdims=True)
    kv_ref[0] = (kv * lax.rsqrt(ms + LN_EPS) * kvg_ref[...]).astype(kv_ref.dtype)
    ki = z[:, 384:448]
    mu = jnp.mean(ki, -1, keepdims=True)
    var = jnp.mean(jnp.square(ki - mu), -1, keepdims=True)
    ki_ref[0] = ((ki - mu) * lax.rsqrt(var + LN_EPS) * kig_ref[...] + kib_ref[...]).astype(ki_ref.dtype)


def _dsa_prep(z_small, kv_g, ki_g, ki_b, tm=512):
    B, S, W = z_small.shape
    return pl.pallas_call(
        _dsa_prep_kernel,
        grid=(B, S // tm),
        in_specs=[
            pl.BlockSpec((1, tm, W), lambda b, i: (b, i, 0)),
            pl.BlockSpec((1, DSA_LATENT), lambda b, i: (0, 0)),
            pl.BlockSpec((1, IDX_DIM), lambda b, i: (0, 0)),
            pl.BlockSpec((1, IDX_DIM), lambda b, i: (0, 0)),
        ],
        out_specs=[
            pl.BlockSpec((1, tm, IDX_HEADS * IDX_DIM), lambda b, i: (b, i, 0)),
            pl.BlockSpec((1, tm, DSA_LATENT), lambda b, i: (b, i, 0)),
            pl.BlockSpec((1, tm, IDX_DIM), lambda b, i: (b, i, 0)),
        ],
        out_shape=[
            jax.ShapeDtypeStruct((B, S, IDX_HEADS * IDX_DIM), MM_DTYPE),
            jax.ShapeDtypeStruct((B, S, DSA_LATENT), MM_DTYPE),
            jax.ShapeDtypeStruct((B, S, IDX_DIM), MM_DTYPE),
        ],
        name="dsa_prep",
    )(z_small, kv_g[None], ki_g[None], ki_b[None])


def _sortable_key(s):
    s = jnp.where(s == 0.0, 0.0, s)
    bits = pltpu.bitcast(s, jnp.int32)
    return bits ^ ((bits >> 31) & 0x7FFFFFFF)


def _col_count(mask_i32):
    tk, tq = mask_i32.shape
    return jnp.sum(mask_i32.reshape(tk // 8, 8, tq), axis=0)


def _dsa_kernel(bfar_ref, q_ref, qi_ref, wit_ref, kv_ref, kvt_ref, ki_ref, b0_ref, b1_ref, o_ref,
                key_ref, madd_ref, m_ref, l_ref, acc_ref, *, topk):
    tq = q_ref.shape[1]
    tk = tq
    i = pl.program_id(1)
    nj = i + 1
    f32 = jnp.float32
    krow = lax.broadcasted_iota(jnp.int32, (tk, tq), 0)
    qcol = lax.broadcasted_iota(jnp.int32, (tk, tq), 1)

    qi = qi_ref[0]
    wit = wit_ref[0] * (IDX_HEADS ** -0.5)

    def score_chunk(j, carry):
        off = pl.multiple_of(j * tk, tk)
        kc = ki_ref[0, pl.ds(off, tk), :]
        s = jnp.zeros((tk, tq), f32)
        for h in range(IDX_HEADS):
            d = lax.dot_general(kc, qi[:, h * IDX_DIM:(h + 1) * IDX_DIM],
                                (((1,), (1,)), ((), ())), preferred_element_type=f32)
            s = s + wit[h:h + 1, :] * jnp.maximum(d * (IDX_DIM ** -0.5), 0.0)
        causal = (krow + j * tk) <= (qcol + i * tq)
        s = jnp.where(causal, s, -jnp.inf)
        key_ref[j] = _sortable_key(s)
        return carry

    lax.fori_loop(0, nj, score_chunk, 0)

    def count_where(pred_fn):
        def body(j, acc):
            return acc + _col_count(pred_fn(key_ref[j], j).astype(jnp.int32))
        acc = lax.fori_loop(0, nj, body, jnp.zeros((8, tq), jnp.int32))
        return jnp.sum(acc, axis=0, keepdims=True)

    def bit_step(it, t_u):
        bit = THRESH_BITS - 1 - it
        cand_u = t_u | jnp.left_shift(jnp.int32(1), bit)
        cand = cand_u ^ INT_MIN
        cnt = count_where(lambda k, j: k >= cand)
        return jnp.where(cnt >= topk, cand_u, t_u)

    t_u = lax.fori_loop(0, THRESH_BITS, bit_step, jnp.zeros((1, tq), jnp.int32))
    thr = t_u ^ INT_MIN
    cnt_gt = count_where(lambda k, j: k > thr)
    cnt_ge = count_where(lambda k, j: k >= thr)
    is_neg = thr == KEY_NEG_INF
    need = jnp.logical_and(cnt_ge > topk, jnp.logical_not(is_neg))
    n_tie_take = topk - cnt_gt

    thr_open = jnp.where(is_neg, thr, thr - 1)
    any_need = jnp.max(need.astype(jnp.int32)) > 0

    @pl.when(jnp.logical_not(any_need))
    def _():
        def body(j, carry):
            madd_ref[j] = jnp.where(key_ref[j] > thr_open, 0.0, MASK_NEG)
            return carry
        lax.fori_loop(0, nj, body, 0)

    @pl.when(any_need)
    def _():
        s_len = tk * key_ref.shape[0]
        n_bits = max(1, int(math.ceil(math.log2(s_len))))

        def idx_step(it, p):
            bit = n_bits - 1 - it
            cand = p | jnp.left_shift(jnp.int32(1), bit)
            cnt = count_where(
                lambda k, j: jnp.where(k == thr, jnp.where((krow + j * tk) < cand, 1, 0), 0))
            return jnp.where(cnt < n_tie_take, cand, p)

        p_idx = lax.fori_loop(0, n_bits, idx_step, jnp.zeros((1, tq), jnp.int32))
        p_idx = jnp.where(need, p_idx, jnp.where(is_neg, -1, s_len))

        def body(j, carry):
            k = key_ref[j]
            tie_ok = jnp.where((krow + j * tk) <= p_idx, 0.0, MASK_NEG)
            madd_ref[j] = jnp.where(k > thr, 0.0, jnp.where(k == thr, tie_ok, MASK_NEG))
            return carry
        lax.fori_loop(0, nj, body, 0)

    m_ref[...] = jnp.full(m_ref.shape, MASK_NEG, f32)
    l_ref[...] = jnp.zeros(l_ref.shape, f32)
    acc_ref[...] = jnp.zeros(acc_ref.shape, f32)
    scale = DSA_LATENT ** -0.5

    def attend(j, bias_fn):
        off = pl.multiple_of(j * tk, tk)
        kc = kv_ref[0, pl.ds(off, tk), :]
        kct = kvt_ref[0, :, pl.ds(off, tk)]
        madd = madd_ref[j]
        for h in range(DSA_HEADS):
            qh = q_ref[0, :, h * DSA_LATENT:(h + 1) * DSA_LATENT]
            lg = lax.dot_general(kc, qh, (((1,), (1,)), ((), ())), preferred_element_type=f32)
            lg = lg * scale + bias_fn(h) + madd
            m_old = m_ref[h]
            m_new = jnp.maximum(m_old, jnp.max(lg, axis=0, keepdims=True))
            alpha = jnp.exp(m_old - m_new)
            p = jnp.exp(lg - m_new)
            l_ref[h] = alpha * l_ref[h] + jnp.sum(p, axis=0, keepdims=True)
            acc_ref[h] = alpha * acc_ref[h] + jnp.dot(kct, p.astype(kct.dtype),
                                                      preferred_element_type=f32)
            m_ref[h] = m_new

    def far_body(j, carry):
        attend(j, lambda h: bfar_ref[h])
        return carry

    lax.fori_loop(0, jnp.maximum(i - 1, 0), far_body, 0)

    @pl.when(i >= 1)
    def _():
        attend(i - 1, lambda h: b1_ref[h])

    attend(i, lambda h: b0_ref[h])

    for h in range(DSA_HEADS):
        out_t = acc_ref[h] / l_ref[h]
        o_ref[0, :, h * DSA_LATENT:(h + 1) * DSA_LATENT] = out_t.T.astype(o_ref.dtype)


def _dsa_bias_tiles(rel_bias, tq):
    dist = jnp.arange(2 * tq, dtype=jnp.int32)
    by_dist = rel_bias[_t5_bucket(dist)].T
    kk = jnp.arange(tq, dtype=jnp.int32)[:, None]
    qq = jnp.arange(tq, dtype=jnp.int32)[None, :]
    b0 = by_dist[:, jnp.maximum(qq - kk, 0)]
    b1 = by_dist[:, tq + qq - kk]
    return b0, b1, by_dist[:, -1]


def _dsa_attention(q, z_small, z_wi_t, kv_g, idx_k_g, idx_k_b, rel_bias, out_dtype):
    B, S, _ = q.shape
    tq = DSA_TQ
    topk = min(TOPK_MAX, S // 4)
    assert S % tq == 0 and tq >= REL_MAX_DIST and topk <= tq
    qi, ckv, kidx = _dsa_prep(z_small, kv_g, idx_k_g, idx_k_b)
    ckv_t = jnp.swapaxes(ckv, 1, 2)
    b0, b1, bfar = _dsa_bias_tiles(rel_bias, tq)
    nq = S // tq
    H, d = DSA_HEADS, DSA_LATENT
    grid_spec = pltpu.PrefetchScalarGridSpec(
        num_scalar_prefetch=0,
        grid=(B, nq),
        in_specs=[
            pl.BlockSpec(memory_space=pltpu.SMEM),
            pl.BlockSpec((1, tq, H * d), lambda b, i: (b, i, 0)),
            pl.BlockSpec((1, tq, IDX_HEADS * IDX_DIM), lambda b, i: (b, i, 0)),
            pl.BlockSpec((1, IDX_HEADS, tq), lambda b, i: (b, 0, i)),
            pl.BlockSpec((1, S, d), lambda b, i: (b, 0, 0)),
            pl.BlockSpec((1, d, S), lambda b, i: (b, 0, 0)),
            pl.BlockSpec((1, S, IDX_DIM), lambda b, i: (b, 0, 0)),
            pl.BlockSpec((H, tq, tq), lambda b, i: (0, 0, 0)),
            pl.BlockSpec((H, tq, tq), lambda b, i: (0, 0, 0)),
        ],
        out_specs=pl.BlockSpec((1, tq, H * d), lambda b, i: (b, i, 0)),
        scratch_shapes=[
            pltpu.VMEM((nq, tq, tq), jnp.int32),
            pltpu.VMEM((nq, tq, tq), jnp.float32),
            pltpu.VMEM((H, 1, tq), jnp.float32),
            pltpu.VMEM((H, 1, tq), jnp.float32),
            pltpu.VMEM((H, d, tq), jnp.float32),
        ],
    )
    return pl.pallas_call(
        partial(_dsa_kernel, topk=topk),
        grid_spec=grid_spec,
        out_shape=jax.ShapeDtypeStruct((B, S, H * d), out_dtype),
        compiler_params=pltpu.CompilerParams(vmem_limit_bytes=48 * 1024 * 1024),
        name="dsa_attention",
    )(bfar, q, qi, z_wi_t, ckv, ckv_t, kidx, b0, b1)


SC_CORES = 2
SC_SUBCORES = 16
SC_LANES = 16
PEER_SC_TOKENS = 8
PEER_SC_RING = 4
PEER_SC_UNROLL = 4
PEER_SLOTS = PEER_HEADS * PEER_TOPK


def _peer_sc_call(body, T, out_width, stage_width):
    mesh = plsc.VectorSubcoreMesh(core_axis_name="c", subcore_axis_name="s")
    return pl.kernel(
        body, mesh=mesh,
        out_type=jax.ShapeDtypeStruct((T, out_width), jnp.float32),
        scratch_types=[
            pltpu.VMEM((PEER_SC_TOKENS, PEER_SLOTS), jnp.int32),
            pltpu.VMEM((PEER_SC_TOKENS, stage_width), jnp.float32),
            pltpu.VMEM((PEER_SC_RING, PEER_TOPK, D_MODEL), jnp.float32),
            pltpu.VMEM((PEER_SC_TOKENS, out_width), jnp.float32),
            pltpu.SemaphoreType.DMA((PEER_SC_RING,)),
        ],
        compiler_params=pltpu.CompilerParams(needs_layout_passes=False),
    )


def _peer_sc_body(compute, zero_out, x_hbm, idx_hbm, tab_hbm, out_hbm, idx_v, x_v, rows_v, out_v, sems):
    T = idx_hbm.shape[0]
    tokens_per_worker = T // (SC_CORES * SC_SUBCORES)
    n_steps = PEER_SC_TOKENS * PEER_HEADS
    worker = lax.axis_index("s") * SC_CORES + lax.axis_index("c")
    base = worker * tokens_per_worker

    def gather(s, b):
        ids = idx_v[s // PEER_HEADS, pl.ds((s % PEER_HEADS) * PEER_TOPK, PEER_TOPK)]
        return pltpu.make_async_copy(tab_hbm.at[ids], rows_v.at[b], sems.at[b])

    @pl.loop(0, tokens_per_worker // PEER_SC_TOKENS)
    def _(blk):
        tok0 = base + blk * PEER_SC_TOKENS
        pltpu.sync_copy(idx_hbm.at[pl.ds(tok0, PEER_SC_TOKENS)], idx_v)
        pltpu.sync_copy(x_hbm.at[pl.ds(tok0, PEER_SC_TOKENS)], x_v)
        for b in range(PEER_SC_RING - 1):
            gather(b, b).start()
        if zero_out:
            @pl.loop(0, PEER_SC_TOKENS)
            def _(t):
                @plsc.parallel_loop(0, out_v.shape[1] // SC_LANES, unroll=PEER_SC_UNROLL)
                def _(c):
                    out_v[t, pl.ds(pl.multiple_of(c * SC_LANES, SC_LANES), SC_LANES)] = (
                        jnp.zeros((SC_LANES,), jnp.float32))

        @pl.loop(0, n_steps, step=PEER_SC_RING)
        def _(s0):
            for b in range(PEER_SC_RING):
                s = s0 + b
                gather(s, b).wait()

                @pl.when(s + PEER_SC_RING - 1 < n_steps)
                def _():
                    gather(s + PEER_SC_RING - 1, (b + PEER_SC_RING - 1) % PEER_SC_RING).start()

                compute(s // PEER_HEADS, s % PEER_HEADS, b, x_v, rows_v, out_v)

        pltpu.sync_copy(out_v, out_hbm.at[pl.ds(tok0, PEER_SC_TOKENS)])


def _peer_dots_compute(t, hd, b, h_v, rows_v, dots_v):
    lane = lax.iota(jnp.int32, SC_LANES)

    def col_step(c, accs):
        off = pl.multiple_of(c * SC_LANES, SC_LANES)
        hv = h_v[t, pl.ds(off, SC_LANES)]
        return tuple(accs[r] + rows_v[b, r, pl.ds(off, SC_LANES)] * hv for r in range(PEER_TOPK))

    accs = plsc.parallel_loop(
        0, D_MODEL // SC_LANES, unroll=PEER_SC_UNROLL,
        carry=tuple(jnp.zeros((SC_LANES,), jnp.float32) for _ in range(PEER_TOPK)))(col_step)
    res = jnp.zeros((SC_LANES,), jnp.float32)
    for r in range(PEER_TOPK):
        res = jnp.where(lane == r, jnp.sum(accs[r]), res)
    dots_v[t, pl.ds(hd * PEER_TOPK, PEER_TOPK)] = res


def _peer_mix_compute(t, hd, b, act_v, rows_v, out_v):
    tvec = jnp.full((SC_LANES,), t, jnp.int32)
    weights = [plsc.load_gather(act_v, [tvec, jnp.full((SC_LANES,), hd * PEER_TOPK + r, jnp.int32)])
               for r in range(PEER_TOPK)]

    @plsc.parallel_loop(0, D_MODEL // SC_LANES, unroll=PEER_SC_UNROLL)
    def _(c):
        off = pl.multiple_of(c * SC_LANES, SC_LANES)
        a = rows_v[b, 0, pl.ds(off, SC_LANES)] * weights[0]
        for r in range(1, PEER_TOPK):
            a = a + rows_v[b, r, pl.ds(off, SC_LANES)] * weights[r]
        plsc.addupdate(out_v.at[t, pl.ds(off, SC_LANES)], a)


def _peer_expert_dots(h, experts, u_tab):
    T = h.shape[0]
    assert T % (SC_CORES * SC_SUBCORES * PEER_SC_TOKENS) == 0
    assert (PEER_SC_TOKENS * PEER_HEADS) % PEER_SC_RING == 0 and PEER_TOPK == SC_LANES
    body = partial(_peer_sc_body, _peer_dots_compute, False)
    return _peer_sc_call(body, T, PEER_SLOTS, D_MODEL)(h, experts, u_tab)


def _peer_expert_mix(act, experts, v_tab):
    T = act.shape[0]
    assert T % (SC_CORES * SC_SUBCORES * PEER_SC_TOKENS) == 0
    body = partial(_peer_sc_body, _peer_mix_compute, True)
    return _peer_sc_call(body, T, D_MODEL, PEER_SLOTS)(act, experts, v_tab)


PEER_TM = 256


def _extract_topk(s, k, payload=None):
    R = s.shape[0]
    riota = lax.broadcasted_iota(jnp.int32, s.shape, 0)
    vals, rows = [], []
    for _ in range(k):
        m = jnp.max(s, axis=0, keepdims=True)
        pos = jnp.min(jnp.where(s == m, riota, R), axis=0, keepdims=True)
        hit = riota == pos
        vals.append(m)
        rows.append(pos if payload is None else jnp.max(jnp.where(hit, payload, -1), axis=0, keepdims=True))
        s = jnp.where(hit, -jnp.inf, s)
    return jnp.concatenate(vals, axis=0), jnp.concatenate(rows, axis=0)


def _peer_route_kernel(x_ref, sc_ref, sh_ref, wq_ref, keys_ref, h_ref, ex_ref, gate_ref, q_ref, ext_ref, gt_ref):
    K = PEER_TOPK
    f32 = jnp.float32
    h = x_ref[0] * (1.0 + sc_ref[0]) + sh_ref[0]
    h_ref[0] = h
    q = jnp.dot(h.astype(wq_ref.dtype), wq_ref[...], preferred_element_type=f32)
    for hd in range(PEER_HEADS):
        q_ref[hd] = q[:, hd * PEER_KEY_DIM:(hd + 1) * PEER_KEY_DIM].astype(q_ref.dtype)

    def head_body(hd, carry):
        qh = q_ref[hd]
        s1 = lax.dot_general(keys_ref[hd, 0], qh[:, :PEER_HALF], _NT, preferred_element_type=f32)
        s2 = lax.dot_general(keys_ref[hd, 1], qh[:, PEER_HALF:], _NT, preferred_element_type=f32)
        v1, i1 = _extract_topk(s1, K)
        v2, i2 = _extract_topk(s2, K)
        tm = v1.shape[1]
        cand_rows, cidx_rows = [], []
        for a in range(K):
            nb = K // (a + 1)
            cand_rows.append(v1[a:a + 1, :] + v2[:nb, :])
            cidx_rows.append(i1[a:a + 1, :] * PEER_N_KEYS + i2[:nb, :])
        n_pad = -sum(r.shape[0] for r in cand_rows) % 8
        cand_rows.append(jnp.full((n_pad, tm), -jnp.inf, f32))
        cidx_rows.append(jnp.full((n_pad, tm), -1, jnp.int32))
        cand = jnp.concatenate(cand_rows, axis=0)
        cidx = jnp.concatenate(cidx_rows, axis=0)
        top_s, experts = _extract_topk(cand, K, payload=cidx)
        e = jnp.exp(top_s - top_s[0:1, :])
        gt_ref[pl.ds(pl.multiple_of(hd * K, K), K), :] = e / jnp.sum(e, axis=0, keepdims=True)
        ext_ref[pl.ds(pl.multiple_of(hd * K, K), K), :] = experts
        return carry

    lax.fori_loop(0, PEER_HEADS, head_body, 0)
    ex_ref[0] = ext_ref[...].T
    gate_ref[0] = gt_ref[...].T


def _peer_route(x, sc, sh, w_pq, sub_keys):
    B, S, D = x.shape
    tm = PEER_TM
    return pl.pallas_call(
        _peer_route_kernel,
        grid=(B, S // tm),
        in_specs=[
            pl.BlockSpec((1, tm, D), lambda b, i: (b, i, 0)),
            pl.BlockSpec((1, 1, D), lambda b, i: (b, 0, 0)),
            pl.BlockSpec((1, 1, D), lambda b, i: (b, 0, 0)),
            pl.BlockSpec((D, PEER_HEADS * PEER_KEY_DIM), lambda b, i: (0, 0)),
            pl.BlockSpec((PEER_HEADS, 2, PEER_N_KEYS, PEER_HALF), lambda b, i: (0, 0, 0, 0)),
        ],
        out_specs=[
            pl.BlockSpec((1, tm, D), lambda b, i: (b, i, 0)),
            pl.BlockSpec((1, tm, PEER_SLOTS), lambda b, i: (b, i, 0)),
            pl.BlockSpec((1, tm, PEER_SLOTS), lambda b, i: (b, i, 0)),
        ],
        out_shape=[
            jax.ShapeDtypeStruct((B, S, D), jnp.float32),
            jax.ShapeDtypeStruct((B, S, PEER_SLOTS), jnp.int32),
            jax.ShapeDtypeStruct((B, S, PEER_SLOTS), jnp.float32),
        ],
        scratch_shapes=[
            pltpu.VMEM((PEER_HEADS, tm, PEER_KEY_DIM), MM_DTYPE),
            pltpu.VMEM((PEER_SLOTS, tm), jnp.int32),
            pltpu.VMEM((PEER_SLOTS, tm), jnp.float32),
        ],
        compiler_params=pltpu.CompilerParams(dimension_semantics=("parallel", "parallel")),
        name="peer_route",
    )(x, sc[:, None, :], sh[:, None, :], w_pq.astype(MM_DTYPE), sub_keys.astype(MM_DTYPE))


def _peer_pre(h2, experts, u_tab):
    B, S, D = h2.shape
    return _peer_expert_dots(h2.reshape(B * S, D), experts.reshape(B * S, PEER_SLOTS), u_tab)


def _peer_act(pre, gates):
    return jax.nn.gelu(pre, approximate=False) * gates.reshape(pre.shape)


def _peer_out(act, experts, v_tab):
    B, S, _ = experts.shape
    return _peer_expert_mix(act, experts.reshape(B * S, PEER_SLOTS), v_tab).reshape(B, S, D_MODEL)


def kernel(x, c, w_ada, b_ada, w_in, rw_mu, rw_w0, rw_w2, rw_a0, rw_a2, rw_g2, rw_k_k, rw_k_a, rw_r_k, rw_gn_g, rw_gn_b, dsa_kv_g, idx_k_g, idx_k_b, rel_bias, w_br_a, w_br_b, w_out, ln1_g, ln1_b, peer_wq, peer_keys, peer_u, peer_v, ln2_g, ln2_b):
    l = 0
    mod = jax.nn.silu(c) @ w_ada[l] + b_ada[l]

    w_rw, w_q, w_kv, w_qi, w_ki, w_wi, w_ga, w_gb = jnp.split(w_in[l], _split_points(IN_SIZES), axis=-1)
    small_pad = jnp.zeros((D_MODEL, 512 - 256 - 128 - 64 - 4), w_in.dtype)
    w_small = jnp.concatenate([w_qi, w_kv, w_ki, w_wi, small_pad], axis=-1)
    w_gates = jnp.concatenate([w_ga, w_gb], axis=-1)

    def project_and_dsa(x, sc1, sh1):
        z_rw = _mod_matmul(x, sc1, sh1, w_rw, jnp.float32, tn=896)
        z_q = _mod_matmul(x, sc1, sh1, w_q, MM_DTYPE, tn=1024)
        z_small = _mod_matmul(x, sc1, sh1, w_small, jnp.float32, tn=512)
        z_g = _mod_matmul(x, sc1, sh1, w_gates, jnp.float32, tn=1024)
        z_wi_t = jnp.swapaxes(z_small[..., 448:448 + IDX_HEADS], 1, 2)
        y_b = _dsa_attention(z_q, z_small, z_wi_t, dsa_kv_g[l], idx_k_g[l], idx_k_b[l],
                             rel_bias, jnp.float32) @ w_br_b[l]
        return y_b, z_rw, z_g

    def rwkv_and_merge(x, gt1, y_b, z_rw, z_g):
        z_ga, z_gb = z_g[..., :D_MODEL], z_g[..., D_MODEL:]
        y_a = _rwkv7_time_mix(z_rw, rw_mu[l], rw_w0[l], rw_w2[l], rw_a0[l], rw_a2[l], rw_g2[l],
                              rw_k_k[l], rw_k_a[l], rw_r_k[l], rw_gn_g[l], rw_gn_b[l], jnp.float32) @ w_br_a[l]
        merged = jax.nn.sigmoid(z_ga) * y_a + jax.nn.sigmoid(z_gb) * y_b
        mix = merged @ w_out[l]
        return _layer_norm(DEEPNORM_ALPHA * x + gt1[:, None] * mix, ln1_g[l], ln1_b[l])

    def finish(st, act):
        y2 = _peer_out(act, st["experts"], peer_v[l])
        return _layer_norm(DEEPNORM_ALPHA * st["x1"] + st["gt2"][:, None] * y2, ln2_g[l], ln2_b[l])

    gsz = x.shape[0] // BATCH_GROUPS
    outs = []
    prev = None
    for g in range(BATCH_GROUPS):
        sh1, sc1, gt1, sh2, sc2, gt2 = jnp.split(mod[g * gsz:(g + 1) * gsz], 6, axis=-1)
        x_g = x[g * gsz:(g + 1) * gsz]
        if prev is not None:
            x_g, prev["experts"] = lax.optimization_barrier((x_g, prev["experts"]))
        proj = project_and_dsa(x_g, sc1, sh1)
        if prev is not None:
            proj, prev["pre"] = lax.optimization_barrier((proj, prev["pre"]))
            proj, act = lax.optimization_barrier((proj, _peer_act(prev["pre"], prev["gates"])))
            outs.append(finish(prev, act))
        x1 = rwkv_and_merge(x_g, gt1, *proj)
        h2, experts, gates = _peer_route(x1, sc2, sh2, peer_wq[l], peer_keys[l])
        prev = dict(x1=x1, gt2=gt2, experts=experts, gates=gates, pre=_peer_pre(h2, experts, peer_u[l]))
    outs.append(finish(prev, _peer_act(prev["pre"], prev["gates"])))
    return jnp.concatenate(outs, axis=0)
```

```python
import math
from functools import partial

import jax
import jax.numpy as jnp
import numpy as np
from jax import lax
from jax.experimental import pallas as pl
from jax.experimental.pallas import tpu as pltpu
from jax.experimental.pallas import tpu_sc as plsc

D_MODEL = 1024
RW_HEADS = 8
RW_HEAD_DIM = 64
RW_DIM = 512
RW_DECAY_LORA = 64
RW_A_LORA = 64
RW_GATE_LORA = 128
RW_SIZES = (RW_DIM, RW_DIM, RW_DIM, RW_DECAY_LORA, RW_A_LORA, RW_GATE_LORA)
RW_COLS = 3 * RW_DIM + RW_DECAY_LORA + RW_A_LORA + RW_GATE_LORA
RW_GN_EPS = 64e-5
DSA_HEADS = 8
DSA_LATENT = 128
DSA_Q_DIM = DSA_HEADS * DSA_LATENT
IDX_HEADS = 4
IDX_DIM = 64
TOPK_MAX = 256
Q_BLOCK = 128
REL_BUCKETS = 32
REL_MAX_DIST = 128
IN_SIZES = (RW_COLS, DSA_Q_DIM, DSA_LATENT, IDX_HEADS * IDX_DIM, IDX_DIM, IDX_HEADS, D_MODEL, D_MODEL)
IN_COLS = sum(IN_SIZES)
PEER_HEADS = 8
PEER_N_KEYS = 128
PEER_KEY_DIM = 128
PEER_HALF = 64
PEER_TOPK = 16
PEER_CHUNK = 128
LN_EPS = 1e-5
DEPTH = 1
DEEPNORM_ALPHA = (2.0 * DEPTH) ** 0.25

LANES = 128
MM_DTYPE = jnp.bfloat16
BATCH_GROUPS = 8


def _split_points(sizes):
    return np.cumsum(sizes)[:-1].tolist()


def _mod_matmul_kernel(x_ref, sc_ref, sh_ref, w_ref, o_ref):
    h = x_ref[0] * (1.0 + sc_ref[0]) + sh_ref[0]
    o_ref[0] = jnp.dot(h.astype(w_ref.dtype), w_ref[...],
                       preferred_element_type=jnp.float32).astype(o_ref.dtype)


def _mod_matmul(x, sc, sh, w, out_dtype, tn, tm=512):
    B, S, D = x.shape
    N = w.shape[1]
    return pl.pallas_call(
        _mod_matmul_kernel,
        grid=(B, S // tm, N // tn),
        in_specs=[
            pl.BlockSpec((1, tm, D), lambda b, i, j: (b, i, 0)),
            pl.BlockSpec((1, 1, D), lambda b, i, j: (b, 0, 0)),
            pl.BlockSpec((1, 1, D), lambda b, i, j: (b, 0, 0)),
            pl.BlockSpec((D, tn), lambda b, i, j: (0, j)),
        ],
        out_specs=pl.BlockSpec((1, tm, tn), lambda b, i, j: (b, i, j)),
        out_shape=jax.ShapeDtypeStruct((B, S, N), out_dtype),
        name="mod_matmul",
    )(x, sc[:, None, :], sh[:, None, :], w.astype(MM_DTYPE))


def _layer_norm(x, g, b):
    mu = jnp.mean(x, -1, keepdims=True)
    var = jnp.mean(jnp.square(x - mu), -1, keepdims=True)
    return (x - mu) * lax.rsqrt(var + LN_EPS) * g + b


def _t5_bucket(n):
    n = jnp.maximum(n, 0)
    max_exact = REL_BUCKETS // 2
    nf = jnp.maximum(n, 1).astype(jnp.float32)
    large = max_exact + (jnp.log(nf / max_exact) / math.log(REL_MAX_DIST / max_exact)
                         * (REL_BUCKETS - max_exact)).astype(jnp.int32)
    large = jnp.minimum(large, REL_BUCKETS - 1)
    return jnp.where(n < max_exact, n, large)


RW_CHUNK = 64
RW_INV_BLOCK = 16
_NN = (((1,), (0,)), ((), ()))
_NT = (((1,), (1,)), ((), ()))
_BNN = (((2,), (1,)), ((0,), (0,)))
_BNT = (((2,), (2,)), ((0,), (0,)))


def _dot_f32(a, b, dims=_NN):
    return lax.dot_general(a, b, dims, precision=lax.Precision.HIGHEST,
                           preferred_element_type=jnp.float32)


def _dot_bf16x3(a, b, dims=_NN):
    f32, bf = jnp.float32, jnp.bfloat16
    a_hi, b_hi = a.astype(bf), b.astype(bf)
    a_lo = (a - a_hi.astype(f32)).astype(bf)
    b_lo = (b - b_hi.astype(f32)).astype(bf)
    out = lax.dot_general(a_hi, b_hi, dims, preferred_element_type=f32)
    out = out + lax.dot_general(a_hi, b_lo, dims, preferred_element_type=f32)
    return out + lax.dot_general(a_lo, b_hi, dims, preferred_element_type=f32)


def _bf16_terms(a):
    f32, bf = jnp.float32, jnp.bfloat16
    hi = a.astype(bf)
    r1 = a - hi.astype(f32)
    mid = r1.astype(bf)
    lo = (r1 - mid.astype(f32)).astype(bf)
    return hi, mid, lo


def _dot_lhs_split(a, b01):
    b = b01.astype(jnp.bfloat16)
    return sum(jnp.dot(t, b, preferred_element_type=jnp.float32) for t in _bf16_terms(a))


def _dot_rhs_split(a01, b):
    a = a01.astype(jnp.bfloat16)
    return sum(jnp.dot(a, t, preferred_element_type=jnp.float32) for t in _bf16_terms(b))


def _rwkv_kernel(z_ref, mu_ref, w0_ref, w2_ref, a0_ref, a2_ref, g2_ref, kk_ref, ka_ref, rk_ref,
                 gng_ref, gnb_ref, bd_ref, o_ref, m_ref, prev_ref, y_ref):
    C = z_ref.shape[1]
    N = RW_HEAD_DIM
    f32 = jnp.float32
    dot3 = _dot_bf16x3

    @pl.when(pl.program_id(1) == 0)
    def _():
        m_ref[...] = jnp.zeros(m_ref.shape, f32)
        prev_ref[...] = jnp.zeros(prev_ref.shape, f32)

    z = z_ref[0]
    row = lax.broadcasted_iota(jnp.int32, z.shape, 0)
    shifted = jnp.where(row == 0, prev_ref[...], pltpu.roll(z, 1, axis=0))
    prev_ref[...] = z[C - 1:C, :]
    zs = z + (shifted - z) * mu_ref[...]
    r = zs[:, 0:RW_DIM]
    k = zs[:, RW_DIM:2 * RW_DIM]
    v = zs[:, 2 * RW_DIM:3 * RW_DIM]
    o1 = 3 * RW_DIM
    wl = zs[:, o1:o1 + RW_DECAY_LORA]
    al = zs[:, o1 + RW_DECAY_LORA:o1 + RW_DECAY_LORA + RW_A_LORA]
    gl = zs[:, o1 + RW_DECAY_LORA + RW_A_LORA:]

    bd = bd_ref[...]
    log_w = -jax.nn.softplus(-(w0_ref[...] + dot3(jnp.tanh(wl), w2_ref[...]))) - 0.5
    ldec = -jnp.exp(log_w)
    a_lr = jax.nn.sigmoid(a0_ref[...] + dot3(al, a2_ref[...]))
    g = dot3(jax.nn.sigmoid(gl), g2_ref[...])
    kk = k * kk_ref[...]
    kk = kk * lax.rsqrt(jnp.maximum(_dot_lhs_split(kk * kk, bd), 1e-24))
    k2 = k * (1.0 + (a_lr - 1.0) * ka_ref[...])
    a_vec = -kk
    b_vec = kk * a_lr

    ti = lax.broadcasted_iota(jnp.int32, (C, C), 0)
    tj = lax.broadcasted_iota(jnp.int32, (C, C), 1)
    cum = _dot_rhs_split((ti >= tj).astype(f32), ldec)
    cum_last = cum[C - 1:C, :]
    w_incl = jnp.exp(cum)
    w_excl = jnp.exp(cum - ldec)
    w_inv = jnp.exp(-cum)
    w_end = jnp.exp(cum_last - cum)
    w_all = jnp.exp(cum_last)
    a_t = a_vec * w_excl
    r_t = r * w_incl
    b_t = b_vec * w_inv
    k_t = k2 * w_inv
    b_e = b_vec * w_end
    k_e = k2 * w_end

    strict = ti > tj
    incl = ti >= tj
    bi, bj = ti // RW_INV_BLOCK, tj // RW_INV_BLOCK
    same_blk = bi == bj
    pair_blk = jnp.logical_and((bi // 2) == (bj // 2), jnp.logical_not(same_blk))
    half_blk = (bi // 2) != (bj // 2)
    eye = (ti == tj).astype(f32)

    H = RW_HEADS
    heads = lambda x: jnp.stack([x[:, h * N:(h + 1) * N] for h in range(H)], axis=0)
    bmm = lambda a, b: dot3(a, b, _BNN)
    A, Rt, Bt, Kt, Be, Ke, V = (heads(t) for t in (a_t, r_t, b_t, k_t, b_e, k_e, v))
    gm = dot3(jnp.concatenate([A, Rt], axis=1), jnp.concatenate([Bt, Kt], axis=1), _BNT)
    a_ab = jnp.where(strict, gm[:, :C, :C], 0.0)
    a_ak = jnp.where(strict, gm[:, :C, C:], 0.0)
    a_rb = jnp.where(incl, gm[:, C:, :C], 0.0)
    a_rk = jnp.where(incl, gm[:, C:, C:], 0.0)
    d1 = jnp.where(same_blk, a_ab, 0.0)
    xinv = eye + d1
    d2 = bmm(d1, d1)
    xinv = xinv + bmm(xinv, d2)
    d4 = bmm(d2, d2)
    xinv = xinv + bmm(xinv, d4)
    d8 = bmm(d4, d4)
    xinv = xinv + bmm(xinv, d8)
    xinv = xinv + bmm(bmm(xinv, jnp.where(pair_blk, a_ab, 0.0)), xinv)
    xinv = xinv + bmm(bmm(xinv, jnp.where(half_blk, a_ab, 0.0)), xinv)

    av = bmm(jnp.concatenate([a_ak, a_rk], axis=1), V)
    p = bmm(xinv, jnp.concatenate([A, av[:, :C]], axis=2))
    qm = bmm(a_rb, p)
    q1 = Rt + qm[:, :, :N]
    q2 = qm[:, :, N:] + av[:, C:]
    gmat = bmm(jnp.swapaxes(Be, 1, 2), p)
    g1 = eye * heads(w_all) + gmat[:, :, :N]
    g2 = gmat[:, :, N:] + bmm(jnp.swapaxes(Ke, 1, 2), V)
    m = m_ref[...]
    yh = _dot_f32(q1, m, _BNN) + q2
    m_ref[...] = _dot_f32(g1, m, _BNN) + g2
    for h in range(H):
        y_ref[:, h * N:(h + 1) * N] = yh[h]

    y = y_ref[...]
    mean = _dot_lhs_split(y, bd) * (1.0 / N)
    yc = y - mean
    var = _dot_lhs_split(yc * yc, bd) * (1.0 / N)
    yn = yc * lax.rsqrt(var + RW_GN_EPS) * gng_ref[...] + gnb_ref[...]
    bonus = _dot_lhs_split(r * k2 * rk_ref[...], bd) * v
    o_ref[0] = ((yn + bonus) * g).astype(o_ref.dtype)


def _rwkv7_time_mix(z_rw, mu, w0, w2, a0, a2, g2, k_k, k_a, r_k, gn_g, gn_b, out_dtype):
    B, S, _ = z_rw.shape
    C = RW_CHUNK
    assert S % C == 0 and C == RW_HEAD_DIM and C % (4 * RW_INV_BLOCK) == 0
    hid = jnp.arange(RW_DIM) // RW_HEAD_DIM
    bd = (hid[:, None] == hid[None, :]).astype(jnp.float32)
    row = lambda a: a.reshape(1, -1)
    full = lambda shape: pl.BlockSpec(shape, lambda b, c: (0,) * len(shape))
    return pl.pallas_call(
        _rwkv_kernel,
        grid=(B, S // C),
        in_specs=[
            pl.BlockSpec((1, C, RW_COLS), lambda b, c: (b, c, 0)),
            full((1, RW_COLS)), full((1, RW_DIM)), full((RW_DECAY_LORA, RW_DIM)), full((1, RW_DIM)),
            full((RW_A_LORA, RW_DIM)), full((RW_GATE_LORA, RW_DIM)), full((1, RW_DIM)), full((1, RW_DIM)),
            full((1, RW_DIM)), full((1, RW_DIM)), full((1, RW_DIM)), full((RW_DIM, RW_DIM)),
        ],
        out_specs=pl.BlockSpec((1, C, RW_DIM), lambda b, c: (b, c, 0)),
        out_shape=jax.ShapeDtypeStruct((B, S, RW_DIM), out_dtype),
        scratch_shapes=[
            pltpu.VMEM((RW_HEADS, RW_HEAD_DIM, RW_HEAD_DIM), jnp.float32),
            pltpu.VMEM((1, RW_COLS), jnp.float32),
            pltpu.VMEM((C, RW_DIM), jnp.float32),
        ],
        compiler_params=pltpu.CompilerParams(dimension_semantics=("parallel", "arbitrary")),
        name="rwkv7_time_mix",
    )(z_rw, row(mu), row(w0), w2, row(a0), a2, g2, row(k_k), row(k_a), row(r_k), row(gn_g), row(gn_b), bd)


DSA_TQ = 256
MASK_NEG = -1e30
INT_MIN = -2 ** 31
KEY_NEG_INF = -2139095041
THRESH_BITS = 32


def _dsa_prep_kernel(z_ref, kvg_ref, kig_ref, kib_ref, qi_ref, kv_ref, ki_ref):
    z = z_ref[0]
    qi_ref[0] = z[:, :IDX_HEADS * IDX_DIM].astype(qi_ref.dtype)
    kv = z[:, 256:384]
    ms = jnp.mean(jnp.square(kv), -1, keepdims=True)
    kv_ref[0] = (kv * lax.rsqrt(ms + LN_EPS) * kvg_ref[...]).astype(kv_ref.dtype)
    ki = z[:, 384:448]
    mu = jnp.mean(ki, -1, keepdims=True)
    var = jnp.mean(jnp.square(ki - mu), -1, keepdims=True)
    ki_ref[0] = ((ki - mu) * lax.rsqrt(var + LN_EPS) * kig_ref[...] + kib_ref[...]).astype(ki_ref.dtype)


def _dsa_prep(z_small, kv_g, ki_g, ki_b, tm=512):
    B, S, W = z_small.shape
    return pl.pallas_call(
        _dsa_prep_kernel,
        grid=(B, S // tm),
        in_specs=[
            pl.BlockSpec((1, tm, W), lambda b, i: (b, i, 0)),
            pl.BlockSpec((1, DSA_LATENT), lambda b, i: (0, 0)),
            pl.BlockSpec((1, IDX_DIM), lambda b, i: (0, 0)),
            pl.BlockSpec((1, IDX_DIM), lambda b, i: (0, 0)),
        ],
        out_specs=[
            pl.BlockSpec((1, tm, IDX_HEADS * IDX_DIM), lambda b, i: (b, i, 0)),
            pl.BlockSpec((1, tm, DSA_LATENT), lambda b, i: (b, i, 0)),
            pl.BlockSpec((1, tm, IDX_DIM), lambda b, i: (b, i, 0)),
        ],
        out_shape=[
            jax.ShapeDtypeStruct((B, S, IDX_HEADS * IDX_DIM), MM_DTYPE),
            jax.ShapeDtypeStruct((B, S, DSA_LATENT), MM_DTYPE),
            jax.ShapeDtypeStruct((B, S, IDX_DIM), MM_DTYPE),
        ],
        name="dsa_prep",
    )(z_small, kv_g[None], ki_g[None], ki_b[None])


def _sortable_key(s):
    s = jnp.where(s == 0.0, 0.0, s)
    bits = pltpu.bitcast(s, jnp.int32)
    return bits ^ ((bits >> 31) & 0x7FFFFFFF)


def _col_count(mask_i32):
    tk, tq = mask_i32.shape
    return jnp.sum(mask_i32.reshape(tk // 8, 8, tq), axis=0)


def _dsa_kernel(bfar_ref, q_ref, qi_ref, wit_ref, kv_ref, kvt_ref, ki_ref, b0_ref, b1_ref, o_ref,
                key_ref, madd_ref, m_ref, l_ref, acc_ref, *, topk):
    tq = q_ref.shape[1]
    tk = tq
    i = pl.program_id(1)
    nj = i + 1
    f32 = jnp.float32
    krow = lax.broadcasted_iota(jnp.int32, (tk, tq), 0)
    qcol = lax.broadcasted_iota(jnp.int32, (tk, tq), 1)

    qi = qi_ref[0]
    wit = wit_ref[0] * (IDX_HEADS ** -0.5)

    def score_chunk(j, carry):
        off = pl.multiple_of(j * tk, tk)
        kc = ki_ref[0, pl.ds(off, tk), :]
        s = jnp.zeros((tk, tq), f32)
        for h in range(IDX_HEADS):
            d = lax.dot_general(kc, qi[:, h * IDX_DIM:(h + 1) * IDX_DIM],
                                (((1,), (1,)), ((), ())), preferred_element_type=f32)
            s = s + wit[h:h + 1, :] * jnp.maximum(d * (IDX_DIM ** -0.5), 0.0)
        causal = (krow + j * tk) <= (qcol + i * tq)
        s = jnp.where(causal, s, -jnp.inf)
        key_ref[j] = _sortable_key(s)
        return carry

    lax.fori_loop(0, nj, score_chunk, 0)

    def count_where(pred_fn):
        def body(j, acc):
            return acc + _col_count(pred_fn(key_ref[j], j).astype(jnp.int32))
        acc = lax.fori_loop(0, nj, body, jnp.zeros((8, tq), jnp.int32))
        return jnp.sum(acc, axis=0, keepdims=True)

    def bit_step(it, t_u):
        bit = THRESH_BITS - 1 - it
        cand_u = t_u | jnp.left_shift(jnp.int32(1), bit)
        cand = cand_u ^ INT_MIN
        cnt = count_where(lambda k, j: k >= cand)
        return jnp.where(cnt >= topk, cand_u, t_u)

    t_u = lax.fori_loop(0, THRESH_BITS, bit_step, jnp.zeros((1, tq), jnp.int32))
    thr = t_u ^ INT_MIN
    cnt_gt = count_where(lambda k, j: k > thr)
    cnt_ge = count_where(lambda k, j: k >= thr)
    is_neg = thr == KEY_NEG_INF
    need = jnp.logical_and(cnt_ge > topk, jnp.logical_not(is_neg))
    n_tie_take = topk - cnt_gt

    thr_open = jnp.where(is_neg, thr, thr - 1)
    any_need = jnp.max(need.astype(jnp.int32)) > 0

    @pl.when(jnp.logical_not(any_need))
    def _():
        def body(j, carry):
            madd_ref[j] = jnp.where(key_ref[j] > thr_open, 0.0, MASK_NEG)
            return carry
        lax.fori_loop(0, nj, body, 0)

    @pl.when(any_need)
    def _():
        s_len = tk * key_ref.shape[0]
        n_bits = max(1, int(math.ceil(math.log2(s_len))))

        def idx_step(it, p):
            bit = n_bits - 1 - it
            cand = p | jnp.left_shift(jnp.int32(1), bit)
            cnt = count_where(
                lambda k, j: jnp.where(k == thr, jnp.where((krow + j * tk) < cand, 1, 0), 0))
            return jnp.where(cnt < n_tie_take, cand, p)

        p_idx = lax.fori_loop(0, n_bits, idx_step, jnp.zeros((1, tq), jnp.int32))
        p_idx = jnp.where(need, p_idx, jnp.where(is_neg, -1, s_len))

        def body(j, carry):
            k = key_ref[j]
            tie_ok = jnp.where((krow + j * tk) <= p_idx, 0.0, MASK_NEG)
            madd_ref[j] = jnp.where(k > thr, 0.0, jnp.where(k == thr, tie_ok, MASK_NEG))
            return carry
        lax.fori_loop(0, nj, body, 0)

    m_ref[...] = jnp.full(m_ref.shape, MASK_NEG, f32)
    l_ref[...] = jnp.zeros(l_ref.shape, f32)
    acc_ref[...] = jnp.zeros(acc_ref.shape, f32)
    scale = DSA_LATENT ** -0.5

    def attend(j, bias_fn):
        off = pl.multiple_of(j * tk, tk)
        kc = kv_ref[0, pl.ds(off, tk), :]
        kct = kvt_ref[0, :, pl.ds(off, tk)]
        madd = madd_ref[j]
        for h in range(DSA_HEADS):
            qh = q_ref[0, :, h * DSA_LATENT:(h + 1) * DSA_LATENT]
            lg = lax.dot_general(kc, qh, (((1,), (1,)), ((), ())), preferred_element_type=f32)
            lg = lg * scale + bias_fn(h) + madd
            m_old = m_ref[h]
            m_new = jnp.maximum(m_old, jnp.max(lg, axis=0, keepdims=True))
            alpha = jnp.exp(m_old - m_new)
            p = jnp.exp(lg - m_new)
            l_ref[h] = alpha * l_ref[h] + jnp.sum(p, axis=0, keepdims=True)
            acc_ref[h] = alpha * acc_ref[h] + jnp.dot(kct, p.astype(kct.dtype),
                                                      preferred_element_type=f32)
            m_ref[h] = m_new

    def far_body(j, carry):
        attend(j, lambda h: bfar_ref[h])
        return carry

    lax.fori_loop(0, jnp.maximum(i - 1, 0), far_body, 0)

    @pl.when(i >= 1)
    def _():
        attend(i - 1, lambda h: b1_ref[h])

    attend(i, lambda h: b0_ref[h])

    for h in range(DSA_HEADS):
        out_t = acc_ref[h] / l_ref[h]
        o_ref[0, :, h * DSA_LATENT:(h + 1) * DSA_LATENT] = out_t.T.astype(o_ref.dtype)


def _dsa_bias_tiles(rel_bias, tq):
    dist = jnp.arange(2 * tq, dtype=jnp.int32)
    by_dist = rel_bias[_t5_bucket(dist)].T
    kk = jnp.arange(tq, dtype=jnp.int32)[:, None]
    qq = jnp.arange(tq, dtype=jnp.int32)[None, :]
    b0 = by_dist[:, jnp.maximum(qq - kk, 0)]
    b1 = by_dist[:, tq + qq - kk]
    return b0, b1, by_dist[:, -1]


def _dsa_attention(q, z_small, z_wi_t, kv_g, idx_k_g, idx_k_b, rel_bias, out_dtype):
    B, S, _ = q.shape
    tq = DSA_TQ
    topk = min(TOPK_MAX, S // 4)
    assert S % tq == 0 and tq >= REL_MAX_DIST and topk <= tq
    qi, ckv, kidx = _dsa_prep(z_small, kv_g, idx_k_g, idx_k_b)
    ckv_t = jnp.swapaxes(ckv, 1, 2)
    b0, b1, bfar = _dsa_bias_tiles(rel_bias, tq)
    nq = S // tq
    H, d = DSA_HEADS, DSA_LATENT
    grid_spec = pltpu.PrefetchScalarGridSpec(
        num_scalar_prefetch=0,
        grid=(B, nq),
        in_specs=[
            pl.BlockSpec(memory_space=pltpu.SMEM),
            pl.BlockSpec((1, tq, H * d), lambda b, i: (b, i, 0)),
            pl.BlockSpec((1, tq, IDX_HEADS * IDX_DIM), lambda b, i: (b, i, 0)),
            pl.BlockSpec((1, IDX_HEADS, tq), lambda b, i: (b, 0, i)),
            pl.BlockSpec((1, S, d), lambda b, i: (b, 0, 0)),
            pl.BlockSpec((1, d, S), lambda b, i: (b, 0, 0)),
            pl.BlockSpec((1, S, IDX_DIM), lambda b, i: (b, 0, 0)),
            pl.BlockSpec((H, tq, tq), lambda b, i: (0, 0, 0)),
            pl.BlockSpec((H, tq, tq), lambda b, i: (0, 0, 0)),
        ],
        out_specs=pl.BlockSpec((1, tq, H * d), lambda b, i: (b, i, 0)),
        scratch_shapes=[
            pltpu.VMEM((nq, tq, tq), jnp.int32),
            pltpu.VMEM((nq, tq, tq), jnp.float32),
            pltpu.VMEM((H, 1, tq), jnp.float32),
            pltpu.VMEM((H, 1, tq), jnp.float32),
            pltpu.VMEM((H, d, tq), jnp.float32),
        ],
    )
    return pl.pallas_call(
        partial(_dsa_kernel, topk=topk),
        grid_spec=grid_spec,
        out_shape=jax.ShapeDtypeStruct((B, S, H * d), out_dtype),
        compiler_params=pltpu.CompilerParams(vmem_limit_bytes=48 * 1024 * 1024),
        name="dsa_attention",
    )(bfar, q, qi, z_wi_t, ckv, ckv_t, kidx, b0, b1)


SC_CORES = 2
SC_SUBCORES = 16
SC_LANES = 16
PEER_SC_TOKENS = 8
PEER_SC_RING = 4
PEER_SC_UNROLL = 4
PEER_SLOTS = PEER_HEADS * PEER_TOPK


def _peer_sc_call(body, T, out_width, stage_width):
    mesh = plsc.VectorSubcoreMesh(core_axis_name="c", subcore_axis_name="s")
    return pl.kernel(
        body, mesh=mesh,
        out_type=jax.ShapeDtypeStruct((T, out_width), jnp.float32),
        scratch_types=[
            pltpu.VMEM((PEER_SC_TOKENS, PEER_SLOTS), jnp.int32),
            pltpu.VMEM((PEER_SC_TOKENS, stage_width), jnp.float32),
            pltpu.VMEM((PEER_SC_RING, PEER_TOPK, D_MODEL), jnp.float32),
            pltpu.VMEM((PEER_SC_TOKENS, out_width), jnp.float32),
            pltpu.SemaphoreType.DMA((PEER_SC_RING,)),
        ],
        compiler_params=pltpu.CompilerParams(needs_layout_passes=False),
    )


def _peer_sc_body(compute, zero_out, x_hbm, idx_hbm, tab_hbm, out_hbm, idx_v, x_v, rows_v, out_v, sems):
    T = idx_hbm.shape[0]
    tokens_per_worker = T // (SC_CORES * SC_SUBCORES)
    n_steps = PEER_SC_TOKENS * PEER_HEADS
    worker = lax.axis_index("s") * SC_CORES + lax.axis_index("c")
    base = worker * tokens_per_worker

    def gather(s, b):
        ids = idx_v[s // PEER_HEADS, pl.ds((s % PEER_HEADS) * PEER_TOPK, PEER_TOPK)]
        return pltpu.make_async_copy(tab_hbm.at[ids], rows_v.at[b], sems.at[b])

    @pl.loop(0, tokens_per_worker // PEER_SC_TOKENS)
    def _(blk):
        tok0 = base + blk * PEER_SC_TOKENS
        pltpu.sync_copy(idx_hbm.at[pl.ds(tok0, PEER_SC_TOKENS)], idx_v)
        pltpu.sync_copy(x_hbm.at[pl.ds(tok0, PEER_SC_TOKENS)], x_v)
        for b in range(PEER_SC_RING - 1):
            gather(b, b).start()
        if zero_out:
            @pl.loop(0, PEER_SC_TOKENS)
            def _(t):
                @plsc.parallel_loop(0, out_v.shape[1] // SC_LANES, unroll=PEER_SC_UNROLL)
                def _(c):
                    out_v[t, pl.ds(pl.multiple_of(c * SC_LANES, SC_LANES), SC_LANES)] = (
                        jnp.zeros((SC_LANES,), jnp.float32))

        @pl.loop(0, n_steps, step=PEER_SC_RING)
        def _(s0):
            for b in range(PEER_SC_RING):
                s = s0 + b
                gather(s, b).wait()

                @pl.when(s + PEER_SC_RING - 1 < n_steps)
                def _():
                    gather(s + PEER_SC_RING - 1, (b + PEER_SC_RING - 1) % PEER_SC_RING).start()

                compute(s // PEER_HEADS, s % PEER_HEADS, b, x_v, rows_v, out_v)

        pltpu.sync_copy(out_v, out_hbm.at[pl.ds(tok0, PEER_SC_TOKENS)])


def _peer_dots_compute(t, hd, b, h_v, rows_v, dots_v):
    lane = lax.iota(jnp.int32, SC_LANES)

    def col_step(c, accs):
        off = pl.multiple_of(c * SC_LANES, SC_LANES)
        hv = h_v[t, pl.ds(off, SC_LANES)]
        return tuple(accs[r] + rows_v[b, r, pl.ds(off, SC_LANES)] * hv for r in range(PEER_TOPK))

    accs = plsc.parallel_loop(
        0, D_MODEL // SC_LANES, unroll=PEER_SC_UNROLL,
        carry=tuple(jnp.zeros((SC_LANES,), jnp.float32) for _ in range(PEER_TOPK)))(col_step)
    res = jnp.zeros((SC_LANES,), jnp.float32)
    for r in range(PEER_TOPK):
        res = jnp.where(lane == r, jnp.sum(accs[r]), res)
    dots_v[t, pl.ds(hd * PEER_TOPK, PEER_TOPK)] = res


def _peer_mix_compute(t, hd, b, act_v, rows_v, out_v):
    tvec = jnp.full((SC_LANES,), t, jnp.int32)
    weights = [plsc.load_gather(act_v, [tvec, jnp.full((SC_LANES,), hd * PEER_TOPK + r, jnp.int32)])
               for r in range(PEER_TOPK)]

    @plsc.parallel_loop(0, D_MODEL // SC_LANES, unroll=PEER_SC_UNROLL)
    def _(c):
        off = pl.multiple_of(c * SC_LANES, SC_LANES)
        a = rows_v[b, 0, pl.ds(off, SC_LANES)] * weights[0]
        for r in range(1, PEER_TOPK):
            a = a + rows_v[b, r, pl.ds(off, SC_LANES)] * weights[r]
        plsc.addupdate(out_v.at[t, pl.ds(off, SC_LANES)], a)


def _peer_expert_dots(h, experts, u_tab):
    T = h.shape[0]
    assert T % (SC_CORES * SC_SUBCORES * PEER_SC_TOKENS) == 0
    assert (PEER_SC_TOKENS * PEER_HEADS) % PEER_SC_RING == 0 and PEER_TOPK == SC_LANES
    body = partial(_peer_sc_body, _peer_dots_compute, False)
    return _peer_sc_call(body, T, PEER_SLOTS, D_MODEL)(h, experts, u_tab)


def _peer_expert_mix(act, experts, v_tab):
    T = act.shape[0]
    assert T % (SC_CORES * SC_SUBCORES * PEER_SC_TOKENS) == 0
    body = partial(_peer_sc_body, _peer_mix_compute, True)
    return _peer_sc_call(body, T, D_MODEL, PEER_SLOTS)(act, experts, v_tab)


PEER_TM = 256


def _extract_topk(s, k, payload=None):
    R = s.shape[0]
    riota = lax.broadcasted_iota(jnp.int32, s.shape, 0)
    vals, rows = [], []
    for _ in range(k):
        m = jnp.max(s, axis=0, keepdims=True)
        pos = jnp.min(jnp.where(s == m, riota, R), axis=0, keepdims=True)
        hit = riota == pos
        vals.append(m)
        rows.append(pos if payload is None else jnp.max(jnp.where(hit, payload, -1), axis=0, keepdims=True))
        s = jnp.where(hit, -jnp.inf, s)
    return jnp.concatenate(vals, axis=0), jnp.concatenate(rows, axis=0)


def _peer_route_kernel(x_ref, sc_ref, sh_ref, wq_ref, keys_ref, h_ref, ex_ref, gate_ref, q_ref, ext_ref, gt_ref):
    K = PEER_TOPK
    f32 = jnp.float32
    h = x_ref[0] * (1.0 + sc_ref[0]) + sh_ref[0]
    h_ref[0] = h
    q = jnp.dot(h.astype(wq_ref.dtype), wq_ref[...], preferred_element_type=f32)
    for hd in range(PEER_HEADS):
        q_ref[hd] = q[:, hd * PEER_KEY_DIM:(hd + 1) * PEER_KEY_DIM].astype(q_ref.dtype)

    def head_body(hd, carry):
        qh = q_ref[hd]
        s1 = lax.dot_general(keys_ref[hd, 0], qh[:, :PEER_HALF], _NT, preferred_element_type=f32)
        s2 = lax.dot_general(keys_ref[hd, 1], qh[:, PEER_HALF:], _NT, preferred_element_type=f32)
        v1, i1 = _extract_topk(s1, K)
        v2, i2 = _extract_topk(s2, K)
        tm = v1.shape[1]
        cand_rows, cidx_rows = [], []
        for a in range(K):
            nb = K // (a + 1)
            cand_rows.append(v1[a:a + 1, :] + v2[:nb, :])
            cidx_rows.append(i1[a:a + 1, :] * PEER_N_KEYS + i2[:nb, :])
        n_pad = -sum(r.shape[0] for r in cand_rows) % 8
        cand_rows.append(jnp.full((n_pad, tm), -jnp.inf, f32))
        cidx_rows.append(jnp.full((n_pad, tm), -1, jnp.int32))
        cand = jnp.concatenate(cand_rows, axis=0)
        cidx = jnp.concatenate(cidx_rows, axis=0)
        top_s, experts = _extract_topk(cand, K, payload=cidx)
        e = jnp.exp(top_s - top_s[0:1, :])
        gt_ref[pl.ds(pl.multiple_of(hd * K, K), K), :] = e / jnp.sum(e, axis=0, keepdims=True)
        ext_ref[pl.ds(pl.multiple_of(hd * K, K), K), :] = experts
        return carry

    lax.fori_loop(0, PEER_HEADS, head_body, 0)
    ex_ref[0] = ext_ref[...].T
    gate_ref[0] = gt_ref[...].T


def _peer_route(x, sc, sh, w_pq, sub_keys):
    B, S, D = x.shape
    tm = PEER_TM
    return pl.pallas_call(
        _peer_route_kernel,
        grid=(B, S // tm),
        in_specs=[
            pl.BlockSpec((1, tm, D), lambda b, i: (b, i, 0)),
            pl.BlockSpec((1, 1, D), lambda b, i: (b, 0, 0)),
            pl.BlockSpec((1, 1, D), lambda b, i: (b, 0, 0)),
            pl.BlockSpec((D, PEER_HEADS * PEER_KEY_DIM), lambda b, i: (0, 0)),
            pl.BlockSpec((PEER_HEADS, 2, PEER_N_KEYS, PEER_HALF), lambda b, i: (0, 0, 0, 0)),
        ],
        out_specs=[
            pl.BlockSpec((1, tm, D), lambda b, i: (b, i, 0)),
            pl.BlockSpec((1, tm, PEER_SLOTS), lambda b, i: (b, i, 0)),
            pl.BlockSpec((1, tm, PEER_SLOTS), lambda b, i: (b, i, 0)),
        ],
        out_shape=[
            jax.ShapeDtypeStruct((B, S, D), jnp.float32),
            jax.ShapeDtypeStruct((B, S, PEER_SLOTS), jnp.int32),
            jax.ShapeDtypeStruct((B, S, PEER_SLOTS), jnp.float32),
        ],
        scratch_shapes=[
            pltpu.VMEM((PEER_HEADS, tm, PEER_KEY_DIM), MM_DTYPE),
            pltpu.VMEM((PEER_SLOTS, tm), jnp.int32),
            pltpu.VMEM((PEER_SLOTS, tm), jnp.float32),
        ],
        compiler_params=pltpu.CompilerParams(dimension_semantics=("parallel", "parallel")),
        name="peer_route",
    )(x, sc[:, None, :], sh[:, None, :], w_pq.astype(MM_DTYPE), sub_keys.astype(MM_DTYPE))


def _peer_pre(h2, experts, u_tab):
    B, S, D = h2.shape
    return _peer_expert_dots(h2.reshape(B * S, D), experts.reshape(B * S, PEER_SLOTS), u_tab)


def _peer_act(pre, gates):
    return jax.nn.gelu(pre, approximate=False) * gates.reshape(pre.shape)


def _peer_out(act, experts, v_tab):
    B, S, _ = experts.shape
    return _peer_expert_mix(act, experts.reshape(B * S, PEER_SLOTS), v_tab).reshape(B, S, D_MODEL)


def kernel(x, c, w_ada, b_ada, w_in, rw_mu, rw_w0, rw_w2, rw_a0, rw_a2, rw_g2, rw_k_k, rw_k_a, rw_r_k, rw_gn_g, rw_gn_b, dsa_kv_g, idx_k_g, idx_k_b, rel_bias, w_br_a, w_br_b, w_out, ln1_g, ln1_b, peer_wq, peer_keys, peer_u, peer_v, ln2_g, ln2_b):
    l = 0
    mod = jax.nn.silu(c) @ w_ada[l] + b_ada[l]

    w_rw, w_q, w_kv, w_qi, w_ki, w_wi, w_ga, w_gb = jnp.split(w_in[l], _split_points(IN_SIZES), axis=-1)
    small_pad = jnp.zeros((D_MODEL, 512 - 256 - 128 - 64 - 4), w_in.dtype)
    w_small = jnp.concatenate([w_qi, w_kv, w_ki, w_wi, small_pad], axis=-1)
    w_gates = jnp.concatenate([w_ga, w_gb], axis=-1)

    def project_and_dsa(x, sc1, sh1):
        z_rw = _mod_matmul(x, sc1, sh1, w_rw, jnp.float32, tn=896)
        z_q = _mod_matmul(x, sc1, sh1, w_q, MM_DTYPE, tn=1024)
        z_small = _mod_matmul(x, sc1, sh1, w_small, jnp.float32, tn=512)
        z_g = _mod_matmul(x, sc1, sh1, w_gates, jnp.float32, tn=1024)
        z_wi_t = jnp.swapaxes(z_small[..., 448:448 + IDX_HEADS], 1, 2)
        y_b = _dsa_attention(z_q, z_small, z_wi_t, dsa_kv_g[l], idx_k_g[l], idx_k_b[l],
                             rel_bias, jnp.float32) @ w_br_b[l]
        return y_b, z_rw, z_g

    def rwkv_and_merge(x, gt1, y_b, z_rw, z_g):
        z_ga, z_gb = z_g[..., :D_MODEL], z_g[..., D_MODEL:]
        y_a = _rwkv7_time_mix(z_rw, rw_mu[l], rw_w0[l], rw_w2[l], rw_a0[l], rw_a2[l], rw_g2[l],
                              rw_k_k[l], rw_k_a[l], rw_r_k[l], rw_gn_g[l], rw_gn_b[l], jnp.float32) @ w_br_a[l]
        merged = jax.nn.sigmoid(z_ga) * y_a + jax.nn.sigmoid(z_gb) * y_b
        mix = merged @ w_out[l]
        return _layer_norm(DEEPNORM_ALPHA * x + gt1[:, None] * mix, ln1_g[l], ln1_b[l])

    def finish(st, act):
        y2 = _peer_out(act, st["experts"], peer_v[l])
        return _layer_norm(DEEPNORM_ALPHA * st["x1"] + st["gt2"][:, None] * y2, ln2_g[l], ln2_b[l])

    gsz = x.shape[0] // BATCH_GROUPS
    outs = []
    prev = None
    for g in range(BATCH_GROUPS):
        sh1, sc1, gt1, sh2, sc2, gt2 = jnp.split(mod[g * gsz:(g + 1) * gsz], 6, axis=-1)
        x_g = x[g * gsz:(g + 1) * gsz]
        if prev is not None:
            x_g, prev["experts"] = lax.optimization_barrier((x_g, prev["experts"]))
        proj = project_and_dsa(x_g, sc1, sh1)
        if prev is not None:
            proj, prev["pre"], outs = lax.optimization_barrier((proj, prev["pre"], outs))
            outs.append(finish(prev, _peer_act(prev["pre"], prev["gates"])))
        x1 = rwkv_and_merge(x_g, gt1, *proj)
        h2, experts, gates = _peer_route(x1, sc2, sh2, peer_wq[l], peer_keys[l])
        prev = dict(x1=x1, gt2=gt2, experts=experts, gates=gates, pre=_peer_pre(h2, experts, peer_u[l]))
    outs.append(finish(prev, _peer_act(prev["pre"], prev["gates"])))
    return jnp.concatenate(outs, axis=0)
```

```python
import math
from functools import partial

import jax
import jax.numpy as jnp
import numpy as np
from jax import lax
from jax.experimental import pallas as pl
from jax.experimental.pallas import tpu as pltpu
from jax.experimental.pallas import tpu_sc as plsc

D_MODEL = 1024
RW_HEADS = 8
RW_HEAD_DIM = 64
RW_DIM = 512
RW_DECAY_LORA = 64
RW_A_LORA = 64
RW_GATE_LORA = 128
RW_COLS = 3 * RW_DIM + RW_DECAY_LORA + RW_A_LORA + RW_GATE_LORA
RW_GN_EPS = 64e-5
DSA_HEADS = 8
DSA_LATENT = 128
DSA_Q_DIM = DSA_HEADS * DSA_LATENT
IDX_HEADS = 4
IDX_DIM = 64
TOPK_MAX = 256
REL_BUCKETS = 32
REL_MAX_DIST = 128
IN_SIZES = (RW_COLS, DSA_Q_DIM, DSA_LATENT, IDX_HEADS * IDX_DIM, IDX_DIM, IDX_HEADS, D_MODEL, D_MODEL)
IN_COLS = sum(IN_SIZES)
PEER_HEADS = 8
PEER_N_KEYS = 128
PEER_KEY_DIM = 128
PEER_HALF = 64
PEER_TOPK = 16
LN_EPS = 1e-5
DEPTH = 1
DEEPNORM_ALPHA = (2.0 * DEPTH) ** 0.25

LANES = 128
SMALL_QI = 0
SMALL_KV = SMALL_QI + IDX_HEADS * IDX_DIM
SMALL_KI = SMALL_KV + DSA_LATENT
SMALL_WI = SMALL_KI + IDX_DIM
SMALL_COLS = 4 * LANES
MM_DTYPE = jnp.bfloat16
BATCH_GROUPS = 8


def _split_points(sizes):
    return np.cumsum(sizes)[:-1].tolist()


def _mod_matmul_kernel(x_ref, sc_ref, sh_ref, w_ref, o_ref):
    h = x_ref[0] * (1.0 + sc_ref[0]) + sh_ref[0]
    o_ref[0] = jnp.dot(h.astype(w_ref.dtype), w_ref[...],
                       preferred_element_type=jnp.float32).astype(o_ref.dtype)


def _mod_matmul(x, sc, sh, w, out_dtype, tn, tm=512):
    B, S, D = x.shape
    N = w.shape[1]
    return pl.pallas_call(
        _mod_matmul_kernel,
        grid=(B, S // tm, N // tn),
        in_specs=[
            pl.BlockSpec((1, tm, D), lambda b, i, j: (b, i, 0)),
            pl.BlockSpec((1, 1, D), lambda b, i, j: (b, 0, 0)),
            pl.BlockSpec((1, 1, D), lambda b, i, j: (b, 0, 0)),
            pl.BlockSpec((D, tn), lambda b, i, j: (0, j)),
        ],
        out_specs=pl.BlockSpec((1, tm, tn), lambda b, i, j: (b, i, j)),
        out_shape=jax.ShapeDtypeStruct((B, S, N), out_dtype),
        name="mod_matmul",
    )(x, sc[:, None, :], sh[:, None, :], w.astype(MM_DTYPE))


def _layer_norm(x, g, b):
    mu = jnp.mean(x, -1, keepdims=True)
    var = jnp.mean(jnp.square(x - mu), -1, keepdims=True)
    return (x - mu) * lax.rsqrt(var + LN_EPS) * g + b


def _t5_bucket(n):
    n = jnp.maximum(n, 0)
    max_exact = REL_BUCKETS // 2
    nf = jnp.maximum(n, 1).astype(jnp.float32)
    large = max_exact + (jnp.log(nf / max_exact) / math.log(REL_MAX_DIST / max_exact)
                         * (REL_BUCKETS - max_exact)).astype(jnp.int32)
    large = jnp.minimum(large, REL_BUCKETS - 1)
    return jnp.where(n < max_exact, n, large)


RW_CHUNK = 64
RW_INV_BLOCK = 16
_NN = (((1,), (0,)), ((), ()))
_NT = (((1,), (1,)), ((), ()))
_BNN = (((2,), (1,)), ((0,), (0,)))
_BNT = (((2,), (2,)), ((0,), (0,)))


def _dot_f32(a, b, dims=_NN):
    return lax.dot_general(a, b, dims, precision=lax.Precision.HIGHEST,
                           preferred_element_type=jnp.float32)


def _dot_bf16x3(a, b, dims=_NN):
    f32, bf = jnp.float32, jnp.bfloat16
    a_hi, b_hi = a.astype(bf), b.astype(bf)
    a_lo = (a - a_hi.astype(f32)).astype(bf)
    b_lo = (b - b_hi.astype(f32)).astype(bf)
    out = lax.dot_general(a_hi, b_hi, dims, preferred_element_type=f32)
    out = out + lax.dot_general(a_hi, b_lo, dims, preferred_element_type=f32)
    return out + lax.dot_general(a_lo, b_hi, dims, preferred_element_type=f32)


def _bf16_terms(a):
    f32, bf = jnp.float32, jnp.bfloat16
    hi = a.astype(bf)
    r1 = a - hi.astype(f32)
    mid = r1.astype(bf)
    lo = (r1 - mid.astype(f32)).astype(bf)
    return hi, mid, lo


def _dot_lhs_split(a, b01):
    b = b01.astype(jnp.bfloat16)
    return sum(jnp.dot(t, b, preferred_element_type=jnp.float32) for t in _bf16_terms(a))


def _dot_rhs_split(a01, b):
    a = a01.astype(jnp.bfloat16)
    return sum(jnp.dot(a, t, preferred_element_type=jnp.float32) for t in _bf16_terms(b))


def _rwkv_kernel(z_ref, mu_ref, w0_ref, w2_ref, a0_ref, a2_ref, g2_ref, kk_ref, ka_ref, rk_ref,
                 gng_ref, gnb_ref, bd_ref, o_ref, m_ref, prev_ref, y_ref):
    C = z_ref.shape[1]
    N = RW_HEAD_DIM
    f32 = jnp.float32
    dot3 = _dot_bf16x3

    @pl.when(pl.program_id(1) == 0)
    def _():
        m_ref[...] = jnp.zeros(m_ref.shape, f32)
        prev_ref[...] = jnp.zeros(prev_ref.shape, f32)

    z = z_ref[0]
    row = lax.broadcasted_iota(jnp.int32, z.shape, 0)
    shifted = jnp.where(row == 0, prev_ref[...], pltpu.roll(z, 1, axis=0))
    prev_ref[...] = z[C - 1:C, :]
    zs = z + (shifted - z) * mu_ref[...]
    r = zs[:, 0:RW_DIM]
    k = zs[:, RW_DIM:2 * RW_DIM]
    v = zs[:, 2 * RW_DIM:3 * RW_DIM]
    o1 = 3 * RW_DIM
    wl = zs[:, o1:o1 + RW_DECAY_LORA]
    al = zs[:, o1 + RW_DECAY_LORA:o1 + RW_DECAY_LORA + RW_A_LORA]
    gl = zs[:, o1 + RW_DECAY_LORA + RW_A_LORA:]

    bd = bd_ref[...]
    log_w = -jax.nn.softplus(-(w0_ref[...] + dot3(jnp.tanh(wl), w2_ref[...]))) - 0.5
    ldec = -jnp.exp(log_w)
    a_lr = jax.nn.sigmoid(a0_ref[...] + dot3(al, a2_ref[...]))
    g = dot3(jax.nn.sigmoid(gl), g2_ref[...])
    kk = k * kk_ref[...]
    kk = kk * lax.rsqrt(jnp.maximum(_dot_lhs_split(kk * kk, bd), 1e-24))
    k2 = k * (1.0 + (a_lr - 1.0) * ka_ref[...])
    a_vec = -kk
    b_vec = kk * a_lr

    ti = lax.broadcasted_iota(jnp.int32, (C, C), 0)
    tj = lax.broadcasted_iota(jnp.int32, (C, C), 1)
    cum = _dot_rhs_split((ti >= tj).astype(f32), ldec)
    cum_last = cum[C - 1:C, :]
    w_incl = jnp.exp(cum)
    w_excl = jnp.exp(cum - ldec)
    w_inv = jnp.exp(-cum)
    w_end = jnp.exp(cum_last - cum)
    w_all = jnp.exp(cum_last)
    a_t = a_vec * w_excl
    r_t = r * w_incl
    b_t = b_vec * w_inv
    k_t = k2 * w_inv
    b_e = b_vec * w_end
    k_e = k2 * w_end

    strict = ti > tj
    incl = ti >= tj
    bi, bj = ti // RW_INV_BLOCK, tj // RW_INV_BLOCK
    same_blk = bi == bj
    pair_blk = jnp.logical_and((bi // 2) == (bj // 2), jnp.logical_not(same_blk))
    half_blk = (bi // 2) != (bj // 2)
    eye = (ti == tj).astype(f32)

    H = RW_HEADS
    heads = lambda x: jnp.stack([x[:, h * N:(h + 1) * N] for h in range(H)], axis=0)
    bmm = lambda a, b: dot3(a, b, _BNN)
    A, Rt, Bt, Kt, Be, Ke, V = (heads(t) for t in (a_t, r_t, b_t, k_t, b_e, k_e, v))
    gm = dot3(jnp.concatenate([A, Rt], axis=1), jnp.concatenate([Bt, Kt], axis=1), _BNT)
    a_ab = jnp.where(strict, gm[:, :C, :C], 0.0)
    a_ak = jnp.where(strict, gm[:, :C, C:], 0.0)
    a_rb = jnp.where(incl, gm[:, C:, :C], 0.0)
    a_rk = jnp.where(incl, gm[:, C:, C:], 0.0)
    d1 = jnp.where(same_blk, a_ab, 0.0)
    xinv = eye + d1
    d2 = bmm(d1, d1)
    xinv = xinv + bmm(xinv, d2)
    d4 = bmm(d2, d2)
    xinv = xinv + bmm(xinv, d4)
    d8 = bmm(d4, d4)
    xinv = xinv + bmm(xinv, d8)
    xinv = xinv + bmm(bmm(xinv, jnp.where(pair_blk, a_ab, 0.0)), xinv)
    xinv = xinv + bmm(bmm(xinv, jnp.where(half_blk, a_ab, 0.0)), xinv)

    av = bmm(jnp.concatenate([a_ak, a_rk], axis=1), V)
    p = bmm(xinv, jnp.concatenate([A, av[:, :C]], axis=2))
    qm = bmm(a_rb, p)
    q1 = Rt + qm[:, :, :N]
    q2 = qm[:, :, N:] + av[:, C:]
    gmat = bmm(jnp.swapaxes(Be, 1, 2), p)
    g1 = eye * heads(w_all) + gmat[:, :, :N]
    g2 = gmat[:, :, N:] + bmm(jnp.swapaxes(Ke, 1, 2), V)
    m = m_ref[...]
    yh = _dot_f32(q1, m, _BNN) + q2
    m_ref[...] = _dot_f32(g1, m, _BNN) + g2
    for h in range(H):
        y_ref[:, h * N:(h + 1) * N] = yh[h]

    y = y_ref[...]
    mean = _dot_lhs_split(y, bd) * (1.0 / N)
    yc = y - mean
    var = _dot_lhs_split(yc * yc, bd) * (1.0 / N)
    yn = yc * lax.rsqrt(var + RW_GN_EPS) * gng_ref[...] + gnb_ref[...]
    bonus = _dot_lhs_split(r * k2 * rk_ref[...], bd) * v
    o_ref[0] = ((yn + bonus) * g).astype(o_ref.dtype)


def _rwkv7_time_mix(z_rw, mu, w0, w2, a0, a2, g2, k_k, k_a, r_k, gn_g, gn_b, out_dtype):
    B, S, _ = z_rw.shape
    C = RW_CHUNK
    assert S % C == 0 and C == RW_HEAD_DIM and C % (4 * RW_INV_BLOCK) == 0
    hid = jnp.arange(RW_DIM) // RW_HEAD_DIM
    bd = (hid[:, None] == hid[None, :]).astype(jnp.float32)
    row = lambda a: a.reshape(1, -1)
    full = lambda shape: pl.BlockSpec(shape, lambda b, c: (0,) * len(shape))
    return pl.pallas_call(
        _rwkv_kernel,
        grid=(B, S // C),
        in_specs=[
            pl.BlockSpec((1, C, RW_COLS), lambda b, c: (b, c, 0)),
            full((1, RW_COLS)), full((1, RW_DIM)), full((RW_DECAY_LORA, RW_DIM)), full((1, RW_DIM)),
            full((RW_A_LORA, RW_DIM)), full((RW_GATE_LORA, RW_DIM)), full((1, RW_DIM)), full((1, RW_DIM)),
            full((1, RW_DIM)), full((1, RW_DIM)), full((1, RW_DIM)), full((RW_DIM, RW_DIM)),
        ],
        out_specs=pl.BlockSpec((1, C, RW_DIM), lambda b, c: (b, c, 0)),
        out_shape=jax.ShapeDtypeStruct((B, S, RW_DIM), out_dtype),
        scratch_shapes=[
            pltpu.VMEM((RW_HEADS, RW_HEAD_DIM, RW_HEAD_DIM), jnp.float32),
            pltpu.VMEM((1, RW_COLS), jnp.float32),
            pltpu.VMEM((C, RW_DIM), jnp.float32),
        ],
        compiler_params=pltpu.CompilerParams(dimension_semantics=("parallel", "arbitrary")),
        name="rwkv7_time_mix",
    )(z_rw, row(mu), row(w0), w2, row(a0), a2, g2, row(k_k), row(k_a), row(r_k), row(gn_g), row(gn_b), bd)


DSA_TQ = 256
MASK_NEG = -1e30
INT_MIN = -2 ** 31
KEY_NEG_INF = -2139095041
THRESH_BITS = 32


def _dsa_prep_kernel(z_ref, kvg_ref, kig_ref, kib_ref, qi_ref, kv_ref, ki_ref):
    z = z_ref[0]
    qi_ref[0] = z[:, SMALL_QI:SMALL_KV].astype(qi_ref.dtype)
    kv = z[:, SMALL_KV:SMALL_KI]
    ms = jnp.mean(jnp.square(kv), -1, keepdims=True)
    kv_ref[0] = (kv * lax.rsqrt(ms + LN_EPS) * kvg_ref[...]).astype(kv_ref.dtype)
    ki = z[:, SMALL_KI:SMALL_WI]
    mu = jnp.mean(ki, -1, keepdims=True)
    var = jnp.mean(jnp.square(ki - mu), -1, keepdims=True)
    ki_ref[0] = ((ki - mu) * lax.rsqrt(var + LN_EPS) * kig_ref[...] + kib_ref[...]).astype(ki_ref.dtype)


def _dsa_prep(z_small, kv_g, ki_g, ki_b, tm=512):
    B, S, W = z_small.shape
    return pl.pallas_call(
        _dsa_prep_kernel,
        grid=(B, S // tm),
        in_specs=[
            pl.BlockSpec((1, tm, W), lambda b, i: (b, i, 0)),
            pl.BlockSpec((1, DSA_LATENT), lambda b, i: (0, 0)),
            pl.BlockSpec((1, IDX_DIM), lambda b, i: (0, 0)),
            pl.BlockSpec((1, IDX_DIM), lambda b, i: (0, 0)),
        ],
        out_specs=[
            pl.BlockSpec((1, tm, IDX_HEADS * IDX_DIM), lambda b, i: (b, i, 0)),
            pl.BlockSpec((1, tm, DSA_LATENT), lambda b, i: (b, i, 0)),
            pl.BlockSpec((1, tm, IDX_DIM), lambda b, i: (b, i, 0)),
        ],
        out_shape=[
            jax.ShapeDtypeStruct((B, S, IDX_HEADS * IDX_DIM), MM_DTYPE),
            jax.ShapeDtypeStruct((B, S, DSA_LATENT), MM_DTYPE),
            jax.ShapeDtypeStruct((B, S, IDX_DIM), MM_DTYPE),
        ],
        name="dsa_prep",
    )(z_small, kv_g[None], ki_g[None], ki_b[None])


def _sortable_key(s):
    s = jnp.where(s == 0.0, 0.0, s)
    bits = pltpu.bitcast(s, jnp.int32)
    return bits ^ ((bits >> 31) & 0x7FFFFFFF)


def _col_count(mask_i32):
    tk, tq = mask_i32.shape
    return jnp.sum(mask_i32.reshape(tk // 8, 8, tq), axis=0)


def _dsa_kernel(bfar_ref, q_ref, qi_ref, wit_ref, kv_ref, kvt_ref, ki_ref, b0_ref, b1_ref, o_ref,
                key_ref, madd_ref, m_ref, l_ref, acc_ref, *, topk):
    tq = q_ref.shape[1]
    tk = tq
    i = pl.program_id(1)
    nj = i + 1
    f32 = jnp.float32
    krow = lax.broadcasted_iota(jnp.int32, (tk, tq), 0)
    qcol = lax.broadcasted_iota(jnp.int32, (tk, tq), 1)

    qi = qi_ref[0]
    wit = wit_ref[0] * (IDX_HEADS ** -0.5)

    def score_chunk(j, carry):
        off = pl.multiple_of(j * tk, tk)
        kc = ki_ref[0, pl.ds(off, tk), :]
        s = jnp.zeros((tk, tq), f32)
        for h in range(IDX_HEADS):
            d = lax.dot_general(kc, qi[:, h * IDX_DIM:(h + 1) * IDX_DIM],
                                (((1,), (1,)), ((), ())), preferred_element_type=f32)
            s = s + wit[h:h + 1, :] * jnp.maximum(d * (IDX_DIM ** -0.5), 0.0)
        causal = (krow + j * tk) <= (qcol + i * tq)
        s = jnp.where(causal, s, -jnp.inf)
        key_ref[j] = _sortable_key(s)
        return carry

    lax.fori_loop(0, nj, score_chunk, 0)

    def count_where(pred_fn):
        def body(j, acc):
            return acc + _col_count(pred_fn(key_ref[j], j).astype(jnp.int32))
        acc = lax.fori_loop(0, nj, body, jnp.zeros((8, tq), jnp.int32))
        return jnp.sum(acc, axis=0, keepdims=True)

    def bit_step(it, t_u):
        bit = THRESH_BITS - 1 - it
        cand_u = t_u | jnp.left_shift(jnp.int32(1), bit)
        cand = cand_u ^ INT_MIN
        cnt = count_where(lambda k, j: k >= cand)
        return jnp.where(cnt >= topk, cand_u, t_u)

    t_u = lax.fori_loop(0, THRESH_BITS, bit_step, jnp.zeros((1, tq), jnp.int32))
    thr = t_u ^ INT_MIN
    cnt_gt = count_where(lambda k, j: k > thr)
    cnt_ge = count_where(lambda k, j: k >= thr)
    is_neg = thr == KEY_NEG_INF
    need = jnp.logical_and(cnt_ge > topk, jnp.logical_not(is_neg))
    n_tie_take = topk - cnt_gt

    thr_open = jnp.where(is_neg, thr, thr - 1)
    any_need = jnp.max(need.astype(jnp.int32)) > 0

    @pl.when(jnp.logical_not(any_need))
    def _():
        def body(j, carry):
            madd_ref[j] = jnp.where(key_ref[j] > thr_open, 0.0, MASK_NEG)
            return carry
        lax.fori_loop(0, nj, body, 0)

    @pl.when(any_need)
    def _():
        s_len = tk * key_ref.shape[0]
        n_bits = max(1, int(math.ceil(math.log2(s_len))))

        def idx_step(it, p):
            bit = n_bits - 1 - it
            cand = p | jnp.left_shift(jnp.int32(1), bit)
            cnt = count_where(
                lambda k, j: jnp.where(k == thr, jnp.where((krow + j * tk) < cand, 1, 0), 0))
            return jnp.where(cnt < n_tie_take, cand, p)

        p_idx = lax.fori_loop(0, n_bits, idx_step, jnp.zeros((1, tq), jnp.int32))
        p_idx = jnp.where(need, p_idx, jnp.where(is_neg, -1, s_len))

        def body(j, carry):
            k = key_ref[j]
            tie_ok = jnp.where((krow + j * tk) <= p_idx, 0.0, MASK_NEG)
            madd_ref[j] = jnp.where(k > thr, 0.0, jnp.where(k == thr, tie_ok, MASK_NEG))
            return carry
        lax.fori_loop(0, nj, body, 0)

    H, d = DSA_HEADS, DSA_LATENT
    m_ref[...] = jnp.full(m_ref.shape, MASK_NEG, f32)
    l_ref[...] = jnp.zeros(l_ref.shape, f32)
    acc_ref[...] = jnp.zeros(acc_ref.shape, f32)
    scale = DSA_LATENT ** -0.5
    q_all = jnp.concatenate([q_ref[0, :, h * d:(h + 1) * d] for h in range(H)], axis=0)

    def attend(j, bias):
        off = pl.multiple_of(j * tk, tk)
        kc = kv_ref[0, pl.ds(off, tk), :]
        kct = kvt_ref[0, :, pl.ds(off, tk)]
        lg = lax.dot_general(kc, q_all, (((1,), (1,)), ((), ())), preferred_element_type=f32)
        lg = lg * scale + bias + jnp.tile(madd_ref[j], (1, H))
        m_old = m_ref[...]
        m_new = jnp.maximum(m_old, jnp.max(lg, axis=0, keepdims=True))
        alpha = jnp.exp(m_old - m_new)
        p = jnp.exp(lg - m_new)
        l_ref[...] = alpha * l_ref[...] + jnp.sum(p, axis=0, keepdims=True)
        acc_ref[...] = alpha * acc_ref[...] + jnp.dot(kct, p.astype(kct.dtype), preferred_element_type=f32)
        m_ref[...] = m_new

    def far_body(j, carry):
        attend(j, bfar_ref[...])
        return carry

    lax.fori_loop(0, jnp.maximum(i - 1, 0), far_body, 0)

    @pl.when(i >= 1)
    def _():
        attend(i - 1, b1_ref[...])

    attend(i, b0_ref[...])

    out_t = acc_ref[...] / l_ref[...]
    for h in range(H):
        o_ref[0, :, h * d:(h + 1) * d] = out_t[:, h * tq:(h + 1) * tq].T.astype(o_ref.dtype)


def _dsa_bias_tiles(rel_bias, tq):
    dist = jnp.arange(2 * tq, dtype=jnp.int32)
    by_dist = rel_bias[_t5_bucket(dist)].T

    def toeplitz(v):
        H, P = v.shape
        skew = jnp.tile(v, (1, tq + 1))[:, :tq * (P + 1)].reshape(H, tq, P + 1)
        return skew[:, ::-1, :tq]

    b0 = toeplitz(jnp.concatenate([jnp.repeat(by_dist[:, :1], tq - 1, axis=1), by_dist[:, :tq + 1]], axis=1))
    b1 = toeplitz(jnp.concatenate([by_dist[:, 1:], by_dist[:, -1:]], axis=1))
    return b0, b1, by_dist[:, -1]


def _dsa_attention(q, z_small, z_wi_t, kv_g, idx_k_g, idx_k_b, rel_bias, out_dtype):
    B, S, _ = q.shape
    tq = DSA_TQ
    topk = min(TOPK_MAX, S // 4)
    assert S % tq == 0 and tq >= REL_MAX_DIST and topk <= tq
    qi, ckv, kidx = _dsa_prep(z_small, kv_g, idx_k_g, idx_k_b)
    ckv_t = jnp.swapaxes(ckv, 1, 2)
    b0, b1, bfar = _dsa_bias_tiles(rel_bias, tq)
    nq = S // tq
    H, d = DSA_HEADS, DSA_LATENT
    b0 = jnp.transpose(b0, (1, 0, 2)).reshape(tq, H * tq)
    b1 = jnp.transpose(b1, (1, 0, 2)).reshape(tq, H * tq)
    bfar = jnp.repeat(bfar, tq)[None, :]
    f32_bytes = 4
    vmem_estimate = (
        2 * nq * tq * tq * f32_bytes
        + 2 * 2 * tq * H * tq * f32_bytes
        + 2 * (2 * S * d + S * LANES) * 2
        + 6 * tq * H * tq * f32_bytes
        + 4 * tq * H * d * f32_bytes
    )
    grid_spec = pltpu.PrefetchScalarGridSpec(
        num_scalar_prefetch=0,
        grid=(B, nq),
        in_specs=[
            pl.BlockSpec((1, H * tq), lambda b, i: (0, 0)),
            pl.BlockSpec((1, tq, H * d), lambda b, i: (b, i, 0)),
            pl.BlockSpec((1, tq, IDX_HEADS * IDX_DIM), lambda b, i: (b, i, 0)),
            pl.BlockSpec((1, IDX_HEADS, tq), lambda b, i: (b, 0, i)),
            pl.BlockSpec((1, S, d), lambda b, i: (b, 0, 0)),
            pl.BlockSpec((1, d, S), lambda b, i: (b, 0, 0)),
            pl.BlockSpec((1, S, IDX_DIM), lambda b, i: (b, 0, 0)),
            pl.BlockSpec((tq, H * tq), lambda b, i: (0, 0)),
            pl.BlockSpec((tq, H * tq), lambda b, i: (0, 0)),
        ],
        out_specs=pl.BlockSpec((1, tq, H * d), lambda b, i: (b, i, 0)),
        scratch_shapes=[
            pltpu.VMEM((nq, tq, tq), jnp.int32),
            pltpu.VMEM((nq, tq, tq), jnp.float32),
            pltpu.VMEM((1, H * tq), jnp.float32),
            pltpu.VMEM((1, H * tq), jnp.float32),
            pltpu.VMEM((d, H * tq), jnp.float32),
        ],
    )
    return pl.pallas_call(
        partial(_dsa_kernel, topk=topk),
        grid_spec=grid_spec,
        out_shape=jax.ShapeDtypeStruct((B, S, H * d), out_dtype),
        compiler_params=pltpu.CompilerParams(vmem_limit_bytes=vmem_estimate),
        name="dsa_attention",
    )(bfar, q, qi, z_wi_t, ckv, ckv_t, kidx, b0, b1)


SC_CORES = 2
SC_SUBCORES = 16
SC_LANES = 16
PEER_SC_TOKENS = 8
PEER_SC_RING = 4
PEER_SC_UNROLL = 4
PEER_SLOTS = PEER_HEADS * PEER_TOPK


def _peer_sc_call(body, T, out_width, stage_width):
    mesh = plsc.VectorSubcoreMesh(core_axis_name="c", subcore_axis_name="s")
    return pl.kernel(
        body, mesh=mesh,
        out_type=jax.ShapeDtypeStruct((T, out_width), jnp.float32),
        scratch_types=[
            pltpu.VMEM((PEER_SC_TOKENS, PEER_SLOTS), jnp.int32),
            pltpu.VMEM((PEER_SC_TOKENS, stage_width), jnp.float32),
            pltpu.VMEM((PEER_SC_RING, PEER_TOPK, D_MODEL), jnp.float32),
            pltpu.VMEM((PEER_SC_TOKENS, out_width), jnp.float32),
            pltpu.SemaphoreType.DMA((PEER_SC_RING,)),
        ],
        compiler_params=pltpu.CompilerParams(needs_layout_passes=False),
    )


def _peer_sc_body(compute, zero_out, x_hbm, idx_hbm, tab_hbm, out_hbm, idx_v, x_v, rows_v, out_v, sems):
    T = idx_hbm.shape[0]
    tokens_per_worker = T // (SC_CORES * SC_SUBCORES)
    n_steps = PEER_SC_TOKENS * PEER_HEADS
    worker = lax.axis_index("s") * SC_CORES + lax.axis_index("c")
    base = worker * tokens_per_worker

    def gather(s, b):
        ids = idx_v[s // PEER_HEADS, pl.ds((s % PEER_HEADS) * PEER_TOPK, PEER_TOPK)]
        return pltpu.make_async_copy(tab_hbm.at[ids], rows_v.at[b], sems.at[b])

    @pl.loop(0, tokens_per_worker // PEER_SC_TOKENS)
    def _(blk):
        tok0 = base + blk * PEER_SC_TOKENS
        pltpu.sync_copy(idx_hbm.at[pl.ds(tok0, PEER_SC_TOKENS)], idx_v)
        pltpu.sync_copy(x_hbm.at[pl.ds(tok0, PEER_SC_TOKENS)], x_v)
        for b in range(PEER_SC_RING - 1):
            gather(b, b).start()
        if zero_out:
            @pl.loop(0, PEER_SC_TOKENS)
            def _(t):
                @plsc.parallel_loop(0, out_v.shape[1] // SC_LANES, unroll=PEER_SC_UNROLL)
                def _(c):
                    out_v[t, pl.ds(pl.multiple_of(c * SC_LANES, SC_LANES), SC_LANES)] = (
                        jnp.zeros((SC_LANES,), jnp.float32))

        @pl.loop(0, n_steps, step=PEER_SC_RING)
        def _(s0):
            for b in range(PEER_SC_RING):
                s = s0 + b
                gather(s, b).wait()

                @pl.when(s + PEER_SC_RING - 1 < n_steps)
                def _():
                    gather(s + PEER_SC_RING - 1, (b + PEER_SC_RING - 1) % PEER_SC_RING).start()

                compute(s // PEER_HEADS, s % PEER_HEADS, b, x_v, rows_v, out_v)

        pltpu.sync_copy(out_v, out_hbm.at[pl.ds(tok0, PEER_SC_TOKENS)])


def _peer_dots_compute(t, hd, b, h_v, rows_v, dots_v):
    lane = lax.iota(jnp.int32, SC_LANES)

    def col_step(c, accs):
        off = pl.multiple_of(c * SC_LANES, SC_LANES)
        hv = h_v[t, pl.ds(off, SC_LANES)]
        return tuple(accs[r] + rows_v[b, r, pl.ds(off, SC_LANES)] * hv for r in range(PEER_TOPK))

    accs = plsc.parallel_loop(
        0, D_MODEL // SC_LANES, unroll=PEER_SC_UNROLL,
        carry=tuple(jnp.zeros((SC_LANES,), jnp.float32) for _ in range(PEER_TOPK)))(col_step)
    res = jnp.zeros((SC_LANES,), jnp.float32)
    for r in range(PEER_TOPK):
        res = jnp.where(lane == r, jnp.sum(accs[r]), res)
    dots_v[t, pl.ds(hd * PEER_TOPK, PEER_TOPK)] = res


def _peer_mix_compute(t, hd, b, act_v, rows_v, out_v):
    tvec = jnp.full((SC_LANES,), t, jnp.int32)
    weights = [plsc.load_gather(act_v, [tvec, jnp.full((SC_LANES,), hd * PEER_TOPK + r, jnp.int32)])
               for r in range(PEER_TOPK)]

    @plsc.parallel_loop(0, D_MODEL // SC_LANES, unroll=PEER_SC_UNROLL)
    def _(c):
        off = pl.multiple_of(c * SC_LANES, SC_LANES)
        a = rows_v[b, 0, pl.ds(off, SC_LANES)] * weights[0]
        for r in range(1, PEER_TOPK):
            a = a + rows_v[b, r, pl.ds(off, SC_LANES)] * weights[r]
        plsc.addupdate(out_v.at[t, pl.ds(off, SC_LANES)], a)


def _peer_expert_dots(h, experts, u_tab):
    T = h.shape[0]
    assert T % (SC_CORES * SC_SUBCORES * PEER_SC_TOKENS) == 0
    assert (PEER_SC_TOKENS * PEER_HEADS) % PEER_SC_RING == 0 and PEER_TOPK == SC_LANES
    body = partial(_peer_sc_body, _peer_dots_compute, False)
    return _peer_sc_call(body, T, PEER_SLOTS, D_MODEL)(h, experts, u_tab)


def _peer_expert_mix(act, experts, v_tab):
    T = act.shape[0]
    assert T % (SC_CORES * SC_SUBCORES * PEER_SC_TOKENS) == 0
    body = partial(_peer_sc_body, _peer_mix_compute, True)
    return _peer_sc_call(body, T, D_MODEL, PEER_SLOTS)(act, experts, v_tab)


PEER_TM = 256


def _extract_topk(s, k, payload=None):
    R = s.shape[0]
    riota = lax.broadcasted_iota(jnp.int32, s.shape, 0)
    vals, rows = [], []
    for _ in range(k):
        m = jnp.max(s, axis=0, keepdims=True)
        pos = jnp.min(jnp.where(s == m, riota, R), axis=0, keepdims=True)
        hit = riota == pos
        vals.append(m)
        rows.append(pos if payload is None else jnp.max(jnp.where(hit, payload, -1), axis=0, keepdims=True))
        s = jnp.where(hit, -jnp.inf, s)
    return jnp.concatenate(vals, axis=0), jnp.concatenate(rows, axis=0)


def _peer_route_kernel(x_ref, sc_ref, sh_ref, wq_ref, keys_ref, h_ref, ex_ref, gate_ref, q_ref, ext_ref, gt_ref):
    K = PEER_TOPK
    f32 = jnp.float32
    h = x_ref[0] * (1.0 + sc_ref[0]) + sh_ref[0]
    h_ref[0] = h
    q = jnp.dot(h.astype(wq_ref.dtype), wq_ref[...], preferred_element_type=f32)
    for hd in range(PEER_HEADS):
        q_ref[hd] = q[:, hd * PEER_KEY_DIM:(hd + 1) * PEER_KEY_DIM].astype(q_ref.dtype)

    def head_body(hd, carry):
        qh = q_ref[hd]
        s1 = lax.dot_general(keys_ref[hd, 0], qh[:, :PEER_HALF], _NT, preferred_element_type=f32)
        s2 = lax.dot_general(keys_ref[hd, 1], qh[:, PEER_HALF:], _NT, preferred_element_type=f32)
        v1, i1 = _extract_topk(s1, K)
        v2, i2 = _extract_topk(s2, K)
        tm = v1.shape[1]
        cand_rows, cidx_rows = [], []
        for a in range(K):
            nb = K // (a + 1)
            cand_rows.append(v1[a:a + 1, :] + v2[:nb, :])
            cidx_rows.append(i1[a:a + 1, :] * PEER_N_KEYS + i2[:nb, :])
        n_pad = -sum(r.shape[0] for r in cand_rows) % 8
        cand_rows.append(jnp.full((n_pad, tm), -jnp.inf, f32))
        cidx_rows.append(jnp.full((n_pad, tm), -1, jnp.int32))
        cand = jnp.concatenate(cand_rows, axis=0)
        cidx = jnp.concatenate(cidx_rows, axis=0)
        top_s, experts = _extract_topk(cand, K, payload=cidx)
        e = jnp.exp(top_s - top_s[0:1, :])
        gt_ref[pl.ds(pl.multiple_of(hd * K, K), K), :] = e / jnp.sum(e, axis=0, keepdims=True)
        ext_ref[pl.ds(pl.multiple_of(hd * K, K), K), :] = experts
        return carry

    lax.fori_loop(0, PEER_HEADS, head_body, 0)
    ex_ref[0] = ext_ref[...].T
    gate_ref[0] = gt_ref[...].T


def _peer_route(x, sc, sh, w_pq, sub_keys):
    B, S, D = x.shape
    tm = PEER_TM
    return pl.pallas_call(
        _peer_route_kernel,
        grid=(B, S // tm),
        in_specs=[
            pl.BlockSpec((1, tm, D), lambda b, i: (b, i, 0)),
            pl.BlockSpec((1, 1, D), lambda b, i: (b, 0, 0)),
            pl.BlockSpec((1, 1, D), lambda b, i: (b, 0, 0)),
            pl.BlockSpec((D, PEER_HEADS * PEER_KEY_DIM), lambda b, i: (0, 0)),
            pl.BlockSpec((PEER_HEADS, 2, PEER_N_KEYS, PEER_HALF), lambda b, i: (0, 0, 0, 0)),
        ],
        out_specs=[
            pl.BlockSpec((1, tm, D), lambda b, i: (b, i, 0)),
            pl.BlockSpec((1, tm, PEER_SLOTS), lambda b, i: (b, i, 0)),
            pl.BlockSpec((1, tm, PEER_SLOTS), lambda b, i: (b, i, 0)),
        ],
        out_shape=[
            jax.ShapeDtypeStruct((B, S, D), jnp.float32),
            jax.ShapeDtypeStruct((B, S, PEER_SLOTS), jnp.int32),
            jax.ShapeDtypeStruct((B, S, PEER_SLOTS), jnp.float32),
        ],
        scratch_shapes=[
            pltpu.VMEM((PEER_HEADS, tm, PEER_KEY_DIM), MM_DTYPE),
            pltpu.VMEM((PEER_SLOTS, tm), jnp.int32),
            pltpu.VMEM((PEER_SLOTS, tm), jnp.float32),
        ],
        compiler_params=pltpu.CompilerParams(dimension_semantics=("parallel", "parallel")),
        name="peer_route",
    )(x, sc[:, None, :], sh[:, None, :], w_pq.astype(MM_DTYPE), sub_keys.astype(MM_DTYPE))


def _peer_pre(h2, experts, u_tab):
    B, S, D = h2.shape
    return _peer_expert_dots(h2.reshape(B * S, D), experts.reshape(B * S, PEER_SLOTS), u_tab)


def _peer_act(pre, gates):
    return jax.nn.gelu(pre, approximate=False) * gates.reshape(pre.shape)


def _peer_out(act, experts, v_tab):
    B, S, _ = experts.shape
    return _peer_expert_mix(act, experts.reshape(B * S, PEER_SLOTS), v_tab).reshape(B, S, D_MODEL)


def kernel(x, c, w_ada, b_ada, w_in, rw_mu, rw_w0, rw_w2, rw_a0, rw_a2, rw_g2, rw_k_k, rw_k_a, rw_r_k, rw_gn_g, rw_gn_b, dsa_kv_g, idx_k_g, idx_k_b, rel_bias, w_br_a, w_br_b, w_out, ln1_g, ln1_b, peer_wq, peer_keys, peer_u, peer_v, ln2_g, ln2_b):
    l = 0
    mod = jax.nn.silu(c) @ w_ada[l] + b_ada[l]

    w_rw, w_q, w_kv, w_qi, w_ki, w_wi, w_ga, w_gb = jnp.split(w_in[l], _split_points(IN_SIZES), axis=-1)
    small_pad = jnp.zeros((D_MODEL, SMALL_COLS - SMALL_WI - IDX_HEADS), w_in.dtype)
    w_small = jnp.concatenate([w_qi, w_kv, w_ki, w_wi, small_pad], axis=-1)
    w_gates = jnp.concatenate([w_ga, w_gb], axis=-1)

    def project_and_dsa(x, sc1, sh1):
        z_rw = _mod_matmul(x, sc1, sh1, w_rw, jnp.float32, tn=896)
        z_q = _mod_matmul(x, sc1, sh1, w_q, MM_DTYPE, tn=1024)
        z_small = _mod_matmul(x, sc1, sh1, w_small, jnp.float32, tn=512)
        z_g = _mod_matmul(x, sc1, sh1, w_gates, jnp.float32, tn=1024)
        z_wi_t = jnp.swapaxes(z_small[..., SMALL_WI:SMALL_WI + IDX_HEADS], 1, 2)
        y_b = _dsa_attention(z_q, z_small, z_wi_t, dsa_kv_g[l], idx_k_g[l], idx_k_b[l],
                             rel_bias, jnp.float32) @ w_br_b[l]
        return y_b, z_rw, z_g

    def rwkv_and_merge(x, gt1, y_b, z_rw, z_g):
        z_ga, z_gb = z_g[..., :D_MODEL], z_g[..., D_MODEL:]
        y_a = _rwkv7_time_mix(z_rw, rw_mu[l], rw_w0[l], rw_w2[l], rw_a0[l], rw_a2[l], rw_g2[l],
                              rw_k_k[l], rw_k_a[l], rw_r_k[l], rw_gn_g[l], rw_gn_b[l], jnp.float32) @ w_br_a[l]
        merged = jax.nn.sigmoid(z_ga) * y_a + jax.nn.sigmoid(z_gb) * y_b
        mix = merged @ w_out[l]
        return _layer_norm(DEEPNORM_ALPHA * x + gt1[:, None] * mix, ln1_g[l], ln1_b[l])

    def finish(st, act):
        y2 = _peer_out(act, st["experts"], peer_v[l])
        return _layer_norm(DEEPNORM_ALPHA * st["x1"] + st["gt2"][:, None] * y2, ln2_g[l], ln2_b[l])

    gsz = x.shape[0] // BATCH_GROUPS
    outs = []
    prev = None
    for g in range(BATCH_GROUPS):
        sh1, sc1, gt1, sh2, sc2, gt2 = jnp.split(mod[g * gsz:(g + 1) * gsz], 6, axis=-1)
        x_g = x[g * gsz:(g + 1) * gsz]
        if prev is not None:
            x_g, prev["experts"] = lax.optimization_barrier((x_g, prev["experts"]))
        proj = project_and_dsa(x_g, sc1, sh1)
        if prev is not None:
            proj, prev["pre"], outs = lax.optimization_barrier((proj, prev["pre"], outs))
            outs.append(finish(prev, _peer_act(prev["pre"], prev["gates"])))
        x1 = rwkv_and_merge(x_g, gt1, *proj)
        h2, experts, gates = _peer_route(x1, sc2, sh2, peer_wq[l], peer_keys[l])
        prev = dict(x1=x1, gt2=gt2, experts=experts, gates=gates, pre=_peer_pre(h2, experts, peer_u[l]))
    outs.append(finish(prev, _peer_act(prev["pre"], prev["gates"])))
    return jnp.concatenate(outs, axis=0)
```

```python
import math
from functools import partial

import jax
import jax.numpy as jnp
import numpy as np
from jax import lax
from jax.experimental import pallas as pl
from jax.experimental.pallas import tpu as pltpu
from jax.experimental.pallas import tpu_sc as plsc

D_MODEL = 1024
RW_HEADS = 8
RW_HEAD_DIM = 64
RW_DIM = 512
RW_DECAY_LORA = 64
RW_A_LORA = 64
RW_GATE_LORA = 128
RW_COLS = 3 * RW_DIM + RW_DECAY_LORA + RW_A_LORA + RW_GATE_LORA
RW_GN_EPS = 64e-5
DSA_HEADS = 8
DSA_LATENT = 128
DSA_Q_DIM = DSA_HEADS * DSA_LATENT
IDX_HEADS = 4
IDX_DIM = 64
TOPK_MAX = 256
REL_BUCKETS = 32
REL_MAX_DIST = 128
IN_SIZES = (RW_COLS, DSA_Q_DIM, DSA_LATENT, IDX_HEADS * IDX_DIM, IDX_DIM, IDX_HEADS, D_MODEL, D_MODEL)
IN_COLS = sum(IN_SIZES)
PEER_HEADS = 8
PEER_N_KEYS = 128
PEER_KEY_DIM = 128
PEER_HALF = 64
PEER_TOPK = 16
LN_EPS = 1e-5
DEPTH = 1
DEEPNORM_ALPHA = (2.0 * DEPTH) ** 0.25

LANES = 128
SMALL_QI = 0
SMALL_KV = SMALL_QI + IDX_HEADS * IDX_DIM
SMALL_KI = SMALL_KV + DSA_LATENT
SMALL_WI = SMALL_KI + IDX_DIM
SMALL_COLS = 4 * LANES
MM_DTYPE = jnp.bfloat16
BATCH_GROUPS = 8


def _split_points(sizes):
    return np.cumsum(sizes)[:-1].tolist()


def _mod_matmul_kernel(x_ref, sc_ref, sh_ref, w_ref, o_ref):
    h = x_ref[0] * (1.0 + sc_ref[0]) + sh_ref[0]
    o_ref[0] = jnp.dot(h.astype(w_ref.dtype), w_ref[...],
                       preferred_element_type=jnp.float32).astype(o_ref.dtype)


def _mod_matmul(x, sc, sh, w, out_dtype, tn, tm=512):
    B, S, D = x.shape
    N = w.shape[1]
    return pl.pallas_call(
        _mod_matmul_kernel,
        grid=(B, S // tm, N // tn),
        in_specs=[
            pl.BlockSpec((1, tm, D), lambda b, i, j: (b, i, 0)),
            pl.BlockSpec((1, 1, D), lambda b, i, j: (b, 0, 0)),
            pl.BlockSpec((1, 1, D), lambda b, i, j: (b, 0, 0)),
            pl.BlockSpec((D, tn), lambda b, i, j: (0, j)),
        ],
        out_specs=pl.BlockSpec((1, tm, tn), lambda b, i, j: (b, i, j)),
        out_shape=jax.ShapeDtypeStruct((B, S, N), out_dtype),
        name="mod_matmul",
    )(x, sc[:, None, :], sh[:, None, :], w.astype(MM_DTYPE))


def _layer_norm(x, g, b):
    mu = jnp.mean(x, -1, keepdims=True)
    var = jnp.mean(jnp.square(x - mu), -1, keepdims=True)
    return (x - mu) * lax.rsqrt(var + LN_EPS) * g + b


def _t5_bucket(n):
    n = jnp.maximum(n, 0)
    max_exact = REL_BUCKETS // 2
    nf = jnp.maximum(n, 1).astype(jnp.float32)
    large = max_exact + (jnp.log(nf / max_exact) / math.log(REL_MAX_DIST / max_exact)
                         * (REL_BUCKETS - max_exact)).astype(jnp.int32)
    large = jnp.minimum(large, REL_BUCKETS - 1)
    return jnp.where(n < max_exact, n, large)


RW_CHUNK = 64
RW_INV_BLOCK = 16
_NN = (((1,), (0,)), ((), ()))
_NT = (((1,), (1,)), ((), ()))
_BNN = (((2,), (1,)), ((0,), (0,)))
_BNT = (((2,), (2,)), ((0,), (0,)))


def _dot_f32(a, b, dims=_NN):
    return lax.dot_general(a, b, dims, precision=lax.Precision.HIGHEST,
                           preferred_element_type=jnp.float32)


def _dot_bf16x3(a, b, dims=_NN):
    f32, bf = jnp.float32, jnp.bfloat16
    a_hi, b_hi = a.astype(bf), b.astype(bf)
    a_lo = (a - a_hi.astype(f32)).astype(bf)
    b_lo = (b - b_hi.astype(f32)).astype(bf)
    out = lax.dot_general(a_hi, b_hi, dims, preferred_element_type=f32)
    out = out + lax.dot_general(a_hi, b_lo, dims, preferred_element_type=f32)
    return out + lax.dot_general(a_lo, b_hi, dims, preferred_element_type=f32)


def _bf16_terms(a):
    f32, bf = jnp.float32, jnp.bfloat16
    hi = a.astype(bf)
    r1 = a - hi.astype(f32)
    mid = r1.astype(bf)
    lo = (r1 - mid.astype(f32)).astype(bf)
    return hi, mid, lo


def _dot_lhs_split(a, b01):
    b = b01.astype(jnp.bfloat16)
    return sum(jnp.dot(t, b, preferred_element_type=jnp.float32) for t in _bf16_terms(a))


def _dot_rhs_split(a01, b):
    a = a01.astype(jnp.bfloat16)
    return sum(jnp.dot(a, t, preferred_element_type=jnp.float32) for t in _bf16_terms(b))


def _rwkv_kernel(z_ref, mu_ref, w0_ref, w2_ref, a0_ref, a2_ref, g2_ref, kk_ref, ka_ref, rk_ref,
                 gng_ref, gnb_ref, bd_ref, o_ref, m_ref, prev_ref, y_ref):
    C = z_ref.shape[1]
    N = RW_HEAD_DIM
    f32 = jnp.float32
    dot3 = _dot_bf16x3

    @pl.when(pl.program_id(1) == 0)
    def _():
        m_ref[...] = jnp.zeros(m_ref.shape, f32)
        prev_ref[...] = jnp.zeros(prev_ref.shape, f32)

    z = z_ref[0]
    row = lax.broadcasted_iota(jnp.int32, z.shape, 0)
    shifted = jnp.where(row == 0, prev_ref[...], pltpu.roll(z, 1, axis=0))
    prev_ref[...] = z[C - 1:C, :]
    zs = z + (shifted - z) * mu_ref[...]
    r = zs[:, 0:RW_DIM]
    k = zs[:, RW_DIM:2 * RW_DIM]
    v = zs[:, 2 * RW_DIM:3 * RW_DIM]
    o1 = 3 * RW_DIM
    wl = zs[:, o1:o1 + RW_DECAY_LORA]
    al = zs[:, o1 + RW_DECAY_LORA:o1 + RW_DECAY_LORA + RW_A_LORA]
    gl = zs[:, o1 + RW_DECAY_LORA + RW_A_LORA:]

    bd = bd_ref[...]
    log_w = -jax.nn.softplus(-(w0_ref[...] + dot3(jnp.tanh(wl), w2_ref[...]))) - 0.5
    ldec = -jnp.exp(log_w)
    a_lr = jax.nn.sigmoid(a0_ref[...] + dot3(al, a2_ref[...]))
    g = dot3(jax.nn.sigmoid(gl), g2_ref[...])
    kk = k * kk_ref[...]
    kk = kk * lax.rsqrt(jnp.maximum(_dot_lhs_split(kk * kk, bd), 1e-24))
    k2 = k * (1.0 + (a_lr - 1.0) * ka_ref[...])
    a_vec = -kk
    b_vec = kk * a_lr

    ti = lax.broadcasted_iota(jnp.int32, (C, C), 0)
    tj = lax.broadcasted_iota(jnp.int32, (C, C), 1)
    cum = _dot_rhs_split((ti >= tj).astype(f32), ldec)
    cum_last = cum[C - 1:C, :]
    w_incl = jnp.exp(cum)
    w_excl = jnp.exp(cum - ldec)
    w_inv = jnp.exp(-cum)
    w_end = jnp.exp(cum_last - cum)
    w_all = jnp.exp(cum_last)
    a_t = a_vec * w_excl
    r_t = r * w_incl
    b_t = b_vec * w_inv
    k_t = k2 * w_inv
    b_e = b_vec * w_end
    k_e = k2 * w_end

    strict = ti > tj
    incl = ti >= tj
    bi, bj = ti // RW_INV_BLOCK, tj // RW_INV_BLOCK
    same_blk = bi == bj
    pair_blk = jnp.logical_and((bi // 2) == (bj // 2), jnp.logical_not(same_blk))
    half_blk = (bi // 2) != (bj // 2)
    eye = (ti == tj).astype(f32)

    H = RW_HEADS
    heads = lambda x: jnp.stack([x[:, h * N:(h + 1) * N] for h in range(H)], axis=0)
    bmm = lambda a, b: dot3(a, b, _BNN)
    A, Rt, Bt, Kt, Be, Ke, V = (heads(t) for t in (a_t, r_t, b_t, k_t, b_e, k_e, v))
    gm = dot3(jnp.concatenate([A, Rt], axis=1), jnp.concatenate([Bt, Kt], axis=1), _BNT)
    a_ab = jnp.where(strict, gm[:, :C, :C], 0.0)
    a_ak = jnp.where(strict, gm[:, :C, C:], 0.0)
    a_rb = jnp.where(incl, gm[:, C:, :C], 0.0)
    a_rk = jnp.where(incl, gm[:, C:, C:], 0.0)
    d1 = jnp.where(same_blk, a_ab, 0.0)
    xinv = eye + d1
    d2 = bmm(d1, d1)
    xinv = xinv + bmm(xinv, d2)
    d4 = bmm(d2, d2)
    xinv = xinv + bmm(xinv, d4)
    d8 = bmm(d4, d4)
    xinv = xinv + bmm(xinv, d8)
    xinv = xinv + bmm(bmm(xinv, jnp.where(pair_blk, a_ab, 0.0)), xinv)
    xinv = xinv + bmm(bmm(xinv, jnp.where(half_blk, a_ab, 0.0)), xinv)

    av = bmm(jnp.concatenate([a_ak, a_rk], axis=1), V)
    p = bmm(xinv, jnp.concatenate([A, av[:, :C]], axis=2))
    qm = bmm(a_rb, p)
    q1 = Rt + qm[:, :, :N]
    q2 = qm[:, :, N:] + av[:, C:]
    gmat = bmm(jnp.swapaxes(Be, 1, 2), p)
    g1 = eye * heads(w_all) + gmat[:, :, :N]
    g2 = gmat[:, :, N:] + bmm(jnp.swapaxes(Ke, 1, 2), V)
    m = m_ref[...]
    yh = _dot_f32(q1, m, _BNN) + q2
    m_ref[...] = _dot_f32(g1, m, _BNN) + g2
    for h in range(H):
        y_ref[:, h * N:(h + 1) * N] = yh[h]

    y = y_ref[...]
    mean = _dot_lhs_split(y, bd) * (1.0 / N)
    yc = y - mean
    var = _dot_lhs_split(yc * yc, bd) * (1.0 / N)
    yn = yc * lax.rsqrt(var + RW_GN_EPS) * gng_ref[...] + gnb_ref[...]
    bonus = _dot_lhs_split(r * k2 * rk_ref[...], bd) * v
    o_ref[0] = ((yn + bonus) * g).astype(o_ref.dtype)


def _rwkv7_time_mix(z_rw, mu, w0, w2, a0, a2, g2, k_k, k_a, r_k, gn_g, gn_b, out_dtype):
    B, S, _ = z_rw.shape
    C = RW_CHUNK
    assert S % C == 0 and C == RW_HEAD_DIM and C % (4 * RW_INV_BLOCK) == 0
    hid = jnp.arange(RW_DIM) // RW_HEAD_DIM
    bd = (hid[:, None] == hid[None, :]).astype(jnp.float32)
    row = lambda a: a.reshape(1, -1)
    full = lambda shape: pl.BlockSpec(shape, lambda b, c: (0,) * len(shape))
    return pl.pallas_call(
        _rwkv_kernel,
        grid=(B, S // C),
        in_specs=[
            pl.BlockSpec((1, C, RW_COLS), lambda b, c: (b, c, 0)),
            full((1, RW_COLS)), full((1, RW_DIM)), full((RW_DECAY_LORA, RW_DIM)), full((1, RW_DIM)),
            full((RW_A_LORA, RW_DIM)), full((RW_GATE_LORA, RW_DIM)), full((1, RW_DIM)), full((1, RW_DIM)),
            full((1, RW_DIM)), full((1, RW_DIM)), full((1, RW_DIM)), full((RW_DIM, RW_DIM)),
        ],
        out_specs=pl.BlockSpec((1, C, RW_DIM), lambda b, c: (b, c, 0)),
        out_shape=jax.ShapeDtypeStruct((B, S, RW_DIM), out_dtype),
        scratch_shapes=[
            pltpu.VMEM((RW_HEADS, RW_HEAD_DIM, RW_HEAD_DIM), jnp.float32),
            pltpu.VMEM((1, RW_COLS), jnp.float32),
            pltpu.VMEM((C, RW_DIM), jnp.float32),
        ],
        compiler_params=pltpu.CompilerParams(dimension_semantics=("parallel", "arbitrary")),
        name="rwkv7_time_mix",
    )(z_rw, row(mu), row(w0), w2, row(a0), a2, g2, row(k_k), row(k_a), row(r_k), row(gn_g), row(gn_b), bd)


DSA_TQ = 256
MASK_NEG = -1e30
INT_MIN = -2 ** 31
KEY_NEG_INF = -2139095041
THRESH_BITS = 32


def _dsa_prep_kernel(z_ref, kvg_ref, kig_ref, kib_ref, qi_ref, kv_ref, ki_ref):
    z = z_ref[0]
    qi_ref[0] = z[:, SMALL_QI:SMALL_KV].astype(qi_ref.dtype)
    kv = z[:, SMALL_KV:SMALL_KI]
    ms = jnp.mean(jnp.square(kv), -1, keepdims=True)
    kv_ref[0] = (kv * lax.rsqrt(ms + LN_EPS) * kvg_ref[...]).astype(kv_ref.dtype)
    ki = z[:, SMALL_KI:SMALL_WI]
    mu = jnp.mean(ki, -1, keepdims=True)
    var = jnp.mean(jnp.square(ki - mu), -1, keepdims=True)
    ki_ref[0] = ((ki - mu) * lax.rsqrt(var + LN_EPS) * kig_ref[...] + kib_ref[...]).astype(ki_ref.dtype)


def _dsa_prep(z_small, kv_g, ki_g, ki_b, tm=512):
    B, S, W = z_small.shape
    return pl.pallas_call(
        _dsa_prep_kernel,
        grid=(B, S // tm),
        in_specs=[
            pl.BlockSpec((1, tm, W), lambda b, i: (b, i, 0)),
            pl.BlockSpec((1, DSA_LATENT), lambda b, i: (0, 0)),
            pl.BlockSpec((1, IDX_DIM), lambda b, i: (0, 0)),
            pl.BlockSpec((1, IDX_DIM), lambda b, i: (0, 0)),
        ],
        out_specs=[
            pl.BlockSpec((1, tm, IDX_HEADS * IDX_DIM), lambda b, i: (b, i, 0)),
            pl.BlockSpec((1, tm, DSA_LATENT), lambda b, i: (b, i, 0)),
            pl.BlockSpec((1, tm, IDX_DIM), lambda b, i: (b, i, 0)),
        ],
        out_shape=[
            jax.ShapeDtypeStruct((B, S, IDX_HEADS * IDX_DIM), MM_DTYPE),
            jax.ShapeDtypeStruct((B, S, DSA_LATENT), MM_DTYPE),
            jax.ShapeDtypeStruct((B, S, IDX_DIM), MM_DTYPE),
        ],
        name="dsa_prep",
    )(z_small, kv_g[None], ki_g[None], ki_b[None])


def _sortable_key(s):
    s = jnp.where(s == 0.0, 0.0, s)
    bits = pltpu.bitcast(s, jnp.int32)
    return bits ^ ((bits >> 31) & 0x7FFFFFFF)


def _col_count(mask_i32):
    tk, tq = mask_i32.shape
    return jnp.sum(mask_i32.reshape(tk // 8, 8, tq), axis=0)


def _dsa_kernel(bfar_ref, q_ref, qi_ref, wit_ref, kv_ref, kvt_ref, ki_ref, b0_ref, b1_ref, o_ref,
                key_ref, madd_ref, m_ref, l_ref, acc_ref, *, topk):
    tq = q_ref.shape[1]
    tk = tq
    i = pl.program_id(1)
    nj = i + 1
    f32 = jnp.float32
    krow = lax.broadcasted_iota(jnp.int32, (tk, tq), 0)
    qcol = lax.broadcasted_iota(jnp.int32, (tk, tq), 1)

    qi = qi_ref[0]
    wit = wit_ref[0] * (IDX_HEADS ** -0.5)

    def score_chunk(j, carry):
        off = pl.multiple_of(j * tk, tk)
        kc = ki_ref[0, pl.ds(off, tk), :]
        s = jnp.zeros((tk, tq), f32)
        for h in range(IDX_HEADS):
            d = lax.dot_general(kc, qi[:, h * IDX_DIM:(h + 1) * IDX_DIM],
                                (((1,), (1,)), ((), ())), preferred_element_type=f32)
            s = s + wit[h:h + 1, :] * jnp.maximum(d * (IDX_DIM ** -0.5), 0.0)
        causal = (krow + j * tk) <= (qcol + i * tq)
        s = jnp.where(causal, s, -jnp.inf)
        key_ref[j] = _sortable_key(s)
        return carry

    lax.fori_loop(0, nj, score_chunk, 0)

    def count_where(pred_fn):
        def body(j, acc):
            return acc + _col_count(pred_fn(key_ref[j], j).astype(jnp.int32))
        acc = lax.fori_loop(0, nj, body, jnp.zeros((8, tq), jnp.int32))
        return jnp.sum(acc, axis=0, keepdims=True)

    def bit_step(it, t_u):
        bit = THRESH_BITS - 1 - it
        cand_u = t_u | jnp.left_shift(jnp.int32(1), bit)
        cand = cand_u ^ INT_MIN
        cnt = count_where(lambda k, j: k >= cand)
        return jnp.where(cnt >= topk, cand_u, t_u)

    t_u = lax.fori_loop(0, THRESH_BITS, bit_step, jnp.zeros((1, tq), jnp.int32))
    thr = t_u ^ INT_MIN
    cnt_gt = count_where(lambda k, j: k > thr)
    cnt_ge = count_where(lambda k, j: k >= thr)
    is_neg = thr == KEY_NEG_INF
    need = jnp.logical_and(cnt_ge > topk, jnp.logical_not(is_neg))
    n_tie_take = topk - cnt_gt

    thr_open = jnp.where(is_neg, thr, thr - 1)
    any_need = jnp.max(need.astype(jnp.int32)) > 0

    @pl.when(jnp.logical_not(any_need))
    def _():
        def body(j, carry):
            madd_ref[j] = jnp.where(key_ref[j] > thr_open, 0.0, MASK_NEG)
            return carry
        lax.fori_loop(0, nj, body, 0)

    @pl.when(any_need)
    def _():
        s_len = tk * key_ref.shape[0]
        n_bits = max(1, int(math.ceil(math.log2(s_len))))

        def idx_step(it, p):
            bit = n_bits - 1 - it
            cand = p | jnp.left_shift(jnp.int32(1), bit)
            cnt = count_where(
                lambda k, j: jnp.where(k == thr, jnp.where((krow + j * tk) < cand, 1, 0), 0))
            return jnp.where(cnt < n_tie_take, cand, p)

        p_idx = lax.fori_loop(0, n_bits, idx_step, jnp.zeros((1, tq), jnp.int32))
        p_idx = jnp.where(need, p_idx, jnp.where(is_neg, -1, s_len))

        def body(j, carry):
            k = key_ref[j]
            tie_ok = jnp.where((krow + j * tk) <= p_idx, 0.0, MASK_NEG)
            madd_ref[j] = jnp.where(k > thr, 0.0, jnp.where(k == thr, tie_ok, MASK_NEG))
            return carry
        lax.fori_loop(0, nj, body, 0)

    H, d = DSA_HEADS, DSA_LATENT
    m_ref[...] = jnp.full(m_ref.shape, MASK_NEG, f32)
    l_ref[...] = jnp.zeros(l_ref.shape, f32)
    acc_ref[...] = jnp.zeros(acc_ref.shape, f32)
    scale = DSA_LATENT ** -0.5
    q_all = jnp.concatenate([q_ref[0, :, h * d:(h + 1) * d] for h in range(H)], axis=0)

    def attend(j, bias):
        off = pl.multiple_of(j * tk, tk)
        kc = kv_ref[0, pl.ds(off, tk), :]
        kct = kvt_ref[0, :, pl.ds(off, tk)]
        lg = lax.dot_general(kc, q_all, (((1,), (1,)), ((), ())), preferred_element_type=f32)
        lg = lg * scale + bias + jnp.tile(madd_ref[j], (1, H))
        m_old = m_ref[...]
        m_new = jnp.maximum(m_old, jnp.max(lg, axis=0, keepdims=True))
        alpha = jnp.exp(m_old - m_new)
        p = jnp.exp(lg - m_new)
        l_ref[...] = alpha * l_ref[...] + jnp.sum(p, axis=0, keepdims=True)
        acc_ref[...] = alpha * acc_ref[...] + jnp.dot(kct, p.astype(kct.dtype), preferred_element_type=f32)
        m_ref[...] = m_new

    def far_body(j, carry):
        attend(j, bfar_ref[...])
        return carry

    lax.fori_loop(0, jnp.maximum(i - 1, 0), far_body, 0)

    @pl.when(i >= 1)
    def _():
        attend(i - 1, b1_ref[...])

    attend(i, b0_ref[...])

    out_t = acc_ref[...] / l_ref[...]
    for h in range(H):
        o_ref[0, :, h * d:(h + 1) * d] = out_t[:, h * tq:(h + 1) * tq].T.astype(o_ref.dtype)


def _dsa_bias_tiles(rel_bias, tq):
    dist = jnp.arange(2 * tq, dtype=jnp.int32)
    by_dist = rel_bias[_t5_bucket(dist)].T

    def toeplitz(v):
        H, P = v.shape
        skew = jnp.tile(v, (1, tq + 1))[:, :tq * (P + 1)].reshape(H, tq, P + 1)
        return skew[:, ::-1, :tq]

    b0 = toeplitz(jnp.concatenate([jnp.repeat(by_dist[:, :1], tq - 1, axis=1), by_dist[:, :tq + 1]], axis=1))
    b1 = toeplitz(jnp.concatenate([by_dist[:, 1:], by_dist[:, -1:]], axis=1))
    return b0, b1, by_dist[:, -1]


def _dsa_attention(q, z_small, z_wi_t, kv_g, idx_k_g, idx_k_b, rel_bias, out_dtype):
    B, S, _ = q.shape
    tq = DSA_TQ
    topk = min(TOPK_MAX, S // 4)
    assert S % tq == 0 and tq >= REL_MAX_DIST and topk <= tq
    qi, ckv, kidx = _dsa_prep(z_small, kv_g, idx_k_g, idx_k_b)
    ckv_t = jnp.swapaxes(ckv, 1, 2)
    b0, b1, bfar = _dsa_bias_tiles(rel_bias, tq)
    nq = S // tq
    H, d = DSA_HEADS, DSA_LATENT
    b0 = jnp.transpose(b0, (1, 0, 2)).reshape(tq, H * tq)
    b1 = jnp.transpose(b1, (1, 0, 2)).reshape(tq, H * tq)
    bfar = jnp.repeat(bfar, tq)[None, :]
    f32_bytes = 4
    vmem_estimate = (
        2 * nq * tq * tq * f32_bytes
        + 2 * 2 * tq * H * tq * f32_bytes
        + 2 * (2 * S * d + S * LANES) * 2
        + 6 * tq * H * tq * f32_bytes
        + 4 * tq * H * d * f32_bytes
    )
    grid_spec = pltpu.PrefetchScalarGridSpec(
        num_scalar_prefetch=0,
        grid=(B, nq),
        in_specs=[
            pl.BlockSpec((1, H * tq), lambda b, i: (0, 0)),
            pl.BlockSpec((1, tq, H * d), lambda b, i: (b, i, 0)),
            pl.BlockSpec((1, tq, IDX_HEADS * IDX_DIM), lambda b, i: (b, i, 0)),
            pl.BlockSpec((1, IDX_HEADS, tq), lambda b, i: (b, 0, i)),
            pl.BlockSpec((1, S, d), lambda b, i: (b, 0, 0)),
            pl.BlockSpec((1, d, S), lambda b, i: (b, 0, 0)),
            pl.BlockSpec((1, S, IDX_DIM), lambda b, i: (b, 0, 0)),
            pl.BlockSpec((tq, H * tq), lambda b, i: (0, 0)),
            pl.BlockSpec((tq, H * tq), lambda b, i: (0, 0)),
        ],
        out_specs=pl.BlockSpec((1, tq, H * d), lambda b, i: (b, i, 0)),
        scratch_shapes=[
            pltpu.VMEM((nq, tq, tq), jnp.int32),
            pltpu.VMEM((nq, tq, tq), jnp.float32),
            pltpu.VMEM((1, H * tq), jnp.float32),
            pltpu.VMEM((1, H * tq), jnp.float32),
            pltpu.VMEM((d, H * tq), jnp.float32),
        ],
    )
    return pl.pallas_call(
        partial(_dsa_kernel, topk=topk),
        grid_spec=grid_spec,
        out_shape=jax.ShapeDtypeStruct((B, S, H * d), out_dtype),
        compiler_params=pltpu.CompilerParams(vmem_limit_bytes=vmem_estimate),
        name="dsa_attention",
    )(bfar, q, qi, z_wi_t, ckv, ckv_t, kidx, b0, b1)


SC_CORES = 2
SC_SUBCORES = 16
SC_LANES = 16
PEER_SC_TOKENS = 8
PEER_SC_RING = 4
PEER_SC_UNROLL = 4
PEER_SLOTS = PEER_HEADS * PEER_TOPK


def _peer_sc_call(body, T, out_width, stage_width):
    mesh = plsc.VectorSubcoreMesh(core_axis_name="c", subcore_axis_name="s")
    return pl.kernel(
        body, mesh=mesh,
        out_type=jax.ShapeDtypeStruct((T, out_width), jnp.float32),
        scratch_types=[
            pltpu.VMEM((PEER_SC_TOKENS, PEER_SLOTS), jnp.int32),
            pltpu.VMEM((PEER_SC_TOKENS, stage_width), jnp.float32),
            pltpu.VMEM((PEER_SC_RING, PEER_TOPK, D_MODEL), jnp.float32),
            pltpu.VMEM((PEER_SC_TOKENS, out_width), jnp.float32),
            pltpu.SemaphoreType.DMA((PEER_SC_RING,)),
        ],
        compiler_params=pltpu.CompilerParams(needs_layout_passes=False),
    )


def _peer_sc_body(compute, zero_out, x_hbm, idx_hbm, tab_hbm, out_hbm, idx_v, x_v, rows_v, out_v, sems):
    T = idx_hbm.shape[0]
    tokens_per_worker = T // (SC_CORES * SC_SUBCORES)
    n_steps = PEER_SC_TOKENS * PEER_HEADS
    worker = lax.axis_index("s") * SC_CORES + lax.axis_index("c")
    base = worker * tokens_per_worker

    def gather(s, b):
        ids = idx_v[s // PEER_HEADS, pl.ds((s % PEER_HEADS) * PEER_TOPK, PEER_TOPK)]
        return pltpu.make_async_copy(tab_hbm.at[ids], rows_v.at[b], sems.at[b])

    @pl.loop(0, tokens_per_worker // PEER_SC_TOKENS)
    def _(blk):
        tok0 = base + blk * PEER_SC_TOKENS
        pltpu.sync_copy(idx_hbm.at[pl.ds(tok0, PEER_SC_TOKENS)], idx_v)
        pltpu.sync_copy(x_hbm.at[pl.ds(tok0, PEER_SC_TOKENS)], x_v)
        for b in range(PEER_SC_RING - 1):
            gather(b, b).start()
        if zero_out:
            @pl.loop(0, PEER_SC_TOKENS)
            def _(t):
                @plsc.parallel_loop(0, out_v.shape[1] // SC_LANES, unroll=PEER_SC_UNROLL)
                def _(c):
                    out_v[t, pl.ds(pl.multiple_of(c * SC_LANES, SC_LANES), SC_LANES)] = (
                        jnp.zeros((SC_LANES,), jnp.float32))

        @pl.loop(0, n_steps, step=PEER_SC_RING)
        def _(s0):
            for b in range(PEER_SC_RING):
                s = s0 + b
                gather(s, b).wait()

                @pl.when(s + PEER_SC_RING - 1 < n_steps)
                def _():
                    gather(s + PEER_SC_RING - 1, (b + PEER_SC_RING - 1) % PEER_SC_RING).start()

                compute(s // PEER_HEADS, s % PEER_HEADS, b, x_v, rows_v, out_v)

        pltpu.sync_copy(out_v, out_hbm.at[pl.ds(tok0, PEER_SC_TOKENS)])


def _peer_dots_compute(t, hd, b, h_v, rows_v, dots_v):
    lane = lax.iota(jnp.int32, SC_LANES)

    def col_step(c, accs):
        off = pl.multiple_of(c * SC_LANES, SC_LANES)
        hv = h_v[t, pl.ds(off, SC_LANES)]
        return tuple(accs[r] + rows_v[b, r, pl.ds(off, SC_LANES)] * hv for r in range(PEER_TOPK))

    accs = plsc.parallel_loop(
        0, D_MODEL // SC_LANES, unroll=PEER_SC_UNROLL,
        carry=tuple(jnp.zeros((SC_LANES,), jnp.float32) for _ in range(PEER_TOPK)))(col_step)
    res = jnp.zeros((SC_LANES,), jnp.float32)
    for r in range(PEER_TOPK):
        res = jnp.where(lane == r, jnp.sum(accs[r]), res)
    dots_v[t, pl.ds(hd * PEER_TOPK, PEER_TOPK)] = res


def _peer_mix_compute(t, hd, b, act_v, rows_v, out_v):
    tvec = jnp.full((SC_LANES,), t, jnp.int32)
    weights = [plsc.load_gather(act_v, [tvec, jnp.full((SC_LANES,), hd * PEER_TOPK + r, jnp.int32)])
               for r in range(PEER_TOPK)]

    @plsc.parallel_loop(0, D_MODEL // SC_LANES, unroll=PEER_SC_UNROLL)
    def _(c):
        off = pl.multiple_of(c * SC_LANES, SC_LANES)
        a = rows_v[b, 0, pl.ds(off, SC_LANES)] * weights[0]
        for r in range(1, PEER_TOPK):
            a = a + rows_v[b, r, pl.ds(off, SC_LANES)] * weights[r]
        plsc.addupdate(out_v.at[t, pl.ds(off, SC_LANES)], a)


def _peer_expert_dots(h, experts, u_tab):
    T = h.shape[0]
    assert T % (SC_CORES * SC_SUBCORES * PEER_SC_TOKENS) == 0
    assert (PEER_SC_TOKENS * PEER_HEADS) % PEER_SC_RING == 0 and PEER_TOPK == SC_LANES
    body = partial(_peer_sc_body, _peer_dots_compute, False)
    return _peer_sc_call(body, T, PEER_SLOTS, D_MODEL)(h, experts, u_tab)


def _peer_expert_mix(act, experts, v_tab):
    T = act.shape[0]
    assert T % (SC_CORES * SC_SUBCORES * PEER_SC_TOKENS) == 0
    body = partial(_peer_sc_body, _peer_mix_compute, True)
    return _peer_sc_call(body, T, D_MODEL, PEER_SLOTS)(act, experts, v_tab)


PEER_TM = 256


def _extract_topk(s, k, payload=None):
    R = s.shape[0]
    riota = lax.broadcasted_iota(jnp.int32, s.shape, 0)
    vals, rows = [], []
    for _ in range(k):
        m = jnp.max(s, axis=0, keepdims=True)
        pos = jnp.min(jnp.where(s == m, riota, R), axis=0, keepdims=True)
        hit = riota == pos
        vals.append(m)
        rows.append(pos if payload is None else jnp.max(jnp.where(hit, payload, -1), axis=0, keepdims=True))
        s = jnp.where(hit, -jnp.inf, s)
    return jnp.concatenate(vals, axis=0), jnp.concatenate(rows, axis=0)


def _peer_route_kernel(x_ref, sc_ref, sh_ref, wq_ref, keys_ref, h_ref, ex_ref, gate_ref, q_ref, ext_ref, gt_ref):
    K = PEER_TOPK
    f32 = jnp.float32
    h = x_ref[0] * (1.0 + sc_ref[0]) + sh_ref[0]
    h_ref[0] = h
    q = jnp.dot(h.astype(wq_ref.dtype), wq_ref[...], preferred_element_type=f32)
    for hd in range(PEER_HEADS):
        q_ref[hd] = q[:, hd * PEER_KEY_DIM:(hd + 1) * PEER_KEY_DIM].astype(q_ref.dtype)

    def head_body(hd, carry):
        qh = q_ref[hd]
        s1 = lax.dot_general(keys_ref[hd, 0], qh[:, :PEER_HALF], _NT, preferred_element_type=f32)
        s2 = lax.dot_general(keys_ref[hd, 1], qh[:, PEER_HALF:], _NT, preferred_element_type=f32)
        v1, i1 = _extract_topk(s1, K)
        v2, i2 = _extract_topk(s2, K)
        tm = v1.shape[1]
        cand_rows, cidx_rows = [], []
        for a in range(K):
            nb = K // (a + 1)
            cand_rows.append(v1[a:a + 1, :] + v2[:nb, :])
            cidx_rows.append(i1[a:a + 1, :] * PEER_N_KEYS + i2[:nb, :])
        n_pad = -sum(r.shape[0] for r in cand_rows) % 8
        cand_rows.append(jnp.full((n_pad, tm), -jnp.inf, f32))
        cidx_rows.append(jnp.full((n_pad, tm), -1, jnp.int32))
        cand = jnp.concatenate(cand_rows, axis=0)
        cidx = jnp.concatenate(cidx_rows, axis=0)
        top_s, experts = _extract_topk(cand, K, payload=cidx)
        e = jnp.exp(top_s - top_s[0:1, :])
        gt_ref[pl.ds(pl.multiple_of(hd * K, K), K), :] = e / jnp.sum(e, axis=0, keepdims=True)
        ext_ref[pl.ds(pl.multiple_of(hd * K, K), K), :] = experts
        return carry

    lax.fori_loop(0, PEER_HEADS, head_body, 0)
    ex_ref[0] = ext_ref[...].T
    gate_ref[0] = gt_ref[...].T


def _peer_route(x, sc, sh, w_pq, sub_keys):
    B, S, D = x.shape
    tm = PEER_TM
    return pl.pallas_call(
        _peer_route_kernel,
        grid=(B, S // tm),
        in_specs=[
            pl.BlockSpec((1, tm, D), lambda b, i: (b, i, 0)),
            pl.BlockSpec((1, 1, D), lambda b, i: (b, 0, 0)),
            pl.BlockSpec((1, 1, D), lambda b, i: (b, 0, 0)),
            pl.BlockSpec((D, PEER_HEADS * PEER_KEY_DIM), lambda b, i: (0, 0)),
            pl.BlockSpec((PEER_HEADS, 2, PEER_N_KEYS, PEER_HALF), lambda b, i: (0, 0, 0, 0)),
        ],
        out_specs=[
            pl.BlockSpec((1, tm, D), lambda b, i: (b, i, 0)),
            pl.BlockSpec((1, tm, PEER_SLOTS), lambda b, i: (b, i, 0)),
            pl.BlockSpec((1, tm, PEER_SLOTS), lambda b, i: (b, i, 0)),
        ],
        out_shape=[
            jax.ShapeDtypeStruct((B, S, D), jnp.float32),
            jax.ShapeDtypeStruct((B, S, PEER_SLOTS), jnp.int32),
            jax.ShapeDtypeStruct((B, S, PEER_SLOTS), jnp.float32),
        ],
        scratch_shapes=[
            pltpu.VMEM((PEER_HEADS, tm, PEER_KEY_DIM), MM_DTYPE),
            pltpu.VMEM((PEER_SLOTS, tm), jnp.int32),
            pltpu.VMEM((PEER_SLOTS, tm), jnp.float32),
        ],
        compiler_params=pltpu.CompilerParams(dimension_semantics=("parallel", "parallel")),
        name="peer_route",
    )(x, sc[:, None, :], sh[:, None, :], w_pq.astype(MM_DTYPE), sub_keys.astype(MM_DTYPE))


def _peer_pre(h2, experts, u_tab):
    B, S, D = h2.shape
    return _peer_expert_dots(h2.reshape(B * S, D), experts.reshape(B * S, PEER_SLOTS), u_tab)


def _peer_act(pre, gates):
    return jax.nn.gelu(pre, approximate=False) * gates.reshape(pre.shape)


def _peer_out(act, experts, v_tab):
    B, S, _ = experts.shape
    return _peer_expert_mix(act, experts.reshape(B * S, PEER_SLOTS), v_tab).reshape(B, S, D_MODEL)


def kernel(x, c, w_ada, b_ada, w_in, rw_mu, rw_w0, rw_w2, rw_a0, rw_a2, rw_g2, rw_k_k, rw_k_a, rw_r_k, rw_gn_g, rw_gn_b, dsa_kv_g, idx_k_g, idx_k_b, rel_bias, w_br_a, w_br_b, w_out, ln1_g, ln1_b, peer_wq, peer_keys, peer_u, peer_v, ln2_g, ln2_b):
    l = 0
    mod = jax.nn.silu(c) @ w_ada[l] + b_ada[l]

    w_rw, w_q, w_kv, w_qi, w_ki, w_wi, w_ga, w_gb = jnp.split(w_in[l], _split_points(IN_SIZES), axis=-1)
    small_pad = jnp.zeros((D_MODEL, SMALL_COLS - SMALL_WI - IDX_HEADS), w_in.dtype)
    w_small = jnp.concatenate([w_qi, w_kv, w_ki, w_wi, small_pad], axis=-1)
    w_gates = jnp.concatenate([w_ga, w_gb], axis=-1)

    def project_and_dsa(x, sc1, sh1):
        z_rw = _mod_matmul(x, sc1, sh1, w_rw, jnp.float32, tn=896)
        z_q = _mod_matmul(x, sc1, sh1, w_q, MM_DTYPE, tn=1024)
        z_small = _mod_matmul(x, sc1, sh1, w_small, jnp.float32, tn=512)
        z_g = _mod_matmul(x, sc1, sh1, w_gates, jnp.float32, tn=1024)
        z_wi_t = jnp.swapaxes(z_small[..., SMALL_WI:SMALL_WI + IDX_HEADS], 1, 2)
        y_b = _dsa_attention(z_q, z_small, z_wi_t, dsa_kv_g[l], idx_k_g[l], idx_k_b[l],
                             rel_bias, jnp.float32) @ w_br_b[l]
        return y_b, z_rw, z_g

    def rwkv_and_merge(x, gt1, y_b, z_rw, z_g):
        z_ga, z_gb = z_g[..., :D_MODEL], z_g[..., D_MODEL:]
        y_a = _rwkv7_time_mix(z_rw, rw_mu[l], rw_w0[l], rw_w2[l], rw_a0[l], rw_a2[l], rw_g2[l],
                              rw_k_k[l], rw_k_a[l], rw_r_k[l], rw_gn_g[l], rw_gn_b[l], jnp.float32) @ w_br_a[l]
        merged = jax.nn.sigmoid(z_ga) * y_a + jax.nn.sigmoid(z_gb) * y_b
        mix = merged @ w_out[l]
        return _layer_norm(DEEPNORM_ALPHA * x + gt1[:, None] * mix, ln1_g[l], ln1_b[l])

    def finish(st, act):
        y2 = _peer_out(act, st["experts"], peer_v[l])
        return _layer_norm(DEEPNORM_ALPHA * st["x1"] + st["gt2"][:, None] * y2, ln2_g[l], ln2_b[l])

    gsz = x.shape[0] // BATCH_GROUPS
    outs = []
    prev = None
    for g in range(BATCH_GROUPS):
        sh1, sc1, gt1, sh2, sc2, gt2 = jnp.split(mod[g * gsz:(g + 1) * gsz], 6, axis=-1)
        x_g = x[g * gsz:(g + 1) * gsz]
        if prev is not None:
            x_g, prev["experts"] = lax.optimization_barrier((x_g, prev["experts"]))
        proj = project_and_dsa(x_g, sc1, sh1)
        if prev is not None:
            proj, prev["pre"], outs = lax.optimization_barrier((proj, prev["pre"], outs))
            outs.append(finish(prev, _peer_act(prev["pre"], prev["gates"])))
        x1 = rwkv_and_merge(x_g, gt1, *proj)
        h2, experts, gates = _peer_route(x1, sc2, sh2, peer_wq[l], peer_keys[l])
        prev = dict(x1=x1, gt2=gt2, experts=experts, gates=gates, pre=_peer_pre(h2, experts, peer_u[l]))
    prev["pre"], outs = lax.optimization_barrier((prev["pre"], outs))
    outs.append(finish(prev, _peer_act(prev["pre"], prev["gates"])))
    return jnp.concatenate(outs, axis=0)
```

```python
import math
from functools import partial

import jax
import jax.numpy as jnp
import numpy as np
from jax import lax
from jax.experimental import pallas as pl
from jax.experimental.pallas import tpu as pltpu
from jax.experimental.pallas import tpu_sc as plsc

D_MODEL = 1024
RW_HEADS = 8
RW_HEAD_DIM = 64
RW_DIM = 512
RW_DECAY_LORA = 64
RW_A_LORA = 64
RW_GATE_LORA = 128
RW_COLS = 3 * RW_DIM + RW_DECAY_LORA + RW_A_LORA + RW_GATE_LORA
RW_GN_EPS = 64e-5
DSA_HEADS = 8
DSA_LATENT = 128
DSA_Q_DIM = DSA_HEADS * DSA_LATENT
IDX_HEADS = 4
IDX_DIM = 64
TOPK_MAX = 256
REL_BUCKETS = 32
REL_MAX_DIST = 128
IN_SIZES = (RW_COLS, DSA_Q_DIM, DSA_LATENT, IDX_HEADS * IDX_DIM, IDX_DIM, IDX_HEADS, D_MODEL, D_MODEL)
IN_COLS = sum(IN_SIZES)
PEER_HEADS = 8
PEER_N_KEYS = 128
PEER_KEY_DIM = 128
PEER_HALF = 64
PEER_TOPK = 16
LN_EPS = 1e-5
DEPTH = 1
DEEPNORM_ALPHA = (2.0 * DEPTH) ** 0.25

LANES = 128
SMALL_QI = 0
SMALL_KV = SMALL_QI + IDX_HEADS * IDX_DIM
SMALL_KI = SMALL_KV + DSA_LATENT
SMALL_WI = SMALL_KI + IDX_DIM
SMALL_COLS = 4 * LANES
MM_DTYPE = jnp.bfloat16
BATCH_GROUPS = 8
SEQ_SEGMENTS = 2


def _split_points(sizes):
    return np.cumsum(sizes)[:-1].tolist()


def _mod_matmul_kernel(x_ref, sc_ref, sh_ref, w_ref, o_ref):
    h = x_ref[0] * (1.0 + sc_ref[0]) + sh_ref[0]
    o_ref[0] = jnp.dot(h.astype(w_ref.dtype), w_ref[...],
                       preferred_element_type=jnp.float32).astype(o_ref.dtype)


def _mod_matmul(x, sc, sh, w, out_dtype, tn, tm=512):
    B, S, D = x.shape
    N = w.shape[1]
    return pl.pallas_call(
        _mod_matmul_kernel,
        grid=(B, S // tm, N // tn),
        in_specs=[
            pl.BlockSpec((1, tm, D), lambda b, i, j: (b, i, 0)),
            pl.BlockSpec((1, 1, D), lambda b, i, j: (b, 0, 0)),
            pl.BlockSpec((1, 1, D), lambda b, i, j: (b, 0, 0)),
            pl.BlockSpec((D, tn), lambda b, i, j: (0, j)),
        ],
        out_specs=pl.BlockSpec((1, tm, tn), lambda b, i, j: (b, i, j)),
        out_shape=jax.ShapeDtypeStruct((B, S, N), out_dtype),
        name="mod_matmul",
    )(x, sc[:, None, :], sh[:, None, :], w.astype(MM_DTYPE))


def _layer_norm(x, g, b):
    mu = jnp.mean(x, -1, keepdims=True)
    var = jnp.mean(jnp.square(x - mu), -1, keepdims=True)
    return (x - mu) * lax.rsqrt(var + LN_EPS) * g + b


def _t5_bucket(n):
    n = jnp.maximum(n, 0)
    max_exact = REL_BUCKETS // 2
    nf = jnp.maximum(n, 1).astype(jnp.float32)
    large = max_exact + (jnp.log(nf / max_exact) / math.log(REL_MAX_DIST / max_exact)
                         * (REL_BUCKETS - max_exact)).astype(jnp.int32)
    large = jnp.minimum(large, REL_BUCKETS - 1)
    return jnp.where(n < max_exact, n, large)


RW_CHUNK = 64
RW_INV_BLOCK = 16
_NN = (((1,), (0,)), ((), ()))
_NT = (((1,), (1,)), ((), ()))
_BNN = (((2,), (1,)), ((0,), (0,)))
_BNT = (((2,), (2,)), ((0,), (0,)))


def _dot_f32(a, b, dims=_NN):
    return lax.dot_general(a, b, dims, precision=lax.Precision.HIGHEST,
                           preferred_element_type=jnp.float32)


def _dot_bf16x3(a, b, dims=_NN):
    f32, bf = jnp.float32, jnp.bfloat16
    a_hi, b_hi = a.astype(bf), b.astype(bf)
    a_lo = (a - a_hi.astype(f32)).astype(bf)
    b_lo = (b - b_hi.astype(f32)).astype(bf)
    out = lax.dot_general(a_hi, b_hi, dims, preferred_element_type=f32)
    out = out + lax.dot_general(a_hi, b_lo, dims, preferred_element_type=f32)
    return out + lax.dot_general(a_lo, b_hi, dims, preferred_element_type=f32)


def _bf16_terms(a):
    f32, bf = jnp.float32, jnp.bfloat16
    hi = a.astype(bf)
    r1 = a - hi.astype(f32)
    mid = r1.astype(bf)
    lo = (r1 - mid.astype(f32)).astype(bf)
    return hi, mid, lo


def _dot_lhs_split(a, b01):
    b = b01.astype(jnp.bfloat16)
    return sum(jnp.dot(t, b, preferred_element_type=jnp.float32) for t in _bf16_terms(a))


def _dot_rhs_split(a01, b):
    a = a01.astype(jnp.bfloat16)
    return sum(jnp.dot(a, t, preferred_element_type=jnp.float32) for t in _bf16_terms(b))


def _rwkv_kernel(z_ref, m0_ref, p0_ref, mu_ref, w0_ref, w2_ref, a0_ref, a2_ref, g2_ref, kk_ref, ka_ref, rk_ref,
                 gng_ref, gnb_ref, bd_ref, o_ref, mo_ref, po_ref, m_ref, prev_ref, y_ref):
    C = z_ref.shape[1]
    N = RW_HEAD_DIM
    f32 = jnp.float32
    dot3 = _dot_bf16x3

    @pl.when(pl.program_id(1) == 0)
    def _():
        m_ref[...] = m0_ref[0]
        prev_ref[...] = p0_ref[0]

    z = z_ref[0]
    row = lax.broadcasted_iota(jnp.int32, z.shape, 0)
    shifted = jnp.where(row == 0, prev_ref[...], pltpu.roll(z, 1, axis=0))
    prev_ref[...] = z[C - 1:C, :]
    zs = z + (shifted - z) * mu_ref[...]
    r = zs[:, 0:RW_DIM]
    k = zs[:, RW_DIM:2 * RW_DIM]
    v = zs[:, 2 * RW_DIM:3 * RW_DIM]
    o1 = 3 * RW_DIM
    wl = zs[:, o1:o1 + RW_DECAY_LORA]
    al = zs[:, o1 + RW_DECAY_LORA:o1 + RW_DECAY_LORA + RW_A_LORA]
    gl = zs[:, o1 + RW_DECAY_LORA + RW_A_LORA:]

    bd = bd_ref[...]
    log_w = -jax.nn.softplus(-(w0_ref[...] + dot3(jnp.tanh(wl), w2_ref[...]))) - 0.5
    ldec = -jnp.exp(log_w)
    a_lr = jax.nn.sigmoid(a0_ref[...] + dot3(al, a2_ref[...]))
    g = dot3(jax.nn.sigmoid(gl), g2_ref[...])
    kk = k * kk_ref[...]
    kk = kk * lax.rsqrt(jnp.maximum(_dot_lhs_split(kk * kk, bd), 1e-24))
    k2 = k * (1.0 + (a_lr - 1.0) * ka_ref[...])
    a_vec = -kk
    b_vec = kk * a_lr

    ti = lax.broadcasted_iota(jnp.int32, (C, C), 0)
    tj = lax.broadcasted_iota(jnp.int32, (C, C), 1)
    cum = _dot_rhs_split((ti >= tj).astype(f32), ldec)
    cum_last = cum[C - 1:C, :]
    w_incl = jnp.exp(cum)
    w_excl = jnp.exp(cum - ldec)
    w_inv = jnp.exp(-cum)
    w_end = jnp.exp(cum_last - cum)
    w_all = jnp.exp(cum_last)
    a_t = a_vec * w_excl
    r_t = r * w_incl
    b_t = b_vec * w_inv
    k_t = k2 * w_inv
    b_e = b_vec * w_end
    k_e = k2 * w_end

    strict = ti > tj
    incl = ti >= tj
    bi, bj = ti // RW_INV_BLOCK, tj // RW_INV_BLOCK
    same_blk = bi == bj
    pair_blk = jnp.logical_and((bi // 2) == (bj // 2), jnp.logical_not(same_blk))
    half_blk = (bi // 2) != (bj // 2)
    eye = (ti == tj).astype(f32)

    H = RW_HEADS
    heads = lambda x: jnp.stack([x[:, h * N:(h + 1) * N] for h in range(H)], axis=0)
    bmm = lambda a, b: dot3(a, b, _BNN)
    A, Rt, Bt, Kt, Be, Ke, V = (heads(t) for t in (a_t, r_t, b_t, k_t, b_e, k_e, v))
    gm = dot3(jnp.concatenate([A, Rt], axis=1), jnp.concatenate([Bt, Kt], axis=1), _BNT)
    a_ab = jnp.where(strict, gm[:, :C, :C], 0.0)
    a_ak = jnp.where(strict, gm[:, :C, C:], 0.0)
    a_rb = jnp.where(incl, gm[:, C:, :C], 0.0)
    a_rk = jnp.where(incl, gm[:, C:, C:], 0.0)
    d1 = jnp.where(same_blk, a_ab, 0.0)
    xinv = eye + d1
    d2 = bmm(d1, d1)
    xinv = xinv + bmm(xinv, d2)
    d4 = bmm(d2, d2)
    xinv = xinv + bmm(xinv, d4)
    d8 = bmm(d4, d4)
    xinv = xinv + bmm(xinv, d8)
    xinv = xinv + bmm(bmm(xinv, jnp.where(pair_blk, a_ab, 0.0)), xinv)
    xinv = xinv + bmm(bmm(xinv, jnp.where(half_blk, a_ab, 0.0)), xinv)

    av = bmm(jnp.concatenate([a_ak, a_rk], axis=1), V)
    p = bmm(xinv, jnp.concatenate([A, av[:, :C]], axis=2))
    qm = bmm(a_rb, p)
    q1 = Rt + qm[:, :, :N]
    q2 = qm[:, :, N:] + av[:, C:]
    gmat = bmm(jnp.swapaxes(Be, 1, 2), p)
    g1 = eye * heads(w_all) + gmat[:, :, :N]
    g2 = gmat[:, :, N:] + bmm(jnp.swapaxes(Ke, 1, 2), V)
    m = m_ref[...]
    yh = _dot_f32(q1, m, _BNN) + q2
    m_ref[...] = _dot_f32(g1, m, _BNN) + g2
    for h in range(H):
        y_ref[:, h * N:(h + 1) * N] = yh[h]

    y = y_ref[...]
    mean = _dot_lhs_split(y, bd) * (1.0 / N)
    yc = y - mean
    var = _dot_lhs_split(yc * yc, bd) * (1.0 / N)
    yn = yc * lax.rsqrt(var + RW_GN_EPS) * gng_ref[...] + gnb_ref[...]
    bonus = _dot_lhs_split(r * k2 * rk_ref[...], bd) * v
    o_ref[0] = ((yn + bonus) * g).astype(o_ref.dtype)

    @pl.when(pl.program_id(1) == pl.num_programs(1) - 1)
    def _():
        mo_ref[0] = m_ref[...]
        po_ref[0] = prev_ref[...]


def _rwkv7_init_carry(batch):
    return (jnp.zeros((batch, RW_HEADS, RW_HEAD_DIM, RW_HEAD_DIM), jnp.float32),
            jnp.zeros((batch, 1, RW_COLS), jnp.float32))


def _rwkv7_time_mix(z_rw, carry, mu, w0, w2, a0, a2, g2, k_k, k_a, r_k, gn_g, gn_b, out_dtype):
    B, S, _ = z_rw.shape
    C = RW_CHUNK
    assert S % C == 0 and C == RW_HEAD_DIM and C % (4 * RW_INV_BLOCK) == 0
    hid = jnp.arange(RW_DIM) // RW_HEAD_DIM
    bd = (hid[:, None] == hid[None, :]).astype(jnp.float32)
    row = lambda a: a.reshape(1, -1)
    full = lambda shape: pl.BlockSpec(shape, lambda b, c: (0,) * len(shape))
    state_spec = pl.BlockSpec((1, RW_HEADS, RW_HEAD_DIM, RW_HEAD_DIM), lambda b, c: (b, 0, 0, 0))
    prev_spec = pl.BlockSpec((1, 1, RW_COLS), lambda b, c: (b, 0, 0))
    y, m_out, p_out = pl.pallas_call(
        _rwkv_kernel,
        grid=(B, S // C),
        in_specs=[
            pl.BlockSpec((1, C, RW_COLS), lambda b, c: (b, c, 0)), state_spec, prev_spec,
            full((1, RW_COLS)), full((1, RW_DIM)), full((RW_DECAY_LORA, RW_DIM)), full((1, RW_DIM)),
            full((RW_A_LORA, RW_DIM)), full((RW_GATE_LORA, RW_DIM)), full((1, RW_DIM)), full((1, RW_DIM)),
            full((1, RW_DIM)), full((1, RW_DIM)), full((1, RW_DIM)), full((RW_DIM, RW_DIM)),
        ],
        out_specs=[pl.BlockSpec((1, C, RW_DIM), lambda b, c: (b, c, 0)), state_spec, prev_spec],
        out_shape=[jax.ShapeDtypeStruct((B, S, RW_DIM), out_dtype),
                   jax.ShapeDtypeStruct(carry[0].shape, jnp.float32),
                   jax.ShapeDtypeStruct(carry[1].shape, jnp.float32)],
        scratch_shapes=[
            pltpu.VMEM((RW_HEADS, RW_HEAD_DIM, RW_HEAD_DIM), jnp.float32),
            pltpu.VMEM((1, RW_COLS), jnp.float32),
            pltpu.VMEM((C, RW_DIM), jnp.float32),
        ],
        compiler_params=pltpu.CompilerParams(dimension_semantics=("parallel", "arbitrary")),
        name="rwkv7_time_mix",
    )(z_rw, carry[0], carry[1], row(mu), row(w0), w2, row(a0), a2, g2, row(k_k), row(k_a), row(r_k),
      row(gn_g), row(gn_b), bd)
    return y, (m_out, p_out)


DSA_TQ = 256
MASK_NEG = -1e30
INT_MIN = -2 ** 31
KEY_NEG_INF = -2139095041
THRESH_BITS = 32


def _dsa_prep_kernel(z_ref, kvg_ref, kig_ref, kib_ref, qi_ref, kv_ref, ki_ref):
    z = z_ref[0]
    qi_ref[0] = z[:, SMALL_QI:SMALL_KV].astype(qi_ref.dtype)
    kv = z[:, SMALL_KV:SMALL_KI]
    ms = jnp.mean(jnp.square(kv), -1, keepdims=True)
    kv_ref[0] = (kv * lax.rsqrt(ms + LN_EPS) * kvg_ref[...]).astype(kv_ref.dtype)
    ki = z[:, SMALL_KI:SMALL_WI]
    mu = jnp.mean(ki, -1, keepdims=True)
    var = jnp.mean(jnp.square(ki - mu), -1, keepdims=True)
    ki_ref[0] = ((ki - mu) * lax.rsqrt(var + LN_EPS) * kig_ref[...] + kib_ref[...]).astype(ki_ref.dtype)


def _dsa_prep(z_small, kv_g, ki_g, ki_b, tm=512):
    B, S, W = z_small.shape
    return pl.pallas_call(
        _dsa_prep_kernel,
        grid=(B, S // tm),
        in_specs=[
            pl.BlockSpec((1, tm, W), lambda b, i: (b, i, 0)),
            pl.BlockSpec((1, DSA_LATENT), lambda b, i: (0, 0)),
            pl.BlockSpec((1, IDX_DIM), lambda b, i: (0, 0)),
            pl.BlockSpec((1, IDX_DIM), lambda b, i: (0, 0)),
        ],
        out_specs=[
            pl.BlockSpec((1, tm, IDX_HEADS * IDX_DIM), lambda b, i: (b, i, 0)),
            pl.BlockSpec((1, tm, DSA_LATENT), lambda b, i: (b, i, 0)),
            pl.BlockSpec((1, tm, IDX_DIM), lambda b, i: (b, i, 0)),
        ],
        out_shape=[
            jax.ShapeDtypeStruct((B, S, IDX_HEADS * IDX_DIM), MM_DTYPE),
            jax.ShapeDtypeStruct((B, S, DSA_LATENT), MM_DTYPE),
            jax.ShapeDtypeStruct((B, S, IDX_DIM), MM_DTYPE),
        ],
        name="dsa_prep",
    )(z_small, kv_g[None], ki_g[None], ki_b[None])


def _sortable_key(s):
    s = jnp.where(s == 0.0, 0.0, s)
    bits = pltpu.bitcast(s, jnp.int32)
    return bits ^ ((bits >> 31) & 0x7FFFFFFF)


def _col_count(mask_i32):
    tk, tq = mask_i32.shape
    return jnp.sum(mask_i32.reshape(tk // 8, 8, tq), axis=0)


def _dsa_kernel(bfar_ref, q_ref, qi_ref, wit_ref, kv_ref, kvt_ref, ki_ref, b0_ref, b1_ref, o_ref,
                key_ref, madd_ref, m_ref, l_ref, acc_ref, *, topk, q_off):
    tq = q_ref.shape[1]
    tk = tq
    i = pl.program_id(1) + q_off
    nj = i + 1
    f32 = jnp.float32
    krow = lax.broadcasted_iota(jnp.int32, (tk, tq), 0)
    qcol = lax.broadcasted_iota(jnp.int32, (tk, tq), 1)

    qi = qi_ref[0]
    wit = wit_ref[0] * (IDX_HEADS ** -0.5)

    def score_chunk(j, carry):
        off = pl.multiple_of(j * tk, tk)
        kc = ki_ref[0, pl.ds(off, tk), :]
        s = jnp.zeros((tk, tq), f32)
        for h in range(IDX_HEADS):
            d = lax.dot_general(kc, qi[:, h * IDX_DIM:(h + 1) * IDX_DIM],
                                (((1,), (1,)), ((), ())), preferred_element_type=f32)
            s = s + wit[h:h + 1, :] * jnp.maximum(d * (IDX_DIM ** -0.5), 0.0)
        causal = (krow + j * tk) <= (qcol + i * tq)
        s = jnp.where(causal, s, -jnp.inf)
        key_ref[j] = _sortable_key(s)
        return carry

    lax.fori_loop(0, nj, score_chunk, 0)

    def count_where(pred_fn):
        def body(j, acc):
            return acc + _col_count(pred_fn(key_ref[j], j).astype(jnp.int32))
        acc = lax.fori_loop(0, nj, body, jnp.zeros((8, tq), jnp.int32))
        return jnp.sum(acc, axis=0, keepdims=True)

    def bit_step(it, t_u):
        bit = THRESH_BITS - 1 - it
        cand_u = t_u | jnp.left_shift(jnp.int32(1), bit)
        cand = cand_u ^ INT_MIN
        cnt = count_where(lambda k, j: k >= cand)
        return jnp.where(cnt >= topk, cand_u, t_u)

    t_u = lax.fori_loop(0, THRESH_BITS, bit_step, jnp.zeros((1, tq), jnp.int32))
    thr = t_u ^ INT_MIN
    cnt_gt = count_where(lambda k, j: k > thr)
    cnt_ge = count_where(lambda k, j: k >= thr)
    is_neg = thr == KEY_NEG_INF
    need = jnp.logical_and(cnt_ge > topk, jnp.logical_not(is_neg))
    n_tie_take = topk - cnt_gt

    thr_open = jnp.where(is_neg, thr, thr - 1)
    any_need = jnp.max(need.astype(jnp.int32)) > 0

    @pl.when(jnp.logical_not(any_need))
    def _():
        def body(j, carry):
            madd_ref[j] = jnp.where(key_ref[j] > thr_open, 0.0, MASK_NEG)
            return carry
        lax.fori_loop(0, nj, body, 0)

    @pl.when(any_need)
    def _():
        s_len = tk * key_ref.shape[0]
        n_bits = max(1, int(math.ceil(math.log2(s_len))))

        def idx_step(it, p):
            bit = n_bits - 1 - it
            cand = p | jnp.left_shift(jnp.int32(1), bit)
            cnt = count_where(
                lambda k, j: jnp.where(k == thr, jnp.where((krow + j * tk) < cand, 1, 0), 0))
            return jnp.where(cnt < n_tie_take, cand, p)

        p_idx = lax.fori_loop(0, n_bits, idx_step, jnp.zeros((1, tq), jnp.int32))
        p_idx = jnp.where(need, p_idx, jnp.where(is_neg, -1, s_len))

        def body(j, carry):
            k = key_ref[j]
            tie_ok = jnp.where((krow + j * tk) <= p_idx, 0.0, MASK_NEG)
            madd_ref[j] = jnp.where(k > thr, 0.0, jnp.where(k == thr, tie_ok, MASK_NEG))
            return carry
        lax.fori_loop(0, nj, body, 0)

    H, d = DSA_HEADS, DSA_LATENT
    m_ref[...] = jnp.full(m_ref.shape, MASK_NEG, f32)
    l_ref[...] = jnp.zeros(l_ref.shape, f32)
    acc_ref[...] = jnp.zeros(acc_ref.shape, f32)
    scale = DSA_LATENT ** -0.5
    q_all = jnp.concatenate([q_ref[0, :, h * d:(h + 1) * d] for h in range(H)], axis=0)

    def attend(j, bias):
        off = pl.multiple_of(j * tk, tk)
        kc = kv_ref[0, pl.ds(off, tk), :]
        kct = kvt_ref[0, :, pl.ds(off, tk)]
        lg = lax.dot_general(kc, q_all, (((1,), (1,)), ((), ())), preferred_element_type=f32)
        lg = lg * scale + bias + jnp.tile(madd_ref[j], (1, H))
        m_old = m_ref[...]
        m_new = jnp.maximum(m_old, jnp.max(lg, axis=0, keepdims=True))
        alpha = jnp.exp(m_old - m_new)
        p = jnp.exp(lg - m_new)
        l_ref[...] = alpha * l_ref[...] + jnp.sum(p, axis=0, keepdims=True)
        acc_ref[...] = alpha * acc_ref[...] + jnp.dot(kct, p.astype(kct.dtype), preferred_element_type=f32)
        m_ref[...] = m_new

    def far_body(j, carry):
        attend(j, bfar_ref[...])
        return carry

    lax.fori_loop(0, jnp.maximum(i - 1, 0), far_body, 0)

    @pl.when(i >= 1)
    def _():
        attend(i - 1, b1_ref[...])

    attend(i, b0_ref[...])

    out_t = acc_ref[...] / l_ref[...]
    for h in range(H):
        o_ref[0, :, h * d:(h + 1) * d] = out_t[:, h * tq:(h + 1) * tq].T.astype(o_ref.dtype)


def _dsa_bias_tiles(rel_bias, tq):
    dist = jnp.arange(2 * tq, dtype=jnp.int32)
    by_dist = rel_bias[_t5_bucket(dist)].T

    def toeplitz(v):
        H, P = v.shape
        skew = jnp.tile(v, (1, tq + 1))[:, :tq * (P + 1)].reshape(H, tq, P + 1)
        return skew[:, ::-1, :tq]

    b0 = toeplitz(jnp.concatenate([jnp.repeat(by_dist[:, :1], tq - 1, axis=1), by_dist[:, :tq + 1]], axis=1))
    b1 = toeplitz(jnp.concatenate([by_dist[:, 1:], by_dist[:, -1:]], axis=1))
    H = by_dist.shape[0]
    b0 = jnp.transpose(b0, (1, 0, 2)).reshape(tq, H * tq)
    b1 = jnp.transpose(b1, (1, 0, 2)).reshape(tq, H * tq)
    return b0, b1, jnp.repeat(by_dist[:, -1], tq)[None, :]


def _dsa_attention(q, qi, z_wi_t, ckv, kidx, bias_tiles, topk, q_off, out_dtype):
    B, Sq, _ = q.shape
    tq = DSA_TQ
    nq = Sq // tq
    S = ckv.shape[1]
    assert Sq % tq == 0 and S == (q_off + nq) * tq and tq >= REL_MAX_DIST and topk <= tq
    ckv_t = jnp.swapaxes(ckv, 1, 2)
    b0, b1, bfar = bias_tiles
    nk = S // tq
    H, d = DSA_HEADS, DSA_LATENT
    f32_bytes = 4
    vmem_estimate = (
        2 * nk * tq * tq * f32_bytes
        + 2 * 2 * tq * H * tq * f32_bytes
        + 2 * (2 * S * d + S * LANES) * 2
        + 6 * tq * H * tq * f32_bytes
        + 4 * tq * H * d * f32_bytes
    )
    grid_spec = pltpu.PrefetchScalarGridSpec(
        num_scalar_prefetch=0,
        grid=(B, nq),
        in_specs=[
            pl.BlockSpec((1, H * tq), lambda b, i: (0, 0)),
            pl.BlockSpec((1, tq, H * d), lambda b, i: (b, i, 0)),
            pl.BlockSpec((1, tq, IDX_HEADS * IDX_DIM), lambda b, i: (b, i, 0)),
            pl.BlockSpec((1, IDX_HEADS, tq), lambda b, i: (b, 0, i)),
            pl.BlockSpec((1, S, d), lambda b, i: (b, 0, 0)),
            pl.BlockSpec((1, d, S), lambda b, i: (b, 0, 0)),
            pl.BlockSpec((1, S, IDX_DIM), lambda b, i: (b, 0, 0)),
            pl.BlockSpec((tq, H * tq), lambda b, i: (0, 0)),
            pl.BlockSpec((tq, H * tq), lambda b, i: (0, 0)),
        ],
        out_specs=pl.BlockSpec((1, tq, H * d), lambda b, i: (b, i, 0)),
        scratch_shapes=[
            pltpu.VMEM((nk, tq, tq), jnp.int32),
            pltpu.VMEM((nk, tq, tq), jnp.float32),
            pltpu.VMEM((1, H * tq), jnp.float32),
            pltpu.VMEM((1, H * tq), jnp.float32),
            pltpu.VMEM((d, H * tq), jnp.float32),
        ],
    )
    return pl.pallas_call(
        partial(_dsa_kernel, topk=topk, q_off=q_off),
        grid_spec=grid_spec,
        out_shape=jax.ShapeDtypeStruct((B, Sq, H * d), out_dtype),
        compiler_params=pltpu.CompilerParams(vmem_limit_bytes=vmem_estimate),
        name="dsa_attention",
    )(bfar, q, qi, z_wi_t, ckv, ckv_t, kidx, b0, b1)


SC_CORES = 2
SC_SUBCORES = 16
SC_LANES = 16
PEER_SC_TOKENS = 8
PEER_SC_RING = 4
PEER_SC_UNROLL = 4
PEER_SLOTS = PEER_HEADS * PEER_TOPK


def _peer_sc_call(body, T, out_width, stage_width):
    mesh = plsc.VectorSubcoreMesh(core_axis_name="c", subcore_axis_name="s")
    return pl.kernel(
        body, mesh=mesh,
        out_type=jax.ShapeDtypeStruct((T, out_width), jnp.float32),
        scratch_types=[
            pltpu.VMEM((PEER_SC_TOKENS, PEER_SLOTS), jnp.int32),
            pltpu.VMEM((PEER_SC_TOKENS, stage_width), jnp.float32),
            pltpu.VMEM((PEER_SC_RING, PEER_TOPK, D_MODEL), jnp.float32),
            pltpu.VMEM((PEER_SC_TOKENS, out_width), jnp.float32),
            pltpu.SemaphoreType.DMA((PEER_SC_RING,)),
        ],
        compiler_params=pltpu.CompilerParams(needs_layout_passes=False),
    )


def _peer_sc_body(compute, zero_out, x_hbm, idx_hbm, tab_hbm, out_hbm, idx_v, x_v, rows_v, out_v, sems):
    T = idx_hbm.shape[0]
    tokens_per_worker = T // (SC_CORES * SC_SUBCORES)
    n_steps = PEER_SC_TOKENS * PEER_HEADS
    worker = lax.axis_index("s") * SC_CORES + lax.axis_index("c")
    base = worker * tokens_per_worker

    def gather(s, b):
        ids = idx_v[s // PEER_HEADS, pl.ds((s % PEER_HEADS) * PEER_TOPK, PEER_TOPK)]
        return pltpu.make_async_copy(tab_hbm.at[ids], rows_v.at[b], sems.at[b])

    @pl.loop(0, tokens_per_worker // PEER_SC_TOKENS)
    def _(blk):
        tok0 = base + blk * PEER_SC_TOKENS
        pltpu.sync_copy(idx_hbm.at[pl.ds(tok0, PEER_SC_TOKENS)], idx_v)
        pltpu.sync_copy(x_hbm.at[pl.ds(tok0, PEER_SC_TOKENS)], x_v)
        for b in range(PEER_SC_RING - 1):
            gather(b, b).start()
        if zero_out:
            @pl.loop(0, PEER_SC_TOKENS)
            def _(t):
                @plsc.parallel_loop(0, out_v.shape[1] // SC_LANES, unroll=PEER_SC_UNROLL)
                def _(c):
                    out_v[t, pl.ds(pl.multiple_of(c * SC_LANES, SC_LANES), SC_LANES)] = (
                        jnp.zeros((SC_LANES,), jnp.float32))

        @pl.loop(0, n_steps, step=PEER_SC_RING)
        def _(s0):
            for b in range(PEER_SC_RING):
                s = s0 + b
                gather(s, b).wait()

                @pl.when(s + PEER_SC_RING - 1 < n_steps)
                def _():
                    gather(s + PEER_SC_RING - 1, (b + PEER_SC_RING - 1) % PEER_SC_RING).start()

                compute(s // PEER_HEADS, s % PEER_HEADS, b, x_v, rows_v, out_v)

        pltpu.sync_copy(out_v, out_hbm.at[pl.ds(tok0, PEER_SC_TOKENS)])


def _peer_dots_compute(t, hd, b, h_v, rows_v, dots_v):
    lane = lax.iota(jnp.int32, SC_LANES)

    def col_step(c, accs):
        off = pl.multiple_of(c * SC_LANES, SC_LANES)
        hv = h_v[t, pl.ds(off, SC_LANES)]
        return tuple(accs[r] + rows_v[b, r, pl.ds(off, SC_LANES)] * hv for r in range(PEER_TOPK))

    accs = plsc.parallel_loop(
        0, D_MODEL // SC_LANES, unroll=PEER_SC_UNROLL,
        carry=tuple(jnp.zeros((SC_LANES,), jnp.float32) for _ in range(PEER_TOPK)))(col_step)
    res = jnp.zeros((SC_LANES,), jnp.float32)
    for r in range(PEER_TOPK):
        res = jnp.where(lane == r, jnp.sum(accs[r]), res)
    dots_v[t, pl.ds(hd * PEER_TOPK, PEER_TOPK)] = res


def _peer_mix_compute(t, hd, b, act_v, rows_v, out_v):
    tvec = jnp.full((SC_LANES,), t, jnp.int32)
    weights = [plsc.load_gather(act_v, [tvec, jnp.full((SC_LANES,), hd * PEER_TOPK + r, jnp.int32)])
               for r in range(PEER_TOPK)]

    @plsc.parallel_loop(0, D_MODEL // SC_LANES, unroll=PEER_SC_UNROLL)
    def _(c):
        off = pl.multiple_of(c * SC_LANES, SC_LANES)
        a = rows_v[b, 0, pl.ds(off, SC_LANES)] * weights[0]
        for r in range(1, PEER_TOPK):
            a = a + rows_v[b, r, pl.ds(off, SC_LANES)] * weights[r]
        plsc.addupdate(out_v.at[t, pl.ds(off, SC_LANES)], a)


def _peer_expert_dots(h, experts, u_tab):
    T = h.shape[0]
    assert T % (SC_CORES * SC_SUBCORES * PEER_SC_TOKENS) == 0
    assert (PEER_SC_TOKENS * PEER_HEADS) % PEER_SC_RING == 0 and PEER_TOPK == SC_LANES
    body = partial(_peer_sc_body, _peer_dots_compute, False)
    return _peer_sc_call(body, T, PEER_SLOTS, D_MODEL)(h, experts, u_tab)


def _peer_expert_mix(act, experts, v_tab):
    T = act.shape[0]
    assert T % (SC_CORES * SC_SUBCORES * PEER_SC_TOKENS) == 0
    body = partial(_peer_sc_body, _peer_mix_compute, True)
    return _peer_sc_call(body, T, D_MODEL, PEER_SLOTS)(act, experts, v_tab)


PEER_TM = 256


def _extract_topk(s, k, payload=None):
    R = s.shape[0]
    riota = lax.broadcasted_iota(jnp.int32, s.shape, 0)
    vals, rows = [], []
    for _ in range(k):
        m = jnp.max(s, axis=0, keepdims=True)
        pos = jnp.min(jnp.where(s == m, riota, R), axis=0, keepdims=True)
        hit = riota == pos
        vals.append(m)
        rows.append(pos if payload is None else jnp.max(jnp.where(hit, payload, -1), axis=0, keepdims=True))
        s = jnp.where(hit, -jnp.inf, s)
    return jnp.concatenate(vals, axis=0), jnp.concatenate(rows, axis=0)


def _peer_route_kernel(x_ref, sc_ref, sh_ref, wq_ref, keys_ref, h_ref, ex_ref, gate_ref, q_ref, ext_ref, gt_ref):
    K = PEER_TOPK
    f32 = jnp.float32
    h = x_ref[0] * (1.0 + sc_ref[0]) + sh_ref[0]
    h_ref[0] = h
    q = jnp.dot(h.astype(wq_ref.dtype), wq_ref[...], preferred_element_type=f32)
    for hd in range(PEER_HEADS):
        q_ref[hd] = q[:, hd * PEER_KEY_DIM:(hd + 1) * PEER_KEY_DIM].astype(q_ref.dtype)

    def head_body(hd, carry):
        qh = q_ref[hd]
        s1 = lax.dot_general(keys_ref[hd, 0], qh[:, :PEER_HALF], _NT, preferred_element_type=f32)
        s2 = lax.dot_general(keys_ref[hd, 1], qh[:, PEER_HALF:], _NT, preferred_element_type=f32)
        v1, i1 = _extract_topk(s1, K)
        v2, i2 = _extract_topk(s2, K)
        tm = v1.shape[1]
        cand_rows, cidx_rows = [], []
        for a in range(K):
            nb = K // (a + 1)
            cand_rows.append(v1[a:a + 1, :] + v2[:nb, :])
            cidx_rows.append(i1[a:a + 1, :] * PEER_N_KEYS + i2[:nb, :])
        n_pad = -sum(r.shape[0] for r in cand_rows) % 8
        cand_rows.append(jnp.full((n_pad, tm), -jnp.inf, f32))
        cidx_rows.append(jnp.full((n_pad, tm), -1, jnp.int32))
        cand = jnp.concatenate(cand_rows, axis=0)
        cidx = jnp.concatenate(cidx_rows, axis=0)
        top_s, experts = _extract_topk(cand, K, payload=cidx)
        e = jnp.exp(top_s - top_s[0:1, :])
        gt_ref[pl.ds(pl.multiple_of(hd * K, K), K), :] = e / jnp.sum(e, axis=0, keepdims=True)
        ext_ref[pl.ds(pl.multiple_of(hd * K, K), K), :] = experts
        return carry

    lax.fori_loop(0, PEER_HEADS, head_body, 0)
    ex_ref[0] = ext_ref[...].T
    gate_ref[0] = gt_ref[...].T


def _peer_route(x, sc, sh, w_pq, sub_keys):
    B, S, D = x.shape
    tm = PEER_TM
    return pl.pallas_call(
        _peer_route_kernel,
        grid=(B, S // tm),
        in_specs=[
            pl.BlockSpec((1, tm, D), lambda b, i: (b, i, 0)),
            pl.BlockSpec((1, 1, D), lambda b, i: (b, 0, 0)),
            pl.BlockSpec((1, 1, D), lambda b, i: (b, 0, 0)),
            pl.BlockSpec((D, PEER_HEADS * PEER_KEY_DIM), lambda b, i: (0, 0)),
            pl.BlockSpec((PEER_HEADS, 2, PEER_N_KEYS, PEER_HALF), lambda b, i: (0, 0, 0, 0)),
        ],
        out_specs=[
            pl.BlockSpec((1, tm, D), lambda b, i: (b, i, 0)),
            pl.BlockSpec((1, tm, PEER_SLOTS), lambda b, i: (b, i, 0)),
            pl.BlockSpec((1, tm, PEER_SLOTS), lambda b, i: (b, i, 0)),
        ],
        out_shape=[
            jax.ShapeDtypeStruct((B, S, D), jnp.float32),
            jax.ShapeDtypeStruct((B, S, PEER_SLOTS), jnp.int32),
            jax.ShapeDtypeStruct((B, S, PEER_SLOTS), jnp.float32),
        ],
        scratch_shapes=[
            pltpu.VMEM((PEER_HEADS, tm, PEER_KEY_DIM), MM_DTYPE),
            pltpu.VMEM((PEER_SLOTS, tm), jnp.int32),
            pltpu.VMEM((PEER_SLOTS, tm), jnp.float32),
        ],
        compiler_params=pltpu.CompilerParams(dimension_semantics=("parallel", "parallel")),
        name="peer_route",
    )(x, sc[:, None, :], sh[:, None, :], w_pq.astype(MM_DTYPE), sub_keys.astype(MM_DTYPE))


def _peer_pre(h2, experts, u_tab):
    B, S, D = h2.shape
    return _peer_expert_dots(h2.reshape(B * S, D), experts.reshape(B * S, PEER_SLOTS), u_tab)


def _peer_act(pre, gates):
    return jax.nn.gelu(pre, approximate=False) * gates.reshape(pre.shape)


def _peer_out(act, experts, v_tab):
    B, S, _ = experts.shape
    return _peer_expert_mix(act, experts.reshape(B * S, PEER_SLOTS), v_tab).reshape(B, S, D_MODEL)


def kernel(x, c, w_ada, b_ada, w_in, rw_mu, rw_w0, rw_w2, rw_a0, rw_a2, rw_g2, rw_k_k, rw_k_a, rw_r_k, rw_gn_g, rw_gn_b, dsa_kv_g, idx_k_g, idx_k_b, rel_bias, w_br_a, w_br_b, w_out, ln1_g, ln1_b, peer_wq, peer_keys, peer_u, peer_v, ln2_g, ln2_b):
    l = 0
    mod = jax.nn.silu(c) @ w_ada[l] + b_ada[l]

    w_rw, w_q, w_kv, w_qi, w_ki, w_wi, w_ga, w_gb = jnp.split(w_in[l], _split_points(IN_SIZES), axis=-1)
    small_pad = jnp.zeros((D_MODEL, SMALL_COLS - SMALL_WI - IDX_HEADS), w_in.dtype)
    w_small = jnp.concatenate([w_qi, w_kv, w_ki, w_wi, small_pad], axis=-1)
    w_gates = jnp.concatenate([w_ga, w_gb], axis=-1)

    B, S, _ = x.shape
    seg_len = S // SEQ_SEGMENTS
    dsa_topk = min(TOPK_MAX, S // 4)
    bias_tiles = _dsa_bias_tiles(rel_bias, DSA_TQ)

    def project_and_dsa(x, sc1, sh1, seg, keys_so_far):
        z_rw = _mod_matmul(x, sc1, sh1, w_rw, jnp.float32, tn=896)
        z_q = _mod_matmul(x, sc1, sh1, w_q, MM_DTYPE, tn=1024)
        z_small = _mod_matmul(x, sc1, sh1, w_small, jnp.float32, tn=512)
        z_g = _mod_matmul(x, sc1, sh1, w_gates, jnp.float32, tn=1024)
        z_wi_t = jnp.swapaxes(z_small[..., SMALL_WI:SMALL_WI + IDX_HEADS], 1, 2)
        qi, ckv, kidx = _dsa_prep(z_small, dsa_kv_g[l], idx_k_g[l], idx_k_b[l])
        if keys_so_far is not None:
            ckv = jnp.concatenate([keys_so_far[0], ckv], axis=1)
            kidx = jnp.concatenate([keys_so_far[1], kidx], axis=1)
        y_b = _dsa_attention(z_q, qi, z_wi_t, ckv, kidx, bias_tiles, dsa_topk,
                             seg * (seg_len // DSA_TQ), jnp.float32) @ w_br_b[l]
        return (y_b, z_rw, z_g), (ckv, kidx)

    def rwkv_and_merge(x, gt1, carry, y_b, z_rw, z_g):
        z_ga, z_gb = z_g[..., :D_MODEL], z_g[..., D_MODEL:]
        y_a, carry = _rwkv7_time_mix(z_rw, carry, rw_mu[l], rw_w0[l], rw_w2[l], rw_a0[l], rw_a2[l], rw_g2[l],
                                     rw_k_k[l], rw_k_a[l], rw_r_k[l], rw_gn_g[l], rw_gn_b[l], jnp.float32)
        merged = jax.nn.sigmoid(z_ga) * (y_a @ w_br_a[l]) + jax.nn.sigmoid(z_gb) * y_b
        mix = merged @ w_out[l]
        return _layer_norm(DEEPNORM_ALPHA * x + gt1[:, None] * mix, ln1_g[l], ln1_b[l]), carry

    def finish(st, act):
        y2 = _peer_out(act, st["experts"], peer_v[l])
        return _layer_norm(DEEPNORM_ALPHA * st["x1"] + st["gt2"][:, None] * y2, ln2_g[l], ln2_b[l])

    gsz = B // BATCH_GROUPS
    outs = []
    prev = None
    for g in range(BATCH_GROUPS):
        sh1, sc1, gt1, sh2, sc2, gt2 = jnp.split(mod[g * gsz:(g + 1) * gsz], 6, axis=-1)
        rw_carry = _rwkv7_init_carry(gsz)
        keys_so_far = None
        for seg in range(SEQ_SEGMENTS):
            x_g = x[g * gsz:(g + 1) * gsz, seg * seg_len:(seg + 1) * seg_len]
            if prev is not None:
                x_g, prev["experts"] = lax.optimization_barrier((x_g, prev["experts"]))
            proj, keys_so_far = project_and_dsa(x_g, sc1, sh1, seg, keys_so_far)
            if prev is not None:
                proj, prev["pre"], outs = lax.optimization_barrier((proj, prev["pre"], outs))
                outs.append(finish(prev, _peer_act(prev["pre"], prev["gates"])))
            x1, rw_carry = rwkv_and_merge(x_g, gt1, rw_carry, *proj)
            h2, experts, gates = _peer_route(x1, sc2, sh2, peer_wq[l], peer_keys[l])
            prev = dict(x1=x1, gt2=gt2, experts=experts, gates=gates, pre=_peer_pre(h2, experts, peer_u[l]))
    prev["pre"], outs = lax.optimization_barrier((prev["pre"], outs))
    outs.append(finish(prev, _peer_act(prev["pre"], prev["gates"])))
    rows = [jnp.concatenate(outs[g * SEQ_SEGMENTS:(g + 1) * SEQ_SEGMENTS], axis=1) for g in range(BATCH_GROUPS)]
    return jnp.concatenate(rows, axis=0)
```

```python
import math
from functools import partial

import jax
import jax.numpy as jnp
import numpy as np
from jax import lax
from jax.experimental import pallas as pl
from jax.experimental.pallas import tpu as pltpu
from jax.experimental.pallas import tpu_sc as plsc

D_MODEL = 1024
RW_HEADS = 8
RW_HEAD_DIM = 64
RW_DIM = 512
RW_DECAY_LORA = 64
RW_A_LORA = 64
RW_GATE_LORA = 128
RW_COLS = 3 * RW_DIM + RW_DECAY_LORA + RW_A_LORA + RW_GATE_LORA
RW_GN_EPS = 64e-5
DSA_HEADS = 8
DSA_LATENT = 128
DSA_Q_DIM = DSA_HEADS * DSA_LATENT
IDX_HEADS = 4
IDX_DIM = 64
TOPK_MAX = 256
REL_BUCKETS = 32
REL_MAX_DIST = 128
IN_SIZES = (RW_COLS, DSA_Q_DIM, DSA_LATENT, IDX_HEADS * IDX_DIM, IDX_DIM, IDX_HEADS, D_MODEL, D_MODEL)
IN_COLS = sum(IN_SIZES)
PEER_HEADS = 8
PEER_N_KEYS = 128
PEER_KEY_DIM = 128
PEER_HALF = 64
PEER_TOPK = 16
LN_EPS = 1e-5
DEPTH = 1
DEEPNORM_ALPHA = (2.0 * DEPTH) ** 0.25

LANES = 128
SMALL_QI = 0
SMALL_KV = SMALL_QI + IDX_HEADS * IDX_DIM
SMALL_KI = SMALL_KV + DSA_LATENT
SMALL_WI = SMALL_KI + IDX_DIM
SMALL_COLS = 4 * LANES
MM_DTYPE = jnp.bfloat16
BATCH_GROUPS = 8
SEQ_SEGMENTS = 2


def _split_points(sizes):
    return np.cumsum(sizes)[:-1].tolist()


def _mod_matmul_kernel(x_ref, sc_ref, sh_ref, w_ref, o_ref):
    h = x_ref[0] * (1.0 + sc_ref[0]) + sh_ref[0]
    o_ref[0] = jnp.dot(h.astype(w_ref.dtype), w_ref[...],
                       preferred_element_type=jnp.float32).astype(o_ref.dtype)


def _mod_matmul(x, sc, sh, w, out_dtype, tn, tm=512):
    B, S, D = x.shape
    N = w.shape[1]
    return pl.pallas_call(
        _mod_matmul_kernel,
        grid=(B, S // tm, N // tn),
        in_specs=[
            pl.BlockSpec((1, tm, D), lambda b, i, j: (b, i, 0)),
            pl.BlockSpec((1, 1, D), lambda b, i, j: (b, 0, 0)),
            pl.BlockSpec((1, 1, D), lambda b, i, j: (b, 0, 0)),
            pl.BlockSpec((D, tn), lambda b, i, j: (0, j)),
        ],
        out_specs=pl.BlockSpec((1, tm, tn), lambda b, i, j: (b, i, j)),
        out_shape=jax.ShapeDtypeStruct((B, S, N), out_dtype),
        name="mod_matmul",
    )(x, sc[:, None, :], sh[:, None, :], w.astype(MM_DTYPE))


ROW_TILE = 256


def _layer_norm(x, g, b):
    mu = jnp.mean(x, -1, keepdims=True)
    var = jnp.mean(jnp.square(x - mu), -1, keepdims=True)
    return (x - mu) * lax.rsqrt(var + LN_EPS) * g + b


def _merge_ln_kernel(x_ref, gt_ref, ya_ref, yb_ref, zg_ref, wa_ref, wb_ref, wo_ref, g_ref, b_ref, o_ref):
    f32 = jnp.float32
    D = x_ref.shape[2]
    ya = jnp.dot(ya_ref[0], wa_ref[...], preferred_element_type=f32)
    yb = jnp.dot(yb_ref[0], wb_ref[...], preferred_element_type=f32)
    zg = zg_ref[0]
    merged = jax.nn.sigmoid(zg[:, :D]) * ya + jax.nn.sigmoid(zg[:, D:]) * yb
    mix = jnp.dot(merged.astype(wo_ref.dtype), wo_ref[...], preferred_element_type=f32)
    o_ref[0] = _layer_norm(DEEPNORM_ALPHA * x_ref[0] + gt_ref[0] * mix, g_ref[...], b_ref[...])


def _merge_ln(x, gt, y_a, y_b, z_g, w_a, w_b, w_o, ln_g, ln_b):
    B, S, D = x.shape
    tm = ROW_TILE
    tok = lambda width: pl.BlockSpec((1, tm, width), lambda b, i: (b, i, 0))
    full = lambda a: pl.BlockSpec(a.shape, lambda b, i: (0,) * a.ndim)
    w_a, w_b, w_o = (w.astype(MM_DTYPE) for w in (w_a, w_b, w_o))
    ln_g, ln_b = ln_g[None], ln_b[None]
    return pl.pallas_call(
        _merge_ln_kernel,
        grid=(B, S // tm),
        in_specs=[tok(D), pl.BlockSpec((1, 1, D), lambda b, i: (b, 0, 0)), tok(y_a.shape[2]), tok(y_b.shape[2]),
                  tok(z_g.shape[2]), full(w_a), full(w_b), full(w_o), full(ln_g), full(ln_b)],
        out_specs=tok(D),
        out_shape=jax.ShapeDtypeStruct((B, S, D), jnp.float32),
        compiler_params=pltpu.CompilerParams(dimension_semantics=("parallel", "parallel")),
        name="merge_ln",
    )(x, gt[:, None, :], y_a, y_b, z_g, w_a, w_b, w_o, ln_g, ln_b)


def _residual_ln_kernel(x_ref, gt_ref, y_ref, g_ref, b_ref, o_ref):
    o_ref[0] = _layer_norm(DEEPNORM_ALPHA * x_ref[0] + gt_ref[0] * y_ref[0], g_ref[...], b_ref[...])


def _residual_ln(x, gt, y, ln_g, ln_b):
    B, S, D = x.shape
    tm = ROW_TILE
    tok = pl.BlockSpec((1, tm, D), lambda b, i: (b, i, 0))
    vec = pl.BlockSpec((1, D), lambda b, i: (0, 0))
    return pl.pallas_call(
        _residual_ln_kernel,
        grid=(B, S // tm),
        in_specs=[tok, pl.BlockSpec((1, 1, D), lambda b, i: (b, 0, 0)), tok, vec, vec],
        out_specs=tok,
        out_shape=jax.ShapeDtypeStruct((B, S, D), jnp.float32),
        compiler_params=pltpu.CompilerParams(dimension_semantics=("parallel", "parallel")),
        name="residual_ln",
    )(x, gt[:, None, :], y, ln_g[None], ln_b[None])


def _t5_bucket(n):
    n = jnp.maximum(n, 0)
    max_exact = REL_BUCKETS // 2
    nf = jnp.maximum(n, 1).astype(jnp.float32)
    large = max_exact + (jnp.log(nf / max_exact) / math.log(REL_MAX_DIST / max_exact)
                         * (REL_BUCKETS - max_exact)).astype(jnp.int32)
    large = jnp.minimum(large, REL_BUCKETS - 1)
    return jnp.where(n < max_exact, n, large)


RW_CHUNK = 64
RW_INV_BLOCK = 16
_NN = (((1,), (0,)), ((), ()))
_NT = (((1,), (1,)), ((), ()))
_BNN = (((2,), (1,)), ((0,), (0,)))
_BNT = (((2,), (2,)), ((0,), (0,)))


def _dot_f32(a, b, dims=_NN):
    return lax.dot_general(a, b, dims, precision=lax.Precision.HIGHEST,
                           preferred_element_type=jnp.float32)


def _dot_bf16x3(a, b, dims=_NN):
    f32, bf = jnp.float32, jnp.bfloat16
    a_hi, b_hi = a.astype(bf), b.astype(bf)
    a_lo = (a - a_hi.astype(f32)).astype(bf)
    b_lo = (b - b_hi.astype(f32)).astype(bf)
    out = lax.dot_general(a_hi, b_hi, dims, preferred_element_type=f32)
    out = out + lax.dot_general(a_hi, b_lo, dims, preferred_element_type=f32)
    return out + lax.dot_general(a_lo, b_hi, dims, preferred_element_type=f32)


def _bf16_terms(a):
    f32, bf = jnp.float32, jnp.bfloat16
    hi = a.astype(bf)
    r1 = a - hi.astype(f32)
    mid = r1.astype(bf)
    lo = (r1 - mid.astype(f32)).astype(bf)
    return hi, mid, lo


def _dot_lhs_split(a, b01):
    b = b01.astype(jnp.bfloat16)
    return sum(jnp.dot(t, b, preferred_element_type=jnp.float32) for t in _bf16_terms(a))


def _dot_rhs_split(a01, b):
    a = a01.astype(jnp.bfloat16)
    return sum(jnp.dot(a, t, preferred_element_type=jnp.float32) for t in _bf16_terms(b))


def _rwkv_kernel(z_ref, m0_ref, p0_ref, mu_ref, w0_ref, w2_ref, a0_ref, a2_ref, g2_ref, kk_ref, ka_ref, rk_ref,
                 gng_ref, gnb_ref, bd_ref, o_ref, mo_ref, po_ref, m_ref, prev_ref, y_ref):
    C = z_ref.shape[1]
    N = RW_HEAD_DIM
    f32 = jnp.float32
    dot3 = _dot_bf16x3

    @pl.when(pl.program_id(1) == 0)
    def _():
        m_ref[...] = m0_ref[0]
        prev_ref[...] = p0_ref[0]

    z = z_ref[0]
    row = lax.broadcasted_iota(jnp.int32, z.shape, 0)
    shifted = jnp.where(row == 0, prev_ref[...], pltpu.roll(z, 1, axis=0))
    prev_ref[...] = z[C - 1:C, :]
    zs = z + (shifted - z) * mu_ref[...]
    r = zs[:, 0:RW_DIM]
    k = zs[:, RW_DIM:2 * RW_DIM]
    v = zs[:, 2 * RW_DIM:3 * RW_DIM]
    o1 = 3 * RW_DIM
    wl = zs[:, o1:o1 + RW_DECAY_LORA]
    al = zs[:, o1 + RW_DECAY_LORA:o1 + RW_DECAY_LORA + RW_A_LORA]
    gl = zs[:, o1 + RW_DECAY_LORA + RW_A_LORA:]

    bd = bd_ref[...]
    log_w = -jax.nn.softplus(-(w0_ref[...] + dot3(jnp.tanh(wl), w2_ref[...]))) - 0.5
    ldec = -jnp.exp(log_w)
    a_lr = jax.nn.sigmoid(a0_ref[...] + dot3(al, a2_ref[...]))
    g = dot3(jax.nn.sigmoid(gl), g2_ref[...])
    kk = k * kk_ref[...]
    kk = kk * lax.rsqrt(jnp.maximum(_dot_lhs_split(kk * kk, bd), 1e-24))
    k2 = k * (1.0 + (a_lr - 1.0) * ka_ref[...])
    a_vec = -kk
    b_vec = kk * a_lr

    ti = lax.broadcasted_iota(jnp.int32, (C, C), 0)
    tj = lax.broadcasted_iota(jnp.int32, (C, C), 1)
    cum = _dot_rhs_split((ti >= tj).astype(f32), ldec)
    cum_last = cum[C - 1:C, :]
    w_incl = jnp.exp(cum)
    w_excl = jnp.exp(cum - ldec)
    w_inv = jnp.exp(-cum)
    w_end = jnp.exp(cum_last - cum)
    w_all = jnp.exp(cum_last)
    a_t = a_vec * w_excl
    r_t = r * w_incl
    b_t = b_vec * w_inv
    k_t = k2 * w_inv
    b_e = b_vec * w_end
    k_e = k2 * w_end

    strict = ti > tj
    incl = ti >= tj
    bi, bj = ti // RW_INV_BLOCK, tj // RW_INV_BLOCK
    same_blk = bi == bj
    pair_blk = jnp.logical_and((bi // 2) == (bj // 2), jnp.logical_not(same_blk))
    half_blk = (bi // 2) != (bj // 2)
    eye = (ti == tj).astype(f32)

    H = RW_HEADS
    heads = lambda x: jnp.stack([x[:, h * N:(h + 1) * N] for h in range(H)], axis=0)
    bmm = lambda a, b: dot3(a, b, _BNN)
    A, Rt, Bt, Kt, Be, Ke, V = (heads(t) for t in (a_t, r_t, b_t, k_t, b_e, k_e, v))
    gm = dot3(jnp.concatenate([A, Rt], axis=1), jnp.concatenate([Bt, Kt], axis=1), _BNT)
    a_ab = jnp.where(strict, gm[:, :C, :C], 0.0)
    a_ak = jnp.where(strict, gm[:, :C, C:], 0.0)
    a_rb = jnp.where(incl, gm[:, C:, :C], 0.0)
    a_rk = jnp.where(incl, gm[:, C:, C:], 0.0)
    d1 = jnp.where(same_blk, a_ab, 0.0)
    xinv = eye + d1
    d2 = bmm(d1, d1)
    xinv = xinv + bmm(xinv, d2)
    d4 = bmm(d2, d2)
    xinv = xinv + bmm(xinv, d4)
    d8 = bmm(d4, d4)
    xinv = xinv + bmm(xinv, d8)
    xinv = xinv + bmm(bmm(xinv, jnp.where(pair_blk, a_ab, 0.0)), xinv)
    xinv = xinv + bmm(bmm(xinv, jnp.where(half_blk, a_ab, 0.0)), xinv)

    av = bmm(jnp.concatenate([a_ak, a_rk], axis=1), V)
    p = bmm(xinv, jnp.concatenate([A, av[:, :C]], axis=2))
    qm = bmm(a_rb, p)
    q1 = Rt + qm[:, :, :N]
    q2 = qm[:, :, N:] + av[:, C:]
    gmat = bmm(jnp.swapaxes(Be, 1, 2), p)
    g1 = eye * heads(w_all) + gmat[:, :, :N]
    g2 = gmat[:, :, N:] + bmm(jnp.swapaxes(Ke, 1, 2), V)
    m = m_ref[...]
    yh = _dot_f32(q1, m, _BNN) + q2
    m_ref[...] = _dot_f32(g1, m, _BNN) + g2
    for h in range(H):
        y_ref[:, h * N:(h + 1) * N] = yh[h]

    y = y_ref[...]
    mean = _dot_lhs_split(y, bd) * (1.0 / N)
    yc = y - mean
    var = _dot_lhs_split(yc * yc, bd) * (1.0 / N)
    yn = yc * lax.rsqrt(var + RW_GN_EPS) * gng_ref[...] + gnb_ref[...]
    bonus = _dot_lhs_split(r * k2 * rk_ref[...], bd) * v
    o_ref[0] = ((yn + bonus) * g).astype(o_ref.dtype)

    @pl.when(pl.program_id(1) == pl.num_programs(1) - 1)
    def _():
        mo_ref[0] = m_ref[...]
        po_ref[0] = prev_ref[...]


def _rwkv7_init_carry(batch):
    return (jnp.zeros((batch, RW_HEADS, RW_HEAD_DIM, RW_HEAD_DIM), jnp.float32),
            jnp.zeros((batch, 1, RW_COLS), jnp.float32))


def _rwkv7_time_mix(z_rw, carry, mu, w0, w2, a0, a2, g2, k_k, k_a, r_k, gn_g, gn_b, out_dtype):
    B, S, _ = z_rw.shape
    C = RW_CHUNK
    assert S % C == 0 and C == RW_HEAD_DIM and C % (4 * RW_INV_BLOCK) == 0
    hid = jnp.arange(RW_DIM) // RW_HEAD_DIM
    bd = (hid[:, None] == hid[None, :]).astype(jnp.float32)
    row = lambda a: a.reshape(1, -1)
    full = lambda shape: pl.BlockSpec(shape, lambda b, c: (0,) * len(shape))
    state_spec = pl.BlockSpec((1, RW_HEADS, RW_HEAD_DIM, RW_HEAD_DIM), lambda b, c: (b, 0, 0, 0))
    prev_spec = pl.BlockSpec((1, 1, RW_COLS), lambda b, c: (b, 0, 0))
    y, m_out, p_out = pl.pallas_call(
        _rwkv_kernel,
        grid=(B, S // C),
        in_specs=[
            pl.BlockSpec((1, C, RW_COLS), lambda b, c: (b, c, 0)), state_spec, prev_spec,
            full((1, RW_COLS)), full((1, RW_DIM)), full((RW_DECAY_LORA, RW_DIM)), full((1, RW_DIM)),
            full((RW_A_LORA, RW_DIM)), full((RW_GATE_LORA, RW_DIM)), full((1, RW_DIM)), full((1, RW_DIM)),
            full((1, RW_DIM)), full((1, RW_DIM)), full((1, RW_DIM)), full((RW_DIM, RW_DIM)),
        ],
        out_specs=[pl.BlockSpec((1, C, RW_DIM), lambda b, c: (b, c, 0)), state_spec, prev_spec],
        out_shape=[jax.ShapeDtypeStruct((B, S, RW_DIM), out_dtype),
                   jax.ShapeDtypeStruct(carry[0].shape, jnp.float32),
                   jax.ShapeDtypeStruct(carry[1].shape, jnp.float32)],
        scratch_shapes=[
            pltpu.VMEM((RW_HEADS, RW_HEAD_DIM, RW_HEAD_DIM), jnp.float32),
            pltpu.VMEM((1, RW_COLS), jnp.float32),
            pltpu.VMEM((C, RW_DIM), jnp.float32),
        ],
        compiler_params=pltpu.CompilerParams(dimension_semantics=("parallel", "arbitrary")),
        name="rwkv7_time_mix",
    )(z_rw, carry[0], carry[1], row(mu), row(w0), w2, row(a0), a2, g2, row(k_k), row(k_a), row(r_k),
      row(gn_g), row(gn_b), bd)
    return y, (m_out, p_out)


DSA_TQ = 256
MASK_NEG = -1e30
INT_MIN = -2 ** 31
KEY_NEG_INF = -2139095041
THRESH_BITS = 32


def _dsa_prep_kernel(z_ref, kvg_ref, kig_ref, kib_ref, qi_ref, kv_ref, ki_ref):
    z = z_ref[0]
    qi_ref[0] = z[:, SMALL_QI:SMALL_KV].astype(qi_ref.dtype)
    kv = z[:, SMALL_KV:SMALL_KI]
    ms = jnp.mean(jnp.square(kv), -1, keepdims=True)
    kv_ref[0] = (kv * lax.rsqrt(ms + LN_EPS) * kvg_ref[...]).astype(kv_ref.dtype)
    ki = z[:, SMALL_KI:SMALL_WI]
    mu = jnp.mean(ki, -1, keepdims=True)
    var = jnp.mean(jnp.square(ki - mu), -1, keepdims=True)
    ki_ref[0] = ((ki - mu) * lax.rsqrt(var + LN_EPS) * kig_ref[...] + kib_ref[...]).astype(ki_ref.dtype)


def _dsa_prep(z_small, kv_g, ki_g, ki_b, tm=512):
    B, S, W = z_small.shape
    return pl.pallas_call(
        _dsa_prep_kernel,
        grid=(B, S // tm),
        in_specs=[
            pl.BlockSpec((1, tm, W), lambda b, i: (b, i, 0)),
            pl.BlockSpec((1, DSA_LATENT), lambda b, i: (0, 0)),
            pl.BlockSpec((1, IDX_DIM), lambda b, i: (0, 0)),
            pl.BlockSpec((1, IDX_DIM), lambda b, i: (0, 0)),
        ],
        out_specs=[
            pl.BlockSpec((1, tm, IDX_HEADS * IDX_DIM), lambda b, i: (b, i, 0)),
            pl.BlockSpec((1, tm, DSA_LATENT), lambda b, i: (b, i, 0)),
            pl.BlockSpec((1, tm, IDX_DIM), lambda b, i: (b, i, 0)),
        ],
        out_shape=[
            jax.ShapeDtypeStruct((B, S, IDX_HEADS * IDX_DIM), MM_DTYPE),
            jax.ShapeDtypeStruct((B, S, DSA_LATENT), MM_DTYPE),
            jax.ShapeDtypeStruct((B, S, IDX_DIM), MM_DTYPE),
        ],
        name="dsa_prep",
    )(z_small, kv_g[None], ki_g[None], ki_b[None])


def _sortable_key(s):
    s = jnp.where(s == 0.0, 0.0, s)
    bits = pltpu.bitcast(s, jnp.int32)
    return bits ^ ((bits >> 31) & 0x7FFFFFFF)


def _col_count(mask_i32):
    tk, tq = mask_i32.shape
    return jnp.sum(mask_i32.reshape(tk // 8, 8, tq), axis=0)


def _dsa_kernel(bfar_ref, q_ref, qi_ref, wit_ref, kv_ref, kvt_ref, ki_ref, b0_ref, b1_ref, o_ref,
                key_ref, madd_ref, m_ref, l_ref, acc_ref, *, topk, q_off):
    tq = q_ref.shape[1]
    tk = tq
    i = pl.program_id(1) + q_off
    nj = i + 1
    f32 = jnp.float32
    krow = lax.broadcasted_iota(jnp.int32, (tk, tq), 0)
    qcol = lax.broadcasted_iota(jnp.int32, (tk, tq), 1)

    qi = qi_ref[0]
    wit = wit_ref[0] * (IDX_HEADS ** -0.5)

    def score_chunk(j, carry):
        off = pl.multiple_of(j * tk, tk)
        kc = ki_ref[0, pl.ds(off, tk), :]
        s = jnp.zeros((tk, tq), f32)
        for h in range(IDX_HEADS):
            d = lax.dot_general(kc, qi[:, h * IDX_DIM:(h + 1) * IDX_DIM],
                                (((1,), (1,)), ((), ())), preferred_element_type=f32)
            s = s + wit[h:h + 1, :] * jnp.maximum(d * (IDX_DIM ** -0.5), 0.0)
        causal = (krow + j * tk) <= (qcol + i * tq)
        s = jnp.where(causal, s, -jnp.inf)
        key_ref[j] = _sortable_key(s)
        return carry

    lax.fori_loop(0, nj, score_chunk, 0)

    def count_where(pred_fn):
        def body(j, acc):
            return acc + _col_count(pred_fn(key_ref[j], j).astype(jnp.int32))
        acc = lax.fori_loop(0, nj, body, jnp.zeros((8, tq), jnp.int32))
        return jnp.sum(acc, axis=0, keepdims=True)

    def bit_step(it, t_u):
        bit = THRESH_BITS - 1 - it
        cand_u = t_u | jnp.left_shift(jnp.int32(1), bit)
        cand = cand_u ^ INT_MIN
        cnt = count_where(lambda k, j: k >= cand)
        return jnp.where(cnt >= topk, cand_u, t_u)

    t_u = lax.fori_loop(0, THRESH_BITS, bit_step, jnp.zeros((1, tq), jnp.int32))
    thr = t_u ^ INT_MIN
    cnt_gt = count_where(lambda k, j: k > thr)
    cnt_ge = count_where(lambda k, j: k >= thr)
    is_neg = thr == KEY_NEG_INF
    need = jnp.logical_and(cnt_ge > topk, jnp.logical_not(is_neg))
    n_tie_take = topk - cnt_gt

    thr_open = jnp.where(is_neg, thr, thr - 1)
    any_need = jnp.max(need.astype(jnp.int32)) > 0

    @pl.when(jnp.logical_not(any_need))
    def _():
        def body(j, carry):
            madd_ref[j] = jnp.where(key_ref[j] > thr_open, 0.0, MASK_NEG)
            return carry
        lax.fori_loop(0, nj, body, 0)

    @pl.when(any_need)
    def _():
        s_len = tk * key_ref.shape[0]
        n_bits = max(1, int(math.ceil(math.log2(s_len))))

        def idx_step(it, p):
            bit = n_bits - 1 - it
            cand = p | jnp.left_shift(jnp.int32(1), bit)
            cnt = count_where(
                lambda k, j: jnp.where(k == thr, jnp.where((krow + j * tk) < cand, 1, 0), 0))
            return jnp.where(cnt < n_tie_take, cand, p)

        p_idx = lax.fori_loop(0, n_bits, idx_step, jnp.zeros((1, tq), jnp.int32))
        p_idx = jnp.where(need, p_idx, jnp.where(is_neg, -1, s_len))

        def body(j, carry):
            k = key_ref[j]
            tie_ok = jnp.where((krow + j * tk) <= p_idx, 0.0, MASK_NEG)
            madd_ref[j] = jnp.where(k > thr, 0.0, jnp.where(k == thr, tie_ok, MASK_NEG))
            return carry
        lax.fori_loop(0, nj, body, 0)

    H, d = DSA_HEADS, DSA_LATENT
    m_ref[...] = jnp.full(m_ref.shape, MASK_NEG, f32)
    l_ref[...] = jnp.zeros(l_ref.shape, f32)
    acc_ref[...] = jnp.zeros(acc_ref.shape, f32)
    scale = DSA_LATENT ** -0.5
    q_all = jnp.concatenate([q_ref[0, :, h * d:(h + 1) * d] for h in range(H)], axis=0)

    def attend(j, bias):
        off = pl.multiple_of(j * tk, tk)
        kc = kv_ref[0, pl.ds(off, tk), :]
        kct = kvt_ref[0, :, pl.ds(off, tk)]
        lg = lax.dot_general(kc, q_all, (((1,), (1,)), ((), ())), preferred_element_type=f32)
        lg = lg * scale + bias + jnp.tile(madd_ref[j], (1, H))
        m_old = m_ref[...]
        m_new = jnp.maximum(m_old, jnp.max(lg, axis=0, keepdims=True))
        alpha = jnp.exp(m_old - m_new)
        p = jnp.exp(lg - m_new)
        l_ref[...] = alpha * l_ref[...] + jnp.sum(p, axis=0, keepdims=True)
        acc_ref[...] = alpha * acc_ref[...] + jnp.dot(kct, p.astype(kct.dtype), preferred_element_type=f32)
        m_ref[...] = m_new

    def far_body(j, carry):
        attend(j, bfar_ref[...])
        return carry

    lax.fori_loop(0, jnp.maximum(i - 1, 0), far_body, 0)

    @pl.when(i >= 1)
    def _():
        attend(i - 1, b1_ref[...])

    attend(i, b0_ref[...])

    out_t = acc_ref[...] / l_ref[...]
    for h in range(H):
        o_ref[0, :, h * d:(h + 1) * d] = out_t[:, h * tq:(h + 1) * tq].T.astype(o_ref.dtype)


def _dsa_bias_tiles(rel_bias, tq):
    dist = jnp.arange(2 * tq, dtype=jnp.int32)
    by_dist = rel_bias[_t5_bucket(dist)].T

    def toeplitz(v):
        H, P = v.shape
        skew = jnp.tile(v, (1, tq + 1))[:, :tq * (P + 1)].reshape(H, tq, P + 1)
        return skew[:, ::-1, :tq]

    b0 = toeplitz(jnp.concatenate([jnp.repeat(by_dist[:, :1], tq - 1, axis=1), by_dist[:, :tq + 1]], axis=1))
    b1 = toeplitz(jnp.concatenate([by_dist[:, 1:], by_dist[:, -1:]], axis=1))
    H = by_dist.shape[0]
    b0 = jnp.transpose(b0, (1, 0, 2)).reshape(tq, H * tq)
    b1 = jnp.transpose(b1, (1, 0, 2)).reshape(tq, H * tq)
    return b0, b1, jnp.repeat(by_dist[:, -1], tq)[None, :]


def _dsa_attention(q, qi, z_wi_t, ckv, kidx, bias_tiles, topk, q_off, out_dtype):
    B, Sq, _ = q.shape
    tq = DSA_TQ
    nq = Sq // tq
    S = ckv.shape[1]
    assert Sq % tq == 0 and S == (q_off + nq) * tq and tq >= REL_MAX_DIST and topk <= tq
    ckv_t = jnp.swapaxes(ckv, 1, 2)
    b0, b1, bfar = bias_tiles
    nk = S // tq
    H, d = DSA_HEADS, DSA_LATENT
    f32_bytes = 4
    vmem_estimate = (
        2 * nk * tq * tq * f32_bytes
        + 2 * 2 * tq * H * tq * f32_bytes
        + 2 * (2 * S * d + S * LANES) * 2
        + 6 * tq * H * tq * f32_bytes
        + 4 * tq * H * d * f32_bytes
    )
    grid_spec = pltpu.PrefetchScalarGridSpec(
        num_scalar_prefetch=0,
        grid=(B, nq),
        in_specs=[
            pl.BlockSpec((1, H * tq), lambda b, i: (0, 0)),
            pl.BlockSpec((1, tq, H * d), lambda b, i: (b, i, 0)),
            pl.BlockSpec((1, tq, IDX_HEADS * IDX_DIM), lambda b, i: (b, i, 0)),
            pl.BlockSpec((1, IDX_HEADS, tq), lambda b, i: (b, 0, i)),
            pl.BlockSpec((1, S, d), lambda b, i: (b, 0, 0)),
            pl.BlockSpec((1, d, S), lambda b, i: (b, 0, 0)),
            pl.BlockSpec((1, S, IDX_DIM), lambda b, i: (b, 0, 0)),
            pl.BlockSpec((tq, H * tq), lambda b, i: (0, 0)),
            pl.BlockSpec((tq, H * tq), lambda b, i: (0, 0)),
        ],
        out_specs=pl.BlockSpec((1, tq, H * d), lambda b, i: (b, i, 0)),
        scratch_shapes=[
            pltpu.VMEM((nk, tq, tq), jnp.int32),
            pltpu.VMEM((nk, tq, tq), jnp.float32),
            pltpu.VMEM((1, H * tq), jnp.float32),
            pltpu.VMEM((1, H * tq), jnp.float32),
            pltpu.VMEM((d, H * tq), jnp.float32),
        ],
    )
    return pl.pallas_call(
        partial(_dsa_kernel, topk=topk, q_off=q_off),
        grid_spec=grid_spec,
        out_shape=jax.ShapeDtypeStruct((B, Sq, H * d), out_dtype),
        compiler_params=pltpu.CompilerParams(vmem_limit_bytes=vmem_estimate),
        name="dsa_attention",
    )(bfar, q, qi, z_wi_t, ckv, ckv_t, kidx, b0, b1)


SC_CORES = 2
SC_SUBCORES = 16
SC_LANES = 16
PEER_SC_TOKENS = 8
PEER_SC_RING = 4
PEER_SC_UNROLL = 4
PEER_SLOTS = PEER_HEADS * PEER_TOPK


def _peer_sc_call(body, T, out_width, stage_width):
    mesh = plsc.VectorSubcoreMesh(core_axis_name="c", subcore_axis_name="s")
    return pl.kernel(
        body, mesh=mesh,
        out_type=jax.ShapeDtypeStruct((T, out_width), jnp.float32),
        scratch_types=[
            pltpu.VMEM((PEER_SC_TOKENS, PEER_SLOTS), jnp.int32),
            pltpu.VMEM((PEER_SC_TOKENS, stage_width), jnp.float32),
            pltpu.VMEM((PEER_SC_RING, PEER_TOPK, D_MODEL), jnp.float32),
            pltpu.VMEM((PEER_SC_TOKENS, out_width), jnp.float32),
            pltpu.SemaphoreType.DMA((PEER_SC_RING,)),
        ],
        compiler_params=pltpu.CompilerParams(needs_layout_passes=False),
    )


def _peer_sc_body(compute, zero_out, x_hbm, idx_hbm, tab_hbm, out_hbm, idx_v, x_v, rows_v, out_v, sems):
    T = idx_hbm.shape[0]
    tokens_per_worker = T // (SC_CORES * SC_SUBCORES)
    n_steps = PEER_SC_TOKENS * PEER_HEADS
    worker = lax.axis_index("s") * SC_CORES + lax.axis_index("c")
    base = worker * tokens_per_worker

    def gather(s, b):
        ids = idx_v[s // PEER_HEADS, pl.ds((s % PEER_HEADS) * PEER_TOPK, PEER_TOPK)]
        return pltpu.make_async_copy(tab_hbm.at[ids], rows_v.at[b], sems.at[b])

    @pl.loop(0, tokens_per_worker // PEER_SC_TOKENS)
    def _(blk):
        tok0 = base + blk * PEER_SC_TOKENS
        pltpu.sync_copy(idx_hbm.at[pl.ds(tok0, PEER_SC_TOKENS)], idx_v)
        pltpu.sync_copy(x_hbm.at[pl.ds(tok0, PEER_SC_TOKENS)], x_v)
        for b in range(PEER_SC_RING - 1):
            gather(b, b).start()
        if zero_out:
            @pl.loop(0, PEER_SC_TOKENS)
            def _(t):
                @plsc.parallel_loop(0, out_v.shape[1] // SC_LANES, unroll=PEER_SC_UNROLL)
                def _(c):
                    out_v[t, pl.ds(pl.multiple_of(c * SC_LANES, SC_LANES), SC_LANES)] = (
                        jnp.zeros((SC_LANES,), jnp.float32))

        @pl.loop(0, n_steps, step=PEER_SC_RING)
        def _(s0):
            for b in range(PEER_SC_RING):
                s = s0 + b
                gather(s, b).wait()

                @pl.when(s + PEER_SC_RING - 1 < n_steps)
                def _():
                    gather(s + PEER_SC_RING - 1, (b + PEER_SC_RING - 1) % PEER_SC_RING).start()

                compute(s // PEER_HEADS, s % PEER_HEADS, b, x_v, rows_v, out_v)

        pltpu.sync_copy(out_v, out_hbm.at[pl.ds(tok0, PEER_SC_TOKENS)])


def _peer_dots_compute(t, hd, b, h_v, rows_v, dots_v):
    lane = lax.iota(jnp.int32, SC_LANES)

    def col_step(c, accs):
        off = pl.multiple_of(c * SC_LANES, SC_LANES)
        hv = h_v[t, pl.ds(off, SC_LANES)]
        return tuple(accs[r] + rows_v[b, r, pl.ds(off, SC_LANES)] * hv for r in range(PEER_TOPK))

    accs = plsc.parallel_loop(
        0, D_MODEL // SC_LANES, unroll=PEER_SC_UNROLL,
        carry=tuple(jnp.zeros((SC_LANES,), jnp.float32) for _ in range(PEER_TOPK)))(col_step)
    res = jnp.zeros((SC_LANES,), jnp.float32)
    for r in range(PEER_TOPK):
        res = jnp.where(lane == r, jnp.sum(accs[r]), res)
    dots_v[t, pl.ds(hd * PEER_TOPK, PEER_TOPK)] = res


def _peer_mix_compute(t, hd, b, act_v, rows_v, out_v):
    tvec = jnp.full((SC_LANES,), t, jnp.int32)
    weights = [plsc.load_gather(act_v, [tvec, jnp.full((SC_LANES,), hd * PEER_TOPK + r, jnp.int32)])
               for r in range(PEER_TOPK)]

    @plsc.parallel_loop(0, D_MODEL // SC_LANES, unroll=PEER_SC_UNROLL)
    def _(c):
        off = pl.multiple_of(c * SC_LANES, SC_LANES)
        a = rows_v[b, 0, pl.ds(off, SC_LANES)] * weights[0]
        for r in range(1, PEER_TOPK):
            a = a + rows_v[b, r, pl.ds(off, SC_LANES)] * weights[r]
        plsc.addupdate(out_v.at[t, pl.ds(off, SC_LANES)], a)


def _peer_expert_dots(h, experts, u_tab):
    T = h.shape[0]
    assert T % (SC_CORES * SC_SUBCORES * PEER_SC_TOKENS) == 0
    assert (PEER_SC_TOKENS * PEER_HEADS) % PEER_SC_RING == 0 and PEER_TOPK == SC_LANES
    body = partial(_peer_sc_body, _peer_dots_compute, False)
    return _peer_sc_call(body, T, PEER_SLOTS, D_MODEL)(h, experts, u_tab)


def _peer_expert_mix(act, experts, v_tab):
    T = act.shape[0]
    assert T % (SC_CORES * SC_SUBCORES * PEER_SC_TOKENS) == 0
    body = partial(_peer_sc_body, _peer_mix_compute, True)
    return _peer_sc_call(body, T, D_MODEL, PEER_SLOTS)(act, experts, v_tab)


PEER_TM = 256


def _extract_topk(s, k, payload=None):
    R = s.shape[0]
    riota = lax.broadcasted_iota(jnp.int32, s.shape, 0)
    vals, rows = [], []
    for _ in range(k):
        m = jnp.max(s, axis=0, keepdims=True)
        pos = jnp.min(jnp.where(s == m, riota, R), axis=0, keepdims=True)
        hit = riota == pos
        vals.append(m)
        rows.append(pos if payload is None else jnp.max(jnp.where(hit, payload, -1), axis=0, keepdims=True))
        s = jnp.where(hit, -jnp.inf, s)
    return jnp.concatenate(vals, axis=0), jnp.concatenate(rows, axis=0)


def _peer_route_kernel(x_ref, sc_ref, sh_ref, wq_ref, keys_ref, h_ref, ex_ref, gate_ref, q_ref, ext_ref, gt_ref):
    K = PEER_TOPK
    f32 = jnp.float32
    h = x_ref[0] * (1.0 + sc_ref[0]) + sh_ref[0]
    h_ref[0] = h
    q = jnp.dot(h.astype(wq_ref.dtype), wq_ref[...], preferred_element_type=f32)
    for hd in range(PEER_HEADS):
        q_ref[hd] = q[:, hd * PEER_KEY_DIM:(hd + 1) * PEER_KEY_DIM].astype(q_ref.dtype)

    def head_body(hd, carry):
        qh = q_ref[hd]
        s1 = lax.dot_general(keys_ref[hd, 0], qh[:, :PEER_HALF], _NT, preferred_element_type=f32)
        s2 = lax.dot_general(keys_ref[hd, 1], qh[:, PEER_HALF:], _NT, preferred_element_type=f32)
        v1, i1 = _extract_topk(s1, K)
        v2, i2 = _extract_topk(s2, K)
        tm = v1.shape[1]
        cand_rows, cidx_rows = [], []
        for a in range(K):
            nb = K // (a + 1)
            cand_rows.append(v1[a:a + 1, :] + v2[:nb, :])
            cidx_rows.append(i1[a:a + 1, :] * PEER_N_KEYS + i2[:nb, :])
        n_pad = -sum(r.shape[0] for r in cand_rows) % 8
        cand_rows.append(jnp.full((n_pad, tm), -jnp.inf, f32))
        cidx_rows.append(jnp.full((n_pad, tm), -1, jnp.int32))
        cand = jnp.concatenate(cand_rows, axis=0)
        cidx = jnp.concatenate(cidx_rows, axis=0)
        top_s, experts = _extract_topk(cand, K, payload=cidx)
        e = jnp.exp(top_s - top_s[0:1, :])
        gt_ref[pl.ds(pl.multiple_of(hd * K, K), K), :] = e / jnp.sum(e, axis=0, keepdims=True)
        ext_ref[pl.ds(pl.multiple_of(hd * K, K), K), :] = experts
        return carry

    lax.fori_loop(0, PEER_HEADS, head_body, 0)
    ex_ref[0] = ext_ref[...].T
    gate_ref[0] = gt_ref[...].T


def _peer_route(x, sc, sh, w_pq, sub_keys):
    B, S, D = x.shape
    tm = PEER_TM
    return pl.pallas_call(
        _peer_route_kernel,
        grid=(B, S // tm),
        in_specs=[
            pl.BlockSpec((1, tm, D), lambda b, i: (b, i, 0)),
            pl.BlockSpec((1, 1, D), lambda b, i: (b, 0, 0)),
            pl.BlockSpec((1, 1, D), lambda b, i: (b, 0, 0)),
            pl.BlockSpec((D, PEER_HEADS * PEER_KEY_DIM), lambda b, i: (0, 0)),
            pl.BlockSpec((PEER_HEADS, 2, PEER_N_KEYS, PEER_HALF), lambda b, i: (0, 0, 0, 0)),
        ],
        out_specs=[
            pl.BlockSpec((1, tm, D), lambda b, i: (b, i, 0)),
            pl.BlockSpec((1, tm, PEER_SLOTS), lambda b, i: (b, i, 0)),
            pl.BlockSpec((1, tm, PEER_SLOTS), lambda b, i: (b, i, 0)),
        ],
        out_shape=[
            jax.ShapeDtypeStruct((B, S, D), jnp.float32),
            jax.ShapeDtypeStruct((B, S, PEER_SLOTS), jnp.int32),
            jax.ShapeDtypeStruct((B, S, PEER_SLOTS), jnp.float32),
        ],
        scratch_shapes=[
            pltpu.VMEM((PEER_HEADS, tm, PEER_KEY_DIM), MM_DTYPE),
            pltpu.VMEM((PEER_SLOTS, tm), jnp.int32),
            pltpu.VMEM((PEER_SLOTS, tm), jnp.float32),
        ],
        compiler_params=pltpu.CompilerParams(dimension_semantics=("parallel", "parallel")),
        name="peer_route",
    )(x, sc[:, None, :], sh[:, None, :], w_pq.astype(MM_DTYPE), sub_keys.astype(MM_DTYPE))


def _peer_pre(h2, experts, u_tab):
    B, S, D = h2.shape
    return _peer_expert_dots(h2.reshape(B * S, D), experts.reshape(B * S, PEER_SLOTS), u_tab)


def _peer_act_kernel(pre_ref, gate_ref, o_ref):
    pre = pre_ref[...]
    o_ref[...] = 0.5 * pre * (1.0 + lax.erf(pre * (2.0 ** -0.5))) * gate_ref[...]


def _peer_act(pre, gates):
    T, W = pre.shape
    tm = 8 * ROW_TILE
    spec = pl.BlockSpec((tm, W), lambda i: (i, 0))
    return pl.pallas_call(
        _peer_act_kernel,
        grid=(T // tm,),
        in_specs=[spec, spec],
        out_specs=spec,
        out_shape=jax.ShapeDtypeStruct((T, W), jnp.float32),
        name="peer_act",
    )(pre, gates.reshape(T, W))


def _peer_out(act, experts, v_tab):
    B, S, _ = experts.shape
    return _peer_expert_mix(act, experts.reshape(B * S, PEER_SLOTS), v_tab).reshape(B, S, D_MODEL)


def kernel(x, c, w_ada, b_ada, w_in, rw_mu, rw_w0, rw_w2, rw_a0, rw_a2, rw_g2, rw_k_k, rw_k_a, rw_r_k, rw_gn_g, rw_gn_b, dsa_kv_g, idx_k_g, idx_k_b, rel_bias, w_br_a, w_br_b, w_out, ln1_g, ln1_b, peer_wq, peer_keys, peer_u, peer_v, ln2_g, ln2_b):
    l = 0
    mod = jax.nn.silu(c) @ w_ada[l] + b_ada[l]

    w_rw, w_q, w_kv, w_qi, w_ki, w_wi, w_ga, w_gb = jnp.split(w_in[l], _split_points(IN_SIZES), axis=-1)
    small_pad = jnp.zeros((D_MODEL, SMALL_COLS - SMALL_WI - IDX_HEADS), w_in.dtype)
    w_small = jnp.concatenate([w_qi, w_kv, w_ki, w_wi, small_pad], axis=-1)
    w_gates = jnp.concatenate([w_ga, w_gb], axis=-1)

    B, S, _ = x.shape
    seg_len = S // SEQ_SEGMENTS
    dsa_topk = min(TOPK_MAX, S // 4)
    bias_tiles = _dsa_bias_tiles(rel_bias, DSA_TQ)

    def project_and_dsa(x, sc1, sh1, seg, keys_so_far):
        z_rw = _mod_matmul(x, sc1, sh1, w_rw, jnp.float32, tn=896)
        z_q = _mod_matmul(x, sc1, sh1, w_q, MM_DTYPE, tn=1024)
        z_small = _mod_matmul(x, sc1, sh1, w_small, jnp.float32, tn=512)
        z_g = _mod_matmul(x, sc1, sh1, w_gates, jnp.float32, tn=1024)
        z_wi_t = jnp.swapaxes(z_small[..., SMALL_WI:SMALL_WI + IDX_HEADS], 1, 2)
        qi, ckv, kidx = _dsa_prep(z_small, dsa_kv_g[l], idx_k_g[l], idx_k_b[l])
        if keys_so_far is not None:
            ckv = jnp.concatenate([keys_so_far[0], ckv], axis=1)
            kidx = jnp.concatenate([keys_so_far[1], kidx], axis=1)
        y_b = _dsa_attention(z_q, qi, z_wi_t, ckv, kidx, bias_tiles, dsa_topk,
                             seg * (seg_len // DSA_TQ), MM_DTYPE)
        return (y_b, z_rw, z_g), (ckv, kidx)

    def rwkv_and_merge(x, gt1, carry, y_b, z_rw, z_g):
        y_a, carry = _rwkv7_time_mix(z_rw, carry, rw_mu[l], rw_w0[l], rw_w2[l], rw_a0[l], rw_a2[l], rw_g2[l],
                                     rw_k_k[l], rw_k_a[l], rw_r_k[l], rw_gn_g[l], rw_gn_b[l], MM_DTYPE)
        return _merge_ln(x, gt1, y_a, y_b, z_g, w_br_a[l], w_br_b[l], w_out[l], ln1_g[l], ln1_b[l]), carry

    def finish(st, act):
        y2 = _peer_out(act, st["experts"], peer_v[l])
        return _residual_ln(st["x1"], st["gt2"], y2, ln2_g[l], ln2_b[l])

    gsz = B // BATCH_GROUPS
    outs = []
    prev = None
    for g in range(BATCH_GROUPS):
        sh1, sc1, gt1, sh2, sc2, gt2 = jnp.split(mod[g * gsz:(g + 1) * gsz], 6, axis=-1)
        rw_carry = _rwkv7_init_carry(gsz)
        keys_so_far = None
        for seg in range(SEQ_SEGMENTS):
            x_g = x[g * gsz:(g + 1) * gsz, seg * seg_len:(seg + 1) * seg_len]
            if prev is not None:
                x_g, prev["experts"] = lax.optimization_barrier((x_g, prev["experts"]))
            proj, keys_so_far = project_and_dsa(x_g, sc1, sh1, seg, keys_so_far)
            if prev is not None:
                proj, prev["pre"], outs = lax.optimization_barrier((proj, prev["pre"], outs))
                outs.append(finish(prev, _peer_act(prev["pre"], prev["gates"])))
            x1, rw_carry = rwkv_and_merge(x_g, gt1, rw_carry, *proj)
            h2, experts, gates = _peer_route(x1, sc2, sh2, peer_wq[l], peer_keys[l])
            prev = dict(x1=x1, gt2=gt2, experts=experts, gates=gates, pre=_peer_pre(h2, experts, peer_u[l]))
    prev["pre"], outs = lax.optimization_barrier((prev["pre"], outs))
    outs.append(finish(prev, _peer_act(prev["pre"], prev["gates"])))
    rows = [jnp.concatenate(outs[g * SEQ_SEGMENTS:(g + 1) * SEQ_SEGMENTS], axis=1) for g in range(BATCH_GROUPS)]
    return jnp.concatenate(rows, axis=0)
```

```python
import math
from functools import partial

import jax
import jax.numpy as jnp
import numpy as np
from jax import lax
from jax.experimental import pallas as pl
from jax.experimental.pallas import tpu as pltpu
from jax.experimental.pallas import tpu_sc as plsc

D_MODEL = 1024
RW_HEADS = 8
RW_HEAD_DIM = 64
RW_DIM = 512
RW_DECAY_LORA = 64
RW_A_LORA = 64
RW_GATE_LORA = 128
RW_COLS = 3 * RW_DIM + RW_DECAY_LORA + RW_A_LORA + RW_GATE_LORA
RW_GN_EPS = 64e-5
DSA_HEADS = 8
DSA_LATENT = 128
DSA_Q_DIM = DSA_HEADS * DSA_LATENT
IDX_HEADS = 4
IDX_DIM = 64
TOPK_MAX = 256
REL_BUCKETS = 32
REL_MAX_DIST = 128
IN_SIZES = (RW_COLS, DSA_Q_DIM, DSA_LATENT, IDX_HEADS * IDX_DIM, IDX_DIM, IDX_HEADS, D_MODEL, D_MODEL)
IN_COLS = sum(IN_SIZES)
PEER_HEADS = 8
PEER_N_KEYS = 128
PEER_KEY_DIM = 128
PEER_HALF = 64
PEER_TOPK = 16
LN_EPS = 1e-5
DEPTH = 1
DEEPNORM_ALPHA = (2.0 * DEPTH) ** 0.25

LANES = 128
SMALL_QI = 0
SMALL_KV = SMALL_QI + IDX_HEADS * IDX_DIM
SMALL_KI = SMALL_KV + DSA_LATENT
SMALL_WI = SMALL_KI + IDX_DIM
SMALL_COLS = 4 * LANES
MM_DTYPE = jnp.bfloat16
BATCH_GROUPS = 8
SEQ_SEGMENTS = 4


def _split_points(sizes):
    return np.cumsum(sizes)[:-1].tolist()


def _mod_matmul_kernel(x_ref, sc_ref, sh_ref, w_ref, o_ref):
    h = x_ref[0] * (1.0 + sc_ref[0]) + sh_ref[0]
    o_ref[0] = jnp.dot(h.astype(w_ref.dtype), w_ref[...],
                       preferred_element_type=jnp.float32).astype(o_ref.dtype)


def _mod_matmul(x, sc, sh, w, out_dtype, tn, tm=512):
    B, S, D = x.shape
    N = w.shape[1]
    assert S % tm == 0 and N % tn == 0
    return pl.pallas_call(
        _mod_matmul_kernel,
        grid=(B, S // tm, N // tn),
        in_specs=[
            pl.BlockSpec((1, tm, D), lambda b, i, j: (b, i, 0)),
            pl.BlockSpec((1, 1, D), lambda b, i, j: (b, 0, 0)),
            pl.BlockSpec((1, 1, D), lambda b, i, j: (b, 0, 0)),
            pl.BlockSpec((D, tn), lambda b, i, j: (0, j)),
        ],
        out_specs=pl.BlockSpec((1, tm, tn), lambda b, i, j: (b, i, j)),
        out_shape=jax.ShapeDtypeStruct((B, S, N), out_dtype),
        name="mod_matmul",
    )(x, sc[:, None, :], sh[:, None, :], w.astype(MM_DTYPE))


ROW_TILE = 256


def _layer_norm(x, g, b):
    mu = jnp.mean(x, -1, keepdims=True)
    var = jnp.mean(jnp.square(x - mu), -1, keepdims=True)
    return (x - mu) * lax.rsqrt(var + LN_EPS) * g + b


def _merge_ln_kernel(x_ref, gt_ref, ya_ref, yb_ref, zg_ref, wa_ref, wb_ref, wo_ref, g_ref, b_ref, o_ref):
    f32 = jnp.float32
    D = x_ref.shape[2]
    ya = jnp.dot(ya_ref[0], wa_ref[...], preferred_element_type=f32)
    yb = jnp.dot(yb_ref[0], wb_ref[...], preferred_element_type=f32)
    zg = zg_ref[0]
    merged = jax.nn.sigmoid(zg[:, :D]) * ya + jax.nn.sigmoid(zg[:, D:]) * yb
    mix = jnp.dot(merged.astype(wo_ref.dtype), wo_ref[...], preferred_element_type=f32)
    o_ref[0] = _layer_norm(DEEPNORM_ALPHA * x_ref[0] + gt_ref[0] * mix, g_ref[...], b_ref[...])


def _merge_ln(x, gt, y_a, y_b, z_g, w_a, w_b, w_o, ln_g, ln_b):
    B, S, D = x.shape
    tm = ROW_TILE
    assert S % tm == 0
    tok = lambda width: pl.BlockSpec((1, tm, width), lambda b, i: (b, i, 0))
    full = lambda a: pl.BlockSpec(a.shape, lambda b, i: (0,) * a.ndim)
    w_a, w_b, w_o = (w.astype(MM_DTYPE) for w in (w_a, w_b, w_o))
    ln_g, ln_b = ln_g[None], ln_b[None]
    return pl.pallas_call(
        _merge_ln_kernel,
        grid=(B, S // tm),
        in_specs=[tok(D), pl.BlockSpec((1, 1, D), lambda b, i: (b, 0, 0)), tok(y_a.shape[2]), tok(y_b.shape[2]),
                  tok(z_g.shape[2]), full(w_a), full(w_b), full(w_o), full(ln_g), full(ln_b)],
        out_specs=tok(D),
        out_shape=jax.ShapeDtypeStruct((B, S, D), jnp.float32),
        compiler_params=pltpu.CompilerParams(dimension_semantics=("parallel", "parallel")),
        name="merge_ln",
    )(x, gt[:, None, :], y_a, y_b, z_g, w_a, w_b, w_o, ln_g, ln_b)


def _residual_ln_kernel(x_ref, gt_ref, y_ref, g_ref, b_ref, o_ref):
    o_ref[0] = _layer_norm(DEEPNORM_ALPHA * x_ref[0] + gt_ref[0] * y_ref[0], g_ref[...], b_ref[...])


def _residual_ln(x, gt, y, ln_g, ln_b):
    B, S, D = x.shape
    tm = ROW_TILE
    assert S % tm == 0
    tok = pl.BlockSpec((1, tm, D), lambda b, i: (b, i, 0))
    vec = pl.BlockSpec((1, D), lambda b, i: (0, 0))
    return pl.pallas_call(
        _residual_ln_kernel,
        grid=(B, S // tm),
        in_specs=[tok, pl.BlockSpec((1, 1, D), lambda b, i: (b, 0, 0)), tok, vec, vec],
        out_specs=tok,
        out_shape=jax.ShapeDtypeStruct((B, S, D), jnp.float32),
        compiler_params=pltpu.CompilerParams(dimension_semantics=("parallel", "parallel")),
        name="residual_ln",
    )(x, gt[:, None, :], y, ln_g[None], ln_b[None])


def _t5_bucket(n):
    n = jnp.maximum(n, 0)
    max_exact = REL_BUCKETS // 2
    nf = jnp.maximum(n, 1).astype(jnp.float32)
    large = max_exact + (jnp.log(nf / max_exact) / math.log(REL_MAX_DIST / max_exact)
                         * (REL_BUCKETS - max_exact)).astype(jnp.int32)
    large = jnp.minimum(large, REL_BUCKETS - 1)
    return jnp.where(n < max_exact, n, large)


RW_CHUNK = 64
RW_INV_BLOCK = 16
_NN = (((1,), (0,)), ((), ()))
_NT = (((1,), (1,)), ((), ()))
_BNN = (((2,), (1,)), ((0,), (0,)))
_BNT = (((2,), (2,)), ((0,), (0,)))


def _dot_f32(a, b, dims=_NN):
    return lax.dot_general(a, b, dims, precision=lax.Precision.HIGHEST,
                           preferred_element_type=jnp.float32)


def _dot_bf16x3(a, b, dims=_NN):
    f32, bf = jnp.float32, jnp.bfloat16
    a_hi, b_hi = a.astype(bf), b.astype(bf)
    a_lo = (a - a_hi.astype(f32)).astype(bf)
    b_lo = (b - b_hi.astype(f32)).astype(bf)
    out = lax.dot_general(a_hi, b_hi, dims, preferred_element_type=f32)
    out = out + lax.dot_general(a_hi, b_lo, dims, preferred_element_type=f32)
    return out + lax.dot_general(a_lo, b_hi, dims, preferred_element_type=f32)


def _bf16_terms(a):
    f32, bf = jnp.float32, jnp.bfloat16
    hi = a.astype(bf)
    r1 = a - hi.astype(f32)
    mid = r1.astype(bf)
    lo = (r1 - mid.astype(f32)).astype(bf)
    return hi, mid, lo


def _dot_lhs_split(a, b01):
    b = b01.astype(jnp.bfloat16)
    return sum(jnp.dot(t, b, preferred_element_type=jnp.float32) for t in _bf16_terms(a))


def _dot_rhs_split(a01, b):
    a = a01.astype(jnp.bfloat16)
    return sum(jnp.dot(a, t, preferred_element_type=jnp.float32) for t in _bf16_terms(b))


def _rwkv_kernel(z_ref, m0_ref, p0_ref, mu_ref, w0_ref, w2_ref, a0_ref, a2_ref, g2_ref, kk_ref, ka_ref, rk_ref,
                 gng_ref, gnb_ref, bd_ref, o_ref, mo_ref, po_ref, m_ref, prev_ref, y_ref):
    C = z_ref.shape[1]
    N = RW_HEAD_DIM
    f32 = jnp.float32
    dot3 = _dot_bf16x3

    @pl.when(pl.program_id(1) == 0)
    def _():
        m_ref[...] = m0_ref[0]
        prev_ref[...] = p0_ref[0]

    z = z_ref[0]
    row = lax.broadcasted_iota(jnp.int32, z.shape, 0)
    shifted = jnp.where(row == 0, prev_ref[...], pltpu.roll(z, 1, axis=0))
    prev_ref[...] = z[C - 1:C, :]
    zs = z + (shifted - z) * mu_ref[...]
    r = zs[:, 0:RW_DIM]
    k = zs[:, RW_DIM:2 * RW_DIM]
    v = zs[:, 2 * RW_DIM:3 * RW_DIM]
    o1 = 3 * RW_DIM
    wl = zs[:, o1:o1 + RW_DECAY_LORA]
    al = zs[:, o1 + RW_DECAY_LORA:o1 + RW_DECAY_LORA + RW_A_LORA]
    gl = zs[:, o1 + RW_DECAY_LORA + RW_A_LORA:]

    bd = bd_ref[...]
    log_w = -jax.nn.softplus(-(w0_ref[...] + dot3(jnp.tanh(wl), w2_ref[...]))) - 0.5
    ldec = -jnp.exp(log_w)
    a_lr = jax.nn.sigmoid(a0_ref[...] + dot3(al, a2_ref[...]))
    g = dot3(jax.nn.sigmoid(gl), g2_ref[...])
    kk = k * kk_ref[...]
    kk = kk * lax.rsqrt(jnp.maximum(_dot_lhs_split(kk * kk, bd), 1e-24))
    k2 = k * (1.0 + (a_lr - 1.0) * ka_ref[...])
    a_vec = -kk
    b_vec = kk * a_lr

    ti = lax.broadcasted_iota(jnp.int32, (C, C), 0)
    tj = lax.broadcasted_iota(jnp.int32, (C, C), 1)
    cum = _dot_rhs_split((ti >= tj).astype(f32), ldec)
    cum_last = cum[C - 1:C, :]
    w_incl = jnp.exp(cum)
    w_excl = jnp.exp(cum - ldec)
    w_inv = jnp.exp(-cum)
    w_end = jnp.exp(cum_last - cum)
    w_all = jnp.exp(cum_last)
    a_t = a_vec * w_excl
    r_t = r * w_incl
    b_t = b_vec * w_inv
    k_t = k2 * w_inv
    b_e = b_vec * w_end
    k_e = k2 * w_end

    strict = ti > tj
    incl = ti >= tj
    bi, bj = ti // RW_INV_BLOCK, tj // RW_INV_BLOCK
    same_blk = bi == bj
    pair_blk = jnp.logical_and((bi // 2) == (bj // 2), jnp.logical_not(same_blk))
    half_blk = (bi // 2) != (bj // 2)
    eye = (ti == tj).astype(f32)

    H = RW_HEADS
    heads = lambda x: jnp.stack([x[:, h * N:(h + 1) * N] for h in range(H)], axis=0)
    bmm = lambda a, b: dot3(a, b, _BNN)
    A, Rt, Bt, Kt, Be, Ke, V = (heads(t) for t in (a_t, r_t, b_t, k_t, b_e, k_e, v))
    gm = dot3(jnp.concatenate([A, Rt], axis=1), jnp.concatenate([Bt, Kt], axis=1), _BNT)
    a_ab = jnp.where(strict, gm[:, :C, :C], 0.0)
    a_ak = jnp.where(strict, gm[:, :C, C:], 0.0)
    a_rb = jnp.where(incl, gm[:, C:, :C], 0.0)
    a_rk = jnp.where(incl, gm[:, C:, C:], 0.0)
    d1 = jnp.where(same_blk, a_ab, 0.0)
    xinv = eye + d1
    d2 = bmm(d1, d1)
    xinv = xinv + bmm(xinv, d2)
    d4 = bmm(d2, d2)
    xinv = xinv + bmm(xinv, d4)
    d8 = bmm(d4, d4)
    xinv = xinv + bmm(xinv, d8)
    xinv = xinv + bmm(bmm(xinv, jnp.where(pair_blk, a_ab, 0.0)), xinv)
    xinv = xinv + bmm(bmm(xinv, jnp.where(half_blk, a_ab, 0.0)), xinv)

    av = bmm(jnp.concatenate([a_ak, a_rk], axis=1), V)
    p = bmm(xinv, jnp.concatenate([A, av[:, :C]], axis=2))
    qm = bmm(a_rb, p)
    q1 = Rt + qm[:, :, :N]
    q2 = qm[:, :, N:] + av[:, C:]
    gmat = bmm(jnp.swapaxes(Be, 1, 2), p)
    g1 = eye * heads(w_all) + gmat[:, :, :N]
    g2 = gmat[:, :, N:] + bmm(jnp.swapaxes(Ke, 1, 2), V)
    m = m_ref[...]
    yh = _dot_f32(q1, m, _BNN) + q2
    m_ref[...] = _dot_f32(g1, m, _BNN) + g2
    for h in range(H):
        y_ref[:, h * N:(h + 1) * N] = yh[h]

    y = y_ref[...]
    mean = _dot_lhs_split(y, bd) * (1.0 / N)
    yc = y - mean
    var = _dot_lhs_split(yc * yc, bd) * (1.0 / N)
    yn = yc * lax.rsqrt(var + RW_GN_EPS) * gng_ref[...] + gnb_ref[...]
    bonus = _dot_lhs_split(r * k2 * rk_ref[...], bd) * v
    o_ref[0] = ((yn + bonus) * g).astype(o_ref.dtype)

    @pl.when(pl.program_id(1) == pl.num_programs(1) - 1)
    def _():
        mo_ref[0] = m_ref[...]
        po_ref[0] = prev_ref[...]


def _rwkv7_init_carry(batch):
    return (jnp.zeros((batch, RW_HEADS, RW_HEAD_DIM, RW_HEAD_DIM), jnp.float32),
            jnp.zeros((batch, 1, RW_COLS), jnp.float32))


def _rwkv7_time_mix(z_rw, carry, mu, w0, w2, a0, a2, g2, k_k, k_a, r_k, gn_g, gn_b, out_dtype):
    B, S, _ = z_rw.shape
    C = RW_CHUNK
    assert S % C == 0 and C == RW_HEAD_DIM and C % (4 * RW_INV_BLOCK) == 0
    hid = jnp.arange(RW_DIM) // RW_HEAD_DIM
    bd = (hid[:, None] == hid[None, :]).astype(jnp.float32)
    row = lambda a: a.reshape(1, -1)
    full = lambda shape: pl.BlockSpec(shape, lambda b, c: (0,) * len(shape))
    state_spec = pl.BlockSpec((1, RW_HEADS, RW_HEAD_DIM, RW_HEAD_DIM), lambda b, c: (b, 0, 0, 0))
    prev_spec = pl.BlockSpec((1, 1, RW_COLS), lambda b, c: (b, 0, 0))
    y, m_out, p_out = pl.pallas_call(
        _rwkv_kernel,
        grid=(B, S // C),
        in_specs=[
            pl.BlockSpec((1, C, RW_COLS), lambda b, c: (b, c, 0)), state_spec, prev_spec,
            full((1, RW_COLS)), full((1, RW_DIM)), full((RW_DECAY_LORA, RW_DIM)), full((1, RW_DIM)),
            full((RW_A_LORA, RW_DIM)), full((RW_GATE_LORA, RW_DIM)), full((1, RW_DIM)), full((1, RW_DIM)),
            full((1, RW_DIM)), full((1, RW_DIM)), full((1, RW_DIM)), full((RW_DIM, RW_DIM)),
        ],
        out_specs=[pl.BlockSpec((1, C, RW_DIM), lambda b, c: (b, c, 0)), state_spec, prev_spec],
        out_shape=[jax.ShapeDtypeStruct((B, S, RW_DIM), out_dtype),
                   jax.ShapeDtypeStruct(carry[0].shape, jnp.float32),
                   jax.ShapeDtypeStruct(carry[1].shape, jnp.float32)],
        scratch_shapes=[
            pltpu.VMEM((RW_HEADS, RW_HEAD_DIM, RW_HEAD_DIM), jnp.float32),
            pltpu.VMEM((1, RW_COLS), jnp.float32),
            pltpu.VMEM((C, RW_DIM), jnp.float32),
        ],
        compiler_params=pltpu.CompilerParams(dimension_semantics=("parallel", "arbitrary")),
        name="rwkv7_time_mix",
    )(z_rw, carry[0], carry[1], row(mu), row(w0), w2, row(a0), a2, g2, row(k_k), row(k_a), row(r_k),
      row(gn_g), row(gn_b), bd)
    return y, (m_out, p_out)


DSA_TQ = 256
MASK_NEG = -1e30
INT_MIN = -2 ** 31
KEY_NEG_INF = -2139095041
THRESH_BITS = 32


def _dsa_prep_kernel(z_ref, kvg_ref, kig_ref, kib_ref, qi_ref, kv_ref, ki_ref):
    z = z_ref[0]
    qi_ref[0] = z[:, SMALL_QI:SMALL_KV].astype(qi_ref.dtype)
    kv = z[:, SMALL_KV:SMALL_KI]
    ms = jnp.mean(jnp.square(kv), -1, keepdims=True)
    kv_ref[0] = (kv * lax.rsqrt(ms + LN_EPS) * kvg_ref[...]).astype(kv_ref.dtype)
    ki = z[:, SMALL_KI:SMALL_WI]
    mu = jnp.mean(ki, -1, keepdims=True)
    var = jnp.mean(jnp.square(ki - mu), -1, keepdims=True)
    ki_ref[0] = ((ki - mu) * lax.rsqrt(var + LN_EPS) * kig_ref[...] + kib_ref[...]).astype(ki_ref.dtype)


def _dsa_prep(z_small, kv_g, ki_g, ki_b, tm=512):
    B, S, W = z_small.shape
    assert S % tm == 0
    return pl.pallas_call(
        _dsa_prep_kernel,
        grid=(B, S // tm),
        in_specs=[
            pl.BlockSpec((1, tm, W), lambda b, i: (b, i, 0)),
            pl.BlockSpec((1, DSA_LATENT), lambda b, i: (0, 0)),
            pl.BlockSpec((1, IDX_DIM), lambda b, i: (0, 0)),
            pl.BlockSpec((1, IDX_DIM), lambda b, i: (0, 0)),
        ],
        out_specs=[
            pl.BlockSpec((1, tm, IDX_HEADS * IDX_DIM), lambda b, i: (b, i, 0)),
            pl.BlockSpec((1, tm, DSA_LATENT), lambda b, i: (b, i, 0)),
            pl.BlockSpec((1, tm, IDX_DIM), lambda b, i: (b, i, 0)),
        ],
        out_shape=[
            jax.ShapeDtypeStruct((B, S, IDX_HEADS * IDX_DIM), MM_DTYPE),
            jax.ShapeDtypeStruct((B, S, DSA_LATENT), MM_DTYPE),
            jax.ShapeDtypeStruct((B, S, IDX_DIM), MM_DTYPE),
        ],
        name="dsa_prep",
    )(z_small, kv_g[None], ki_g[None], ki_b[None])


def _sortable_key(s):
    s = jnp.where(s == 0.0, 0.0, s)
    bits = pltpu.bitcast(s, jnp.int32)
    return bits ^ ((bits >> 31) & 0x7FFFFFFF)


def _col_count(mask_i32):
    tk, tq = mask_i32.shape
    return jnp.sum(mask_i32.reshape(tk // 8, 8, tq), axis=0)


def _dsa_kernel(bfar_ref, q_ref, qi_ref, wit_ref, kv_ref, kvt_ref, ki_ref, b0_ref, b1_ref, o_ref,
                key_ref, madd_ref, m_ref, l_ref, acc_ref, *, topk, q_off):
    tq = q_ref.shape[1]
    tk = tq
    i = pl.program_id(1) + q_off
    nj = i + 1
    f32 = jnp.float32
    krow = lax.broadcasted_iota(jnp.int32, (tk, tq), 0)
    qcol = lax.broadcasted_iota(jnp.int32, (tk, tq), 1)

    qi = qi_ref[0]
    wit = wit_ref[0] * (IDX_HEADS ** -0.5)

    def score_chunk(j, carry):
        off = pl.multiple_of(j * tk, tk)
        kc = ki_ref[0, pl.ds(off, tk), :]
        s = jnp.zeros((tk, tq), f32)
        for h in range(IDX_HEADS):
            d = lax.dot_general(kc, qi[:, h * IDX_DIM:(h + 1) * IDX_DIM],
                                (((1,), (1,)), ((), ())), preferred_element_type=f32)
            s = s + wit[h:h + 1, :] * jnp.maximum(d * (IDX_DIM ** -0.5), 0.0)
        causal = (krow + j * tk) <= (qcol + i * tq)
        s = jnp.where(causal, s, -jnp.inf)
        key_ref[j] = _sortable_key(s)
        return carry

    lax.fori_loop(0, nj, score_chunk, 0)

    def count_where(pred_fn):
        def body(j, acc):
            return acc + _col_count(pred_fn(key_ref[j], j).astype(jnp.int32))
        acc = lax.fori_loop(0, nj, body, jnp.zeros((8, tq), jnp.int32))
        return jnp.sum(acc, axis=0, keepdims=True)

    def bit_step(it, t_u):
        bit = THRESH_BITS - 1 - it
        cand_u = t_u | jnp.left_shift(jnp.int32(1), bit)
        cand = cand_u ^ INT_MIN
        cnt = count_where(lambda k, j: k >= cand)
        return jnp.where(cnt >= topk, cand_u, t_u)

    t_u = lax.fori_loop(0, THRESH_BITS, bit_step, jnp.zeros((1, tq), jnp.int32))
    thr = t_u ^ INT_MIN
    cnt_gt = count_where(lambda k, j: k > thr)
    cnt_ge = count_where(lambda k, j: k >= thr)
    is_neg = thr == KEY_NEG_INF
    need = jnp.logical_and(cnt_ge > topk, jnp.logical_not(is_neg))
    n_tie_take = topk - cnt_gt

    thr_open = jnp.where(is_neg, thr, thr - 1)
    any_need = jnp.max(need.astype(jnp.int32)) > 0

    @pl.when(jnp.logical_not(any_need))
    def _():
        def body(j, carry):
            madd_ref[j] = jnp.where(key_ref[j] > thr_open, 0.0, MASK_NEG)
            return carry
        lax.fori_loop(0, nj, body, 0)

    @pl.when(any_need)
    def _():
        s_len = tk * key_ref.shape[0]
        n_bits = max(1, int(math.ceil(math.log2(s_len))))

        def idx_step(it, p):
            bit = n_bits - 1 - it
            cand = p | jnp.left_shift(jnp.int32(1), bit)
            cnt = count_where(
                lambda k, j: jnp.where(k == thr, jnp.where((krow + j * tk) < cand, 1, 0), 0))
            return jnp.where(cnt < n_tie_take, cand, p)

        p_idx = lax.fori_loop(0, n_bits, idx_step, jnp.zeros((1, tq), jnp.int32))
        p_idx = jnp.where(need, p_idx, jnp.where(is_neg, -1, s_len))

        def body(j, carry):
            k = key_ref[j]
            tie_ok = jnp.where((krow + j * tk) <= p_idx, 0.0, MASK_NEG)
            madd_ref[j] = jnp.where(k > thr, 0.0, jnp.where(k == thr, tie_ok, MASK_NEG))
            return carry
        lax.fori_loop(0, nj, body, 0)

    H, d = DSA_HEADS, DSA_LATENT
    m_ref[...] = jnp.full(m_ref.shape, MASK_NEG, f32)
    l_ref[...] = jnp.zeros(l_ref.shape, f32)
    acc_ref[...] = jnp.zeros(acc_ref.shape, f32)
    scale = DSA_LATENT ** -0.5
    q_all = jnp.concatenate([q_ref[0, :, h * d:(h + 1) * d] for h in range(H)], axis=0)

    def attend(j, bias):
        off = pl.multiple_of(j * tk, tk)
        kc = kv_ref[0, pl.ds(off, tk), :]
        kct = kvt_ref[0, :, pl.ds(off, tk)]
        lg = lax.dot_general(kc, q_all, (((1,), (1,)), ((), ())), preferred_element_type=f32)
        lg = lg * scale + bias + jnp.tile(madd_ref[j], (1, H))
        m_old = m_ref[...]
        m_new = jnp.maximum(m_old, jnp.max(lg, axis=0, keepdims=True))
        alpha = jnp.exp(m_old - m_new)
        p = jnp.exp(lg - m_new)
        l_ref[...] = alpha * l_ref[...] + jnp.sum(p, axis=0, keepdims=True)
        acc_ref[...] = alpha * acc_ref[...] + jnp.dot(kct, p.astype(kct.dtype), preferred_element_type=f32)
        m_ref[...] = m_new

    def far_body(j, carry):
        attend(j, bfar_ref[...])
        return carry

    lax.fori_loop(0, jnp.maximum(i - 1, 0), far_body, 0)

    @pl.when(i >= 1)
    def _():
        attend(i - 1, b1_ref[...])

    attend(i, b0_ref[...])

    out_t = acc_ref[...] / l_ref[...]
    for h in range(H):
        o_ref[0, :, h * d:(h + 1) * d] = out_t[:, h * tq:(h + 1) * tq].T.astype(o_ref.dtype)


def _dsa_bias_tiles(rel_bias, tq):
    dist = jnp.arange(2 * tq, dtype=jnp.int32)
    by_dist = rel_bias[_t5_bucket(dist)].T

    def toeplitz(v):
        H, P = v.shape
        skew = jnp.tile(v, (1, tq + 1))[:, :tq * (P + 1)].reshape(H, tq, P + 1)
        return skew[:, ::-1, :tq]

    b0 = toeplitz(jnp.concatenate([jnp.repeat(by_dist[:, :1], tq - 1, axis=1), by_dist[:, :tq + 1]], axis=1))
    b1 = toeplitz(jnp.concatenate([by_dist[:, 1:], by_dist[:, -1:]], axis=1))
    H = by_dist.shape[0]
    b0 = jnp.transpose(b0, (1, 0, 2)).reshape(tq, H * tq)
    b1 = jnp.transpose(b1, (1, 0, 2)).reshape(tq, H * tq)
    return b0, b1, jnp.repeat(by_dist[:, -1], tq)[None, :]


def _dsa_attention(q, qi, z_wi_t, ckv, kidx, bias_tiles, topk, q_off, out_dtype):
    B, Sq, _ = q.shape
    tq = DSA_TQ
    nq = Sq // tq
    S = ckv.shape[1]
    assert Sq % tq == 0 and S == (q_off + nq) * tq and tq >= REL_MAX_DIST and topk <= tq
    ckv_t = jnp.swapaxes(ckv, 1, 2)
    b0, b1, bfar = bias_tiles
    nk = S // tq
    H, d = DSA_HEADS, DSA_LATENT
    f32_bytes = 4
    vmem_estimate = (
        2 * nk * tq * tq * f32_bytes
        + 2 * 2 * tq * H * tq * f32_bytes
        + 2 * (2 * S * d + S * LANES) * 2
        + 6 * tq * H * tq * f32_bytes
        + 4 * tq * H * d * f32_bytes
    )
    grid_spec = pltpu.PrefetchScalarGridSpec(
        num_scalar_prefetch=0,
        grid=(B, nq),
        in_specs=[
            pl.BlockSpec((1, H * tq), lambda b, i: (0, 0)),
            pl.BlockSpec((1, tq, H * d), lambda b, i: (b, i, 0)),
            pl.BlockSpec((1, tq, IDX_HEADS * IDX_DIM), lambda b, i: (b, i, 0)),
            pl.BlockSpec((1, IDX_HEADS, tq), lambda b, i: (b, 0, i)),
            pl.BlockSpec((1, S, d), lambda b, i: (b, 0, 0)),
            pl.BlockSpec((1, d, S), lambda b, i: (b, 0, 0)),
            pl.BlockSpec((1, S, IDX_DIM), lambda b, i: (b, 0, 0)),
            pl.BlockSpec((tq, H * tq), lambda b, i: (0, 0)),
            pl.BlockSpec((tq, H * tq), lambda b, i: (0, 0)),
        ],
        out_specs=pl.BlockSpec((1, tq, H * d), lambda b, i: (b, i, 0)),
        scratch_shapes=[
            pltpu.VMEM((nk, tq, tq), jnp.int32),
            pltpu.VMEM((nk, tq, tq), jnp.float32),
            pltpu.VMEM((1, H * tq), jnp.float32),
            pltpu.VMEM((1, H * tq), jnp.float32),
            pltpu.VMEM((d, H * tq), jnp.float32),
        ],
    )
    return pl.pallas_call(
        partial(_dsa_kernel, topk=topk, q_off=q_off),
        grid_spec=grid_spec,
        out_shape=jax.ShapeDtypeStruct((B, Sq, H * d), out_dtype),
        compiler_params=pltpu.CompilerParams(vmem_limit_bytes=vmem_estimate),
        name="dsa_attention",
    )(bfar, q, qi, z_wi_t, ckv, ckv_t, kidx, b0, b1)


SC_CORES = 2
SC_SUBCORES = 16
SC_LANES = 16
PEER_SC_TOKENS = 8
PEER_SC_RING = 4
PEER_SC_UNROLL = 4
PEER_SLOTS = PEER_HEADS * PEER_TOPK


def _peer_sc_call(body, T, out_width, stage_width):
    mesh = plsc.VectorSubcoreMesh(core_axis_name="c", subcore_axis_name="s")
    return pl.kernel(
        body, mesh=mesh,
        out_type=jax.ShapeDtypeStruct((T, out_width), jnp.float32),
        scratch_types=[
            pltpu.VMEM((PEER_SC_TOKENS, PEER_SLOTS), jnp.int32),
            pltpu.VMEM((PEER_SC_TOKENS, stage_width), jnp.float32),
            pltpu.VMEM((PEER_SC_RING, PEER_TOPK, D_MODEL), jnp.float32),
            pltpu.VMEM((PEER_SC_TOKENS, out_width), jnp.float32),
            pltpu.SemaphoreType.DMA((PEER_SC_RING,)),
        ],
        compiler_params=pltpu.CompilerParams(needs_layout_passes=False),
    )


def _peer_sc_body(compute, zero_out, x_hbm, idx_hbm, tab_hbm, out_hbm, idx_v, x_v, rows_v, out_v, sems):
    T = idx_hbm.shape[0]
    tokens_per_worker = T // (SC_CORES * SC_SUBCORES)
    n_steps = PEER_SC_TOKENS * PEER_HEADS
    worker = lax.axis_index("s") * SC_CORES + lax.axis_index("c")
    base = worker * tokens_per_worker

    def gather(s, b):
        ids = idx_v[s // PEER_HEADS, pl.ds((s % PEER_HEADS) * PEER_TOPK, PEER_TOPK)]
        return pltpu.make_async_copy(tab_hbm.at[ids], rows_v.at[b], sems.at[b])

    @pl.loop(0, tokens_per_worker // PEER_SC_TOKENS)
    def _(blk):
        tok0 = base + blk * PEER_SC_TOKENS
        pltpu.sync_copy(idx_hbm.at[pl.ds(tok0, PEER_SC_TOKENS)], idx_v)
        pltpu.sync_copy(x_hbm.at[pl.ds(tok0, PEER_SC_TOKENS)], x_v)
        for b in range(PEER_SC_RING - 1):
            gather(b, b).start()
        if zero_out:
            @pl.loop(0, PEER_SC_TOKENS)
            def _(t):
                @plsc.parallel_loop(0, out_v.shape[1] // SC_LANES, unroll=PEER_SC_UNROLL)
                def _(c):
                    out_v[t, pl.ds(pl.multiple_of(c * SC_LANES, SC_LANES), SC_LANES)] = (
                        jnp.zeros((SC_LANES,), jnp.float32))

        @pl.loop(0, n_steps, step=PEER_SC_RING)
        def _(s0):
            for b in range(PEER_SC_RING):
                s = s0 + b
                gather(s, b).wait()

                @pl.when(s + PEER_SC_RING - 1 < n_steps)
                def _():
                    gather(s + PEER_SC_RING - 1, (b + PEER_SC_RING - 1) % PEER_SC_RING).start()

                compute(s // PEER_HEADS, s % PEER_HEADS, b, x_v, rows_v, out_v)

        pltpu.sync_copy(out_v, out_hbm.at[pl.ds(tok0, PEER_SC_TOKENS)])


def _peer_dots_compute(t, hd, b, h_v, rows_v, dots_v):
    lane = lax.iota(jnp.int32, SC_LANES)

    def col_step(c, accs):
        off = pl.multiple_of(c * SC_LANES, SC_LANES)
        hv = h_v[t, pl.ds(off, SC_LANES)]
        return tuple(accs[r] + rows_v[b, r, pl.ds(off, SC_LANES)] * hv for r in range(PEER_TOPK))

    accs = plsc.parallel_loop(
        0, D_MODEL // SC_LANES, unroll=PEER_SC_UNROLL,
        carry=tuple(jnp.zeros((SC_LANES,), jnp.float32) for _ in range(PEER_TOPK)))(col_step)
    res = jnp.zeros((SC_LANES,), jnp.float32)
    for r in range(PEER_TOPK):
        res = jnp.where(lane == r, jnp.sum(accs[r]), res)
    dots_v[t, pl.ds(hd * PEER_TOPK, PEER_TOPK)] = res


def _peer_mix_compute(t, hd, b, act_v, rows_v, out_v):
    tvec = jnp.full((SC_LANES,), t, jnp.int32)
    weights = [plsc.load_gather(act_v, [tvec, jnp.full((SC_LANES,), hd * PEER_TOPK + r, jnp.int32)])
               for r in range(PEER_TOPK)]

    @plsc.parallel_loop(0, D_MODEL // SC_LANES, unroll=PEER_SC_UNROLL)
    def _(c):
        off = pl.multiple_of(c * SC_LANES, SC_LANES)
        a = rows_v[b, 0, pl.ds(off, SC_LANES)] * weights[0]
        for r in range(1, PEER_TOPK):
            a = a + rows_v[b, r, pl.ds(off, SC_LANES)] * weights[r]
        plsc.addupdate(out_v.at[t, pl.ds(off, SC_LANES)], a)


def _peer_expert_dots(h, experts, u_tab):
    T = h.shape[0]
    assert T % (SC_CORES * SC_SUBCORES * PEER_SC_TOKENS) == 0
    assert (PEER_SC_TOKENS * PEER_HEADS) % PEER_SC_RING == 0 and PEER_TOPK == SC_LANES
    body = partial(_peer_sc_body, _peer_dots_compute, False)
    return _peer_sc_call(body, T, PEER_SLOTS, D_MODEL)(h, experts, u_tab)


def _peer_expert_mix(act, experts, v_tab):
    T = act.shape[0]
    assert T % (SC_CORES * SC_SUBCORES * PEER_SC_TOKENS) == 0
    body = partial(_peer_sc_body, _peer_mix_compute, True)
    return _peer_sc_call(body, T, D_MODEL, PEER_SLOTS)(act, experts, v_tab)


PEER_TM = 256


def _extract_topk(s, k, payload=None):
    R = s.shape[0]
    riota = lax.broadcasted_iota(jnp.int32, s.shape, 0)
    vals, rows = [], []
    for _ in range(k):
        m = jnp.max(s, axis=0, keepdims=True)
        pos = jnp.min(jnp.where(s == m, riota, R), axis=0, keepdims=True)
        hit = riota == pos
        vals.append(m)
        rows.append(pos if payload is None else jnp.max(jnp.where(hit, payload, -1), axis=0, keepdims=True))
        s = jnp.where(hit, -jnp.inf, s)
    return jnp.concatenate(vals, axis=0), jnp.concatenate(rows, axis=0)


def _peer_route_kernel(x_ref, sc_ref, sh_ref, wq_ref, keys_ref, h_ref, ex_ref, gate_ref, q_ref, ext_ref, gt_ref):
    K = PEER_TOPK
    f32 = jnp.float32
    h = x_ref[0] * (1.0 + sc_ref[0]) + sh_ref[0]
    h_ref[0] = h
    q = jnp.dot(h.astype(wq_ref.dtype), wq_ref[...], preferred_element_type=f32)
    for hd in range(PEER_HEADS):
        q_ref[hd] = q[:, hd * PEER_KEY_DIM:(hd + 1) * PEER_KEY_DIM].astype(q_ref.dtype)

    def head_body(hd, carry):
        qh = q_ref[hd]
        s1 = lax.dot_general(keys_ref[hd, 0], qh[:, :PEER_HALF], _NT, preferred_element_type=f32)
        s2 = lax.dot_general(keys_ref[hd, 1], qh[:, PEER_HALF:], _NT, preferred_element_type=f32)
        v1, i1 = _extract_topk(s1, K)
        v2, i2 = _extract_topk(s2, K)
        tm = v1.shape[1]
        cand_rows, cidx_rows = [], []
        for a in range(K):
            nb = K // (a + 1)
            cand_rows.append(v1[a:a + 1, :] + v2[:nb, :])
            cidx_rows.append(i1[a:a + 1, :] * PEER_N_KEYS + i2[:nb, :])
        n_pad = -sum(r.shape[0] for r in cand_rows) % 8
        cand_rows.append(jnp.full((n_pad, tm), -jnp.inf, f32))
        cidx_rows.append(jnp.full((n_pad, tm), -1, jnp.int32))
        cand = jnp.concatenate(cand_rows, axis=0)
        cidx = jnp.concatenate(cidx_rows, axis=0)
        top_s, experts = _extract_topk(cand, K, payload=cidx)
        e = jnp.exp(top_s - top_s[0:1, :])
        gt_ref[pl.ds(pl.multiple_of(hd * K, K), K), :] = e / jnp.sum(e, axis=0, keepdims=True)
        ext_ref[pl.ds(pl.multiple_of(hd * K, K), K), :] = experts
        return carry

    lax.fori_loop(0, PEER_HEADS, head_body, 0)
    ex_ref[0] = ext_ref[...].T
    gate_ref[0] = gt_ref[...].T


def _peer_route(x, sc, sh, w_pq, sub_keys):
    B, S, D = x.shape
    tm = PEER_TM
    assert S % tm == 0
    return pl.pallas_call(
        _peer_route_kernel,
        grid=(B, S // tm),
        in_specs=[
            pl.BlockSpec((1, tm, D), lambda b, i: (b, i, 0)),
            pl.BlockSpec((1, 1, D), lambda b, i: (b, 0, 0)),
            pl.BlockSpec((1, 1, D), lambda b, i: (b, 0, 0)),
            pl.BlockSpec((D, PEER_HEADS * PEER_KEY_DIM), lambda b, i: (0, 0)),
            pl.BlockSpec((PEER_HEADS, 2, PEER_N_KEYS, PEER_HALF), lambda b, i: (0, 0, 0, 0)),
        ],
        out_specs=[
            pl.BlockSpec((1, tm, D), lambda b, i: (b, i, 0)),
            pl.BlockSpec((1, tm, PEER_SLOTS), lambda b, i: (b, i, 0)),
            pl.BlockSpec((1, tm, PEER_SLOTS), lambda b, i: (b, i, 0)),
        ],
        out_shape=[
            jax.ShapeDtypeStruct((B, S, D), jnp.float32),
            jax.ShapeDtypeStruct((B, S, PEER_SLOTS), jnp.int32),
            jax.ShapeDtypeStruct((B, S, PEER_SLOTS), jnp.float32),
        ],
        scratch_shapes=[
            pltpu.VMEM((PEER_HEADS, tm, PEER_KEY_DIM), MM_DTYPE),
            pltpu.VMEM((PEER_SLOTS, tm), jnp.int32),
            pltpu.VMEM((PEER_SLOTS, tm), jnp.float32),
        ],
        compiler_params=pltpu.CompilerParams(dimension_semantics=("parallel", "parallel")),
        name="peer_route",
    )(x, sc[:, None, :], sh[:, None, :], w_pq.astype(MM_DTYPE), sub_keys.astype(MM_DTYPE))


def _peer_pre(h2, experts, u_tab):
    B, S, D = h2.shape
    return _peer_expert_dots(h2.reshape(B * S, D), experts.reshape(B * S, PEER_SLOTS), u_tab)


def _peer_act_kernel(pre_ref, gate_ref, o_ref):
    pre = pre_ref[...]
    o_ref[...] = 0.5 * pre * (1.0 + lax.erf(pre * (2.0 ** -0.5))) * gate_ref[...]


def _peer_act(pre, gates):
    T, W = pre.shape
    tm = min(8 * ROW_TILE, T)
    assert T % tm == 0
    spec = pl.BlockSpec((tm, W), lambda i: (i, 0))
    return pl.pallas_call(
        _peer_act_kernel,
        grid=(T // tm,),
        in_specs=[spec, spec],
        out_specs=spec,
        out_shape=jax.ShapeDtypeStruct((T, W), jnp.float32),
        name="peer_act",
    )(pre, gates.reshape(T, W))


def _peer_out(act, experts, v_tab):
    B, S, _ = experts.shape
    return _peer_expert_mix(act, experts.reshape(B * S, PEER_SLOTS), v_tab).reshape(B, S, D_MODEL)


def kernel(x, c, w_ada, b_ada, w_in, rw_mu, rw_w0, rw_w2, rw_a0, rw_a2, rw_g2, rw_k_k, rw_k_a, rw_r_k, rw_gn_g, rw_gn_b, dsa_kv_g, idx_k_g, idx_k_b, rel_bias, w_br_a, w_br_b, w_out, ln1_g, ln1_b, peer_wq, peer_keys, peer_u, peer_v, ln2_g, ln2_b):
    l = 0
    mod = jax.nn.silu(c) @ w_ada[l] + b_ada[l]

    w_rw, w_q, w_kv, w_qi, w_ki, w_wi, w_ga, w_gb = jnp.split(w_in[l], _split_points(IN_SIZES), axis=-1)
    small_pad = jnp.zeros((D_MODEL, SMALL_COLS - SMALL_WI - IDX_HEADS), w_in.dtype)
    w_small = jnp.concatenate([w_qi, w_kv, w_ki, w_wi, small_pad], axis=-1)
    w_gates = jnp.concatenate([w_ga, w_gb], axis=-1)

    B, S, _ = x.shape
    seg_len = S // SEQ_SEGMENTS
    dsa_topk = min(TOPK_MAX, S // 4)
    bias_tiles = _dsa_bias_tiles(rel_bias, DSA_TQ)

    def project_and_dsa(x, sc1, sh1, seg, keys_so_far):
        z_rw = _mod_matmul(x, sc1, sh1, w_rw, jnp.float32, tn=896)
        z_q = _mod_matmul(x, sc1, sh1, w_q, MM_DTYPE, tn=1024)
        z_small = _mod_matmul(x, sc1, sh1, w_small, jnp.float32, tn=512)
        z_g = _mod_matmul(x, sc1, sh1, w_gates, jnp.float32, tn=1024)
        z_wi_t = jnp.swapaxes(z_small[..., SMALL_WI:SMALL_WI + IDX_HEADS], 1, 2)
        qi, ckv, kidx = _dsa_prep(z_small, dsa_kv_g[l], idx_k_g[l], idx_k_b[l])
        if keys_so_far is not None:
            ckv = jnp.concatenate([keys_so_far[0], ckv], axis=1)
            kidx = jnp.concatenate([keys_so_far[1], kidx], axis=1)
        y_b = _dsa_attention(z_q, qi, z_wi_t, ckv, kidx, bias_tiles, dsa_topk,
                             seg * (seg_len // DSA_TQ), MM_DTYPE)
        return (y_b, z_rw, z_g), (ckv, kidx)

    def rwkv_and_merge(x, gt1, carry, y_b, z_rw, z_g):
        y_a, carry = _rwkv7_time_mix(z_rw, carry, rw_mu[l], rw_w0[l], rw_w2[l], rw_a0[l], rw_a2[l], rw_g2[l],
                                     rw_k_k[l], rw_k_a[l], rw_r_k[l], rw_gn_g[l], rw_gn_b[l], MM_DTYPE)
        return _merge_ln(x, gt1, y_a, y_b, z_g, w_br_a[l], w_br_b[l], w_out[l], ln1_g[l], ln1_b[l]), carry

    def finish(st, act):
        y2 = _peer_out(act, st["experts"], peer_v[l])
        return _residual_ln(st["x1"], st["gt2"], y2, ln2_g[l], ln2_b[l])

    assert B % BATCH_GROUPS == 0 and S % (SEQ_SEGMENTS * DSA_TQ) == 0
    gsz = B // BATCH_GROUPS
    outs = []
    prev = None
    for g in range(BATCH_GROUPS):
        sh1, sc1, gt1, sh2, sc2, gt2 = jnp.split(mod[g * gsz:(g + 1) * gsz], 6, axis=-1)
        rw_carry = _rwkv7_init_carry(gsz)
        keys_so_far = None
        for seg in range(SEQ_SEGMENTS):
            x_g = x[g * gsz:(g + 1) * gsz, seg * seg_len:(seg + 1) * seg_len]
            if prev is not None:
                x_g, prev["experts"] = lax.optimization_barrier((x_g, prev["experts"]))
            proj, keys_so_far = project_and_dsa(x_g, sc1, sh1, seg, keys_so_far)
            if prev is not None:
                proj, prev["pre"], outs = lax.optimization_barrier((proj, prev["pre"], outs))
                outs.append(finish(prev, _peer_act(prev["pre"], prev["gates"])))
            x1, rw_carry = rwkv_and_merge(x_g, gt1, rw_carry, *proj)
            h2, experts, gates = _peer_route(x1, sc2, sh2, peer_wq[l], peer_keys[l])
            prev = dict(x1=x1, gt2=gt2, experts=experts, gates=gates, pre=_peer_pre(h2, experts, peer_u[l]))
    prev["pre"], outs = lax.optimization_barrier((prev["pre"], outs))
    outs.append(finish(prev, _peer_act(prev["pre"], prev["gates"])))
    rows = [jnp.concatenate(outs[g * SEQ_SEGMENTS:(g + 1) * SEQ_SEGMENTS], axis=1) for g in range(BATCH_GROUPS)]
    return jnp.concatenate(rows, axis=0)
```

```python
import math
from functools import partial

import jax
import jax.numpy as jnp
import numpy as np
from jax import lax
from jax.experimental import pallas as pl
from jax.experimental.pallas import tpu as pltpu
from jax.experimental.pallas import tpu_sc as plsc

D_MODEL = 1024
RW_HEADS = 8
RW_HEAD_DIM = 64
RW_DIM = 512
RW_DECAY_LORA = 64
RW_A_LORA = 64
RW_GATE_LORA = 128
RW_COLS = 3 * RW_DIM + RW_DECAY_LORA + RW_A_LORA + RW_GATE_LORA
RW_GN_EPS = 64e-5
DSA_HEADS = 8
DSA_LATENT = 128
DSA_Q_DIM = DSA_HEADS * DSA_LATENT
IDX_HEADS = 4
IDX_DIM = 64
TOPK_MAX = 256
REL_BUCKETS = 32
REL_MAX_DIST = 128
IN_SIZES = (RW_COLS, DSA_Q_DIM, DSA_LATENT, IDX_HEADS * IDX_DIM, IDX_DIM, IDX_HEADS, D_MODEL, D_MODEL)
IN_COLS = sum(IN_SIZES)
PEER_HEADS = 8
PEER_N_KEYS = 128
PEER_KEY_DIM = 128
PEER_HALF = 64
PEER_TOPK = 16
LN_EPS = 1e-5
DEPTH = 1
DEEPNORM_ALPHA = (2.0 * DEPTH) ** 0.25

LANES = 128
SMALL_QI = 0
SMALL_KV = SMALL_QI + IDX_HEADS * IDX_DIM
SMALL_KI = SMALL_KV + DSA_LATENT
SMALL_WI = SMALL_KI + IDX_DIM
SMALL_COLS = 4 * LANES
MM_DTYPE = jnp.bfloat16
BATCH_GROUPS = 8
SEQ_SEGMENTS = 2


def _split_points(sizes):
    return np.cumsum(sizes)[:-1].tolist()


def _mod_matmul_kernel(x_ref, sc_ref, sh_ref, w_ref, o_ref):
    h = x_ref[0] * (1.0 + sc_ref[0]) + sh_ref[0]
    o_ref[0] = jnp.dot(h.astype(w_ref.dtype), w_ref[...],
                       preferred_element_type=jnp.float32).astype(o_ref.dtype)


def _mod_matmul(x, sc, sh, w, out_dtype, tn, tm=512):
    B, S, D = x.shape
    N = w.shape[1]
    assert S % tm == 0 and N % tn == 0
    return pl.pallas_call(
        _mod_matmul_kernel,
        grid=(B, S // tm, N // tn),
        in_specs=[
            pl.BlockSpec((1, tm, D), lambda b, i, j: (b, i, 0)),
            pl.BlockSpec((1, 1, D), lambda b, i, j: (b, 0, 0)),
            pl.BlockSpec((1, 1, D), lambda b, i, j: (b, 0, 0)),
            pl.BlockSpec((D, tn), lambda b, i, j: (0, j)),
        ],
        out_specs=pl.BlockSpec((1, tm, tn), lambda b, i, j: (b, i, j)),
        out_shape=jax.ShapeDtypeStruct((B, S, N), out_dtype),
        name="mod_matmul",
    )(x, sc[:, None, :], sh[:, None, :], w.astype(MM_DTYPE))


ROW_TILE = 256


def _layer_norm(x, g, b):
    mu = jnp.mean(x, -1, keepdims=True)
    var = jnp.mean(jnp.square(x - mu), -1, keepdims=True)
    return (x - mu) * lax.rsqrt(var + LN_EPS) * g + b


def _merge_ln_kernel(x_ref, gt_ref, ya_ref, yb_ref, zg_ref, wa_ref, wb_ref, wo_ref, g_ref, b_ref, o_ref):
    f32 = jnp.float32
    D = x_ref.shape[2]
    ya = jnp.dot(ya_ref[0], wa_ref[...], preferred_element_type=f32)
    yb = jnp.dot(yb_ref[0], wb_ref[...], preferred_element_type=f32)
    zg = zg_ref[0]
    merged = jax.nn.sigmoid(zg[:, :D]) * ya + jax.nn.sigmoid(zg[:, D:]) * yb
    mix = jnp.dot(merged.astype(wo_ref.dtype), wo_ref[...], preferred_element_type=f32)
    o_ref[0] = _layer_norm(DEEPNORM_ALPHA * x_ref[0] + gt_ref[0] * mix, g_ref[...], b_ref[...])


def _merge_ln(x, gt, y_a, y_b, z_g, w_a, w_b, w_o, ln_g, ln_b):
    B, S, D = x.shape
    tm = ROW_TILE
    assert S % tm == 0
    tok = lambda width: pl.BlockSpec((1, tm, width), lambda b, i: (b, i, 0))
    full = lambda a: pl.BlockSpec(a.shape, lambda b, i: (0,) * a.ndim)
    w_a, w_b, w_o = (w.astype(MM_DTYPE) for w in (w_a, w_b, w_o))
    ln_g, ln_b = ln_g[None], ln_b[None]
    return pl.pallas_call(
        _merge_ln_kernel,
        grid=(B, S // tm),
        in_specs=[tok(D), pl.BlockSpec((1, 1, D), lambda b, i: (b, 0, 0)), tok(y_a.shape[2]), tok(y_b.shape[2]),
                  tok(z_g.shape[2]), full(w_a), full(w_b), full(w_o), full(ln_g), full(ln_b)],
        out_specs=tok(D),
        out_shape=jax.ShapeDtypeStruct((B, S, D), jnp.float32),
        compiler_params=pltpu.CompilerParams(dimension_semantics=("parallel", "parallel")),
        name="merge_ln",
    )(x, gt[:, None, :], y_a, y_b, z_g, w_a, w_b, w_o, ln_g, ln_b)


def _residual_ln_kernel(x_ref, gt_ref, y_ref, g_ref, b_ref, o_ref):
    o_ref[0] = _layer_norm(DEEPNORM_ALPHA * x_ref[0] + gt_ref[0] * y_ref[0], g_ref[...], b_ref[...])


def _residual_ln(x, gt, y, ln_g, ln_b):
    B, S, D = x.shape
    tm = ROW_TILE
    assert S % tm == 0
    tok = pl.BlockSpec((1, tm, D), lambda b, i: (b, i, 0))
    vec = pl.BlockSpec((1, D), lambda b, i: (0, 0))
    return pl.pallas_call(
        _residual_ln_kernel,
        grid=(B, S // tm),
        in_specs=[tok, pl.BlockSpec((1, 1, D), lambda b, i: (b, 0, 0)), tok, vec, vec],
        out_specs=tok,
        out_shape=jax.ShapeDtypeStruct((B, S, D), jnp.float32),
        compiler_params=pltpu.CompilerParams(dimension_semantics=("parallel", "parallel")),
        name="residual_ln",
    )(x, gt[:, None, :], y, ln_g[None], ln_b[None])


def _t5_bucket(n):
    n = jnp.maximum(n, 0)
    max_exact = REL_BUCKETS // 2
    nf = jnp.maximum(n, 1).astype(jnp.float32)
    large = max_exact + (jnp.log(nf / max_exact) / math.log(REL_MAX_DIST / max_exact)
                         * (REL_BUCKETS - max_exact)).astype(jnp.int32)
    large = jnp.minimum(large, REL_BUCKETS - 1)
    return jnp.where(n < max_exact, n, large)


RW_CHUNK = 64
RW_INV_BLOCK = 16
_NN = (((1,), (0,)), ((), ()))
_NT = (((1,), (1,)), ((), ()))
_BNN = (((2,), (1,)), ((0,), (0,)))
_BNT = (((2,), (2,)), ((0,), (0,)))


def _dot_f32(a, b, dims=_NN):
    return lax.dot_general(a, b, dims, precision=lax.Precision.HIGHEST,
                           preferred_element_type=jnp.float32)


def _dot_bf16x3(a, b, dims=_NN):
    f32, bf = jnp.float32, jnp.bfloat16
    a_hi, b_hi = a.astype(bf), b.astype(bf)
    a_lo = (a - a_hi.astype(f32)).astype(bf)
    b_lo = (b - b_hi.astype(f32)).astype(bf)
    out = lax.dot_general(a_hi, b_hi, dims, preferred_element_type=f32)
    out = out + lax.dot_general(a_hi, b_lo, dims, preferred_element_type=f32)
    return out + lax.dot_general(a_lo, b_hi, dims, preferred_element_type=f32)


def _bf16_terms(a):
    f32, bf = jnp.float32, jnp.bfloat16
    hi = a.astype(bf)
    r1 = a - hi.astype(f32)
    mid = r1.astype(bf)
    lo = (r1 - mid.astype(f32)).astype(bf)
    return hi, mid, lo


def _dot_lhs_split(a, b01):
    b = b01.astype(jnp.bfloat16)
    return sum(jnp.dot(t, b, preferred_element_type=jnp.float32) for t in _bf16_terms(a))


def _dot_rhs_split(a01, b):
    a = a01.astype(jnp.bfloat16)
    return sum(jnp.dot(a, t, preferred_element_type=jnp.float32) for t in _bf16_terms(b))


def _rwkv_kernel(z_ref, m0_ref, p0_ref, mu_ref, w0_ref, w2_ref, a0_ref, a2_ref, g2_ref, kk_ref, ka_ref, rk_ref,
                 gng_ref, gnb_ref, bd_ref, o_ref, mo_ref, po_ref, m_ref, prev_ref, y_ref):
    C = z_ref.shape[1]
    N = RW_HEAD_DIM
    f32 = jnp.float32
    dot3 = _dot_bf16x3

    @pl.when(pl.program_id(1) == 0)
    def _():
        m_ref[...] = m0_ref[0]
        prev_ref[...] = p0_ref[0]

    z = z_ref[0]
    row = lax.broadcasted_iota(jnp.int32, z.shape, 0)
    shifted = jnp.where(row == 0, prev_ref[...], pltpu.roll(z, 1, axis=0))
    prev_ref[...] = z[C - 1:C, :]
    zs = z + (shifted - z) * mu_ref[...]
    r = zs[:, 0:RW_DIM]
    k = zs[:, RW_DIM:2 * RW_DIM]
    v = zs[:, 2 * RW_DIM:3 * RW_DIM]
    o1 = 3 * RW_DIM
    wl = zs[:, o1:o1 + RW_DECAY_LORA]
    al = zs[:, o1 + RW_DECAY_LORA:o1 + RW_DECAY_LORA + RW_A_LORA]
    gl = zs[:, o1 + RW_DECAY_LORA + RW_A_LORA:]

    bd = bd_ref[...]
    log_w = -jax.nn.softplus(-(w0_ref[...] + dot3(jnp.tanh(wl), w2_ref[...]))) - 0.5
    ldec = -jnp.exp(log_w)
    a_lr = jax.nn.sigmoid(a0_ref[...] + dot3(al, a2_ref[...]))
    g = dot3(jax.nn.sigmoid(gl), g2_ref[...])
    kk = k * kk_ref[...]
    kk = kk * lax.rsqrt(jnp.maximum(_dot_lhs_split(kk * kk, bd), 1e-24))
    k2 = k * (1.0 + (a_lr - 1.0) * ka_ref[...])
    a_vec = -kk
    b_vec = kk * a_lr

    ti = lax.broadcasted_iota(jnp.int32, (C, C), 0)
    tj = lax.broadcasted_iota(jnp.int32, (C, C), 1)
    cum = _dot_rhs_split((ti >= tj).astype(f32), ldec)
    cum_last = cum[C - 1:C, :]
    w_incl = jnp.exp(cum)
    w_excl = jnp.exp(cum - ldec)
    w_inv = jnp.exp(-cum)
    w_end = jnp.exp(cum_last - cum)
    w_all = jnp.exp(cum_last)
    a_t = a_vec * w_excl
    r_t = r * w_incl
    b_t = b_vec * w_inv
    k_t = k2 * w_inv
    b_e = b_vec * w_end
    k_e = k2 * w_end

    strict = ti > tj
    incl = ti >= tj
    bi, bj = ti // RW_INV_BLOCK, tj // RW_INV_BLOCK
    same_blk = bi == bj
    pair_blk = jnp.logical_and((bi // 2) == (bj // 2), jnp.logical_not(same_blk))
    half_blk = (bi // 2) != (bj // 2)
    eye = (ti == tj).astype(f32)

    H = RW_HEADS
    heads = lambda x: jnp.stack([x[:, h * N:(h + 1) * N] for h in range(H)], axis=0)
    bmm = lambda a, b: dot3(a, b, _BNN)
    A, Rt, Bt, Kt, Be, Ke, V = (heads(t) for t in (a_t, r_t, b_t, k_t, b_e, k_e, v))
    gm = dot3(jnp.concatenate([A, Rt], axis=1), jnp.concatenate([Bt, Kt], axis=1), _BNT)
    a_ab = jnp.where(strict, gm[:, :C, :C], 0.0)
    a_ak = jnp.where(strict, gm[:, :C, C:], 0.0)
    a_rb = jnp.where(incl, gm[:, C:, :C], 0.0)
    a_rk = jnp.where(incl, gm[:, C:, C:], 0.0)
    d1 = jnp.where(same_blk, a_ab, 0.0)
    xinv = eye + d1
    d2 = bmm(d1, d1)
    xinv = xinv + bmm(xinv, d2)
    d4 = bmm(d2, d2)
    xinv = xinv + bmm(xinv, d4)
    d8 = bmm(d4, d4)
    xinv = xinv + bmm(xinv, d8)
    xinv = xinv + bmm(bmm(xinv, jnp.where(pair_blk, a_ab, 0.0)), xinv)
    xinv = xinv + bmm(bmm(xinv, jnp.where(half_blk, a_ab, 0.0)), xinv)

    av = bmm(jnp.concatenate([a_ak, a_rk], axis=1), V)
    p = bmm(xinv, jnp.concatenate([A, av[:, :C]], axis=2))
    qm = bmm(a_rb, p)
    q1 = Rt + qm[:, :, :N]
    q2 = qm[:, :, N:] + av[:, C:]
    gmat = bmm(jnp.swapaxes(Be, 1, 2), p)
    g1 = eye * heads(w_all) + gmat[:, :, :N]
    g2 = gmat[:, :, N:] + bmm(jnp.swapaxes(Ke, 1, 2), V)
    m = m_ref[...]
    yh = _dot_f32(q1, m, _BNN) + q2
    m_ref[...] = _dot_f32(g1, m, _BNN) + g2
    for h in range(H):
        y_ref[:, h * N:(h + 1) * N] = yh[h]

    y = y_ref[...]
    mean = _dot_lhs_split(y, bd) * (1.0 / N)
    yc = y - mean
    var = _dot_lhs_split(yc * yc, bd) * (1.0 / N)
    yn = yc * lax.rsqrt(var + RW_GN_EPS) * gng_ref[...] + gnb_ref[...]
    bonus = _dot_lhs_split(r * k2 * rk_ref[...], bd) * v
    o_ref[0] = ((yn + bonus) * g).astype(o_ref.dtype)

    @pl.when(pl.program_id(1) == pl.num_programs(1) - 1)
    def _():
        mo_ref[0] = m_ref[...]
        po_ref[0] = prev_ref[...]


def _rwkv7_init_carry(batch):
    return (jnp.zeros((batch, RW_HEADS, RW_HEAD_DIM, RW_HEAD_DIM), jnp.float32),
            jnp.zeros((batch, 1, RW_COLS), jnp.float32))


def _rwkv7_time_mix(z_rw, carry, mu, w0, w2, a0, a2, g2, k_k, k_a, r_k, gn_g, gn_b, out_dtype):
    B, S, _ = z_rw.shape
    C = RW_CHUNK
    assert S % C == 0 and C == RW_HEAD_DIM and C % (4 * RW_INV_BLOCK) == 0
    hid = jnp.arange(RW_DIM) // RW_HEAD_DIM
    bd = (hid[:, None] == hid[None, :]).astype(jnp.float32)
    row = lambda a: a.reshape(1, -1)
    full = lambda shape: pl.BlockSpec(shape, lambda b, c: (0,) * len(shape))
    state_spec = pl.BlockSpec((1, RW_HEADS, RW_HEAD_DIM, RW_HEAD_DIM), lambda b, c: (b, 0, 0, 0))
    prev_spec = pl.BlockSpec((1, 1, RW_COLS), lambda b, c: (b, 0, 0))
    y, m_out, p_out = pl.pallas_call(
        _rwkv_kernel,
        grid=(B, S // C),
        in_specs=[
            pl.BlockSpec((1, C, RW_COLS), lambda b, c: (b, c, 0)), state_spec, prev_spec,
            full((1, RW_COLS)), full((1, RW_DIM)), full((RW_DECAY_LORA, RW_DIM)), full((1, RW_DIM)),
            full((RW_A_LORA, RW_DIM)), full((RW_GATE_LORA, RW_DIM)), full((1, RW_DIM)), full((1, RW_DIM)),
            full((1, RW_DIM)), full((1, RW_DIM)), full((1, RW_DIM)), full((RW_DIM, RW_DIM)),
        ],
        out_specs=[pl.BlockSpec((1, C, RW_DIM), lambda b, c: (b, c, 0)), state_spec, prev_spec],
        out_shape=[jax.ShapeDtypeStruct((B, S, RW_DIM), out_dtype),
                   jax.ShapeDtypeStruct(carry[0].shape, jnp.float32),
                   jax.ShapeDtypeStruct(carry[1].shape, jnp.float32)],
        scratch_shapes=[
            pltpu.VMEM((RW_HEADS, RW_HEAD_DIM, RW_HEAD_DIM), jnp.float32),
            pltpu.VMEM((1, RW_COLS), jnp.float32),
            pltpu.VMEM((C, RW_DIM), jnp.float32),
        ],
        compiler_params=pltpu.CompilerParams(dimension_semantics=("parallel", "arbitrary")),
        name="rwkv7_time_mix",
    )(z_rw, carry[0], carry[1], row(mu), row(w0), w2, row(a0), a2, g2, row(k_k), row(k_a), row(r_k),
      row(gn_g), row(gn_b), bd)
    return y, (m_out, p_out)


DSA_TQ = 256
MASK_NEG = -1e30
INT_MIN = -2 ** 31
KEY_NEG_INF = -2139095041
THRESH_BITS = 32


def _dsa_prep_kernel(z_ref, kvg_ref, kig_ref, kib_ref, qi_ref, kv_ref, ki_ref):
    z = z_ref[0]
    qi_ref[0] = z[:, SMALL_QI:SMALL_KV].astype(qi_ref.dtype)
    kv = z[:, SMALL_KV:SMALL_KI]
    ms = jnp.mean(jnp.square(kv), -1, keepdims=True)
    kv_ref[0] = (kv * lax.rsqrt(ms + LN_EPS) * kvg_ref[...]).astype(kv_ref.dtype)
    ki = z[:, SMALL_KI:SMALL_WI]
    mu = jnp.mean(ki, -1, keepdims=True)
    var = jnp.mean(jnp.square(ki - mu), -1, keepdims=True)
    ki_ref[0] = ((ki - mu) * lax.rsqrt(var + LN_EPS) * kig_ref[...] + kib_ref[...]).astype(ki_ref.dtype)


def _dsa_prep(z_small, kv_g, ki_g, ki_b, tm=512):
    B, S, W = z_small.shape
    assert S % tm == 0
    return pl.pallas_call(
        _dsa_prep_kernel,
        grid=(B, S // tm),
        in_specs=[
            pl.BlockSpec((1, tm, W), lambda b, i: (b, i, 0)),
            pl.BlockSpec((1, DSA_LATENT), lambda b, i: (0, 0)),
            pl.BlockSpec((1, IDX_DIM), lambda b, i: (0, 0)),
            pl.BlockSpec((1, IDX_DIM), lambda b, i: (0, 0)),
        ],
        out_specs=[
            pl.BlockSpec((1, tm, IDX_HEADS * IDX_DIM), lambda b, i: (b, i, 0)),
            pl.BlockSpec((1, tm, DSA_LATENT), lambda b, i: (b, i, 0)),
            pl.BlockSpec((1, tm, IDX_DIM), lambda b, i: (b, i, 0)),
        ],
        out_shape=[
            jax.ShapeDtypeStruct((B, S, IDX_HEADS * IDX_DIM), MM_DTYPE),
            jax.ShapeDtypeStruct((B, S, DSA_LATENT), MM_DTYPE),
            jax.ShapeDtypeStruct((B, S, IDX_DIM), MM_DTYPE),
        ],
        name="dsa_prep",
    )(z_small, kv_g[None], ki_g[None], ki_b[None])


def _sortable_key(s):
    s = jnp.where(s == 0.0, 0.0, s)
    bits = pltpu.bitcast(s, jnp.int32)
    return bits ^ ((bits >> 31) & 0x7FFFFFFF)


def _col_count(mask_i32):
    tk, tq = mask_i32.shape
    return jnp.sum(mask_i32.reshape(tk // 8, 8, tq), axis=0)


def _dsa_kernel(bfar_ref, q_ref, qi_ref, wit_ref, kv_ref, kvt_ref, ki_ref, b0_ref, b1_ref, o_ref,
                key_ref, madd_ref, m_ref, l_ref, acc_ref, *, topk, q_off):
    tq = q_ref.shape[1]
    tk = tq
    i = pl.program_id(1) + q_off
    nj = i + 1
    f32 = jnp.float32
    krow = lax.broadcasted_iota(jnp.int32, (tk, tq), 0)
    qcol = lax.broadcasted_iota(jnp.int32, (tk, tq), 1)

    qi = qi_ref[0]
    wit = wit_ref[0] * (IDX_HEADS ** -0.5)

    def score_chunk(j, carry):
        off = pl.multiple_of(j * tk, tk)
        kc = ki_ref[0, pl.ds(off, tk), :]
        s = jnp.zeros((tk, tq), f32)
        for h in range(IDX_HEADS):
            d = lax.dot_general(kc, qi[:, h * IDX_DIM:(h + 1) * IDX_DIM],
                                (((1,), (1,)), ((), ())), preferred_element_type=f32)
            s = s + wit[h:h + 1, :] * jnp.maximum(d * (IDX_DIM ** -0.5), 0.0)
        causal = (krow + j * tk) <= (qcol + i * tq)
        s = jnp.where(causal, s, -jnp.inf)
        key_ref[j] = _sortable_key(s)
        return carry

    lax.fori_loop(0, nj, score_chunk, 0)

    def count_where(pred_fn):
        def body(j, acc):
            return acc + _col_count(pred_fn(key_ref[j], j).astype(jnp.int32))
        acc = lax.fori_loop(0, nj, body, jnp.zeros((8, tq), jnp.int32))
        return jnp.sum(acc, axis=0, keepdims=True)

    def bit_step(it, t_u):
        bit = THRESH_BITS - 1 - it
        cand_u = t_u | jnp.left_shift(jnp.int32(1), bit)
        cand = cand_u ^ INT_MIN
        cnt = count_where(lambda k, j: k >= cand)
        return jnp.where(cnt >= topk, cand_u, t_u)

    t_u = lax.fori_loop(0, THRESH_BITS, bit_step, jnp.zeros((1, tq), jnp.int32))
    thr = t_u ^ INT_MIN
    cnt_gt = count_where(lambda k, j: k > thr)
    cnt_ge = count_where(lambda k, j: k >= thr)
    is_neg = thr == KEY_NEG_INF
    need = jnp.logical_and(cnt_ge > topk, jnp.logical_not(is_neg))
    n_tie_take = topk - cnt_gt

    thr_open = jnp.where(is_neg, thr, thr - 1)
    any_need = jnp.max(need.astype(jnp.int32)) > 0

    @pl.when(jnp.logical_not(any_need))
    def _():
        def body(j, carry):
            madd_ref[j] = jnp.where(key_ref[j] > thr_open, 0.0, MASK_NEG)
            return carry
        lax.fori_loop(0, nj, body, 0)

    @pl.when(any_need)
    def _():
        s_len = tk * key_ref.shape[0]
        n_bits = max(1, int(math.ceil(math.log2(s_len))))

        def idx_step(it, p):
            bit = n_bits - 1 - it
            cand = p | jnp.left_shift(jnp.int32(1), bit)
            cnt = count_where(
                lambda k, j: jnp.where(k == thr, jnp.where((krow + j * tk) < cand, 1, 0), 0))
            return jnp.where(cnt < n_tie_take, cand, p)

        p_idx = lax.fori_loop(0, n_bits, idx_step, jnp.zeros((1, tq), jnp.int32))
        p_idx = jnp.where(need, p_idx, jnp.where(is_neg, -1, s_len))

        def body(j, carry):
            k = key_ref[j]
            tie_ok = jnp.where((krow + j * tk) <= p_idx, 0.0, MASK_NEG)
            madd_ref[j] = jnp.where(k > thr, 0.0, jnp.where(k == thr, tie_ok, MASK_NEG))
            return carry
        lax.fori_loop(0, nj, body, 0)

    H, d = DSA_HEADS, DSA_LATENT
    m_ref[...] = jnp.full(m_ref.shape, MASK_NEG, f32)
    l_ref[...] = jnp.zeros(l_ref.shape, f32)
    acc_ref[...] = jnp.zeros(acc_ref.shape, f32)
    scale = DSA_LATENT ** -0.5
    q_all = jnp.concatenate([q_ref[0, :, h * d:(h + 1) * d] for h in range(H)], axis=0)

    def attend(j, bias):
        off = pl.multiple_of(j * tk, tk)
        kc = kv_ref[0, pl.ds(off, tk), :]
        kct = kvt_ref[0, :, pl.ds(off, tk)]
        lg = lax.dot_general(kc, q_all, (((1,), (1,)), ((), ())), preferred_element_type=f32)
        lg = lg * scale + bias + jnp.tile(madd_ref[j], (1, H))
        m_old = m_ref[...]
        m_new = jnp.maximum(m_old, jnp.max(lg, axis=0, keepdims=True))
        alpha = jnp.exp(m_old - m_new)
        p = jnp.exp(lg - m_new)
        l_ref[...] = alpha * l_ref[...] + jnp.sum(p, axis=0, keepdims=True)
        acc_ref[...] = alpha * acc_ref[...] + jnp.dot(kct, p.astype(kct.dtype), preferred_element_type=f32)
        m_ref[...] = m_new

    def far_body(j, carry):
        attend(j, bfar_ref[...])
        return carry

    lax.fori_loop(0, jnp.maximum(i - 1, 0), far_body, 0)

    @pl.when(i >= 1)
    def _():
        attend(i - 1, b1_ref[...])

    attend(i, b0_ref[...])

    out_t = acc_ref[...] / l_ref[...]
    for h in range(H):
        o_ref[0, :, h * d:(h + 1) * d] = out_t[:, h * tq:(h + 1) * tq].T.astype(o_ref.dtype)


def _dsa_bias_tiles(rel_bias, tq):
    dist = jnp.arange(2 * tq, dtype=jnp.int32)
    by_dist = rel_bias[_t5_bucket(dist)].T

    def toeplitz(v):
        H, P = v.shape
        skew = jnp.tile(v, (1, tq + 1))[:, :tq * (P + 1)].reshape(H, tq, P + 1)
        return skew[:, ::-1, :tq]

    b0 = toeplitz(jnp.concatenate([jnp.repeat(by_dist[:, :1], tq - 1, axis=1), by_dist[:, :tq + 1]], axis=1))
    b1 = toeplitz(jnp.concatenate([by_dist[:, 1:], by_dist[:, -1:]], axis=1))
    H = by_dist.shape[0]
    b0 = jnp.transpose(b0, (1, 0, 2)).reshape(tq, H * tq)
    b1 = jnp.transpose(b1, (1, 0, 2)).reshape(tq, H * tq)
    return b0, b1, jnp.repeat(by_dist[:, -1], tq)[None, :]


def _dsa_attention(q, qi, z_wi_t, ckv, kidx, bias_tiles, topk, q_off, out_dtype):
    B, Sq, _ = q.shape
    tq = DSA_TQ
    nq = Sq // tq
    S = ckv.shape[1]
    assert Sq % tq == 0 and S == (q_off + nq) * tq and tq >= REL_MAX_DIST and topk <= tq
    ckv_t = jnp.swapaxes(ckv, 1, 2)
    b0, b1, bfar = bias_tiles
    nk = S // tq
    H, d = DSA_HEADS, DSA_LATENT
    f32_bytes = 4
    vmem_estimate = (
        2 * nk * tq * tq * f32_bytes
        + 2 * 2 * tq * H * tq * f32_bytes
        + 2 * (2 * S * d + S * LANES) * 2
        + 6 * tq * H * tq * f32_bytes
        + 4 * tq * H * d * f32_bytes
    )
    grid_spec = pltpu.PrefetchScalarGridSpec(
        num_scalar_prefetch=0,
        grid=(B, nq),
        in_specs=[
            pl.BlockSpec((1, H * tq), lambda b, i: (0, 0)),
            pl.BlockSpec((1, tq, H * d), lambda b, i: (b, i, 0)),
            pl.BlockSpec((1, tq, IDX_HEADS * IDX_DIM), lambda b, i: (b, i, 0)),
            pl.BlockSpec((1, IDX_HEADS, tq), lambda b, i: (b, 0, i)),
            pl.BlockSpec((1, S, d), lambda b, i: (b, 0, 0)),
            pl.BlockSpec((1, d, S), lambda b, i: (b, 0, 0)),
            pl.BlockSpec((1, S, IDX_DIM), lambda b, i: (b, 0, 0)),
            pl.BlockSpec((tq, H * tq), lambda b, i: (0, 0)),
            pl.BlockSpec((tq, H * tq), lambda b, i: (0, 0)),
        ],
        out_specs=pl.BlockSpec((1, tq, H * d), lambda b, i: (b, i, 0)),
        scratch_shapes=[
            pltpu.VMEM((nk, tq, tq), jnp.int32),
            pltpu.VMEM((nk, tq, tq), jnp.float32),
            pltpu.VMEM((1, H * tq), jnp.float32),
            pltpu.VMEM((1, H * tq), jnp.float32),
            pltpu.VMEM((d, H * tq), jnp.float32),
        ],
    )
    return pl.pallas_call(
        partial(_dsa_kernel, topk=topk, q_off=q_off),
        grid_spec=grid_spec,
        out_shape=jax.ShapeDtypeStruct((B, Sq, H * d), out_dtype),
        compiler_params=pltpu.CompilerParams(vmem_limit_bytes=vmem_estimate),
        name="dsa_attention",
    )(bfar, q, qi, z_wi_t, ckv, ckv_t, kidx, b0, b1)


SC_CORES = 2
SC_SUBCORES = 16
SC_LANES = 16
PEER_SC_TOKENS = 8
PEER_SC_RING = 4
PEER_SC_UNROLL = 4
PEER_SLOTS = PEER_HEADS * PEER_TOPK


def _peer_sc_call(body, T, out_width, stage_width):
    mesh = plsc.VectorSubcoreMesh(core_axis_name="c", subcore_axis_name="s")
    return pl.kernel(
        body, mesh=mesh,
        out_type=jax.ShapeDtypeStruct((T, out_width), jnp.float32),
        scratch_types=[
            pltpu.VMEM((PEER_SC_TOKENS, PEER_SLOTS), jnp.int32),
            pltpu.VMEM((PEER_SC_TOKENS, stage_width), jnp.float32),
            pltpu.VMEM((PEER_SC_RING, PEER_TOPK, D_MODEL), jnp.float32),
            pltpu.VMEM((PEER_SC_TOKENS, out_width), jnp.float32),
            pltpu.SemaphoreType.DMA((PEER_SC_RING,)),
        ],
        compiler_params=pltpu.CompilerParams(needs_layout_passes=False),
    )


def _peer_sc_body(compute, zero_out, x_hbm, idx_hbm, tab_hbm, out_hbm, idx_v, x_v, rows_v, out_v, sems):
    T = idx_hbm.shape[0]
    tokens_per_worker = T // (SC_CORES * SC_SUBCORES)
    n_steps = PEER_SC_TOKENS * PEER_HEADS
    worker = lax.axis_index("s") * SC_CORES + lax.axis_index("c")
    base = worker * tokens_per_worker

    def gather(s, b):
        ids = idx_v[s // PEER_HEADS, pl.ds((s % PEER_HEADS) * PEER_TOPK, PEER_TOPK)]
        return pltpu.make_async_copy(tab_hbm.at[ids], rows_v.at[b], sems.at[b])

    @pl.loop(0, tokens_per_worker // PEER_SC_TOKENS)
    def _(blk):
        tok0 = base + blk * PEER_SC_TOKENS
        pltpu.sync_copy(idx_hbm.at[pl.ds(tok0, PEER_SC_TOKENS)], idx_v)
        pltpu.sync_copy(x_hbm.at[pl.ds(tok0, PEER_SC_TOKENS)], x_v)
        for b in range(PEER_SC_RING - 1):
            gather(b, b).start()
        if zero_out:
            @pl.loop(0, PEER_SC_TOKENS)
            def _(t):
                @plsc.parallel_loop(0, out_v.shape[1] // SC_LANES, unroll=PEER_SC_UNROLL)
                def _(c):
                    out_v[t, pl.ds(pl.multiple_of(c * SC_LANES, SC_LANES), SC_LANES)] = (
                        jnp.zeros((SC_LANES,), jnp.float32))

        @pl.loop(0, n_steps, step=PEER_SC_RING)
        def _(s0):
            for b in range(PEER_SC_RING):
                s = s0 + b
                gather(s, b).wait()

                @pl.when(s + PEER_SC_RING - 1 < n_steps)
                def _():
                    gather(s + PEER_SC_RING - 1, (b + PEER_SC_RING - 1) % PEER_SC_RING).start()

                compute(s // PEER_HEADS, s % PEER_HEADS, b, x_v, rows_v, out_v)

        pltpu.sync_copy(out_v, out_hbm.at[pl.ds(tok0, PEER_SC_TOKENS)])


def _peer_dots_compute(t, hd, b, h_v, rows_v, dots_v):
    lane = lax.iota(jnp.int32, SC_LANES)

    def col_step(c, accs):
        off = pl.multiple_of(c * SC_LANES, SC_LANES)
        hv = h_v[t, pl.ds(off, SC_LANES)]
        return tuple(accs[r] + rows_v[b, r, pl.ds(off, SC_LANES)] * hv for r in range(PEER_TOPK))

    accs = plsc.parallel_loop(
        0, D_MODEL // SC_LANES, unroll=PEER_SC_UNROLL,
        carry=tuple(jnp.zeros((SC_LANES,), jnp.float32) for _ in range(PEER_TOPK)))(col_step)
    res = jnp.zeros((SC_LANES,), jnp.float32)
    for r in range(PEER_TOPK):
        res = jnp.where(lane == r, jnp.sum(accs[r]), res)
    dots_v[t, pl.ds(hd * PEER_TOPK, PEER_TOPK)] = res


def _peer_mix_compute(t, hd, b, act_v, rows_v, out_v):
    tvec = jnp.full((SC_LANES,), t, jnp.int32)
    weights = [plsc.load_gather(act_v, [tvec, jnp.full((SC_LANES,), hd * PEER_TOPK + r, jnp.int32)])
               for r in range(PEER_TOPK)]

    @plsc.parallel_loop(0, D_MODEL // SC_LANES, unroll=PEER_SC_UNROLL)
    def _(c):
        off = pl.multiple_of(c * SC_LANES, SC_LANES)
        a = rows_v[b, 0, pl.ds(off, SC_LANES)] * weights[0]
        for r in range(1, PEER_TOPK):
            a = a + rows_v[b, r, pl.ds(off, SC_LANES)] * weights[r]
        plsc.addupdate(out_v.at[t, pl.ds(off, SC_LANES)], a)


def _peer_expert_dots(h, experts, u_tab):
    T = h.shape[0]
    assert T % (SC_CORES * SC_SUBCORES * PEER_SC_TOKENS) == 0
    assert (PEER_SC_TOKENS * PEER_HEADS) % PEER_SC_RING == 0 and PEER_TOPK == SC_LANES
    body = partial(_peer_sc_body, _peer_dots_compute, False)
    return _peer_sc_call(body, T, PEER_SLOTS, D_MODEL)(h, experts, u_tab)


def _peer_expert_mix(act, experts, v_tab):
    T = act.shape[0]
    assert T % (SC_CORES * SC_SUBCORES * PEER_SC_TOKENS) == 0
    body = partial(_peer_sc_body, _peer_mix_compute, True)
    return _peer_sc_call(body, T, D_MODEL, PEER_SLOTS)(act, experts, v_tab)


PEER_TM = 256


def _extract_topk(s, k, payload=None):
    R = s.shape[0]
    riota = lax.broadcasted_iota(jnp.int32, s.shape, 0)
    vals, rows = [], []
    for _ in range(k):
        m = jnp.max(s, axis=0, keepdims=True)
        pos = jnp.min(jnp.where(s == m, riota, R), axis=0, keepdims=True)
        hit = riota == pos
        vals.append(m)
        rows.append(pos if payload is None else jnp.max(jnp.where(hit, payload, -1), axis=0, keepdims=True))
        s = jnp.where(hit, -jnp.inf, s)
    return jnp.concatenate(vals, axis=0), jnp.concatenate(rows, axis=0)


def _peer_route_kernel(x_ref, sc_ref, sh_ref, wq_ref, keys_ref, h_ref, ex_ref, gate_ref, q_ref, ext_ref, gt_ref):
    K = PEER_TOPK
    f32 = jnp.float32
    h = x_ref[0] * (1.0 + sc_ref[0]) + sh_ref[0]
    h_ref[0] = h
    q = jnp.dot(h.astype(wq_ref.dtype), wq_ref[...], preferred_element_type=f32)
    for hd in range(PEER_HEADS):
        q_ref[hd] = q[:, hd * PEER_KEY_DIM:(hd + 1) * PEER_KEY_DIM].astype(q_ref.dtype)

    def head_body(hd, carry):
        qh = q_ref[hd]
        s1 = lax.dot_general(keys_ref[hd, 0], qh[:, :PEER_HALF], _NT, preferred_element_type=f32)
        s2 = lax.dot_general(keys_ref[hd, 1], qh[:, PEER_HALF:], _NT, preferred_element_type=f32)
        v1, i1 = _extract_topk(s1, K)
        v2, i2 = _extract_topk(s2, K)
        tm = v1.shape[1]
        cand_rows, cidx_rows = [], []
        for a in range(K):
            nb = K // (a + 1)
            cand_rows.append(v1[a:a + 1, :] + v2[:nb, :])
            cidx_rows.append(i1[a:a + 1, :] * PEER_N_KEYS + i2[:nb, :])
        n_pad = -sum(r.shape[0] for r in cand_rows) % 8
        cand_rows.append(jnp.full((n_pad, tm), -jnp.inf, f32))
        cidx_rows.append(jnp.full((n_pad, tm), -1, jnp.int32))
        cand = jnp.concatenate(cand_rows, axis=0)
        cidx = jnp.concatenate(cidx_rows, axis=0)
        top_s, experts = _extract_topk(cand, K, payload=cidx)
        e = jnp.exp(top_s - top_s[0:1, :])
        gt_ref[pl.ds(pl.multiple_of(hd * K, K), K), :] = e / jnp.sum(e, axis=0, keepdims=True)
        ext_ref[pl.ds(pl.multiple_of(hd * K, K), K), :] = experts
        return carry

    lax.fori_loop(0, PEER_HEADS, head_body, 0)
    ex_ref[0] = ext_ref[...].T
    gate_ref[0] = gt_ref[...].T


def _peer_route(x, sc, sh, w_pq, sub_keys):
    B, S, D = x.shape
    tm = PEER_TM
    assert S % tm == 0
    return pl.pallas_call(
        _peer_route_kernel,
        grid=(B, S // tm),
        in_specs=[
            pl.BlockSpec((1, tm, D), lambda b, i: (b, i, 0)),
            pl.BlockSpec((1, 1, D), lambda b, i: (b, 0, 0)),
            pl.BlockSpec((1, 1, D), lambda b, i: (b, 0, 0)),
            pl.BlockSpec((D, PEER_HEADS * PEER_KEY_DIM), lambda b, i: (0, 0)),
            pl.BlockSpec((PEER_HEADS, 2, PEER_N_KEYS, PEER_HALF), lambda b, i: (0, 0, 0, 0)),
        ],
        out_specs=[
            pl.BlockSpec((1, tm, D), lambda b, i: (b, i, 0)),
            pl.BlockSpec((1, tm, PEER_SLOTS), lambda b, i: (b, i, 0)),
            pl.BlockSpec((1, tm, PEER_SLOTS), lambda b, i: (b, i, 0)),
        ],
        out_shape=[
            jax.ShapeDtypeStruct((B, S, D), jnp.float32),
            jax.ShapeDtypeStruct((B, S, PEER_SLOTS), jnp.int32),
            jax.ShapeDtypeStruct((B, S, PEER_SLOTS), jnp.float32),
        ],
        scratch_shapes=[
            pltpu.VMEM((PEER_HEADS, tm, PEER_KEY_DIM), MM_DTYPE),
            pltpu.VMEM((PEER_SLOTS, tm), jnp.int32),
            pltpu.VMEM((PEER_SLOTS, tm), jnp.float32),
        ],
        compiler_params=pltpu.CompilerParams(dimension_semantics=("parallel", "parallel")),
        name="peer_route",
    )(x, sc[:, None, :], sh[:, None, :], w_pq.astype(MM_DTYPE), sub_keys.astype(MM_DTYPE))


def _peer_pre(h2, experts, u_tab):
    B, S, D = h2.shape
    return _peer_expert_dots(h2.reshape(B * S, D), experts.reshape(B * S, PEER_SLOTS), u_tab)


def _peer_act_kernel(pre_ref, gate_ref, o_ref):
    pre = pre_ref[...]
    o_ref[...] = 0.5 * pre * (1.0 + lax.erf(pre * (2.0 ** -0.5))) * gate_ref[...]


def _peer_act(pre, gates):
    T, W = pre.shape
    tm = min(8 * ROW_TILE, T)
    assert T % tm == 0
    spec = pl.BlockSpec((tm, W), lambda i: (i, 0))
    return pl.pallas_call(
        _peer_act_kernel,
        grid=(T // tm,),
        in_specs=[spec, spec],
        out_specs=spec,
        out_shape=jax.ShapeDtypeStruct((T, W), jnp.float32),
        name="peer_act",
    )(pre, gates.reshape(T, W))


def _peer_out(act, experts, v_tab):
    B, S, _ = experts.shape
    return _peer_expert_mix(act, experts.reshape(B * S, PEER_SLOTS), v_tab).reshape(B, S, D_MODEL)


def kernel(x, c, w_ada, b_ada, w_in, rw_mu, rw_w0, rw_w2, rw_a0, rw_a2, rw_g2, rw_k_k, rw_k_a, rw_r_k, rw_gn_g, rw_gn_b, dsa_kv_g, idx_k_g, idx_k_b, rel_bias, w_br_a, w_br_b, w_out, ln1_g, ln1_b, peer_wq, peer_keys, peer_u, peer_v, ln2_g, ln2_b):
    l = 0
    mod = jax.nn.silu(c) @ w_ada[l] + b_ada[l]

    w_rw, w_q, w_kv, w_qi, w_ki, w_wi, w_ga, w_gb = jnp.split(w_in[l], _split_points(IN_SIZES), axis=-1)
    small_pad = jnp.zeros((D_MODEL, SMALL_COLS - SMALL_WI - IDX_HEADS), w_in.dtype)
    w_small = jnp.concatenate([w_qi, w_kv, w_ki, w_wi, small_pad], axis=-1)
    w_gates = jnp.concatenate([w_ga, w_gb], axis=-1)

    B, S, _ = x.shape
    seg_len = S // SEQ_SEGMENTS
    dsa_topk = min(TOPK_MAX, S // 4)
    bias_tiles = _dsa_bias_tiles(rel_bias, DSA_TQ)

    def project_and_dsa(x, sc1, sh1, seg, keys_so_far):
        z_rw = _mod_matmul(x, sc1, sh1, w_rw, jnp.float32, tn=896)
        z_q = _mod_matmul(x, sc1, sh1, w_q, MM_DTYPE, tn=1024)
        z_small = _mod_matmul(x, sc1, sh1, w_small, jnp.float32, tn=512)
        z_g = _mod_matmul(x, sc1, sh1, w_gates, jnp.float32, tn=1024)
        z_wi_t = jnp.swapaxes(z_small[..., SMALL_WI:SMALL_WI + IDX_HEADS], 1, 2)
        qi, ckv, kidx = _dsa_prep(z_small, dsa_kv_g[l], idx_k_g[l], idx_k_b[l])
        if keys_so_far is not None:
            ckv = jnp.concatenate([keys_so_far[0], ckv], axis=1)
            kidx = jnp.concatenate([keys_so_far[1], kidx], axis=1)
        y_b = _dsa_attention(z_q, qi, z_wi_t, ckv, kidx, bias_tiles, dsa_topk,
                             seg * (seg_len // DSA_TQ), MM_DTYPE)
        return (y_b, z_rw, z_g), (ckv, kidx)

    def rwkv_and_merge(x, gt1, carry, y_b, z_rw, z_g):
        y_a, carry = _rwkv7_time_mix(z_rw, carry, rw_mu[l], rw_w0[l], rw_w2[l], rw_a0[l], rw_a2[l], rw_g2[l],
                                     rw_k_k[l], rw_k_a[l], rw_r_k[l], rw_gn_g[l], rw_gn_b[l], MM_DTYPE)
        return _merge_ln(x, gt1, y_a, y_b, z_g, w_br_a[l], w_br_b[l], w_out[l], ln1_g[l], ln1_b[l]), carry

    def finish(st, act):
        y2 = _peer_out(act, st["experts"], peer_v[l])
        return _residual_ln(st["x1"], st["gt2"], y2, ln2_g[l], ln2_b[l])

    assert B % BATCH_GROUPS == 0 and S % (SEQ_SEGMENTS * DSA_TQ) == 0
    gsz = B // BATCH_GROUPS
    outs = []
    prev = None
    for g in range(BATCH_GROUPS):
        sh1, sc1, gt1, sh2, sc2, gt2 = jnp.split(mod[g * gsz:(g + 1) * gsz], 6, axis=-1)
        rw_carry = _rwkv7_init_carry(gsz)
        keys_so_far = None
        for seg in range(SEQ_SEGMENTS):
            x_g = x[g * gsz:(g + 1) * gsz, seg * seg_len:(seg + 1) * seg_len]
            if prev is not None:
                x_g, prev["experts"] = lax.optimization_barrier((x_g, prev["experts"]))
            proj, keys_so_far = project_and_dsa(x_g, sc1, sh1, seg, keys_so_far)
            if prev is not None:
                proj, prev["pre"], outs = lax.optimization_barrier((proj, prev["pre"], outs))
                outs.append(finish(prev, _peer_act(prev["pre"], prev["gates"])))
            x1, rw_carry = rwkv_and_merge(x_g, gt1, rw_carry, *proj)
            h2, experts, gates = _peer_route(x1, sc2, sh2, peer_wq[l], peer_keys[l])
            prev = dict(x1=x1, gt2=gt2, experts=experts, gates=gates, pre=_peer_pre(h2, experts, peer_u[l]))
    prev["pre"], outs = lax.optimization_barrier((prev["pre"], outs))
    outs.append(finish(prev, _peer_act(prev["pre"], prev["gates"])))
    rows = [jnp.concatenate(outs[g * SEQ_SEGMENTS:(g + 1) * SEQ_SEGMENTS], axis=1) for g in range(BATCH_GROUPS)]
    return jnp.concatenate(rows, axis=0)
```

```python
import math
from functools import partial

import jax
import jax.numpy as jnp
import numpy as np
from jax import lax
from jax.experimental import pallas as pl
from jax.experimental.pallas import tpu as pltpu
from jax.experimental.pallas import tpu_sc as plsc

D_MODEL = 1024
RW_HEADS = 8
RW_HEAD_DIM = 64
RW_DIM = 512
RW_DECAY_LORA = 64
RW_A_LORA = 64
RW_GATE_LORA = 128
RW_COLS = 3 * RW_DIM + RW_DECAY_LORA + RW_A_LORA + RW_GATE_LORA
RW_GN_EPS = 64e-5
DSA_HEADS = 8
DSA_LATENT = 128
DSA_Q_DIM = DSA_HEADS * DSA_LATENT
IDX_HEADS = 4
IDX_DIM = 64
TOPK_MAX = 256
REL_BUCKETS = 32
REL_MAX_DIST = 128
IN_SIZES = (RW_COLS, DSA_Q_DIM, DSA_LATENT, IDX_HEADS * IDX_DIM, IDX_DIM, IDX_HEADS, D_MODEL, D_MODEL)
IN_COLS = sum(IN_SIZES)
PEER_HEADS = 8
PEER_N_KEYS = 128
PEER_KEY_DIM = 128
PEER_HALF = 64
PEER_TOPK = 16
LN_EPS = 1e-5
DEPTH = 1
DEEPNORM_ALPHA = (2.0 * DEPTH) ** 0.25

LANES = 128
SMALL_QI = 0
SMALL_KV = SMALL_QI + IDX_HEADS * IDX_DIM
SMALL_KI = SMALL_KV + DSA_LATENT
SMALL_WI = SMALL_KI + IDX_DIM
SMALL_COLS = 4 * LANES
MM_DTYPE = jnp.bfloat16
BATCH_GROUPS = 8
SEQ_SEGMENTS = 2


def _split_points(sizes):
    return np.cumsum(sizes)[:-1].tolist()


def _mod_matmul_kernel(x_ref, sc_ref, sh_ref, w_ref, o_ref):
    h = x_ref[0] * (1.0 + sc_ref[0]) + sh_ref[0]
    o_ref[0] = jnp.dot(h.astype(w_ref.dtype), w_ref[...],
                       preferred_element_type=jnp.float32).astype(o_ref.dtype)


def _mod_matmul(x, sc, sh, w, out_dtype, tn, tm=512):
    B, S, D = x.shape
    N = w.shape[1]
    assert S % tm == 0 and N % tn == 0
    return pl.pallas_call(
        _mod_matmul_kernel,
        grid=(B, S // tm, N // tn),
        in_specs=[
            pl.BlockSpec((1, tm, D), lambda b, i, j: (b, i, 0)),
            pl.BlockSpec((1, 1, D), lambda b, i, j: (b, 0, 0)),
            pl.BlockSpec((1, 1, D), lambda b, i, j: (b, 0, 0)),
            pl.BlockSpec((D, tn), lambda b, i, j: (0, j)),
        ],
        out_specs=pl.BlockSpec((1, tm, tn), lambda b, i, j: (b, i, j)),
        out_shape=jax.ShapeDtypeStruct((B, S, N), out_dtype),
        name="mod_matmul",
    )(x, sc[:, None, :], sh[:, None, :], w.astype(MM_DTYPE))


ROW_TILE = 256


def _layer_norm(x, g, b):
    mu = jnp.mean(x, -1, keepdims=True)
    var = jnp.mean(jnp.square(x - mu), -1, keepdims=True)
    return (x - mu) * lax.rsqrt(var + LN_EPS) * g + b


def _merge_ln_kernel(x_ref, gt_ref, ya_ref, yb_ref, zg_ref, wa_ref, wb_ref, wo_ref, g_ref, b_ref, o_ref):
    f32 = jnp.float32
    D = x_ref.shape[2]
    ya = jnp.dot(ya_ref[0], wa_ref[...], preferred_element_type=f32)
    yb = jnp.dot(yb_ref[0], wb_ref[...], preferred_element_type=f32)
    zg = zg_ref[0]
    merged = jax.nn.sigmoid(zg[:, :D]) * ya + jax.nn.sigmoid(zg[:, D:]) * yb
    mix = jnp.dot(merged.astype(wo_ref.dtype), wo_ref[...], preferred_element_type=f32)
    o_ref[0] = _layer_norm(DEEPNORM_ALPHA * x_ref[0] + gt_ref[0] * mix, g_ref[...], b_ref[...])


def _merge_ln(x, gt, y_a, y_b, z_g, w_a, w_b, w_o, ln_g, ln_b):
    B, S, D = x.shape
    tm = ROW_TILE
    assert S % tm == 0
    tok = lambda width: pl.BlockSpec((1, tm, width), lambda b, i: (b, i, 0))
    full = lambda a: pl.BlockSpec(a.shape, lambda b, i: (0,) * a.ndim)
    w_a, w_b, w_o = (w.astype(MM_DTYPE) for w in (w_a, w_b, w_o))
    ln_g, ln_b = ln_g[None], ln_b[None]
    return pl.pallas_call(
        _merge_ln_kernel,
        grid=(B, S // tm),
        in_specs=[tok(D), pl.BlockSpec((1, 1, D), lambda b, i: (b, 0, 0)), tok(y_a.shape[2]), tok(y_b.shape[2]),
                  tok(z_g.shape[2]), full(w_a), full(w_b), full(w_o), full(ln_g), full(ln_b)],
        out_specs=tok(D),
        out_shape=jax.ShapeDtypeStruct((B, S, D), jnp.float32),
        compiler_params=pltpu.CompilerParams(dimension_semantics=("parallel", "parallel")),
        name="merge_ln",
    )(x, gt[:, None, :], y_a, y_b, z_g, w_a, w_b, w_o, ln_g, ln_b)


def _residual_ln_kernel(x_ref, gt_ref, y_ref, g_ref, b_ref, o_ref):
    o_ref[0] = _layer_norm(DEEPNORM_ALPHA * x_ref[0] + gt_ref[0] * y_ref[0], g_ref[...], b_ref[...])


def _residual_ln(x, gt, y, ln_g, ln_b):
    B, S, D = x.shape
    tm = ROW_TILE
    assert S % tm == 0
    tok = pl.BlockSpec((1, tm, D), lambda b, i: (b, i, 0))
    vec = pl.BlockSpec((1, D), lambda b, i: (0, 0))
    return pl.pallas_call(
        _residual_ln_kernel,
        grid=(B, S // tm),
        in_specs=[tok, pl.BlockSpec((1, 1, D), lambda b, i: (b, 0, 0)), tok, vec, vec],
        out_specs=tok,
        out_shape=jax.ShapeDtypeStruct((B, S, D), jnp.float32),
        compiler_params=pltpu.CompilerParams(dimension_semantics=("parallel", "parallel")),
        name="residual_ln",
    )(x, gt[:, None, :], y, ln_g[None], ln_b[None])


def _t5_bucket(n):
    n = jnp.maximum(n, 0)
    max_exact = REL_BUCKETS // 2
    nf = jnp.maximum(n, 1).astype(jnp.float32)
    large = max_exact + (jnp.log(nf / max_exact) / math.log(REL_MAX_DIST / max_exact)
                         * (REL_BUCKETS - max_exact)).astype(jnp.int32)
    large = jnp.minimum(large, REL_BUCKETS - 1)
    return jnp.where(n < max_exact, n, large)


RW_CHUNK = 64
RW_INV_BLOCK = 16
_NN = (((1,), (0,)), ((), ()))
_NT = (((1,), (1,)), ((), ()))
_BNN = (((2,), (1,)), ((0,), (0,)))
_BNT = (((2,), (2,)), ((0,), (0,)))


def _dot_f32(a, b, dims=_NN):
    return lax.dot_general(a, b, dims, precision=lax.Precision.HIGHEST,
                           preferred_element_type=jnp.float32)


def _dot_bf16x3(a, b, dims=_NN):
    f32, bf = jnp.float32, jnp.bfloat16
    a_hi, b_hi = a.astype(bf), b.astype(bf)
    a_lo = (a - a_hi.astype(f32)).astype(bf)
    b_lo = (b - b_hi.astype(f32)).astype(bf)
    out = lax.dot_general(a_hi, b_hi, dims, preferred_element_type=f32)
    out = out + lax.dot_general(a_hi, b_lo, dims, preferred_element_type=f32)
    return out + lax.dot_general(a_lo, b_hi, dims, preferred_element_type=f32)


def _bf16_terms(a):
    f32, bf = jnp.float32, jnp.bfloat16
    hi = a.astype(bf)
    r1 = a - hi.astype(f32)
    mid = r1.astype(bf)
    lo = (r1 - mid.astype(f32)).astype(bf)
    return hi, mid, lo


def _dot_lhs_split(a, b01):
    b = b01.astype(jnp.bfloat16)
    return sum(jnp.dot(t, b, preferred_element_type=jnp.float32) for t in _bf16_terms(a))


def _dot_rhs_split(a01, b):
    a = a01.astype(jnp.bfloat16)
    return sum(jnp.dot(a, t, preferred_element_type=jnp.float32) for t in _bf16_terms(b))


def _rwkv_kernel(z_ref, m0_ref, p0_ref, mu_ref, w0_ref, w2_ref, a0_ref, a2_ref, g2_ref, kk_ref, ka_ref, rk_ref,
                 gng_ref, gnb_ref, bd_ref, o_ref, mo_ref, po_ref, m_ref, prev_ref, y_ref):
    C = z_ref.shape[1]
    N = RW_HEAD_DIM
    f32 = jnp.float32
    dot3 = _dot_bf16x3

    @pl.when(pl.program_id(1) == 0)
    def _():
        m_ref[...] = m0_ref[0]
        prev_ref[...] = p0_ref[0]

    z = z_ref[0]
    row = lax.broadcasted_iota(jnp.int32, z.shape, 0)
    shifted = jnp.where(row == 0, prev_ref[...], pltpu.roll(z, 1, axis=0))
    prev_ref[...] = z[C - 1:C, :]
    zs = z + (shifted - z) * mu_ref[...]
    r = zs[:, 0:RW_DIM]
    k = zs[:, RW_DIM:2 * RW_DIM]
    v = zs[:, 2 * RW_DIM:3 * RW_DIM]
    o1 = 3 * RW_DIM
    wl = zs[:, o1:o1 + RW_DECAY_LORA]
    al = zs[:, o1 + RW_DECAY_LORA:o1 + RW_DECAY_LORA + RW_A_LORA]
    gl = zs[:, o1 + RW_DECAY_LORA + RW_A_LORA:]

    bd = bd_ref[...]
    log_w = -jax.nn.softplus(-(w0_ref[...] + dot3(jnp.tanh(wl), w2_ref[...]))) - 0.5
    ldec = -jnp.exp(log_w)
    a_lr = jax.nn.sigmoid(a0_ref[...] + dot3(al, a2_ref[...]))
    g = dot3(jax.nn.sigmoid(gl), g2_ref[...])
    kk = k * kk_ref[...]
    kk = kk * lax.rsqrt(jnp.maximum(_dot_lhs_split(kk * kk, bd), 1e-24))
    k2 = k * (1.0 + (a_lr - 1.0) * ka_ref[...])
    a_vec = -kk
    b_vec = kk * a_lr

    ti = lax.broadcasted_iota(jnp.int32, (C, C), 0)
    tj = lax.broadcasted_iota(jnp.int32, (C, C), 1)
    cum = _dot_rhs_split((ti >= tj).astype(f32), ldec)
    cum_last = cum[C - 1:C, :]
    w_incl = jnp.exp(cum)
    w_excl = jnp.exp(cum - ldec)
    w_inv = jnp.exp(-cum)
    w_end = jnp.exp(cum_last - cum)
    w_all = jnp.exp(cum_last)
    a_t = a_vec * w_excl
    r_t = r * w_incl
    b_t = b_vec * w_inv
    k_t = k2 * w_inv
    b_e = b_vec * w_end
    k_e = k2 * w_end

    strict = ti > tj
    incl = ti >= tj
    bi, bj = ti // RW_INV_BLOCK, tj // RW_INV_BLOCK
    same_blk = bi == bj
    pair_blk = jnp.logical_and((bi // 2) == (bj // 2), jnp.logical_not(same_blk))
    half_blk = (bi // 2) != (bj // 2)
    eye = (ti == tj).astype(f32)

    H = RW_HEADS
    heads = lambda x: jnp.stack([x[:, h * N:(h + 1) * N] for h in range(H)], axis=0)
    bmm = lambda a, b: dot3(a, b, _BNN)
    A, Rt, Bt, Kt, Be, Ke, V = (heads(t) for t in (a_t, r_t, b_t, k_t, b_e, k_e, v))
    gm = dot3(jnp.concatenate([A, Rt], axis=1), jnp.concatenate([Bt, Kt], axis=1), _BNT)
    a_ab = jnp.where(strict, gm[:, :C, :C], 0.0)
    a_ak = jnp.where(strict, gm[:, :C, C:], 0.0)
    a_rb = jnp.where(incl, gm[:, C:, :C], 0.0)
    a_rk = jnp.where(incl, gm[:, C:, C:], 0.0)
    d1 = jnp.where(same_blk, a_ab, 0.0)
    xinv = eye + d1
    d2 = bmm(d1, d1)
    xinv = xinv + bmm(xinv, d2)
    d4 = bmm(d2, d2)
    xinv = xinv + bmm(xinv, d4)
    d8 = bmm(d4, d4)
    xinv = xinv + bmm(xinv, d8)
    xinv = xinv + bmm(bmm(xinv, jnp.where(pair_blk, a_ab, 0.0)), xinv)
    xinv = xinv + bmm(bmm(xinv, jnp.where(half_blk, a_ab, 0.0)), xinv)

    av = bmm(jnp.concatenate([a_ak, a_rk], axis=1), V)
    p = bmm(xinv, jnp.concatenate([A, av[:, :C]], axis=2))
    qm = bmm(a_rb, p)
    q1 = Rt + qm[:, :, :N]
    q2 = qm[:, :, N:] + av[:, C:]
    gmat = bmm(jnp.swapaxes(Be, 1, 2), p)
    g1 = eye * heads(w_all) + gmat[:, :, :N]
    g2 = gmat[:, :, N:] + bmm(jnp.swapaxes(Ke, 1, 2), V)
    m = m_ref[...]
    yh = _dot_f32(q1, m, _BNN) + q2
    m_ref[...] = _dot_f32(g1, m, _BNN) + g2
    for h in range(H):
        y_ref[:, h * N:(h + 1) * N] = yh[h]

    y = y_ref[...]
    mean = _dot_lhs_split(y, bd) * (1.0 / N)
    yc = y - mean
    var = _dot_lhs_split(yc * yc, bd) * (1.0 / N)
    yn = yc * lax.rsqrt(var + RW_GN_EPS) * gng_ref[...] + gnb_ref[...]
    bonus = _dot_lhs_split(r * k2 * rk_ref[...], bd) * v
    o_ref[0] = ((yn + bonus) * g).astype(o_ref.dtype)

    @pl.when(pl.program_id(1) == pl.num_programs(1) - 1)
    def _():
        mo_ref[0] = m_ref[...]
        po_ref[0] = prev_ref[...]


def _rwkv7_init_carry(batch):
    return (jnp.zeros((batch, RW_HEADS, RW_HEAD_DIM, RW_HEAD_DIM), jnp.float32),
            jnp.zeros((batch, 1, RW_COLS), jnp.float32))


def _rwkv7_time_mix(z_rw, carry, mu, w0, w2, a0, a2, g2, k_k, k_a, r_k, gn_g, gn_b, out_dtype):
    B, S, _ = z_rw.shape
    C = RW_CHUNK
    assert S % C == 0 and C == RW_HEAD_DIM and C % (4 * RW_INV_BLOCK) == 0
    hid = jnp.arange(RW_DIM) // RW_HEAD_DIM
    bd = (hid[:, None] == hid[None, :]).astype(jnp.float32)
    row = lambda a: a.reshape(1, -1)
    full = lambda shape: pl.BlockSpec(shape, lambda b, c: (0,) * len(shape))
    state_spec = pl.BlockSpec((1, RW_HEADS, RW_HEAD_DIM, RW_HEAD_DIM), lambda b, c: (b, 0, 0, 0))
    prev_spec = pl.BlockSpec((1, 1, RW_COLS), lambda b, c: (b, 0, 0))
    y, m_out, p_out = pl.pallas_call(
        _rwkv_kernel,
        grid=(B, S // C),
        in_specs=[
            pl.BlockSpec((1, C, RW_COLS), lambda b, c: (b, c, 0)), state_spec, prev_spec,
            full((1, RW_COLS)), full((1, RW_DIM)), full((RW_DECAY_LORA, RW_DIM)), full((1, RW_DIM)),
            full((RW_A_LORA, RW_DIM)), full((RW_GATE_LORA, RW_DIM)), full((1, RW_DIM)), full((1, RW_DIM)),
            full((1, RW_DIM)), full((1, RW_DIM)), full((1, RW_DIM)), full((RW_DIM, RW_DIM)),
        ],
        out_specs=[pl.BlockSpec((1, C, RW_DIM), lambda b, c: (b, c, 0)), state_spec, prev_spec],
        out_shape=[jax.ShapeDtypeStruct((B, S, RW_DIM), out_dtype),
                   jax.ShapeDtypeStruct(carry[0].shape, jnp.float32),
                   jax.ShapeDtypeStruct(carry[1].shape, jnp.float32)],
        scratch_shapes=[
            pltpu.VMEM((RW_HEADS, RW_HEAD_DIM, RW_HEAD_DIM), jnp.float32),
            pltpu.VMEM((1, RW_COLS), jnp.float32),
            pltpu.VMEM((C, RW_DIM), jnp.float32),
        ],
        compiler_params=pltpu.CompilerParams(dimension_semantics=("parallel", "arbitrary")),
        name="rwkv7_time_mix",
    )(z_rw, carry[0], carry[1], row(mu), row(w0), w2, row(a0), a2, g2, row(k_k), row(k_a), row(r_k),
      row(gn_g), row(gn_b), bd)
    return y, (m_out, p_out)


DSA_TQ = 256
MASK_NEG = -1e30
INT_MIN = -2 ** 31
KEY_NEG_INF = -2139095041
THRESH_BITS = 32


def _dsa_prep_kernel(z_ref, kvg_ref, kig_ref, kib_ref, qi_ref, kv_ref, ki_ref):
    z = z_ref[0]
    qi_ref[0] = z[:, SMALL_QI:SMALL_KV].astype(qi_ref.dtype)
    kv = z[:, SMALL_KV:SMALL_KI]
    ms = jnp.mean(jnp.square(kv), -1, keepdims=True)
    kv_ref[0] = (kv * lax.rsqrt(ms + LN_EPS) * kvg_ref[...]).astype(kv_ref.dtype)
    ki = z[:, SMALL_KI:SMALL_WI]
    mu = jnp.mean(ki, -1, keepdims=True)
    var = jnp.mean(jnp.square(ki - mu), -1, keepdims=True)
    ki_ref[0] = ((ki - mu) * lax.rsqrt(var + LN_EPS) * kig_ref[...] + kib_ref[...]).astype(ki_ref.dtype)


def _dsa_prep(z_small, kv_g, ki_g, ki_b, tm=512):
    B, S, W = z_small.shape
    assert S % tm == 0
    return pl.pallas_call(
        _dsa_prep_kernel,
        grid=(B, S // tm),
        in_specs=[
            pl.BlockSpec((1, tm, W), lambda b, i: (b, i, 0)),
            pl.BlockSpec((1, DSA_LATENT), lambda b, i: (0, 0)),
            pl.BlockSpec((1, IDX_DIM), lambda b, i: (0, 0)),
            pl.BlockSpec((1, IDX_DIM), lambda b, i: (0, 0)),
        ],
        out_specs=[
            pl.BlockSpec((1, tm, IDX_HEADS * IDX_DIM), lambda b, i: (b, i, 0)),
            pl.BlockSpec((1, tm, DSA_LATENT), lambda b, i: (b, i, 0)),
            pl.BlockSpec((1, tm, IDX_DIM), lambda b, i: (b, i, 0)),
        ],
        out_shape=[
            jax.ShapeDtypeStruct((B, S, IDX_HEADS * IDX_DIM), MM_DTYPE),
            jax.ShapeDtypeStruct((B, S, DSA_LATENT), MM_DTYPE),
            jax.ShapeDtypeStruct((B, S, IDX_DIM), MM_DTYPE),
        ],
        name="dsa_prep",
    )(z_small, kv_g[None], ki_g[None], ki_b[None])


def _sortable_key(s):
    s = jnp.where(s == 0.0, 0.0, s)
    bits = pltpu.bitcast(s, jnp.int32)
    return bits ^ ((bits >> 31) & 0x7FFFFFFF)


def _col_count(mask_i32):
    tk, tq = mask_i32.shape
    return jnp.sum(mask_i32.reshape(tk // 8, 8, tq), axis=0)


def _dsa_kernel(bfar_ref, q_ref, qi_ref, wit_ref, kv_ref, kvt_ref, ki_ref, b0_ref, b1_ref, o_ref,
                key_ref, madd_ref, m_ref, l_ref, acc_ref, *, topk, q_off):
    tq = q_ref.shape[1]
    tk = tq
    i = pl.program_id(1) + q_off
    nj = i + 1
    f32 = jnp.float32
    krow = lax.broadcasted_iota(jnp.int32, (tk, tq), 0)
    qcol = lax.broadcasted_iota(jnp.int32, (tk, tq), 1)

    qi = qi_ref[0]
    wit = wit_ref[0] * (IDX_HEADS ** -0.5)

    def score_chunk(j, carry):
        off = pl.multiple_of(j * tk, tk)
        kc = ki_ref[0, pl.ds(off, tk), :]
        s = jnp.zeros((tk, tq), f32)
        for h in range(IDX_HEADS):
            d = lax.dot_general(kc, qi[:, h * IDX_DIM:(h + 1) * IDX_DIM],
                                (((1,), (1,)), ((), ())), preferred_element_type=f32)
            s = s + wit[h:h + 1, :] * jnp.maximum(d * (IDX_DIM ** -0.5), 0.0)
        causal = (krow + j * tk) <= (qcol + i * tq)
        s = jnp.where(causal, s, -jnp.inf)
        key_ref[j] = _sortable_key(s)
        return carry

    lax.fori_loop(0, nj, score_chunk, 0)

    def count_where(pred_fn):
        def body(j, acc):
            return acc + _col_count(pred_fn(key_ref[j], j).astype(jnp.int32))
        acc = lax.fori_loop(0, nj, body, jnp.zeros((8, tq), jnp.int32))
        return jnp.sum(acc, axis=0, keepdims=True)

    def bit_step(it, t_u):
        bit = THRESH_BITS - 1 - it
        cand_u = t_u | jnp.left_shift(jnp.int32(1), bit)
        cand = cand_u ^ INT_MIN
        cnt = count_where(lambda k, j: k >= cand)
        return jnp.where(cnt >= topk, cand_u, t_u)

    t_u = lax.fori_loop(0, THRESH_BITS, bit_step, jnp.zeros((1, tq), jnp.int32))
    thr = t_u ^ INT_MIN
    cnt_gt = count_where(lambda k, j: k > thr)
    cnt_ge = count_where(lambda k, j: k >= thr)
    is_neg = thr == KEY_NEG_INF
    need = jnp.logical_and(cnt_ge > topk, jnp.logical_not(is_neg))
    n_tie_take = topk - cnt_gt

    thr_open = jnp.where(is_neg, thr, thr - 1)
    any_need = jnp.max(need.astype(jnp.int32)) > 0

    @pl.when(jnp.logical_not(any_need))
    def _():
        def body(j, carry):
            madd_ref[j] = jnp.where(key_ref[j] > thr_open, 0.0, MASK_NEG)
            return carry
        lax.fori_loop(0, nj, body, 0)

    @pl.when(any_need)
    def _():
        s_len = tk * key_ref.shape[0]
        n_bits = max(1, int(math.ceil(math.log2(s_len))))

        def idx_step(it, p):
            bit = n_bits - 1 - it
            cand = p | jnp.left_shift(jnp.int32(1), bit)
            cnt = count_where(
                lambda k, j: jnp.where(k == thr, jnp.where((krow + j * tk) < cand, 1, 0), 0))
            return jnp.where(cnt < n_tie_take, cand, p)

        p_idx = lax.fori_loop(0, n_bits, idx_step, jnp.zeros((1, tq), jnp.int32))
        p_idx = jnp.where(need, p_idx, jnp.where(is_neg, -1, s_len))

        def body(j, carry):
            k = key_ref[j]
            tie_ok = jnp.where((krow + j * tk) <= p_idx, 0.0, MASK_NEG)
            madd_ref[j] = jnp.where(k > thr, 0.0, jnp.where(k == thr, tie_ok, MASK_NEG))
            return carry
        lax.fori_loop(0, nj, body, 0)

    H, d = DSA_HEADS, DSA_LATENT
    m_ref[...] = jnp.full(m_ref.shape, MASK_NEG, f32)
    l_ref[...] = jnp.zeros(l_ref.shape, f32)
    acc_ref[...] = jnp.zeros(acc_ref.shape, f32)
    scale = DSA_LATENT ** -0.5
    q_all = jnp.concatenate([q_ref[0, :, h * d:(h + 1) * d] for h in range(H)], axis=0)

    def attend(j, bias):
        off = pl.multiple_of(j * tk, tk)
        kc = kv_ref[0, pl.ds(off, tk), :]
        kct = kvt_ref[0, :, pl.ds(off, tk)]
        lg = lax.dot_general(kc, q_all, (((1,), (1,)), ((), ())), preferred_element_type=f32)
        lg = lg * scale + bias + jnp.tile(madd_ref[j], (1, H))
        m_old = m_ref[...]
        m_new = jnp.maximum(m_old, jnp.max(lg, axis=0, keepdims=True))
        alpha = jnp.exp(m_old - m_new)
        p = jnp.exp(lg - m_new)
        l_ref[...] = alpha * l_ref[...] + jnp.sum(p, axis=0, keepdims=True)
        acc_ref[...] = alpha * acc_ref[...] + jnp.dot(kct, p.astype(kct.dtype), preferred_element_type=f32)
        m_ref[...] = m_new

    def far_body(j, carry):
        attend(j, bfar_ref[...])
        return carry

    lax.fori_loop(0, jnp.maximum(i - 1, 0), far_body, 0)

    @pl.when(i >= 1)
    def _():
        attend(i - 1, b1_ref[...])

    attend(i, b0_ref[...])

    out_t = acc_ref[...] / l_ref[...]
    for h in range(H):
        o_ref[0, :, h * d:(h + 1) * d] = out_t[:, h * tq:(h + 1) * tq].T.astype(o_ref.dtype)


def _dsa_bias_tiles(rel_bias, tq):
    dist = jnp.arange(2 * tq, dtype=jnp.int32)
    by_dist = rel_bias[_t5_bucket(dist)].T

    def toeplitz(v):
        H, P = v.shape
        skew = jnp.tile(v, (1, tq + 1))[:, :tq * (P + 1)].reshape(H, tq, P + 1)
        return skew[:, ::-1, :tq]

    b0 = toeplitz(jnp.concatenate([jnp.repeat(by_dist[:, :1], tq - 1, axis=1), by_dist[:, :tq + 1]], axis=1))
    b1 = toeplitz(jnp.concatenate([by_dist[:, 1:], by_dist[:, -1:]], axis=1))
    H = by_dist.shape[0]
    b0 = jnp.transpose(b0, (1, 0, 2)).reshape(tq, H * tq)
    b1 = jnp.transpose(b1, (1, 0, 2)).reshape(tq, H * tq)
    return b0, b1, jnp.repeat(by_dist[:, -1], tq)[None, :]


def _dsa_attention(q, qi, z_wi_t, ckv, kidx, bias_tiles, topk, q_off, out_dtype):
    B, Sq, _ = q.shape
    tq = DSA_TQ
    nq = Sq // tq
    S = ckv.shape[1]
    assert Sq % tq == 0 and S == (q_off + nq) * tq and tq >= REL_MAX_DIST and topk <= tq
    ckv_t = jnp.swapaxes(ckv, 1, 2)
    b0, b1, bfar = bias_tiles
    nk = S // tq
    H, d = DSA_HEADS, DSA_LATENT
    f32_bytes = 4
    vmem_estimate = (
        2 * nk * tq * tq * f32_bytes
        + 2 * 2 * tq * H * tq * f32_bytes
        + 2 * (2 * S * d + S * LANES) * 2
        + 6 * tq * H * tq * f32_bytes
        + 4 * tq * H * d * f32_bytes
    )
    grid_spec = pltpu.PrefetchScalarGridSpec(
        num_scalar_prefetch=0,
        grid=(B, nq),
        in_specs=[
            pl.BlockSpec((1, H * tq), lambda b, i: (0, 0)),
            pl.BlockSpec((1, tq, H * d), lambda b, i: (b, i, 0)),
            pl.BlockSpec((1, tq, IDX_HEADS * IDX_DIM), lambda b, i: (b, i, 0)),
            pl.BlockSpec((1, IDX_HEADS, tq), lambda b, i: (b, 0, i)),
            pl.BlockSpec((1, S, d), lambda b, i: (b, 0, 0)),
            pl.BlockSpec((1, d, S), lambda b, i: (b, 0, 0)),
            pl.BlockSpec((1, S, IDX_DIM), lambda b, i: (b, 0, 0)),
            pl.BlockSpec((tq, H * tq), lambda b, i: (0, 0)),
            pl.BlockSpec((tq, H * tq), lambda b, i: (0, 0)),
        ],
        out_specs=pl.BlockSpec((1, tq, H * d), lambda b, i: (b, i, 0)),
        scratch_shapes=[
            pltpu.VMEM((nk, tq, tq), jnp.int32),
            pltpu.VMEM((nk, tq, tq), jnp.float32),
            pltpu.VMEM((1, H * tq), jnp.float32),
            pltpu.VMEM((1, H * tq), jnp.float32),
            pltpu.VMEM((d, H * tq), jnp.float32),
        ],
    )
    return pl.pallas_call(
        partial(_dsa_kernel, topk=topk, q_off=q_off),
        grid_spec=grid_spec,
        out_shape=jax.ShapeDtypeStruct((B, Sq, H * d), out_dtype),
        compiler_params=pltpu.CompilerParams(vmem_limit_bytes=vmem_estimate),
        name="dsa_attention",
    )(bfar, q, qi, z_wi_t, ckv, ckv_t, kidx, b0, b1)


SC_CORES = 2
SC_SUBCORES = 16
SC_LANES = 16
PEER_SC_TOKENS = 8
PEER_SC_RING = 4
PEER_SC_UNROLL = 4
PEER_SLOTS = PEER_HEADS * PEER_TOPK


def _peer_sc_call(body, T, out_width, stage_width):
    mesh = plsc.VectorSubcoreMesh(core_axis_name="c", subcore_axis_name="s")
    return pl.kernel(
        body, mesh=mesh,
        out_type=jax.ShapeDtypeStruct((T, out_width), jnp.float32),
        scratch_types=[
            pltpu.VMEM((2, PEER_SC_TOKENS, PEER_SLOTS), jnp.int32),
            pltpu.VMEM((2, PEER_SC_TOKENS, stage_width), jnp.float32),
            pltpu.VMEM((PEER_SC_RING, PEER_TOPK, D_MODEL), jnp.float32),
            pltpu.VMEM((PEER_SC_TOKENS, out_width), jnp.float32),
            pltpu.SemaphoreType.DMA((PEER_SC_RING + 2,)),
        ],
        compiler_params=pltpu.CompilerParams(needs_layout_passes=False),
    )


def _peer_sc_body(compute, zero_out, x_hbm, idx_hbm, tab_hbm, out_hbm, idx_v, x_v, rows_v, out_v, sems):
    T = idx_hbm.shape[0]
    tokens_per_worker = T // (SC_CORES * SC_SUBCORES)
    n_blocks = tokens_per_worker // PEER_SC_TOKENS
    n_steps = PEER_SC_TOKENS * PEER_HEADS
    ring = PEER_SC_RING
    worker = lax.axis_index("s") * SC_CORES + lax.axis_index("c")
    base = worker * tokens_per_worker

    def stage(blk, slot):
        tok0 = base + blk * PEER_SC_TOKENS
        return (pltpu.make_async_copy(idx_hbm.at[pl.ds(tok0, PEER_SC_TOKENS)], idx_v.at[slot], sems.at[ring]),
                pltpu.make_async_copy(x_hbm.at[pl.ds(tok0, PEER_SC_TOKENS)], x_v.at[slot], sems.at[ring + 1]))

    def gather(slot, s, b):
        ids = idx_v[slot, s // PEER_HEADS, pl.ds((s % PEER_HEADS) * PEER_TOPK, PEER_TOPK)]
        return pltpu.make_async_copy(tab_hbm.at[ids], rows_v.at[b], sems.at[b])

    for copy in stage(0, 0):
        copy.start()
    for copy in stage(0, 0):
        copy.wait()
    for b in range(ring - 1):
        gather(0, b, b).start()

    @pl.loop(0, n_blocks)
    def _(blk):
        slot = blk % 2
        has_next = blk + 1 < n_blocks

        @pl.when(has_next)
        def _():
            for copy in stage(blk + 1, 1 - slot):
                copy.start()

        if zero_out:
            @pl.loop(0, PEER_SC_TOKENS)
            def _(t):
                @plsc.parallel_loop(0, out_v.shape[1] // SC_LANES, unroll=PEER_SC_UNROLL)
                def _(c):
                    out_v[t, pl.ds(pl.multiple_of(c * SC_LANES, SC_LANES), SC_LANES)] = (
                        jnp.zeros((SC_LANES,), jnp.float32))

        @pl.loop(0, n_steps, step=ring)
        def _(s0):
            for b in range(ring):
                s = s0 + b
                gather(slot, s, b).wait()
                ahead = s + ring - 1
                ahead_buf = (b + ring - 1) % ring

                @pl.when(ahead < n_steps)
                def _():
                    gather(slot, ahead, ahead_buf).start()

                if b >= 1:
                    @pl.when(jnp.logical_and(ahead >= n_steps, has_next))
                    def _():
                        if b == 1:
                            for copy in stage(blk + 1, 1 - slot):
                                copy.wait()
                        gather(1 - slot, ahead - n_steps, ahead_buf).start()

                compute(slot, s // PEER_HEADS, s % PEER_HEADS, b, x_v, rows_v, out_v)

        pltpu.sync_copy(out_v, out_hbm.at[pl.ds(base + blk * PEER_SC_TOKENS, PEER_SC_TOKENS)])


def _peer_dots_compute(slot, t, hd, b, h_v, rows_v, dots_v):
    lane = lax.iota(jnp.int32, SC_LANES)

    def col_step(c, accs):
        off = pl.multiple_of(c * SC_LANES, SC_LANES)
        hv = h_v[slot, t, pl.ds(off, SC_LANES)]
        return tuple(accs[r] + rows_v[b, r, pl.ds(off, SC_LANES)] * hv for r in range(PEER_TOPK))

    accs = plsc.parallel_loop(
        0, D_MODEL // SC_LANES, unroll=PEER_SC_UNROLL,
        carry=tuple(jnp.zeros((SC_LANES,), jnp.float32) for _ in range(PEER_TOPK)))(col_step)
    res = jnp.zeros((SC_LANES,), jnp.float32)
    for r in range(PEER_TOPK):
        res = jnp.where(lane == r, jnp.sum(accs[r]), res)
    dots_v[t, pl.ds(hd * PEER_TOPK, PEER_TOPK)] = res


def _peer_mix_compute(slot, t, hd, b, act_v, rows_v, out_v):
    svec = jnp.full((SC_LANES,), slot, jnp.int32)
    tvec = jnp.full((SC_LANES,), t, jnp.int32)
    weights = [plsc.load_gather(act_v, [svec, tvec, jnp.full((SC_LANES,), hd * PEER_TOPK + r, jnp.int32)])
               for r in range(PEER_TOPK)]

    @plsc.parallel_loop(0, D_MODEL // SC_LANES, unroll=PEER_SC_UNROLL)
    def _(c):
        off = pl.multiple_of(c * SC_LANES, SC_LANES)
        a = rows_v[b, 0, pl.ds(off, SC_LANES)] * weights[0]
        for r in range(1, PEER_TOPK):
            a = a + rows_v[b, r, pl.ds(off, SC_LANES)] * weights[r]
        plsc.addupdate(out_v.at[t, pl.ds(off, SC_LANES)], a)


def _peer_expert_dots(h, experts, u_tab):
    T = h.shape[0]
    assert T % (SC_CORES * SC_SUBCORES * PEER_SC_TOKENS) == 0
    assert (PEER_SC_TOKENS * PEER_HEADS) % PEER_SC_RING == 0 and PEER_TOPK == SC_LANES
    body = partial(_peer_sc_body, _peer_dots_compute, False)
    return _peer_sc_call(body, T, PEER_SLOTS, D_MODEL)(h, experts, u_tab)


def _peer_expert_mix(act, experts, v_tab):
    T = act.shape[0]
    assert T % (SC_CORES * SC_SUBCORES * PEER_SC_TOKENS) == 0
    body = partial(_peer_sc_body, _peer_mix_compute, True)
    return _peer_sc_call(body, T, D_MODEL, PEER_SLOTS)(act, experts, v_tab)


PEER_TM = 256


def _extract_topk(s, k, payload=None):
    R = s.shape[0]
    riota = lax.broadcasted_iota(jnp.int32, s.shape, 0)
    vals, rows = [], []
    for _ in range(k):
        m = jnp.max(s, axis=0, keepdims=True)
        pos = jnp.min(jnp.where(s == m, riota, R), axis=0, keepdims=True)
        hit = riota == pos
        vals.append(m)
        rows.append(pos if payload is None else jnp.max(jnp.where(hit, payload, -1), axis=0, keepdims=True))
        s = jnp.where(hit, -jnp.inf, s)
    return jnp.concatenate(vals, axis=0), jnp.concatenate(rows, axis=0)


def _peer_route_kernel(x_ref, sc_ref, sh_ref, wq_ref, keys_ref, h_ref, ex_ref, gate_ref, q_ref, ext_ref, gt_ref):
    K = PEER_TOPK
    f32 = jnp.float32
    h = x_ref[0] * (1.0 + sc_ref[0]) + sh_ref[0]
    h_ref[0] = h
    q = jnp.dot(h.astype(wq_ref.dtype), wq_ref[...], preferred_element_type=f32)
    for hd in range(PEER_HEADS):
        q_ref[hd] = q[:, hd * PEER_KEY_DIM:(hd + 1) * PEER_KEY_DIM].astype(q_ref.dtype)

    def head_body(hd, carry):
        qh = q_ref[hd]
        s1 = lax.dot_general(keys_ref[hd, 0], qh[:, :PEER_HALF], _NT, preferred_element_type=f32)
        s2 = lax.dot_general(keys_ref[hd, 1], qh[:, PEER_HALF:], _NT, preferred_element_type=f32)
        v1, i1 = _extract_topk(s1, K)
        v2, i2 = _extract_topk(s2, K)
        tm = v1.shape[1]
        cand_rows, cidx_rows = [], []
        for a in range(K):
            nb = K // (a + 1)
            cand_rows.append(v1[a:a + 1, :] + v2[:nb, :])
            cidx_rows.append(i1[a:a + 1, :] * PEER_N_KEYS + i2[:nb, :])
        n_pad = -sum(r.shape[0] for r in cand_rows) % 8
        cand_rows.append(jnp.full((n_pad, tm), -jnp.inf, f32))
        cidx_rows.append(jnp.full((n_pad, tm), -1, jnp.int32))
        cand = jnp.concatenate(cand_rows, axis=0)
        cidx = jnp.concatenate(cidx_rows, axis=0)
        top_s, experts = _extract_topk(cand, K, payload=cidx)
        e = jnp.exp(top_s - top_s[0:1, :])
        gt_ref[pl.ds(pl.multiple_of(hd * K, K), K), :] = e / jnp.sum(e, axis=0, keepdims=True)
        ext_ref[pl.ds(pl.multiple_of(hd * K, K), K), :] = experts
        return carry

    lax.fori_loop(0, PEER_HEADS, head_body, 0)
    ex_ref[0] = ext_ref[...].T
    gate_ref[0] = gt_ref[...].T


def _peer_route(x, sc, sh, w_pq, sub_keys):
    B, S, D = x.shape
    tm = PEER_TM
    assert S % tm == 0
    return pl.pallas_call(
        _peer_route_kernel,
        grid=(B, S // tm),
        in_specs=[
            pl.BlockSpec((1, tm, D), lambda b, i: (b, i, 0)),
            pl.BlockSpec((1, 1, D), lambda b, i: (b, 0, 0)),
            pl.BlockSpec((1, 1, D), lambda b, i: (b, 0, 0)),
            pl.BlockSpec((D, PEER_HEADS * PEER_KEY_DIM), lambda b, i: (0, 0)),
            pl.BlockSpec((PEER_HEADS, 2, PEER_N_KEYS, PEER_HALF), lambda b, i: (0, 0, 0, 0)),
        ],
        out_specs=[
            pl.BlockSpec((1, tm, D), lambda b, i: (b, i, 0)),
            pl.BlockSpec((1, tm, PEER_SLOTS), lambda b, i: (b, i, 0)),
            pl.BlockSpec((1, tm, PEER_SLOTS), lambda b, i: (b, i, 0)),
        ],
        out_shape=[
            jax.ShapeDtypeStruct((B, S, D), jnp.float32),
            jax.ShapeDtypeStruct((B, S, PEER_SLOTS), jnp.int32),
            jax.ShapeDtypeStruct((B, S, PEER_SLOTS), jnp.float32),
        ],
        scratch_shapes=[
            pltpu.VMEM((PEER_HEADS, tm, PEER_KEY_DIM), MM_DTYPE),
            pltpu.VMEM((PEER_SLOTS, tm), jnp.int32),
            pltpu.VMEM((PEER_SLOTS, tm), jnp.float32),
        ],
        compiler_params=pltpu.CompilerParams(dimension_semantics=("parallel", "parallel")),
        name="peer_route",
    )(x, sc[:, None, :], sh[:, None, :], w_pq.astype(MM_DTYPE), sub_keys.astype(MM_DTYPE))


def _peer_pre(h2, experts, u_tab):
    B, S, D = h2.shape
    return _peer_expert_dots(h2.reshape(B * S, D), experts.reshape(B * S, PEER_SLOTS), u_tab)


def _peer_act_kernel(pre_ref, gate_ref, o_ref):
    pre = pre_ref[...]
    o_ref[...] = 0.5 * pre * (1.0 + lax.erf(pre * (2.0 ** -0.5))) * gate_ref[...]


def _peer_act(pre, gates):
    T, W = pre.shape
    tm = min(8 * ROW_TILE, T)
    assert T % tm == 0
    spec = pl.BlockSpec((tm, W), lambda i: (i, 0))
    return pl.pallas_call(
        _peer_act_kernel,
        grid=(T // tm,),
        in_specs=[spec, spec],
        out_specs=spec,
        out_shape=jax.ShapeDtypeStruct((T, W), jnp.float32),
        name="peer_act",
    )(pre, gates.reshape(T, W))


def _peer_out(act, experts, v_tab):
    B, S, _ = experts.shape
    return _peer_expert_mix(act, experts.reshape(B * S, PEER_SLOTS), v_tab).reshape(B, S, D_MODEL)


def kernel(x, c, w_ada, b_ada, w_in, rw_mu, rw_w0, rw_w2, rw_a0, rw_a2, rw_g2, rw_k_k, rw_k_a, rw_r_k, rw_gn_g, rw_gn_b, dsa_kv_g, idx_k_g, idx_k_b, rel_bias, w_br_a, w_br_b, w_out, ln1_g, ln1_b, peer_wq, peer_keys, peer_u, peer_v, ln2_g, ln2_b):
    l = 0
    mod = jax.nn.silu(c) @ w_ada[l] + b_ada[l]

    w_rw, w_q, w_kv, w_qi, w_ki, w_wi, w_ga, w_gb = jnp.split(w_in[l], _split_points(IN_SIZES), axis=-1)
    small_pad = jnp.zeros((D_MODEL, SMALL_COLS - SMALL_WI - IDX_HEADS), w_in.dtype)
    w_small = jnp.concatenate([w_qi, w_kv, w_ki, w_wi, small_pad], axis=-1)
    w_gates = jnp.concatenate([w_ga, w_gb], axis=-1)

    B, S, _ = x.shape
    seg_len = S // SEQ_SEGMENTS
    dsa_topk = min(TOPK_MAX, S // 4)
    bias_tiles = _dsa_bias_tiles(rel_bias, DSA_TQ)

    def project_and_dsa(x, sc1, sh1, seg, keys_so_far):
        z_rw = _mod_matmul(x, sc1, sh1, w_rw, jnp.float32, tn=896)
        z_q = _mod_matmul(x, sc1, sh1, w_q, MM_DTYPE, tn=1024)
        z_small = _mod_matmul(x, sc1, sh1, w_small, jnp.float32, tn=512)
        z_g = _mod_matmul(x, sc1, sh1, w_gates, jnp.float32, tn=1024)
        z_wi_t = jnp.swapaxes(z_small[..., SMALL_WI:SMALL_WI + IDX_HEADS], 1, 2)
        qi, ckv, kidx = _dsa_prep(z_small, dsa_kv_g[l], idx_k_g[l], idx_k_b[l])
        if keys_so_far is not None:
            ckv = jnp.concatenate([keys_so_far[0], ckv], axis=1)
            kidx = jnp.concatenate([keys_so_far[1], kidx], axis=1)
        y_b = _dsa_attention(z_q, qi, z_wi_t, ckv, kidx, bias_tiles, dsa_topk,
                             seg * (seg_len // DSA_TQ), MM_DTYPE)
        return (y_b, z_rw, z_g), (ckv, kidx)

    def rwkv_and_merge(x, gt1, carry, y_b, z_rw, z_g):
        y_a, carry = _rwkv7_time_mix(z_rw, carry, rw_mu[l], rw_w0[l], rw_w2[l], rw_a0[l], rw_a2[l], rw_g2[l],
                                     rw_k_k[l], rw_k_a[l], rw_r_k[l], rw_gn_g[l], rw_gn_b[l], MM_DTYPE)
        return _merge_ln(x, gt1, y_a, y_b, z_g, w_br_a[l], w_br_b[l], w_out[l], ln1_g[l], ln1_b[l]), carry

    def finish(st, act):
        y2 = _peer_out(act, st["experts"], peer_v[l])
        return _residual_ln(st["x1"], st["gt2"], y2, ln2_g[l], ln2_b[l])

    assert B % BATCH_GROUPS == 0 and S % (SEQ_SEGMENTS * DSA_TQ) == 0
    gsz = B // BATCH_GROUPS
    outs = []
    prev = None
    for g in range(BATCH_GROUPS):
        sh1, sc1, gt1, sh2, sc2, gt2 = jnp.split(mod[g * gsz:(g + 1) * gsz], 6, axis=-1)
        rw_carry = _rwkv7_init_carry(gsz)
        keys_so_far = None
        for seg in range(SEQ_SEGMENTS):
            x_g = x[g * gsz:(g + 1) * gsz, seg * seg_len:(seg + 1) * seg_len]
            if prev is not None:
                x_g, prev["experts"] = lax.optimization_barrier((x_g, prev["experts"]))
            proj, keys_so_far = project_and_dsa(x_g, sc1, sh1, seg, keys_so_far)
            if prev is not None:
                proj, prev["pre"], outs = lax.optimization_barrier((proj, prev["pre"], outs))
                outs.append(finish(prev, _peer_act(prev["pre"], prev["gates"])))
            x1, rw_carry = rwkv_and_merge(x_g, gt1, rw_carry, *proj)
            h2, experts, gates = _peer_route(x1, sc2, sh2, peer_wq[l], peer_keys[l])
            prev = dict(x1=x1, gt2=gt2, experts=experts, gates=gates, pre=_peer_pre(h2, experts, peer_u[l]))
    prev["pre"], outs = lax.optimization_barrier((prev["pre"], outs))
    outs.append(finish(prev, _peer_act(prev["pre"], prev["gates"])))
    rows = [jnp.concatenate(outs[g * SEQ_SEGMENTS:(g + 1) * SEQ_SEGMENTS], axis=1) for g in range(BATCH_GROUPS)]
    return jnp.concatenate(rows, axis=0)
```

```python
import math
from functools import partial

import jax
import jax.numpy as jnp
import numpy as np
from jax import lax
from jax.experimental import pallas as pl
from jax.experimental.pallas import tpu as pltpu
from jax.experimental.pallas import tpu_sc as plsc

D_MODEL = 1024
RW_HEADS = 8
RW_HEAD_DIM = 64
RW_DIM = 512
RW_DECAY_LORA = 64
RW_A_LORA = 64
RW_GATE_LORA = 128
RW_COLS = 3 * RW_DIM + RW_DECAY_LORA + RW_A_LORA + RW_GATE_LORA
RW_GN_EPS = 64e-5
DSA_HEADS = 8
DSA_LATENT = 128
DSA_Q_DIM = DSA_HEADS * DSA_LATENT
IDX_HEADS = 4
IDX_DIM = 64
TOPK_MAX = 256
REL_BUCKETS = 32
REL_MAX_DIST = 128
IN_SIZES = (RW_COLS, DSA_Q_DIM, DSA_LATENT, IDX_HEADS * IDX_DIM, IDX_DIM, IDX_HEADS, D_MODEL, D_MODEL)
IN_COLS = sum(IN_SIZES)
PEER_HEADS = 8
PEER_N_KEYS = 128
PEER_KEY_DIM = 128
PEER_HALF = 64
PEER_TOPK = 16
LN_EPS = 1e-5
DEPTH = 1
DEEPNORM_ALPHA = (2.0 * DEPTH) ** 0.25

LANES = 128
SMALL_QI = 0
SMALL_KV = SMALL_QI + IDX_HEADS * IDX_DIM
SMALL_KI = SMALL_KV + DSA_LATENT
SMALL_WI = SMALL_KI + IDX_DIM
SMALL_COLS = 4 * LANES
MM_DTYPE = jnp.bfloat16
BATCH_GROUPS = 8
SEQ_SEGMENTS = 2


def _split_points(sizes):
    return np.cumsum(sizes)[:-1].tolist()


def _mod_matmul_kernel(x_ref, sc_ref, sh_ref, w_ref, o_ref):
    h = x_ref[0] * (1.0 + sc_ref[0]) + sh_ref[0]
    o_ref[0] = jnp.dot(h.astype(w_ref.dtype), w_ref[...],
                       preferred_element_type=jnp.float32).astype(o_ref.dtype)


def _mod_matmul(x, sc, sh, w, out_dtype, tn, tm=512):
    B, S, D = x.shape
    N = w.shape[1]
    assert S % tm == 0 and N % tn == 0
    return pl.pallas_call(
        _mod_matmul_kernel,
        grid=(B, S // tm, N // tn),
        in_specs=[
            pl.BlockSpec((1, tm, D), lambda b, i, j: (b, i, 0)),
            pl.BlockSpec((1, 1, D), lambda b, i, j: (b, 0, 0)),
            pl.BlockSpec((1, 1, D), lambda b, i, j: (b, 0, 0)),
            pl.BlockSpec((D, tn), lambda b, i, j: (0, j)),
        ],
        out_specs=pl.BlockSpec((1, tm, tn), lambda b, i, j: (b, i, j)),
        out_shape=jax.ShapeDtypeStruct((B, S, N), out_dtype),
        name="mod_matmul",
    )(x, sc[:, None, :], sh[:, None, :], w.astype(MM_DTYPE))


ROW_TILE = 256


def _layer_norm(x, g, b):
    mu = jnp.mean(x, -1, keepdims=True)
    var = jnp.mean(jnp.square(x - mu), -1, keepdims=True)
    return (x - mu) * lax.rsqrt(var + LN_EPS) * g + b


def _merge_ln_kernel(x_ref, gt_ref, ya_ref, yb_ref, zg_ref, wa_ref, wb_ref, wo_ref, g_ref, b_ref, o_ref):
    f32 = jnp.float32
    D = x_ref.shape[2]
    ya = jnp.dot(ya_ref[0], wa_ref[...], preferred_element_type=f32)
    yb = jnp.dot(yb_ref[0], wb_ref[...], preferred_element_type=f32)
    zg = zg_ref[0]
    merged = jax.nn.sigmoid(zg[:, :D]) * ya + jax.nn.sigmoid(zg[:, D:]) * yb
    mix = jnp.dot(merged.astype(wo_ref.dtype), wo_ref[...], preferred_element_type=f32)
    o_ref[0] = _layer_norm(DEEPNORM_ALPHA * x_ref[0] + gt_ref[0] * mix, g_ref[...], b_ref[...])


def _merge_ln(x, gt, y_a, y_b, z_g, w_a, w_b, w_o, ln_g, ln_b):
    B, S, D = x.shape
    tm = ROW_TILE
    assert S % tm == 0
    tok = lambda width: pl.BlockSpec((1, tm, width), lambda b, i: (b, i, 0))
    full = lambda a: pl.BlockSpec(a.shape, lambda b, i: (0,) * a.ndim)
    w_a, w_b, w_o = (w.astype(MM_DTYPE) for w in (w_a, w_b, w_o))
    ln_g, ln_b = ln_g[None], ln_b[None]
    return pl.pallas_call(
        _merge_ln_kernel,
        grid=(B, S // tm),
        in_specs=[tok(D), pl.BlockSpec((1, 1, D), lambda b, i: (b, 0, 0)), tok(y_a.shape[2]), tok(y_b.shape[2]),
                  tok(z_g.shape[2]), full(w_a), full(w_b), full(w_o), full(ln_g), full(ln_b)],
        out_specs=tok(D),
        out_shape=jax.ShapeDtypeStruct((B, S, D), jnp.float32),
        compiler_params=pltpu.CompilerParams(dimension_semantics=("parallel", "parallel")),
        name="merge_ln",
    )(x, gt[:, None, :], y_a, y_b, z_g, w_a, w_b, w_o, ln_g, ln_b)


def _residual_ln_kernel(x_ref, gt_ref, y_ref, g_ref, b_ref, o_ref):
    o_ref[0] = _layer_norm(DEEPNORM_ALPHA * x_ref[0] + gt_ref[0] * y_ref[0], g_ref[...], b_ref[...])


def _residual_ln(x, gt, y, ln_g, ln_b):
    B, S, D = x.shape
    tm = ROW_TILE
    assert S % tm == 0
    tok = pl.BlockSpec((1, tm, D), lambda b, i: (b, i, 0))
    vec = pl.BlockSpec((1, D), lambda b, i: (0, 0))
    return pl.pallas_call(
        _residual_ln_kernel,
        grid=(B, S // tm),
        in_specs=[tok, pl.BlockSpec((1, 1, D), lambda b, i: (b, 0, 0)), tok, vec, vec],
        out_specs=tok,
        out_shape=jax.ShapeDtypeStruct((B, S, D), jnp.float32),
        compiler_params=pltpu.CompilerParams(dimension_semantics=("parallel", "parallel")),
        name="residual_ln",
    )(x, gt[:, None, :], y, ln_g[None], ln_b[None])


def _t5_bucket(n):
    n = jnp.maximum(n, 0)
    max_exact = REL_BUCKETS // 2
    nf = jnp.maximum(n, 1).astype(jnp.float32)
    large = max_exact + (jnp.log(nf / max_exact) / math.log(REL_MAX_DIST / max_exact)
                         * (REL_BUCKETS - max_exact)).astype(jnp.int32)
    large = jnp.minimum(large, REL_BUCKETS - 1)
    return jnp.where(n < max_exact, n, large)


RW_CHUNK = 64
RW_INV_BLOCK = 16
_NN = (((1,), (0,)), ((), ()))
_NT = (((1,), (1,)), ((), ()))
_BNN = (((2,), (1,)), ((0,), (0,)))
_BNT = (((2,), (2,)), ((0,), (0,)))


def _dot_f32(a, b, dims=_NN):
    return lax.dot_general(a, b, dims, precision=lax.Precision.HIGHEST,
                           preferred_element_type=jnp.float32)


def _dot_bf16x3(a, b, dims=_NN):
    f32, bf = jnp.float32, jnp.bfloat16
    a_hi, b_hi = a.astype(bf), b.astype(bf)
    a_lo = (a - a_hi.astype(f32)).astype(bf)
    b_lo = (b - b_hi.astype(f32)).astype(bf)
    out = lax.dot_general(a_hi, b_hi, dims, preferred_element_type=f32)
    out = out + lax.dot_general(a_hi, b_lo, dims, preferred_element_type=f32)
    return out + lax.dot_general(a_lo, b_hi, dims, preferred_element_type=f32)


def _bf16_terms(a):
    f32, bf = jnp.float32, jnp.bfloat16
    hi = a.astype(bf)
    r1 = a - hi.astype(f32)
    mid = r1.astype(bf)
    lo = (r1 - mid.astype(f32)).astype(bf)
    return hi, mid, lo


def _dot_lhs_split(a, b01):
    b = b01.astype(jnp.bfloat16)
    return sum(jnp.dot(t, b, preferred_element_type=jnp.float32) for t in _bf16_terms(a))


def _dot_rhs_split(a01, b):
    a = a01.astype(jnp.bfloat16)
    return sum(jnp.dot(a, t, preferred_element_type=jnp.float32) for t in _bf16_terms(b))


def _rwkv_kernel(z_ref, m0_ref, p0_ref, mu_ref, w0_ref, w2_ref, a0_ref, a2_ref, g2_ref, kk_ref, ka_ref, rk_ref,
                 gng_ref, gnb_ref, bd_ref, o_ref, mo_ref, po_ref, m_ref, prev_ref, y_ref):
    C = z_ref.shape[1]
    N = RW_HEAD_DIM
    f32 = jnp.float32
    dot3 = _dot_bf16x3

    @pl.when(pl.program_id(1) == 0)
    def _():
        m_ref[...] = m0_ref[0]
        prev_ref[...] = p0_ref[0]

    z = z_ref[0]
    row = lax.broadcasted_iota(jnp.int32, z.shape, 0)
    shifted = jnp.where(row == 0, prev_ref[...], pltpu.roll(z, 1, axis=0))
    prev_ref[...] = z[C - 1:C, :]
    zs = z + (shifted - z) * mu_ref[...]
    r = zs[:, 0:RW_DIM]
    k = zs[:, RW_DIM:2 * RW_DIM]
    v = zs[:, 2 * RW_DIM:3 * RW_DIM]
    o1 = 3 * RW_DIM
    wl = zs[:, o1:o1 + RW_DECAY_LORA]
    al = zs[:, o1 + RW_DECAY_LORA:o1 + RW_DECAY_LORA + RW_A_LORA]
    gl = zs[:, o1 + RW_DECAY_LORA + RW_A_LORA:]

    bd = bd_ref[...]
    log_w = -jax.nn.softplus(-(w0_ref[...] + dot3(jnp.tanh(wl), w2_ref[...]))) - 0.5
    ldec = -jnp.exp(log_w)
    a_lr = jax.nn.sigmoid(a0_ref[...] + dot3(al, a2_ref[...]))
    g = dot3(jax.nn.sigmoid(gl), g2_ref[...])
    kk = k * kk_ref[...]
    kk = kk * lax.rsqrt(jnp.maximum(_dot_lhs_split(kk * kk, bd), 1e-24))
    k2 = k * (1.0 + (a_lr - 1.0) * ka_ref[...])
    a_vec = -kk
    b_vec = kk * a_lr

    ti = lax.broadcasted_iota(jnp.int32, (C, C), 0)
    tj = lax.broadcasted_iota(jnp.int32, (C, C), 1)
    cum = _dot_rhs_split((ti >= tj).astype(f32), ldec)
    cum_last = cum[C - 1:C, :]
    w_incl = jnp.exp(cum)
    w_excl = jnp.exp(cum - ldec)
    w_inv = jnp.exp(-cum)
    w_end = jnp.exp(cum_last - cum)
    w_all = jnp.exp(cum_last)
    a_t = a_vec * w_excl
    r_t = r * w_incl
    b_t = b_vec * w_inv
    k_t = k2 * w_inv
    b_e = b_vec * w_end
    k_e = k2 * w_end

    strict = ti > tj
    incl = ti >= tj
    bi, bj = ti // RW_INV_BLOCK, tj // RW_INV_BLOCK
    same_blk = bi == bj
    pair_blk = jnp.logical_and((bi // 2) == (bj // 2), jnp.logical_not(same_blk))
    half_blk = (bi // 2) != (bj // 2)
    eye = (ti == tj).astype(f32)

    H = RW_HEADS
    heads = lambda x: jnp.stack([x[:, h * N:(h + 1) * N] for h in range(H)], axis=0)
    bmm = lambda a, b: dot3(a, b, _BNN)
    A, Rt, Bt, Kt, Be, Ke, V = (heads(t) for t in (a_t, r_t, b_t, k_t, b_e, k_e, v))
    gm = dot3(jnp.concatenate([A, Rt], axis=1), jnp.concatenate([Bt, Kt], axis=1), _BNT)
    a_ab = jnp.where(strict, gm[:, :C, :C], 0.0)
    a_ak = jnp.where(strict, gm[:, :C, C:], 0.0)
    a_rb = jnp.where(incl, gm[:, C:, :C], 0.0)
    a_rk = jnp.where(incl, gm[:, C:, C:], 0.0)
    d1 = jnp.where(same_blk, a_ab, 0.0)
    xinv = eye + d1
    d2 = bmm(d1, d1)
    xinv = xinv + bmm(xinv, d2)
    d4 = bmm(d2, d2)
    xinv = xinv + bmm(xinv, d4)
    d8 = bmm(d4, d4)
    xinv = xinv + bmm(xinv, d8)
    xinv = xinv + bmm(bmm(xinv, jnp.where(pair_blk, a_ab, 0.0)), xinv)
    xinv = xinv + bmm(bmm(xinv, jnp.where(half_blk, a_ab, 0.0)), xinv)

    av = bmm(jnp.concatenate([a_ak, a_rk], axis=1), V)
    p = bmm(xinv, jnp.concatenate([A, av[:, :C]], axis=2))
    qm = bmm(a_rb, p)
    q1 = Rt + qm[:, :, :N]
    q2 = qm[:, :, N:] + av[:, C:]
    gmat = bmm(jnp.swapaxes(Be, 1, 2), p)
    g1 = eye * heads(w_all) + gmat[:, :, :N]
    g2 = gmat[:, :, N:] + bmm(jnp.swapaxes(Ke, 1, 2), V)
    m = m_ref[...]
    yh = _dot_f32(q1, m, _BNN) + q2
    m_ref[...] = _dot_f32(g1, m, _BNN) + g2
    for h in range(H):
        y_ref[:, h * N:(h + 1) * N] = yh[h]

    y = y_ref[...]
    mean = _dot_lhs_split(y, bd) * (1.0 / N)
    yc = y - mean
    var = _dot_lhs_split(yc * yc, bd) * (1.0 / N)
    yn = yc * lax.rsqrt(var + RW_GN_EPS) * gng_ref[...] + gnb_ref[...]
    bonus = _dot_lhs_split(r * k2 * rk_ref[...], bd) * v
    o_ref[0] = ((yn + bonus) * g).astype(o_ref.dtype)

    @pl.when(pl.program_id(1) == pl.num_programs(1) - 1)
    def _():
        mo_ref[0] = m_ref[...]
        po_ref[0] = prev_ref[...]


def _rwkv7_init_carry(batch):
    return (jnp.zeros((batch, RW_HEADS, RW_HEAD_DIM, RW_HEAD_DIM), jnp.float32),
            jnp.zeros((batch, 1, RW_COLS), jnp.float32))


def _rwkv7_time_mix(z_rw, carry, mu, w0, w2, a0, a2, g2, k_k, k_a, r_k, gn_g, gn_b, out_dtype):
    B, S, _ = z_rw.shape
    C = RW_CHUNK
    assert S % C == 0 and C == RW_HEAD_DIM and C % (4 * RW_INV_BLOCK) == 0
    hid = jnp.arange(RW_DIM) // RW_HEAD_DIM
    bd = (hid[:, None] == hid[None, :]).astype(jnp.float32)
    row = lambda a: a.reshape(1, -1)
    full = lambda shape: pl.BlockSpec(shape, lambda b, c: (0,) * len(shape))
    state_spec = pl.BlockSpec((1, RW_HEADS, RW_HEAD_DIM, RW_HEAD_DIM), lambda b, c: (b, 0, 0, 0))
    prev_spec = pl.BlockSpec((1, 1, RW_COLS), lambda b, c: (b, 0, 0))
    y, m_out, p_out = pl.pallas_call(
        _rwkv_kernel,
        grid=(B, S // C),
        in_specs=[
            pl.BlockSpec((1, C, RW_COLS), lambda b, c: (b, c, 0)), state_spec, prev_spec,
            full((1, RW_COLS)), full((1, RW_DIM)), full((RW_DECAY_LORA, RW_DIM)), full((1, RW_DIM)),
            full((RW_A_LORA, RW_DIM)), full((RW_GATE_LORA, RW_DIM)), full((1, RW_DIM)), full((1, RW_DIM)),
            full((1, RW_DIM)), full((1, RW_DIM)), full((1, RW_DIM)), full((RW_DIM, RW_DIM)),
        ],
        out_specs=[pl.BlockSpec((1, C, RW_DIM), lambda b, c: (b, c, 0)), state_spec, prev_spec],
        out_shape=[jax.ShapeDtypeStruct((B, S, RW_DIM), out_dtype),
                   jax.ShapeDtypeStruct(carry[0].shape, jnp.float32),
                   jax.ShapeDtypeStruct(carry[1].shape, jnp.float32)],
        scratch_shapes=[
            pltpu.VMEM((RW_HEADS, RW_HEAD_DIM, RW_HEAD_DIM), jnp.float32),
            pltpu.VMEM((1, RW_COLS), jnp.float32),
            pltpu.VMEM((C, RW_DIM), jnp.float32),
        ],
        compiler_params=pltpu.CompilerParams(dimension_semantics=("parallel", "arbitrary")),
        name="rwkv7_time_mix",
    )(z_rw, carry[0], carry[1], row(mu), row(w0), w2, row(a0), a2, g2, row(k_k), row(k_a), row(r_k),
      row(gn_g), row(gn_b), bd)
    return y, (m_out, p_out)


DSA_TQ = 256
MASK_NEG = -1e30
INT_MIN = -2 ** 31
KEY_NEG_INF = -2139095041
THRESH_BITS = 32


def _dsa_prep_kernel(z_ref, kvg_ref, kig_ref, kib_ref, qi_ref, kv_ref, ki_ref):
    z = z_ref[0]
    qi_ref[0] = z[:, SMALL_QI:SMALL_KV].astype(qi_ref.dtype)
    kv = z[:, SMALL_KV:SMALL_KI]
    ms = jnp.mean(jnp.square(kv), -1, keepdims=True)
    kv_ref[0] = (kv * lax.rsqrt(ms + LN_EPS) * kvg_ref[...]).astype(kv_ref.dtype)
    ki = z[:, SMALL_KI:SMALL_WI]
    mu = jnp.mean(ki, -1, keepdims=True)
    var = jnp.mean(jnp.square(ki - mu), -1, keepdims=True)
    ki_ref[0] = ((ki - mu) * lax.rsqrt(var + LN_EPS) * kig_ref[...] + kib_ref[...]).astype(ki_ref.dtype)


def _dsa_prep(z_small, kv_g, ki_g, ki_b, tm=512):
    B, S, W = z_small.shape
    assert S % tm == 0
    return pl.pallas_call(
        _dsa_prep_kernel,
        grid=(B, S // tm),
        in_specs=[
            pl.BlockSpec((1, tm, W), lambda b, i: (b, i, 0)),
            pl.BlockSpec((1, DSA_LATENT), lambda b, i: (0, 0)),
            pl.BlockSpec((1, IDX_DIM), lambda b, i: (0, 0)),
            pl.BlockSpec((1, IDX_DIM), lambda b, i: (0, 0)),
        ],
        out_specs=[
            pl.BlockSpec((1, tm, IDX_HEADS * IDX_DIM), lambda b, i: (b, i, 0)),
            pl.BlockSpec((1, tm, DSA_LATENT), lambda b, i: (b, i, 0)),
            pl.BlockSpec((1, tm, IDX_DIM), lambda b, i: (b, i, 0)),
        ],
        out_shape=[
            jax.ShapeDtypeStruct((B, S, IDX_HEADS * IDX_DIM), MM_DTYPE),
            jax.ShapeDtypeStruct((B, S, DSA_LATENT), MM_DTYPE),
            jax.ShapeDtypeStruct((B, S, IDX_DIM), MM_DTYPE),
        ],
        name="dsa_prep",
    )(z_small, kv_g[None], ki_g[None], ki_b[None])


def _sortable_key(s):
    s = jnp.where(s == 0.0, 0.0, s)
    bits = pltpu.bitcast(s, jnp.int32)
    return bits ^ ((bits >> 31) & 0x7FFFFFFF)


def _col_count(mask_i32):
    tk, tq = mask_i32.shape
    return jnp.sum(mask_i32.reshape(tk // 8, 8, tq), axis=0)


def _dsa_kernel(bfar_ref, q_ref, qi_ref, wit_ref, kv_ref, kvt_ref, ki_ref, b0_ref, b1_ref, o_ref,
                key_ref, madd_ref, m_ref, l_ref, acc_ref, *, topk, q_off):
    tq = q_ref.shape[1]
    tk = tq
    i = pl.program_id(1) + q_off
    nj = i + 1
    f32 = jnp.float32
    krow = lax.broadcasted_iota(jnp.int32, (tk, tq), 0)
    qcol = lax.broadcasted_iota(jnp.int32, (tk, tq), 1)

    qi = qi_ref[0]
    wit = wit_ref[0] * (IDX_HEADS ** -0.5)

    def score_chunk(j, carry):
        off = pl.multiple_of(j * tk, tk)
        kc = ki_ref[0, pl.ds(off, tk), :]
        s = jnp.zeros((tk, tq), f32)
        for h in range(IDX_HEADS):
            d = lax.dot_general(kc, qi[:, h * IDX_DIM:(h + 1) * IDX_DIM],
                                (((1,), (1,)), ((), ())), preferred_element_type=f32)
            s = s + wit[h:h + 1, :] * jnp.maximum(d * (IDX_DIM ** -0.5), 0.0)
        causal = (krow + j * tk) <= (qcol + i * tq)
        s = jnp.where(causal, s, -jnp.inf)
        key_ref[j] = _sortable_key(s)
        return carry

    lax.fori_loop(0, nj, score_chunk, 0)

    def count_where(pred_fn):
        def body(j, acc):
            return acc + _col_count(pred_fn(key_ref[j], j).astype(jnp.int32))
        acc = lax.fori_loop(0, nj, body, jnp.zeros((8, tq), jnp.int32))
        return jnp.sum(acc, axis=0, keepdims=True)

    def bit_step(it, t_u):
        bit = THRESH_BITS - 1 - it
        cand_u = t_u | jnp.left_shift(jnp.int32(1), bit)
        cand = cand_u ^ INT_MIN
        cnt = count_where(lambda k, j: k >= cand)
        return jnp.where(cnt >= topk, cand_u, t_u)

    t_u = lax.fori_loop(0, THRESH_BITS, bit_step, jnp.zeros((1, tq), jnp.int32))
    thr = t_u ^ INT_MIN
    cnt_gt = count_where(lambda k, j: k > thr)
    cnt_ge = count_where(lambda k, j: k >= thr)
    is_neg = thr == KEY_NEG_INF
    need = jnp.logical_and(cnt_ge > topk, jnp.logical_not(is_neg))
    n_tie_take = topk - cnt_gt

    thr_open = jnp.where(is_neg, thr, thr - 1)
    any_need = jnp.max(need.astype(jnp.int32)) > 0

    @pl.when(jnp.logical_not(any_need))
    def _():
        def body(j, carry):
            madd_ref[j] = jnp.where(key_ref[j] > thr_open, 0.0, MASK_NEG)
            return carry
        lax.fori_loop(0, nj, body, 0)

    @pl.when(any_need)
    def _():
        s_len = tk * key_ref.shape[0]
        n_bits = max(1, int(math.ceil(math.log2(s_len))))

        def idx_step(it, p):
            bit = n_bits - 1 - it
            cand = p | jnp.left_shift(jnp.int32(1), bit)
            cnt = count_where(
                lambda k, j: jnp.where(k == thr, jnp.where((krow + j * tk) < cand, 1, 0), 0))
            return jnp.where(cnt < n_tie_take, cand, p)

        p_idx = lax.fori_loop(0, n_bits, idx_step, jnp.zeros((1, tq), jnp.int32))
        p_idx = jnp.where(need, p_idx, jnp.where(is_neg, -1, s_len))

        def body(j, carry):
            k = key_ref[j]
            tie_ok = jnp.where((krow + j * tk) <= p_idx, 0.0, MASK_NEG)
            madd_ref[j] = jnp.where(k > thr, 0.0, jnp.where(k == thr, tie_ok, MASK_NEG))
            return carry
        lax.fori_loop(0, nj, body, 0)

    H, d = DSA_HEADS, DSA_LATENT
    m_ref[...] = jnp.full(m_ref.shape, MASK_NEG, f32)
    l_ref[...] = jnp.zeros(l_ref.shape, f32)
    acc_ref[...] = jnp.zeros(acc_ref.shape, f32)
    scale = DSA_LATENT ** -0.5
    q_all = jnp.concatenate([q_ref[0, :, h * d:(h + 1) * d] for h in range(H)], axis=0)

    def attend(j, bias):
        off = pl.multiple_of(j * tk, tk)
        kc = kv_ref[0, pl.ds(off, tk), :]
        kct = kvt_ref[0, :, pl.ds(off, tk)]
        lg = lax.dot_general(kc, q_all, (((1,), (1,)), ((), ())), preferred_element_type=f32)
        lg = lg * scale + bias + jnp.tile(madd_ref[j], (1, H))
        m_old = m_ref[...]
        m_new = jnp.maximum(m_old, jnp.max(lg, axis=0, keepdims=True))
        alpha = jnp.exp(m_old - m_new)
        p = jnp.exp(lg - m_new)
        l_ref[...] = alpha * l_ref[...] + jnp.sum(p, axis=0, keepdims=True)
        acc_ref[...] = alpha * acc_ref[...] + jnp.dot(kct, p.astype(kct.dtype), preferred_element_type=f32)
        m_ref[...] = m_new

    def far_body(j, carry):
        attend(j, bfar_ref[...])
        return carry

    lax.fori_loop(0, jnp.maximum(i - 1, 0), far_body, 0)

    @pl.when(i >= 1)
    def _():
        attend(i - 1, b1_ref[...])

    attend(i, b0_ref[...])

    out_t = acc_ref[...] / l_ref[...]
    for h in range(H):
        o_ref[0, :, h * d:(h + 1) * d] = out_t[:, h * tq:(h + 1) * tq].T.astype(o_ref.dtype)


def _dsa_bias_tiles(rel_bias, tq):
    dist = jnp.arange(2 * tq, dtype=jnp.int32)
    by_dist = rel_bias[_t5_bucket(dist)].T

    def toeplitz(v):
        H, P = v.shape
        skew = jnp.tile(v, (1, tq + 1))[:, :tq * (P + 1)].reshape(H, tq, P + 1)
        return skew[:, ::-1, :tq]

    b0 = toeplitz(jnp.concatenate([jnp.repeat(by_dist[:, :1], tq - 1, axis=1), by_dist[:, :tq + 1]], axis=1))
    b1 = toeplitz(jnp.concatenate([by_dist[:, 1:], by_dist[:, -1:]], axis=1))
    H = by_dist.shape[0]
    b0 = jnp.transpose(b0, (1, 0, 2)).reshape(tq, H * tq)
    b1 = jnp.transpose(b1, (1, 0, 2)).reshape(tq, H * tq)
    return b0, b1, jnp.repeat(by_dist[:, -1], tq)[None, :]


def _dsa_attention(q, qi, z_wi_t, ckv, kidx, bias_tiles, topk, q_off, out_dtype):
    B, Sq, _ = q.shape
    tq = DSA_TQ
    nq = Sq // tq
    S = ckv.shape[1]
    assert Sq % tq == 0 and S == (q_off + nq) * tq and tq >= REL_MAX_DIST and topk <= tq
    ckv_t = jnp.swapaxes(ckv, 1, 2)
    b0, b1, bfar = bias_tiles
    nk = S // tq
    H, d = DSA_HEADS, DSA_LATENT
    f32_bytes = 4
    vmem_estimate = (
        2 * nk * tq * tq * f32_bytes
        + 2 * 2 * tq * H * tq * f32_bytes
        + 2 * (2 * S * d + S * LANES) * 2
        + 6 * tq * H * tq * f32_bytes
        + 4 * tq * H * d * f32_bytes
    )
    grid_spec = pltpu.PrefetchScalarGridSpec(
        num_scalar_prefetch=0,
        grid=(B, nq),
        in_specs=[
            pl.BlockSpec((1, H * tq), lambda b, i: (0, 0)),
            pl.BlockSpec((1, tq, H * d), lambda b, i: (b, i, 0)),
            pl.BlockSpec((1, tq, IDX_HEADS * IDX_DIM), lambda b, i: (b, i, 0)),
            pl.BlockSpec((1, IDX_HEADS, tq), lambda b, i: (b, 0, i)),
            pl.BlockSpec((1, S, d), lambda b, i: (b, 0, 0)),
            pl.BlockSpec((1, d, S), lambda b, i: (b, 0, 0)),
            pl.BlockSpec((1, S, IDX_DIM), lambda b, i: (b, 0, 0)),
            pl.BlockSpec((tq, H * tq), lambda b, i: (0, 0)),
            pl.BlockSpec((tq, H * tq), lambda b, i: (0, 0)),
        ],
        out_specs=pl.BlockSpec((1, tq, H * d), lambda b, i: (b, i, 0)),
        scratch_shapes=[
            pltpu.VMEM((nk, tq, tq), jnp.int32),
            pltpu.VMEM((nk, tq, tq), jnp.float32),
            pltpu.VMEM((1, H * tq), jnp.float32),
            pltpu.VMEM((1, H * tq), jnp.float32),
            pltpu.VMEM((d, H * tq), jnp.float32),
        ],
    )
    return pl.pallas_call(
        partial(_dsa_kernel, topk=topk, q_off=q_off),
        grid_spec=grid_spec,
        out_shape=jax.ShapeDtypeStruct((B, Sq, H * d), out_dtype),
        compiler_params=pltpu.CompilerParams(vmem_limit_bytes=vmem_estimate),
        name="dsa_attention",
    )(bfar, q, qi, z_wi_t, ckv, ckv_t, kidx, b0, b1)


SC_CORES = 2
SC_SUBCORES = 16
SC_LANES = 16
PEER_SC_TOKENS = 16
PEER_SC_RING = 4
PEER_SC_UNROLL = 4
PEER_SLOTS = PEER_HEADS * PEER_TOPK


def _peer_sc_call(body, T, out_width, stage_width):
    mesh = plsc.VectorSubcoreMesh(core_axis_name="c", subcore_axis_name="s")
    return pl.kernel(
        body, mesh=mesh,
        out_type=jax.ShapeDtypeStruct((T, out_width), jnp.float32),
        scratch_types=[
            pltpu.VMEM((2, PEER_SC_TOKENS, PEER_SLOTS), jnp.int32),
            pltpu.VMEM((2, PEER_SC_TOKENS, stage_width), jnp.float32),
            pltpu.VMEM((PEER_SC_RING, PEER_TOPK, D_MODEL), jnp.float32),
            pltpu.VMEM((PEER_SC_TOKENS, out_width), jnp.float32),
            pltpu.SemaphoreType.DMA((PEER_SC_RING + 2,)),
        ],
        compiler_params=pltpu.CompilerParams(needs_layout_passes=False),
    )


def _peer_sc_body(compute, zero_out, x_hbm, idx_hbm, tab_hbm, out_hbm, idx_v, x_v, rows_v, out_v, sems):
    T = idx_hbm.shape[0]
    tokens_per_worker = T // (SC_CORES * SC_SUBCORES)
    n_blocks = tokens_per_worker // PEER_SC_TOKENS
    n_steps = PEER_SC_TOKENS * PEER_HEADS
    ring = PEER_SC_RING
    worker = lax.axis_index("s") * SC_CORES + lax.axis_index("c")
    base = worker * tokens_per_worker

    def stage(blk, slot):
        tok0 = base + blk * PEER_SC_TOKENS
        return (pltpu.make_async_copy(idx_hbm.at[pl.ds(tok0, PEER_SC_TOKENS)], idx_v.at[slot], sems.at[ring]),
                pltpu.make_async_copy(x_hbm.at[pl.ds(tok0, PEER_SC_TOKENS)], x_v.at[slot], sems.at[ring + 1]))

    def gather(slot, s, b):
        ids = idx_v[slot, s // PEER_HEADS, pl.ds((s % PEER_HEADS) * PEER_TOPK, PEER_TOPK)]
        return pltpu.make_async_copy(tab_hbm.at[ids], rows_v.at[b], sems.at[b])

    for copy in stage(0, 0):
        copy.start()
    for copy in stage(0, 0):
        copy.wait()
    for b in range(ring - 1):
        gather(0, b, b).start()

    @pl.loop(0, n_blocks)
    def _(blk):
        slot = blk % 2
        has_next = blk + 1 < n_blocks

        @pl.when(has_next)
        def _():
            for copy in stage(blk + 1, 1 - slot):
                copy.start()

        if zero_out:
            @pl.loop(0, PEER_SC_TOKENS)
            def _(t):
                @plsc.parallel_loop(0, out_v.shape[1] // SC_LANES, unroll=PEER_SC_UNROLL)
                def _(c):
                    out_v[t, pl.ds(pl.multiple_of(c * SC_LANES, SC_LANES), SC_LANES)] = (
                        jnp.zeros((SC_LANES,), jnp.float32))

        @pl.loop(0, n_steps, step=ring)
        def _(s0):
            for b in range(ring):
                s = s0 + b
                gather(slot, s, b).wait()
                ahead = s + ring - 1
                ahead_buf = (b + ring - 1) % ring

                @pl.when(ahead < n_steps)
                def _():
                    gather(slot, ahead, ahead_buf).start()

                if b >= 1:
                    @pl.when(jnp.logical_and(ahead >= n_steps, has_next))
                    def _():
                        if b == 1:
                            for copy in stage(blk + 1, 1 - slot):
                                copy.wait()
                        gather(1 - slot, ahead - n_steps, ahead_buf).start()

                compute(slot, s // PEER_HEADS, s % PEER_HEADS, b, x_v, rows_v, out_v)

        pltpu.sync_copy(out_v, out_hbm.at[pl.ds(base + blk * PEER_SC_TOKENS, PEER_SC_TOKENS)])


def _peer_dots_compute(slot, t, hd, b, h_v, rows_v, dots_v):
    lane = lax.iota(jnp.int32, SC_LANES)

    def col_step(c, accs):
        off = pl.multiple_of(c * SC_LANES, SC_LANES)
        hv = h_v[slot, t, pl.ds(off, SC_LANES)]
        return tuple(accs[r] + rows_v[b, r, pl.ds(off, SC_LANES)] * hv for r in range(PEER_TOPK))

    accs = plsc.parallel_loop(
        0, D_MODEL // SC_LANES, unroll=PEER_SC_UNROLL,
        carry=tuple(jnp.zeros((SC_LANES,), jnp.float32) for _ in range(PEER_TOPK)))(col_step)
    res = jnp.zeros((SC_LANES,), jnp.float32)
    for r in range(PEER_TOPK):
        res = jnp.where(lane == r, jnp.sum(accs[r]), res)
    dots_v[t, pl.ds(hd * PEER_TOPK, PEER_TOPK)] = res


def _peer_mix_compute(slot, t, hd, b, act_v, rows_v, out_v):
    svec = jnp.full((SC_LANES,), slot, jnp.int32)
    tvec = jnp.full((SC_LANES,), t, jnp.int32)
    weights = [plsc.load_gather(act_v, [svec, tvec, jnp.full((SC_LANES,), hd * PEER_TOPK + r, jnp.int32)])
               for r in range(PEER_TOPK)]

    @plsc.parallel_loop(0, D_MODEL // SC_LANES, unroll=PEER_SC_UNROLL)
    def _(c):
        off = pl.multiple_of(c * SC_LANES, SC_LANES)
        a = rows_v[b, 0, pl.ds(off, SC_LANES)] * weights[0]
        for r in range(1, PEER_TOPK):
            a = a + rows_v[b, r, pl.ds(off, SC_LANES)] * weights[r]
        plsc.addupdate(out_v.at[t, pl.ds(off, SC_LANES)], a)


def _peer_expert_dots(h, experts, u_tab):
    T = h.shape[0]
    assert T % (SC_CORES * SC_SUBCORES * PEER_SC_TOKENS) == 0
    assert (PEER_SC_TOKENS * PEER_HEADS) % PEER_SC_RING == 0 and PEER_TOPK == SC_LANES
    body = partial(_peer_sc_body, _peer_dots_compute, False)
    return _peer_sc_call(body, T, PEER_SLOTS, D_MODEL)(h, experts, u_tab)


def _peer_expert_mix(act, experts, v_tab):
    T = act.shape[0]
    assert T % (SC_CORES * SC_SUBCORES * PEER_SC_TOKENS) == 0
    body = partial(_peer_sc_body, _peer_mix_compute, True)
    return _peer_sc_call(body, T, D_MODEL, PEER_SLOTS)(act, experts, v_tab)


PEER_TM = 256


def _extract_topk(s, k, payload=None):
    R = s.shape[0]
    riota = lax.broadcasted_iota(jnp.int32, s.shape, 0)
    vals, rows = [], []
    for _ in range(k):
        m = jnp.max(s, axis=0, keepdims=True)
        pos = jnp.min(jnp.where(s == m, riota, R), axis=0, keepdims=True)
        hit = riota == pos
        vals.append(m)
        rows.append(pos if payload is None else jnp.max(jnp.where(hit, payload, -1), axis=0, keepdims=True))
        s = jnp.where(hit, -jnp.inf, s)
    return jnp.concatenate(vals, axis=0), jnp.concatenate(rows, axis=0)


def _peer_route_kernel(x_ref, sc_ref, sh_ref, wq_ref, keys_ref, h_ref, ex_ref, gate_ref, q_ref, ext_ref, gt_ref):
    K = PEER_TOPK
    f32 = jnp.float32
    h = x_ref[0] * (1.0 + sc_ref[0]) + sh_ref[0]
    h_ref[0] = h
    q = jnp.dot(h.astype(wq_ref.dtype), wq_ref[...], preferred_element_type=f32)
    for hd in range(PEER_HEADS):
        q_ref[hd] = q[:, hd * PEER_KEY_DIM:(hd + 1) * PEER_KEY_DIM].astype(q_ref.dtype)

    def head_body(hd, carry):
        qh = q_ref[hd]
        s1 = lax.dot_general(keys_ref[hd, 0], qh[:, :PEER_HALF], _NT, preferred_element_type=f32)
        s2 = lax.dot_general(keys_ref[hd, 1], qh[:, PEER_HALF:], _NT, preferred_element_type=f32)
        v1, i1 = _extract_topk(s1, K)
        v2, i2 = _extract_topk(s2, K)
        tm = v1.shape[1]
        cand_rows, cidx_rows = [], []
        for a in range(K):
            nb = K // (a + 1)
            cand_rows.append(v1[a:a + 1, :] + v2[:nb, :])
            cidx_rows.append(i1[a:a + 1, :] * PEER_N_KEYS + i2[:nb, :])
        n_pad = -sum(r.shape[0] for r in cand_rows) % 8
        cand_rows.append(jnp.full((n_pad, tm), -jnp.inf, f32))
        cidx_rows.append(jnp.full((n_pad, tm), -1, jnp.int32))
        cand = jnp.concatenate(cand_rows, axis=0)
        cidx = jnp.concatenate(cidx_rows, axis=0)
        top_s, experts = _extract_topk(cand, K, payload=cidx)
        e = jnp.exp(top_s - top_s[0:1, :])
        gt_ref[pl.ds(pl.multiple_of(hd * K, K), K), :] = e / jnp.sum(e, axis=0, keepdims=True)
        ext_ref[pl.ds(pl.multiple_of(hd * K, K), K), :] = experts
        return carry

    lax.fori_loop(0, PEER_HEADS, head_body, 0)
    ex_ref[0] = ext_ref[...].T
    gate_ref[0] = gt_ref[...].T


def _peer_route(x, sc, sh, w_pq, sub_keys):
    B, S, D = x.shape
    tm = PEER_TM
    assert S % tm == 0
    return pl.pallas_call(
        _peer_route_kernel,
        grid=(B, S // tm),
        in_specs=[
            pl.BlockSpec((1, tm, D), lambda b, i: (b, i, 0)),
            pl.BlockSpec((1, 1, D), lambda b, i: (b, 0, 0)),
            pl.BlockSpec((1, 1, D), lambda b, i: (b, 0, 0)),
            pl.BlockSpec((D, PEER_HEADS * PEER_KEY_DIM), lambda b, i: (0, 0)),
            pl.BlockSpec((PEER_HEADS, 2, PEER_N_KEYS, PEER_HALF), lambda b, i: (0, 0, 0, 0)),
        ],
        out_specs=[
            pl.BlockSpec((1, tm, D), lambda b, i: (b, i, 0)),
            pl.BlockSpec((1, tm, PEER_SLOTS), lambda b, i: (b, i, 0)),
            pl.BlockSpec((1, tm, PEER_SLOTS), lambda b, i: (b, i, 0)),
        ],
        out_shape=[
            jax.ShapeDtypeStruct((B, S, D), jnp.float32),
            jax.ShapeDtypeStruct((B, S, PEER_SLOTS), jnp.int32),
            jax.ShapeDtypeStruct((B, S, PEER_SLOTS), jnp.float32),
        ],
        scratch_shapes=[
            pltpu.VMEM((PEER_HEADS, tm, PEER_KEY_DIM), MM_DTYPE),
            pltpu.VMEM((PEER_SLOTS, tm), jnp.int32),
            pltpu.VMEM((PEER_SLOTS, tm), jnp.float32),
        ],
        compiler_params=pltpu.CompilerParams(dimension_semantics=("parallel", "parallel")),
        name="peer_route",
    )(x, sc[:, None, :], sh[:, None, :], w_pq.astype(MM_DTYPE), sub_keys.astype(MM_DTYPE))


def _peer_pre(h2, experts, u_tab):
    B, S, D = h2.shape
    return _peer_expert_dots(h2.reshape(B * S, D), experts.reshape(B * S, PEER_SLOTS), u_tab)


def _peer_act_kernel(pre_ref, gate_ref, o_ref):
    pre = pre_ref[...]
    o_ref[...] = 0.5 * pre * (1.0 + lax.erf(pre * (2.0 ** -0.5))) * gate_ref[...]


def _peer_act(pre, gates):
    T, W = pre.shape
    tm = min(8 * ROW_TILE, T)
    assert T % tm == 0
    spec = pl.BlockSpec((tm, W), lambda i: (i, 0))
    return pl.pallas_call(
        _peer_act_kernel,
        grid=(T // tm,),
        in_specs=[spec, spec],
        out_specs=spec,
        out_shape=jax.ShapeDtypeStruct((T, W), jnp.float32),
        name="peer_act",
    )(pre, gates.reshape(T, W))


def _peer_out(act, experts, v_tab):
    B, S, _ = experts.shape
    return _peer_expert_mix(act, experts.reshape(B * S, PEER_SLOTS), v_tab).reshape(B, S, D_MODEL)


def kernel(x, c, w_ada, b_ada, w_in, rw_mu, rw_w0, rw_w2, rw_a0, rw_a2, rw_g2, rw_k_k, rw_k_a, rw_r_k, rw_gn_g, rw_gn_b, dsa_kv_g, idx_k_g, idx_k_b, rel_bias, w_br_a, w_br_b, w_out, ln1_g, ln1_b, peer_wq, peer_keys, peer_u, peer_v, ln2_g, ln2_b):
    l = 0
    mod = jax.nn.silu(c) @ w_ada[l] + b_ada[l]

    w_rw, w_q, w_kv, w_qi, w_ki, w_wi, w_ga, w_gb = jnp.split(w_in[l], _split_points(IN_SIZES), axis=-1)
    small_pad = jnp.zeros((D_MODEL, SMALL_COLS - SMALL_WI - IDX_HEADS), w_in.dtype)
    w_small = jnp.concatenate([w_qi, w_kv, w_ki, w_wi, small_pad], axis=-1)
    w_gates = jnp.concatenate([w_ga, w_gb], axis=-1)

    B, S, _ = x.shape
    seg_len = S // SEQ_SEGMENTS
    dsa_topk = min(TOPK_MAX, S // 4)
    bias_tiles = _dsa_bias_tiles(rel_bias, DSA_TQ)

    def project_and_dsa(x, sc1, sh1, seg, keys_so_far):
        z_rw = _mod_matmul(x, sc1, sh1, w_rw, jnp.float32, tn=896)
        z_q = _mod_matmul(x, sc1, sh1, w_q, MM_DTYPE, tn=1024)
        z_small = _mod_matmul(x, sc1, sh1, w_small, jnp.float32, tn=512)
        z_g = _mod_matmul(x, sc1, sh1, w_gates, jnp.float32, tn=1024)
        z_wi_t = jnp.swapaxes(z_small[..., SMALL_WI:SMALL_WI + IDX_HEADS], 1, 2)
        qi, ckv, kidx = _dsa_prep(z_small, dsa_kv_g[l], idx_k_g[l], idx_k_b[l])
        if keys_so_far is not None:
            ckv = jnp.concatenate([keys_so_far[0], ckv], axis=1)
            kidx = jnp.concatenate([keys_so_far[1], kidx], axis=1)
        y_b = _dsa_attention(z_q, qi, z_wi_t, ckv, kidx, bias_tiles, dsa_topk,
                             seg * (seg_len // DSA_TQ), MM_DTYPE)
        return (y_b, z_rw, z_g), (ckv, kidx)

    def rwkv_and_merge(x, gt1, carry, y_b, z_rw, z_g):
        y_a, carry = _rwkv7_time_mix(z_rw, carry, rw_mu[l], rw_w0[l], rw_w2[l], rw_a0[l], rw_a2[l], rw_g2[l],
                                     rw_k_k[l], rw_k_a[l], rw_r_k[l], rw_gn_g[l], rw_gn_b[l], MM_DTYPE)
        return _merge_ln(x, gt1, y_a, y_b, z_g, w_br_a[l], w_br_b[l], w_out[l], ln1_g[l], ln1_b[l]), carry

    def finish(st, act):
        y2 = _peer_out(act, st["experts"], peer_v[l])
        return _residual_ln(st["x1"], st["gt2"], y2, ln2_g[l], ln2_b[l])

    assert B % BATCH_GROUPS == 0 and S % (SEQ_SEGMENTS * DSA_TQ) == 0
    gsz = B // BATCH_GROUPS
    outs = []
    prev = None
    for g in range(BATCH_GROUPS):
        sh1, sc1, gt1, sh2, sc2, gt2 = jnp.split(mod[g * gsz:(g + 1) * gsz], 6, axis=-1)
        rw_carry = _rwkv7_init_carry(gsz)
        keys_so_far = None
        for seg in range(SEQ_SEGMENTS):
            x_g = x[g * gsz:(g + 1) * gsz, seg * seg_len:(seg + 1) * seg_len]
            if prev is not None:
                x_g, prev["experts"] = lax.optimization_barrier((x_g, prev["experts"]))
            proj, keys_so_far = project_and_dsa(x_g, sc1, sh1, seg, keys_so_far)
            if prev is not None:
                proj, prev["pre"], outs = lax.optimization_barrier((proj, prev["pre"], outs))
                outs.append(finish(prev, _peer_act(prev["pre"], prev["gates"])))
            x1, rw_carry = rwkv_and_merge(x_g, gt1, rw_carry, *proj)
            h2, experts, gates = _peer_route(x1, sc2, sh2, peer_wq[l], peer_keys[l])
            prev = dict(x1=x1, gt2=gt2, experts=experts, gates=gates, pre=_peer_pre(h2, experts, peer_u[l]))
    prev["pre"], outs = lax.optimization_barrier((prev["pre"], outs))
    outs.append(finish(prev, _peer_act(prev["pre"], prev["gates"])))
    rows = [jnp.concatenate(outs[g * SEQ_SEGMENTS:(g + 1) * SEQ_SEGMENTS], axis=1) for g in range(BATCH_GROUPS)]
    return jnp.concatenate(rows, axis=0)
```

```python
import math
from functools import partial

import jax
import jax.numpy as jnp
import numpy as np
from jax import lax
from jax.experimental import pallas as pl
from jax.experimental.pallas import tpu as pltpu
from jax.experimental.pallas import tpu_sc as plsc

D_MODEL = 1024
RW_HEADS = 8
RW_HEAD_DIM = 64
RW_DIM = 512
RW_DECAY_LORA = 64
RW_A_LORA = 64
RW_GATE_LORA = 128
RW_COLS = 3 * RW_DIM + RW_DECAY_LORA + RW_A_LORA + RW_GATE_LORA
RW_GN_EPS = 64e-5
DSA_HEADS = 8
DSA_LATENT = 128
DSA_Q_DIM = DSA_HEADS * DSA_LATENT
IDX_HEADS = 4
IDX_DIM = 64
TOPK_MAX = 256
REL_BUCKETS = 32
REL_MAX_DIST = 128
IN_SIZES = (RW_COLS, DSA_Q_DIM, DSA_LATENT, IDX_HEADS * IDX_DIM, IDX_DIM, IDX_HEADS, D_MODEL, D_MODEL)
IN_COLS = sum(IN_SIZES)
PEER_HEADS = 8
PEER_N_KEYS = 128
PEER_KEY_DIM = 128
PEER_HALF = 64
PEER_TOPK = 16
LN_EPS = 1e-5
DEPTH = 1
DEEPNORM_ALPHA = (2.0 * DEPTH) ** 0.25

LANES = 128
SMALL_QI = 0
SMALL_KV = SMALL_QI + IDX_HEADS * IDX_DIM
SMALL_KI = SMALL_KV + DSA_LATENT
SMALL_WI = SMALL_KI + IDX_DIM
SMALL_COLS = 4 * LANES
MM_DTYPE = jnp.bfloat16
BATCH_GROUPS = 8
SEQ_SEGMENTS = 2


def _split_points(sizes):
    return np.cumsum(sizes)[:-1].tolist()


def _mod_matmul_kernel(x_ref, sc_ref, sh_ref, w_ref, o_ref):
    h = x_ref[0] * (1.0 + sc_ref[0]) + sh_ref[0]
    o_ref[0] = jnp.dot(h.astype(w_ref.dtype), w_ref[...],
                       preferred_element_type=jnp.float32).astype(o_ref.dtype)


def _mod_matmul(x, sc, sh, w, out_dtype, tn, tm=512):
    B, S, D = x.shape
    N = w.shape[1]
    assert S % tm == 0 and N % tn == 0
    return pl.pallas_call(
        _mod_matmul_kernel,
        grid=(B, S // tm, N // tn),
        in_specs=[
            pl.BlockSpec((1, tm, D), lambda b, i, j: (b, i, 0)),
            pl.BlockSpec((1, 1, D), lambda b, i, j: (b, 0, 0)),
            pl.BlockSpec((1, 1, D), lambda b, i, j: (b, 0, 0)),
            pl.BlockSpec((D, tn), lambda b, i, j: (0, j)),
        ],
        out_specs=pl.BlockSpec((1, tm, tn), lambda b, i, j: (b, i, j)),
        out_shape=jax.ShapeDtypeStruct((B, S, N), out_dtype),
        name="mod_matmul",
    )(x, sc[:, None, :], sh[:, None, :], w.astype(MM_DTYPE))


ROW_TILE = 256


def _layer_norm(x, g, b):
    mu = jnp.mean(x, -1, keepdims=True)
    var = jnp.mean(jnp.square(x - mu), -1, keepdims=True)
    return (x - mu) * lax.rsqrt(var + LN_EPS) * g + b


def _merge_ln_kernel(x_ref, gt_ref, ya_ref, yb_ref, zg_ref, wa_ref, wb_ref, wo_ref, g_ref, b_ref, o_ref):
    f32 = jnp.float32
    D = x_ref.shape[2]
    ya = jnp.dot(ya_ref[0], wa_ref[...], preferred_element_type=f32)
    yb = jnp.dot(yb_ref[0], wb_ref[...], preferred_element_type=f32)
    zg = zg_ref[0]
    merged = jax.nn.sigmoid(zg[:, :D]) * ya + jax.nn.sigmoid(zg[:, D:]) * yb
    mix = jnp.dot(merged.astype(wo_ref.dtype), wo_ref[...], preferred_element_type=f32)
    o_ref[0] = _layer_norm(DEEPNORM_ALPHA * x_ref[0] + gt_ref[0] * mix, g_ref[...], b_ref[...])


def _merge_ln(x, gt, y_a, y_b, z_g, w_a, w_b, w_o, ln_g, ln_b):
    B, S, D = x.shape
    tm = ROW_TILE
    assert S % tm == 0
    tok = lambda width: pl.BlockSpec((1, tm, width), lambda b, i: (b, i, 0))
    full = lambda a: pl.BlockSpec(a.shape, lambda b, i: (0,) * a.ndim)
    w_a, w_b, w_o = (w.astype(MM_DTYPE) for w in (w_a, w_b, w_o))
    ln_g, ln_b = ln_g[None], ln_b[None]
    return pl.pallas_call(
        _merge_ln_kernel,
        grid=(B, S // tm),
        in_specs=[tok(D), pl.BlockSpec((1, 1, D), lambda b, i: (b, 0, 0)), tok(y_a.shape[2]), tok(y_b.shape[2]),
                  tok(z_g.shape[2]), full(w_a), full(w_b), full(w_o), full(ln_g), full(ln_b)],
        out_specs=tok(D),
        out_shape=jax.ShapeDtypeStruct((B, S, D), jnp.float32),
        compiler_params=pltpu.CompilerParams(dimension_semantics=("parallel", "parallel")),
        name="merge_ln",
    )(x, gt[:, None, :], y_a, y_b, z_g, w_a, w_b, w_o, ln_g, ln_b)


def _residual_ln_kernel(x_ref, gt_ref, y_ref, g_ref, b_ref, o_ref):
    o_ref[0] = _layer_norm(DEEPNORM_ALPHA * x_ref[0] + gt_ref[0] * y_ref[0], g_ref[...], b_ref[...])


def _residual_ln(x, gt, y, ln_g, ln_b):
    B, S, D = x.shape
    tm = ROW_TILE
    assert S % tm == 0
    tok = pl.BlockSpec((1, tm, D), lambda b, i: (b, i, 0))
    vec = pl.BlockSpec((1, D), lambda b, i: (0, 0))
    return pl.pallas_call(
        _residual_ln_kernel,
        grid=(B, S // tm),
        in_specs=[tok, pl.BlockSpec((1, 1, D), lambda b, i: (b, 0, 0)), tok, vec, vec],
        out_specs=tok,
        out_shape=jax.ShapeDtypeStruct((B, S, D), jnp.float32),
        compiler_params=pltpu.CompilerParams(dimension_semantics=("parallel", "parallel")),
        name="residual_ln",
    )(x, gt[:, None, :], y, ln_g[None], ln_b[None])


def _t5_bucket(n):
    n = jnp.maximum(n, 0)
    max_exact = REL_BUCKETS // 2
    nf = jnp.maximum(n, 1).astype(jnp.float32)
    large = max_exact + (jnp.log(nf / max_exact) / math.log(REL_MAX_DIST / max_exact)
                         * (REL_BUCKETS - max_exact)).astype(jnp.int32)
    large = jnp.minimum(large, REL_BUCKETS - 1)
    return jnp.where(n < max_exact, n, large)


RW_CHUNK = 64
RW_INV_BLOCK = 16
_NN = (((1,), (0,)), ((), ()))
_NT = (((1,), (1,)), ((), ()))
_BNN = (((2,), (1,)), ((0,), (0,)))
_BNT = (((2,), (2,)), ((0,), (0,)))


def _dot_f32(a, b, dims=_NN):
    return lax.dot_general(a, b, dims, precision=lax.Precision.HIGHEST,
                           preferred_element_type=jnp.float32)


def _dot_bf16x3(a, b, dims=_NN):
    f32, bf = jnp.float32, jnp.bfloat16
    a_hi, b_hi = a.astype(bf), b.astype(bf)
    a_lo = (a - a_hi.astype(f32)).astype(bf)
    b_lo = (b - b_hi.astype(f32)).astype(bf)
    out = lax.dot_general(a_hi, b_hi, dims, preferred_element_type=f32)
    out = out + lax.dot_general(a_hi, b_lo, dims, preferred_element_type=f32)
    return out + lax.dot_general(a_lo, b_hi, dims, preferred_element_type=f32)


def _bf16_terms(a):
    f32, bf = jnp.float32, jnp.bfloat16
    hi = a.astype(bf)
    r1 = a - hi.astype(f32)
    mid = r1.astype(bf)
    lo = (r1 - mid.astype(f32)).astype(bf)
    return hi, mid, lo


def _dot_lhs_split(a, b01):
    b = b01.astype(jnp.bfloat16)
    return sum(jnp.dot(t, b, preferred_element_type=jnp.float32) for t in _bf16_terms(a))


def _dot_rhs_split(a01, b):
    a = a01.astype(jnp.bfloat16)
    return sum(jnp.dot(a, t, preferred_element_type=jnp.float32) for t in _bf16_terms(b))


def _rwkv_kernel(z_ref, m0_ref, p0_ref, mu_ref, w0_ref, w2_ref, a0_ref, a2_ref, g2_ref, kk_ref, ka_ref, rk_ref,
                 gng_ref, gnb_ref, bd_ref, o_ref, mo_ref, po_ref, m_ref, prev_ref, y_ref):
    C = z_ref.shape[1]
    N = RW_HEAD_DIM
    f32 = jnp.float32
    dot3 = _dot_bf16x3

    @pl.when(pl.program_id(1) == 0)
    def _():
        m_ref[...] = m0_ref[0]
        prev_ref[...] = p0_ref[0]

    z = z_ref[0]
    row = lax.broadcasted_iota(jnp.int32, z.shape, 0)
    shifted = jnp.where(row == 0, prev_ref[...], pltpu.roll(z, 1, axis=0))
    prev_ref[...] = z[C - 1:C, :]
    zs = z + (shifted - z) * mu_ref[...]
    r = zs[:, 0:RW_DIM]
    k = zs[:, RW_DIM:2 * RW_DIM]
    v = zs[:, 2 * RW_DIM:3 * RW_DIM]
    o1 = 3 * RW_DIM
    wl = zs[:, o1:o1 + RW_DECAY_LORA]
    al = zs[:, o1 + RW_DECAY_LORA:o1 + RW_DECAY_LORA + RW_A_LORA]
    gl = zs[:, o1 + RW_DECAY_LORA + RW_A_LORA:]

    bd = bd_ref[...]
    log_w = -jax.nn.softplus(-(w0_ref[...] + dot3(jnp.tanh(wl), w2_ref[...]))) - 0.5
    ldec = -jnp.exp(log_w)
    a_lr = jax.nn.sigmoid(a0_ref[...] + dot3(al, a2_ref[...]))
    g = dot3(jax.nn.sigmoid(gl), g2_ref[...])
    kk = k * kk_ref[...]
    kk = kk * lax.rsqrt(jnp.maximum(_dot_lhs_split(kk * kk, bd), 1e-24))
    k2 = k * (1.0 + (a_lr - 1.0) * ka_ref[...])
    a_vec = -kk
    b_vec = kk * a_lr

    ti = lax.broadcasted_iota(jnp.int32, (C, C), 0)
    tj = lax.broadcasted_iota(jnp.int32, (C, C), 1)
    cum = _dot_rhs_split((ti >= tj).astype(f32), ldec)
    cum_last = cum[C - 1:C, :]
    w_incl = jnp.exp(cum)
    w_excl = jnp.exp(cum - ldec)
    w_inv = jnp.exp(-cum)
    w_end = jnp.exp(cum_last - cum)
    w_all = jnp.exp(cum_last)
    a_t = a_vec * w_excl
    r_t = r * w_incl
    b_t = b_vec * w_inv
    k_t = k2 * w_inv
    b_e = b_vec * w_end
    k_e = k2 * w_end

    strict = ti > tj
    incl = ti >= tj
    bi, bj = ti // RW_INV_BLOCK, tj // RW_INV_BLOCK
    same_blk = bi == bj
    pair_blk = jnp.logical_and((bi // 2) == (bj // 2), jnp.logical_not(same_blk))
    half_blk = (bi // 2) != (bj // 2)
    eye = (ti == tj).astype(f32)

    H = RW_HEADS
    heads = lambda x: jnp.stack([x[:, h * N:(h + 1) * N] for h in range(H)], axis=0)
    bmm = lambda a, b: dot3(a, b, _BNN)
    A, Rt, Bt, Kt, Be, Ke, V = (heads(t) for t in (a_t, r_t, b_t, k_t, b_e, k_e, v))
    gm = dot3(jnp.concatenate([A, Rt], axis=1), jnp.concatenate([Bt, Kt], axis=1), _BNT)
    a_ab = jnp.where(strict, gm[:, :C, :C], 0.0)
    a_ak = jnp.where(strict, gm[:, :C, C:], 0.0)
    a_rb = jnp.where(incl, gm[:, C:, :C], 0.0)
    a_rk = jnp.where(incl, gm[:, C:, C:], 0.0)
    d1 = jnp.where(same_blk, a_ab, 0.0)
    xinv = eye + d1
    d2 = bmm(d1, d1)
    xinv = xinv + bmm(xinv, d2)
    d4 = bmm(d2, d2)
    xinv = xinv + bmm(xinv, d4)
    d8 = bmm(d4, d4)
    xinv = xinv + bmm(xinv, d8)
    xinv = xinv + bmm(bmm(xinv, jnp.where(pair_blk, a_ab, 0.0)), xinv)
    xinv = xinv + bmm(bmm(xinv, jnp.where(half_blk, a_ab, 0.0)), xinv)

    av = bmm(jnp.concatenate([a_ak, a_rk], axis=1), V)
    p = bmm(xinv, jnp.concatenate([A, av[:, :C]], axis=2))
    qm = bmm(a_rb, p)
    q1 = Rt + qm[:, :, :N]
    q2 = qm[:, :, N:] + av[:, C:]
    gmat = bmm(jnp.swapaxes(Be, 1, 2), p)
    g1 = eye * heads(w_all) + gmat[:, :, :N]
    g2 = gmat[:, :, N:] + bmm(jnp.swapaxes(Ke, 1, 2), V)
    m = m_ref[...]
    yh = _dot_f32(q1, m, _BNN) + q2
    m_ref[...] = _dot_f32(g1, m, _BNN) + g2
    for h in range(H):
        y_ref[:, h * N:(h + 1) * N] = yh[h]

    y = y_ref[...]
    mean = _dot_lhs_split(y, bd) * (1.0 / N)
    yc = y - mean
    var = _dot_lhs_split(yc * yc, bd) * (1.0 / N)
    yn = yc * lax.rsqrt(var + RW_GN_EPS) * gng_ref[...] + gnb_ref[...]
    bonus = _dot_lhs_split(r * k2 * rk_ref[...], bd) * v
    o_ref[0] = ((yn + bonus) * g).astype(o_ref.dtype)

    @pl.when(pl.program_id(1) == pl.num_programs(1) - 1)
    def _():
        mo_ref[0] = m_ref[...]
        po_ref[0] = prev_ref[...]


def _rwkv7_init_carry(batch):
    return (jnp.zeros((batch, RW_HEADS, RW_HEAD_DIM, RW_HEAD_DIM), jnp.float32),
            jnp.zeros((batch, 1, RW_COLS), jnp.float32))


def _rwkv7_time_mix(z_rw, carry, mu, w0, w2, a0, a2, g2, k_k, k_a, r_k, gn_g, gn_b, out_dtype):
    B, S, _ = z_rw.shape
    C = RW_CHUNK
    assert S % C == 0 and C == RW_HEAD_DIM and C % (4 * RW_INV_BLOCK) == 0
    hid = jnp.arange(RW_DIM) // RW_HEAD_DIM
    bd = (hid[:, None] == hid[None, :]).astype(jnp.float32)
    row = lambda a: a.reshape(1, -1)
    full = lambda shape: pl.BlockSpec(shape, lambda b, c: (0,) * len(shape))
    state_spec = pl.BlockSpec((1, RW_HEADS, RW_HEAD_DIM, RW_HEAD_DIM), lambda b, c: (b, 0, 0, 0))
    prev_spec = pl.BlockSpec((1, 1, RW_COLS), lambda b, c: (b, 0, 0))
    y, m_out, p_out = pl.pallas_call(
        _rwkv_kernel,
        grid=(B, S // C),
        in_specs=[
            pl.BlockSpec((1, C, RW_COLS), lambda b, c: (b, c, 0)), state_spec, prev_spec,
            full((1, RW_COLS)), full((1, RW_DIM)), full((RW_DECAY_LORA, RW_DIM)), full((1, RW_DIM)),
            full((RW_A_LORA, RW_DIM)), full((RW_GATE_LORA, RW_DIM)), full((1, RW_DIM)), full((1, RW_DIM)),
            full((1, RW_DIM)), full((1, RW_DIM)), full((1, RW_DIM)), full((RW_DIM, RW_DIM)),
        ],
        out_specs=[pl.BlockSpec((1, C, RW_DIM), lambda b, c: (b, c, 0)), state_spec, prev_spec],
        out_shape=[jax.ShapeDtypeStruct((B, S, RW_DIM), out_dtype),
                   jax.ShapeDtypeStruct(carry[0].shape, jnp.float32),
                   jax.ShapeDtypeStruct(carry[1].shape, jnp.float32)],
        scratch_shapes=[
            pltpu.VMEM((RW_HEADS, RW_HEAD_DIM, RW_HEAD_DIM), jnp.float32),
            pltpu.VMEM((1, RW_COLS), jnp.float32),
            pltpu.VMEM((C, RW_DIM), jnp.float32),
        ],
        compiler_params=pltpu.CompilerParams(dimension_semantics=("parallel", "arbitrary")),
        name="rwkv7_time_mix",
    )(z_rw, carry[0], carry[1], row(mu), row(w0), w2, row(a0), a2, g2, row(k_k), row(k_a), row(r_k),
      row(gn_g), row(gn_b), bd)
    return y, (m_out, p_out)


DSA_TQ = 256
MASK_NEG = -1e30
INT_MIN = -2 ** 31
KEY_NEG_INF = -2139095041
THRESH_BITS = 32


def _dsa_prep_kernel(z_ref, kvg_ref, kig_ref, kib_ref, qi_ref, kv_ref, ki_ref):
    z = z_ref[0]
    qi_ref[0] = z[:, SMALL_QI:SMALL_KV].astype(qi_ref.dtype)
    kv = z[:, SMALL_KV:SMALL_KI]
    ms = jnp.mean(jnp.square(kv), -1, keepdims=True)
    kv_ref[0] = (kv * lax.rsqrt(ms + LN_EPS) * kvg_ref[...]).astype(kv_ref.dtype)
    ki = z[:, SMALL_KI:SMALL_WI]
    mu = jnp.mean(ki, -1, keepdims=True)
    var = jnp.mean(jnp.square(ki - mu), -1, keepdims=True)
    ki_ref[0] = ((ki - mu) * lax.rsqrt(var + LN_EPS) * kig_ref[...] + kib_ref[...]).astype(ki_ref.dtype)


def _dsa_prep(z_small, kv_g, ki_g, ki_b, tm=512):
    B, S, W = z_small.shape
    assert S % tm == 0
    return pl.pallas_call(
        _dsa_prep_kernel,
        grid=(B, S // tm),
        in_specs=[
            pl.BlockSpec((1, tm, W), lambda b, i: (b, i, 0)),
            pl.BlockSpec((1, DSA_LATENT), lambda b, i: (0, 0)),
            pl.BlockSpec((1, IDX_DIM), lambda b, i: (0, 0)),
            pl.BlockSpec((1, IDX_DIM), lambda b, i: (0, 0)),
        ],
        out_specs=[
            pl.BlockSpec((1, tm, IDX_HEADS * IDX_DIM), lambda b, i: (b, i, 0)),
            pl.BlockSpec((1, tm, DSA_LATENT), lambda b, i: (b, i, 0)),
            pl.BlockSpec((1, tm, IDX_DIM), lambda b, i: (b, i, 0)),
        ],
        out_shape=[
            jax.ShapeDtypeStruct((B, S, IDX_HEADS * IDX_DIM), MM_DTYPE),
            jax.ShapeDtypeStruct((B, S, DSA_LATENT), MM_DTYPE),
            jax.ShapeDtypeStruct((B, S, IDX_DIM), MM_DTYPE),
        ],
        name="dsa_prep",
    )(z_small, kv_g[None], ki_g[None], ki_b[None])


def _sortable_key(s):
    s = jnp.where(s == 0.0, 0.0, s)
    bits = pltpu.bitcast(s, jnp.int32)
    return bits ^ ((bits >> 31) & 0x7FFFFFFF)


def _col_count(mask_i32):
    tk, tq = mask_i32.shape
    return jnp.sum(mask_i32.reshape(tk // 8, 8, tq), axis=0)


def _dsa_kernel(bfar_ref, q_ref, qi_ref, wit_ref, kv_ref, kvt_ref, ki_ref, b0_ref, b1_ref, o_ref,
                key_ref, madd_ref, m_ref, l_ref, acc_ref, *, topk, q_off):
    tq = q_ref.shape[1]
    tk = tq
    i = pl.program_id(1) + q_off
    nj = i + 1
    f32 = jnp.float32
    krow = lax.broadcasted_iota(jnp.int32, (tk, tq), 0)
    qcol = lax.broadcasted_iota(jnp.int32, (tk, tq), 1)

    qi = qi_ref[0]
    wit = wit_ref[0] * (IDX_HEADS ** -0.5)

    def score_chunk(j, carry):
        off = pl.multiple_of(j * tk, tk)
        kc = ki_ref[0, pl.ds(off, tk), :]
        s = jnp.zeros((tk, tq), f32)
        for h in range(IDX_HEADS):
            d = lax.dot_general(kc, qi[:, h * IDX_DIM:(h + 1) * IDX_DIM],
                                (((1,), (1,)), ((), ())), preferred_element_type=f32)
            s = s + wit[h:h + 1, :] * jnp.maximum(d * (IDX_DIM ** -0.5), 0.0)
        causal = (krow + j * tk) <= (qcol + i * tq)
        s = jnp.where(causal, s, -jnp.inf)
        key_ref[j] = _sortable_key(s)
        return carry

    lax.fori_loop(0, nj, score_chunk, 0)

    def count_where(pred_fn):
        def body(j, acc):
            return acc + _col_count(pred_fn(key_ref[j], j).astype(jnp.int32))
        acc = lax.fori_loop(0, nj, body, jnp.zeros((8, tq), jnp.int32))
        return jnp.sum(acc, axis=0, keepdims=True)

    def bits_left(c):
        it, _, cnt_ge = c
        unsettled = jnp.max(jnp.where(cnt_ge == topk, 0, 1))
        return jnp.logical_and(it < THRESH_BITS, unsettled > 0)

    def bit_step(c):
        it, t_u, cnt_ge = c
        cand_u = t_u | jnp.left_shift(jnp.int32(1), THRESH_BITS - 1 - it)
        cand = cand_u ^ INT_MIN
        cnt = count_where(lambda k, j: k >= cand)
        take = cnt >= topk
        return it + 1, jnp.where(take, cand_u, t_u), jnp.where(take, cnt, cnt_ge)

    _, t_u, cnt_ge = lax.while_loop(
        bits_left, bit_step,
        (jnp.int32(0), jnp.zeros((1, tq), jnp.int32), jnp.full((1, tq), 1, jnp.int32) * (nj * tk)))
    thr = t_u ^ INT_MIN
    is_neg = thr <= KEY_NEG_INF
    need = jnp.logical_and(cnt_ge > topk, jnp.logical_not(is_neg))

    thr_open = jnp.where(is_neg, KEY_NEG_INF, thr - 1)
    any_need = jnp.max(need.astype(jnp.int32)) > 0

    @pl.when(jnp.logical_not(any_need))
    def _():
        def body(j, carry):
            madd_ref[j] = jnp.where(key_ref[j] > thr_open, 0.0, MASK_NEG)
            return carry
        lax.fori_loop(0, nj, body, 0)

    @pl.when(any_need)
    def _():
        n_tie_take = topk - count_where(lambda k, j: k > thr)
        s_len = tk * key_ref.shape[0]
        n_bits = max(1, int(math.ceil(math.log2(s_len))))

        def idx_step(it, p):
            bit = n_bits - 1 - it
            cand = p | jnp.left_shift(jnp.int32(1), bit)
            cnt = count_where(
                lambda k, j: jnp.where(k == thr, jnp.where((krow + j * tk) < cand, 1, 0), 0))
            return jnp.where(cnt < n_tie_take, cand, p)

        p_idx = lax.fori_loop(0, n_bits, idx_step, jnp.zeros((1, tq), jnp.int32))
        p_idx = jnp.where(need, p_idx, jnp.where(is_neg, -1, s_len))

        def body(j, carry):
            k = key_ref[j]
            tie_ok = jnp.where((krow + j * tk) <= p_idx, 0.0, MASK_NEG)
            madd_ref[j] = jnp.where(k > thr, 0.0, jnp.where(k == thr, tie_ok, MASK_NEG))
            return carry
        lax.fori_loop(0, nj, body, 0)

    H, d = DSA_HEADS, DSA_LATENT
    m_ref[...] = jnp.full(m_ref.shape, MASK_NEG, f32)
    l_ref[...] = jnp.zeros(l_ref.shape, f32)
    acc_ref[...] = jnp.zeros(acc_ref.shape, f32)
    scale = DSA_LATENT ** -0.5
    q_all = jnp.concatenate([q_ref[0, :, h * d:(h + 1) * d] for h in range(H)], axis=0)

    def attend(j, bias):
        off = pl.multiple_of(j * tk, tk)
        kc = kv_ref[0, pl.ds(off, tk), :]
        kct = kvt_ref[0, :, pl.ds(off, tk)]
        lg = lax.dot_general(kc, q_all, (((1,), (1,)), ((), ())), preferred_element_type=f32)
        lg = lg * scale + bias + jnp.tile(madd_ref[j], (1, H))
        m_old = m_ref[...]
        m_new = jnp.maximum(m_old, jnp.max(lg, axis=0, keepdims=True))
        alpha = jnp.exp(m_old - m_new)
        p = jnp.exp(lg - m_new)
        l_ref[...] = alpha * l_ref[...] + jnp.sum(p, axis=0, keepdims=True)
        acc_ref[...] = alpha * acc_ref[...] + jnp.dot(kct, p.astype(kct.dtype), preferred_element_type=f32)
        m_ref[...] = m_new

    def far_body(j, carry):
        attend(j, bfar_ref[...])
        return carry

    lax.fori_loop(0, jnp.maximum(i - 1, 0), far_body, 0)

    @pl.when(i >= 1)
    def _():
        attend(i - 1, b1_ref[...])

    attend(i, b0_ref[...])

    out_t = acc_ref[...] / l_ref[...]
    for h in range(H):
        o_ref[0, :, h * d:(h + 1) * d] = out_t[:, h * tq:(h + 1) * tq].T.astype(o_ref.dtype)


def _dsa_bias_tiles(rel_bias, tq):
    dist = jnp.arange(2 * tq, dtype=jnp.int32)
    by_dist = rel_bias[_t5_bucket(dist)].T

    def toeplitz(v):
        H, P = v.shape
        skew = jnp.tile(v, (1, tq + 1))[:, :tq * (P + 1)].reshape(H, tq, P + 1)
        return skew[:, ::-1, :tq]

    b0 = toeplitz(jnp.concatenate([jnp.repeat(by_dist[:, :1], tq - 1, axis=1), by_dist[:, :tq + 1]], axis=1))
    b1 = toeplitz(jnp.concatenate([by_dist[:, 1:], by_dist[:, -1:]], axis=1))
    H = by_dist.shape[0]
    b0 = jnp.transpose(b0, (1, 0, 2)).reshape(tq, H * tq)
    b1 = jnp.transpose(b1, (1, 0, 2)).reshape(tq, H * tq)
    return b0, b1, jnp.repeat(by_dist[:, -1], tq)[None, :]


def _dsa_attention(q, qi, z_wi_t, ckv, kidx, bias_tiles, topk, q_off, out_dtype):
    B, Sq, _ = q.shape
    tq = DSA_TQ
    nq = Sq // tq
    S = ckv.shape[1]
    assert Sq % tq == 0 and S == (q_off + nq) * tq and tq >= REL_MAX_DIST and topk <= tq
    ckv_t = jnp.swapaxes(ckv, 1, 2)
    b0, b1, bfar = bias_tiles
    nk = S // tq
    H, d = DSA_HEADS, DSA_LATENT
    f32_bytes = 4
    vmem_estimate = (
        2 * nk * tq * tq * f32_bytes
        + 2 * 2 * tq * H * tq * f32_bytes
        + 2 * (2 * S * d + S * LANES) * 2
        + 6 * tq * H * tq * f32_bytes
        + 4 * tq * H * d * f32_bytes
    )
    grid_spec = pltpu.PrefetchScalarGridSpec(
        num_scalar_prefetch=0,
        grid=(B, nq),
        in_specs=[
            pl.BlockSpec((1, H * tq), lambda b, i: (0, 0)),
            pl.BlockSpec((1, tq, H * d), lambda b, i: (b, i, 0)),
            pl.BlockSpec((1, tq, IDX_HEADS * IDX_DIM), lambda b, i: (b, i, 0)),
            pl.BlockSpec((1, IDX_HEADS, tq), lambda b, i: (b, 0, i)),
            pl.BlockSpec((1, S, d), lambda b, i: (b, 0, 0)),
            pl.BlockSpec((1, d, S), lambda b, i: (b, 0, 0)),
            pl.BlockSpec((1, S, IDX_DIM), lambda b, i: (b, 0, 0)),
            pl.BlockSpec((tq, H * tq), lambda b, i: (0, 0)),
            pl.BlockSpec((tq, H * tq), lambda b, i: (0, 0)),
        ],
        out_specs=pl.BlockSpec((1, tq, H * d), lambda b, i: (b, i, 0)),
        scratch_shapes=[
            pltpu.VMEM((nk, tq, tq), jnp.int32),
            pltpu.VMEM((nk, tq, tq), jnp.float32),
            pltpu.VMEM((1, H * tq), jnp.float32),
            pltpu.VMEM((1, H * tq), jnp.float32),
            pltpu.VMEM((d, H * tq), jnp.float32),
        ],
    )
    return pl.pallas_call(
        partial(_dsa_kernel, topk=topk, q_off=q_off),
        grid_spec=grid_spec,
        out_shape=jax.ShapeDtypeStruct((B, Sq, H * d), out_dtype),
        compiler_params=pltpu.CompilerParams(vmem_limit_bytes=vmem_estimate),
        name="dsa_attention",
    )(bfar, q, qi, z_wi_t, ckv, ckv_t, kidx, b0, b1)


SC_CORES = 2
SC_SUBCORES = 16
SC_LANES = 16
PEER_SC_TOKENS = 8
PEER_SC_RING = 4
PEER_SC_UNROLL = 4
PEER_SLOTS = PEER_HEADS * PEER_TOPK


def _peer_sc_call(body, T, out_width, stage_width):
    mesh = plsc.VectorSubcoreMesh(core_axis_name="c", subcore_axis_name="s")
    return pl.kernel(
        body, mesh=mesh,
        out_type=jax.ShapeDtypeStruct((T, out_width), jnp.float32),
        scratch_types=[
            pltpu.VMEM((2, PEER_SC_TOKENS, PEER_SLOTS), jnp.int32),
            pltpu.VMEM((2, PEER_SC_TOKENS, stage_width), jnp.float32),
            pltpu.VMEM((PEER_SC_RING, PEER_TOPK, D_MODEL), jnp.float32),
            pltpu.VMEM((PEER_SC_TOKENS, out_width), jnp.float32),
            pltpu.SemaphoreType.DMA((PEER_SC_RING + 2,)),
        ],
        compiler_params=pltpu.CompilerParams(needs_layout_passes=False),
    )


def _peer_sc_body(compute, zero_out, x_hbm, idx_hbm, tab_hbm, out_hbm, idx_v, x_v, rows_v, out_v, sems):
    T = idx_hbm.shape[0]
    tokens_per_worker = T // (SC_CORES * SC_SUBCORES)
    n_blocks = tokens_per_worker // PEER_SC_TOKENS
    n_steps = PEER_SC_TOKENS * PEER_HEADS
    ring = PEER_SC_RING
    worker = lax.axis_index("s") * SC_CORES + lax.axis_index("c")
    base = worker * tokens_per_worker

    def stage(blk, slot):
        tok0 = base + blk * PEER_SC_TOKENS
        return (pltpu.make_async_copy(idx_hbm.at[pl.ds(tok0, PEER_SC_TOKENS)], idx_v.at[slot], sems.at[ring]),
                pltpu.make_async_copy(x_hbm.at[pl.ds(tok0, PEER_SC_TOKENS)], x_v.at[slot], sems.at[ring + 1]))

    def gather(slot, s, b):
        ids = idx_v[slot, s // PEER_HEADS, pl.ds((s % PEER_HEADS) * PEER_TOPK, PEER_TOPK)]
        return pltpu.make_async_copy(tab_hbm.at[ids], rows_v.at[b], sems.at[b])

    for copy in stage(0, 0):
        copy.start()
    for copy in stage(0, 0):
        copy.wait()
    for b in range(ring - 1):
        gather(0, b, b).start()

    @pl.loop(0, n_blocks)
    def _(blk):
        slot = blk % 2
        has_next = blk + 1 < n_blocks

        @pl.when(has_next)
        def _():
            for copy in stage(blk + 1, 1 - slot):
                copy.start()

        if zero_out:
            @pl.loop(0, PEER_SC_TOKENS)
            def _(t):
                @plsc.parallel_loop(0, out_v.shape[1] // SC_LANES, unroll=PEER_SC_UNROLL)
                def _(c):
                    out_v[t, pl.ds(pl.multiple_of(c * SC_LANES, SC_LANES), SC_LANES)] = (
                        jnp.zeros((SC_LANES,), jnp.float32))

        @pl.loop(0, n_steps, step=ring)
        def _(s0):
            for b in range(ring):
                s = s0 + b
                gather(slot, s, b).wait()
                ahead = s + ring - 1
                ahead_buf = (b + ring - 1) % ring

                @pl.when(ahead < n_steps)
                def _():
                    gather(slot, ahead, ahead_buf).start()

                if b >= 1:
                    @pl.when(jnp.logical_and(ahead >= n_steps, has_next))
                    def _():
                        if b == 1:
                            for copy in stage(blk + 1, 1 - slot):
                                copy.wait()
                        gather(1 - slot, ahead - n_steps, ahead_buf).start()

                compute(slot, s // PEER_HEADS, s % PEER_HEADS, b, x_v, rows_v, out_v)

        pltpu.sync_copy(out_v, out_hbm.at[pl.ds(base + blk * PEER_SC_TOKENS, PEER_SC_TOKENS)])


def _peer_dots_compute(slot, t, hd, b, h_v, rows_v, dots_v):
    lane = lax.iota(jnp.int32, SC_LANES)

    def col_step(c, accs):
        off = pl.multiple_of(c * SC_LANES, SC_LANES)
        hv = h_v[slot, t, pl.ds(off, SC_LANES)]
        return tuple(accs[r] + rows_v[b, r, pl.ds(off, SC_LANES)] * hv for r in range(PEER_TOPK))

    accs = plsc.parallel_loop(
        0, D_MODEL // SC_LANES, unroll=PEER_SC_UNROLL,
        carry=tuple(jnp.zeros((SC_LANES,), jnp.float32) for _ in range(PEER_TOPK)))(col_step)
    res = jnp.zeros((SC_LANES,), jnp.float32)
    for r in range(PEER_TOPK):
        res = jnp.where(lane == r, jnp.sum(accs[r]), res)
    dots_v[t, pl.ds(hd * PEER_TOPK, PEER_TOPK)] = res


def _peer_mix_compute(slot, t, hd, b, act_v, rows_v, out_v):
    svec = jnp.full((SC_LANES,), slot, jnp.int32)
    tvec = jnp.full((SC_LANES,), t, jnp.int32)
    weights = [plsc.load_gather(act_v, [svec, tvec, jnp.full((SC_LANES,), hd * PEER_TOPK + r, jnp.int32)])
               for r in range(PEER_TOPK)]

    @plsc.parallel_loop(0, D_MODEL // SC_LANES, unroll=PEER_SC_UNROLL)
    def _(c):
        off = pl.multiple_of(c * SC_LANES, SC_LANES)
        a = rows_v[b, 0, pl.ds(off, SC_LANES)] * weights[0]
        for r in range(1, PEER_TOPK):
            a = a + rows_v[b, r, pl.ds(off, SC_LANES)] * weights[r]
        plsc.addupdate(out_v.at[t, pl.ds(off, SC_LANES)], a)


def _peer_expert_dots(h, experts, u_tab):
    T = h.shape[0]
    assert T % (SC_CORES * SC_SUBCORES * PEER_SC_TOKENS) == 0
    assert (PEER_SC_TOKENS * PEER_HEADS) % PEER_SC_RING == 0 and PEER_TOPK == SC_LANES
    body = partial(_peer_sc_body, _peer_dots_compute, False)
    return _peer_sc_call(body, T, PEER_SLOTS, D_MODEL)(h, experts, u_tab)


def _peer_expert_mix(act, experts, v_tab):
    T = act.shape[0]
    assert T % (SC_CORES * SC_SUBCORES * PEER_SC_TOKENS) == 0
    body = partial(_peer_sc_body, _peer_mix_compute, True)
    return _peer_sc_call(body, T, D_MODEL, PEER_SLOTS)(act, experts, v_tab)


PEER_TM = 256


def _extract_topk(s, k, payload=None):
    R = s.shape[0]
    riota = lax.broadcasted_iota(jnp.int32, s.shape, 0)
    vals, rows = [], []
    for _ in range(k):
        m = jnp.max(s, axis=0, keepdims=True)
        pos = jnp.min(jnp.where(s == m, riota, R), axis=0, keepdims=True)
        hit = riota == pos
        vals.append(m)
        rows.append(pos if payload is None else jnp.max(jnp.where(hit, payload, -1), axis=0, keepdims=True))
        s = jnp.where(hit, -jnp.inf, s)
    return jnp.concatenate(vals, axis=0), jnp.concatenate(rows, axis=0)


def _peer_route_kernel(x_ref, sc_ref, sh_ref, wq_ref, keys_ref, h_ref, ex_ref, gate_ref, q_ref, ext_ref, gt_ref):
    K = PEER_TOPK
    f32 = jnp.float32
    h = x_ref[0] * (1.0 + sc_ref[0]) + sh_ref[0]
    h_ref[0] = h
    q = jnp.dot(h.astype(wq_ref.dtype), wq_ref[...], preferred_element_type=f32)
    for hd in range(PEER_HEADS):
        q_ref[hd] = q[:, hd * PEER_KEY_DIM:(hd + 1) * PEER_KEY_DIM].astype(q_ref.dtype)

    def head_body(hd, carry):
        qh = q_ref[hd]
        s1 = lax.dot_general(keys_ref[hd, 0], qh[:, :PEER_HALF], _NT, preferred_element_type=f32)
        s2 = lax.dot_general(keys_ref[hd, 1], qh[:, PEER_HALF:], _NT, preferred_element_type=f32)
        v1, i1 = _extract_topk(s1, K)
        v2, i2 = _extract_topk(s2, K)
        tm = v1.shape[1]
        cand_rows, cidx_rows = [], []
        for a in range(K):
            nb = K // (a + 1)
            cand_rows.append(v1[a:a + 1, :] + v2[:nb, :])
            cidx_rows.append(i1[a:a + 1, :] * PEER_N_KEYS + i2[:nb, :])
        n_pad = -sum(r.shape[0] for r in cand_rows) % 8
        cand_rows.append(jnp.full((n_pad, tm), -jnp.inf, f32))
        cidx_rows.append(jnp.full((n_pad, tm), -1, jnp.int32))
        cand = jnp.concatenate(cand_rows, axis=0)
        cidx = jnp.concatenate(cidx_rows, axis=0)
        top_s, experts = _extract_topk(cand, K, payload=cidx)
        e = jnp.exp(top_s - top_s[0:1, :])
        gt_ref[pl.ds(pl.multiple_of(hd * K, K), K), :] = e / jnp.sum(e, axis=0, keepdims=True)
        ext_ref[pl.ds(pl.multiple_of(hd * K, K), K), :] = experts
        return carry

    lax.fori_loop(0, PEER_HEADS, head_body, 0)
    ex_ref[0] = ext_ref[...].T
    gate_ref[0] = gt_ref[...].T


def _peer_route(x, sc, sh, w_pq, sub_keys):
    B, S, D = x.shape
    tm = PEER_TM
    assert S % tm == 0
    return pl.pallas_call(
        _peer_route_kernel,
        grid=(B, S // tm),
        in_specs=[
            pl.BlockSpec((1, tm, D), lambda b, i: (b, i, 0)),
            pl.BlockSpec((1, 1, D), lambda b, i: (b, 0, 0)),
            pl.BlockSpec((1, 1, D), lambda b, i: (b, 0, 0)),
            pl.BlockSpec((D, PEER_HEADS * PEER_KEY_DIM), lambda b, i: (0, 0)),
            pl.BlockSpec((PEER_HEADS, 2, PEER_N_KEYS, PEER_HALF), lambda b, i: (0, 0, 0, 0)),
        ],
        out_specs=[
            pl.BlockSpec((1, tm, D), lambda b, i: (b, i, 0)),
            pl.BlockSpec((1, tm, PEER_SLOTS), lambda b, i: (b, i, 0)),
            pl.BlockSpec((1, tm, PEER_SLOTS), lambda b, i: (b, i, 0)),
        ],
        out_shape=[
            jax.ShapeDtypeStruct((B, S, D), jnp.float32),
            jax.ShapeDtypeStruct((B, S, PEER_SLOTS), jnp.int32),
            jax.ShapeDtypeStruct((B, S, PEER_SLOTS), jnp.float32),
        ],
        scratch_shapes=[
            pltpu.VMEM((PEER_HEADS, tm, PEER_KEY_DIM), MM_DTYPE),
            pltpu.VMEM((PEER_SLOTS, tm), jnp.int32),
            pltpu.VMEM((PEER_SLOTS, tm), jnp.float32),
        ],
        compiler_params=pltpu.CompilerParams(dimension_semantics=("parallel", "parallel")),
        name="peer_route",
    )(x, sc[:, None, :], sh[:, None, :], w_pq.astype(MM_DTYPE), sub_keys.astype(MM_DTYPE))


def _peer_pre(h2, experts, u_tab):
    B, S, D = h2.shape
    return _peer_expert_dots(h2.reshape(B * S, D), experts.reshape(B * S, PEER_SLOTS), u_tab)


def _peer_act_kernel(pre_ref, gate_ref, o_ref):
    pre = pre_ref[...]
    o_ref[...] = 0.5 * pre * (1.0 + lax.erf(pre * (2.0 ** -0.5))) * gate_ref[...]


def _peer_act(pre, gates):
    T, W = pre.shape
    tm = min(8 * ROW_TILE, T)
    assert T % tm == 0
    spec = pl.BlockSpec((tm, W), lambda i: (i, 0))
    return pl.pallas_call(
        _peer_act_kernel,
        grid=(T // tm,),
        in_specs=[spec, spec],
        out_specs=spec,
        out_shape=jax.ShapeDtypeStruct((T, W), jnp.float32),
        name="peer_act",
    )(pre, gates.reshape(T, W))


def _peer_out(act, experts, v_tab):
    B, S, _ = experts.shape
    return _peer_expert_mix(act, experts.reshape(B * S, PEER_SLOTS), v_tab).reshape(B, S, D_MODEL)


def kernel(x, c, w_ada, b_ada, w_in, rw_mu, rw_w0, rw_w2, rw_a0, rw_a2, rw_g2, rw_k_k, rw_k_a, rw_r_k, rw_gn_g, rw_gn_b, dsa_kv_g, idx_k_g, idx_k_b, rel_bias, w_br_a, w_br_b, w_out, ln1_g, ln1_b, peer_wq, peer_keys, peer_u, peer_v, ln2_g, ln2_b):
    l = 0
    mod = jax.nn.silu(c) @ w_ada[l] + b_ada[l]

    w_rw, w_q, w_kv, w_qi, w_ki, w_wi, w_ga, w_gb = jnp.split(w_in[l], _split_points(IN_SIZES), axis=-1)
    small_pad = jnp.zeros((D_MODEL, SMALL_COLS - SMALL_WI - IDX_HEADS), w_in.dtype)
    w_small = jnp.concatenate([w_qi, w_kv, w_ki, w_wi, small_pad], axis=-1)
    w_gates = jnp.concatenate([w_ga, w_gb], axis=-1)

    B, S, _ = x.shape
    seg_len = S // SEQ_SEGMENTS
    dsa_topk = min(TOPK_MAX, S // 4)
    bias_tiles = _dsa_bias_tiles(rel_bias, DSA_TQ)

    def project_and_dsa(x, sc1, sh1, seg, keys_so_far):
        z_rw = _mod_matmul(x, sc1, sh1, w_rw, jnp.float32, tn=896)
        z_q = _mod_matmul(x, sc1, sh1, w_q, MM_DTYPE, tn=1024)
        z_small = _mod_matmul(x, sc1, sh1, w_small, jnp.float32, tn=512)
        z_g = _mod_matmul(x, sc1, sh1, w_gates, jnp.float32, tn=1024)
        z_wi_t = jnp.swapaxes(z_small[..., SMALL_WI:SMALL_WI + IDX_HEADS], 1, 2)
        qi, ckv, kidx = _dsa_prep(z_small, dsa_kv_g[l], idx_k_g[l], idx_k_b[l])
        if keys_so_far is not None:
            ckv = jnp.concatenate([keys_so_far[0], ckv], axis=1)
            kidx = jnp.concatenate([keys_so_far[1], kidx], axis=1)
        y_b = _dsa_attention(z_q, qi, z_wi_t, ckv, kidx, bias_tiles, dsa_topk,
                             seg * (seg_len // DSA_TQ), MM_DTYPE)
        return (y_b, z_rw, z_g), (ckv, kidx)

    def rwkv_and_merge(x, gt1, carry, y_b, z_rw, z_g):
        y_a, carry = _rwkv7_time_mix(z_rw, carry, rw_mu[l], rw_w0[l], rw_w2[l], rw_a0[l], rw_a2[l], rw_g2[l],
                                     rw_k_k[l], rw_k_a[l], rw_r_k[l], rw_gn_g[l], rw_gn_b[l], MM_DTYPE)
        return _merge_ln(x, gt1, y_a, y_b, z_g, w_br_a[l], w_br_b[l], w_out[l], ln1_g[l], ln1_b[l]), carry

    def finish(st, act):
        y2 = _peer_out(act, st["experts"], peer_v[l])
        return _residual_ln(st["x1"], st["gt2"], y2, ln2_g[l], ln2_b[l])

    assert B % BATCH_GROUPS == 0 and S % (SEQ_SEGMENTS * DSA_TQ) == 0
    gsz = B // BATCH_GROUPS
    outs = []
    prev = None
    for g in range(BATCH_GROUPS):
        sh1, sc1, gt1, sh2, sc2, gt2 = jnp.split(mod[g * gsz:(g + 1) * gsz], 6, axis=-1)
        rw_carry = _rwkv7_init_carry(gsz)
        keys_so_far = None
        for seg in range(SEQ_SEGMENTS):
            x_g = x[g * gsz:(g + 1) * gsz, seg * seg_len:(seg + 1) * seg_len]
            if prev is not None:
                x_g, prev["experts"] = lax.optimization_barrier((x_g, prev["experts"]))
            proj, keys_so_far = project_and_dsa(x_g, sc1, sh1, seg, keys_so_far)
            if prev is not None:
                proj, prev["pre"], outs = lax.optimization_barrier((proj, prev["pre"], outs))
                outs.append(finish(prev, _peer_act(prev["pre"], prev["gates"])))
            x1, rw_carry = rwkv_and_merge(x_g, gt1, rw_carry, *proj)
            h2, experts, gates = _peer_route(x1, sc2, sh2, peer_wq[l], peer_keys[l])
            prev = dict(x1=x1, gt2=gt2, experts=experts, gates=gates, pre=_peer_pre(h2, experts, peer_u[l]))
    prev["pre"], outs = lax.optimization_barrier((prev["pre"], outs))
    outs.append(finish(prev, _peer_act(prev["pre"], prev["gates"])))
    rows = [jnp.concatenate(outs[g * SEQ_SEGMENTS:(g + 1) * SEQ_SEGMENTS], axis=1) for g in range(BATCH_GROUPS)]
    return jnp.concatenate(rows, axis=0)
```

```python
import math
from functools import partial

import jax
import jax.numpy as jnp
import numpy as np
from jax import lax
from jax.experimental import pallas as pl
from jax.experimental.pallas import tpu as pltpu
from jax.experimental.pallas import tpu_sc as plsc

D_MODEL = 1024
RW_HEADS = 8
RW_HEAD_DIM = 64
RW_DIM = 512
RW_DECAY_LORA = 64
RW_A_LORA = 64
RW_GATE_LORA = 128
RW_COLS = 3 * RW_DIM + RW_DECAY_LORA + RW_A_LORA + RW_GATE_LORA
RW_GN_EPS = 64e-5
DSA_HEADS = 8
DSA_LATENT = 128
DSA_Q_DIM = DSA_HEADS * DSA_LATENT
IDX_HEADS = 4
IDX_DIM = 64
TOPK_MAX = 256
REL_BUCKETS = 32
REL_MAX_DIST = 128
IN_SIZES = (RW_COLS, DSA_Q_DIM, DSA_LATENT, IDX_HEADS * IDX_DIM, IDX_DIM, IDX_HEADS, D_MODEL, D_MODEL)
IN_COLS = sum(IN_SIZES)
PEER_HEADS = 8
PEER_N_KEYS = 128
PEER_KEY_DIM = 128
PEER_HALF = 64
PEER_TOPK = 16
LN_EPS = 1e-5
DEPTH = 1
DEEPNORM_ALPHA = (2.0 * DEPTH) ** 0.25

LANES = 128
SMALL_QI = 0
SMALL_KV = SMALL_QI + IDX_HEADS * IDX_DIM
SMALL_KI = SMALL_KV + DSA_LATENT
SMALL_WI = SMALL_KI + IDX_DIM
SMALL_COLS = 4 * LANES
MM_DTYPE = jnp.bfloat16
BATCH_GROUPS = 8
SEQ_SEGMENTS = 2


def _split_points(sizes):
    return np.cumsum(sizes)[:-1].tolist()


def _mod_matmul_kernel(x_ref, sc_ref, sh_ref, w_ref, o_ref):
    h = x_ref[0] * (1.0 + sc_ref[0]) + sh_ref[0]
    o_ref[0] = jnp.dot(h.astype(w_ref.dtype), w_ref[...],
                       preferred_element_type=jnp.float32).astype(o_ref.dtype)


def _mod_matmul(x, sc, sh, w, out_dtype, tn, tm=512):
    B, S, D = x.shape
    N = w.shape[1]
    assert S % tm == 0 and N % tn == 0
    return pl.pallas_call(
        _mod_matmul_kernel,
        grid=(B, S // tm, N // tn),
        in_specs=[
            pl.BlockSpec((1, tm, D), lambda b, i, j: (b, i, 0)),
            pl.BlockSpec((1, 1, D), lambda b, i, j: (b, 0, 0)),
            pl.BlockSpec((1, 1, D), lambda b, i, j: (b, 0, 0)),
            pl.BlockSpec((D, tn), lambda b, i, j: (0, j)),
        ],
        out_specs=pl.BlockSpec((1, tm, tn), lambda b, i, j: (b, i, j)),
        out_shape=jax.ShapeDtypeStruct((B, S, N), out_dtype),
        name="mod_matmul",
    )(x, sc[:, None, :], sh[:, None, :], w.astype(MM_DTYPE))


ROW_TILE = 256


def _layer_norm(x, g, b):
    mu = jnp.mean(x, -1, keepdims=True)
    var = jnp.mean(jnp.square(x - mu), -1, keepdims=True)
    return (x - mu) * lax.rsqrt(var + LN_EPS) * g + b


def _merge_ln_kernel(x_ref, gt_ref, ya_ref, yb_ref, zg_ref, wa_ref, wb_ref, wo_ref, g_ref, b_ref, o_ref):
    f32 = jnp.float32
    D = x_ref.shape[2]
    ya = jnp.dot(ya_ref[0], wa_ref[...], preferred_element_type=f32)
    yb = jnp.dot(yb_ref[0], wb_ref[...], preferred_element_type=f32)
    zg = zg_ref[0]
    merged = jax.nn.sigmoid(zg[:, :D]) * ya + jax.nn.sigmoid(zg[:, D:]) * yb
    mix = jnp.dot(merged.astype(wo_ref.dtype), wo_ref[...], preferred_element_type=f32)
    o_ref[0] = _layer_norm(DEEPNORM_ALPHA * x_ref[0] + gt_ref[0] * mix, g_ref[...], b_ref[...])


def _merge_ln(x, gt, y_a, y_b, z_g, w_a, w_b, w_o, ln_g, ln_b):
    B, S, D = x.shape
    tm = ROW_TILE
    assert S % tm == 0
    tok = lambda width: pl.BlockSpec((1, tm, width), lambda b, i: (b, i, 0))
    full = lambda a: pl.BlockSpec(a.shape, lambda b, i: (0,) * a.ndim)
    w_a, w_b, w_o = (w.astype(MM_DTYPE) for w in (w_a, w_b, w_o))
    ln_g, ln_b = ln_g[None], ln_b[None]
    return pl.pallas_call(
        _merge_ln_kernel,
        grid=(B, S // tm),
        in_specs=[tok(D), pl.BlockSpec((1, 1, D), lambda b, i: (b, 0, 0)), tok(y_a.shape[2]), tok(y_b.shape[2]),
                  tok(z_g.shape[2]), full(w_a), full(w_b), full(w_o), full(ln_g), full(ln_b)],
        out_specs=tok(D),
        out_shape=jax.ShapeDtypeStruct((B, S, D), jnp.float32),
        compiler_params=pltpu.CompilerParams(dimension_semantics=("parallel", "parallel")),
        name="merge_ln",
    )(x, gt[:, None, :], y_a, y_b, z_g, w_a, w_b, w_o, ln_g, ln_b)


def _residual_ln_kernel(x_ref, gt_ref, y_ref, g_ref, b_ref, o_ref):
    o_ref[0] = _layer_norm(DEEPNORM_ALPHA * x_ref[0] + gt_ref[0] * y_ref[0], g_ref[...], b_ref[...])


def _residual_ln(x, gt, y, ln_g, ln_b):
    B, S, D = x.shape
    tm = ROW_TILE
    assert S % tm == 0
    tok = pl.BlockSpec((1, tm, D), lambda b, i: (b, i, 0))
    vec = pl.BlockSpec((1, D), lambda b, i: (0, 0))
    return pl.pallas_call(
        _residual_ln_kernel,
        grid=(B, S // tm),
        in_specs=[tok, pl.BlockSpec((1, 1, D), lambda b, i: (b, 0, 0)), tok, vec, vec],
        out_specs=tok,
        out_shape=jax.ShapeDtypeStruct((B, S, D), jnp.float32),
        compiler_params=pltpu.CompilerParams(dimension_semantics=("parallel", "parallel")),
        name="residual_ln",
    )(x, gt[:, None, :], y, ln_g[None], ln_b[None])


def _t5_bucket(n):
    n = jnp.maximum(n, 0)
    max_exact = REL_BUCKETS // 2
    nf = jnp.maximum(n, 1).astype(jnp.float32)
    large = max_exact + (jnp.log(nf / max_exact) / math.log(REL_MAX_DIST / max_exact)
                         * (REL_BUCKETS - max_exact)).astype(jnp.int32)
    large = jnp.minimum(large, REL_BUCKETS - 1)
    return jnp.where(n < max_exact, n, large)


RW_CHUNK = 64
RW_INV_BLOCK = 16
_NN = (((1,), (0,)), ((), ()))
_NT = (((1,), (1,)), ((), ()))
_BNN = (((2,), (1,)), ((0,), (0,)))
_BNT = (((2,), (2,)), ((0,), (0,)))


def _dot_f32(a, b, dims=_NN):
    return lax.dot_general(a, b, dims, precision=lax.Precision.HIGHEST,
                           preferred_element_type=jnp.float32)


def _dot_bf16x3(a, b, dims=_NN):
    f32, bf = jnp.float32, jnp.bfloat16
    a_hi, b_hi = a.astype(bf), b.astype(bf)
    a_lo = (a - a_hi.astype(f32)).astype(bf)
    b_lo = (b - b_hi.astype(f32)).astype(bf)
    out = lax.dot_general(a_hi, b_hi, dims, preferred_element_type=f32)
    out = out + lax.dot_general(a_hi, b_lo, dims, preferred_element_type=f32)
    return out + lax.dot_general(a_lo, b_hi, dims, preferred_element_type=f32)


def _bf16_terms(a):
    f32, bf = jnp.float32, jnp.bfloat16
    hi = a.astype(bf)
    r1 = a - hi.astype(f32)
    mid = r1.astype(bf)
    lo = (r1 - mid.astype(f32)).astype(bf)
    return hi, mid, lo


def _dot_lhs_split(a, b01):
    b = b01.astype(jnp.bfloat16)
    return sum(jnp.dot(t, b, preferred_element_type=jnp.float32) for t in _bf16_terms(a))


def _dot_rhs_split(a01, b):
    a = a01.astype(jnp.bfloat16)
    return sum(jnp.dot(a, t, preferred_element_type=jnp.float32) for t in _bf16_terms(b))


def _rwkv_kernel(z_ref, m0_ref, p0_ref, mu_ref, w0_ref, w2_ref, a0_ref, a2_ref, g2_ref, kk_ref, ka_ref, rk_ref,
                 gng_ref, gnb_ref, bd_ref, o_ref, mo_ref, po_ref, m_ref, prev_ref, y_ref):
    C = z_ref.shape[1]
    N = RW_HEAD_DIM
    f32 = jnp.float32
    dot3 = _dot_bf16x3

    @pl.when(pl.program_id(1) == 0)
    def _():
        m_ref[...] = m0_ref[0]
        prev_ref[...] = p0_ref[0]

    z = z_ref[0]
    row = lax.broadcasted_iota(jnp.int32, z.shape, 0)
    shifted = jnp.where(row == 0, prev_ref[...], pltpu.roll(z, 1, axis=0))
    prev_ref[...] = z[C - 1:C, :]
    zs = z + (shifted - z) * mu_ref[...]
    r = zs[:, 0:RW_DIM]
    k = zs[:, RW_DIM:2 * RW_DIM]
    v = zs[:, 2 * RW_DIM:3 * RW_DIM]
    o1 = 3 * RW_DIM
    wl = zs[:, o1:o1 + RW_DECAY_LORA]
    al = zs[:, o1 + RW_DECAY_LORA:o1 + RW_DECAY_LORA + RW_A_LORA]
    gl = zs[:, o1 + RW_DECAY_LORA + RW_A_LORA:]

    bd = bd_ref[...]
    log_w = -jax.nn.softplus(-(w0_ref[...] + dot3(jnp.tanh(wl), w2_ref[...]))) - 0.5
    ldec = -jnp.exp(log_w)
    a_lr = jax.nn.sigmoid(a0_ref[...] + dot3(al, a2_ref[...]))
    g = dot3(jax.nn.sigmoid(gl), g2_ref[...])
    kk = k * kk_ref[...]
    kk = kk * lax.rsqrt(jnp.maximum(_dot_lhs_split(kk * kk, bd), 1e-24))
    k2 = k * (1.0 + (a_lr - 1.0) * ka_ref[...])
    a_vec = -kk
    b_vec = kk * a_lr

    ti = lax.broadcasted_iota(jnp.int32, (C, C), 0)
    tj = lax.broadcasted_iota(jnp.int32, (C, C), 1)
    cum = _dot_rhs_split((ti >= tj).astype(f32), ldec)
    cum_last = cum[C - 1:C, :]
    w_incl = jnp.exp(cum)
    w_excl = jnp.exp(cum - ldec)
    w_inv = jnp.exp(-cum)
    w_end = jnp.exp(cum_last - cum)
    w_all = jnp.exp(cum_last)
    a_t = a_vec * w_excl
    r_t = r * w_incl
    b_t = b_vec * w_inv
    k_t = k2 * w_inv
    b_e = b_vec * w_end
    k_e = k2 * w_end

    strict = ti > tj
    incl = ti >= tj
    bi, bj = ti // RW_INV_BLOCK, tj // RW_INV_BLOCK
    same_blk = bi == bj
    pair_blk = jnp.logical_and((bi // 2) == (bj // 2), jnp.logical_not(same_blk))
    half_blk = (bi // 2) != (bj // 2)
    eye = (ti == tj).astype(f32)

    H = RW_HEADS
    heads = lambda x: jnp.stack([x[:, h * N:(h + 1) * N] for h in range(H)], axis=0)
    bmm = lambda a, b: dot3(a, b, _BNN)
    A, Rt, Bt, Kt, Be, Ke, V = (heads(t) for t in (a_t, r_t, b_t, k_t, b_e, k_e, v))
    gm = dot3(jnp.concatenate([A, Rt], axis=1), jnp.concatenate([Bt, Kt], axis=1), _BNT)
    a_ab = jnp.where(strict, gm[:, :C, :C], 0.0)
    a_ak = jnp.where(strict, gm[:, :C, C:], 0.0)
    a_rb = jnp.where(incl, gm[:, C:, :C], 0.0)
    a_rk = jnp.where(incl, gm[:, C:, C:], 0.0)
    d1 = jnp.where(same_blk, a_ab, 0.0)
    xinv = eye + d1
    d2 = bmm(d1, d1)
    xinv = xinv + bmm(xinv, d2)
    d4 = bmm(d2, d2)
    xinv = xinv + bmm(xinv, d4)
    d8 = bmm(d4, d4)
    xinv = xinv + bmm(xinv, d8)
    xinv = xinv + bmm(bmm(xinv, jnp.where(pair_blk, a_ab, 0.0)), xinv)
    xinv = xinv + bmm(bmm(xinv, jnp.where(half_blk, a_ab, 0.0)), xinv)

    av = bmm(jnp.concatenate([a_ak, a_rk], axis=1), V)
    p = bmm(xinv, jnp.concatenate([A, av[:, :C]], axis=2))
    qm = bmm(a_rb, p)
    q1 = Rt + qm[:, :, :N]
    q2 = qm[:, :, N:] + av[:, C:]
    gmat = bmm(jnp.swapaxes(Be, 1, 2), p)
    g1 = eye * heads(w_all) + gmat[:, :, :N]
    g2 = gmat[:, :, N:] + bmm(jnp.swapaxes(Ke, 1, 2), V)
    m = m_ref[...]
    yh = _dot_f32(q1, m, _BNN) + q2
    m_ref[...] = _dot_f32(g1, m, _BNN) + g2
    for h in range(H):
        y_ref[:, h * N:(h + 1) * N] = yh[h]

    y = y_ref[...]
    mean = _dot_lhs_split(y, bd) * (1.0 / N)
    yc = y - mean
    var = _dot_lhs_split(yc * yc, bd) * (1.0 / N)
    yn = yc * lax.rsqrt(var + RW_GN_EPS) * gng_ref[...] + gnb_ref[...]
    bonus = _dot_lhs_split(r * k2 * rk_ref[...], bd) * v
    o_ref[0] = ((yn + bonus) * g).astype(o_ref.dtype)

    @pl.when(pl.program_id(1) == pl.num_programs(1) - 1)
    def _():
        mo_ref[0] = m_ref[...]
        po_ref[0] = prev_ref[...]


def _rwkv7_init_carry(batch):
    return (jnp.zeros((batch, RW_HEADS, RW_HEAD_DIM, RW_HEAD_DIM), jnp.float32),
            jnp.zeros((batch, 1, RW_COLS), jnp.float32))


def _rwkv7_time_mix(z_rw, carry, mu, w0, w2, a0, a2, g2, k_k, k_a, r_k, gn_g, gn_b, out_dtype):
    B, S, _ = z_rw.shape
    C = RW_CHUNK
    assert S % C == 0 and C == RW_HEAD_DIM and C % (4 * RW_INV_BLOCK) == 0
    hid = jnp.arange(RW_DIM) // RW_HEAD_DIM
    bd = (hid[:, None] == hid[None, :]).astype(jnp.float32)
    row = lambda a: a.reshape(1, -1)
    full = lambda shape: pl.BlockSpec(shape, lambda b, c: (0,) * len(shape))
    state_spec = pl.BlockSpec((1, RW_HEADS, RW_HEAD_DIM, RW_HEAD_DIM), lambda b, c: (b, 0, 0, 0))
    prev_spec = pl.BlockSpec((1, 1, RW_COLS), lambda b, c: (b, 0, 0))
    y, m_out, p_out = pl.pallas_call(
        _rwkv_kernel,
        grid=(B, S // C),
        in_specs=[
            pl.BlockSpec((1, C, RW_COLS), lambda b, c: (b, c, 0)), state_spec, prev_spec,
            full((1, RW_COLS)), full((1, RW_DIM)), full((RW_DECAY_LORA, RW_DIM)), full((1, RW_DIM)),
            full((RW_A_LORA, RW_DIM)), full((RW_GATE_LORA, RW_DIM)), full((1, RW_DIM)), full((1, RW_DIM)),
            full((1, RW_DIM)), full((1, RW_DIM)), full((1, RW_DIM)), full((RW_DIM, RW_DIM)),
        ],
        out_specs=[pl.BlockSpec((1, C, RW_DIM), lambda b, c: (b, c, 0)), state_spec, prev_spec],
        out_shape=[jax.ShapeDtypeStruct((B, S, RW_DIM), out_dtype),
                   jax.ShapeDtypeStruct(carry[0].shape, jnp.float32),
                   jax.ShapeDtypeStruct(carry[1].shape, jnp.float32)],
        scratch_shapes=[
            pltpu.VMEM((RW_HEADS, RW_HEAD_DIM, RW_HEAD_DIM), jnp.float32),
            pltpu.VMEM((1, RW_COLS), jnp.float32),
            pltpu.VMEM((C, RW_DIM), jnp.float32),
        ],
        compiler_params=pltpu.CompilerParams(dimension_semantics=("parallel", "arbitrary")),
        name="rwkv7_time_mix",
    )(z_rw, carry[0], carry[1], row(mu), row(w0), w2, row(a0), a2, g2, row(k_k), row(k_a), row(r_k),
      row(gn_g), row(gn_b), bd)
    return y, (m_out, p_out)


DSA_TQ = 256
MASK_NEG = -1e30
INT_MIN = -2 ** 31
KEY_NEG_INF = -2139095041
THRESH_BITS = 32


def _dsa_prep_kernel(z_ref, kvg_ref, kig_ref, kib_ref, qi_ref, kv_ref, ki_ref):
    z = z_ref[0]
    qi_ref[0] = z[:, SMALL_QI:SMALL_KV].astype(qi_ref.dtype)
    kv = z[:, SMALL_KV:SMALL_KI]
    ms = jnp.mean(jnp.square(kv), -1, keepdims=True)
    kv_ref[0] = (kv * lax.rsqrt(ms + LN_EPS) * kvg_ref[...]).astype(kv_ref.dtype)
    ki = z[:, SMALL_KI:SMALL_WI]
    mu = jnp.mean(ki, -1, keepdims=True)
    var = jnp.mean(jnp.square(ki - mu), -1, keepdims=True)
    ki_ref[0] = ((ki - mu) * lax.rsqrt(var + LN_EPS) * kig_ref[...] + kib_ref[...]).astype(ki_ref.dtype)


def _dsa_prep(z_small, kv_g, ki_g, ki_b, tm=512):
    B, S, W = z_small.shape
    assert S % tm == 0
    return pl.pallas_call(
        _dsa_prep_kernel,
        grid=(B, S // tm),
        in_specs=[
            pl.BlockSpec((1, tm, W), lambda b, i: (b, i, 0)),
            pl.BlockSpec((1, DSA_LATENT), lambda b, i: (0, 0)),
            pl.BlockSpec((1, IDX_DIM), lambda b, i: (0, 0)),
            pl.BlockSpec((1, IDX_DIM), lambda b, i: (0, 0)),
        ],
        out_specs=[
            pl.BlockSpec((1, tm, IDX_HEADS * IDX_DIM), lambda b, i: (b, i, 0)),
            pl.BlockSpec((1, tm, DSA_LATENT), lambda b, i: (b, i, 0)),
            pl.BlockSpec((1, tm, IDX_DIM), lambda b, i: (b, i, 0)),
        ],
        out_shape=[
            jax.ShapeDtypeStruct((B, S, IDX_HEADS * IDX_DIM), MM_DTYPE),
            jax.ShapeDtypeStruct((B, S, DSA_LATENT), MM_DTYPE),
            jax.ShapeDtypeStruct((B, S, IDX_DIM), MM_DTYPE),
        ],
        name="dsa_prep",
    )(z_small, kv_g[None], ki_g[None], ki_b[None])


def _sortable_key(s):
    s = jnp.where(s == 0.0, 0.0, s)
    bits = pltpu.bitcast(s, jnp.int32)
    return bits ^ ((bits >> 31) & 0x7FFFFFFF)


def _col_count(mask_i32):
    tk, tq = mask_i32.shape
    return jnp.sum(mask_i32.reshape(tk // 8, 8, tq), axis=0)


def _dsa_kernel(bfar_ref, q_ref, qi_ref, wit_ref, kv_ref, kvt_ref, ki_ref, b0_ref, b1_ref, o_ref,
                key_ref, madd_ref, m_ref, l_ref, acc_ref, *, topk, q_off):
    tq = q_ref.shape[1]
    tk = tq
    i = pl.program_id(1) + q_off
    nj = i + 1
    f32 = jnp.float32
    krow = lax.broadcasted_iota(jnp.int32, (tk, tq), 0)
    qcol = lax.broadcasted_iota(jnp.int32, (tk, tq), 1)

    qi = qi_ref[0]
    wit = wit_ref[0] * (IDX_HEADS ** -0.5)

    def score_chunk(j, carry):
        off = pl.multiple_of(j * tk, tk)
        kc = ki_ref[0, pl.ds(off, tk), :]
        s = jnp.zeros((tk, tq), f32)
        for h in range(IDX_HEADS):
            d = lax.dot_general(kc, qi[:, h * IDX_DIM:(h + 1) * IDX_DIM],
                                (((1,), (1,)), ((), ())), preferred_element_type=f32)
            s = s + wit[h:h + 1, :] * jnp.maximum(d * (IDX_DIM ** -0.5), 0.0)
        causal = (krow + j * tk) <= (qcol + i * tq)
        s = jnp.where(causal, s, -jnp.inf)
        key_ref[j] = _sortable_key(s)
        return carry

    lax.fori_loop(0, nj, score_chunk, 0)

    def count_where(pred_fn):
        def body(j, acc):
            return acc + _col_count(pred_fn(key_ref[j], j).astype(jnp.int32))
        acc = lax.fori_loop(0, nj, body, jnp.zeros((8, tq), jnp.int32))
        return jnp.sum(acc, axis=0, keepdims=True)

    def bit_step(it, t_u):
        bit = THRESH_BITS - 1 - it
        cand_u = t_u | jnp.left_shift(jnp.int32(1), bit)
        cand = cand_u ^ INT_MIN
        cnt = count_where(lambda k, j: k >= cand)
        return jnp.where(cnt >= topk, cand_u, t_u)

    t_u = lax.fori_loop(0, THRESH_BITS, bit_step, jnp.zeros((1, tq), jnp.int32))
    thr = t_u ^ INT_MIN
    cnt_gt = count_where(lambda k, j: k > thr)
    cnt_ge = count_where(lambda k, j: k >= thr)
    is_neg = thr == KEY_NEG_INF
    need = jnp.logical_and(cnt_ge > topk, jnp.logical_not(is_neg))
    n_tie_take = topk - cnt_gt

    thr_open = jnp.where(is_neg, thr, thr - 1)
    any_need = jnp.max(need.astype(jnp.int32)) > 0

    @pl.when(jnp.logical_not(any_need))
    def _():
        def body(j, carry):
            madd_ref[j] = jnp.where(key_ref[j] > thr_open, 0.0, MASK_NEG)
            return carry
        lax.fori_loop(0, nj, body, 0)

    @pl.when(any_need)
    def _():
        s_len = tk * key_ref.shape[0]
        n_bits = max(1, int(math.ceil(math.log2(s_len))))

        def idx_step(it, p):
            bit = n_bits - 1 - it
            cand = p | jnp.left_shift(jnp.int32(1), bit)
            cnt = count_where(
                lambda k, j: jnp.where(k == thr, jnp.where((krow + j * tk) < cand, 1, 0), 0))
            return jnp.where(cnt < n_tie_take, cand, p)

        p_idx = lax.fori_loop(0, n_bits, idx_step, jnp.zeros((1, tq), jnp.int32))
        p_idx = jnp.where(need, p_idx, jnp.where(is_neg, -1, s_len))

        def body(j, carry):
            k = key_ref[j]
            tie_ok = jnp.where((krow + j * tk) <= p_idx, 0.0, MASK_NEG)
            madd_ref[j] = jnp.where(k > thr, 0.0, jnp.where(k == thr, tie_ok, MASK_NEG))
            return carry
        lax.fori_loop(0, nj, body, 0)

    H, d = DSA_HEADS, DSA_LATENT
    m_ref[...] = jnp.full(m_ref.shape, MASK_NEG, f32)
    l_ref[...] = jnp.zeros(l_ref.shape, f32)
    acc_ref[...] = jnp.zeros(acc_ref.shape, f32)
    scale = DSA_LATENT ** -0.5
    q_all = jnp.concatenate([q_ref[0, :, h * d:(h + 1) * d] for h in range(H)], axis=0)

    def attend(j, bias):
        off = pl.multiple_of(j * tk, tk)
        kc = kv_ref[0, pl.ds(off, tk), :]
        kct = kvt_ref[0, :, pl.ds(off, tk)]
        lg = lax.dot_general(kc, q_all, (((1,), (1,)), ((), ())), preferred_element_type=f32)
        lg = lg * scale + bias + jnp.tile(madd_ref[j], (1, H))
        m_old = m_ref[...]
        m_new = jnp.maximum(m_old, jnp.max(lg, axis=0, keepdims=True))
        alpha = jnp.exp(m_old - m_new)
        p = jnp.exp(lg - m_new)
        l_ref[...] = alpha * l_ref[...] + jnp.sum(p, axis=0, keepdims=True)
        acc_ref[...] = alpha * acc_ref[...] + jnp.dot(kct, p.astype(kct.dtype), preferred_element_type=f32)
        m_ref[...] = m_new

    def far_body(j, carry):
        attend(j, bfar_ref[...])
        return carry

    lax.fori_loop(0, jnp.maximum(i - 1, 0), far_body, 0)

    @pl.when(i >= 1)
    def _():
        attend(i - 1, b1_ref[...])

    attend(i, b0_ref[...])

    out_t = acc_ref[...] / l_ref[...]
    for h in range(H):
        o_ref[0, :, h * d:(h + 1) * d] = out_t[:, h * tq:(h + 1) * tq].T.astype(o_ref.dtype)


def _dsa_bias_tiles(rel_bias, tq):
    dist = jnp.arange(2 * tq, dtype=jnp.int32)
    by_dist = rel_bias[_t5_bucket(dist)].T

    def toeplitz(v):
        H, P = v.shape
        skew = jnp.tile(v, (1, tq + 1))[:, :tq * (P + 1)].reshape(H, tq, P + 1)
        return skew[:, ::-1, :tq]

    b0 = toeplitz(jnp.concatenate([jnp.repeat(by_dist[:, :1], tq - 1, axis=1), by_dist[:, :tq + 1]], axis=1))
    b1 = toeplitz(jnp.concatenate([by_dist[:, 1:], by_dist[:, -1:]], axis=1))
    H = by_dist.shape[0]
    b0 = jnp.transpose(b0, (1, 0, 2)).reshape(tq, H * tq)
    b1 = jnp.transpose(b1, (1, 0, 2)).reshape(tq, H * tq)
    return b0, b1, jnp.repeat(by_dist[:, -1], tq)[None, :]


def _dsa_attention(q, qi, z_wi_t, ckv, kidx, bias_tiles, topk, q_off, out_dtype):
    B, Sq, _ = q.shape
    tq = DSA_TQ
    nq = Sq // tq
    S = ckv.shape[1]
    assert Sq % tq == 0 and S == (q_off + nq) * tq and tq >= REL_MAX_DIST and topk <= tq
    ckv_t = jnp.swapaxes(ckv, 1, 2)
    b0, b1, bfar = bias_tiles
    nk = S // tq
    H, d = DSA_HEADS, DSA_LATENT
    f32_bytes = 4
    vmem_estimate = (
        2 * nk * tq * tq * f32_bytes
        + 2 * 2 * tq * H * tq * f32_bytes
        + 2 * (2 * S * d + S * LANES) * 2
        + 6 * tq * H * tq * f32_bytes
        + 4 * tq * H * d * f32_bytes
    )
    grid_spec = pltpu.PrefetchScalarGridSpec(
        num_scalar_prefetch=0,
        grid=(B, nq),
        in_specs=[
            pl.BlockSpec((1, H * tq), lambda b, i: (0, 0)),
            pl.BlockSpec((1, tq, H * d), lambda b, i: (b, i, 0)),
            pl.BlockSpec((1, tq, IDX_HEADS * IDX_DIM), lambda b, i: (b, i, 0)),
            pl.BlockSpec((1, IDX_HEADS, tq), lambda b, i: (b, 0, i)),
            pl.BlockSpec((1, S, d), lambda b, i: (b, 0, 0)),
            pl.BlockSpec((1, d, S), lambda b, i: (b, 0, 0)),
            pl.BlockSpec((1, S, IDX_DIM), lambda b, i: (b, 0, 0)),
            pl.BlockSpec((tq, H * tq), lambda b, i: (0, 0)),
            pl.BlockSpec((tq, H * tq), lambda b, i: (0, 0)),
        ],
        out_specs=pl.BlockSpec((1, tq, H * d), lambda b, i: (b, i, 0)),
        scratch_shapes=[
            pltpu.VMEM((nk, tq, tq), jnp.int32),
            pltpu.VMEM((nk, tq, tq), jnp.float32),
            pltpu.VMEM((1, H * tq), jnp.float32),
            pltpu.VMEM((1, H * tq), jnp.float32),
            pltpu.VMEM((d, H * tq), jnp.float32),
        ],
    )
    return pl.pallas_call(
        partial(_dsa_kernel, topk=topk, q_off=q_off),
        grid_spec=grid_spec,
        out_shape=jax.ShapeDtypeStruct((B, Sq, H * d), out_dtype),
        compiler_params=pltpu.CompilerParams(vmem_limit_bytes=vmem_estimate),
        name="dsa_attention",
    )(bfar, q, qi, z_wi_t, ckv, ckv_t, kidx, b0, b1)


SC_CORES = 2
SC_SUBCORES = 16
SC_LANES = 16
PEER_SC_TOKENS = 8
PEER_SC_RING = 4
PEER_SC_UNROLL = 4
PEER_SLOTS = PEER_HEADS * PEER_TOPK


def _peer_sc_call(body, T, out_width, stage_width):
    mesh = plsc.VectorSubcoreMesh(core_axis_name="c", subcore_axis_name="s")
    return pl.kernel(
        body, mesh=mesh,
        out_type=jax.ShapeDtypeStruct((T, out_width), jnp.float32),
        scratch_types=[
            pltpu.VMEM((2, PEER_SC_TOKENS, PEER_SLOTS), jnp.int32),
            pltpu.VMEM((2, PEER_SC_TOKENS, stage_width), jnp.float32),
            pltpu.VMEM((PEER_SC_RING, PEER_TOPK, D_MODEL), jnp.float32),
            pltpu.VMEM((PEER_SC_TOKENS, out_width), jnp.float32),
            pltpu.SemaphoreType.DMA((PEER_SC_RING + 2,)),
        ],
        compiler_params=pltpu.CompilerParams(needs_layout_passes=False),
    )


def _peer_sc_body(compute, zero_out, x_hbm, idx_hbm, tab_hbm, out_hbm, idx_v, x_v, rows_v, out_v, sems):
    T = idx_hbm.shape[0]
    tokens_per_worker = T // (SC_CORES * SC_SUBCORES)
    n_blocks = tokens_per_worker // PEER_SC_TOKENS
    n_steps = PEER_SC_TOKENS * PEER_HEADS
    ring = PEER_SC_RING
    worker = lax.axis_index("s") * SC_CORES + lax.axis_index("c")
    base = worker * tokens_per_worker

    def stage(blk, slot):
        tok0 = base + blk * PEER_SC_TOKENS
        return (pltpu.make_async_copy(idx_hbm.at[pl.ds(tok0, PEER_SC_TOKENS)], idx_v.at[slot], sems.at[ring]),
                pltpu.make_async_copy(x_hbm.at[pl.ds(tok0, PEER_SC_TOKENS)], x_v.at[slot], sems.at[ring + 1]))

    def gather(slot, s, b):
        ids = idx_v[slot, s // PEER_HEADS, pl.ds((s % PEER_HEADS) * PEER_TOPK, PEER_TOPK)]
        return pltpu.make_async_copy(tab_hbm.at[ids], rows_v.at[b], sems.at[b])

    for copy in stage(0, 0):
        copy.start()
    for copy in stage(0, 0):
        copy.wait()
    for b in range(ring - 1):
        gather(0, b, b).start()

    @pl.loop(0, n_blocks)
    def _(blk):
        slot = blk % 2
        has_next = blk + 1 < n_blocks

        @pl.when(has_next)
        def _():
            for copy in stage(blk + 1, 1 - slot):
                copy.start()

        if zero_out:
            @pl.loop(0, PEER_SC_TOKENS)
            def _(t):
                @plsc.parallel_loop(0, out_v.shape[1] // SC_LANES, unroll=PEER_SC_UNROLL)
                def _(c):
                    out_v[t, pl.ds(pl.multiple_of(c * SC_LANES, SC_LANES), SC_LANES)] = (
                        jnp.zeros((SC_LANES,), jnp.float32))

        @pl.loop(0, n_steps, step=ring)
        def _(s0):
            for b in range(ring):
                s = s0 + b
                gather(slot, s, b).wait()
                ahead = s + ring - 1
                ahead_buf = (b + ring - 1) % ring

                @pl.when(ahead < n_steps)
                def _():
                    gather(slot, ahead, ahead_buf).start()

                if b >= 1:
                    @pl.when(jnp.logical_and(ahead >= n_steps, has_next))
                    def _():
                        if b == 1:
                            for copy in stage(blk + 1, 1 - slot):
                                copy.wait()
                        gather(1 - slot, ahead - n_steps, ahead_buf).start()

                compute(slot, s // PEER_HEADS, s % PEER_HEADS, b, x_v, rows_v, out_v)

        pltpu.sync_copy(out_v, out_hbm.at[pl.ds(base + blk * PEER_SC_TOKENS, PEER_SC_TOKENS)])


def _peer_dots_compute(slot, t, hd, b, h_v, rows_v, dots_v):
    lane = lax.iota(jnp.int32, SC_LANES)

    def col_step(c, accs):
        off = pl.multiple_of(c * SC_LANES, SC_LANES)
        hv = h_v[slot, t, pl.ds(off, SC_LANES)]
        return tuple(accs[r] + rows_v[b, r, pl.ds(off, SC_LANES)] * hv for r in range(PEER_TOPK))

    accs = plsc.parallel_loop(
        0, D_MODEL // SC_LANES, unroll=PEER_SC_UNROLL,
        carry=tuple(jnp.zeros((SC_LANES,), jnp.float32) for _ in range(PEER_TOPK)))(col_step)
    res = jnp.zeros((SC_LANES,), jnp.float32)
    for r in range(PEER_TOPK):
        res = jnp.where(lane == r, jnp.sum(accs[r]), res)
    dots_v[t, pl.ds(hd * PEER_TOPK, PEER_TOPK)] = res


def _peer_mix_compute(slot, t, hd, b, act_v, rows_v, out_v):
    svec = jnp.full((SC_LANES,), slot, jnp.int32)
    tvec = jnp.full((SC_LANES,), t, jnp.int32)
    weights = [plsc.load_gather(act_v, [svec, tvec, jnp.full((SC_LANES,), hd * PEER_TOPK + r, jnp.int32)])
               for r in range(PEER_TOPK)]

    @plsc.parallel_loop(0, D_MODEL // SC_LANES, unroll=PEER_SC_UNROLL)
    def _(c):
        off = pl.multiple_of(c * SC_LANES, SC_LANES)
        a = rows_v[b, 0, pl.ds(off, SC_LANES)] * weights[0]
        for r in range(1, PEER_TOPK):
            a = a + rows_v[b, r, pl.ds(off, SC_LANES)] * weights[r]
        plsc.addupdate(out_v.at[t, pl.ds(off, SC_LANES)], a)


def _peer_expert_dots(h, experts, u_tab):
    T = h.shape[0]
    assert T % (SC_CORES * SC_SUBCORES * PEER_SC_TOKENS) == 0
    assert (PEER_SC_TOKENS * PEER_HEADS) % PEER_SC_RING == 0 and PEER_TOPK == SC_LANES
    body = partial(_peer_sc_body, _peer_dots_compute, False)
    return _peer_sc_call(body, T, PEER_SLOTS, D_MODEL)(h, experts, u_tab)


def _peer_expert_mix(act, experts, v_tab):
    T = act.shape[0]
    assert T % (SC_CORES * SC_SUBCORES * PEER_SC_TOKENS) == 0
    body = partial(_peer_sc_body, _peer_mix_compute, True)
    return _peer_sc_call(body, T, D_MODEL, PEER_SLOTS)(act, experts, v_tab)


PEER_TM = 256


def _extract_topk(s, k, payload=None):
    R = s.shape[0]
    riota = lax.broadcasted_iota(jnp.int32, s.shape, 0)
    vals, rows = [], []
    for _ in range(k):
        m = jnp.max(s, axis=0, keepdims=True)
        pos = jnp.min(jnp.where(s == m, riota, R), axis=0, keepdims=True)
        hit = riota == pos
        vals.append(m)
        rows.append(pos if payload is None else jnp.max(jnp.where(hit, payload, -1), axis=0, keepdims=True))
        s = jnp.where(hit, -jnp.inf, s)
    return jnp.concatenate(vals, axis=0), jnp.concatenate(rows, axis=0)


def _peer_route_kernel(x_ref, sc_ref, sh_ref, wq_ref, keys_ref, h_ref, ex_ref, gate_ref, q_ref, ext_ref, gt_ref):
    K = PEER_TOPK
    f32 = jnp.float32
    h = x_ref[0] * (1.0 + sc_ref[0]) + sh_ref[0]
    h_ref[0] = h
    q = jnp.dot(h.astype(wq_ref.dtype), wq_ref[...], preferred_element_type=f32)
    for hd in range(PEER_HEADS):
        q_ref[hd] = q[:, hd * PEER_KEY_DIM:(hd + 1) * PEER_KEY_DIM].astype(q_ref.dtype)

    def head_body(hd, carry):
        qh = q_ref[hd]
        s1 = lax.dot_general(keys_ref[hd, 0], qh[:, :PEER_HALF], _NT, preferred_element_type=f32)
        s2 = lax.dot_general(keys_ref[hd, 1], qh[:, PEER_HALF:], _NT, preferred_element_type=f32)
        v1, i1 = _extract_topk(s1, K)
        v2, i2 = _extract_topk(s2, K)
        tm = v1.shape[1]
        cand_rows, cidx_rows = [], []
        for a in range(K):
            nb = K // (a + 1)
            cand_rows.append(v1[a:a + 1, :] + v2[:nb, :])
            cidx_rows.append(i1[a:a + 1, :] * PEER_N_KEYS + i2[:nb, :])
        n_pad = -sum(r.shape[0] for r in cand_rows) % 8
        cand_rows.append(jnp.full((n_pad, tm), -jnp.inf, f32))
        cidx_rows.append(jnp.full((n_pad, tm), -1, jnp.int32))
        cand = jnp.concatenate(cand_rows, axis=0)
        cidx = jnp.concatenate(cidx_rows, axis=0)
        top_s, experts = _extract_topk(cand, K, payload=cidx)
        e = jnp.exp(top_s - top_s[0:1, :])
        gt_ref[pl.ds(pl.multiple_of(hd * K, K), K), :] = e / jnp.sum(e, axis=0, keepdims=True)
        ext_ref[pl.ds(pl.multiple_of(hd * K, K), K), :] = experts
        return carry

    lax.fori_loop(0, PEER_HEADS, head_body, 0)
    ex_ref[0] = ext_ref[...].T
    gate_ref[0] = gt_ref[...].T


def _peer_route(x, sc, sh, w_pq, sub_keys):
    B, S, D = x.shape
    tm = PEER_TM
    assert S % tm == 0
    return pl.pallas_call(
        _peer_route_kernel,
        grid=(B, S // tm),
        in_specs=[
            pl.BlockSpec((1, tm, D), lambda b, i: (b, i, 0)),
            pl.BlockSpec((1, 1, D), lambda b, i: (b, 0, 0)),
            pl.BlockSpec((1, 1, D), lambda b, i: (b, 0, 0)),
            pl.BlockSpec((D, PEER_HEADS * PEER_KEY_DIM), lambda b, i: (0, 0)),
            pl.BlockSpec((PEER_HEADS, 2, PEER_N_KEYS, PEER_HALF), lambda b, i: (0, 0, 0, 0)),
        ],
        out_specs=[
            pl.BlockSpec((1, tm, D), lambda b, i: (b, i, 0)),
            pl.BlockSpec((1, tm, PEER_SLOTS), lambda b, i: (b, i, 0)),
            pl.BlockSpec((1, tm, PEER_SLOTS), lambda b, i: (b, i, 0)),
        ],
        out_shape=[
            jax.ShapeDtypeStruct((B, S, D), jnp.float32),
            jax.ShapeDtypeStruct((B, S, PEER_SLOTS), jnp.int32),
            jax.ShapeDtypeStruct((B, S, PEER_SLOTS), jnp.float32),
        ],
        scratch_shapes=[
            pltpu.VMEM((PEER_HEADS, tm, PEER_KEY_DIM), MM_DTYPE),
            pltpu.VMEM((PEER_SLOTS, tm), jnp.int32),
            pltpu.VMEM((PEER_SLOTS, tm), jnp.float32),
        ],
        compiler_params=pltpu.CompilerParams(dimension_semantics=("parallel", "parallel")),
        name="peer_route",
    )(x, sc[:, None, :], sh[:, None, :], w_pq.astype(MM_DTYPE), sub_keys.astype(MM_DTYPE))


def _peer_pre(h2, experts, u_tab):
    B, S, D = h2.shape
    return _peer_expert_dots(h2.reshape(B * S, D), experts.reshape(B * S, PEER_SLOTS), u_tab)


def _peer_act_kernel(pre_ref, gate_ref, o_ref):
    pre = pre_ref[...]
    o_ref[...] = 0.5 * pre * (1.0 + lax.erf(pre * (2.0 ** -0.5))) * gate_ref[...]


def _peer_act(pre, gates):
    T, W = pre.shape
    tm = min(8 * ROW_TILE, T)
    assert T % tm == 0
    spec = pl.BlockSpec((tm, W), lambda i: (i, 0))
    return pl.pallas_call(
        _peer_act_kernel,
        grid=(T // tm,),
        in_specs=[spec, spec],
        out_specs=spec,
        out_shape=jax.ShapeDtypeStruct((T, W), jnp.float32),
        name="peer_act",
    )(pre, gates.reshape(T, W))


def _peer_out(act, experts, v_tab):
    B, S, _ = experts.shape
    return _peer_expert_mix(act, experts.reshape(B * S, PEER_SLOTS), v_tab).reshape(B, S, D_MODEL)


def kernel(x, c, w_ada, b_ada, w_in, rw_mu, rw_w0, rw_w2, rw_a0, rw_a2, rw_g2, rw_k_k, rw_k_a, rw_r_k, rw_gn_g, rw_gn_b, dsa_kv_g, idx_k_g, idx_k_b, rel_bias, w_br_a, w_br_b, w_out, ln1_g, ln1_b, peer_wq, peer_keys, peer_u, peer_v, ln2_g, ln2_b):
    l = 0
    mod = jax.nn.silu(c) @ w_ada[l] + b_ada[l]

    w_rw, w_q, w_kv, w_qi, w_ki, w_wi, w_ga, w_gb = jnp.split(w_in[l], _split_points(IN_SIZES), axis=-1)
    small_pad = jnp.zeros((D_MODEL, SMALL_COLS - SMALL_WI - IDX_HEADS), w_in.dtype)
    w_small = jnp.concatenate([w_qi, w_kv, w_ki, w_wi, small_pad], axis=-1)
    w_gates = jnp.concatenate([w_ga, w_gb], axis=-1)

    B, S, _ = x.shape
    seg_len = S // SEQ_SEGMENTS
    dsa_topk = min(TOPK_MAX, S // 4)
    bias_tiles = _dsa_bias_tiles(rel_bias, DSA_TQ)

    def project_and_dsa(x, sc1, sh1, seg, keys_so_far):
        z_rw = _mod_matmul(x, sc1, sh1, w_rw, jnp.float32, tn=896)
        z_q = _mod_matmul(x, sc1, sh1, w_q, MM_DTYPE, tn=1024)
        z_small = _mod_matmul(x, sc1, sh1, w_small, jnp.float32, tn=512)
        z_g = _mod_matmul(x, sc1, sh1, w_gates, jnp.float32, tn=1024)
        z_wi_t = jnp.swapaxes(z_small[..., SMALL_WI:SMALL_WI + IDX_HEADS], 1, 2)
        qi, ckv, kidx = _dsa_prep(z_small, dsa_kv_g[l], idx_k_g[l], idx_k_b[l])
        if keys_so_far is not None:
            ckv = jnp.concatenate([keys_so_far[0], ckv], axis=1)
            kidx = jnp.concatenate([keys_so_far[1], kidx], axis=1)
        y_b = _dsa_attention(z_q, qi, z_wi_t, ckv, kidx, bias_tiles, dsa_topk,
                             seg * (seg_len // DSA_TQ), MM_DTYPE)
        return (y_b, z_rw, z_g), (ckv, kidx)

    def rwkv_and_merge(x, gt1, carry, y_b, z_rw, z_g):
        y_a, carry = _rwkv7_time_mix(z_rw, carry, rw_mu[l], rw_w0[l], rw_w2[l], rw_a0[l], rw_a2[l], rw_g2[l],
                                     rw_k_k[l], rw_k_a[l], rw_r_k[l], rw_gn_g[l], rw_gn_b[l], MM_DTYPE)
        return _merge_ln(x, gt1, y_a, y_b, z_g, w_br_a[l], w_br_b[l], w_out[l], ln1_g[l], ln1_b[l]), carry

    def finish(st, act):
        y2 = _peer_out(act, st["experts"], peer_v[l])
        return _residual_ln(st["x1"], st["gt2"], y2, ln2_g[l], ln2_b[l])

    assert B % BATCH_GROUPS == 0 and S % (SEQ_SEGMENTS * DSA_TQ) == 0
    gsz = B // BATCH_GROUPS
    outs = []
    prev = routed_before = None
    for g in range(BATCH_GROUPS):
        sh1, sc1, gt1, sh2, sc2, gt2 = jnp.split(mod[g * gsz:(g + 1) * gsz], 6, axis=-1)
        rw_carry = _rwkv7_init_carry(gsz)
        keys_so_far = None
        for seg in range(SEQ_SEGMENTS):
            x_g = x[g * gsz:(g + 1) * gsz, seg * seg_len:(seg + 1) * seg_len]
            if routed_before is not None:
                x_g, routed_before = lax.optimization_barrier((x_g, routed_before))
            routed_before = prev["experts"] if prev is not None else None
            proj, keys_so_far = project_and_dsa(x_g, sc1, sh1, seg, keys_so_far)
            if prev is not None:
                proj, prev["pre"], outs = lax.optimization_barrier((proj, prev["pre"], outs))
                outs.append(finish(prev, _peer_act(prev["pre"], prev["gates"])))
            x1, rw_carry = rwkv_and_merge(x_g, gt1, rw_carry, *proj)
            h2, experts, gates = _peer_route(x1, sc2, sh2, peer_wq[l], peer_keys[l])
            prev = dict(x1=x1, gt2=gt2, experts=experts, gates=gates, pre=_peer_pre(h2, experts, peer_u[l]))
    prev["pre"], outs = lax.optimization_barrier((prev["pre"], outs))
    outs.append(finish(prev, _peer_act(prev["pre"], prev["gates"])))
    rows = [jnp.concatenate(outs[g * SEQ_SEGMENTS:(g + 1) * SEQ_SEGMENTS], axis=1) for g in range(BATCH_GROUPS)]
    return jnp.concatenate(rows, axis=0)
```

```python
import math
from functools import partial

import jax
import jax.numpy as jnp
import numpy as np
from jax import lax
from jax.experimental import pallas as pl
from jax.experimental.pallas import tpu as pltpu
from jax.experimental.pallas import tpu_sc as plsc

D_MODEL = 1024
RW_HEADS = 8
RW_HEAD_DIM = 64
RW_DIM = 512
RW_DECAY_LORA = 64
RW_A_LORA = 64
RW_GATE_LORA = 128
RW_COLS = 3 * RW_DIM + RW_DECAY_LORA + RW_A_LORA + RW_GATE_LORA
RW_GN_EPS = 64e-5
DSA_HEADS = 8
DSA_LATENT = 128
DSA_Q_DIM = DSA_HEADS * DSA_LATENT
IDX_HEADS = 4
IDX_DIM = 64
TOPK_MAX = 256
REL_BUCKETS = 32
REL_MAX_DIST = 128
IN_SIZES = (RW_COLS, DSA_Q_DIM, DSA_LATENT, IDX_HEADS * IDX_DIM, IDX_DIM, IDX_HEADS, D_MODEL, D_MODEL)
IN_COLS = sum(IN_SIZES)
PEER_HEADS = 8
PEER_N_KEYS = 128
PEER_KEY_DIM = 128
PEER_HALF = 64
PEER_TOPK = 16
LN_EPS = 1e-5
DEPTH = 1
DEEPNORM_ALPHA = (2.0 * DEPTH) ** 0.25

LANES = 128
SMALL_QI = 0
SMALL_KV = SMALL_QI + IDX_HEADS * IDX_DIM
SMALL_KI = SMALL_KV + DSA_LATENT
SMALL_WI = SMALL_KI + IDX_DIM
SMALL_COLS = 4 * LANES
MM_DTYPE = jnp.bfloat16
BATCH_GROUPS = 8
SEQ_SEGMENTS = 2


def _split_points(sizes):
    return np.cumsum(sizes)[:-1].tolist()


def _mod_matmul_kernel(x_ref, sc_ref, sh_ref, w_ref, o_ref):
    h = x_ref[0] * (1.0 + sc_ref[0]) + sh_ref[0]
    o_ref[0] = jnp.dot(h.astype(w_ref.dtype), w_ref[...],
                       preferred_element_type=jnp.float32).astype(o_ref.dtype)


def _mod_matmul(x, sc, sh, w, out_dtype, tn, tm=512):
    B, S, D = x.shape
    N = w.shape[1]
    assert S % tm == 0 and N % tn == 0
    return pl.pallas_call(
        _mod_matmul_kernel,
        grid=(B, S // tm, N // tn),
        in_specs=[
            pl.BlockSpec((1, tm, D), lambda b, i, j: (b, i, 0)),
            pl.BlockSpec((1, 1, D), lambda b, i, j: (b, 0, 0)),
            pl.BlockSpec((1, 1, D), lambda b, i, j: (b, 0, 0)),
            pl.BlockSpec((D, tn), lambda b, i, j: (0, j)),
        ],
        out_specs=pl.BlockSpec((1, tm, tn), lambda b, i, j: (b, i, j)),
        out_shape=jax.ShapeDtypeStruct((B, S, N), out_dtype),
        name="mod_matmul",
    )(x, sc[:, None, :], sh[:, None, :], w.astype(MM_DTYPE))


ROW_TILE = 256


def _layer_norm(x, g, b):
    mu = jnp.mean(x, -1, keepdims=True)
    var = jnp.mean(jnp.square(x - mu), -1, keepdims=True)
    return (x - mu) * lax.rsqrt(var + LN_EPS) * g + b


def _merge_ln_kernel(x_ref, gt_ref, ya_ref, yb_ref, zg_ref, wa_ref, wb_ref, wo_ref, g_ref, b_ref, o_ref):
    f32 = jnp.float32
    D = x_ref.shape[2]
    ya = jnp.dot(ya_ref[0], wa_ref[...], preferred_element_type=f32)
    yb = jnp.dot(yb_ref[0], wb_ref[...], preferred_element_type=f32)
    zg = zg_ref[0]
    merged = jax.nn.sigmoid(zg[:, :D]) * ya + jax.nn.sigmoid(zg[:, D:]) * yb
    mix = jnp.dot(merged.astype(wo_ref.dtype), wo_ref[...], preferred_element_type=f32)
    o_ref[0] = _layer_norm(DEEPNORM_ALPHA * x_ref[0] + gt_ref[0] * mix, g_ref[...], b_ref[...])


def _merge_ln(x, gt, y_a, y_b, z_g, w_a, w_b, w_o, ln_g, ln_b):
    B, S, D = x.shape
    tm = ROW_TILE
    assert S % tm == 0
    tok = lambda width: pl.BlockSpec((1, tm, width), lambda b, i: (b, i, 0))
    full = lambda a: pl.BlockSpec(a.shape, lambda b, i: (0,) * a.ndim)
    w_a, w_b, w_o = (w.astype(MM_DTYPE) for w in (w_a, w_b, w_o))
    ln_g, ln_b = ln_g[None], ln_b[None]
    return pl.pallas_call(
        _merge_ln_kernel,
        grid=(B, S // tm),
        in_specs=[tok(D), pl.BlockSpec((1, 1, D), lambda b, i: (b, 0, 0)), tok(y_a.shape[2]), tok(y_b.shape[2]),
                  tok(z_g.shape[2]), full(w_a), full(w_b), full(w_o), full(ln_g), full(ln_b)],
        out_specs=tok(D),
        out_shape=jax.ShapeDtypeStruct((B, S, D), jnp.float32),
        compiler_params=pltpu.CompilerParams(dimension_semantics=("parallel", "parallel")),
        name="merge_ln",
    )(x, gt[:, None, :], y_a, y_b, z_g, w_a, w_b, w_o, ln_g, ln_b)


def _residual_ln_kernel(x_ref, gt_ref, y_ref, g_ref, b_ref, o_ref):
    o_ref[0] = _layer_norm(DEEPNORM_ALPHA * x_ref[0] + gt_ref[0] * y_ref[0], g_ref[...], b_ref[...])


def _residual_ln(x, gt, y, ln_g, ln_b):
    B, S, D = x.shape
    tm = ROW_TILE
    assert S % tm == 0
    tok = pl.BlockSpec((1, tm, D), lambda b, i: (b, i, 0))
    vec = pl.BlockSpec((1, D), lambda b, i: (0, 0))
    return pl.pallas_call(
        _residual_ln_kernel,
        grid=(B, S // tm),
        in_specs=[tok, pl.BlockSpec((1, 1, D), lambda b, i: (b, 0, 0)), tok, vec, vec],
        out_specs=tok,
        out_shape=jax.ShapeDtypeStruct((B, S, D), jnp.float32),
        compiler_params=pltpu.CompilerParams(dimension_semantics=("parallel", "parallel")),
        name="residual_ln",
    )(x, gt[:, None, :], y, ln_g[None], ln_b[None])


def _t5_bucket(n):
    n = jnp.maximum(n, 0)
    max_exact = REL_BUCKETS // 2
    nf = jnp.maximum(n, 1).astype(jnp.float32)
    large = max_exact + (jnp.log(nf / max_exact) / math.log(REL_MAX_DIST / max_exact)
                         * (REL_BUCKETS - max_exact)).astype(jnp.int32)
    large = jnp.minimum(large, REL_BUCKETS - 1)
    return jnp.where(n < max_exact, n, large)


RW_CHUNK = 64
RW_INV_BLOCK = 16
_NN = (((1,), (0,)), ((), ()))
_NT = (((1,), (1,)), ((), ()))
_BNN = (((2,), (1,)), ((0,), (0,)))
_BNT = (((2,), (2,)), ((0,), (0,)))


def _dot_f32(a, b, dims=_NN):
    return lax.dot_general(a, b, dims, precision=lax.Precision.HIGHEST,
                           preferred_element_type=jnp.float32)


def _dot_bf16x3(a, b, dims=_NN):
    f32, bf = jnp.float32, jnp.bfloat16
    a_hi, b_hi = a.astype(bf), b.astype(bf)
    a_lo = (a - a_hi.astype(f32)).astype(bf)
    b_lo = (b - b_hi.astype(f32)).astype(bf)
    out = lax.dot_general(a_hi, b_hi, dims, preferred_element_type=f32)
    out = out + lax.dot_general(a_hi, b_lo, dims, preferred_element_type=f32)
    return out + lax.dot_general(a_lo, b_hi, dims, preferred_element_type=f32)


def _bf16_terms(a):
    f32, bf = jnp.float32, jnp.bfloat16
    hi = a.astype(bf)
    r1 = a - hi.astype(f32)
    mid = r1.astype(bf)
    lo = (r1 - mid.astype(f32)).astype(bf)
    return hi, mid, lo


def _dot_lhs_split(a, b01):
    b = b01.astype(jnp.bfloat16)
    return sum(jnp.dot(t, b, preferred_element_type=jnp.float32) for t in _bf16_terms(a))


def _dot_rhs_split(a01, b):
    a = a01.astype(jnp.bfloat16)
    return sum(jnp.dot(a, t, preferred_element_type=jnp.float32) for t in _bf16_terms(b))


def _rwkv_kernel(z_ref, m0_ref, p0_ref, mu_ref, w0_ref, w2_ref, a0_ref, a2_ref, g2_ref, kk_ref, ka_ref, rk_ref,
                 gng_ref, gnb_ref, bd_ref, o_ref, mo_ref, po_ref, m_ref, prev_ref, y_ref):
    C = z_ref.shape[1]
    N = RW_HEAD_DIM
    f32 = jnp.float32
    dot3 = _dot_bf16x3

    @pl.when(pl.program_id(1) == 0)
    def _():
        m_ref[...] = m0_ref[0]
        prev_ref[...] = p0_ref[0]

    z = z_ref[0]
    row = lax.broadcasted_iota(jnp.int32, z.shape, 0)
    shifted = jnp.where(row == 0, prev_ref[...], pltpu.roll(z, 1, axis=0))
    prev_ref[...] = z[C - 1:C, :]
    zs = z + (shifted - z) * mu_ref[...]
    r = zs[:, 0:RW_DIM]
    k = zs[:, RW_DIM:2 * RW_DIM]
    v = zs[:, 2 * RW_DIM:3 * RW_DIM]
    o1 = 3 * RW_DIM
    wl = zs[:, o1:o1 + RW_DECAY_LORA]
    al = zs[:, o1 + RW_DECAY_LORA:o1 + RW_DECAY_LORA + RW_A_LORA]
    gl = zs[:, o1 + RW_DECAY_LORA + RW_A_LORA:]

    bd = bd_ref[...]
    log_w = -jax.nn.softplus(-(w0_ref[...] + dot3(jnp.tanh(wl), w2_ref[...]))) - 0.5
    ldec = -jnp.exp(log_w)
    a_lr = jax.nn.sigmoid(a0_ref[...] + dot3(al, a2_ref[...]))
    g = dot3(jax.nn.sigmoid(gl), g2_ref[...])
    kk = k * kk_ref[...]
    kk = kk * lax.rsqrt(jnp.maximum(_dot_lhs_split(kk * kk, bd), 1e-24))
    k2 = k * (1.0 + (a_lr - 1.0) * ka_ref[...])
    a_vec = -kk
    b_vec = kk * a_lr

    ti = lax.broadcasted_iota(jnp.int32, (C, C), 0)
    tj = lax.broadcasted_iota(jnp.int32, (C, C), 1)
    cum = _dot_rhs_split((ti >= tj).astype(f32), ldec)
    cum_last = cum[C - 1:C, :]
    w_incl = jnp.exp(cum)
    w_excl = jnp.exp(cum - ldec)
    w_inv = jnp.exp(-cum)
    w_end = jnp.exp(cum_last - cum)
    w_all = jnp.exp(cum_last)
    a_t = a_vec * w_excl
    r_t = r * w_incl
    b_t = b_vec * w_inv
    k_t = k2 * w_inv
    b_e = b_vec * w_end
    k_e = k2 * w_end

    strict = ti > tj
    incl = ti >= tj
    bi, bj = ti // RW_INV_BLOCK, tj // RW_INV_BLOCK
    same_blk = bi == bj
    pair_blk = jnp.logical_and((bi // 2) == (bj // 2), jnp.logical_not(same_blk))
    half_blk = (bi // 2) != (bj // 2)
    eye = (ti == tj).astype(f32)

    H = RW_HEADS
    heads = lambda x: jnp.stack([x[:, h * N:(h + 1) * N] for h in range(H)], axis=0)
    bmm = lambda a, b: dot3(a, b, _BNN)
    A, Rt, Bt, Kt, Be, Ke, V = (heads(t) for t in (a_t, r_t, b_t, k_t, b_e, k_e, v))
    gm = dot3(jnp.concatenate([A, Rt], axis=1), jnp.concatenate([Bt, Kt], axis=1), _BNT)
    a_ab = jnp.where(strict, gm[:, :C, :C], 0.0)
    a_ak = jnp.where(strict, gm[:, :C, C:], 0.0)
    a_rb = jnp.where(incl, gm[:, C:, :C], 0.0)
    a_rk = jnp.where(incl, gm[:, C:, C:], 0.0)
    d1 = jnp.where(same_blk, a_ab, 0.0)
    xinv = eye + d1
    d2 = bmm(d1, d1)
    xinv = xinv + bmm(xinv, d2)
    d4 = bmm(d2, d2)
    xinv = xinv + bmm(xinv, d4)
    d8 = bmm(d4, d4)
    xinv = xinv + bmm(xinv, d8)
    xinv = xinv + bmm(bmm(xinv, jnp.where(pair_blk, a_ab, 0.0)), xinv)
    xinv = xinv + bmm(bmm(xinv, jnp.where(half_blk, a_ab, 0.0)), xinv)

    av = bmm(jnp.concatenate([a_ak, a_rk], axis=1), V)
    p = bmm(xinv, jnp.concatenate([A, av[:, :C]], axis=2))
    qm = bmm(a_rb, p)
    q1 = Rt + qm[:, :, :N]
    q2 = qm[:, :, N:] + av[:, C:]
    gmat = bmm(jnp.swapaxes(Be, 1, 2), p)
    g1 = eye * heads(w_all) + gmat[:, :, :N]
    g2 = gmat[:, :, N:] + bmm(jnp.swapaxes(Ke, 1, 2), V)
    m = m_ref[...]
    yh = _dot_f32(q1, m, _BNN) + q2
    m_ref[...] = _dot_f32(g1, m, _BNN) + g2
    for h in range(H):
        y_ref[:, h * N:(h + 1) * N] = yh[h]

    y = y_ref[...]
    mean = _dot_lhs_split(y, bd) * (1.0 / N)
    yc = y - mean
    var = _dot_lhs_split(yc * yc, bd) * (1.0 / N)
    yn = yc * lax.rsqrt(var + RW_GN_EPS) * gng_ref[...] + gnb_ref[...]
    bonus = _dot_lhs_split(r * k2 * rk_ref[...], bd) * v
    o_ref[0] = ((yn + bonus) * g).astype(o_ref.dtype)

    @pl.when(pl.program_id(1) == pl.num_programs(1) - 1)
    def _():
        mo_ref[0] = m_ref[...]
        po_ref[0] = prev_ref[...]


def _rwkv7_init_carry(batch):
    return (jnp.zeros((batch, RW_HEADS, RW_HEAD_DIM, RW_HEAD_DIM), jnp.float32),
            jnp.zeros((batch, 1, RW_COLS), jnp.float32))


def _rwkv7_time_mix(z_rw, carry, mu, w0, w2, a0, a2, g2, k_k, k_a, r_k, gn_g, gn_b, out_dtype):
    B, S, _ = z_rw.shape
    C = RW_CHUNK
    assert S % C == 0 and C == RW_HEAD_DIM and C % (4 * RW_INV_BLOCK) == 0
    hid = jnp.arange(RW_DIM) // RW_HEAD_DIM
    bd = (hid[:, None] == hid[None, :]).astype(jnp.float32)
    row = lambda a: a.reshape(1, -1)
    full = lambda shape: pl.BlockSpec(shape, lambda b, c: (0,) * len(shape))
    state_spec = pl.BlockSpec((1, RW_HEADS, RW_HEAD_DIM, RW_HEAD_DIM), lambda b, c: (b, 0, 0, 0))
    prev_spec = pl.BlockSpec((1, 1, RW_COLS), lambda b, c: (b, 0, 0))
    y, m_out, p_out = pl.pallas_call(
        _rwkv_kernel,
        grid=(B, S // C),
        in_specs=[
            pl.BlockSpec((1, C, RW_COLS), lambda b, c: (b, c, 0)), state_spec, prev_spec,
            full((1, RW_COLS)), full((1, RW_DIM)), full((RW_DECAY_LORA, RW_DIM)), full((1, RW_DIM)),
            full((RW_A_LORA, RW_DIM)), full((RW_GATE_LORA, RW_DIM)), full((1, RW_DIM)), full((1, RW_DIM)),
            full((1, RW_DIM)), full((1, RW_DIM)), full((1, RW_DIM)), full((RW_DIM, RW_DIM)),
        ],
        out_specs=[pl.BlockSpec((1, C, RW_DIM), lambda b, c: (b, c, 0)), state_spec, prev_spec],
        out_shape=[jax.ShapeDtypeStruct((B, S, RW_DIM), out_dtype),
                   jax.ShapeDtypeStruct(carry[0].shape, jnp.float32),
                   jax.ShapeDtypeStruct(carry[1].shape, jnp.float32)],
        scratch_shapes=[
            pltpu.VMEM((RW_HEADS, RW_HEAD_DIM, RW_HEAD_DIM), jnp.float32),
            pltpu.VMEM((1, RW_COLS), jnp.float32),
            pltpu.VMEM((C, RW_DIM), jnp.float32),
        ],
        compiler_params=pltpu.CompilerParams(dimension_semantics=("parallel", "arbitrary")),
        name="rwkv7_time_mix",
    )(z_rw, carry[0], carry[1], row(mu), row(w0), w2, row(a0), a2, g2, row(k_k), row(k_a), row(r_k),
      row(gn_g), row(gn_b), bd)
    return y, (m_out, p_out)


DSA_TQ = 256
MASK_NEG = -1e30
INT_MIN = -2 ** 31
KEY_NEG_INF = -2139095041
THRESH_BITS = 32


def _dsa_prep_kernel(z_ref, kvg_ref, kig_ref, kib_ref, qi_ref, kv_ref, ki_ref):
    z = z_ref[0]
    qi_ref[0] = z[:, SMALL_QI:SMALL_KV].astype(qi_ref.dtype)
    kv = z[:, SMALL_KV:SMALL_KI]
    ms = jnp.mean(jnp.square(kv), -1, keepdims=True)
    kv_ref[0] = (kv * lax.rsqrt(ms + LN_EPS) * kvg_ref[...]).astype(kv_ref.dtype)
    ki = z[:, SMALL_KI:SMALL_WI]
    mu = jnp.mean(ki, -1, keepdims=True)
    var = jnp.mean(jnp.square(ki - mu), -1, keepdims=True)
    ki_ref[0] = ((ki - mu) * lax.rsqrt(var + LN_EPS) * kig_ref[...] + kib_ref[...]).astype(ki_ref.dtype)


def _dsa_prep(z_small, kv_g, ki_g, ki_b, tm=512):
    B, S, W = z_small.shape
    assert S % tm == 0
    return pl.pallas_call(
        _dsa_prep_kernel,
        grid=(B, S // tm),
        in_specs=[
            pl.BlockSpec((1, tm, W), lambda b, i: (b, i, 0)),
            pl.BlockSpec((1, DSA_LATENT), lambda b, i: (0, 0)),
            pl.BlockSpec((1, IDX_DIM), lambda b, i: (0, 0)),
            pl.BlockSpec((1, IDX_DIM), lambda b, i: (0, 0)),
        ],
        out_specs=[
            pl.BlockSpec((1, tm, IDX_HEADS * IDX_DIM), lambda b, i: (b, i, 0)),
            pl.BlockSpec((1, tm, DSA_LATENT), lambda b, i: (b, i, 0)),
            pl.BlockSpec((1, tm, IDX_DIM), lambda b, i: (b, i, 0)),
        ],
        out_shape=[
            jax.ShapeDtypeStruct((B, S, IDX_HEADS * IDX_DIM), MM_DTYPE),
            jax.ShapeDtypeStruct((B, S, DSA_LATENT), MM_DTYPE),
            jax.ShapeDtypeStruct((B, S, IDX_DIM), MM_DTYPE),
        ],
        name="dsa_prep",
    )(z_small, kv_g[None], ki_g[None], ki_b[None])


def _sortable_key(s):
    s = jnp.where(s == 0.0, 0.0, s)
    bits = pltpu.bitcast(s, jnp.int32)
    return bits ^ ((bits >> 31) & 0x7FFFFFFF)


def _col_count(mask_i32):
    tk, tq = mask_i32.shape
    return jnp.sum(mask_i32.reshape(tk // 8, 8, tq), axis=0)


def _dsa_kernel(bfar_ref, q_ref, qi_ref, wit_ref, kv_ref, kvt_ref, ki_ref, b0_ref, b1_ref, o_ref,
                key_ref, madd_ref, m_ref, l_ref, acc_ref, *, topk, q_off):
    tq = q_ref.shape[1]
    tk = tq
    i = pl.program_id(1) + q_off
    nj = i + 1
    f32 = jnp.float32
    krow = lax.broadcasted_iota(jnp.int32, (tk, tq), 0)
    qcol = lax.broadcasted_iota(jnp.int32, (tk, tq), 1)

    qi = qi_ref[0]
    wit = wit_ref[0] * (IDX_HEADS ** -0.5)

    def score_chunk(j, carry):
        off = pl.multiple_of(j * tk, tk)
        kc = ki_ref[0, pl.ds(off, tk), :]
        s = jnp.zeros((tk, tq), f32)
        for h in range(IDX_HEADS):
            d = lax.dot_general(kc, qi[:, h * IDX_DIM:(h + 1) * IDX_DIM],
                                (((1,), (1,)), ((), ())), preferred_element_type=f32)
            s = s + wit[h:h + 1, :] * jnp.maximum(d * (IDX_DIM ** -0.5), 0.0)
        causal = (krow + j * tk) <= (qcol + i * tq)
        s = jnp.where(causal, s, -jnp.inf)
        key_ref[j] = _sortable_key(s)
        return carry

    lax.fori_loop(0, nj, score_chunk, 0)

    def count_where(pred_fn):
        def body(j, acc):
            return acc + _col_count(pred_fn(key_ref[j], j).astype(jnp.int32))
        acc = lax.fori_loop(0, nj, body, jnp.zeros((8, tq), jnp.int32))
        return jnp.sum(acc, axis=0, keepdims=True)

    def bit_step(it, t_u):
        bit = THRESH_BITS - 1 - it
        cand_u = t_u | jnp.left_shift(jnp.int32(1), bit)
        cand = cand_u ^ INT_MIN
        cnt = count_where(lambda k, j: k >= cand)
        return jnp.where(cnt >= topk, cand_u, t_u)

    t_u = lax.fori_loop(0, THRESH_BITS, bit_step, jnp.zeros((1, tq), jnp.int32))
    thr = t_u ^ INT_MIN
    cnt_gt = count_where(lambda k, j: k > thr)
    cnt_ge = count_where(lambda k, j: k >= thr)
    is_neg = thr == KEY_NEG_INF
    need = jnp.logical_and(cnt_ge > topk, jnp.logical_not(is_neg))
    n_tie_take = topk - cnt_gt

    thr_open = jnp.where(is_neg, thr, thr - 1)
    any_need = jnp.max(need.astype(jnp.int32)) > 0

    @pl.when(jnp.logical_not(any_need))
    def _():
        def body(j, carry):
            madd_ref[j] = jnp.where(key_ref[j] > thr_open, 0.0, MASK_NEG)
            return carry
        lax.fori_loop(0, nj, body, 0)

    @pl.when(any_need)
    def _():
        s_len = tk * key_ref.shape[0]
        n_bits = max(1, int(math.ceil(math.log2(s_len))))

        def idx_step(it, p):
            bit = n_bits - 1 - it
            cand = p | jnp.left_shift(jnp.int32(1), bit)
            cnt = count_where(
                lambda k, j: jnp.where(k == thr, jnp.where((krow + j * tk) < cand, 1, 0), 0))
            return jnp.where(cnt < n_tie_take, cand, p)

        p_idx = lax.fori_loop(0, n_bits, idx_step, jnp.zeros((1, tq), jnp.int32))
        p_idx = jnp.where(need, p_idx, jnp.where(is_neg, -1, s_len))

        def body(j, carry):
            k = key_ref[j]
            tie_ok = jnp.where((krow + j * tk) <= p_idx, 0.0, MASK_NEG)
            madd_ref[j] = jnp.where(k > thr, 0.0, jnp.where(k == thr, tie_ok, MASK_NEG))
            return carry
        lax.fori_loop(0, nj, body, 0)

    H, d = DSA_HEADS, DSA_LATENT
    m_ref[...] = jnp.full(m_ref.shape, MASK_NEG, f32)
    l_ref[...] = jnp.zeros(l_ref.shape, f32)
    acc_ref[...] = jnp.zeros(acc_ref.shape, f32)
    scale = DSA_LATENT ** -0.5
    q_all = jnp.concatenate([q_ref[0, :, h * d:(h + 1) * d] for h in range(H)], axis=0)

    def attend(j, bias):
        off = pl.multiple_of(j * tk, tk)
        kc = kv_ref[0, pl.ds(off, tk), :]
        kct = kvt_ref[0, :, pl.ds(off, tk)]
        lg = lax.dot_general(kc, q_all, (((1,), (1,)), ((), ())), preferred_element_type=f32)
        lg = lg * scale + bias + jnp.tile(madd_ref[j], (1, H))
        m_old = m_ref[...]
        m_new = jnp.maximum(m_old, jnp.max(lg, axis=0, keepdims=True))
        alpha = jnp.exp(m_old - m_new)
        p = jnp.exp(lg - m_new)
        l_ref[...] = alpha * l_ref[...] + jnp.sum(p, axis=0, keepdims=True)
        acc_ref[...] = alpha * acc_ref[...] + jnp.dot(kct, p.astype(kct.dtype), preferred_element_type=f32)
        m_ref[...] = m_new

    def far_body(j, carry):
        attend(j, bfar_ref[...])
        return carry

    lax.fori_loop(0, jnp.maximum(i - 1, 0), far_body, 0)

    @pl.when(i >= 1)
    def _():
        attend(i - 1, b1_ref[...])

    attend(i, b0_ref[...])

    out_t = acc_ref[...] / l_ref[...]
    for h in range(H):
        o_ref[0, :, h * d:(h + 1) * d] = out_t[:, h * tq:(h + 1) * tq].T.astype(o_ref.dtype)


def _dsa_bias_tiles(rel_bias, tq):
    dist = jnp.arange(2 * tq, dtype=jnp.int32)
    by_dist = rel_bias[_t5_bucket(dist)].T

    def toeplitz(v):
        H, P = v.shape
        skew = jnp.tile(v, (1, tq + 1))[:, :tq * (P + 1)].reshape(H, tq, P + 1)
        return skew[:, ::-1, :tq]

    b0 = toeplitz(jnp.concatenate([jnp.repeat(by_dist[:, :1], tq - 1, axis=1), by_dist[:, :tq + 1]], axis=1))
    b1 = toeplitz(jnp.concatenate([by_dist[:, 1:], by_dist[:, -1:]], axis=1))
    H = by_dist.shape[0]
    b0 = jnp.transpose(b0, (1, 0, 2)).reshape(tq, H * tq)
    b1 = jnp.transpose(b1, (1, 0, 2)).reshape(tq, H * tq)
    return b0, b1, jnp.repeat(by_dist[:, -1], tq)[None, :]


def _dsa_attention(q, qi, z_wi_t, ckv, kidx, bias_tiles, topk, q_off, out_dtype):
    B, Sq, _ = q.shape
    tq = DSA_TQ
    nq = Sq // tq
    S = ckv.shape[1]
    assert Sq % tq == 0 and S == (q_off + nq) * tq and tq >= REL_MAX_DIST and topk <= tq
    ckv_t = jnp.swapaxes(ckv, 1, 2)
    b0, b1, bfar = bias_tiles
    nk = S // tq
    H, d = DSA_HEADS, DSA_LATENT
    f32_bytes = 4
    vmem_estimate = (
        2 * nk * tq * tq * f32_bytes
        + 2 * 2 * tq * H * tq * f32_bytes
        + 2 * (2 * S * d + S * LANES) * 2
        + 6 * tq * H * tq * f32_bytes
        + 4 * tq * H * d * f32_bytes
    )
    grid_spec = pltpu.PrefetchScalarGridSpec(
        num_scalar_prefetch=0,
        grid=(B, nq),
        in_specs=[
            pl.BlockSpec((1, H * tq), lambda b, i: (0, 0)),
            pl.BlockSpec((1, tq, H * d), lambda b, i: (b, i, 0)),
            pl.BlockSpec((1, tq, IDX_HEADS * IDX_DIM), lambda b, i: (b, i, 0)),
            pl.BlockSpec((1, IDX_HEADS, tq), lambda b, i: (b, 0, i)),
            pl.BlockSpec((1, S, d), lambda b, i: (b, 0, 0)),
            pl.BlockSpec((1, d, S), lambda b, i: (b, 0, 0)),
            pl.BlockSpec((1, S, IDX_DIM), lambda b, i: (b, 0, 0)),
            pl.BlockSpec((tq, H * tq), lambda b, i: (0, 0)),
            pl.BlockSpec((tq, H * tq), lambda b, i: (0, 0)),
        ],
        out_specs=pl.BlockSpec((1, tq, H * d), lambda b, i: (b, i, 0)),
        scratch_shapes=[
            pltpu.VMEM((nk, tq, tq), jnp.int32),
            pltpu.VMEM((nk, tq, tq), jnp.float32),
            pltpu.VMEM((1, H * tq), jnp.float32),
            pltpu.VMEM((1, H * tq), jnp.float32),
            pltpu.VMEM((d, H * tq), jnp.float32),
        ],
    )
    return pl.pallas_call(
        partial(_dsa_kernel, topk=topk, q_off=q_off),
        grid_spec=grid_spec,
        out_shape=jax.ShapeDtypeStruct((B, Sq, H * d), out_dtype),
        compiler_params=pltpu.CompilerParams(vmem_limit_bytes=vmem_estimate),
        name="dsa_attention",
    )(bfar, q, qi, z_wi_t, ckv, ckv_t, kidx, b0, b1)


SC_CORES = 2
SC_SUBCORES = 16
SC_LANES = 16
PEER_SC_TOKENS = 8
PEER_SC_RING = 4
PEER_SC_UNROLL = 4
PEER_SLOTS = PEER_HEADS * PEER_TOPK


def _peer_sc_call(body, T, out_width, stage_width):
    mesh = plsc.VectorSubcoreMesh(core_axis_name="c", subcore_axis_name="s")
    return pl.kernel(
        body, mesh=mesh,
        out_type=jax.ShapeDtypeStruct((T, out_width), jnp.float32),
        scratch_types=[
            pltpu.VMEM((2, PEER_SC_TOKENS, PEER_SLOTS), jnp.int32),
            pltpu.VMEM((2, PEER_SC_TOKENS, stage_width), jnp.float32),
            pltpu.VMEM((PEER_SC_RING, PEER_TOPK, D_MODEL), jnp.float32),
            pltpu.VMEM((2, PEER_SC_TOKENS, out_width), jnp.float32),
            pltpu.SemaphoreType.DMA((PEER_SC_RING + 4,)),
        ],
        compiler_params=pltpu.CompilerParams(needs_layout_passes=False),
    )


def _peer_sc_body(compute, zero_out, x_hbm, idx_hbm, tab_hbm, out_hbm, idx_v, x_v, rows_v, out_v, sems):
    T = idx_hbm.shape[0]
    tokens_per_worker = T // (SC_CORES * SC_SUBCORES)
    n_blocks = tokens_per_worker // PEER_SC_TOKENS
    n_steps = PEER_SC_TOKENS * PEER_HEADS
    ring = PEER_SC_RING
    worker = lax.axis_index("s") * SC_CORES + lax.axis_index("c")
    base = worker * tokens_per_worker

    def stage(blk, slot):
        tok0 = base + blk * PEER_SC_TOKENS
        return (pltpu.make_async_copy(idx_hbm.at[pl.ds(tok0, PEER_SC_TOKENS)], idx_v.at[slot], sems.at[ring]),
                pltpu.make_async_copy(x_hbm.at[pl.ds(tok0, PEER_SC_TOKENS)], x_v.at[slot], sems.at[ring + 1]))

    def gather(slot, s, b):
        ids = idx_v[slot, s // PEER_HEADS, pl.ds((s % PEER_HEADS) * PEER_TOPK, PEER_TOPK)]
        return pltpu.make_async_copy(tab_hbm.at[ids], rows_v.at[b], sems.at[b])

    def write_back(blk, slot):
        rows = out_hbm.at[pl.ds(base + blk * PEER_SC_TOKENS, PEER_SC_TOKENS)]
        return pltpu.make_async_copy(out_v.at[slot], rows, sems.at[ring + 2 + slot])

    for copy in stage(0, 0):
        copy.start()
    for copy in stage(0, 0):
        copy.wait()
    for b in range(ring - 1):
        gather(0, b, b).start()

    @pl.loop(0, n_blocks)
    def _(blk):
        slot = blk % 2
        has_next = blk + 1 < n_blocks

        @pl.when(has_next)
        def _():
            for copy in stage(blk + 1, 1 - slot):
                copy.start()

        @pl.when(blk >= 2)
        def _():
            write_back(blk - 2, slot).wait()

        if zero_out:
            @pl.loop(0, PEER_SC_TOKENS)
            def _(t):
                @plsc.parallel_loop(0, out_v.shape[2] // SC_LANES, unroll=PEER_SC_UNROLL)
                def _(c):
                    out_v[slot, t, pl.ds(pl.multiple_of(c * SC_LANES, SC_LANES), SC_LANES)] = (
                        jnp.zeros((SC_LANES,), jnp.float32))

        @pl.loop(0, n_steps, step=ring)
        def _(s0):
            for b in range(ring):
                s = s0 + b
                gather(slot, s, b).wait()
                ahead = s + ring - 1
                ahead_buf = (b + ring - 1) % ring

                @pl.when(ahead < n_steps)
                def _():
                    gather(slot, ahead, ahead_buf).start()

                if b >= 1:
                    @pl.when(jnp.logical_and(ahead >= n_steps, has_next))
                    def _():
                        if b == 1:
                            for copy in stage(blk + 1, 1 - slot):
                                copy.wait()
                        gather(1 - slot, ahead - n_steps, ahead_buf).start()

                compute(slot, s // PEER_HEADS, s % PEER_HEADS, b, x_v, rows_v, out_v)

        write_back(blk, slot).start()

    for blk in range(max(0, n_blocks - 2), n_blocks):
        write_back(blk, blk % 2).wait()


def _peer_dots_compute(slot, t, hd, b, h_v, rows_v, dots_v):
    lane = lax.iota(jnp.int32, SC_LANES)

    def col_step(c, accs):
        off = pl.multiple_of(c * SC_LANES, SC_LANES)
        hv = h_v[slot, t, pl.ds(off, SC_LANES)]
        return tuple(accs[r] + rows_v[b, r, pl.ds(off, SC_LANES)] * hv for r in range(PEER_TOPK))

    accs = plsc.parallel_loop(
        0, D_MODEL // SC_LANES, unroll=PEER_SC_UNROLL,
        carry=tuple(jnp.zeros((SC_LANES,), jnp.float32) for _ in range(PEER_TOPK)))(col_step)
    res = jnp.zeros((SC_LANES,), jnp.float32)
    for r in range(PEER_TOPK):
        res = jnp.where(lane == r, jnp.sum(accs[r]), res)
    dots_v[slot, t, pl.ds(hd * PEER_TOPK, PEER_TOPK)] = res


def _peer_mix_compute(slot, t, hd, b, act_v, rows_v, out_v):
    svec = jnp.full((SC_LANES,), slot, jnp.int32)
    tvec = jnp.full((SC_LANES,), t, jnp.int32)
    weights = [plsc.load_gather(act_v, [svec, tvec, jnp.full((SC_LANES,), hd * PEER_TOPK + r, jnp.int32)])
               for r in range(PEER_TOPK)]

    @plsc.parallel_loop(0, D_MODEL // SC_LANES, unroll=PEER_SC_UNROLL)
    def _(c):
        off = pl.multiple_of(c * SC_LANES, SC_LANES)
        a = rows_v[b, 0, pl.ds(off, SC_LANES)] * weights[0]
        for r in range(1, PEER_TOPK):
            a = a + rows_v[b, r, pl.ds(off, SC_LANES)] * weights[r]
        plsc.addupdate(out_v.at[slot, t, pl.ds(off, SC_LANES)], a)


def _peer_expert_dots(h, experts, u_tab):
    T = h.shape[0]
    assert T % (SC_CORES * SC_SUBCORES * PEER_SC_TOKENS) == 0
    assert (PEER_SC_TOKENS * PEER_HEADS) % PEER_SC_RING == 0 and PEER_TOPK == SC_LANES
    body = partial(_peer_sc_body, _peer_dots_compute, False)
    return _peer_sc_call(body, T, PEER_SLOTS, D_MODEL)(h, experts, u_tab)


def _peer_expert_mix(act, experts, v_tab):
    T = act.shape[0]
    assert T % (SC_CORES * SC_SUBCORES * PEER_SC_TOKENS) == 0
    body = partial(_peer_sc_body, _peer_mix_compute, True)
    return _peer_sc_call(body, T, D_MODEL, PEER_SLOTS)(act, experts, v_tab)


PEER_TM = 256


def _extract_topk(s, k, payload=None):
    R = s.shape[0]
    riota = lax.broadcasted_iota(jnp.int32, s.shape, 0)
    vals, rows = [], []
    for _ in range(k):
        m = jnp.max(s, axis=0, keepdims=True)
        pos = jnp.min(jnp.where(s == m, riota, R), axis=0, keepdims=True)
        hit = riota == pos
        vals.append(m)
        rows.append(pos if payload is None else jnp.max(jnp.where(hit, payload, -1), axis=0, keepdims=True))
        s = jnp.where(hit, -jnp.inf, s)
    return jnp.concatenate(vals, axis=0), jnp.concatenate(rows, axis=0)


def _peer_route_kernel(x_ref, sc_ref, sh_ref, wq_ref, keys_ref, h_ref, ex_ref, gate_ref, q_ref, ext_ref, gt_ref):
    K = PEER_TOPK
    f32 = jnp.float32
    h = x_ref[0] * (1.0 + sc_ref[0]) + sh_ref[0]
    h_ref[0] = h
    q = jnp.dot(h.astype(wq_ref.dtype), wq_ref[...], preferred_element_type=f32)
    for hd in range(PEER_HEADS):
        q_ref[hd] = q[:, hd * PEER_KEY_DIM:(hd + 1) * PEER_KEY_DIM].astype(q_ref.dtype)

    def head_body(hd, carry):
        qh = q_ref[hd]
        s1 = lax.dot_general(keys_ref[hd, 0], qh[:, :PEER_HALF], _NT, preferred_element_type=f32)
        s2 = lax.dot_general(keys_ref[hd, 1], qh[:, PEER_HALF:], _NT, preferred_element_type=f32)
        v1, i1 = _extract_topk(s1, K)
        v2, i2 = _extract_topk(s2, K)
        tm = v1.shape[1]
        cand_rows, cidx_rows = [], []
        for a in range(K):
            nb = K // (a + 1)
            cand_rows.append(v1[a:a + 1, :] + v2[:nb, :])
            cidx_rows.append(i1[a:a + 1, :] * PEER_N_KEYS + i2[:nb, :])
        n_pad = -sum(r.shape[0] for r in cand_rows) % 8
        cand_rows.append(jnp.full((n_pad, tm), -jnp.inf, f32))
        cidx_rows.append(jnp.full((n_pad, tm), -1, jnp.int32))
        cand = jnp.concatenate(cand_rows, axis=0)
        cidx = jnp.concatenate(cidx_rows, axis=0)
        top_s, experts = _extract_topk(cand, K, payload=cidx)
        e = jnp.exp(top_s - top_s[0:1, :])
        gt_ref[pl.ds(pl.multiple_of(hd * K, K), K), :] = e / jnp.sum(e, axis=0, keepdims=True)
        ext_ref[pl.ds(pl.multiple_of(hd * K, K), K), :] = experts
        return carry

    lax.fori_loop(0, PEER_HEADS, head_body, 0)
    ex_ref[0] = ext_ref[...].T
    gate_ref[0] = gt_ref[...].T


def _peer_route(x, sc, sh, w_pq, sub_keys):
    B, S, D = x.shape
    tm = PEER_TM
    assert S % tm == 0
    return pl.pallas_call(
        _peer_route_kernel,
        grid=(B, S // tm),
        in_specs=[
            pl.BlockSpec((1, tm, D), lambda b, i: (b, i, 0)),
            pl.BlockSpec((1, 1, D), lambda b, i: (b, 0, 0)),
            pl.BlockSpec((1, 1, D), lambda b, i: (b, 0, 0)),
            pl.BlockSpec((D, PEER_HEADS * PEER_KEY_DIM), lambda b, i: (0, 0)),
            pl.BlockSpec((PEER_HEADS, 2, PEER_N_KEYS, PEER_HALF), lambda b, i: (0, 0, 0, 0)),
        ],
        out_specs=[
            pl.BlockSpec((1, tm, D), lambda b, i: (b, i, 0)),
            pl.BlockSpec((1, tm, PEER_SLOTS), lambda b, i: (b, i, 0)),
            pl.BlockSpec((1, tm, PEER_SLOTS), lambda b, i: (b, i, 0)),
        ],
        out_shape=[
            jax.ShapeDtypeStruct((B, S, D), jnp.float32),
            jax.ShapeDtypeStruct((B, S, PEER_SLOTS), jnp.int32),
            jax.ShapeDtypeStruct((B, S, PEER_SLOTS), jnp.float32),
        ],
        scratch_shapes=[
            pltpu.VMEM((PEER_HEADS, tm, PEER_KEY_DIM), MM_DTYPE),
            pltpu.VMEM((PEER_SLOTS, tm), jnp.int32),
            pltpu.VMEM((PEER_SLOTS, tm), jnp.float32),
        ],
        compiler_params=pltpu.CompilerParams(dimension_semantics=("parallel", "parallel")),
        name="peer_route",
    )(x, sc[:, None, :], sh[:, None, :], w_pq.astype(MM_DTYPE), sub_keys.astype(MM_DTYPE))


def _peer_pre(h2, experts, u_tab):
    B, S, D = h2.shape
    return _peer_expert_dots(h2.reshape(B * S, D), experts.reshape(B * S, PEER_SLOTS), u_tab)


def _peer_act_kernel(pre_ref, gate_ref, o_ref):
    pre = pre_ref[...]
    o_ref[...] = 0.5 * pre * (1.0 + lax.erf(pre * (2.0 ** -0.5))) * gate_ref[...]


def _peer_act(pre, gates):
    T, W = pre.shape
    tm = min(8 * ROW_TILE, T)
    assert T % tm == 0
    spec = pl.BlockSpec((tm, W), lambda i: (i, 0))
    return pl.pallas_call(
        _peer_act_kernel,
        grid=(T // tm,),
        in_specs=[spec, spec],
        out_specs=spec,
        out_shape=jax.ShapeDtypeStruct((T, W), jnp.float32),
        name="peer_act",
    )(pre, gates.reshape(T, W))


def _peer_out(act, experts, v_tab):
    B, S, _ = experts.shape
    return _peer_expert_mix(act, experts.reshape(B * S, PEER_SLOTS), v_tab).reshape(B, S, D_MODEL)


def kernel(x, c, w_ada, b_ada, w_in, rw_mu, rw_w0, rw_w2, rw_a0, rw_a2, rw_g2, rw_k_k, rw_k_a, rw_r_k, rw_gn_g, rw_gn_b, dsa_kv_g, idx_k_g, idx_k_b, rel_bias, w_br_a, w_br_b, w_out, ln1_g, ln1_b, peer_wq, peer_keys, peer_u, peer_v, ln2_g, ln2_b):
    l = 0
    mod = jax.nn.silu(c) @ w_ada[l] + b_ada[l]

    w_rw, w_q, w_kv, w_qi, w_ki, w_wi, w_ga, w_gb = jnp.split(w_in[l], _split_points(IN_SIZES), axis=-1)
    small_pad = jnp.zeros((D_MODEL, SMALL_COLS - SMALL_WI - IDX_HEADS), w_in.dtype)
    w_small = jnp.concatenate([w_qi, w_kv, w_ki, w_wi, small_pad], axis=-1)
    w_gates = jnp.concatenate([w_ga, w_gb], axis=-1)

    B, S, _ = x.shape
    seg_len = S // SEQ_SEGMENTS
    dsa_topk = min(TOPK_MAX, S // 4)
    bias_tiles = _dsa_bias_tiles(rel_bias, DSA_TQ)

    def project_and_dsa(x, sc1, sh1, seg, keys_so_far):
        z_rw = _mod_matmul(x, sc1, sh1, w_rw, jnp.float32, tn=896)
        z_q = _mod_matmul(x, sc1, sh1, w_q, MM_DTYPE, tn=1024)
        z_small = _mod_matmul(x, sc1, sh1, w_small, jnp.float32, tn=512)
        z_g = _mod_matmul(x, sc1, sh1, w_gates, jnp.float32, tn=1024)
        z_wi_t = jnp.swapaxes(z_small[..., SMALL_WI:SMALL_WI + IDX_HEADS], 1, 2)
        qi, ckv, kidx = _dsa_prep(z_small, dsa_kv_g[l], idx_k_g[l], idx_k_b[l])
        if keys_so_far is not None:
            ckv = jnp.concatenate([keys_so_far[0], ckv], axis=1)
            kidx = jnp.concatenate([keys_so_far[1], kidx], axis=1)
        y_b = _dsa_attention(z_q, qi, z_wi_t, ckv, kidx, bias_tiles, dsa_topk,
                             seg * (seg_len // DSA_TQ), MM_DTYPE)
        return (y_b, z_rw, z_g), (ckv, kidx)

    def rwkv_and_merge(x, gt1, carry, y_b, z_rw, z_g):
        y_a, carry = _rwkv7_time_mix(z_rw, carry, rw_mu[l], rw_w0[l], rw_w2[l], rw_a0[l], rw_a2[l], rw_g2[l],
                                     rw_k_k[l], rw_k_a[l], rw_r_k[l], rw_gn_g[l], rw_gn_b[l], MM_DTYPE)
        return _merge_ln(x, gt1, y_a, y_b, z_g, w_br_a[l], w_br_b[l], w_out[l], ln1_g[l], ln1_b[l]), carry

    def finish(st, act):
        y2 = _peer_out(act, st["experts"], peer_v[l])
        return _residual_ln(st["x1"], st["gt2"], y2, ln2_g[l], ln2_b[l])

    assert B % BATCH_GROUPS == 0 and S % (SEQ_SEGMENTS * DSA_TQ) == 0
    gsz = B // BATCH_GROUPS
    outs = []
    prev = None
    for g in range(BATCH_GROUPS):
        sh1, sc1, gt1, sh2, sc2, gt2 = jnp.split(mod[g * gsz:(g + 1) * gsz], 6, axis=-1)
        rw_carry = _rwkv7_init_carry(gsz)
        keys_so_far = None
        for seg in range(SEQ_SEGMENTS):
            x_g = x[g * gsz:(g + 1) * gsz, seg * seg_len:(seg + 1) * seg_len]
            if prev is not None:
                x_g, prev["experts"] = lax.optimization_barrier((x_g, prev["experts"]))
            proj, keys_so_far = project_and_dsa(x_g, sc1, sh1, seg, keys_so_far)
            if prev is not None:
                proj, prev["pre"], outs = lax.optimization_barrier((proj, prev["pre"], outs))
                outs.append(finish(prev, _peer_act(prev["pre"], prev["gates"])))
            x1, rw_carry = rwkv_and_merge(x_g, gt1, rw_carry, *proj)
            h2, experts, gates = _peer_route(x1, sc2, sh2, peer_wq[l], peer_keys[l])
            prev = dict(x1=x1, gt2=gt2, experts=experts, gates=gates, pre=_peer_pre(h2, experts, peer_u[l]))
    prev["pre"], outs = lax.optimization_barrier((prev["pre"], outs))
    outs.append(finish(prev, _peer_act(prev["pre"], prev["gates"])))
    rows = [jnp.concatenate(outs[g * SEQ_SEGMENTS:(g + 1) * SEQ_SEGMENTS], axis=1) for g in range(BATCH_GROUPS)]
    return jnp.concatenate(rows, axis=0)
```

```python
import math
from functools import partial

import jax
import jax.numpy as jnp
import numpy as np
from jax import lax
from jax.experimental import pallas as pl
from jax.experimental.pallas import tpu as pltpu
from jax.experimental.pallas import tpu_sc as plsc

D_MODEL = 1024
RW_HEADS = 8
RW_HEAD_DIM = 64
RW_DIM = 512
RW_DECAY_LORA = 64
RW_A_LORA = 64
RW_GATE_LORA = 128
RW_COLS = 3 * RW_DIM + RW_DECAY_LORA + RW_A_LORA + RW_GATE_LORA
RW_GN_EPS = 64e-5
DSA_HEADS = 8
DSA_LATENT = 128
DSA_Q_DIM = DSA_HEADS * DSA_LATENT
IDX_HEADS = 4
IDX_DIM = 64
TOPK_MAX = 256
REL_BUCKETS = 32
REL_MAX_DIST = 128
IN_SIZES = (RW_COLS, DSA_Q_DIM, DSA_LATENT, IDX_HEADS * IDX_DIM, IDX_DIM, IDX_HEADS, D_MODEL, D_MODEL)
IN_COLS = sum(IN_SIZES)
PEER_HEADS = 8
PEER_N_KEYS = 128
PEER_KEY_DIM = 128
PEER_HALF = 64
PEER_TOPK = 16
LN_EPS = 1e-5
DEPTH = 1
DEEPNORM_ALPHA = (2.0 * DEPTH) ** 0.25

LANES = 128
SMALL_QI = 0
SMALL_KV = SMALL_QI + IDX_HEADS * IDX_DIM
SMALL_KI = SMALL_KV + DSA_LATENT
SMALL_WI = SMALL_KI + IDX_DIM
SMALL_COLS = 4 * LANES
MM_DTYPE = jnp.bfloat16
BATCH_GROUPS = 8
SEQ_SEGMENTS = 2


def _split_points(sizes):
    return np.cumsum(sizes)[:-1].tolist()


def _mod_matmul_kernel(x_ref, sc_ref, sh_ref, w_ref, o_ref):
    h = x_ref[0] * (1.0 + sc_ref[0]) + sh_ref[0]
    o_ref[0] = jnp.dot(h.astype(w_ref.dtype), w_ref[...],
                       preferred_element_type=jnp.float32).astype(o_ref.dtype)


def _mod_matmul(x, sc, sh, w, out_dtype, tn, tm=512):
    B, S, D = x.shape
    N = w.shape[1]
    assert S % tm == 0 and N % tn == 0
    return pl.pallas_call(
        _mod_matmul_kernel,
        grid=(B, S // tm, N // tn),
        in_specs=[
            pl.BlockSpec((1, tm, D), lambda b, i, j: (b, i, 0)),
            pl.BlockSpec((1, 1, D), lambda b, i, j: (b, 0, 0)),
            pl.BlockSpec((1, 1, D), lambda b, i, j: (b, 0, 0)),
            pl.BlockSpec((D, tn), lambda b, i, j: (0, j)),
        ],
        out_specs=pl.BlockSpec((1, tm, tn), lambda b, i, j: (b, i, j)),
        out_shape=jax.ShapeDtypeStruct((B, S, N), out_dtype),
        name="mod_matmul",
    )(x, sc[:, None, :], sh[:, None, :], w.astype(MM_DTYPE))


ROW_TILE = 256


def _layer_norm(x, g, b):
    mu = jnp.mean(x, -1, keepdims=True)
    var = jnp.mean(jnp.square(x - mu), -1, keepdims=True)
    return (x - mu) * lax.rsqrt(var + LN_EPS) * g + b


def _merge_ln_kernel(x_ref, gt_ref, ya_ref, yb_ref, zg_ref, wa_ref, wb_ref, wo_ref, g_ref, b_ref, o_ref):
    f32 = jnp.float32
    D = x_ref.shape[2]
    ya = jnp.dot(ya_ref[0], wa_ref[...], preferred_element_type=f32)
    yb = jnp.dot(yb_ref[0], wb_ref[...], preferred_element_type=f32)
    zg = zg_ref[0]
    merged = jax.nn.sigmoid(zg[:, :D]) * ya + jax.nn.sigmoid(zg[:, D:]) * yb
    mix = jnp.dot(merged.astype(wo_ref.dtype), wo_ref[...], preferred_element_type=f32)
    o_ref[0] = _layer_norm(DEEPNORM_ALPHA * x_ref[0] + gt_ref[0] * mix, g_ref[...], b_ref[...])


def _merge_ln(x, gt, y_a, y_b, z_g, w_a, w_b, w_o, ln_g, ln_b):
    B, S, D = x.shape
    tm = ROW_TILE
    assert S % tm == 0
    tok = lambda width: pl.BlockSpec((1, tm, width), lambda b, i: (b, i, 0))
    full = lambda a: pl.BlockSpec(a.shape, lambda b, i: (0,) * a.ndim)
    w_a, w_b, w_o = (w.astype(MM_DTYPE) for w in (w_a, w_b, w_o))
    ln_g, ln_b = ln_g[None], ln_b[None]
    return pl.pallas_call(
        _merge_ln_kernel,
        grid=(B, S // tm),
        in_specs=[tok(D), pl.BlockSpec((1, 1, D), lambda b, i: (b, 0, 0)), tok(y_a.shape[2]), tok(y_b.shape[2]),
                  tok(z_g.shape[2]), full(w_a), full(w_b), full(w_o), full(ln_g), full(ln_b)],
        out_specs=tok(D),
        out_shape=jax.ShapeDtypeStruct((B, S, D), jnp.float32),
        compiler_params=pltpu.CompilerParams(dimension_semantics=("parallel", "parallel")),
        name="merge_ln",
    )(x, gt[:, None, :], y_a, y_b, z_g, w_a, w_b, w_o, ln_g, ln_b)


def _residual_ln_kernel(x_ref, gt_ref, y_ref, g_ref, b_ref, o_ref):
    o_ref[0] = _layer_norm(DEEPNORM_ALPHA * x_ref[0] + gt_ref[0] * y_ref[0], g_ref[...], b_ref[...])


def _residual_ln(x, gt, y, ln_g, ln_b):
    B, S, D = x.shape
    tm = ROW_TILE
    assert S % tm == 0
    tok = pl.BlockSpec((1, tm, D), lambda b, i: (b, i, 0))
    vec = pl.BlockSpec((1, D), lambda b, i: (0, 0))
    return pl.pallas_call(
        _residual_ln_kernel,
        grid=(B, S // tm),
        in_specs=[tok, pl.BlockSpec((1, 1, D), lambda b, i: (b, 0, 0)), tok, vec, vec],
        out_specs=tok,
        out_shape=jax.ShapeDtypeStruct((B, S, D), jnp.float32),
        compiler_params=pltpu.CompilerParams(dimension_semantics=("parallel", "parallel")),
        name="residual_ln",
    )(x, gt[:, None, :], y, ln_g[None], ln_b[None])


def _t5_bucket(n):
    n = jnp.maximum(n, 0)
    max_exact = REL_BUCKETS // 2
    nf = jnp.maximum(n, 1).astype(jnp.float32)
    large = max_exact + (jnp.log(nf / max_exact) / math.log(REL_MAX_DIST / max_exact)
                         * (REL_BUCKETS - max_exact)).astype(jnp.int32)
    large = jnp.minimum(large, REL_BUCKETS - 1)
    return jnp.where(n < max_exact, n, large)


RW_CHUNK = 64
RW_INV_BLOCK = 16
_NN = (((1,), (0,)), ((), ()))
_NT = (((1,), (1,)), ((), ()))
_BNN = (((2,), (1,)), ((0,), (0,)))
_BNT = (((2,), (2,)), ((0,), (0,)))


def _dot_f32(a, b, dims=_NN):
    return lax.dot_general(a, b, dims, precision=lax.Precision.HIGHEST,
                           preferred_element_type=jnp.float32)


def _dot_bf16x3(a, b, dims=_NN):
    f32, bf = jnp.float32, jnp.bfloat16
    a_hi, b_hi = a.astype(bf), b.astype(bf)
    a_lo = (a - a_hi.astype(f32)).astype(bf)
    b_lo = (b - b_hi.astype(f32)).astype(bf)
    out = lax.dot_general(a_hi, b_hi, dims, preferred_element_type=f32)
    out = out + lax.dot_general(a_hi, b_lo, dims, preferred_element_type=f32)
    return out + lax.dot_general(a_lo, b_hi, dims, preferred_element_type=f32)


def _bf16_terms(a):
    f32, bf = jnp.float32, jnp.bfloat16
    hi = a.astype(bf)
    r1 = a - hi.astype(f32)
    mid = r1.astype(bf)
    lo = (r1 - mid.astype(f32)).astype(bf)
    return hi, mid, lo


def _dot_lhs_split(a, b01):
    b = b01.astype(jnp.bfloat16)
    return sum(jnp.dot(t, b, preferred_element_type=jnp.float32) for t in _bf16_terms(a))


def _dot_rhs_split(a01, b):
    a = a01.astype(jnp.bfloat16)
    return sum(jnp.dot(a, t, preferred_element_type=jnp.float32) for t in _bf16_terms(b))


def _rwkv_kernel(z_ref, m0_ref, p0_ref, mu_ref, w0_ref, w2_ref, a0_ref, a2_ref, g2_ref, kk_ref, ka_ref, rk_ref,
                 gng_ref, gnb_ref, bd_ref, o_ref, mo_ref, po_ref, m_ref, prev_ref, y_ref):
    C = z_ref.shape[1]
    N = RW_HEAD_DIM
    f32 = jnp.float32
    dot3 = _dot_bf16x3

    @pl.when(pl.program_id(1) == 0)
    def _():
        m_ref[...] = m0_ref[0]
        prev_ref[...] = p0_ref[0]

    z = z_ref[0]
    row = lax.broadcasted_iota(jnp.int32, z.shape, 0)
    shifted = jnp.where(row == 0, prev_ref[...], pltpu.roll(z, 1, axis=0))
    prev_ref[...] = z[C - 1:C, :]
    zs = z + (shifted - z) * mu_ref[...]
    r = zs[:, 0:RW_DIM]
    k = zs[:, RW_DIM:2 * RW_DIM]
    v = zs[:, 2 * RW_DIM:3 * RW_DIM]
    o1 = 3 * RW_DIM
    wl = zs[:, o1:o1 + RW_DECAY_LORA]
    al = zs[:, o1 + RW_DECAY_LORA:o1 + RW_DECAY_LORA + RW_A_LORA]
    gl = zs[:, o1 + RW_DECAY_LORA + RW_A_LORA:]

    bd = bd_ref[...]
    log_w = -jax.nn.softplus(-(w0_ref[...] + dot3(jnp.tanh(wl), w2_ref[...]))) - 0.5
    ldec = -jnp.exp(log_w)
    a_lr = jax.nn.sigmoid(a0_ref[...] + dot3(al, a2_ref[...]))
    g = dot3(jax.nn.sigmoid(gl), g2_ref[...])
    kk = k * kk_ref[...]
    kk = kk * lax.rsqrt(jnp.maximum(_dot_lhs_split(kk * kk, bd), 1e-24))
    k2 = k * (1.0 + (a_lr - 1.0) * ka_ref[...])
    a_vec = -kk
    b_vec = kk * a_lr

    ti = lax.broadcasted_iota(jnp.int32, (C, C), 0)
    tj = lax.broadcasted_iota(jnp.int32, (C, C), 1)
    cum = _dot_rhs_split((ti >= tj).astype(f32), ldec)
    cum_last = cum[C - 1:C, :]
    w_incl = jnp.exp(cum)
    w_excl = jnp.exp(cum - ldec)
    w_inv = jnp.exp(-cum)
    w_end = jnp.exp(cum_last - cum)
    w_all = jnp.exp(cum_last)
    a_t = a_vec * w_excl
    r_t = r * w_incl
    b_t = b_vec * w_inv
    k_t = k2 * w_inv
    b_e = b_vec * w_end
    k_e = k2 * w_end

    strict = ti > tj
    incl = ti >= tj
    bi, bj = ti // RW_INV_BLOCK, tj // RW_INV_BLOCK
    same_blk = bi == bj
    pair_blk = jnp.logical_and((bi // 2) == (bj // 2), jnp.logical_not(same_blk))
    half_blk = (bi // 2) != (bj // 2)
    eye = (ti == tj).astype(f32)

    H = RW_HEADS
    heads = lambda x: jnp.stack([x[:, h * N:(h + 1) * N] for h in range(H)], axis=0)
    bmm = lambda a, b: dot3(a, b, _BNN)
    A, Rt, Bt, Kt, Be, Ke, V = (heads(t) for t in (a_t, r_t, b_t, k_t, b_e, k_e, v))
    gm = dot3(jnp.concatenate([A, Rt], axis=1), jnp.concatenate([Bt, Kt], axis=1), _BNT)
    a_ab = jnp.where(strict, gm[:, :C, :C], 0.0)
    a_ak = jnp.where(strict, gm[:, :C, C:], 0.0)
    a_rb = jnp.where(incl, gm[:, C:, :C], 0.0)
    a_rk = jnp.where(incl, gm[:, C:, C:], 0.0)
    d1 = jnp.where(same_blk, a_ab, 0.0)
    xinv = eye + d1
    d2 = bmm(d1, d1)
    xinv = xinv + bmm(xinv, d2)
    d4 = bmm(d2, d2)
    xinv = xinv + bmm(xinv, d4)
    d8 = bmm(d4, d4)
    xinv = xinv + bmm(xinv, d8)
    xinv = xinv + bmm(bmm(xinv, jnp.where(pair_blk, a_ab, 0.0)), xinv)
    xinv = xinv + bmm(bmm(xinv, jnp.where(half_blk, a_ab, 0.0)), xinv)

    av = bmm(jnp.concatenate([a_ak, a_rk], axis=1), V)
    p = bmm(xinv, jnp.concatenate([A, av[:, :C]], axis=2))
    qm = bmm(a_rb, p)
    q1 = Rt + qm[:, :, :N]
    q2 = qm[:, :, N:] + av[:, C:]
    gmat = bmm(jnp.swapaxes(Be, 1, 2), p)
    g1 = eye * heads(w_all) + gmat[:, :, :N]
    g2 = gmat[:, :, N:] + bmm(jnp.swapaxes(Ke, 1, 2), V)
    m = m_ref[...]
    yh = _dot_f32(q1, m, _BNN) + q2
    m_ref[...] = _dot_f32(g1, m, _BNN) + g2
    for h in range(H):
        y_ref[:, h * N:(h + 1) * N] = yh[h]

    y = y_ref[...]
    mean = _dot_lhs_split(y, bd) * (1.0 / N)
    yc = y - mean
    var = _dot_lhs_split(yc * yc, bd) * (1.0 / N)
    yn = yc * lax.rsqrt(var + RW_GN_EPS) * gng_ref[...] + gnb_ref[...]
    bonus = _dot_lhs_split(r * k2 * rk_ref[...], bd) * v
    o_ref[0] = ((yn + bonus) * g).astype(o_ref.dtype)

    @pl.when(pl.program_id(1) == pl.num_programs(1) - 1)
    def _():
        mo_ref[0] = m_ref[...]
        po_ref[0] = prev_ref[...]


def _rwkv7_init_carry(batch):
    return (jnp.zeros((batch, RW_HEADS, RW_HEAD_DIM, RW_HEAD_DIM), jnp.float32),
            jnp.zeros((batch, 1, RW_COLS), jnp.float32))


def _rwkv7_time_mix(z_rw, carry, mu, w0, w2, a0, a2, g2, k_k, k_a, r_k, gn_g, gn_b, out_dtype):
    B, S, _ = z_rw.shape
    C = RW_CHUNK
    assert S % C == 0 and C == RW_HEAD_DIM and C % (4 * RW_INV_BLOCK) == 0
    hid = jnp.arange(RW_DIM) // RW_HEAD_DIM
    bd = (hid[:, None] == hid[None, :]).astype(jnp.float32)
    row = lambda a: a.reshape(1, -1)
    full = lambda shape: pl.BlockSpec(shape, lambda b, c: (0,) * len(shape))
    state_spec = pl.BlockSpec((1, RW_HEADS, RW_HEAD_DIM, RW_HEAD_DIM), lambda b, c: (b, 0, 0, 0))
    prev_spec = pl.BlockSpec((1, 1, RW_COLS), lambda b, c: (b, 0, 0))
    y, m_out, p_out = pl.pallas_call(
        _rwkv_kernel,
        grid=(B, S // C),
        in_specs=[
            pl.BlockSpec((1, C, RW_COLS), lambda b, c: (b, c, 0)), state_spec, prev_spec,
            full((1, RW_COLS)), full((1, RW_DIM)), full((RW_DECAY_LORA, RW_DIM)), full((1, RW_DIM)),
            full((RW_A_LORA, RW_DIM)), full((RW_GATE_LORA, RW_DIM)), full((1, RW_DIM)), full((1, RW_DIM)),
            full((1, RW_DIM)), full((1, RW_DIM)), full((1, RW_DIM)), full((RW_DIM, RW_DIM)),
        ],
        out_specs=[pl.BlockSpec((1, C, RW_DIM), lambda b, c: (b, c, 0)), state_spec, prev_spec],
        out_shape=[jax.ShapeDtypeStruct((B, S, RW_DIM), out_dtype),
                   jax.ShapeDtypeStruct(carry[0].shape, jnp.float32),
                   jax.ShapeDtypeStruct(carry[1].shape, jnp.float32)],
        scratch_shapes=[
            pltpu.VMEM((RW_HEADS, RW_HEAD_DIM, RW_HEAD_DIM), jnp.float32),
            pltpu.VMEM((1, RW_COLS), jnp.float32),
            pltpu.VMEM((C, RW_DIM), jnp.float32),
        ],
        compiler_params=pltpu.CompilerParams(dimension_semantics=("parallel", "arbitrary")),
        name="rwkv7_time_mix",
    )(z_rw, carry[0], carry[1], row(mu), row(w0), w2, row(a0), a2, g2, row(k_k), row(k_a), row(r_k),
      row(gn_g), row(gn_b), bd)
    return y, (m_out, p_out)


DSA_TQ = 256
MASK_NEG = -1e30
INT_MIN = -2 ** 31
KEY_NEG_INF = -2139095041
THRESH_BITS = 32


def _dsa_prep_kernel(z_ref, kvg_ref, kig_ref, kib_ref, qi_ref, kv_ref, ki_ref):
    z = z_ref[0]
    qi_ref[0] = z[:, SMALL_QI:SMALL_KV].astype(qi_ref.dtype)
    kv = z[:, SMALL_KV:SMALL_KI]
    ms = jnp.mean(jnp.square(kv), -1, keepdims=True)
    kv_ref[0] = (kv * lax.rsqrt(ms + LN_EPS) * kvg_ref[...]).astype(kv_ref.dtype)
    ki = z[:, SMALL_KI:SMALL_WI]
    mu = jnp.mean(ki, -1, keepdims=True)
    var = jnp.mean(jnp.square(ki - mu), -1, keepdims=True)
    ki_ref[0] = ((ki - mu) * lax.rsqrt(var + LN_EPS) * kig_ref[...] + kib_ref[...]).astype(ki_ref.dtype)


def _dsa_prep(z_small, kv_g, ki_g, ki_b, tm=512):
    B, S, W = z_small.shape
    assert S % tm == 0
    return pl.pallas_call(
        _dsa_prep_kernel,
        grid=(B, S // tm),
        in_specs=[
            pl.BlockSpec((1, tm, W), lambda b, i: (b, i, 0)),
            pl.BlockSpec((1, DSA_LATENT), lambda b, i: (0, 0)),
            pl.BlockSpec((1, IDX_DIM), lambda b, i: (0, 0)),
            pl.BlockSpec((1, IDX_DIM), lambda b, i: (0, 0)),
        ],
        out_specs=[
            pl.BlockSpec((1, tm, IDX_HEADS * IDX_DIM), lambda b, i: (b, i, 0)),
            pl.BlockSpec((1, tm, DSA_LATENT), lambda b, i: (b, i, 0)),
            pl.BlockSpec((1, tm, IDX_DIM), lambda b, i: (b, i, 0)),
        ],
        out_shape=[
            jax.ShapeDtypeStruct((B, S, IDX_HEADS * IDX_DIM), MM_DTYPE),
            jax.ShapeDtypeStruct((B, S, DSA_LATENT), MM_DTYPE),
            jax.ShapeDtypeStruct((B, S, IDX_DIM), MM_DTYPE),
        ],
        name="dsa_prep",
    )(z_small, kv_g[None], ki_g[None], ki_b[None])


def _sortable_key(s):
    s = jnp.where(s == 0.0, 0.0, s)
    bits = pltpu.bitcast(s, jnp.int32)
    return bits ^ ((bits >> 31) & 0x7FFFFFFF)


def _col_count(mask_i32):
    tk, tq = mask_i32.shape
    return jnp.sum(mask_i32.reshape(tk // 8, 8, tq), axis=0)


def _dsa_kernel(bfar_ref, q_ref, qi_ref, wit_ref, kv_ref, kvt_ref, ki_ref, b0_ref, b1_ref, o_ref,
                key_ref, madd_ref, m_ref, l_ref, acc_ref, *, topk, q_off):
    tq = q_ref.shape[1]
    tk = tq
    i = pl.program_id(1) + q_off
    nj = i + 1
    f32 = jnp.float32
    krow = lax.broadcasted_iota(jnp.int32, (tk, tq), 0)
    qcol = lax.broadcasted_iota(jnp.int32, (tk, tq), 1)

    qi = qi_ref[0]
    wit = wit_ref[0] * (IDX_HEADS ** -0.5)

    def score_chunk(j, carry):
        off = pl.multiple_of(j * tk, tk)
        kc = ki_ref[0, pl.ds(off, tk), :]
        s = jnp.zeros((tk, tq), f32)
        for h in range(IDX_HEADS):
            d = lax.dot_general(kc, qi[:, h * IDX_DIM:(h + 1) * IDX_DIM],
                                (((1,), (1,)), ((), ())), preferred_element_type=f32)
            s = s + wit[h:h + 1, :] * jnp.maximum(d * (IDX_DIM ** -0.5), 0.0)
        causal = (krow + j * tk) <= (qcol + i * tq)
        s = jnp.where(causal, s, -jnp.inf)
        key_ref[j] = _sortable_key(s)
        return carry

    lax.fori_loop(0, nj, score_chunk, 0)

    def count_where(pred_fn):
        def body(j, acc):
            return acc + _col_count(pred_fn(key_ref[j], j).astype(jnp.int32))
        acc = lax.fori_loop(0, nj, body, jnp.zeros((8, tq), jnp.int32))
        return jnp.sum(acc, axis=0, keepdims=True)

    def bit_step(it, t_u):
        bit = THRESH_BITS - 1 - it
        cand_u = t_u | jnp.left_shift(jnp.int32(1), bit)
        cand = cand_u ^ INT_MIN
        cnt = count_where(lambda k, j: k >= cand)
        return jnp.where(cnt >= topk, cand_u, t_u)

    t_u = lax.fori_loop(0, THRESH_BITS, bit_step, jnp.zeros((1, tq), jnp.int32))
    thr = t_u ^ INT_MIN
    cnt_gt = count_where(lambda k, j: k > thr)
    cnt_ge = count_where(lambda k, j: k >= thr)
    is_neg = thr == KEY_NEG_INF
    need = jnp.logical_and(cnt_ge > topk, jnp.logical_not(is_neg))
    n_tie_take = topk - cnt_gt

    thr_open = jnp.where(is_neg, thr, thr - 1)
    any_need = jnp.max(need.astype(jnp.int32)) > 0

    @pl.when(jnp.logical_not(any_need))
    def _():
        def body(j, carry):
            madd_ref[j] = jnp.where(key_ref[j] > thr_open, 0.0, MASK_NEG)
            return carry
        lax.fori_loop(0, nj, body, 0)

    @pl.when(any_need)
    def _():
        s_len = tk * key_ref.shape[0]
        n_bits = max(1, int(math.ceil(math.log2(s_len))))

        def idx_step(it, p):
            bit = n_bits - 1 - it
            cand = p | jnp.left_shift(jnp.int32(1), bit)
            cnt = count_where(
                lambda k, j: jnp.where(k == thr, jnp.where((krow + j * tk) < cand, 1, 0), 0))
            return jnp.where(cnt < n_tie_take, cand, p)

        p_idx = lax.fori_loop(0, n_bits, idx_step, jnp.zeros((1, tq), jnp.int32))
        p_idx = jnp.where(need, p_idx, jnp.where(is_neg, -1, s_len))

        def body(j, carry):
            k = key_ref[j]
            tie_ok = jnp.where((krow + j * tk) <= p_idx, 0.0, MASK_NEG)
            madd_ref[j] = jnp.where(k > thr, 0.0, jnp.where(k == thr, tie_ok, MASK_NEG))
            return carry
        lax.fori_loop(0, nj, body, 0)

    H, d = DSA_HEADS, DSA_LATENT
    m_ref[...] = jnp.full(m_ref.shape, MASK_NEG, f32)
    l_ref[...] = jnp.zeros(l_ref.shape, f32)
    acc_ref[...] = jnp.zeros(acc_ref.shape, f32)
    scale = DSA_LATENT ** -0.5
    q_all = jnp.concatenate([q_ref[0, :, h * d:(h + 1) * d] for h in range(H)], axis=0)

    def attend(j, bias):
        off = pl.multiple_of(j * tk, tk)
        kc = kv_ref[0, pl.ds(off, tk), :]
        kct = kvt_ref[0, :, pl.ds(off, tk)]
        lg = lax.dot_general(kc, q_all, (((1,), (1,)), ((), ())), preferred_element_type=f32)
        lg = lg * scale + bias + jnp.tile(madd_ref[j], (1, H))
        m_old = m_ref[...]
        m_new = jnp.maximum(m_old, jnp.max(lg, axis=0, keepdims=True))
        alpha = jnp.exp(m_old - m_new)
        p = jnp.exp(lg - m_new)
        l_ref[...] = alpha * l_ref[...] + jnp.sum(p, axis=0, keepdims=True)
        acc_ref[...] = alpha * acc_ref[...] + jnp.dot(kct, p.astype(kct.dtype), preferred_element_type=f32)
        m_ref[...] = m_new

    def far_body(j, carry):
        attend(j, bfar_ref[...])
        return carry

    lax.fori_loop(0, jnp.maximum(i - 1, 0), far_body, 0)

    @pl.when(i >= 1)
    def _():
        attend(i - 1, b1_ref[...])

    attend(i, b0_ref[...])

    out_t = acc_ref[...] / l_ref[...]
    for h in range(H):
        o_ref[0, :, h * d:(h + 1) * d] = out_t[:, h * tq:(h + 1) * tq].T.astype(o_ref.dtype)


def _dsa_bias_tiles(rel_bias, tq):
    dist = jnp.arange(2 * tq, dtype=jnp.int32)
    by_dist = rel_bias[_t5_bucket(dist)].T

    def toeplitz(v):
        H, P = v.shape
        skew = jnp.tile(v, (1, tq + 1))[:, :tq * (P + 1)].reshape(H, tq, P + 1)
        return skew[:, ::-1, :tq]

    b0 = toeplitz(jnp.concatenate([jnp.repeat(by_dist[:, :1], tq - 1, axis=1), by_dist[:, :tq + 1]], axis=1))
    b1 = toeplitz(jnp.concatenate([by_dist[:, 1:], by_dist[:, -1:]], axis=1))
    H = by_dist.shape[0]
    b0 = jnp.transpose(b0, (1, 0, 2)).reshape(tq, H * tq)
    b1 = jnp.transpose(b1, (1, 0, 2)).reshape(tq, H * tq)
    return b0, b1, jnp.repeat(by_dist[:, -1], tq)[None, :]


def _dsa_attention(q, qi, z_wi_t, ckv, kidx, bias_tiles, topk, q_off, out_dtype):
    B, Sq, _ = q.shape
    tq = DSA_TQ
    nq = Sq // tq
    S = ckv.shape[1]
    assert Sq % tq == 0 and S == (q_off + nq) * tq and tq >= REL_MAX_DIST and topk <= tq
    ckv_t = jnp.swapaxes(ckv, 1, 2)
    b0, b1, bfar = bias_tiles
    nk = S // tq
    H, d = DSA_HEADS, DSA_LATENT
    f32_bytes = 4
    vmem_estimate = (
        2 * nk * tq * tq * f32_bytes
        + 2 * 2 * tq * H * tq * f32_bytes
        + 2 * (2 * S * d + S * LANES) * 2
        + 6 * tq * H * tq * f32_bytes
        + 4 * tq * H * d * f32_bytes
    )
    grid_spec = pltpu.PrefetchScalarGridSpec(
        num_scalar_prefetch=0,
        grid=(B, nq),
        in_specs=[
            pl.BlockSpec((1, H * tq), lambda b, i: (0, 0)),
            pl.BlockSpec((1, tq, H * d), lambda b, i: (b, i, 0)),
            pl.BlockSpec((1, tq, IDX_HEADS * IDX_DIM), lambda b, i: (b, i, 0)),
            pl.BlockSpec((1, IDX_HEADS, tq), lambda b, i: (b, 0, i)),
            pl.BlockSpec((1, S, d), lambda b, i: (b, 0, 0)),
            pl.BlockSpec((1, d, S), lambda b, i: (b, 0, 0)),
            pl.BlockSpec((1, S, IDX_DIM), lambda b, i: (b, 0, 0)),
            pl.BlockSpec((tq, H * tq), lambda b, i: (0, 0)),
            pl.BlockSpec((tq, H * tq), lambda b, i: (0, 0)),
        ],
        out_specs=pl.BlockSpec((1, tq, H * d), lambda b, i: (b, i, 0)),
        scratch_shapes=[
            pltpu.VMEM((nk, tq, tq), jnp.int32),
            pltpu.VMEM((nk, tq, tq), jnp.float32),
            pltpu.VMEM((1, H * tq), jnp.float32),
            pltpu.VMEM((1, H * tq), jnp.float32),
            pltpu.VMEM((d, H * tq), jnp.float32),
        ],
    )
    return pl.pallas_call(
        partial(_dsa_kernel, topk=topk, q_off=q_off),
        grid_spec=grid_spec,
        out_shape=jax.ShapeDtypeStruct((B, Sq, H * d), out_dtype),
        compiler_params=pltpu.CompilerParams(vmem_limit_bytes=vmem_estimate),
        name="dsa_attention",
    )(bfar, q, qi, z_wi_t, ckv, ckv_t, kidx, b0, b1)


SC_CORES = 2
SC_SUBCORES = 16
SC_LANES = 16
PEER_SC_TOKENS = 8
PEER_SC_RING = 4
PEER_SC_UNROLL = 4
PEER_SLOTS = PEER_HEADS * PEER_TOPK


def _peer_sc_call(body, T, out_width, stage_width):
    mesh = plsc.VectorSubcoreMesh(core_axis_name="c", subcore_axis_name="s")
    return pl.kernel(
        body, mesh=mesh,
        out_type=jax.ShapeDtypeStruct((T, out_width), jnp.float32),
        scratch_types=[
            pltpu.VMEM((2, PEER_SC_TOKENS, PEER_SLOTS), jnp.int32),
            pltpu.VMEM((2, PEER_SC_TOKENS, stage_width), jnp.float32),
            pltpu.VMEM((PEER_SC_RING, PEER_TOPK, D_MODEL), jnp.float32),
            pltpu.VMEM((PEER_SC_TOKENS, out_width), jnp.float32),
            pltpu.SemaphoreType.DMA((PEER_SC_RING + 2,)),
        ],
        compiler_params=pltpu.CompilerParams(needs_layout_passes=False),
    )


def _peer_sc_body(compute, zero_out, x_hbm, idx_hbm, tab_hbm, out_hbm, idx_v, x_v, rows_v, out_v, sems):
    T = idx_hbm.shape[0]
    tokens_per_worker = T // (SC_CORES * SC_SUBCORES)
    n_blocks = tokens_per_worker // PEER_SC_TOKENS
    n_steps = PEER_SC_TOKENS * PEER_HEADS
    ring = PEER_SC_RING
    worker = lax.axis_index("s") * SC_CORES + lax.axis_index("c")
    base = worker * tokens_per_worker

    def stage(blk, slot):
        tok0 = base + blk * PEER_SC_TOKENS
        return (pltpu.make_async_copy(idx_hbm.at[pl.ds(tok0, PEER_SC_TOKENS)], idx_v.at[slot], sems.at[ring]),
                pltpu.make_async_copy(x_hbm.at[pl.ds(tok0, PEER_SC_TOKENS)], x_v.at[slot], sems.at[ring + 1]))

    def gather(slot, s, b):
        ids = idx_v[slot, s // PEER_HEADS, pl.ds((s % PEER_HEADS) * PEER_TOPK, PEER_TOPK)]
        return pltpu.make_async_copy(tab_hbm.at[ids], rows_v.at[b], sems.at[b])

    for copy in stage(0, 0):
        copy.start()
    for copy in stage(0, 0):
        copy.wait()
    for b in range(ring - 1):
        gather(0, b, b).start()

    @pl.loop(0, n_blocks)
    def _(blk):
        slot = blk % 2
        has_next = blk + 1 < n_blocks

        @pl.when(has_next)
        def _():
            for copy in stage(blk + 1, 1 - slot):
                copy.start()

        if zero_out:
            @pl.loop(0, PEER_SC_TOKENS)
            def _(t):
                @plsc.parallel_loop(0, out_v.shape[1] // SC_LANES, unroll=PEER_SC_UNROLL)
                def _(c):
                    out_v[t, pl.ds(pl.multiple_of(c * SC_LANES, SC_LANES), SC_LANES)] = (
                        jnp.zeros((SC_LANES,), jnp.float32))

        @pl.loop(0, n_steps, step=ring)
        def _(s0):
            for b in range(ring):
                s = s0 + b
                gather(slot, s, b).wait()
                ahead = s + ring - 1
                ahead_buf = (b + ring - 1) % ring

                @pl.when(ahead < n_steps)
                def _():
                    gather(slot, ahead, ahead_buf).start()

                if b >= 1:
                    @pl.when(jnp.logical_and(ahead >= n_steps, has_next))
                    def _():
                        if b == 1:
                            for copy in stage(blk + 1, 1 - slot):
                                copy.wait()
                        gather(1 - slot, ahead - n_steps, ahead_buf).start()

                compute(slot, s // PEER_HEADS, s % PEER_HEADS, b, x_v, rows_v, out_v)

        pltpu.sync_copy(out_v, out_hbm.at[pl.ds(base + blk * PEER_SC_TOKENS, PEER_SC_TOKENS)])


def _peer_dots_compute(slot, t, hd, b, h_v, rows_v, dots_v):
    lane = lax.iota(jnp.int32, SC_LANES)

    def col_step(c, accs):
        off = pl.multiple_of(c * SC_LANES, SC_LANES)
        hv = h_v[slot, t, pl.ds(off, SC_LANES)]
        return tuple(accs[r] + rows_v[b, r, pl.ds(off, SC_LANES)] * hv for r in range(PEER_TOPK))

    accs = plsc.parallel_loop(
        0, D_MODEL // SC_LANES, unroll=PEER_SC_UNROLL,
        carry=tuple(jnp.zeros((SC_LANES,), jnp.float32) for _ in range(PEER_TOPK)))(col_step)
    res = jnp.zeros((SC_LANES,), jnp.float32)
    for r in range(PEER_TOPK):
        res = jnp.where(lane == r, jnp.sum(accs[r]), res)
    dots_v[t, pl.ds(hd * PEER_TOPK, PEER_TOPK)] = res


def _peer_mix_compute(slot, t, hd, b, act_v, rows_v, out_v):
    svec = jnp.full((SC_LANES,), slot, jnp.int32)
    tvec = jnp.full((SC_LANES,), t, jnp.int32)
    weights = [plsc.load_gather(act_v, [svec, tvec, jnp.full((SC_LANES,), hd * PEER_TOPK + r, jnp.int32)])
               for r in range(PEER_TOPK)]

    @plsc.parallel_loop(0, D_MODEL // SC_LANES, unroll=PEER_SC_UNROLL)
    def _(c):
        off = pl.multiple_of(c * SC_LANES, SC_LANES)
        a = rows_v[b, 0, pl.ds(off, SC_LANES)] * weights[0]
        for r in range(1, PEER_TOPK):
            a = a + rows_v[b, r, pl.ds(off, SC_LANES)] * weights[r]
        plsc.addupdate(out_v.at[t, pl.ds(off, SC_LANES)], a)


def _peer_expert_dots(h, experts, u_tab):
    T = h.shape[0]
    assert T % (SC_CORES * SC_SUBCORES * PEER_SC_TOKENS) == 0
    assert (PEER_SC_TOKENS * PEER_HEADS) % PEER_SC_RING == 0 and PEER_TOPK == SC_LANES
    body = partial(_peer_sc_body, _peer_dots_compute, False)
    return _peer_sc_call(body, T, PEER_SLOTS, D_MODEL)(h, experts, u_tab)


def _peer_expert_mix(act, experts, v_tab):
    T = act.shape[0]
    assert T % (SC_CORES * SC_SUBCORES * PEER_SC_TOKENS) == 0
    body = partial(_peer_sc_body, _peer_mix_compute, True)
    return _peer_sc_call(body, T, D_MODEL, PEER_SLOTS)(act, experts, v_tab)


PEER_TM = 256


def _extract_topk(s, k, payload=None):
    R = s.shape[0]
    riota = lax.broadcasted_iota(jnp.int32, s.shape, 0)
    vals, rows = [], []
    for _ in range(k):
        m = jnp.max(s, axis=0, keepdims=True)
        pos = jnp.min(jnp.where(s == m, riota, R), axis=0, keepdims=True)
        hit = riota == pos
        vals.append(m)
        rows.append(pos if payload is None else jnp.max(jnp.where(hit, payload, -1), axis=0, keepdims=True))
        s = jnp.where(hit, -jnp.inf, s)
    return jnp.concatenate(vals, axis=0), jnp.concatenate(rows, axis=0)


def _peer_route_kernel(x_ref, sc_ref, sh_ref, wq_ref, keys_ref, h_ref, ex_ref, gate_ref, q_ref, ext_ref, gt_ref):
    K = PEER_TOPK
    f32 = jnp.float32
    h = x_ref[0] * (1.0 + sc_ref[0]) + sh_ref[0]
    h_ref[0] = h
    q = jnp.dot(h.astype(wq_ref.dtype), wq_ref[...], preferred_element_type=f32)
    for hd in range(PEER_HEADS):
        q_ref[hd] = q[:, hd * PEER_KEY_DIM:(hd + 1) * PEER_KEY_DIM].astype(q_ref.dtype)

    def head_body(hd, carry):
        qh = q_ref[hd]
        s1 = lax.dot_general(keys_ref[hd, 0], qh[:, :PEER_HALF], _NT, preferred_element_type=f32)
        s2 = lax.dot_general(keys_ref[hd, 1], qh[:, PEER_HALF:], _NT, preferred_element_type=f32)
        v1, i1 = _extract_topk(s1, K)
        v2, i2 = _extract_topk(s2, K)
        tm = v1.shape[1]
        cand_rows, cidx_rows = [], []
        for a in range(K):
            nb = K // (a + 1)
            cand_rows.append(v1[a:a + 1, :] + v2[:nb, :])
            cidx_rows.append(i1[a:a + 1, :] * PEER_N_KEYS + i2[:nb, :])
        n_pad = -sum(r.shape[0] for r in cand_rows) % 8
        cand_rows.append(jnp.full((n_pad, tm), -jnp.inf, f32))
        cidx_rows.append(jnp.full((n_pad, tm), -1, jnp.int32))
        cand = jnp.concatenate(cand_rows, axis=0)
        cidx = jnp.concatenate(cidx_rows, axis=0)
        top_s, experts = _extract_topk(cand, K, payload=cidx)
        e = jnp.exp(top_s - top_s[0:1, :])
        gt_ref[pl.ds(pl.multiple_of(hd * K, K), K), :] = e / jnp.sum(e, axis=0, keepdims=True)
        ext_ref[pl.ds(pl.multiple_of(hd * K, K), K), :] = experts
        return carry

    lax.fori_loop(0, PEER_HEADS, head_body, 0)
    ex_ref[0] = ext_ref[...].T
    gate_ref[0] = gt_ref[...].T


def _peer_route(x, sc, sh, w_pq, sub_keys):
    B, S, D = x.shape
    tm = PEER_TM
    assert S % tm == 0
    return pl.pallas_call(
        _peer_route_kernel,
        grid=(B, S // tm),
        in_specs=[
            pl.BlockSpec((1, tm, D), lambda b, i: (b, i, 0)),
            pl.BlockSpec((1, 1, D), lambda b, i: (b, 0, 0)),
            pl.BlockSpec((1, 1, D), lambda b, i: (b, 0, 0)),
            pl.BlockSpec((D, PEER_HEADS * PEER_KEY_DIM), lambda b, i: (0, 0)),
            pl.BlockSpec((PEER_HEADS, 2, PEER_N_KEYS, PEER_HALF), lambda b, i: (0, 0, 0, 0)),
        ],
        out_specs=[
            pl.BlockSpec((1, tm, D), lambda b, i: (b, i, 0)),
            pl.BlockSpec((1, tm, PEER_SLOTS), lambda b, i: (b, i, 0)),
            pl.BlockSpec((1, tm, PEER_SLOTS), lambda b, i: (b, i, 0)),
        ],
        out_shape=[
            jax.ShapeDtypeStruct((B, S, D), jnp.float32),
            jax.ShapeDtypeStruct((B, S, PEER_SLOTS), jnp.int32),
            jax.ShapeDtypeStruct((B, S, PEER_SLOTS), jnp.float32),
        ],
        scratch_shapes=[
            pltpu.VMEM((PEER_HEADS, tm, PEER_KEY_DIM), MM_DTYPE),
            pltpu.VMEM((PEER_SLOTS, tm), jnp.int32),
            pltpu.VMEM((PEER_SLOTS, tm), jnp.float32),
        ],
        compiler_params=pltpu.CompilerParams(dimension_semantics=("parallel", "parallel")),
        name="peer_route",
    )(x, sc[:, None, :], sh[:, None, :], w_pq.astype(MM_DTYPE), sub_keys.astype(MM_DTYPE))


def _peer_pre(h2, experts, u_tab):
    B, S, D = h2.shape
    return _peer_expert_dots(h2.reshape(B * S, D), experts.reshape(B * S, PEER_SLOTS), u_tab)


def _peer_act_kernel(pre_ref, gate_ref, o_ref):
    pre = pre_ref[...]
    o_ref[...] = 0.5 * pre * (1.0 + lax.erf(pre * (2.0 ** -0.5))) * gate_ref[...]


def _peer_act(pre, gates):
    T, W = pre.shape
    tm = min(8 * ROW_TILE, T)
    assert T % tm == 0
    spec = pl.BlockSpec((tm, W), lambda i: (i, 0))
    return pl.pallas_call(
        _peer_act_kernel,
        grid=(T // tm,),
        in_specs=[spec, spec],
        out_specs=spec,
        out_shape=jax.ShapeDtypeStruct((T, W), jnp.float32),
        name="peer_act",
    )(pre, gates.reshape(T, W))


def _peer_out(act, experts, v_tab):
    B, S, _ = experts.shape
    return _peer_expert_mix(act, experts.reshape(B * S, PEER_SLOTS), v_tab).reshape(B, S, D_MODEL)


def kernel(x, c, w_ada, b_ada, w_in, rw_mu, rw_w0, rw_w2, rw_a0, rw_a2, rw_g2, rw_k_k, rw_k_a, rw_r_k, rw_gn_g, rw_gn_b, dsa_kv_g, idx_k_g, idx_k_b, rel_bias, w_br_a, w_br_b, w_out, ln1_g, ln1_b, peer_wq, peer_keys, peer_u, peer_v, ln2_g, ln2_b):
    assert w_in.shape[0] == DEPTH == 1
    l = 0
    mod = jax.nn.silu(c) @ w_ada[l] + b_ada[l]

    w_rw, w_q, w_kv, w_qi, w_ki, w_wi, w_ga, w_gb = jnp.split(w_in[l], _split_points(IN_SIZES), axis=-1)
    small_pad = jnp.zeros((D_MODEL, SMALL_COLS - SMALL_WI - IDX_HEADS), w_in.dtype)
    w_small = jnp.concatenate([w_qi, w_kv, w_ki, w_wi, small_pad], axis=-1)
    w_gates = jnp.concatenate([w_ga, w_gb], axis=-1)

    B, S, _ = x.shape
    seg_len = S // SEQ_SEGMENTS
    dsa_topk = min(TOPK_MAX, S // 4)
    bias_tiles = _dsa_bias_tiles(rel_bias, DSA_TQ)

    def project_and_dsa(x, sc1, sh1, seg, keys_so_far):
        z_rw = _mod_matmul(x, sc1, sh1, w_rw, jnp.float32, tn=896)
        z_q = _mod_matmul(x, sc1, sh1, w_q, MM_DTYPE, tn=1024)
        z_small = _mod_matmul(x, sc1, sh1, w_small, jnp.float32, tn=512)
        z_g = _mod_matmul(x, sc1, sh1, w_gates, jnp.float32, tn=1024)
        z_wi_t = jnp.swapaxes(z_small[..., SMALL_WI:SMALL_WI + IDX_HEADS], 1, 2)
        qi, ckv, kidx = _dsa_prep(z_small, dsa_kv_g[l], idx_k_g[l], idx_k_b[l])
        if keys_so_far is not None:
            ckv = jnp.concatenate([keys_so_far[0], ckv], axis=1)
            kidx = jnp.concatenate([keys_so_far[1], kidx], axis=1)
        y_b = _dsa_attention(z_q, qi, z_wi_t, ckv, kidx, bias_tiles, dsa_topk,
                             seg * (seg_len // DSA_TQ), MM_DTYPE)
        return (y_b, z_rw, z_g), (ckv, kidx)

    def rwkv_and_merge(x, gt1, carry, y_b, z_rw, z_g):
        y_a, carry = _rwkv7_time_mix(z_rw, carry, rw_mu[l], rw_w0[l], rw_w2[l], rw_a0[l], rw_a2[l], rw_g2[l],
                                     rw_k_k[l], rw_k_a[l], rw_r_k[l], rw_gn_g[l], rw_gn_b[l], MM_DTYPE)
        return _merge_ln(x, gt1, y_a, y_b, z_g, w_br_a[l], w_br_b[l], w_out[l], ln1_g[l], ln1_b[l]), carry

    def finish(st, act):
        y2 = _peer_out(act, st["experts"], peer_v[l])
        return _residual_ln(st["x1"], st["gt2"], y2, ln2_g[l], ln2_b[l])

    assert B % BATCH_GROUPS == 0 and S % (SEQ_SEGMENTS * DSA_TQ) == 0
    gsz = B // BATCH_GROUPS
    outs = []
    prev = None
    for g in range(BATCH_GROUPS):
        sh1, sc1, gt1, sh2, sc2, gt2 = jnp.split(mod[g * gsz:(g + 1) * gsz], 6, axis=-1)
        rw_carry = _rwkv7_init_carry(gsz)
        keys_so_far = None
        for seg in range(SEQ_SEGMENTS):
            x_g = x[g * gsz:(g + 1) * gsz, seg * seg_len:(seg + 1) * seg_len]
            if prev is not None:
                x_g, prev["experts"] = lax.optimization_barrier((x_g, prev["experts"]))
            proj, keys_so_far = project_and_dsa(x_g, sc1, sh1, seg, keys_so_far)
            if prev is not None:
                proj, prev["pre"], outs = lax.optimization_barrier((proj, prev["pre"], outs))
                outs.append(finish(prev, _peer_act(prev["pre"], prev["gates"])))
            x1, rw_carry = rwkv_and_merge(x_g, gt1, rw_carry, *proj)
            h2, experts, gates = _peer_route(x1, sc2, sh2, peer_wq[l], peer_keys[l])
            prev = dict(x1=x1, gt2=gt2, experts=experts, gates=gates, pre=_peer_pre(h2, experts, peer_u[l]))
    prev["pre"], outs = lax.optimization_barrier((prev["pre"], outs))
    outs.append(finish(prev, _peer_act(prev["pre"], prev["gates"])))
    rows = [jnp.concatenate(outs[g * SEQ_SEGMENTS:(g + 1) * SEQ_SEGMENTS], axis=1) for g in range(BATCH_GROUPS)]
    return jnp.concatenate(rows, axis=0)
```

```python
import math
from functools import partial

import jax
import jax.numpy as jnp
import numpy as np
from jax import lax
from jax.experimental import pallas as pl
from jax.experimental.pallas import tpu as pltpu
from jax.experimental.pallas import tpu_sc as plsc

D_MODEL = 1024
RW_HEADS = 8
RW_HEAD_DIM = 64
RW_DIM = 512
RW_DECAY_LORA = 64
RW_A_LORA = 64
RW_GATE_LORA = 128
RW_COLS = 3 * RW_DIM + RW_DECAY_LORA + RW_A_LORA + RW_GATE_LORA
RW_GN_EPS = 64e-5
DSA_HEADS = 8
DSA_LATENT = 128
DSA_Q_DIM = DSA_HEADS * DSA_LATENT
IDX_HEADS = 4
IDX_DIM = 64
TOPK_MAX = 256
REL_BUCKETS = 32
REL_MAX_DIST = 128
IN_SIZES = (RW_COLS, DSA_Q_DIM, DSA_LATENT, IDX_HEADS * IDX_DIM, IDX_DIM, IDX_HEADS, D_MODEL, D_MODEL)
IN_COLS = sum(IN_SIZES)
PEER_HEADS = 8
PEER_N_KEYS = 128
PEER_KEY_DIM = 128
PEER_HALF = 64
PEER_TOPK = 16
LN_EPS = 1e-5
DEPTH = 1
DEEPNORM_ALPHA = (2.0 * DEPTH) ** 0.25

LANES = 128
SMALL_QI = 0
SMALL_KV = SMALL_QI + IDX_HEADS * IDX_DIM
SMALL_KI = SMALL_KV + DSA_LATENT
SMALL_WI = SMALL_KI + IDX_DIM
SMALL_COLS = 4 * LANES
MM_DTYPE = jnp.bfloat16
BATCH_GROUPS = 8
SEQ_SEGMENTS = 2


def _split_points(sizes):
    return np.cumsum(sizes)[:-1].tolist()


def _mod_matmul_kernel(x_ref, sc_ref, sh_ref, w_ref, o_ref):
    h = x_ref[0] * (1.0 + sc_ref[0]) + sh_ref[0]
    o_ref[0] = jnp.dot(h.astype(w_ref.dtype), w_ref[...],
                       preferred_element_type=jnp.float32).astype(o_ref.dtype)


def _mod_matmul(x, window, sc, sh, w, out_dtype, tn, tm=512):
    b0, B, t0, S = window
    D = x.shape[2]
    N = w.shape[1]
    assert S % tm == 0 and t0 % tm == 0 and N % tn == 0
    return pl.pallas_call(
        _mod_matmul_kernel,
        grid=(B, S // tm, N // tn),
        in_specs=[
            pl.BlockSpec((1, tm, D), lambda b, i, j: (b + b0, i + t0 // tm, 0)),
            pl.BlockSpec((1, 1, D), lambda b, i, j: (b, 0, 0)),
            pl.BlockSpec((1, 1, D), lambda b, i, j: (b, 0, 0)),
            pl.BlockSpec((D, tn), lambda b, i, j: (0, j)),
        ],
        out_specs=pl.BlockSpec((1, tm, tn), lambda b, i, j: (b, i, j)),
        out_shape=jax.ShapeDtypeStruct((B, S, N), out_dtype),
        name="mod_matmul",
    )(x, sc[:, None, :], sh[:, None, :], w.astype(MM_DTYPE))


ROW_TILE = 256


def _layer_norm(x, g, b):
    mu = jnp.mean(x, -1, keepdims=True)
    var = jnp.mean(jnp.square(x - mu), -1, keepdims=True)
    return (x - mu) * lax.rsqrt(var + LN_EPS) * g + b


def _merge_ln_kernel(x_ref, gt_ref, ya_ref, yb_ref, zg_ref, wa_ref, wb_ref, wo_ref, g_ref, b_ref, o_ref):
    f32 = jnp.float32
    D = x_ref.shape[2]
    ya = jnp.dot(ya_ref[0], wa_ref[...], preferred_element_type=f32)
    yb = jnp.dot(yb_ref[0], wb_ref[...], preferred_element_type=f32)
    zg = zg_ref[0]
    merged = jax.nn.sigmoid(zg[:, :D]) * ya + jax.nn.sigmoid(zg[:, D:]) * yb
    mix = jnp.dot(merged.astype(wo_ref.dtype), wo_ref[...], preferred_element_type=f32)
    o_ref[0] = _layer_norm(DEEPNORM_ALPHA * x_ref[0] + gt_ref[0] * mix, g_ref[...], b_ref[...])


def _merge_ln(x, window, gt, y_a, y_b, z_g, w_a, w_b, w_o, ln_g, ln_b):
    b0, B, t0, S = window
    D = x.shape[2]
    tm = ROW_TILE
    assert S % tm == 0 and t0 % tm == 0
    tok = lambda width: pl.BlockSpec((1, tm, width), lambda b, i: (b, i, 0))
    x_spec = pl.BlockSpec((1, tm, D), lambda b, i: (b + b0, i + t0 // tm, 0))
    full = lambda a: pl.BlockSpec(a.shape, lambda b, i: (0,) * a.ndim)
    w_a, w_b, w_o = (w.astype(MM_DTYPE) for w in (w_a, w_b, w_o))
    ln_g, ln_b = ln_g[None], ln_b[None]
    return pl.pallas_call(
        _merge_ln_kernel,
        grid=(B, S // tm),
        in_specs=[x_spec, pl.BlockSpec((1, 1, D), lambda b, i: (b, 0, 0)), tok(y_a.shape[2]), tok(y_b.shape[2]),
                  tok(z_g.shape[2]), full(w_a), full(w_b), full(w_o), full(ln_g), full(ln_b)],
        out_specs=tok(D),
        out_shape=jax.ShapeDtypeStruct((B, S, D), jnp.float32),
        compiler_params=pltpu.CompilerParams(dimension_semantics=("parallel", "parallel")),
        name="merge_ln",
    )(x, gt[:, None, :], y_a, y_b, z_g, w_a, w_b, w_o, ln_g, ln_b)


def _residual_ln_kernel(x_ref, gt_ref, y_ref, g_ref, b_ref, o_ref):
    o_ref[0] = _layer_norm(DEEPNORM_ALPHA * x_ref[0] + gt_ref[0] * y_ref[0], g_ref[...], b_ref[...])


def _residual_ln(x, gt, y, ln_g, ln_b):
    B, S, D = x.shape
    tm = ROW_TILE
    assert S % tm == 0
    tok = pl.BlockSpec((1, tm, D), lambda b, i: (b, i, 0))
    vec = pl.BlockSpec((1, D), lambda b, i: (0, 0))
    return pl.pallas_call(
        _residual_ln_kernel,
        grid=(B, S // tm),
        in_specs=[tok, pl.BlockSpec((1, 1, D), lambda b, i: (b, 0, 0)), tok, vec, vec],
        out_specs=tok,
        out_shape=jax.ShapeDtypeStruct((B, S, D), jnp.float32),
        compiler_params=pltpu.CompilerParams(dimension_semantics=("parallel", "parallel")),
        name="residual_ln",
    )(x, gt[:, None, :], y, ln_g[None], ln_b[None])


def _t5_bucket(n):
    n = jnp.maximum(n, 0)
    max_exact = REL_BUCKETS // 2
    nf = jnp.maximum(n, 1).astype(jnp.float32)
    large = max_exact + (jnp.log(nf / max_exact) / math.log(REL_MAX_DIST / max_exact)
                         * (REL_BUCKETS - max_exact)).astype(jnp.int32)
    large = jnp.minimum(large, REL_BUCKETS - 1)
    return jnp.where(n < max_exact, n, large)


RW_CHUNK = 64
RW_INV_BLOCK = 16
_NN = (((1,), (0,)), ((), ()))
_NT = (((1,), (1,)), ((), ()))
_BNN = (((2,), (1,)), ((0,), (0,)))
_BNT = (((2,), (2,)), ((0,), (0,)))


def _dot_f32(a, b, dims=_NN):
    return lax.dot_general(a, b, dims, precision=lax.Precision.HIGHEST,
                           preferred_element_type=jnp.float32)


def _dot_bf16x3(a, b, dims=_NN):
    f32, bf = jnp.float32, jnp.bfloat16
    a_hi, b_hi = a.astype(bf), b.astype(bf)
    a_lo = (a - a_hi.astype(f32)).astype(bf)
    b_lo = (b - b_hi.astype(f32)).astype(bf)
    out = lax.dot_general(a_hi, b_hi, dims, preferred_element_type=f32)
    out = out + lax.dot_general(a_hi, b_lo, dims, preferred_element_type=f32)
    return out + lax.dot_general(a_lo, b_hi, dims, preferred_element_type=f32)


def _bf16_terms(a):
    f32, bf = jnp.float32, jnp.bfloat16
    hi = a.astype(bf)
    r1 = a - hi.astype(f32)
    mid = r1.astype(bf)
    lo = (r1 - mid.astype(f32)).astype(bf)
    return hi, mid, lo


def _dot_lhs_split(a, b01):
    b = b01.astype(jnp.bfloat16)
    return sum(jnp.dot(t, b, preferred_element_type=jnp.float32) for t in _bf16_terms(a))


def _dot_rhs_split(a01, b):
    a = a01.astype(jnp.bfloat16)
    return sum(jnp.dot(a, t, preferred_element_type=jnp.float32) for t in _bf16_terms(b))


def _rwkv_kernel(z_ref, m0_ref, p0_ref, mu_ref, w0_ref, w2_ref, a0_ref, a2_ref, g2_ref, kk_ref, ka_ref, rk_ref,
                 gng_ref, gnb_ref, bd_ref, o_ref, mo_ref, po_ref, m_ref, prev_ref, y_ref):
    C = z_ref.shape[1]
    N = RW_HEAD_DIM
    f32 = jnp.float32
    dot3 = _dot_bf16x3

    @pl.when(pl.program_id(1) == 0)
    def _():
        m_ref[...] = m0_ref[0]
        prev_ref[...] = p0_ref[0]

    z = z_ref[0]
    row = lax.broadcasted_iota(jnp.int32, z.shape, 0)
    shifted = jnp.where(row == 0, prev_ref[...], pltpu.roll(z, 1, axis=0))
    prev_ref[...] = z[C - 1:C, :]
    zs = z + (shifted - z) * mu_ref[...]
    r = zs[:, 0:RW_DIM]
    k = zs[:, RW_DIM:2 * RW_DIM]
    v = zs[:, 2 * RW_DIM:3 * RW_DIM]
    o1 = 3 * RW_DIM
    wl = zs[:, o1:o1 + RW_DECAY_LORA]
    al = zs[:, o1 + RW_DECAY_LORA:o1 + RW_DECAY_LORA + RW_A_LORA]
    gl = zs[:, o1 + RW_DECAY_LORA + RW_A_LORA:]

    bd = bd_ref[...]
    log_w = -jax.nn.softplus(-(w0_ref[...] + dot3(jnp.tanh(wl), w2_ref[...]))) - 0.5
    ldec = -jnp.exp(log_w)
    a_lr = jax.nn.sigmoid(a0_ref[...] + dot3(al, a2_ref[...]))
    g = dot3(jax.nn.sigmoid(gl), g2_ref[...])
    kk = k * kk_ref[...]
    kk = kk * lax.rsqrt(jnp.maximum(_dot_lhs_split(kk * kk, bd), 1e-24))
    k2 = k * (1.0 + (a_lr - 1.0) * ka_ref[...])
    a_vec = -kk
    b_vec = kk * a_lr

    ti = lax.broadcasted_iota(jnp.int32, (C, C), 0)
    tj = lax.broadcasted_iota(jnp.int32, (C, C), 1)
    cum = _dot_rhs_split((ti >= tj).astype(f32), ldec)
    cum_last = cum[C - 1:C, :]
    w_incl = jnp.exp(cum)
    w_excl = jnp.exp(cum - ldec)
    w_inv = jnp.exp(-cum)
    w_end = jnp.exp(cum_last - cum)
    w_all = jnp.exp(cum_last)
    a_t = a_vec * w_excl
    r_t = r * w_incl
    b_t = b_vec * w_inv
    k_t = k2 * w_inv
    b_e = b_vec * w_end
    k_e = k2 * w_end

    strict = ti > tj
    incl = ti >= tj
    bi, bj = ti // RW_INV_BLOCK, tj // RW_INV_BLOCK
    same_blk = bi == bj
    pair_blk = jnp.logical_and((bi // 2) == (bj // 2), jnp.logical_not(same_blk))
    half_blk = (bi // 2) != (bj // 2)
    eye = (ti == tj).astype(f32)

    H = RW_HEADS
    heads = lambda x: jnp.stack([x[:, h * N:(h + 1) * N] for h in range(H)], axis=0)
    bmm = lambda a, b: dot3(a, b, _BNN)
    A, Rt, Bt, Kt, Be, Ke, V = (heads(t) for t in (a_t, r_t, b_t, k_t, b_e, k_e, v))
    gm = dot3(jnp.concatenate([A, Rt], axis=1), jnp.concatenate([Bt, Kt], axis=1), _BNT)
    a_ab = jnp.where(strict, gm[:, :C, :C], 0.0)
    a_ak = jnp.where(strict, gm[:, :C, C:], 0.0)
    a_rb = jnp.where(incl, gm[:, C:, :C], 0.0)
    a_rk = jnp.where(incl, gm[:, C:, C:], 0.0)
    d1 = jnp.where(same_blk, a_ab, 0.0)
    xinv = eye + d1
    d2 = bmm(d1, d1)
    xinv = xinv + bmm(xinv, d2)
    d4 = bmm(d2, d2)
    xinv = xinv + bmm(xinv, d4)
    d8 = bmm(d4, d4)
    xinv = xinv + bmm(xinv, d8)
    xinv = xinv + bmm(bmm(xinv, jnp.where(pair_blk, a_ab, 0.0)), xinv)
    xinv = xinv + bmm(bmm(xinv, jnp.where(half_blk, a_ab, 0.0)), xinv)

    av = bmm(jnp.concatenate([a_ak, a_rk], axis=1), V)
    p = bmm(xinv, jnp.concatenate([A, av[:, :C]], axis=2))
    qm = bmm(a_rb, p)
    q1 = Rt + qm[:, :, :N]
    q2 = qm[:, :, N:] + av[:, C:]
    gmat = bmm(jnp.swapaxes(Be, 1, 2), p)
    g1 = eye * heads(w_all) + gmat[:, :, :N]
    g2 = gmat[:, :, N:] + bmm(jnp.swapaxes(Ke, 1, 2), V)
    m = m_ref[...]
    yh = _dot_f32(q1, m, _BNN) + q2
    m_ref[...] = _dot_f32(g1, m, _BNN) + g2
    for h in range(H):
        y_ref[:, h * N:(h + 1) * N] = yh[h]

    y = y_ref[...]
    mean = _dot_lhs_split(y, bd) * (1.0 / N)
    yc = y - mean
    var = _dot_lhs_split(yc * yc, bd) * (1.0 / N)
    yn = yc * lax.rsqrt(var + RW_GN_EPS) * gng_ref[...] + gnb_ref[...]
    bonus = _dot_lhs_split(r * k2 * rk_ref[...], bd) * v
    o_ref[0] = ((yn + bonus) * g).astype(o_ref.dtype)

    @pl.when(pl.program_id(1) == pl.num_programs(1) - 1)
    def _():
        mo_ref[0] = m_ref[...]
        po_ref[0] = prev_ref[...]


def _rwkv7_init_carry(batch):
    return (jnp.zeros((batch, RW_HEADS, RW_HEAD_DIM, RW_HEAD_DIM), jnp.float32),
            jnp.zeros((batch, 1, RW_COLS), jnp.float32))


def _rwkv7_time_mix(z_rw, carry, mu, w0, w2, a0, a2, g2, k_k, k_a, r_k, gn_g, gn_b, out_dtype):
    B, S, _ = z_rw.shape
    C = RW_CHUNK
    assert S % C == 0 and C == RW_HEAD_DIM and C % (4 * RW_INV_BLOCK) == 0
    hid = jnp.arange(RW_DIM) // RW_HEAD_DIM
    bd = (hid[:, None] == hid[None, :]).astype(jnp.float32)
    row = lambda a: a.reshape(1, -1)
    full = lambda shape: pl.BlockSpec(shape, lambda b, c: (0,) * len(shape))
    state_spec = pl.BlockSpec((1, RW_HEADS, RW_HEAD_DIM, RW_HEAD_DIM), lambda b, c: (b, 0, 0, 0))
    prev_spec = pl.BlockSpec((1, 1, RW_COLS), lambda b, c: (b, 0, 0))
    y, m_out, p_out = pl.pallas_call(
        _rwkv_kernel,
        grid=(B, S // C),
        in_specs=[
            pl.BlockSpec((1, C, RW_COLS), lambda b, c: (b, c, 0)), state_spec, prev_spec,
            full((1, RW_COLS)), full((1, RW_DIM)), full((RW_DECAY_LORA, RW_DIM)), full((1, RW_DIM)),
            full((RW_A_LORA, RW_DIM)), full((RW_GATE_LORA, RW_DIM)), full((1, RW_DIM)), full((1, RW_DIM)),
            full((1, RW_DIM)), full((1, RW_DIM)), full((1, RW_DIM)), full((RW_DIM, RW_DIM)),
        ],
        out_specs=[pl.BlockSpec((1, C, RW_DIM), lambda b, c: (b, c, 0)), state_spec, prev_spec],
        out_shape=[jax.ShapeDtypeStruct((B, S, RW_DIM), out_dtype),
                   jax.ShapeDtypeStruct(carry[0].shape, jnp.float32),
                   jax.ShapeDtypeStruct(carry[1].shape, jnp.float32)],
        scratch_shapes=[
            pltpu.VMEM((RW_HEADS, RW_HEAD_DIM, RW_HEAD_DIM), jnp.float32),
            pltpu.VMEM((1, RW_COLS), jnp.float32),
            pltpu.VMEM((C, RW_DIM), jnp.float32),
        ],
        compiler_params=pltpu.CompilerParams(dimension_semantics=("parallel", "arbitrary")),
        name="rwkv7_time_mix",
    )(z_rw, carry[0], carry[1], row(mu), row(w0), w2, row(a0), a2, g2, row(k_k), row(k_a), row(r_k),
      row(gn_g), row(gn_b), bd)
    return y, (m_out, p_out)


DSA_TQ = 256
MASK_NEG = -1e30
INT_MIN = -2 ** 31
KEY_NEG_INF = -2139095041
THRESH_BITS = 32


def _dsa_prep_kernel(z_ref, kvg_ref, kig_ref, kib_ref, qi_ref, kv_ref, ki_ref):
    z = z_ref[0]
    qi_ref[0] = z[:, SMALL_QI:SMALL_KV].astype(qi_ref.dtype)
    kv = z[:, SMALL_KV:SMALL_KI]
    ms = jnp.mean(jnp.square(kv), -1, keepdims=True)
    kv_ref[0] = (kv * lax.rsqrt(ms + LN_EPS) * kvg_ref[...]).astype(kv_ref.dtype)
    ki = z[:, SMALL_KI:SMALL_WI]
    mu = jnp.mean(ki, -1, keepdims=True)
    var = jnp.mean(jnp.square(ki - mu), -1, keepdims=True)
    ki_ref[0] = ((ki - mu) * lax.rsqrt(var + LN_EPS) * kig_ref[...] + kib_ref[...]).astype(ki_ref.dtype)


def _dsa_prep(z_small, kv_g, ki_g, ki_b, tm=512):
    B, S, W = z_small.shape
    assert S % tm == 0
    return pl.pallas_call(
        _dsa_prep_kernel,
        grid=(B, S // tm),
        in_specs=[
            pl.BlockSpec((1, tm, W), lambda b, i: (b, i, 0)),
            pl.BlockSpec((1, DSA_LATENT), lambda b, i: (0, 0)),
            pl.BlockSpec((1, IDX_DIM), lambda b, i: (0, 0)),
            pl.BlockSpec((1, IDX_DIM), lambda b, i: (0, 0)),
        ],
        out_specs=[
            pl.BlockSpec((1, tm, IDX_HEADS * IDX_DIM), lambda b, i: (b, i, 0)),
            pl.BlockSpec((1, tm, DSA_LATENT), lambda b, i: (b, i, 0)),
            pl.BlockSpec((1, tm, IDX_DIM), lambda b, i: (b, i, 0)),
        ],
        out_shape=[
            jax.ShapeDtypeStruct((B, S, IDX_HEADS * IDX_DIM), MM_DTYPE),
            jax.ShapeDtypeStruct((B, S, DSA_LATENT), MM_DTYPE),
            jax.ShapeDtypeStruct((B, S, IDX_DIM), MM_DTYPE),
        ],
        name="dsa_prep",
    )(z_small, kv_g[None], ki_g[None], ki_b[None])


def _sortable_key(s):
    s = jnp.where(s == 0.0, 0.0, s)
    bits = pltpu.bitcast(s, jnp.int32)
    return bits ^ ((bits >> 31) & 0x7FFFFFFF)


def _col_count(mask_i32):
    tk, tq = mask_i32.shape
    return jnp.sum(mask_i32.reshape(tk // 8, 8, tq), axis=0)


def _dsa_kernel(bfar_ref, q_ref, qi_ref, wit_ref, kv_ref, kvt_ref, ki_ref, b0_ref, b1_ref, o_ref,
                key_ref, madd_ref, m_ref, l_ref, acc_ref, *, topk, q_off):
    tq = q_ref.shape[1]
    tk = tq
    i = pl.program_id(1) + q_off
    nj = i + 1
    f32 = jnp.float32
    krow = lax.broadcasted_iota(jnp.int32, (tk, tq), 0)
    qcol = lax.broadcasted_iota(jnp.int32, (tk, tq), 1)

    qi = qi_ref[0]
    wit = wit_ref[0] * (IDX_HEADS ** -0.5)

    def score_chunk(j, carry):
        off = pl.multiple_of(j * tk, tk)
        kc = ki_ref[0, pl.ds(off, tk), :]
        s = jnp.zeros((tk, tq), f32)
        for h in range(IDX_HEADS):
            d = lax.dot_general(kc, qi[:, h * IDX_DIM:(h + 1) * IDX_DIM],
                                (((1,), (1,)), ((), ())), preferred_element_type=f32)
            s = s + wit[h:h + 1, :] * jnp.maximum(d * (IDX_DIM ** -0.5), 0.0)
        causal = (krow + j * tk) <= (qcol + i * tq)
        s = jnp.where(causal, s, -jnp.inf)
        key_ref[j] = _sortable_key(s)
        return carry

    lax.fori_loop(0, nj, score_chunk, 0)

    def count_where(pred_fn):
        def body(j, acc):
            return acc + _col_count(pred_fn(key_ref[j], j).astype(jnp.int32))
        acc = lax.fori_loop(0, nj, body, jnp.zeros((8, tq), jnp.int32))
        return jnp.sum(acc, axis=0, keepdims=True)

    def bit_step(it, t_u):
        bit = THRESH_BITS - 1 - it
        cand_u = t_u | jnp.left_shift(jnp.int32(1), bit)
        cand = cand_u ^ INT_MIN
        cnt = count_where(lambda k, j: k >= cand)
        return jnp.where(cnt >= topk, cand_u, t_u)

    t_u = lax.fori_loop(0, THRESH_BITS, bit_step, jnp.zeros((1, tq), jnp.int32))
    thr = t_u ^ INT_MIN
    cnt_gt = count_where(lambda k, j: k > thr)
    cnt_ge = count_where(lambda k, j: k >= thr)
    is_neg = thr == KEY_NEG_INF
    need = jnp.logical_and(cnt_ge > topk, jnp.logical_not(is_neg))
    n_tie_take = topk - cnt_gt

    thr_open = jnp.where(is_neg, thr, thr - 1)
    any_need = jnp.max(need.astype(jnp.int32)) > 0

    @pl.when(jnp.logical_not(any_need))
    def _():
        def body(j, carry):
            madd_ref[j] = jnp.where(key_ref[j] > thr_open, 0.0, MASK_NEG)
            return carry
        lax.fori_loop(0, nj, body, 0)

    @pl.when(any_need)
    def _():
        s_len = tk * key_ref.shape[0]
        n_bits = max(1, int(math.ceil(math.log2(s_len))))

        def idx_step(it, p):
            bit = n_bits - 1 - it
            cand = p | jnp.left_shift(jnp.int32(1), bit)
            cnt = count_where(
                lambda k, j: jnp.where(k == thr, jnp.where((krow + j * tk) < cand, 1, 0), 0))
            return jnp.where(cnt < n_tie_take, cand, p)

        p_idx = lax.fori_loop(0, n_bits, idx_step, jnp.zeros((1, tq), jnp.int32))
        p_idx = jnp.where(need, p_idx, jnp.where(is_neg, -1, s_len))

        def body(j, carry):
            k = key_ref[j]
            tie_ok = jnp.where((krow + j * tk) <= p_idx, 0.0, MASK_NEG)
            madd_ref[j] = jnp.where(k > thr, 0.0, jnp.where(k == thr, tie_ok, MASK_NEG))
            return carry
        lax.fori_loop(0, nj, body, 0)

    H, d = DSA_HEADS, DSA_LATENT
    m_ref[...] = jnp.full(m_ref.shape, MASK_NEG, f32)
    l_ref[...] = jnp.zeros(l_ref.shape, f32)
    acc_ref[...] = jnp.zeros(acc_ref.shape, f32)
    scale = DSA_LATENT ** -0.5
    q_all = jnp.concatenate([q_ref[0, :, h * d:(h + 1) * d] for h in range(H)], axis=0)

    def attend(j, bias):
        off = pl.multiple_of(j * tk, tk)
        kc = kv_ref[0, pl.ds(off, tk), :]
        kct = kvt_ref[0, :, pl.ds(off, tk)]
        lg = lax.dot_general(kc, q_all, (((1,), (1,)), ((), ())), preferred_element_type=f32)
        lg = lg * scale + bias + jnp.tile(madd_ref[j], (1, H))
        m_old = m_ref[...]
        m_new = jnp.maximum(m_old, jnp.max(lg, axis=0, keepdims=True))
        alpha = jnp.exp(m_old - m_new)
        p = jnp.exp(lg - m_new)
        l_ref[...] = alpha * l_ref[...] + jnp.sum(p, axis=0, keepdims=True)
        acc_ref[...] = alpha * acc_ref[...] + jnp.dot(kct, p.astype(kct.dtype), preferred_element_type=f32)
        m_ref[...] = m_new

    def far_body(j, carry):
        attend(j, bfar_ref[...])
        return carry

    lax.fori_loop(0, jnp.maximum(i - 1, 0), far_body, 0)

    @pl.when(i >= 1)
    def _():
        attend(i - 1, b1_ref[...])

    attend(i, b0_ref[...])

    out_t = acc_ref[...] / l_ref[...]
    for h in range(H):
        o_ref[0, :, h * d:(h + 1) * d] = out_t[:, h * tq:(h + 1) * tq].T.astype(o_ref.dtype)


def _dsa_bias_tiles(rel_bias, tq):
    dist = jnp.arange(2 * tq, dtype=jnp.int32)
    by_dist = rel_bias[_t5_bucket(dist)].T

    def toeplitz(v):
        H, P = v.shape
        skew = jnp.tile(v, (1, tq + 1))[:, :tq * (P + 1)].reshape(H, tq, P + 1)
        return skew[:, ::-1, :tq]

    b0 = toeplitz(jnp.concatenate([jnp.repeat(by_dist[:, :1], tq - 1, axis=1), by_dist[:, :tq + 1]], axis=1))
    b1 = toeplitz(jnp.concatenate([by_dist[:, 1:], by_dist[:, -1:]], axis=1))
    H = by_dist.shape[0]
    b0 = jnp.transpose(b0, (1, 0, 2)).reshape(tq, H * tq)
    b1 = jnp.transpose(b1, (1, 0, 2)).reshape(tq, H * tq)
    return b0, b1, jnp.repeat(by_dist[:, -1], tq)[None, :]


def _dsa_attention(q, qi, z_wi_t, ckv, kidx, bias_tiles, topk, q_off, out_dtype):
    B, Sq, _ = q.shape
    tq = DSA_TQ
    nq = Sq // tq
    S = ckv.shape[1]
    assert Sq % tq == 0 and S == (q_off + nq) * tq and tq >= REL_MAX_DIST and topk <= tq
    ckv_t = jnp.swapaxes(ckv, 1, 2)
    b0, b1, bfar = bias_tiles
    nk = S // tq
    H, d = DSA_HEADS, DSA_LATENT
    f32_bytes = 4
    vmem_estimate = (
        2 * nk * tq * tq * f32_bytes
        + 2 * 2 * tq * H * tq * f32_bytes
        + 2 * (2 * S * d + S * LANES) * 2
        + 6 * tq * H * tq * f32_bytes
        + 4 * tq * H * d * f32_bytes
    )
    grid_spec = pltpu.PrefetchScalarGridSpec(
        num_scalar_prefetch=0,
        grid=(B, nq),
        in_specs=[
            pl.BlockSpec((1, H * tq), lambda b, i: (0, 0)),
            pl.BlockSpec((1, tq, H * d), lambda b, i: (b, i, 0)),
            pl.BlockSpec((1, tq, IDX_HEADS * IDX_DIM), lambda b, i: (b, i, 0)),
            pl.BlockSpec((1, IDX_HEADS, tq), lambda b, i: (b, 0, i)),
            pl.BlockSpec((1, S, d), lambda b, i: (b, 0, 0)),
            pl.BlockSpec((1, d, S), lambda b, i: (b, 0, 0)),
            pl.BlockSpec((1, S, IDX_DIM), lambda b, i: (b, 0, 0)),
            pl.BlockSpec((tq, H * tq), lambda b, i: (0, 0)),
            pl.BlockSpec((tq, H * tq), lambda b, i: (0, 0)),
        ],
        out_specs=pl.BlockSpec((1, tq, H * d), lambda b, i: (b, i, 0)),
        scratch_shapes=[
            pltpu.VMEM((nk, tq, tq), jnp.int32),
            pltpu.VMEM((nk, tq, tq), jnp.float32),
            pltpu.VMEM((1, H * tq), jnp.float32),
            pltpu.VMEM((1, H * tq), jnp.float32),
            pltpu.VMEM((d, H * tq), jnp.float32),
        ],
    )
    return pl.pallas_call(
        partial(_dsa_kernel, topk=topk, q_off=q_off),
        grid_spec=grid_spec,
        out_shape=jax.ShapeDtypeStruct((B, Sq, H * d), out_dtype),
        compiler_params=pltpu.CompilerParams(vmem_limit_bytes=vmem_estimate),
        name="dsa_attention",
    )(bfar, q, qi, z_wi_t, ckv, ckv_t, kidx, b0, b1)


SC_CORES = 2
SC_SUBCORES = 16
SC_LANES = 16
PEER_SC_TOKENS = 8
PEER_SC_RING = 4
PEER_SC_UNROLL = 4
PEER_SLOTS = PEER_HEADS * PEER_TOPK


def _peer_sc_call(body, T, out_width, stage_width):
    mesh = plsc.VectorSubcoreMesh(core_axis_name="c", subcore_axis_name="s")
    return pl.kernel(
        body, mesh=mesh,
        out_type=jax.ShapeDtypeStruct((T, out_width), jnp.float32),
        scratch_types=[
            pltpu.VMEM((2, PEER_SC_TOKENS, PEER_SLOTS), jnp.int32),
            pltpu.VMEM((2, PEER_SC_TOKENS, stage_width), jnp.float32),
            pltpu.VMEM((PEER_SC_RING, PEER_TOPK, D_MODEL), jnp.float32),
            pltpu.VMEM((PEER_SC_TOKENS, out_width), jnp.float32),
            pltpu.SemaphoreType.DMA((PEER_SC_RING + 2,)),
        ],
        compiler_params=pltpu.CompilerParams(needs_layout_passes=False),
    )


def _peer_sc_body(compute, zero_out, x_hbm, idx_hbm, tab_hbm, out_hbm, idx_v, x_v, rows_v, out_v, sems):
    T = idx_hbm.shape[0]
    tokens_per_worker = T // (SC_CORES * SC_SUBCORES)
    n_blocks = tokens_per_worker // PEER_SC_TOKENS
    n_steps = PEER_SC_TOKENS * PEER_HEADS
    ring = PEER_SC_RING
    worker = lax.axis_index("s") * SC_CORES + lax.axis_index("c")
    base = worker * tokens_per_worker

    def stage(blk, slot):
        tok0 = base + blk * PEER_SC_TOKENS
        return (pltpu.make_async_copy(idx_hbm.at[pl.ds(tok0, PEER_SC_TOKENS)], idx_v.at[slot], sems.at[ring]),
                pltpu.make_async_copy(x_hbm.at[pl.ds(tok0, PEER_SC_TOKENS)], x_v.at[slot], sems.at[ring + 1]))

    def gather(slot, s, b):
        ids = idx_v[slot, s // PEER_HEADS, pl.ds((s % PEER_HEADS) * PEER_TOPK, PEER_TOPK)]
        return pltpu.make_async_copy(tab_hbm.at[ids], rows_v.at[b], sems.at[b])

    for copy in stage(0, 0):
        copy.start()
    for copy in stage(0, 0):
        copy.wait()
    for b in range(ring - 1):
        gather(0, b, b).start()

    @pl.loop(0, n_blocks)
    def _(blk):
        slot = blk % 2
        has_next = blk + 1 < n_blocks

        @pl.when(has_next)
        def _():
            for copy in stage(blk + 1, 1 - slot):
                copy.start()

        if zero_out:
            @pl.loop(0, PEER_SC_TOKENS)
            def _(t):
                @plsc.parallel_loop(0, out_v.shape[1] // SC_LANES, unroll=PEER_SC_UNROLL)
                def _(c):
                    out_v[t, pl.ds(pl.multiple_of(c * SC_LANES, SC_LANES), SC_LANES)] = (
                        jnp.zeros((SC_LANES,), jnp.float32))

        @pl.loop(0, n_steps, step=ring)
        def _(s0):
            for b in range(ring):
                s = s0 + b
                gather(slot, s, b).wait()
                ahead = s + ring - 1
                ahead_buf = (b + ring - 1) % ring

                @pl.when(ahead < n_steps)
                def _():
                    gather(slot, ahead, ahead_buf).start()

                if b >= 1:
                    @pl.when(jnp.logical_and(ahead >= n_steps, has_next))
                    def _():
                        if b == 1:
                            for copy in stage(blk + 1, 1 - slot):
                                copy.wait()
                        gather(1 - slot, ahead - n_steps, ahead_buf).start()

                compute(slot, s // PEER_HEADS, s % PEER_HEADS, b, x_v, rows_v, out_v)

        pltpu.sync_copy(out_v, out_hbm.at[pl.ds(base + blk * PEER_SC_TOKENS, PEER_SC_TOKENS)])


def _peer_dots_compute(slot, t, hd, b, h_v, rows_v, dots_v):
    lane = lax.iota(jnp.int32, SC_LANES)

    def col_step(c, accs):
        off = pl.multiple_of(c * SC_LANES, SC_LANES)
        hv = h_v[slot, t, pl.ds(off, SC_LANES)]
        return tuple(accs[r] + rows_v[b, r, pl.ds(off, SC_LANES)] * hv for r in range(PEER_TOPK))

    accs = plsc.parallel_loop(
        0, D_MODEL // SC_LANES, unroll=PEER_SC_UNROLL,
        carry=tuple(jnp.zeros((SC_LANES,), jnp.float32) for _ in range(PEER_TOPK)))(col_step)
    res = jnp.zeros((SC_LANES,), jnp.float32)
    for r in range(PEER_TOPK):
        res = jnp.where(lane == r, jnp.sum(accs[r]), res)
    dots_v[t, pl.ds(hd * PEER_TOPK, PEER_TOPK)] = res


def _peer_mix_compute(slot, t, hd, b, act_v, rows_v, out_v):
    svec = jnp.full((SC_LANES,), slot, jnp.int32)
    tvec = jnp.full((SC_LANES,), t, jnp.int32)
    weights = [plsc.load_gather(act_v, [svec, tvec, jnp.full((SC_LANES,), hd * PEER_TOPK + r, jnp.int32)])
               for r in range(PEER_TOPK)]

    @plsc.parallel_loop(0, D_MODEL // SC_LANES, unroll=PEER_SC_UNROLL)
    def _(c):
        off = pl.multiple_of(c * SC_LANES, SC_LANES)
        a = rows_v[b, 0, pl.ds(off, SC_LANES)] * weights[0]
        for r in range(1, PEER_TOPK):
            a = a + rows_v[b, r, pl.ds(off, SC_LANES)] * weights[r]
        plsc.addupdate(out_v.at[t, pl.ds(off, SC_LANES)], a)


def _peer_expert_dots(h, experts, u_tab):
    T = h.shape[0]
    assert T % (SC_CORES * SC_SUBCORES * PEER_SC_TOKENS) == 0
    assert (PEER_SC_TOKENS * PEER_HEADS) % PEER_SC_RING == 0 and PEER_TOPK == SC_LANES
    body = partial(_peer_sc_body, _peer_dots_compute, False)
    return _peer_sc_call(body, T, PEER_SLOTS, D_MODEL)(h, experts, u_tab)


def _peer_expert_mix(act, experts, v_tab):
    T = act.shape[0]
    assert T % (SC_CORES * SC_SUBCORES * PEER_SC_TOKENS) == 0
    body = partial(_peer_sc_body, _peer_mix_compute, True)
    return _peer_sc_call(body, T, D_MODEL, PEER_SLOTS)(act, experts, v_tab)


PEER_TM = 256


def _extract_topk(s, k, payload=None):
    R = s.shape[0]
    riota = lax.broadcasted_iota(jnp.int32, s.shape, 0)
    vals, rows = [], []
    for _ in range(k):
        m = jnp.max(s, axis=0, keepdims=True)
        pos = jnp.min(jnp.where(s == m, riota, R), axis=0, keepdims=True)
        hit = riota == pos
        vals.append(m)
        rows.append(pos if payload is None else jnp.max(jnp.where(hit, payload, -1), axis=0, keepdims=True))
        s = jnp.where(hit, -jnp.inf, s)
    return jnp.concatenate(vals, axis=0), jnp.concatenate(rows, axis=0)


def _peer_route_kernel(x_ref, sc_ref, sh_ref, wq_ref, keys_ref, h_ref, ex_ref, gate_ref, q_ref, ext_ref, gt_ref):
    K = PEER_TOPK
    f32 = jnp.float32
    h = x_ref[0] * (1.0 + sc_ref[0]) + sh_ref[0]
    h_ref[0] = h
    q = jnp.dot(h.astype(wq_ref.dtype), wq_ref[...], preferred_element_type=f32)
    for hd in range(PEER_HEADS):
        q_ref[hd] = q[:, hd * PEER_KEY_DIM:(hd + 1) * PEER_KEY_DIM].astype(q_ref.dtype)

    def head_body(hd, carry):
        qh = q_ref[hd]
        s1 = lax.dot_general(keys_ref[hd, 0], qh[:, :PEER_HALF], _NT, preferred_element_type=f32)
        s2 = lax.dot_general(keys_ref[hd, 1], qh[:, PEER_HALF:], _NT, preferred_element_type=f32)
        v1, i1 = _extract_topk(s1, K)
        v2, i2 = _extract_topk(s2, K)
        tm = v1.shape[1]
        cand_rows, cidx_rows = [], []
        for a in range(K):
            nb = K // (a + 1)
            cand_rows.append(v1[a:a + 1, :] + v2[:nb, :])
            cidx_rows.append(i1[a:a + 1, :] * PEER_N_KEYS + i2[:nb, :])
        n_pad = -sum(r.shape[0] for r in cand_rows) % 8
        cand_rows.append(jnp.full((n_pad, tm), -jnp.inf, f32))
        cidx_rows.append(jnp.full((n_pad, tm), -1, jnp.int32))
        cand = jnp.concatenate(cand_rows, axis=0)
        cidx = jnp.concatenate(cidx_rows, axis=0)
        top_s, experts = _extract_topk(cand, K, payload=cidx)
        e = jnp.exp(top_s - top_s[0:1, :])
        gt_ref[pl.ds(pl.multiple_of(hd * K, K), K), :] = e / jnp.sum(e, axis=0, keepdims=True)
        ext_ref[pl.ds(pl.multiple_of(hd * K, K), K), :] = experts
        return carry

    lax.fori_loop(0, PEER_HEADS, head_body, 0)
    ex_ref[0] = ext_ref[...].T
    gate_ref[0] = gt_ref[...].T


def _peer_route(x, sc, sh, w_pq, sub_keys):
    B, S, D = x.shape
    tm = PEER_TM
    assert S % tm == 0
    return pl.pallas_call(
        _peer_route_kernel,
        grid=(B, S // tm),
        in_specs=[
            pl.BlockSpec((1, tm, D), lambda b, i: (b, i, 0)),
            pl.BlockSpec((1, 1, D), lambda b, i: (b, 0, 0)),
            pl.BlockSpec((1, 1, D), lambda b, i: (b, 0, 0)),
            pl.BlockSpec((D, PEER_HEADS * PEER_KEY_DIM), lambda b, i: (0, 0)),
            pl.BlockSpec((PEER_HEADS, 2, PEER_N_KEYS, PEER_HALF), lambda b, i: (0, 0, 0, 0)),
        ],
        out_specs=[
            pl.BlockSpec((1, tm, D), lambda b, i: (b, i, 0)),
            pl.BlockSpec((1, tm, PEER_SLOTS), lambda b, i: (b, i, 0)),
            pl.BlockSpec((1, tm, PEER_SLOTS), lambda b, i: (b, i, 0)),
        ],
        out_shape=[
            jax.ShapeDtypeStruct((B, S, D), jnp.float32),
            jax.ShapeDtypeStruct((B, S, PEER_SLOTS), jnp.int32),
            jax.ShapeDtypeStruct((B, S, PEER_SLOTS), jnp.float32),
        ],
        scratch_shapes=[
            pltpu.VMEM((PEER_HEADS, tm, PEER_KEY_DIM), MM_DTYPE),
            pltpu.VMEM((PEER_SLOTS, tm), jnp.int32),
            pltpu.VMEM((PEER_SLOTS, tm), jnp.float32),
        ],
        compiler_params=pltpu.CompilerParams(dimension_semantics=("parallel", "parallel")),
        name="peer_route",
    )(x, sc[:, None, :], sh[:, None, :], w_pq.astype(MM_DTYPE), sub_keys.astype(MM_DTYPE))


def _peer_pre(h2, experts, u_tab):
    B, S, D = h2.shape
    return _peer_expert_dots(h2.reshape(B * S, D), experts.reshape(B * S, PEER_SLOTS), u_tab)


def _peer_act_kernel(pre_ref, gate_ref, o_ref):
    pre = pre_ref[...]
    o_ref[...] = 0.5 * pre * (1.0 + lax.erf(pre * (2.0 ** -0.5))) * gate_ref[...]


def _peer_act(pre, gates):
    T, W = pre.shape
    tm = min(8 * ROW_TILE, T)
    assert T % tm == 0
    spec = pl.BlockSpec((tm, W), lambda i: (i, 0))
    return pl.pallas_call(
        _peer_act_kernel,
        grid=(T // tm,),
        in_specs=[spec, spec],
        out_specs=spec,
        out_shape=jax.ShapeDtypeStruct((T, W), jnp.float32),
        name="peer_act",
    )(pre, gates.reshape(T, W))


def _peer_out(act, experts, v_tab):
    B, S, _ = experts.shape
    return _peer_expert_mix(act, experts.reshape(B * S, PEER_SLOTS), v_tab).reshape(B, S, D_MODEL)


def kernel(x, c, w_ada, b_ada, w_in, rw_mu, rw_w0, rw_w2, rw_a0, rw_a2, rw_g2, rw_k_k, rw_k_a, rw_r_k, rw_gn_g, rw_gn_b, dsa_kv_g, idx_k_g, idx_k_b, rel_bias, w_br_a, w_br_b, w_out, ln1_g, ln1_b, peer_wq, peer_keys, peer_u, peer_v, ln2_g, ln2_b):
    assert w_in.shape[0] == DEPTH == 1
    l = 0
    mod = jax.nn.silu(c) @ w_ada[l] + b_ada[l]

    w_rw, w_q, w_kv, w_qi, w_ki, w_wi, w_ga, w_gb = jnp.split(w_in[l], _split_points(IN_SIZES), axis=-1)
    small_pad = jnp.zeros((D_MODEL, SMALL_COLS - SMALL_WI - IDX_HEADS), w_in.dtype)
    w_small = jnp.concatenate([w_qi, w_kv, w_ki, w_wi, small_pad], axis=-1)
    w_gates = jnp.concatenate([w_ga, w_gb], axis=-1)

    B, S, _ = x.shape
    seg_len = S // SEQ_SEGMENTS
    dsa_topk = min(TOPK_MAX, S // 4)
    bias_tiles = _dsa_bias_tiles(rel_bias, DSA_TQ)

    def project_and_dsa(window, sc1, sh1, seg, keys_so_far):
        z_rw = _mod_matmul(x, window, sc1, sh1, w_rw, jnp.float32, tn=896)
        z_q = _mod_matmul(x, window, sc1, sh1, w_q, MM_DTYPE, tn=1024)
        z_small = _mod_matmul(x, window, sc1, sh1, w_small, jnp.float32, tn=512)
        z_g = _mod_matmul(x, window, sc1, sh1, w_gates, jnp.float32, tn=1024)
        z_wi_t = jnp.swapaxes(z_small[..., SMALL_WI:SMALL_WI + IDX_HEADS], 1, 2)
        qi, ckv, kidx = _dsa_prep(z_small, dsa_kv_g[l], idx_k_g[l], idx_k_b[l])
        if keys_so_far is not None:
            ckv = jnp.concatenate([keys_so_far[0], ckv], axis=1)
            kidx = jnp.concatenate([keys_so_far[1], kidx], axis=1)
        y_b = _dsa_attention(z_q, qi, z_wi_t, ckv, kidx, bias_tiles, dsa_topk,
                             seg * (seg_len // DSA_TQ), MM_DTYPE)
        return (y_b, z_rw, z_g), (ckv, kidx)

    def rwkv_and_merge(window, gt1, carry, y_b, z_rw, z_g):
        y_a, carry = _rwkv7_time_mix(z_rw, carry, rw_mu[l], rw_w0[l], rw_w2[l], rw_a0[l], rw_a2[l], rw_g2[l],
                                     rw_k_k[l], rw_k_a[l], rw_r_k[l], rw_gn_g[l], rw_gn_b[l], MM_DTYPE)
        return _merge_ln(x, window, gt1, y_a, y_b, z_g, w_br_a[l], w_br_b[l], w_out[l], ln1_g[l], ln1_b[l]), carry

    def finish(st, act):
        y2 = _peer_out(act, st["experts"], peer_v[l])
        return _residual_ln(st["x1"], st["gt2"], y2, ln2_g[l], ln2_b[l])

    assert B % BATCH_GROUPS == 0 and S % (SEQ_SEGMENTS * DSA_TQ) == 0
    gsz = B // BATCH_GROUPS
    outs = []
    prev = None
    for g in range(BATCH_GROUPS):
        sh1, sc1, gt1, sh2, sc2, gt2 = jnp.split(mod[g * gsz:(g + 1) * gsz], 6, axis=-1)
        rw_carry = _rwkv7_init_carry(gsz)
        keys_so_far = None
        for seg in range(SEQ_SEGMENTS):
            window = (g * gsz, gsz, seg * seg_len, seg_len)
            sc1_g = sc1
            if prev is not None:
                sc1_g, prev["experts"] = lax.optimization_barrier((sc1_g, prev["experts"]))
            proj, keys_so_far = project_and_dsa(window, sc1_g, sh1, seg, keys_so_far)
            if prev is not None:
                proj, prev["pre"], outs = lax.optimization_barrier((proj, prev["pre"], outs))
                outs.append(finish(prev, _peer_act(prev["pre"], prev["gates"])))
            x1, rw_carry = rwkv_and_merge(window, gt1, rw_carry, *proj)
            h2, experts, gates = _peer_route(x1, sc2, sh2, peer_wq[l], peer_keys[l])
            prev = dict(x1=x1, gt2=gt2, experts=experts, gates=gates, pre=_peer_pre(h2, experts, peer_u[l]))
    prev["pre"], outs = lax.optimization_barrier((prev["pre"], outs))
    outs.append(finish(prev, _peer_act(prev["pre"], prev["gates"])))
    rows = [jnp.concatenate(outs[g * SEQ_SEGMENTS:(g + 1) * SEQ_SEGMENTS], axis=1) for g in range(BATCH_GROUPS)]
    return jnp.concatenate(rows, axis=0)
```

```python
import math
from functools import partial

import jax
import jax.numpy as jnp
import numpy as np
from jax import lax
from jax.experimental import pallas as pl
from jax.experimental.pallas import tpu as pltpu
from jax.experimental.pallas import tpu_sc as plsc

D_MODEL = 1024
RW_HEADS = 8
RW_HEAD_DIM = 64
RW_DIM = 512
RW_DECAY_LORA = 64
RW_A_LORA = 64
RW_GATE_LORA = 128
RW_COLS = 3 * RW_DIM + RW_DECAY_LORA + RW_A_LORA + RW_GATE_LORA
RW_GN_EPS = 64e-5
DSA_HEADS = 8
DSA_LATENT = 128
DSA_Q_DIM = DSA_HEADS * DSA_LATENT
IDX_HEADS = 4
IDX_DIM = 64
TOPK_MAX = 256
REL_BUCKETS = 32
REL_MAX_DIST = 128
IN_SIZES = (RW_COLS, DSA_Q_DIM, DSA_LATENT, IDX_HEADS * IDX_DIM, IDX_DIM, IDX_HEADS, D_MODEL, D_MODEL)
IN_COLS = sum(IN_SIZES)
PEER_HEADS = 8
PEER_N_KEYS = 128
PEER_KEY_DIM = 128
PEER_HALF = 64
PEER_TOPK = 16
LN_EPS = 1e-5
DEPTH = 1
DEEPNORM_ALPHA = (2.0 * DEPTH) ** 0.25

LANES = 128
SMALL_QI = 0
SMALL_KV = SMALL_QI + IDX_HEADS * IDX_DIM
SMALL_KI = SMALL_KV + DSA_LATENT
SMALL_WI = SMALL_KI + IDX_DIM
SMALL_COLS = 4 * LANES
MM_DTYPE = jnp.bfloat16
BATCH_GROUPS = 8
SEQ_SEGMENTS = 2


def _split_points(sizes):
    return np.cumsum(sizes)[:-1].tolist()


def _proj_all_kernel(x_ref, sc_ref, sh_ref, wrw_ref, wq_ref, wsm_ref, wg_ref, zrw_ref, zq_ref, zsm_ref, zg_ref):
    h = (x_ref[0] * (1.0 + sc_ref[0]) + sh_ref[0]).astype(wrw_ref.dtype)
    for w_ref, z_ref in ((wrw_ref, zrw_ref), (wq_ref, zq_ref), (wsm_ref, zsm_ref), (wg_ref, zg_ref)):
        z_ref[0] = jnp.dot(h, w_ref[...], preferred_element_type=jnp.float32).astype(z_ref.dtype)


def _proj_all(x, window, sc, sh, weights, out_dtypes, tm=256):
    b0, B, t0, S = window
    D = x.shape[2]
    assert S % tm == 0 and t0 % tm == 0
    weights = [w.astype(MM_DTYPE) for w in weights]
    vec = pl.BlockSpec((1, 1, D), lambda b, i: (b, 0, 0))
    out_bytes = sum(tm * w.shape[1] * jnp.dtype(dt).itemsize for w, dt in zip(weights, out_dtypes))
    vmem_estimate = 2 * (sum(w.size for w in weights) * 2 + tm * D * 4 + out_bytes) + 4 * tm * D * 4
    return pl.pallas_call(
        _proj_all_kernel,
        grid=(B, S // tm),
        in_specs=[pl.BlockSpec((1, tm, D), lambda b, i: (b + b0, i + t0 // tm, 0)), vec, vec]
        + [pl.BlockSpec(w.shape, lambda b, i: (0, 0)) for w in weights],
        out_specs=[pl.BlockSpec((1, tm, w.shape[1]), lambda b, i: (b, i, 0)) for w in weights],
        out_shape=[jax.ShapeDtypeStruct((B, S, w.shape[1]), dt) for w, dt in zip(weights, out_dtypes)],
        compiler_params=pltpu.CompilerParams(dimension_semantics=("parallel", "parallel"),
                                             vmem_limit_bytes=vmem_estimate),
        name="proj_all",
    )(x, sc[:, None, :], sh[:, None, :], *weights)


ROW_TILE = 256


def _layer_norm(x, g, b):
    mu = jnp.mean(x, -1, keepdims=True)
    var = jnp.mean(jnp.square(x - mu), -1, keepdims=True)
    return (x - mu) * lax.rsqrt(var + LN_EPS) * g + b


def _merge_ln_kernel(x_ref, gt_ref, ya_ref, yb_ref, zg_ref, wa_ref, wb_ref, wo_ref, g_ref, b_ref, o_ref):
    f32 = jnp.float32
    D = x_ref.shape[2]
    ya = jnp.dot(ya_ref[0], wa_ref[...], preferred_element_type=f32)
    yb = jnp.dot(yb_ref[0], wb_ref[...], preferred_element_type=f32)
    zg = zg_ref[0]
    merged = jax.nn.sigmoid(zg[:, :D]) * ya + jax.nn.sigmoid(zg[:, D:]) * yb
    mix = jnp.dot(merged.astype(wo_ref.dtype), wo_ref[...], preferred_element_type=f32)
    o_ref[0] = _layer_norm(DEEPNORM_ALPHA * x_ref[0] + gt_ref[0] * mix, g_ref[...], b_ref[...])


def _merge_ln(x, window, gt, y_a, y_b, z_g, w_a, w_b, w_o, ln_g, ln_b):
    b0, B, t0, S = window
    D = x.shape[2]
    tm = ROW_TILE
    assert S % tm == 0 and t0 % tm == 0
    tok = lambda width: pl.BlockSpec((1, tm, width), lambda b, i: (b, i, 0))
    x_spec = pl.BlockSpec((1, tm, D), lambda b, i: (b + b0, i + t0 // tm, 0))
    full = lambda a: pl.BlockSpec(a.shape, lambda b, i: (0,) * a.ndim)
    w_a, w_b, w_o = (w.astype(MM_DTYPE) for w in (w_a, w_b, w_o))
    ln_g, ln_b = ln_g[None], ln_b[None]
    return pl.pallas_call(
        _merge_ln_kernel,
        grid=(B, S // tm),
        in_specs=[x_spec, pl.BlockSpec((1, 1, D), lambda b, i: (b, 0, 0)), tok(y_a.shape[2]), tok(y_b.shape[2]),
                  tok(z_g.shape[2]), full(w_a), full(w_b), full(w_o), full(ln_g), full(ln_b)],
        out_specs=tok(D),
        out_shape=jax.ShapeDtypeStruct((B, S, D), jnp.float32),
        compiler_params=pltpu.CompilerParams(dimension_semantics=("parallel", "parallel")),
        name="merge_ln",
    )(x, gt[:, None, :], y_a, y_b, z_g, w_a, w_b, w_o, ln_g, ln_b)


def _residual_ln_kernel(x_ref, gt_ref, y_ref, g_ref, b_ref, o_ref):
    o_ref[0] = _layer_norm(DEEPNORM_ALPHA * x_ref[0] + gt_ref[0] * y_ref[0], g_ref[...], b_ref[...])


def _residual_ln(x, gt, y, ln_g, ln_b):
    B, S, D = x.shape
    tm = ROW_TILE
    assert S % tm == 0
    tok = pl.BlockSpec((1, tm, D), lambda b, i: (b, i, 0))
    vec = pl.BlockSpec((1, D), lambda b, i: (0, 0))
    return pl.pallas_call(
        _residual_ln_kernel,
        grid=(B, S // tm),
        in_specs=[tok, pl.BlockSpec((1, 1, D), lambda b, i: (b, 0, 0)), tok, vec, vec],
        out_specs=tok,
        out_shape=jax.ShapeDtypeStruct((B, S, D), jnp.float32),
        compiler_params=pltpu.CompilerParams(dimension_semantics=("parallel", "parallel")),
        name="residual_ln",
    )(x, gt[:, None, :], y, ln_g[None], ln_b[None])


def _t5_bucket(n):
    n = jnp.maximum(n, 0)
    max_exact = REL_BUCKETS // 2
    nf = jnp.maximum(n, 1).astype(jnp.float32)
    large = max_exact + (jnp.log(nf / max_exact) / math.log(REL_MAX_DIST / max_exact)
                         * (REL_BUCKETS - max_exact)).astype(jnp.int32)
    large = jnp.minimum(large, REL_BUCKETS - 1)
    return jnp.where(n < max_exact, n, large)


RW_CHUNK = 64
RW_INV_BLOCK = 16
_NN = (((1,), (0,)), ((), ()))
_NT = (((1,), (1,)), ((), ()))
_BNN = (((2,), (1,)), ((0,), (0,)))
_BNT = (((2,), (2,)), ((0,), (0,)))


def _dot_f32(a, b, dims=_NN):
    return lax.dot_general(a, b, dims, precision=lax.Precision.HIGHEST,
                           preferred_element_type=jnp.float32)


def _dot_bf16x3(a, b, dims=_NN):
    f32, bf = jnp.float32, jnp.bfloat16
    a_hi, b_hi = a.astype(bf), b.astype(bf)
    a_lo = (a - a_hi.astype(f32)).astype(bf)
    b_lo = (b - b_hi.astype(f32)).astype(bf)
    out = lax.dot_general(a_hi, b_hi, dims, preferred_element_type=f32)
    out = out + lax.dot_general(a_hi, b_lo, dims, preferred_element_type=f32)
    return out + lax.dot_general(a_lo, b_hi, dims, preferred_element_type=f32)


def _bf16_terms(a):
    f32, bf = jnp.float32, jnp.bfloat16
    hi = a.astype(bf)
    r1 = a - hi.astype(f32)
    mid = r1.astype(bf)
    lo = (r1 - mid.astype(f32)).astype(bf)
    return hi, mid, lo


def _dot_lhs_split(a, b01):
    b = b01.astype(jnp.bfloat16)
    return sum(jnp.dot(t, b, preferred_element_type=jnp.float32) for t in _bf16_terms(a))


def _dot_rhs_split(a01, b):
    a = a01.astype(jnp.bfloat16)
    return sum(jnp.dot(a, t, preferred_element_type=jnp.float32) for t in _bf16_terms(b))


def _rwkv_kernel(z_ref, m0_ref, p0_ref, mu_ref, w0_ref, w2_ref, a0_ref, a2_ref, g2_ref, kk_ref, ka_ref, rk_ref,
                 gng_ref, gnb_ref, bd_ref, o_ref, mo_ref, po_ref, m_ref, prev_ref, y_ref):
    C = z_ref.shape[1]
    N = RW_HEAD_DIM
    f32 = jnp.float32
    dot3 = _dot_bf16x3

    @pl.when(pl.program_id(1) == 0)
    def _():
        m_ref[...] = m0_ref[0]
        prev_ref[...] = p0_ref[0]

    z = z_ref[0]
    row = lax.broadcasted_iota(jnp.int32, z.shape, 0)
    shifted = jnp.where(row == 0, prev_ref[...], pltpu.roll(z, 1, axis=0))
    prev_ref[...] = z[C - 1:C, :]
    zs = z + (shifted - z) * mu_ref[...]
    r = zs[:, 0:RW_DIM]
    k = zs[:, RW_DIM:2 * RW_DIM]
    v = zs[:, 2 * RW_DIM:3 * RW_DIM]
    o1 = 3 * RW_DIM
    wl = zs[:, o1:o1 + RW_DECAY_LORA]
    al = zs[:, o1 + RW_DECAY_LORA:o1 + RW_DECAY_LORA + RW_A_LORA]
    gl = zs[:, o1 + RW_DECAY_LORA + RW_A_LORA:]

    bd = bd_ref[...]
    log_w = -jax.nn.softplus(-(w0_ref[...] + dot3(jnp.tanh(wl), w2_ref[...]))) - 0.5
    ldec = -jnp.exp(log_w)
    a_lr = jax.nn.sigmoid(a0_ref[...] + dot3(al, a2_ref[...]))
    g = dot3(jax.nn.sigmoid(gl), g2_ref[...])
    kk = k * kk_ref[...]
    kk = kk * lax.rsqrt(jnp.maximum(_dot_lhs_split(kk * kk, bd), 1e-24))
    k2 = k * (1.0 + (a_lr - 1.0) * ka_ref[...])
    a_vec = -kk
    b_vec = kk * a_lr

    ti = lax.broadcasted_iota(jnp.int32, (C, C), 0)
    tj = lax.broadcasted_iota(jnp.int32, (C, C), 1)
    cum = _dot_rhs_split((ti >= tj).astype(f32), ldec)
    cum_last = cum[C - 1:C, :]
    w_incl = jnp.exp(cum)
    w_excl = jnp.exp(cum - ldec)
    w_inv = jnp.exp(-cum)
    w_end = jnp.exp(cum_last - cum)
    w_all = jnp.exp(cum_last)
    a_t = a_vec * w_excl
    r_t = r * w_incl
    b_t = b_vec * w_inv
    k_t = k2 * w_inv
    b_e = b_vec * w_end
    k_e = k2 * w_end

    strict = ti > tj
    incl = ti >= tj
    bi, bj = ti // RW_INV_BLOCK, tj // RW_INV_BLOCK
    same_blk = bi == bj
    pair_blk = jnp.logical_and((bi // 2) == (bj // 2), jnp.logical_not(same_blk))
    half_blk = (bi // 2) != (bj // 2)
    eye = (ti == tj).astype(f32)

    H = RW_HEADS
    heads = lambda x: jnp.stack([x[:, h * N:(h + 1) * N] for h in range(H)], axis=0)
    bmm = lambda a, b: dot3(a, b, _BNN)
    A, Rt, Bt, Kt, Be, Ke, V = (heads(t) for t in (a_t, r_t, b_t, k_t, b_e, k_e, v))
    gm = dot3(jnp.concatenate([A, Rt], axis=1), jnp.concatenate([Bt, Kt], axis=1), _BNT)
    a_ab = jnp.where(strict, gm[:, :C, :C], 0.0)
    a_ak = jnp.where(strict, gm[:, :C, C:], 0.0)
    a_rb = jnp.where(incl, gm[:, C:, :C], 0.0)
    a_rk = jnp.where(incl, gm[:, C:, C:], 0.0)
    d1 = jnp.where(same_blk, a_ab, 0.0)
    xinv = eye + d1
    d2 = bmm(d1, d1)
    xinv = xinv + bmm(xinv, d2)
    d4 = bmm(d2, d2)
    xinv = xinv + bmm(xinv, d4)
    d8 = bmm(d4, d4)
    xinv = xinv + bmm(xinv, d8)
    xinv = xinv + bmm(bmm(xinv, jnp.where(pair_blk, a_ab, 0.0)), xinv)
    xinv = xinv + bmm(bmm(xinv, jnp.where(half_blk, a_ab, 0.0)), xinv)

    av = bmm(jnp.concatenate([a_ak, a_rk], axis=1), V)
    p = bmm(xinv, jnp.concatenate([A, av[:, :C]], axis=2))
    qm = bmm(a_rb, p)
    q1 = Rt + qm[:, :, :N]
    q2 = qm[:, :, N:] + av[:, C:]
    gmat = bmm(jnp.swapaxes(Be, 1, 2), p)
    g1 = eye * heads(w_all) + gmat[:, :, :N]
    g2 = gmat[:, :, N:] + bmm(jnp.swapaxes(Ke, 1, 2), V)
    m = m_ref[...]
    yh = _dot_f32(q1, m, _BNN) + q2
    m_ref[...] = _dot_f32(g1, m, _BNN) + g2
    for h in range(H):
        y_ref[:, h * N:(h + 1) * N] = yh[h]

    y = y_ref[...]
    mean = _dot_lhs_split(y, bd) * (1.0 / N)
    yc = y - mean
    var = _dot_lhs_split(yc * yc, bd) * (1.0 / N)
    yn = yc * lax.rsqrt(var + RW_GN_EPS) * gng_ref[...] + gnb_ref[...]
    bonus = _dot_lhs_split(r * k2 * rk_ref[...], bd) * v
    o_ref[0] = ((yn + bonus) * g).astype(o_ref.dtype)

    @pl.when(pl.program_id(1) == pl.num_programs(1) - 1)
    def _():
        mo_ref[0] = m_ref[...]
        po_ref[0] = prev_ref[...]


def _rwkv7_init_carry(batch):
    return (jnp.zeros((batch, RW_HEADS, RW_HEAD_DIM, RW_HEAD_DIM), jnp.float32),
            jnp.zeros((batch, 1, RW_COLS), jnp.float32))


def _rwkv7_time_mix(z_rw, carry, mu, w0, w2, a0, a2, g2, k_k, k_a, r_k, gn_g, gn_b, out_dtype):
    B, S, _ = z_rw.shape
    C = RW_CHUNK
    assert S % C == 0 and C == RW_HEAD_DIM and C % (4 * RW_INV_BLOCK) == 0
    hid = jnp.arange(RW_DIM) // RW_HEAD_DIM
    bd = (hid[:, None] == hid[None, :]).astype(jnp.float32)
    row = lambda a: a.reshape(1, -1)
    full = lambda shape: pl.BlockSpec(shape, lambda b, c: (0,) * len(shape))
    state_spec = pl.BlockSpec((1, RW_HEADS, RW_HEAD_DIM, RW_HEAD_DIM), lambda b, c: (b, 0, 0, 0))
    prev_spec = pl.BlockSpec((1, 1, RW_COLS), lambda b, c: (b, 0, 0))
    y, m_out, p_out = pl.pallas_call(
        _rwkv_kernel,
        grid=(B, S // C),
        in_specs=[
            pl.BlockSpec((1, C, RW_COLS), lambda b, c: (b, c, 0)), state_spec, prev_spec,
            full((1, RW_COLS)), full((1, RW_DIM)), full((RW_DECAY_LORA, RW_DIM)), full((1, RW_DIM)),
            full((RW_A_LORA, RW_DIM)), full((RW_GATE_LORA, RW_DIM)), full((1, RW_DIM)), full((1, RW_DIM)),
            full((1, RW_DIM)), full((1, RW_DIM)), full((1, RW_DIM)), full((RW_DIM, RW_DIM)),
        ],
        out_specs=[pl.BlockSpec((1, C, RW_DIM), lambda b, c: (b, c, 0)), state_spec, prev_spec],
        out_shape=[jax.ShapeDtypeStruct((B, S, RW_DIM), out_dtype),
                   jax.ShapeDtypeStruct(carry[0].shape, jnp.float32),
                   jax.ShapeDtypeStruct(carry[1].shape, jnp.float32)],
        scratch_shapes=[
            pltpu.VMEM((RW_HEADS, RW_HEAD_DIM, RW_HEAD_DIM), jnp.float32),
            pltpu.VMEM((1, RW_COLS), jnp.float32),
            pltpu.VMEM((C, RW_DIM), jnp.float32),
        ],
        compiler_params=pltpu.CompilerParams(dimension_semantics=("parallel", "arbitrary")),
        name="rwkv7_time_mix",
    )(z_rw, carry[0], carry[1], row(mu), row(w0), w2, row(a0), a2, g2, row(k_k), row(k_a), row(r_k),
      row(gn_g), row(gn_b), bd)
    return y, (m_out, p_out)


DSA_TQ = 256
MASK_NEG = -1e30
INT_MIN = -2 ** 31
KEY_NEG_INF = -2139095041
THRESH_BITS = 32


def _dsa_prep_kernel(z_ref, kvg_ref, kig_ref, kib_ref, qi_ref, kv_ref, ki_ref):
    z = z_ref[0]
    qi_ref[0] = z[:, SMALL_QI:SMALL_KV].astype(qi_ref.dtype)
    kv = z[:, SMALL_KV:SMALL_KI]
    ms = jnp.mean(jnp.square(kv), -1, keepdims=True)
    kv_ref[0] = (kv * lax.rsqrt(ms + LN_EPS) * kvg_ref[...]).astype(kv_ref.dtype)
    ki = z[:, SMALL_KI:SMALL_WI]
    mu = jnp.mean(ki, -1, keepdims=True)
    var = jnp.mean(jnp.square(ki - mu), -1, keepdims=True)
    ki_ref[0] = ((ki - mu) * lax.rsqrt(var + LN_EPS) * kig_ref[...] + kib_ref[...]).astype(ki_ref.dtype)


def _dsa_prep(z_small, kv_g, ki_g, ki_b, tm=512):
    B, S, W = z_small.shape
    assert S % tm == 0
    return pl.pallas_call(
        _dsa_prep_kernel,
        grid=(B, S // tm),
        in_specs=[
            pl.BlockSpec((1, tm, W), lambda b, i: (b, i, 0)),
            pl.BlockSpec((1, DSA_LATENT), lambda b, i: (0, 0)),
            pl.BlockSpec((1, IDX_DIM), lambda b, i: (0, 0)),
            pl.BlockSpec((1, IDX_DIM), lambda b, i: (0, 0)),
        ],
        out_specs=[
            pl.BlockSpec((1, tm, IDX_HEADS * IDX_DIM), lambda b, i: (b, i, 0)),
            pl.BlockSpec((1, tm, DSA_LATENT), lambda b, i: (b, i, 0)),
            pl.BlockSpec((1, tm, IDX_DIM), lambda b, i: (b, i, 0)),
        ],
        out_shape=[
            jax.ShapeDtypeStruct((B, S, IDX_HEADS * IDX_DIM), MM_DTYPE),
            jax.ShapeDtypeStruct((B, S, DSA_LATENT), MM_DTYPE),
            jax.ShapeDtypeStruct((B, S, IDX_DIM), MM_DTYPE),
        ],
        name="dsa_prep",
    )(z_small, kv_g[None], ki_g[None], ki_b[None])


def _sortable_key(s):
    s = jnp.where(s == 0.0, 0.0, s)
    bits = pltpu.bitcast(s, jnp.int32)
    return bits ^ ((bits >> 31) & 0x7FFFFFFF)


def _col_count(mask_i32):
    tk, tq = mask_i32.shape
    return jnp.sum(mask_i32.reshape(tk // 8, 8, tq), axis=0)


def _dsa_kernel(bfar_ref, q_ref, qi_ref, wit_ref, kv_ref, kvt_ref, ki_ref, b0_ref, b1_ref, o_ref,
                key_ref, madd_ref, m_ref, l_ref, acc_ref, *, topk, q_off):
    tq = q_ref.shape[1]
    tk = tq
    i = pl.program_id(1) + q_off
    nj = i + 1
    f32 = jnp.float32
    krow = lax.broadcasted_iota(jnp.int32, (tk, tq), 0)
    qcol = lax.broadcasted_iota(jnp.int32, (tk, tq), 1)

    qi = qi_ref[0]
    wit = wit_ref[0] * (IDX_HEADS ** -0.5)

    def score_chunk(j, carry):
        off = pl.multiple_of(j * tk, tk)
        kc = ki_ref[0, pl.ds(off, tk), :]
        s = jnp.zeros((tk, tq), f32)
        for h in range(IDX_HEADS):
            d = lax.dot_general(kc, qi[:, h * IDX_DIM:(h + 1) * IDX_DIM],
                                (((1,), (1,)), ((), ())), preferred_element_type=f32)
            s = s + wit[h:h + 1, :] * jnp.maximum(d * (IDX_DIM ** -0.5), 0.0)
        causal = (krow + j * tk) <= (qcol + i * tq)
        s = jnp.where(causal, s, -jnp.inf)
        key_ref[j] = _sortable_key(s)
        return carry

    lax.fori_loop(0, nj, score_chunk, 0)

    def count_where(pred_fn):
        def body(j, acc):
            return acc + _col_count(pred_fn(key_ref[j], j).astype(jnp.int32))
        acc = lax.fori_loop(0, nj, body, jnp.zeros((8, tq), jnp.int32))
        return jnp.sum(acc, axis=0, keepdims=True)

    def bit_step(it, t_u):
        bit = THRESH_BITS - 1 - it
        cand_u = t_u | jnp.left_shift(jnp.int32(1), bit)
        cand = cand_u ^ INT_MIN
        cnt = count_where(lambda k, j: k >= cand)
        return jnp.where(cnt >= topk, cand_u, t_u)

    t_u = lax.fori_loop(0, THRESH_BITS, bit_step, jnp.zeros((1, tq), jnp.int32))
    thr = t_u ^ INT_MIN
    cnt_gt = count_where(lambda k, j: k > thr)
    cnt_ge = count_where(lambda k, j: k >= thr)
    is_neg = thr == KEY_NEG_INF
    need = jnp.logical_and(cnt_ge > topk, jnp.logical_not(is_neg))
    n_tie_take = topk - cnt_gt

    thr_open = jnp.where(is_neg, thr, thr - 1)
    any_need = jnp.max(need.astype(jnp.int32)) > 0

    @pl.when(jnp.logical_not(any_need))
    def _():
        def body(j, carry):
            madd_ref[j] = jnp.where(key_ref[j] > thr_open, 0.0, MASK_NEG)
            return carry
        lax.fori_loop(0, nj, body, 0)

    @pl.when(any_need)
    def _():
        s_len = tk * key_ref.shape[0]
        n_bits = max(1, int(math.ceil(math.log2(s_len))))

        def idx_step(it, p):
            bit = n_bits - 1 - it
            cand = p | jnp.left_shift(jnp.int32(1), bit)
            cnt = count_where(
                lambda k, j: jnp.where(k == thr, jnp.where((krow + j * tk) < cand, 1, 0), 0))
            return jnp.where(cnt < n_tie_take, cand, p)

        p_idx = lax.fori_loop(0, n_bits, idx_step, jnp.zeros((1, tq), jnp.int32))
        p_idx = jnp.where(need, p_idx, jnp.where(is_neg, -1, s_len))

        def body(j, carry):
            k = key_ref[j]
            tie_ok = jnp.where((krow + j * tk) <= p_idx, 0.0, MASK_NEG)
            madd_ref[j] = jnp.where(k > thr, 0.0, jnp.where(k == thr, tie_ok, MASK_NEG))
            return carry
        lax.fori_loop(0, nj, body, 0)

    H, d = DSA_HEADS, DSA_LATENT
    m_ref[...] = jnp.full(m_ref.shape, MASK_NEG, f32)
    l_ref[...] = jnp.zeros(l_ref.shape, f32)
    acc_ref[...] = jnp.zeros(acc_ref.shape, f32)
    scale = DSA_LATENT ** -0.5
    q_all = jnp.concatenate([q_ref[0, :, h * d:(h + 1) * d] for h in range(H)], axis=0)

    def attend(j, bias):
        off = pl.multiple_of(j * tk, tk)
        kc = kv_ref[0, pl.ds(off, tk), :]
        kct = kvt_ref[0, :, pl.ds(off, tk)]
        lg = lax.dot_general(kc, q_all, (((1,), (1,)), ((), ())), preferred_element_type=f32)
        lg = lg * scale + bias + jnp.tile(madd_ref[j], (1, H))
        m_old = m_ref[...]
        m_new = jnp.maximum(m_old, jnp.max(lg, axis=0, keepdims=True))
        alpha = jnp.exp(m_old - m_new)
        p = jnp.exp(lg - m_new)
        l_ref[...] = alpha * l_ref[...] + jnp.sum(p, axis=0, keepdims=True)
        acc_ref[...] = alpha * acc_ref[...] + jnp.dot(kct, p.astype(kct.dtype), preferred_element_type=f32)
        m_ref[...] = m_new

    def far_body(j, carry):
        attend(j, bfar_ref[...])
        return carry

    lax.fori_loop(0, jnp.maximum(i - 1, 0), far_body, 0)

    @pl.when(i >= 1)
    def _():
        attend(i - 1, b1_ref[...])

    attend(i, b0_ref[...])

    out_t = acc_ref[...] / l_ref[...]
    for h in range(H):
        o_ref[0, :, h * d:(h + 1) * d] = out_t[:, h * tq:(h + 1) * tq].T.astype(o_ref.dtype)


def _dsa_bias_tiles(rel_bias, tq):
    dist = jnp.arange(2 * tq, dtype=jnp.int32)
    by_dist = rel_bias[_t5_bucket(dist)].T

    def toeplitz(v):
        H, P = v.shape
        skew = jnp.tile(v, (1, tq + 1))[:, :tq * (P + 1)].reshape(H, tq, P + 1)
        return skew[:, ::-1, :tq]

    b0 = toeplitz(jnp.concatenate([jnp.repeat(by_dist[:, :1], tq - 1, axis=1), by_dist[:, :tq + 1]], axis=1))
    b1 = toeplitz(jnp.concatenate([by_dist[:, 1:], by_dist[:, -1:]], axis=1))
    H = by_dist.shape[0]
    b0 = jnp.transpose(b0, (1, 0, 2)).reshape(tq, H * tq)
    b1 = jnp.transpose(b1, (1, 0, 2)).reshape(tq, H * tq)
    return b0, b1, jnp.repeat(by_dist[:, -1], tq)[None, :]


def _dsa_attention(q, qi, z_wi_t, ckv, kidx, bias_tiles, topk, q_off, out_dtype):
    B, Sq, _ = q.shape
    tq = DSA_TQ
    nq = Sq // tq
    S = ckv.shape[1]
    assert Sq % tq == 0 and S == (q_off + nq) * tq and tq >= REL_MAX_DIST and topk <= tq
    ckv_t = jnp.swapaxes(ckv, 1, 2)
    b0, b1, bfar = bias_tiles
    nk = S // tq
    H, d = DSA_HEADS, DSA_LATENT
    f32_bytes = 4
    vmem_estimate = (
        2 * nk * tq * tq * f32_bytes
        + 2 * 2 * tq * H * tq * f32_bytes
        + 2 * (2 * S * d + S * LANES) * 2
        + 6 * tq * H * tq * f32_bytes
        + 4 * tq * H * d * f32_bytes
    )
    grid_spec = pltpu.PrefetchScalarGridSpec(
        num_scalar_prefetch=0,
        grid=(B, nq),
        in_specs=[
            pl.BlockSpec((1, H * tq), lambda b, i: (0, 0)),
            pl.BlockSpec((1, tq, H * d), lambda b, i: (b, i, 0)),
            pl.BlockSpec((1, tq, IDX_HEADS * IDX_DIM), lambda b, i: (b, i, 0)),
            pl.BlockSpec((1, IDX_HEADS, tq), lambda b, i: (b, 0, i)),
            pl.BlockSpec((1, S, d), lambda b, i: (b, 0, 0)),
            pl.BlockSpec((1, d, S), lambda b, i: (b, 0, 0)),
            pl.BlockSpec((1, S, IDX_DIM), lambda b, i: (b, 0, 0)),
            pl.BlockSpec((tq, H * tq), lambda b, i: (0, 0)),
            pl.BlockSpec((tq, H * tq), lambda b, i: (0, 0)),
        ],
        out_specs=pl.BlockSpec((1, tq, H * d), lambda b, i: (b, i, 0)),
        scratch_shapes=[
            pltpu.VMEM((nk, tq, tq), jnp.int32),
            pltpu.VMEM((nk, tq, tq), jnp.float32),
            pltpu.VMEM((1, H * tq), jnp.float32),
            pltpu.VMEM((1, H * tq), jnp.float32),
            pltpu.VMEM((d, H * tq), jnp.float32),
        ],
    )
    return pl.pallas_call(
        partial(_dsa_kernel, topk=topk, q_off=q_off),
        grid_spec=grid_spec,
        out_shape=jax.ShapeDtypeStruct((B, Sq, H * d), out_dtype),
        compiler_params=pltpu.CompilerParams(vmem_limit_bytes=vmem_estimate),
        name="dsa_attention",
    )(bfar, q, qi, z_wi_t, ckv, ckv_t, kidx, b0, b1)


SC_CORES = 2
SC_SUBCORES = 16
SC_LANES = 16
PEER_SC_TOKENS = 8
PEER_SC_RING = 4
PEER_SC_UNROLL = 4
PEER_SLOTS = PEER_HEADS * PEER_TOPK


def _peer_sc_call(body, T, out_width, stage_width):
    mesh = plsc.VectorSubcoreMesh(core_axis_name="c", subcore_axis_name="s")
    return pl.kernel(
        body, mesh=mesh,
        out_type=jax.ShapeDtypeStruct((T, out_width), jnp.float32),
        scratch_types=[
            pltpu.VMEM((2, PEER_SC_TOKENS, PEER_SLOTS), jnp.int32),
            pltpu.VMEM((2, PEER_SC_TOKENS, stage_width), jnp.float32),
            pltpu.VMEM((PEER_SC_RING, PEER_TOPK, D_MODEL), jnp.float32),
            pltpu.VMEM((PEER_SC_TOKENS, out_width), jnp.float32),
            pltpu.SemaphoreType.DMA((PEER_SC_RING + 2,)),
        ],
        compiler_params=pltpu.CompilerParams(needs_layout_passes=False),
    )


def _peer_sc_body(compute, zero_out, x_hbm, idx_hbm, tab_hbm, out_hbm, idx_v, x_v, rows_v, out_v, sems):
    T = idx_hbm.shape[0]
    tokens_per_worker = T // (SC_CORES * SC_SUBCORES)
    n_blocks = tokens_per_worker // PEER_SC_TOKENS
    n_steps = PEER_SC_TOKENS * PEER_HEADS
    ring = PEER_SC_RING
    worker = lax.axis_index("s") * SC_CORES + lax.axis_index("c")
    base = worker * tokens_per_worker

    def stage(blk, slot):
        tok0 = base + blk * PEER_SC_TOKENS
        return (pltpu.make_async_copy(idx_hbm.at[pl.ds(tok0, PEER_SC_TOKENS)], idx_v.at[slot], sems.at[ring]),
                pltpu.make_async_copy(x_hbm.at[pl.ds(tok0, PEER_SC_TOKENS)], x_v.at[slot], sems.at[ring + 1]))

    def gather(slot, s, b):
        ids = idx_v[slot, s // PEER_HEADS, pl.ds((s % PEER_HEADS) * PEER_TOPK, PEER_TOPK)]
        return pltpu.make_async_copy(tab_hbm.at[ids], rows_v.at[b], sems.at[b])

    for copy in stage(0, 0):
        copy.start()
    for copy in stage(0, 0):
        copy.wait()
    for b in range(ring - 1):
        gather(0, b, b).start()

    @pl.loop(0, n_blocks)
    def _(blk):
        slot = blk % 2
        has_next = blk + 1 < n_blocks

        @pl.when(has_next)
        def _():
            for copy in stage(blk + 1, 1 - slot):
                copy.start()

        if zero_out:
            @pl.loop(0, PEER_SC_TOKENS)
            def _(t):
                @plsc.parallel_loop(0, out_v.shape[1] // SC_LANES, unroll=PEER_SC_UNROLL)
                def _(c):
                    out_v[t, pl.ds(pl.multiple_of(c * SC_LANES, SC_LANES), SC_LANES)] = (
                        jnp.zeros((SC_LANES,), jnp.float32))

        @pl.loop(0, n_steps, step=ring)
        def _(s0):
            for b in range(ring):
                s = s0 + b
                gather(slot, s, b).wait()
                ahead = s + ring - 1
                ahead_buf = (b + ring - 1) % ring

                @pl.when(ahead < n_steps)
                def _():
                    gather(slot, ahead, ahead_buf).start()

                if b >= 1:
                    @pl.when(jnp.logical_and(ahead >= n_steps, has_next))
                    def _():
                        if b == 1:
                            for copy in stage(blk + 1, 1 - slot):
                                copy.wait()
                        gather(1 - slot, ahead - n_steps, ahead_buf).start()

                compute(slot, s // PEER_HEADS, s % PEER_HEADS, b, x_v, rows_v, out_v)

        pltpu.sync_copy(out_v, out_hbm.at[pl.ds(base + blk * PEER_SC_TOKENS, PEER_SC_TOKENS)])


def _peer_dots_compute(slot, t, hd, b, h_v, rows_v, dots_v):
    lane = lax.iota(jnp.int32, SC_LANES)

    def col_step(c, accs):
        off = pl.multiple_of(c * SC_LANES, SC_LANES)
        hv = h_v[slot, t, pl.ds(off, SC_LANES)]
        return tuple(accs[r] + rows_v[b, r, pl.ds(off, SC_LANES)] * hv for r in range(PEER_TOPK))

    accs = plsc.parallel_loop(
        0, D_MODEL // SC_LANES, unroll=PEER_SC_UNROLL,
        carry=tuple(jnp.zeros((SC_LANES,), jnp.float32) for _ in range(PEER_TOPK)))(col_step)
    res = jnp.zeros((SC_LANES,), jnp.float32)
    for r in range(PEER_TOPK):
        res = jnp.where(lane == r, jnp.sum(accs[r]), res)
    dots_v[t, pl.ds(hd * PEER_TOPK, PEER_TOPK)] = res


def _peer_mix_compute(slot, t, hd, b, act_v, rows_v, out_v):
    svec = jnp.full((SC_LANES,), slot, jnp.int32)
    tvec = jnp.full((SC_LANES,), t, jnp.int32)
    weights = [plsc.load_gather(act_v, [svec, tvec, jnp.full((SC_LANES,), hd * PEER_TOPK + r, jnp.int32)])
               for r in range(PEER_TOPK)]

    @plsc.parallel_loop(0, D_MODEL // SC_LANES, unroll=PEER_SC_UNROLL)
    def _(c):
        off = pl.multiple_of(c * SC_LANES, SC_LANES)
        a = rows_v[b, 0, pl.ds(off, SC_LANES)] * weights[0]
        for r in range(1, PEER_TOPK):
            a = a + rows_v[b, r, pl.ds(off, SC_LANES)] * weights[r]
        plsc.addupdate(out_v.at[t, pl.ds(off, SC_LANES)], a)


def _peer_expert_dots(h, experts, u_tab):
    T = h.shape[0]
    assert T % (SC_CORES * SC_SUBCORES * PEER_SC_TOKENS) == 0
    assert (PEER_SC_TOKENS * PEER_HEADS) % PEER_SC_RING == 0 and PEER_TOPK == SC_LANES
    body = partial(_peer_sc_body, _peer_dots_compute, False)
    return _peer_sc_call(body, T, PEER_SLOTS, D_MODEL)(h, experts, u_tab)


def _peer_expert_mix(act, experts, v_tab):
    T = act.shape[0]
    assert T % (SC_CORES * SC_SUBCORES * PEER_SC_TOKENS) == 0
    body = partial(_peer_sc_body, _peer_mix_compute, True)
    return _peer_sc_call(body, T, D_MODEL, PEER_SLOTS)(act, experts, v_tab)


PEER_TM = 256


def _extract_topk(s, k, payload=None):
    R = s.shape[0]
    riota = lax.broadcasted_iota(jnp.int32, s.shape, 0)
    vals, rows = [], []
    for _ in range(k):
        m = jnp.max(s, axis=0, keepdims=True)
        pos = jnp.min(jnp.where(s == m, riota, R), axis=0, keepdims=True)
        hit = riota == pos
        vals.append(m)
        rows.append(pos if payload is None else jnp.max(jnp.where(hit, payload, -1), axis=0, keepdims=True))
        s = jnp.where(hit, -jnp.inf, s)
    return jnp.concatenate(vals, axis=0), jnp.concatenate(rows, axis=0)


def _peer_route_kernel(x_ref, sc_ref, sh_ref, wq_ref, keys_ref, h_ref, ex_ref, gate_ref, q_ref, ext_ref, gt_ref):
    K = PEER_TOPK
    f32 = jnp.float32
    h = x_ref[0] * (1.0 + sc_ref[0]) + sh_ref[0]
    h_ref[0] = h
    q = jnp.dot(h.astype(wq_ref.dtype), wq_ref[...], preferred_element_type=f32)
    for hd in range(PEER_HEADS):
        q_ref[hd] = q[:, hd * PEER_KEY_DIM:(hd + 1) * PEER_KEY_DIM].astype(q_ref.dtype)

    def head_body(hd, carry):
        qh = q_ref[hd]
        s1 = lax.dot_general(keys_ref[hd, 0], qh[:, :PEER_HALF], _NT, preferred_element_type=f32)
        s2 = lax.dot_general(keys_ref[hd, 1], qh[:, PEER_HALF:], _NT, preferred_element_type=f32)
        v1, i1 = _extract_topk(s1, K)
        v2, i2 = _extract_topk(s2, K)
        tm = v1.shape[1]
        cand_rows, cidx_rows = [], []
        for a in range(K):
            nb = K // (a + 1)
            cand_rows.append(v1[a:a + 1, :] + v2[:nb, :])
            cidx_rows.append(i1[a:a + 1, :] * PEER_N_KEYS + i2[:nb, :])
        n_pad = -sum(r.shape[0] for r in cand_rows) % 8
        cand_rows.append(jnp.full((n_pad, tm), -jnp.inf, f32))
        cidx_rows.append(jnp.full((n_pad, tm), -1, jnp.int32))
        cand = jnp.concatenate(cand_rows, axis=0)
        cidx = jnp.concatenate(cidx_rows, axis=0)
        top_s, experts = _extract_topk(cand, K, payload=cidx)
        e = jnp.exp(top_s - top_s[0:1, :])
        gt_ref[pl.ds(pl.multiple_of(hd * K, K), K), :] = e / jnp.sum(e, axis=0, keepdims=True)
        ext_ref[pl.ds(pl.multiple_of(hd * K, K), K), :] = experts
        return carry

    lax.fori_loop(0, PEER_HEADS, head_body, 0)
    ex_ref[0] = ext_ref[...].T
    gate_ref[0] = gt_ref[...].T


def _peer_route(x, sc, sh, w_pq, sub_keys):
    B, S, D = x.shape
    tm = PEER_TM
    assert S % tm == 0
    return pl.pallas_call(
        _peer_route_kernel,
        grid=(B, S // tm),
        in_specs=[
            pl.BlockSpec((1, tm, D), lambda b, i: (b, i, 0)),
            pl.BlockSpec((1, 1, D), lambda b, i: (b, 0, 0)),
            pl.BlockSpec((1, 1, D), lambda b, i: (b, 0, 0)),
            pl.BlockSpec((D, PEER_HEADS * PEER_KEY_DIM), lambda b, i: (0, 0)),
            pl.BlockSpec((PEER_HEADS, 2, PEER_N_KEYS, PEER_HALF), lambda b, i: (0, 0, 0, 0)),
        ],
        out_specs=[
            pl.BlockSpec((1, tm, D), lambda b, i: (b, i, 0)),
            pl.BlockSpec((1, tm, PEER_SLOTS), lambda b, i: (b, i, 0)),
            pl.BlockSpec((1, tm, PEER_SLOTS), lambda b, i: (b, i, 0)),
        ],
        out_shape=[
            jax.ShapeDtypeStruct((B, S, D), jnp.float32),
            jax.ShapeDtypeStruct((B, S, PEER_SLOTS), jnp.int32),
            jax.ShapeDtypeStruct((B, S, PEER_SLOTS), jnp.float32),
        ],
        scratch_shapes=[
            pltpu.VMEM((PEER_HEADS, tm, PEER_KEY_DIM), MM_DTYPE),
            pltpu.VMEM((PEER_SLOTS, tm), jnp.int32),
            pltpu.VMEM((PEER_SLOTS, tm), jnp.float32),
        ],
        compiler_params=pltpu.CompilerParams(dimension_semantics=("parallel", "parallel")),
        name="peer_route",
    )(x, sc[:, None, :], sh[:, None, :], w_pq.astype(MM_DTYPE), sub_keys.astype(MM_DTYPE))


def _peer_pre(h2, experts, u_tab):
    B, S, D = h2.shape
    return _peer_expert_dots(h2.reshape(B * S, D), experts.reshape(B * S, PEER_SLOTS), u_tab)


def _peer_act_kernel(pre_ref, gate_ref, o_ref):
    pre = pre_ref[...]
    o_ref[...] = 0.5 * pre * (1.0 + lax.erf(pre * (2.0 ** -0.5))) * gate_ref[...]


def _peer_act(pre, gates):
    T, W = pre.shape
    tm = min(8 * ROW_TILE, T)
    assert T % tm == 0
    spec = pl.BlockSpec((tm, W), lambda i: (i, 0))
    return pl.pallas_call(
        _peer_act_kernel,
        grid=(T // tm,),
        in_specs=[spec, spec],
        out_specs=spec,
        out_shape=jax.ShapeDtypeStruct((T, W), jnp.float32),
        name="peer_act",
    )(pre, gates.reshape(T, W))


def _peer_out(act, experts, v_tab):
    B, S, _ = experts.shape
    return _peer_expert_mix(act, experts.reshape(B * S, PEER_SLOTS), v_tab).reshape(B, S, D_MODEL)


def kernel(x, c, w_ada, b_ada, w_in, rw_mu, rw_w0, rw_w2, rw_a0, rw_a2, rw_g2, rw_k_k, rw_k_a, rw_r_k, rw_gn_g, rw_gn_b, dsa_kv_g, idx_k_g, idx_k_b, rel_bias, w_br_a, w_br_b, w_out, ln1_g, ln1_b, peer_wq, peer_keys, peer_u, peer_v, ln2_g, ln2_b):
    assert w_in.shape[0] == DEPTH == 1
    l = 0
    mod = jax.nn.silu(c) @ w_ada[l] + b_ada[l]

    w_rw, w_q, w_kv, w_qi, w_ki, w_wi, w_ga, w_gb = jnp.split(w_in[l], _split_points(IN_SIZES), axis=-1)
    small_pad = jnp.zeros((D_MODEL, SMALL_COLS - SMALL_WI - IDX_HEADS), w_in.dtype)
    w_small = jnp.concatenate([w_qi, w_kv, w_ki, w_wi, small_pad], axis=-1)
    w_gates = jnp.concatenate([w_ga, w_gb], axis=-1)

    B, S, _ = x.shape
    seg_len = S // SEQ_SEGMENTS
    dsa_topk = min(TOPK_MAX, S // 4)
    bias_tiles = _dsa_bias_tiles(rel_bias, DSA_TQ)

    def project_and_dsa(window, sc1, sh1, seg, keys_so_far):
        z_rw, z_q, z_small, z_g = _proj_all(x, window, sc1, sh1, (w_rw, w_q, w_small, w_gates),
                                            (jnp.float32, MM_DTYPE, jnp.float32, jnp.float32))
        z_wi_t = jnp.swapaxes(z_small[..., SMALL_WI:SMALL_WI + IDX_HEADS], 1, 2)
        qi, ckv, kidx = _dsa_prep(z_small, dsa_kv_g[l], idx_k_g[l], idx_k_b[l])
        if keys_so_far is not None:
            ckv = jnp.concatenate([keys_so_far[0], ckv], axis=1)
            kidx = jnp.concatenate([keys_so_far[1], kidx], axis=1)
        y_b = _dsa_attention(z_q, qi, z_wi_t, ckv, kidx, bias_tiles, dsa_topk,
                             seg * (seg_len // DSA_TQ), MM_DTYPE)
        return (y_b, z_rw, z_g), (ckv, kidx)

    def rwkv_and_merge(window, gt1, carry, y_b, z_rw, z_g):
        y_a, carry = _rwkv7_time_mix(z_rw, carry, rw_mu[l], rw_w0[l], rw_w2[l], rw_a0[l], rw_a2[l], rw_g2[l],
                                     rw_k_k[l], rw_k_a[l], rw_r_k[l], rw_gn_g[l], rw_gn_b[l], MM_DTYPE)
        return _merge_ln(x, window, gt1, y_a, y_b, z_g, w_br_a[l], w_br_b[l], w_out[l], ln1_g[l], ln1_b[l]), carry

    def finish(st, act):
        y2 = _peer_out(act, st["experts"], peer_v[l])
        return _residual_ln(st["x1"], st["gt2"], y2, ln2_g[l], ln2_b[l])

    assert B % BATCH_GROUPS == 0 and S % (SEQ_SEGMENTS * DSA_TQ) == 0
    gsz = B // BATCH_GROUPS
    outs = []
    prev = None
    for g in range(BATCH_GROUPS):
        sh1, sc1, gt1, sh2, sc2, gt2 = jnp.split(mod[g * gsz:(g + 1) * gsz], 6, axis=-1)
        rw_carry = _rwkv7_init_carry(gsz)
        keys_so_far = None
        for seg in range(SEQ_SEGMENTS):
            window = (g * gsz, gsz, seg * seg_len, seg_len)
            sc1_g = sc1
            if prev is not None:
                sc1_g, prev["experts"] = lax.optimization_barrier((sc1_g, prev["experts"]))
            proj, keys_so_far = project_and_dsa(window, sc1_g, sh1, seg, keys_so_far)
            if prev is not None:
                proj, prev["pre"], outs = lax.optimization_barrier((proj, prev["pre"], outs))
                outs.append(finish(prev, _peer_act(prev["pre"], prev["gates"])))
            x1, rw_carry = rwkv_and_merge(window, gt1, rw_carry, *proj)
            h2, experts, gates = _peer_route(x1, sc2, sh2, peer_wq[l], peer_keys[l])
            prev = dict(x1=x1, gt2=gt2, experts=experts, gates=gates, pre=_peer_pre(h2, experts, peer_u[l]))
    prev["pre"], outs = lax.optimization_barrier((prev["pre"], outs))
    outs.append(finish(prev, _peer_act(prev["pre"], prev["gates"])))
    rows = [jnp.concatenate(outs[g * SEQ_SEGMENTS:(g + 1) * SEQ_SEGMENTS], axis=1) for g in range(BATCH_GROUPS)]
    return jnp.concatenate(rows, axis=0)
```

```python
import math
from functools import partial

import jax
import jax.numpy as jnp
import numpy as np
from jax import lax
from jax.experimental import pallas as pl
from jax.experimental.pallas import tpu as pltpu
from jax.experimental.pallas import tpu_sc as plsc

D_MODEL = 1024
RW_HEADS = 8
RW_HEAD_DIM = 64
RW_DIM = 512
RW_DECAY_LORA = 64
RW_A_LORA = 64
RW_GATE_LORA = 128
RW_COLS = 3 * RW_DIM + RW_DECAY_LORA + RW_A_LORA + RW_GATE_LORA
RW_GN_EPS = 64e-5
DSA_HEADS = 8
DSA_LATENT = 128
DSA_Q_DIM = DSA_HEADS * DSA_LATENT
IDX_HEADS = 4
IDX_DIM = 64
TOPK_MAX = 256
REL_BUCKETS = 32
REL_MAX_DIST = 128
IN_SIZES = (RW_COLS, DSA_Q_DIM, DSA_LATENT, IDX_HEADS * IDX_DIM, IDX_DIM, IDX_HEADS, D_MODEL, D_MODEL)
IN_COLS = sum(IN_SIZES)
PEER_HEADS = 8
PEER_N_KEYS = 128
PEER_KEY_DIM = 128
PEER_HALF = 64
PEER_TOPK = 16
LN_EPS = 1e-5
DEPTH = 1
DEEPNORM_ALPHA = (2.0 * DEPTH) ** 0.25

LANES = 128
SMALL_QI = 0
SMALL_KV = SMALL_QI + IDX_HEADS * IDX_DIM
SMALL_KI = SMALL_KV + DSA_LATENT
SMALL_WI = SMALL_KI + IDX_DIM
SMALL_COLS = 4 * LANES
MM_DTYPE = jnp.bfloat16
BATCH_GROUPS = 8
SEQ_SEGMENTS = 2


def _split_points(sizes):
    return np.cumsum(sizes)[:-1].tolist()


def _proj_all_kernel(x_ref, sc_ref, sh_ref, *refs):
    w_refs, z_refs = refs[:len(refs) // 2], refs[len(refs) // 2:]
    h = (x_ref[0] * (1.0 + sc_ref[0]) + sh_ref[0]).astype(w_refs[0].dtype)
    for w_ref, z_ref in zip(w_refs, z_refs):
        z_ref[0] = jnp.dot(h, w_ref[...], preferred_element_type=jnp.float32).astype(z_ref.dtype)


def _proj_all(x, window, sc, sh, weights, out_dtypes, tm=256):
    b0, B, t0, S = window
    D = x.shape[2]
    assert S % tm == 0 and t0 % tm == 0
    weights = [w.astype(MM_DTYPE) for w in weights]
    vec = pl.BlockSpec((1, 1, D), lambda b, i: (b, 0, 0))
    out_bytes = sum(tm * w.shape[1] * jnp.dtype(dt).itemsize for w, dt in zip(weights, out_dtypes))
    vmem_estimate = 2 * (sum(w.size for w in weights) * 2 + tm * D * 4 + out_bytes) + 4 * tm * D * 4
    return pl.pallas_call(
        _proj_all_kernel,
        grid=(B, S // tm),
        in_specs=[pl.BlockSpec((1, tm, D), lambda b, i: (b + b0, i + t0 // tm, 0)), vec, vec]
        + [pl.BlockSpec(w.shape, lambda b, i: (0, 0)) for w in weights],
        out_specs=[pl.BlockSpec((1, tm, w.shape[1]), lambda b, i: (b, i, 0)) for w in weights],
        out_shape=[jax.ShapeDtypeStruct((B, S, w.shape[1]), dt) for w, dt in zip(weights, out_dtypes)],
        compiler_params=pltpu.CompilerParams(dimension_semantics=("parallel", "parallel"),
                                             vmem_limit_bytes=vmem_estimate),
        name="proj_all",
    )(x, sc[:, None, :], sh[:, None, :], *weights)


ROW_TILE = 256


def _layer_norm(x, g, b):
    mu = jnp.mean(x, -1, keepdims=True)
    var = jnp.mean(jnp.square(x - mu), -1, keepdims=True)
    return (x - mu) * lax.rsqrt(var + LN_EPS) * g + b


def _merge_ln_kernel(x_ref, gt_ref, sc_ref, sh_ref, ya_ref, yb_ref, wg_ref, wa_ref, wb_ref, wo_ref, g_ref, b_ref,
                     o_ref):
    f32 = jnp.float32
    D = x_ref.shape[2]
    ya = jnp.dot(ya_ref[0], wa_ref[...], preferred_element_type=f32)
    yb = jnp.dot(yb_ref[0], wb_ref[...], preferred_element_type=f32)
    h = (x_ref[0] * (1.0 + sc_ref[0]) + sh_ref[0]).astype(wg_ref.dtype)
    zg = jnp.dot(h, wg_ref[...], preferred_element_type=f32)
    merged = jax.nn.sigmoid(zg[:, :D]) * ya + jax.nn.sigmoid(zg[:, D:]) * yb
    mix = jnp.dot(merged.astype(wo_ref.dtype), wo_ref[...], preferred_element_type=f32)
    o_ref[0] = _layer_norm(DEEPNORM_ALPHA * x_ref[0] + gt_ref[0] * mix, g_ref[...], b_ref[...])


def _merge_ln(x, window, gt, sc, sh, y_a, y_b, w_g, w_a, w_b, w_o, ln_g, ln_b):
    b0, B, t0, S = window
    D = x.shape[2]
    tm = ROW_TILE
    assert S % tm == 0 and t0 % tm == 0
    tok = lambda width: pl.BlockSpec((1, tm, width), lambda b, i: (b, i, 0))
    x_spec = pl.BlockSpec((1, tm, D), lambda b, i: (b + b0, i + t0 // tm, 0))
    vec = pl.BlockSpec((1, 1, D), lambda b, i: (b, 0, 0))
    full = lambda a: pl.BlockSpec(a.shape, lambda b, i: (0,) * a.ndim)
    w_g, w_a, w_b, w_o = (w.astype(MM_DTYPE) for w in (w_g, w_a, w_b, w_o))
    ln_g, ln_b = ln_g[None], ln_b[None]
    weight_bytes = 2 * sum(w.size for w in (w_g, w_a, w_b, w_o))
    vmem_estimate = 2 * weight_bytes + 16 * tm * D * 4
    return pl.pallas_call(
        _merge_ln_kernel,
        grid=(B, S // tm),
        in_specs=[x_spec, vec, vec, vec, tok(y_a.shape[2]), tok(y_b.shape[2]),
                  full(w_g), full(w_a), full(w_b), full(w_o), full(ln_g), full(ln_b)],
        out_specs=tok(D),
        out_shape=jax.ShapeDtypeStruct((B, S, D), jnp.float32),
        compiler_params=pltpu.CompilerParams(dimension_semantics=("parallel", "parallel"),
                                             vmem_limit_bytes=vmem_estimate),
        name="merge_ln",
    )(x, gt[:, None, :], sc[:, None, :], sh[:, None, :], y_a, y_b, w_g, w_a, w_b, w_o, ln_g, ln_b)


def _residual_ln_kernel(x_ref, gt_ref, y_ref, g_ref, b_ref, o_ref):
    o_ref[0] = _layer_norm(DEEPNORM_ALPHA * x_ref[0] + gt_ref[0] * y_ref[0], g_ref[...], b_ref[...])


def _residual_ln(x, gt, y, ln_g, ln_b):
    B, S, D = x.shape
    tm = ROW_TILE
    assert S % tm == 0
    tok = pl.BlockSpec((1, tm, D), lambda b, i: (b, i, 0))
    vec = pl.BlockSpec((1, D), lambda b, i: (0, 0))
    return pl.pallas_call(
        _residual_ln_kernel,
        grid=(B, S // tm),
        in_specs=[tok, pl.BlockSpec((1, 1, D), lambda b, i: (b, 0, 0)), tok, vec, vec],
        out_specs=tok,
        out_shape=jax.ShapeDtypeStruct((B, S, D), jnp.float32),
        compiler_params=pltpu.CompilerParams(dimension_semantics=("parallel", "parallel")),
        name="residual_ln",
    )(x, gt[:, None, :], y, ln_g[None], ln_b[None])


def _t5_bucket(n):
    n = jnp.maximum(n, 0)
    max_exact = REL_BUCKETS // 2
    nf = jnp.maximum(n, 1).astype(jnp.float32)
    large = max_exact + (jnp.log(nf / max_exact) / math.log(REL_MAX_DIST / max_exact)
                         * (REL_BUCKETS - max_exact)).astype(jnp.int32)
    large = jnp.minimum(large, REL_BUCKETS - 1)
    return jnp.where(n < max_exact, n, large)


RW_CHUNK = 64
RW_INV_BLOCK = 16
_NN = (((1,), (0,)), ((), ()))
_NT = (((1,), (1,)), ((), ()))
_BNN = (((2,), (1,)), ((0,), (0,)))
_BNT = (((2,), (2,)), ((0,), (0,)))


def _dot_f32(a, b, dims=_NN):
    return lax.dot_general(a, b, dims, precision=lax.Precision.HIGHEST,
                           preferred_element_type=jnp.float32)


def _dot_bf16x3(a, b, dims=_NN):
    f32, bf = jnp.float32, jnp.bfloat16
    a_hi, b_hi = a.astype(bf), b.astype(bf)
    a_lo = (a - a_hi.astype(f32)).astype(bf)
    b_lo = (b - b_hi.astype(f32)).astype(bf)
    out = lax.dot_general(a_hi, b_hi, dims, preferred_element_type=f32)
    out = out + lax.dot_general(a_hi, b_lo, dims, preferred_element_type=f32)
    return out + lax.dot_general(a_lo, b_hi, dims, preferred_element_type=f32)


def _bf16_terms(a):
    f32, bf = jnp.float32, jnp.bfloat16
    hi = a.astype(bf)
    r1 = a - hi.astype(f32)
    mid = r1.astype(bf)
    lo = (r1 - mid.astype(f32)).astype(bf)
    return hi, mid, lo


def _dot_lhs_split(a, b01):
    b = b01.astype(jnp.bfloat16)
    return sum(jnp.dot(t, b, preferred_element_type=jnp.float32) for t in _bf16_terms(a))


def _dot_rhs_split(a01, b):
    a = a01.astype(jnp.bfloat16)
    return sum(jnp.dot(a, t, preferred_element_type=jnp.float32) for t in _bf16_terms(b))


def _rwkv_kernel(z_ref, m0_ref, p0_ref, mu_ref, w0_ref, w2_ref, a0_ref, a2_ref, g2_ref, kk_ref, ka_ref, rk_ref,
                 gng_ref, gnb_ref, bd_ref, o_ref, mo_ref, po_ref, m_ref, prev_ref, y_ref):
    C = z_ref.shape[1]
    N = RW_HEAD_DIM
    f32 = jnp.float32
    dot3 = _dot_bf16x3

    @pl.when(pl.program_id(1) == 0)
    def _():
        m_ref[...] = m0_ref[0]
        prev_ref[...] = p0_ref[0]

    z = z_ref[0]
    row = lax.broadcasted_iota(jnp.int32, z.shape, 0)
    shifted = jnp.where(row == 0, prev_ref[...], pltpu.roll(z, 1, axis=0))
    prev_ref[...] = z[C - 1:C, :]
    zs = z + (shifted - z) * mu_ref[...]
    r = zs[:, 0:RW_DIM]
    k = zs[:, RW_DIM:2 * RW_DIM]
    v = zs[:, 2 * RW_DIM:3 * RW_DIM]
    o1 = 3 * RW_DIM
    wl = zs[:, o1:o1 + RW_DECAY_LORA]
    al = zs[:, o1 + RW_DECAY_LORA:o1 + RW_DECAY_LORA + RW_A_LORA]
    gl = zs[:, o1 + RW_DECAY_LORA + RW_A_LORA:]

    bd = bd_ref[...]
    log_w = -jax.nn.softplus(-(w0_ref[...] + dot3(jnp.tanh(wl), w2_ref[...]))) - 0.5
    ldec = -jnp.exp(log_w)
    a_lr = jax.nn.sigmoid(a0_ref[...] + dot3(al, a2_ref[...]))
    g = dot3(jax.nn.sigmoid(gl), g2_ref[...])
    kk = k * kk_ref[...]
    kk = kk * lax.rsqrt(jnp.maximum(_dot_lhs_split(kk * kk, bd), 1e-24))
    k2 = k * (1.0 + (a_lr - 1.0) * ka_ref[...])
    a_vec = -kk
    b_vec = kk * a_lr

    ti = lax.broadcasted_iota(jnp.int32, (C, C), 0)
    tj = lax.broadcasted_iota(jnp.int32, (C, C), 1)
    cum = _dot_rhs_split((ti >= tj).astype(f32), ldec)
    cum_last = cum[C - 1:C, :]
    w_incl = jnp.exp(cum)
    w_excl = jnp.exp(cum - ldec)
    w_inv = jnp.exp(-cum)
    w_end = jnp.exp(cum_last - cum)
    w_all = jnp.exp(cum_last)
    a_t = a_vec * w_excl
    r_t = r * w_incl
    b_t = b_vec * w_inv
    k_t = k2 * w_inv
    b_e = b_vec * w_end
    k_e = k2 * w_end

    strict = ti > tj
    incl = ti >= tj
    bi, bj = ti // RW_INV_BLOCK, tj // RW_INV_BLOCK
    same_blk = bi == bj
    pair_blk = jnp.logical_and((bi // 2) == (bj // 2), jnp.logical_not(same_blk))
    half_blk = (bi // 2) != (bj // 2)
    eye = (ti == tj).astype(f32)

    H = RW_HEADS
    heads = lambda x: jnp.stack([x[:, h * N:(h + 1) * N] for h in range(H)], axis=0)
    bmm = lambda a, b: dot3(a, b, _BNN)
    A, Rt, Bt, Kt, Be, Ke, V = (heads(t) for t in (a_t, r_t, b_t, k_t, b_e, k_e, v))
    gm = dot3(jnp.concatenate([A, Rt], axis=1), jnp.concatenate([Bt, Kt], axis=1), _BNT)
    a_ab = jnp.where(strict, gm[:, :C, :C], 0.0)
    a_ak = jnp.where(strict, gm[:, :C, C:], 0.0)
    a_rb = jnp.where(incl, gm[:, C:, :C], 0.0)
    a_rk = jnp.where(incl, gm[:, C:, C:], 0.0)
    d1 = jnp.where(same_blk, a_ab, 0.0)
    xinv = eye + d1
    d2 = bmm(d1, d1)
    xinv = xinv + bmm(xinv, d2)
    d4 = bmm(d2, d2)
    xinv = xinv + bmm(xinv, d4)
    d8 = bmm(d4, d4)
    xinv = xinv + bmm(xinv, d8)
    xinv = xinv + bmm(bmm(xinv, jnp.where(pair_blk, a_ab, 0.0)), xinv)
    xinv = xinv + bmm(bmm(xinv, jnp.where(half_blk, a_ab, 0.0)), xinv)

    av = bmm(jnp.concatenate([a_ak, a_rk], axis=1), V)
    p = bmm(xinv, jnp.concatenate([A, av[:, :C]], axis=2))
    qm = bmm(a_rb, p)
    q1 = Rt + qm[:, :, :N]
    q2 = qm[:, :, N:] + av[:, C:]
    gmat = bmm(jnp.swapaxes(Be, 1, 2), p)
    g1 = eye * heads(w_all) + gmat[:, :, :N]
    g2 = gmat[:, :, N:] + bmm(jnp.swapaxes(Ke, 1, 2), V)
    m = m_ref[...]
    yh = _dot_f32(q1, m, _BNN) + q2
    m_ref[...] = _dot_f32(g1, m, _BNN) + g2
    for h in range(H):
        y_ref[:, h * N:(h + 1) * N] = yh[h]

    y = y_ref[...]
    mean = _dot_lhs_split(y, bd) * (1.0 / N)
    yc = y - mean
    var = _dot_lhs_split(yc * yc, bd) * (1.0 / N)
    yn = yc * lax.rsqrt(var + RW_GN_EPS) * gng_ref[...] + gnb_ref[...]
    bonus = _dot_lhs_split(r * k2 * rk_ref[...], bd) * v
    o_ref[0] = ((yn + bonus) * g).astype(o_ref.dtype)

    @pl.when(pl.program_id(1) == pl.num_programs(1) - 1)
    def _():
        mo_ref[0] = m_ref[...]
        po_ref[0] = prev_ref[...]


def _rwkv7_init_carry(batch):
    return (jnp.zeros((batch, RW_HEADS, RW_HEAD_DIM, RW_HEAD_DIM), jnp.float32),
            jnp.zeros((batch, 1, RW_COLS), jnp.float32))


def _rwkv7_time_mix(z_rw, carry, mu, w0, w2, a0, a2, g2, k_k, k_a, r_k, gn_g, gn_b, out_dtype):
    B, S, _ = z_rw.shape
    C = RW_CHUNK
    assert S % C == 0 and C == RW_HEAD_DIM and C % (4 * RW_INV_BLOCK) == 0
    hid = jnp.arange(RW_DIM) // RW_HEAD_DIM
    bd = (hid[:, None] == hid[None, :]).astype(jnp.float32)
    row = lambda a: a.reshape(1, -1)
    full = lambda shape: pl.BlockSpec(shape, lambda b, c: (0,) * len(shape))
    state_spec = pl.BlockSpec((1, RW_HEADS, RW_HEAD_DIM, RW_HEAD_DIM), lambda b, c: (b, 0, 0, 0))
    prev_spec = pl.BlockSpec((1, 1, RW_COLS), lambda b, c: (b, 0, 0))
    y, m_out, p_out = pl.pallas_call(
        _rwkv_kernel,
        grid=(B, S // C),
        in_specs=[
            pl.BlockSpec((1, C, RW_COLS), lambda b, c: (b, c, 0)), state_spec, prev_spec,
            full((1, RW_COLS)), full((1, RW_DIM)), full((RW_DECAY_LORA, RW_DIM)), full((1, RW_DIM)),
            full((RW_A_LORA, RW_DIM)), full((RW_GATE_LORA, RW_DIM)), full((1, RW_DIM)), full((1, RW_DIM)),
            full((1, RW_DIM)), full((1, RW_DIM)), full((1, RW_DIM)), full((RW_DIM, RW_DIM)),
        ],
        out_specs=[pl.BlockSpec((1, C, RW_DIM), lambda b, c: (b, c, 0)), state_spec, prev_spec],
        out_shape=[jax.ShapeDtypeStruct((B, S, RW_DIM), out_dtype),
                   jax.ShapeDtypeStruct(carry[0].shape, jnp.float32),
                   jax.ShapeDtypeStruct(carry[1].shape, jnp.float32)],
        scratch_shapes=[
            pltpu.VMEM((RW_HEADS, RW_HEAD_DIM, RW_HEAD_DIM), jnp.float32),
            pltpu.VMEM((1, RW_COLS), jnp.float32),
            pltpu.VMEM((C, RW_DIM), jnp.float32),
        ],
        compiler_params=pltpu.CompilerParams(dimension_semantics=("parallel", "arbitrary")),
        name="rwkv7_time_mix",
    )(z_rw, carry[0], carry[1], row(mu), row(w0), w2, row(a0), a2, g2, row(k_k), row(k_a), row(r_k),
      row(gn_g), row(gn_b), bd)
    return y, (m_out, p_out)


DSA_TQ = 256
MASK_NEG = -1e30
INT_MIN = -2 ** 31
KEY_NEG_INF = -2139095041
THRESH_BITS = 32


def _dsa_prep_kernel(z_ref, kvg_ref, kig_ref, kib_ref, qi_ref, kv_ref, ki_ref):
    z = z_ref[0]
    qi_ref[0] = z[:, SMALL_QI:SMALL_KV].astype(qi_ref.dtype)
    kv = z[:, SMALL_KV:SMALL_KI]
    ms = jnp.mean(jnp.square(kv), -1, keepdims=True)
    kv_ref[0] = (kv * lax.rsqrt(ms + LN_EPS) * kvg_ref[...]).astype(kv_ref.dtype)
    ki = z[:, SMALL_KI:SMALL_WI]
    mu = jnp.mean(ki, -1, keepdims=True)
    var = jnp.mean(jnp.square(ki - mu), -1, keepdims=True)
    ki_ref[0] = ((ki - mu) * lax.rsqrt(var + LN_EPS) * kig_ref[...] + kib_ref[...]).astype(ki_ref.dtype)


def _dsa_prep(z_small, kv_g, ki_g, ki_b, tm=512):
    B, S, W = z_small.shape
    assert S % tm == 0
    return pl.pallas_call(
        _dsa_prep_kernel,
        grid=(B, S // tm),
        in_specs=[
            pl.BlockSpec((1, tm, W), lambda b, i: (b, i, 0)),
            pl.BlockSpec((1, DSA_LATENT), lambda b, i: (0, 0)),
            pl.BlockSpec((1, IDX_DIM), lambda b, i: (0, 0)),
            pl.BlockSpec((1, IDX_DIM), lambda b, i: (0, 0)),
        ],
        out_specs=[
            pl.BlockSpec((1, tm, IDX_HEADS * IDX_DIM), lambda b, i: (b, i, 0)),
            pl.BlockSpec((1, tm, DSA_LATENT), lambda b, i: (b, i, 0)),
            pl.BlockSpec((1, tm, IDX_DIM), lambda b, i: (b, i, 0)),
        ],
        out_shape=[
            jax.ShapeDtypeStruct((B, S, IDX_HEADS * IDX_DIM), MM_DTYPE),
            jax.ShapeDtypeStruct((B, S, DSA_LATENT), MM_DTYPE),
            jax.ShapeDtypeStruct((B, S, IDX_DIM), MM_DTYPE),
        ],
        name="dsa_prep",
    )(z_small, kv_g[None], ki_g[None], ki_b[None])


def _sortable_key(s):
    s = jnp.where(s == 0.0, 0.0, s)
    bits = pltpu.bitcast(s, jnp.int32)
    return bits ^ ((bits >> 31) & 0x7FFFFFFF)


def _col_count(mask_i32):
    tk, tq = mask_i32.shape
    return jnp.sum(mask_i32.reshape(tk // 8, 8, tq), axis=0)


def _dsa_kernel(bfar_ref, q_ref, qi_ref, wit_ref, kv_ref, kvt_ref, ki_ref, b0_ref, b1_ref, o_ref,
                key_ref, madd_ref, m_ref, l_ref, acc_ref, *, topk, q_off):
    tq = q_ref.shape[1]
    tk = tq
    i = pl.program_id(1) + q_off
    nj = i + 1
    f32 = jnp.float32
    krow = lax.broadcasted_iota(jnp.int32, (tk, tq), 0)
    qcol = lax.broadcasted_iota(jnp.int32, (tk, tq), 1)

    qi = qi_ref[0]
    wit = wit_ref[0] * (IDX_HEADS ** -0.5)

    def score_chunk(j, carry):
        off = pl.multiple_of(j * tk, tk)
        kc = ki_ref[0, pl.ds(off, tk), :]
        s = jnp.zeros((tk, tq), f32)
        for h in range(IDX_HEADS):
            d = lax.dot_general(kc, qi[:, h * IDX_DIM:(h + 1) * IDX_DIM],
                                (((1,), (1,)), ((), ())), preferred_element_type=f32)
            s = s + wit[h:h + 1, :] * jnp.maximum(d * (IDX_DIM ** -0.5), 0.0)
        causal = (krow + j * tk) <= (qcol + i * tq)
        s = jnp.where(causal, s, -jnp.inf)
        key_ref[j] = _sortable_key(s)
        return carry

    lax.fori_loop(0, nj, score_chunk, 0)

    def count_where(pred_fn):
        def body(j, acc):
            return acc + _col_count(pred_fn(key_ref[j], j).astype(jnp.int32))
        acc = lax.fori_loop(0, nj, body, jnp.zeros((8, tq), jnp.int32))
        return jnp.sum(acc, axis=0, keepdims=True)

    def bit_step(it, t_u):
        bit = THRESH_BITS - 1 - it
        cand_u = t_u | jnp.left_shift(jnp.int32(1), bit)
        cand = cand_u ^ INT_MIN
        cnt = count_where(lambda k, j: k >= cand)
        return jnp.where(cnt >= topk, cand_u, t_u)

    t_u = lax.fori_loop(0, THRESH_BITS, bit_step, jnp.zeros((1, tq), jnp.int32))
    thr = t_u ^ INT_MIN
    cnt_gt = count_where(lambda k, j: k > thr)
    cnt_ge = count_where(lambda k, j: k >= thr)
    is_neg = thr == KEY_NEG_INF
    need = jnp.logical_and(cnt_ge > topk, jnp.logical_not(is_neg))
    n_tie_take = topk - cnt_gt

    thr_open = jnp.where(is_neg, thr, thr - 1)
    any_need = jnp.max(need.astype(jnp.int32)) > 0

    @pl.when(jnp.logical_not(any_need))
    def _():
        def body(j, carry):
            madd_ref[j] = jnp.where(key_ref[j] > thr_open, 0.0, MASK_NEG)
            return carry
        lax.fori_loop(0, nj, body, 0)

    @pl.when(any_need)
    def _():
        s_len = tk * key_ref.shape[0]
        n_bits = max(1, int(math.ceil(math.log2(s_len))))

        def idx_step(it, p):
            bit = n_bits - 1 - it
            cand = p | jnp.left_shift(jnp.int32(1), bit)
            cnt = count_where(
                lambda k, j: jnp.where(k == thr, jnp.where((krow + j * tk) < cand, 1, 0), 0))
            return jnp.where(cnt < n_tie_take, cand, p)

        p_idx = lax.fori_loop(0, n_bits, idx_step, jnp.zeros((1, tq), jnp.int32))
        p_idx = jnp.where(need, p_idx, jnp.where(is_neg, -1, s_len))

        def body(j, carry):
            k = key_ref[j]
            tie_ok = jnp.where((krow + j * tk) <= p_idx, 0.0, MASK_NEG)
            madd_ref[j] = jnp.where(k > thr, 0.0, jnp.where(k == thr, tie_ok, MASK_NEG))
            return carry
        lax.fori_loop(0, nj, body, 0)

    H, d = DSA_HEADS, DSA_LATENT
    m_ref[...] = jnp.full(m_ref.shape, MASK_NEG, f32)
    l_ref[...] = jnp.zeros(l_ref.shape, f32)
    acc_ref[...] = jnp.zeros(acc_ref.shape, f32)
    scale = DSA_LATENT ** -0.5
    q_all = jnp.concatenate([q_ref[0, :, h * d:(h + 1) * d] for h in range(H)], axis=0)

    def attend(j, bias):
        off = pl.multiple_of(j * tk, tk)
        kc = kv_ref[0, pl.ds(off, tk), :]
        kct = kvt_ref[0, :, pl.ds(off, tk)]
        lg = lax.dot_general(kc, q_all, (((1,), (1,)), ((), ())), preferred_element_type=f32)
        lg = lg * scale + bias + jnp.tile(madd_ref[j], (1, H))
        m_old = m_ref[...]
        m_new = jnp.maximum(m_old, jnp.max(lg, axis=0, keepdims=True))
        alpha = jnp.exp(m_old - m_new)
        p = jnp.exp(lg - m_new)
        l_ref[...] = alpha * l_ref[...] + jnp.sum(p, axis=0, keepdims=True)
        acc_ref[...] = alpha * acc_ref[...] + jnp.dot(kct, p.astype(kct.dtype), preferred_element_type=f32)
        m_ref[...] = m_new

    def far_body(j, carry):
        attend(j, bfar_ref[...])
        return carry

    lax.fori_loop(0, jnp.maximum(i - 1, 0), far_body, 0)

    @pl.when(i >= 1)
    def _():
        attend(i - 1, b1_ref[...])

    attend(i, b0_ref[...])

    out_t = acc_ref[...] / l_ref[...]
    for h in range(H):
        o_ref[0, :, h * d:(h + 1) * d] = out_t[:, h * tq:(h + 1) * tq].T.astype(o_ref.dtype)


def _dsa_bias_tiles(rel_bias, tq):
    dist = jnp.arange(2 * tq, dtype=jnp.int32)
    by_dist = rel_bias[_t5_bucket(dist)].T

    def toeplitz(v):
        H, P = v.shape
        skew = jnp.tile(v, (1, tq + 1))[:, :tq * (P + 1)].reshape(H, tq, P + 1)
        return skew[:, ::-1, :tq]

    b0 = toeplitz(jnp.concatenate([jnp.repeat(by_dist[:, :1], tq - 1, axis=1), by_dist[:, :tq + 1]], axis=1))
    b1 = toeplitz(jnp.concatenate([by_dist[:, 1:], by_dist[:, -1:]], axis=1))
    H = by_dist.shape[0]
    b0 = jnp.transpose(b0, (1, 0, 2)).reshape(tq, H * tq)
    b1 = jnp.transpose(b1, (1, 0, 2)).reshape(tq, H * tq)
    return b0, b1, jnp.repeat(by_dist[:, -1], tq)[None, :]


def _dsa_attention(q, qi, z_wi_t, ckv, kidx, bias_tiles, topk, q_off, out_dtype):
    B, Sq, _ = q.shape
    tq = DSA_TQ
    nq = Sq // tq
    S = ckv.shape[1]
    assert Sq % tq == 0 and S == (q_off + nq) * tq and tq >= REL_MAX_DIST and topk <= tq
    ckv_t = jnp.swapaxes(ckv, 1, 2)
    b0, b1, bfar = bias_tiles
    nk = S // tq
    H, d = DSA_HEADS, DSA_LATENT
    f32_bytes = 4
    vmem_estimate = (
        2 * nk * tq * tq * f32_bytes
        + 2 * 2 * tq * H * tq * f32_bytes
        + 2 * (2 * S * d + S * LANES) * 2
        + 6 * tq * H * tq * f32_bytes
        + 4 * tq * H * d * f32_bytes
    )
    grid_spec = pltpu.PrefetchScalarGridSpec(
        num_scalar_prefetch=0,
        grid=(B, nq),
        in_specs=[
            pl.BlockSpec((1, H * tq), lambda b, i: (0, 0)),
            pl.BlockSpec((1, tq, H * d), lambda b, i: (b, i, 0)),
            pl.BlockSpec((1, tq, IDX_HEADS * IDX_DIM), lambda b, i: (b, i, 0)),
            pl.BlockSpec((1, IDX_HEADS, tq), lambda b, i: (b, 0, i)),
            pl.BlockSpec((1, S, d), lambda b, i: (b, 0, 0)),
            pl.BlockSpec((1, d, S), lambda b, i: (b, 0, 0)),
            pl.BlockSpec((1, S, IDX_DIM), lambda b, i: (b, 0, 0)),
            pl.BlockSpec((tq, H * tq), lambda b, i: (0, 0)),
            pl.BlockSpec((tq, H * tq), lambda b, i: (0, 0)),
        ],
        out_specs=pl.BlockSpec((1, tq, H * d), lambda b, i: (b, i, 0)),
        scratch_shapes=[
            pltpu.VMEM((nk, tq, tq), jnp.int32),
            pltpu.VMEM((nk, tq, tq), jnp.float32),
            pltpu.VMEM((1, H * tq), jnp.float32),
            pltpu.VMEM((1, H * tq), jnp.float32),
            pltpu.VMEM((d, H * tq), jnp.float32),
        ],
    )
    return pl.pallas_call(
        partial(_dsa_kernel, topk=topk, q_off=q_off),
        grid_spec=grid_spec,
        out_shape=jax.ShapeDtypeStruct((B, Sq, H * d), out_dtype),
        compiler_params=pltpu.CompilerParams(vmem_limit_bytes=vmem_estimate),
        name="dsa_attention",
    )(bfar, q, qi, z_wi_t, ckv, ckv_t, kidx, b0, b1)


SC_CORES = 2
SC_SUBCORES = 16
SC_LANES = 16
PEER_SC_TOKENS = 8
PEER_SC_RING = 4
PEER_SC_UNROLL = 4
PEER_SLOTS = PEER_HEADS * PEER_TOPK


def _peer_sc_call(body, T, out_width, stage_width):
    mesh = plsc.VectorSubcoreMesh(core_axis_name="c", subcore_axis_name="s")
    return pl.kernel(
        body, mesh=mesh,
        out_type=jax.ShapeDtypeStruct((T, out_width), jnp.float32),
        scratch_types=[
            pltpu.VMEM((2, PEER_SC_TOKENS, PEER_SLOTS), jnp.int32),
            pltpu.VMEM((2, PEER_SC_TOKENS, stage_width), jnp.float32),
            pltpu.VMEM((PEER_SC_RING, PEER_TOPK, D_MODEL), jnp.float32),
            pltpu.VMEM((PEER_SC_TOKENS, out_width), jnp.float32),
            pltpu.SemaphoreType.DMA((PEER_SC_RING + 2,)),
        ],
        compiler_params=pltpu.CompilerParams(needs_layout_passes=False),
    )


def _peer_sc_body(compute, zero_out, x_hbm, idx_hbm, tab_hbm, out_hbm, idx_v, x_v, rows_v, out_v, sems):
    T = idx_hbm.shape[0]
    tokens_per_worker = T // (SC_CORES * SC_SUBCORES)
    n_blocks = tokens_per_worker // PEER_SC_TOKENS
    n_steps = PEER_SC_TOKENS * PEER_HEADS
    ring = PEER_SC_RING
    worker = lax.axis_index("s") * SC_CORES + lax.axis_index("c")
    base = worker * tokens_per_worker

    def stage(blk, slot):
        tok0 = base + blk * PEER_SC_TOKENS
        return (pltpu.make_async_copy(idx_hbm.at[pl.ds(tok0, PEER_SC_TOKENS)], idx_v.at[slot], sems.at[ring]),
                pltpu.make_async_copy(x_hbm.at[pl.ds(tok0, PEER_SC_TOKENS)], x_v.at[slot], sems.at[ring + 1]))

    def gather(slot, s, b):
        ids = idx_v[slot, s // PEER_HEADS, pl.ds((s % PEER_HEADS) * PEER_TOPK, PEER_TOPK)]
        return pltpu.make_async_copy(tab_hbm.at[ids], rows_v.at[b], sems.at[b])

    for copy in stage(0, 0):
        copy.start()
    for copy in stage(0, 0):
        copy.wait()
    for b in range(ring - 1):
        gather(0, b, b).start()

    @pl.loop(0, n_blocks)
    def _(blk):
        slot = blk % 2
        has_next = blk + 1 < n_blocks

        @pl.when(has_next)
        def _():
            for copy in stage(blk + 1, 1 - slot):
                copy.start()

        if zero_out:
            @pl.loop(0, PEER_SC_TOKENS)
            def _(t):
                @plsc.parallel_loop(0, out_v.shape[1] // SC_LANES, unroll=PEER_SC_UNROLL)
                def _(c):
                    out_v[t, pl.ds(pl.multiple_of(c * SC_LANES, SC_LANES), SC_LANES)] = (
                        jnp.zeros((SC_LANES,), jnp.float32))

        @pl.loop(0, n_steps, step=ring)
        def _(s0):
            for b in range(ring):
                s = s0 + b
                gather(slot, s, b).wait()
                ahead = s + ring - 1
                ahead_buf = (b + ring - 1) % ring

                @pl.when(ahead < n_steps)
                def _():
                    gather(slot, ahead, ahead_buf).start()

                if b >= 1:
                    @pl.when(jnp.logical_and(ahead >= n_steps, has_next))
                    def _():
                        if b == 1:
                            for copy in stage(blk + 1, 1 - slot):
                                copy.wait()
                        gather(1 - slot, ahead - n_steps, ahead_buf).start()

                compute(slot, s // PEER_HEADS, s % PEER_HEADS, b, x_v, rows_v, out_v)

        pltpu.sync_copy(out_v, out_hbm.at[pl.ds(base + blk * PEER_SC_TOKENS, PEER_SC_TOKENS)])


def _peer_dots_compute(slot, t, hd, b, h_v, rows_v, dots_v):
    lane = lax.iota(jnp.int32, SC_LANES)

    def col_step(c, accs):
        off = pl.multiple_of(c * SC_LANES, SC_LANES)
        hv = h_v[slot, t, pl.ds(off, SC_LANES)]
        return tuple(accs[r] + rows_v[b, r, pl.ds(off, SC_LANES)] * hv for r in range(PEER_TOPK))

    accs = plsc.parallel_loop(
        0, D_MODEL // SC_LANES, unroll=PEER_SC_UNROLL,
        carry=tuple(jnp.zeros((SC_LANES,), jnp.float32) for _ in range(PEER_TOPK)))(col_step)
    res = jnp.zeros((SC_LANES,), jnp.float32)
    for r in range(PEER_TOPK):
        res = jnp.where(lane == r, jnp.sum(accs[r]), res)
    dots_v[t, pl.ds(hd * PEER_TOPK, PEER_TOPK)] = res


def _peer_mix_compute(slot, t, hd, b, act_v, rows_v, out_v):
    svec = jnp.full((SC_LANES,), slot, jnp.int32)
    tvec = jnp.full((SC_LANES,), t, jnp.int32)
    weights = [plsc.load_gather(act_v, [svec, tvec, jnp.full((SC_LANES,), hd * PEER_TOPK + r, jnp.int32)])
               for r in range(PEER_TOPK)]

    @plsc.parallel_loop(0, D_MODEL // SC_LANES, unroll=PEER_SC_UNROLL)
    def _(c):
        off = pl.multiple_of(c * SC_LANES, SC_LANES)
        a = rows_v[b, 0, pl.ds(off, SC_LANES)] * weights[0]
        for r in range(1, PEER_TOPK):
            a = a + rows_v[b, r, pl.ds(off, SC_LANES)] * weights[r]
        plsc.addupdate(out_v.at[t, pl.ds(off, SC_LANES)], a)


def _peer_expert_dots(h, experts, u_tab):
    T = h.shape[0]
    assert T % (SC_CORES * SC_SUBCORES * PEER_SC_TOKENS) == 0
    assert (PEER_SC_TOKENS * PEER_HEADS) % PEER_SC_RING == 0 and PEER_TOPK == SC_LANES
    body = partial(_peer_sc_body, _peer_dots_compute, False)
    return _peer_sc_call(body, T, PEER_SLOTS, D_MODEL)(h, experts, u_tab)


def _peer_expert_mix(act, experts, v_tab):
    T = act.shape[0]
    assert T % (SC_CORES * SC_SUBCORES * PEER_SC_TOKENS) == 0
    body = partial(_peer_sc_body, _peer_mix_compute, True)
    return _peer_sc_call(body, T, D_MODEL, PEER_SLOTS)(act, experts, v_tab)


PEER_TM = 256


def _extract_topk(s, k, payload=None):
    R = s.shape[0]
    riota = lax.broadcasted_iota(jnp.int32, s.shape, 0)
    vals, rows = [], []
    for _ in range(k):
        m = jnp.max(s, axis=0, keepdims=True)
        pos = jnp.min(jnp.where(s == m, riota, R), axis=0, keepdims=True)
        hit = riota == pos
        vals.append(m)
        rows.append(pos if payload is None else jnp.max(jnp.where(hit, payload, -1), axis=0, keepdims=True))
        s = jnp.where(hit, -jnp.inf, s)
    return jnp.concatenate(vals, axis=0), jnp.concatenate(rows, axis=0)


def _peer_route_kernel(x_ref, sc_ref, sh_ref, wq_ref, keys_ref, h_ref, ex_ref, gate_ref, q_ref, ext_ref, gt_ref):
    K = PEER_TOPK
    f32 = jnp.float32
    h = x_ref[0] * (1.0 + sc_ref[0]) + sh_ref[0]
    h_ref[0] = h
    q = jnp.dot(h.astype(wq_ref.dtype), wq_ref[...], preferred_element_type=f32)
    for hd in range(PEER_HEADS):
        q_ref[hd] = q[:, hd * PEER_KEY_DIM:(hd + 1) * PEER_KEY_DIM].astype(q_ref.dtype)

    def head_body(hd, carry):
        qh = q_ref[hd]
        s1 = lax.dot_general(keys_ref[hd, 0], qh[:, :PEER_HALF], _NT, preferred_element_type=f32)
        s2 = lax.dot_general(keys_ref[hd, 1], qh[:, PEER_HALF:], _NT, preferred_element_type=f32)
        v1, i1 = _extract_topk(s1, K)
        v2, i2 = _extract_topk(s2, K)
        tm = v1.shape[1]
        cand_rows, cidx_rows = [], []
        for a in range(K):
            nb = K // (a + 1)
            cand_rows.append(v1[a:a + 1, :] + v2[:nb, :])
            cidx_rows.append(i1[a:a + 1, :] * PEER_N_KEYS + i2[:nb, :])
        n_pad = -sum(r.shape[0] for r in cand_rows) % 8
        cand_rows.append(jnp.full((n_pad, tm), -jnp.inf, f32))
        cidx_rows.append(jnp.full((n_pad, tm), -1, jnp.int32))
        cand = jnp.concatenate(cand_rows, axis=0)
        cidx = jnp.concatenate(cidx_rows, axis=0)
        top_s, experts = _extract_topk(cand, K, payload=cidx)
        e = jnp.exp(top_s - top_s[0:1, :])
        gt_ref[pl.ds(pl.multiple_of(hd * K, K), K), :] = e / jnp.sum(e, axis=0, keepdims=True)
        ext_ref[pl.ds(pl.multiple_of(hd * K, K), K), :] = experts
        return carry

    lax.fori_loop(0, PEER_HEADS, head_body, 0)
    ex_ref[0] = ext_ref[...].T
    gate_ref[0] = gt_ref[...].T


def _peer_route(x, sc, sh, w_pq, sub_keys):
    B, S, D = x.shape
    tm = PEER_TM
    assert S % tm == 0
    return pl.pallas_call(
        _peer_route_kernel,
        grid=(B, S // tm),
        in_specs=[
            pl.BlockSpec((1, tm, D), lambda b, i: (b, i, 0)),
            pl.BlockSpec((1, 1, D), lambda b, i: (b, 0, 0)),
            pl.BlockSpec((1, 1, D), lambda b, i: (b, 0, 0)),
            pl.BlockSpec((D, PEER_HEADS * PEER_KEY_DIM), lambda b, i: (0, 0)),
            pl.BlockSpec((PEER_HEADS, 2, PEER_N_KEYS, PEER_HALF), lambda b, i: (0, 0, 0, 0)),
        ],
        out_specs=[
            pl.BlockSpec((1, tm, D), lambda b, i: (b, i, 0)),
            pl.BlockSpec((1, tm, PEER_SLOTS), lambda b, i: (b, i, 0)),
            pl.BlockSpec((1, tm, PEER_SLOTS), lambda b, i: (b, i, 0)),
        ],
        out_shape=[
            jax.ShapeDtypeStruct((B, S, D), jnp.float32),
            jax.ShapeDtypeStruct((B, S, PEER_SLOTS), jnp.int32),
            jax.ShapeDtypeStruct((B, S, PEER_SLOTS), jnp.float32),
        ],
        scratch_shapes=[
            pltpu.VMEM((PEER_HEADS, tm, PEER_KEY_DIM), MM_DTYPE),
            pltpu.VMEM((PEER_SLOTS, tm), jnp.int32),
            pltpu.VMEM((PEER_SLOTS, tm), jnp.float32),
        ],
        compiler_params=pltpu.CompilerParams(dimension_semantics=("parallel", "parallel")),
        name="peer_route",
    )(x, sc[:, None, :], sh[:, None, :], w_pq.astype(MM_DTYPE), sub_keys.astype(MM_DTYPE))


def _peer_pre(h2, experts, u_tab):
    B, S, D = h2.shape
    return _peer_expert_dots(h2.reshape(B * S, D), experts.reshape(B * S, PEER_SLOTS), u_tab)


def _peer_act_kernel(pre_ref, gate_ref, o_ref):
    pre = pre_ref[...]
    o_ref[...] = 0.5 * pre * (1.0 + lax.erf(pre * (2.0 ** -0.5))) * gate_ref[...]


def _peer_act(pre, gates):
    T, W = pre.shape
    tm = min(8 * ROW_TILE, T)
    assert T % tm == 0
    spec = pl.BlockSpec((tm, W), lambda i: (i, 0))
    return pl.pallas_call(
        _peer_act_kernel,
        grid=(T // tm,),
        in_specs=[spec, spec],
        out_specs=spec,
        out_shape=jax.ShapeDtypeStruct((T, W), jnp.float32),
        name="peer_act",
    )(pre, gates.reshape(T, W))


def _peer_out(act, experts, v_tab):
    B, S, _ = experts.shape
    return _peer_expert_mix(act, experts.reshape(B * S, PEER_SLOTS), v_tab).reshape(B, S, D_MODEL)


def kernel(x, c, w_ada, b_ada, w_in, rw_mu, rw_w0, rw_w2, rw_a0, rw_a2, rw_g2, rw_k_k, rw_k_a, rw_r_k, rw_gn_g, rw_gn_b, dsa_kv_g, idx_k_g, idx_k_b, rel_bias, w_br_a, w_br_b, w_out, ln1_g, ln1_b, peer_wq, peer_keys, peer_u, peer_v, ln2_g, ln2_b):
    assert w_in.shape[0] == DEPTH == 1
    l = 0
    mod = jax.nn.silu(c) @ w_ada[l] + b_ada[l]

    w_rw, w_q, w_kv, w_qi, w_ki, w_wi, w_ga, w_gb = jnp.split(w_in[l], _split_points(IN_SIZES), axis=-1)
    small_pad = jnp.zeros((D_MODEL, SMALL_COLS - SMALL_WI - IDX_HEADS), w_in.dtype)
    w_small = jnp.concatenate([w_qi, w_kv, w_ki, w_wi, small_pad], axis=-1)
    w_gates = jnp.concatenate([w_ga, w_gb], axis=-1)

    B, S, _ = x.shape
    seg_len = S // SEQ_SEGMENTS
    dsa_topk = min(TOPK_MAX, S // 4)
    bias_tiles = _dsa_bias_tiles(rel_bias, DSA_TQ)

    def project_and_dsa(window, sc1, sh1, seg, keys_so_far):
        z_rw, z_q, z_small = _proj_all(x, window, sc1, sh1, (w_rw, w_q, w_small),
                                       (jnp.float32, MM_DTYPE, jnp.float32))
        z_wi_t = jnp.swapaxes(z_small[..., SMALL_WI:SMALL_WI + IDX_HEADS], 1, 2)
        qi, ckv, kidx = _dsa_prep(z_small, dsa_kv_g[l], idx_k_g[l], idx_k_b[l])
        if keys_so_far is not None:
            ckv = jnp.concatenate([keys_so_far[0], ckv], axis=1)
            kidx = jnp.concatenate([keys_so_far[1], kidx], axis=1)
        y_b = _dsa_attention(z_q, qi, z_wi_t, ckv, kidx, bias_tiles, dsa_topk,
                             seg * (seg_len // DSA_TQ), MM_DTYPE)
        return (y_b, z_rw), (ckv, kidx)

    def rwkv_and_merge(window, gt1, sc1, sh1, carry, y_b, z_rw):
        y_a, carry = _rwkv7_time_mix(z_rw, carry, rw_mu[l], rw_w0[l], rw_w2[l], rw_a0[l], rw_a2[l], rw_g2[l],
                                     rw_k_k[l], rw_k_a[l], rw_r_k[l], rw_gn_g[l], rw_gn_b[l], MM_DTYPE)
        return _merge_ln(x, window, gt1, sc1, sh1, y_a, y_b, w_gates, w_br_a[l], w_br_b[l], w_out[l],
                         ln1_g[l], ln1_b[l]), carry

    def finish(st, act):
        y2 = _peer_out(act, st["experts"], peer_v[l])
        return _residual_ln(st["x1"], st["gt2"], y2, ln2_g[l], ln2_b[l])

    assert B % BATCH_GROUPS == 0 and S % (SEQ_SEGMENTS * DSA_TQ) == 0
    gsz = B // BATCH_GROUPS
    outs = []
    prev = None
    for g in range(BATCH_GROUPS):
        sh1, sc1, gt1, sh2, sc2, gt2 = jnp.split(mod[g * gsz:(g + 1) * gsz], 6, axis=-1)
        rw_carry = _rwkv7_init_carry(gsz)
        keys_so_far = None
        for seg in range(SEQ_SEGMENTS):
            window = (g * gsz, gsz, seg * seg_len, seg_len)
            sc1_g = sc1
            if prev is not None:
                sc1_g, prev["experts"] = lax.optimization_barrier((sc1_g, prev["experts"]))
            proj, keys_so_far = project_and_dsa(window, sc1_g, sh1, seg, keys_so_far)
            if prev is not None:
                proj, prev["pre"], outs = lax.optimization_barrier((proj, prev["pre"], outs))
                outs.append(finish(prev, _peer_act(prev["pre"], prev["gates"])))
            x1, rw_carry = rwkv_and_merge(window, gt1, sc1, sh1, rw_carry, *proj)
            h2, experts, gates = _peer_route(x1, sc2, sh2, peer_wq[l], peer_keys[l])
            prev = dict(x1=x1, gt2=gt2, experts=experts, gates=gates, pre=_peer_pre(h2, experts, peer_u[l]))
    prev["pre"], outs = lax.optimization_barrier((prev["pre"], outs))
    outs.append(finish(prev, _peer_act(prev["pre"], prev["gates"])))
    rows = [jnp.concatenate(outs[g * SEQ_SEGMENTS:(g + 1) * SEQ_SEGMENTS], axis=1) for g in range(BATCH_GROUPS)]
    return jnp.concatenate(rows, axis=0)
```
